```python
import jax, jax.numpy as jnp
from jax import lax
import numpy as np

D_MODEL = 2048
BATCH = 8
SEQ = 2048
DEPTH = 4

A_WIDTH = D_MODEL // 2
A_GROUPS = 8
A_GROUP_DIM = A_WIDTH // A_GROUPS
CHUNK = 128
B_HEAD_DIM = 64
B_Q_HEADS = (D_MODEL // 2) // B_HEAD_DIM
B_KV_HEADS = 2
B_WIDTH = B_Q_HEADS * B_HEAD_DIM
B_KV_WIDTH = B_KV_HEADS * B_HEAD_DIM
WINDOW = 128
ROT_DIM = B_HEAD_DIM // 4
ROPE_THETA = 500000.0
C_WIDTH = D_MODEL // 2
C_HEADS = 8
C_HEAD_DIM = C_WIDTH // C_HEADS
CONV_WIDTH = 4
LRU_C = 8.0
D_WIDTH = D_MODEL // 2
POOL_WINDOWS = (2, 4, 8, 16)
D_GROUPS = len(POOL_WINDOWS)
D_GROUP_DIM = D_WIDTH // D_GROUPS

EVEN_SIZES = (A_WIDTH, A_WIDTH, A_WIDTH, B_WIDTH, B_KV_WIDTH, B_KV_WIDTH, B_WIDTH)
ODD_SIZES = (C_WIDTH, C_WIDTH, D_WIDTH, D_WIDTH)
EVEN_IN = sum(EVEN_SIZES)
ODD_IN = sum(ODD_SIZES)
EVEN_MIX = A_WIDTH + B_WIDTH
ODD_MIX = C_WIDTH + D_WIDTH
N_EVEN = (DEPTH + 1) // 2
N_ODD = DEPTH // 2
DN_ALPHA = (2 * DEPTH) ** 0.25
DN_BETA = (8 * DEPTH) ** -0.25
LN_EPS = 1e-5

kernel_name = "hybrid_gmlp_swa_rglru_pool_deepnorm"


def _split(h, sizes):
    idx = [int(i) for i in np.cumsum(sizes)[:-1]]
    return jnp.split(h, idx, axis=-1)


def layer_norm(x, g, b):
    xf = x.astype(jnp.float32)
    mu = jnp.mean(xf, axis=-1, keepdims=True)
    xc = xf - mu
    var = jnp.mean(xc * xc, axis=-1, keepdims=True)
    y = xc * lax.rsqrt(var + LN_EPS) * g.astype(jnp.float32) + b.astype(jnp.float32)
    return y.astype(x.dtype)


def partial_rope(t, cos, sin):
    half = ROT_DIM // 2
    t1, t2, rest = t[..., :half], t[..., half:ROT_DIM], t[..., ROT_DIM:]
    return jnp.concatenate([t1 * cos - t2 * sin, t2 * cos + t1 * sin, rest], axis=-1)


def chunked_gmlp(u, v, ln_g, ln_b, w_s, b_s):
    bsz, s, _ = v.shape
    nc = s // CHUNK
    v = layer_norm(v, ln_g, ln_b)
    vb = v.reshape(bsz, nc, CHUNK, A_GROUPS, A_GROUP_DIM)
    causal = jnp.tril(jnp.ones((CHUNK, CHUNK), dtype=bool))
    ws = jnp.where(causal[None], w_s, jnp.zeros_like(w_s))
    mixed = jnp.einsum('gts,bcsgd->bctgd', ws, vb) + b_s.T[None, None, :, :, None]
    return u * mixed.reshape(bsz, s, A_WIDTH)


def swa_with_sinks(q, k, v, sinks):
    bsz, s, hq, dh = q.shape
    grp = hq // B_KV_HEADS
    nb = s // WINDOW
    qb = q.reshape(bsz, nb, WINDOW, B_KV_HEADS, grp, dh)
    pad = ((0, 0), (1, 0), (0, 0), (0, 0), (0, 0))
    kb = k.reshape(bsz, nb, WINDOW, B_KV_HEADS, dh)
    vb = v.reshape(bsz, nb, WINDOW, B_KV_HEADS, dh)
    kband = jnp.concatenate([jnp.pad(kb[:, :-1], pad), kb], axis=2)
    vband = jnp.concatenate([jnp.pad(vb[:, :-1], pad), vb], axis=2)
    scores = jnp.einsum('bnqhgd,bnkhd->bnhgqk', qb, kband).astype(jnp.float32) * (dh ** -0.5)
    qi = jnp.arange(WINDOW)[:, None]
    kj = jnp.arange(2 * WINDOW)[None, :]
    diff = qi + WINDOW - kj
    band = (diff >= 0) & (diff < WINDOW)
    blk = jnp.arange(nb)[:, None, None]
    mask = band[None] & ((blk > 0) | (kj[None] >= WINDOW))
    scores = jnp.where(mask[None, :, None, None], scores, -jnp.inf)
    sink = jnp.broadcast_to(
        sinks.astype(jnp.float32).reshape(B_KV_HEADS, grp)[None, None, :, :, None, None],
        scores.shape[:-1] + (1,))
    probs = jax.nn.softmax(jnp.concatenate([scores, sink], axis=-1), axis=-1)[..., :-1]
    out = jnp.einsum('bnhgqk,bnkhd->bnqhgd', probs.astype(v.dtype), vband)
    return out.reshape(bsz, s, hq * dh)


def rg_lru(xc, conv_w, conv_b, w_a, b_a, w_x, b_x, lam):
    bsz, s, _ = xc.shape
    xconv = lax.conv_general_dilated(
        xc, conv_w[:, None, :], window_strides=(1,), padding=[(CONV_WIDTH - 1, 0)],
        dimension_numbers=('NWC', 'WIO', 'NWC'), feature_group_count=C_WIDTH) + conv_b
    xh = xconv.reshape(bsz, s, C_HEADS, C_HEAD_DIM)
    r = jax.nn.sigmoid(jnp.einsum('bshi,hij->bshj', xh, w_a).reshape(bsz, s, C_WIDTH) + b_a)
    i = jax.nn.sigmoid(jnp.einsum('bshi,hij->bshj', xh, w_x).reshape(bsz, s, C_WIDTH) + b_x)
    log_a = -LRU_C * r.astype(jnp.float32) * jax.nn.softplus(-lam.astype(jnp.float32))
    a = jnp.exp(log_a)
    mult = jnp.sqrt(-jnp.expm1(2.0 * log_a))
    bterm = mult * (i * xconv).astype(jnp.float32)

    def combine(left, right):
        a1, b1 = left
        a2, b2 = right
        return a1 * a2, a2 * b1 + b2

    _, h = lax.associative_scan(combine, (a, bterm), axis=1)
    return h.astype(xc.dtype)


def multiscale_pool(xd, w_pool, d_scale):
    bsz, s, _ = xd.shape
    xf = xd.astype(jnp.float32)
    csum = jnp.cumsum(xf, axis=1)
    pos1 = jnp.arange(s, dtype=jnp.float32)[None, :, None] + 1.0
    outs = []
    for g, w in enumerate(POOL_WINDOWS):
        sl = slice(g * D_GROUP_DIM, (g + 1) * D_GROUP_DIM)
        cg = csum[..., sl]
        shifted = jnp.pad(cg[:, :-w], ((0, 0), (w, 0), (0, 0)))
        mean = (cg - shifted) / jnp.minimum(pos1, float(w))
        outs.append(mean - xf[..., sl])
    pooled = jnp.stack(outs, axis=2).astype(xd.dtype)
    mixed = jnp.einsum('bsgi,gij->bsgj', pooled, w_pool).reshape(bsz, s, D_WIDTH)
    return mixed * d_scale


def even_layer(x, cos, sin, w_in, a_ln_g, a_ln_b, a_ws, a_bs, b_sinks, w_out, ln_g, ln_b):
    bsz, s, _ = x.shape
    h = x @ w_in
    u, v, a_gate, q, k, vv, b_gate = _split(h, EVEN_SIZES)
    a_out = chunked_gmlp(u, v, a_ln_g, a_ln_b, a_ws, a_bs) * jax.nn.silu(a_gate)
    q = partial_rope(q.reshape(bsz, s, B_Q_HEADS, B_HEAD_DIM), cos, sin)
    k = partial_rope(k.reshape(bsz, s, B_KV_HEADS, B_HEAD_DIM), cos, sin)
    vv = vv.reshape(bsz, s, B_KV_HEADS, B_HEAD_DIM)
    b_out = swa_with_sinks(q, k, vv, b_sinks) * jax.nn.silu(b_gate)
    y = jnp.concatenate([a_out, b_out], axis=-1) @ w_out
    return layer_norm(DN_ALPHA * x + y, ln_g, ln_b)


def odd_layer(x, w_in, conv_w, conv_b, w_a, b_a, w_x, b_x, lam, w_pool, d_scale,
              w_out, ln_g, ln_b):
    h = x @ w_in
    xc, c_gate, xd, d_gate = _split(h, ODD_SIZES)
    c_out = rg_lru(xc, conv_w, conv_b, w_a, b_a, w_x, b_x, lam) * jax.nn.silu(c_gate)
    d_out = multiscale_pool(xd, w_pool, d_scale) * jax.nn.silu(d_gate)
    y = jnp.concatenate([c_out, d_out], axis=-1) @ w_out
    return layer_norm(DN_ALPHA * x + y, ln_g, ln_b)


def _fwd_setup_inputs(seed: int = 0) -> dict:
    key = jax.random.key(seed)
    ks = jax.random.split(key, 26)
    f32 = jnp.float32
    nrm = lambda k, shp, sc: sc * jax.random.normal(k, shp, f32)
    ne, no = N_EVEN, N_ODD
    u = jax.random.uniform(ks[25], (no, C_WIDTH), f32, 0.9, 0.999)
    a_base = u ** (1.0 / LRU_C)
    lam = jnp.log(a_base) - jnp.log1p(-a_base)
    return {
        "x": nrm(ks[0], (BATCH, SEQ, D_MODEL), 1.0),
        "positions": jnp.broadcast_to(jnp.arange(SEQ, dtype=jnp.int32)[None], (BATCH, SEQ)),
        "even_w_in": nrm(ks[1], (ne, D_MODEL, EVEN_IN), D_MODEL ** -0.5),
        "even_a_ln_g": 1.0 + nrm(ks[2], (ne, A_WIDTH), 0.01),
        "even_a_ln_b": nrm(ks[3], (ne, A_WIDTH), 0.01),
        "even_a_ws": nrm(ks[4], (ne, A_GROUPS, CHUNK, CHUNK), CHUNK ** -0.5),
        "even_a_bs": 1.0 + nrm(ks[5], (ne, A_GROUPS, CHUNK), 0.1),
        "even_b_sinks": nrm(ks[6], (ne, B_Q_HEADS), 0.5),
        "even_w_out": nrm(ks[7], (ne, EVEN_MIX, D_MODEL), DN_BETA * EVEN_MIX ** -0.5),
        "even_ln_g": 1.0 + nrm(ks[8], (ne, D_MODEL), 0.01),
        "even_ln_b": nrm(ks[9], (ne, D_MODEL), 0.01),
        "odd_w_in": nrm(ks[10], (no, D_MODEL, ODD_IN), D_MODEL ** -0.5),
        "odd_conv_w": nrm(ks[11], (no, CONV_WIDTH, C_WIDTH), CONV_WIDTH ** -0.5),
        "odd_conv_b": nrm(ks[12], (no, C_WIDTH), 0.01),
        "odd_w_a": nrm(ks[13], (no, C_HEADS, C_HEAD_DIM, C_HEAD_DIM), C_HEAD_DIM ** -0.5),
        "odd_b_a": nrm(ks[14], (no, C_WIDTH), 0.01),
        "odd_w_x": nrm(ks[15], (no, C_HEADS, C_HEAD_DIM, C_HEAD_DIM), C_HEAD_DIM ** -0.5),
        "odd_b_x": nrm(ks[16], (no, C_WIDTH), 0.01),
        "odd_lam": lam,
        "odd_w_pool": nrm(ks[17], (no, D_GROUPS, D_GROUP_DIM, D_GROUP_DIM), D_GROUP_DIM ** -0.5),
        "odd_d_scale": 1.0 + nrm(ks[18], (no, D_WIDTH), 0.1),
        "odd_w_out": nrm(ks[19], (no, ODD_MIX, D_MODEL), DN_BETA * ODD_MIX ** -0.5),
        "odd_ln_g": 1.0 + nrm(ks[20], (no, D_MODEL), 0.01),
        "odd_ln_b": nrm(ks[21], (no, D_MODEL), 0.01),
    }


def _fwd_reference(x, positions, even_w_in, even_a_ln_g, even_a_ln_b, even_a_ws, even_a_bs,
              even_b_sinks, even_w_out, even_ln_g, even_ln_b, odd_w_in, odd_conv_w,
              odd_conv_b, odd_w_a, odd_b_a, odd_w_x, odd_b_x, odd_lam, odd_w_pool,
              odd_d_scale, odd_w_out, odd_ln_g, odd_ln_b):
    inv_freq = ROPE_THETA ** (-jnp.arange(0, ROT_DIM, 2, dtype=jnp.float32) / ROT_DIM)
    ang = positions.astype(jnp.float32)[..., None] * inv_freq
    cos = jnp.cos(ang)[:, :, None, :].astype(x.dtype)
    sin = jnp.sin(ang)[:, :, None, :].astype(x.dtype)
    for layer in range(DEPTH):
        j = layer // 2
        if layer % 2 == 0:
            x = even_layer(x, cos, sin, even_w_in[j], even_a_ln_g[j], even_a_ln_b[j],
                           even_a_ws[j], even_a_bs[j], even_b_sinks[j], even_w_out[j],
                           even_ln_g[j], even_ln_b[j])
        else:
            x = odd_layer(x, odd_w_in[j], odd_conv_w[j], odd_conv_b[j], odd_w_a[j],
                          odd_b_a[j], odd_w_x[j], odd_b_x[j], odd_lam[j], odd_w_pool[j],
                          odd_d_scale[j], odd_w_out[j], odd_ln_g[j], odd_ln_b[j])
    return x


import jax as _jax
import jax.numpy as _jnp

TWIN_FORMAT = 'train_step'
FWD_PARAMS = ['x', 'positions', 'even_w_in', 'even_a_ln_g', 'even_a_ln_b', 'even_a_ws', 'even_a_bs', 'even_b_sinks', 'even_w_out', 'even_ln_g', 'even_ln_b', 'odd_w_in', 'odd_conv_w', 'odd_conv_b', 'odd_w_a', 'odd_b_a', 'odd_w_x', 'odd_b_x', 'odd_lam', 'odd_w_pool', 'odd_d_scale', 'odd_w_out', 'odd_ln_g', 'odd_ln_b']
TWIN_WEIGHTS = ['even_w_in', 'even_a_ln_g', 'even_a_ln_b', 'even_a_ws', 'even_a_bs', 'even_b_sinks', 'even_w_out', 'even_ln_g', 'even_ln_b', 'odd_w_in', 'odd_conv_w', 'odd_conv_b', 'odd_w_a', 'odd_b_a', 'odd_w_x', 'odd_b_x', 'odd_lam', 'odd_w_pool', 'odd_d_scale', 'odd_w_out', 'odd_ln_g', 'odd_ln_b']
TWIN_DIFF_INPUT = 'x'
TWIN_INPUTS = ['x', 'positions', 'even_w_in', 'even_a_ln_g', 'even_a_ln_b', 'even_a_ws', 'even_a_bs', 'even_b_sinks', 'even_w_out', 'even_ln_g', 'even_ln_b', 'odd_w_in', 'odd_conv_w', 'odd_conv_b', 'odd_w_a', 'odd_b_a', 'odd_w_x', 'odd_b_x', 'odd_lam', 'odd_w_pool', 'odd_d_scale', 'odd_w_out', 'odd_ln_g', 'odd_ln_b', 'loss_target', 'm_even_w_in', 'm_even_a_ln_g', 'm_even_a_ln_b', 'm_even_a_ws', 'm_even_a_bs', 'm_even_b_sinks', 'm_even_w_out', 'm_even_ln_g', 'm_even_ln_b', 'm_odd_w_in', 'm_odd_conv_w', 'm_odd_conv_b', 'm_odd_w_a', 'm_odd_b_a', 'm_odd_w_x', 'm_odd_b_x', 'm_odd_lam', 'm_odd_w_pool', 'm_odd_d_scale', 'm_odd_w_out', 'm_odd_ln_g', 'm_odd_ln_b', 'v_even_w_in', 'v_even_a_ln_g', 'v_even_a_ln_b', 'v_even_a_ws', 'v_even_a_bs', 'v_even_b_sinks', 'v_even_w_out', 'v_even_ln_g', 'v_even_ln_b', 'v_odd_w_in', 'v_odd_conv_w', 'v_odd_conv_b', 'v_odd_w_a', 'v_odd_b_a', 'v_odd_w_x', 'v_odd_b_x', 'v_odd_lam', 'v_odd_w_pool', 'v_odd_d_scale', 'v_odd_w_out', 'v_odd_ln_g', 'v_odd_ln_b']
TWIN_OUTPUTS = ['loss', 'grad_x', 'grad_even_w_in', 'grad_even_a_ln_g', 'grad_even_a_ln_b', 'grad_even_a_ws', 'grad_even_a_bs', 'grad_even_b_sinks', 'grad_even_w_out', 'grad_even_ln_g', 'grad_even_ln_b', 'grad_odd_w_in', 'grad_odd_conv_w', 'grad_odd_conv_b', 'grad_odd_w_a', 'grad_odd_b_a', 'grad_odd_w_x', 'grad_odd_b_x', 'grad_odd_lam', 'grad_odd_w_pool', 'grad_odd_d_scale', 'grad_odd_w_out', 'grad_odd_ln_g', 'grad_odd_ln_b', 'delta_even_w_in', 'delta_even_a_ln_g', 'delta_even_a_ln_b', 'delta_even_a_ws', 'delta_even_a_bs', 'delta_even_b_sinks', 'delta_even_w_out', 'delta_even_ln_g', 'delta_even_ln_b', 'delta_odd_w_in', 'delta_odd_conv_w', 'delta_odd_conv_b', 'delta_odd_w_a', 'delta_odd_b_a', 'delta_odd_w_x', 'delta_odd_b_x', 'delta_odd_lam', 'delta_odd_w_pool', 'delta_odd_d_scale', 'delta_odd_w_out', 'delta_odd_ln_g', 'delta_odd_ln_b', 'new_m_even_w_in', 'new_m_even_a_ln_g', 'new_m_even_a_ln_b', 'new_m_even_a_ws', 'new_m_even_a_bs', 'new_m_even_b_sinks', 'new_m_even_w_out', 'new_m_even_ln_g', 'new_m_even_ln_b', 'new_m_odd_w_in', 'new_m_odd_conv_w', 'new_m_odd_conv_b', 'new_m_odd_w_a', 'new_m_odd_b_a', 'new_m_odd_w_x', 'new_m_odd_b_x', 'new_m_odd_lam', 'new_m_odd_w_pool', 'new_m_odd_d_scale', 'new_m_odd_w_out', 'new_m_odd_ln_g', 'new_m_odd_ln_b', 'new_v_even_w_in', 'new_v_even_a_ln_g', 'new_v_even_a_ln_b', 'new_v_even_a_ws', 'new_v_even_a_bs', 'new_v_even_b_sinks', 'new_v_even_w_out', 'new_v_even_ln_g', 'new_v_even_ln_b', 'new_v_odd_w_in', 'new_v_odd_conv_w', 'new_v_odd_conv_b', 'new_v_odd_w_a', 'new_v_odd_b_a', 'new_v_odd_w_x', 'new_v_odd_b_x', 'new_v_odd_lam', 'new_v_odd_w_pool', 'new_v_odd_d_scale', 'new_v_odd_w_out', 'new_v_odd_ln_g', 'new_v_odd_ln_b']
TWIN_LEAF_KINDS = {'loss': 'loss', 'grad_x': 'grad_x', 'grad_even_w_in': 'grad_w', 'grad_even_a_ln_g': 'grad_w', 'grad_even_a_ln_b': 'grad_w', 'grad_even_a_ws': 'grad_w', 'grad_even_a_bs': 'grad_w', 'grad_even_b_sinks': 'grad_w', 'grad_even_w_out': 'grad_w', 'grad_even_ln_g': 'grad_w', 'grad_even_ln_b': 'grad_w', 'grad_odd_w_in': 'grad_w', 'grad_odd_conv_w': 'grad_w', 'grad_odd_conv_b': 'grad_w', 'grad_odd_w_a': 'grad_w', 'grad_odd_b_a': 'grad_w', 'grad_odd_w_x': 'grad_w', 'grad_odd_b_x': 'grad_w', 'grad_odd_lam': 'grad_w', 'grad_odd_w_pool': 'grad_w', 'grad_odd_d_scale': 'grad_w', 'grad_odd_w_out': 'grad_w', 'grad_odd_ln_g': 'grad_w', 'grad_odd_ln_b': 'grad_w', 'delta_even_w_in': 'delta_w', 'delta_even_a_ln_g': 'delta_w', 'delta_even_a_ln_b': 'delta_w', 'delta_even_a_ws': 'delta_w', 'delta_even_a_bs': 'delta_w', 'delta_even_b_sinks': 'delta_w', 'delta_even_w_out': 'delta_w', 'delta_even_ln_g': 'delta_w', 'delta_even_ln_b': 'delta_w', 'delta_odd_w_in': 'delta_w', 'delta_odd_conv_w': 'delta_w', 'delta_odd_conv_b': 'delta_w', 'delta_odd_w_a': 'delta_w', 'delta_odd_b_a': 'delta_w', 'delta_odd_w_x': 'delta_w', 'delta_odd_b_x': 'delta_w', 'delta_odd_lam': 'delta_w', 'delta_odd_w_pool': 'delta_w', 'delta_odd_d_scale': 'delta_w', 'delta_odd_w_out': 'delta_w', 'delta_odd_ln_g': 'delta_w', 'delta_odd_ln_b': 'delta_w', 'new_m_even_w_in': 'new_m', 'new_m_even_a_ln_g': 'new_m', 'new_m_even_a_ln_b': 'new_m', 'new_m_even_a_ws': 'new_m', 'new_m_even_a_bs': 'new_m', 'new_m_even_b_sinks': 'new_m', 'new_m_even_w_out': 'new_m', 'new_m_even_ln_g': 'new_m', 'new_m_even_ln_b': 'new_m', 'new_m_odd_w_in': 'new_m', 'new_m_odd_conv_w': 'new_m', 'new_m_odd_conv_b': 'new_m', 'new_m_odd_w_a': 'new_m', 'new_m_odd_b_a': 'new_m', 'new_m_odd_w_x': 'new_m', 'new_m_odd_b_x': 'new_m', 'new_m_odd_lam': 'new_m', 'new_m_odd_w_pool': 'new_m', 'new_m_odd_d_scale': 'new_m', 'new_m_odd_w_out': 'new_m', 'new_m_odd_ln_g': 'new_m', 'new_m_odd_ln_b': 'new_m', 'new_v_even_w_in': 'new_v', 'new_v_even_a_ln_g': 'new_v', 'new_v_even_a_ln_b': 'new_v', 'new_v_even_a_ws': 'new_v', 'new_v_even_a_bs': 'new_v', 'new_v_even_b_sinks': 'new_v', 'new_v_even_w_out': 'new_v', 'new_v_even_ln_g': 'new_v', 'new_v_even_ln_b': 'new_v', 'new_v_odd_w_in': 'new_v', 'new_v_odd_conv_w': 'new_v', 'new_v_odd_conv_b': 'new_v', 'new_v_odd_w_a': 'new_v', 'new_v_odd_b_a': 'new_v', 'new_v_odd_w_x': 'new_v', 'new_v_odd_b_x': 'new_v', 'new_v_odd_lam': 'new_v', 'new_v_odd_w_pool': 'new_v', 'new_v_odd_d_scale': 'new_v', 'new_v_odd_w_out': 'new_v', 'new_v_odd_ln_g': 'new_v', 'new_v_odd_ln_b': 'new_v'}


def _forward(args):
    return _fwd_reference(*[args[k] for k in FWD_PARAMS])


def _output_shape():
    out = _jax.eval_shape(lambda: _forward(_fwd_setup_inputs(0)))
    return out.shape, out.dtype

N_MICROBATCH = 1
ADAM_LR = 0.001
ADAM_B1 = 0.9
ADAM_B2 = 0.999
ADAM_EPS = 1e-08
ADAM_WD = 0.01
ADAM_STEP = 10
PER_EXAMPLE_BATCH_AXIS = {'x': 0, 'positions': 0, 'loss_target': 0}
SHARED_INPUTS = []
_WEIGHT_DTYPES = {'even_w_in': _jnp.float32, 'even_a_ln_g': _jnp.float32, 'even_a_ln_b': _jnp.float32, 'even_a_ws': _jnp.float32, 'even_a_bs': _jnp.float32, 'even_b_sinks': _jnp.float32, 'even_w_out': _jnp.float32, 'even_ln_g': _jnp.float32, 'even_ln_b': _jnp.float32, 'odd_w_in': _jnp.float32, 'odd_conv_w': _jnp.float32, 'odd_conv_b': _jnp.float32, 'odd_w_a': _jnp.float32, 'odd_b_a': _jnp.float32, 'odd_w_x': _jnp.float32, 'odd_b_x': _jnp.float32, 'odd_lam': _jnp.float32, 'odd_w_pool': _jnp.float32, 'odd_d_scale': _jnp.float32, 'odd_w_out': _jnp.float32, 'odd_ln_g': _jnp.float32, 'odd_ln_b': _jnp.float32}
MOMENT_SCALE = {'even_w_in': 7.966867e-03, 'even_a_ln_g': 6.611781e-03, 'even_a_ln_b': 6.856010e-03, 'even_a_ws': 6.676206e-03, 'even_a_bs': 9.530810e-03, 'even_b_sinks': 1.471353e-03, 'even_w_out': 1.969283e-02, 'even_ln_g': 1.391905e-01, 'even_ln_b': 1.012029e-01, 'odd_w_in': 7.306669e-03, 'odd_conv_w': 6.270090e-03, 'odd_conv_b': 6.468350e-02, 'odd_w_a': 1.945704e-03, 'odd_b_a': 1.597250e-03, 'odd_w_x': 3.490630e-03, 'odd_b_x': 2.168224e-03, 'odd_lam': 3.202270e-03, 'odd_w_pool': 8.240257e-03, 'odd_d_scale': 8.317453e-03, 'odd_w_out': 1.736677e-02, 'odd_ln_g': 5.657500e+00, 'odd_ln_b': 1.464506e-01}


def _to_microbatches(a, axis):
    t = _jnp.moveaxis(a, axis, 0)
    t = t.reshape((N_MICROBATCH, t.shape[0] // N_MICROBATCH) + t.shape[1:])
    return _jnp.moveaxis(t, 1, axis + 1)


def setup_inputs(seed: int = 0) -> dict:
    inp = _fwd_setup_inputs(seed)
    key = _jax.random.fold_in(_jax.random.key(seed), 7919)
    shape, _ = _output_shape()
    out = dict(inp)
    out["loss_target"] = _jax.random.normal(_jax.random.fold_in(key, 0), shape, _jnp.float32)
    for i, name in enumerate(TWIN_WEIGHTS):
        w = inp[name].astype(_jnp.float32)
        if MOMENT_SCALE is None:
            s = _jnp.sqrt(_jnp.mean(_jnp.square(w)) + 1e-30)
        else:
            s = MOMENT_SCALE[name]
        km, kv = _jax.random.split(_jax.random.fold_in(key, i + 1))
        out[name] = w
        out["m_" + name] = s * _jax.random.normal(km, w.shape, _jnp.float32)
        out["v_" + name] = (s * s) * _jax.random.uniform(kv, w.shape, _jnp.float32, 0.5, 1.5)
    if N_MICROBATCH > 1:
        for name, axis in PER_EXAMPLE_BATCH_AXIS.items():
            out[name] = _to_microbatches(out[name], axis)
    return {'x': out['x'], 'positions': out['positions'], 'even_w_in': out['even_w_in'], 'even_a_ln_g': out['even_a_ln_g'], 'even_a_ln_b': out['even_a_ln_b'], 'even_a_ws': out['even_a_ws'], 'even_a_bs': out['even_a_bs'], 'even_b_sinks': out['even_b_sinks'], 'even_w_out': out['even_w_out'], 'even_ln_g': out['even_ln_g'], 'even_ln_b': out['even_ln_b'], 'odd_w_in': out['odd_w_in'], 'odd_conv_w': out['odd_conv_w'], 'odd_conv_b': out['odd_conv_b'], 'odd_w_a': out['odd_w_a'], 'odd_b_a': out['odd_b_a'], 'odd_w_x': out['odd_w_x'], 'odd_b_x': out['odd_b_x'], 'odd_lam': out['odd_lam'], 'odd_w_pool': out['odd_w_pool'], 'odd_d_scale': out['odd_d_scale'], 'odd_w_out': out['odd_w_out'], 'odd_ln_g': out['odd_ln_g'], 'odd_ln_b': out['odd_ln_b'], 'loss_target': out['loss_target'], 'm_even_w_in': out['m_even_w_in'], 'm_even_a_ln_g': out['m_even_a_ln_g'], 'm_even_a_ln_b': out['m_even_a_ln_b'], 'm_even_a_ws': out['m_even_a_ws'], 'm_even_a_bs': out['m_even_a_bs'], 'm_even_b_sinks': out['m_even_b_sinks'], 'm_even_w_out': out['m_even_w_out'], 'm_even_ln_g': out['m_even_ln_g'], 'm_even_ln_b': out['m_even_ln_b'], 'm_odd_w_in': out['m_odd_w_in'], 'm_odd_conv_w': out['m_odd_conv_w'], 'm_odd_conv_b': out['m_odd_conv_b'], 'm_odd_w_a': out['m_odd_w_a'], 'm_odd_b_a': out['m_odd_b_a'], 'm_odd_w_x': out['m_odd_w_x'], 'm_odd_b_x': out['m_odd_b_x'], 'm_odd_lam': out['m_odd_lam'], 'm_odd_w_pool': out['m_odd_w_pool'], 'm_odd_d_scale': out['m_odd_d_scale'], 'm_odd_w_out': out['m_odd_w_out'], 'm_odd_ln_g': out['m_odd_ln_g'], 'm_odd_ln_b': out['m_odd_ln_b'], 'v_even_w_in': out['v_even_w_in'], 'v_even_a_ln_g': out['v_even_a_ln_g'], 'v_even_a_ln_b': out['v_even_a_ln_b'], 'v_even_a_ws': out['v_even_a_ws'], 'v_even_a_bs': out['v_even_a_bs'], 'v_even_b_sinks': out['v_even_b_sinks'], 'v_even_w_out': out['v_even_w_out'], 'v_even_ln_g': out['v_even_ln_g'], 'v_even_ln_b': out['v_even_ln_b'], 'v_odd_w_in': out['v_odd_w_in'], 'v_odd_conv_w': out['v_odd_conv_w'], 'v_odd_conv_b': out['v_odd_conv_b'], 'v_odd_w_a': out['v_odd_w_a'], 'v_odd_b_a': out['v_odd_b_a'], 'v_odd_w_x': out['v_odd_w_x'], 'v_odd_b_x': out['v_odd_b_x'], 'v_odd_lam': out['v_odd_lam'], 'v_odd_w_pool': out['v_odd_w_pool'], 'v_odd_d_scale': out['v_odd_d_scale'], 'v_odd_w_out': out['v_odd_w_out'], 'v_odd_ln_g': out['v_odd_ln_g'], 'v_odd_ln_b': out['v_odd_ln_b']}


def _loss(weights, diff, rest, loss_target):
    with _jax.named_scope("forward"):
        args = {**rest, TWIN_DIFF_INPUT: diff, **{k: w.astype(_WEIGHT_DTYPES[k]) for k, w in weights.items()}}
        y = _forward(args)
    with _jax.named_scope("loss_head"):
        err = _jnp.square(y.astype(_jnp.float32) - loss_target)
        return 0.5 * _jnp.sum(_jnp.mean(err, axis=-1)) if err.ndim else 0.5 * err


def _adamw(w, g, m, v):
    m = ADAM_B1 * m + (1.0 - ADAM_B1) * g
    v = ADAM_B2 * v + (1.0 - ADAM_B2) * _jnp.square(g)
    m_hat = m / (1.0 - ADAM_B1 ** ADAM_STEP)
    v_hat = v / (1.0 - ADAM_B2 ** ADAM_STEP)
    delta = -ADAM_LR * (m_hat / (_jnp.sqrt(v_hat) + ADAM_EPS) + ADAM_WD * w)
    return delta, m, v


def reference(x, positions, even_w_in, even_a_ln_g, even_a_ln_b, even_a_ws, even_a_bs, even_b_sinks, even_w_out, even_ln_g, even_ln_b, odd_w_in, odd_conv_w, odd_conv_b, odd_w_a, odd_b_a, odd_w_x, odd_b_x, odd_lam, odd_w_pool, odd_d_scale, odd_w_out, odd_ln_g, odd_ln_b, loss_target, m_even_w_in, m_even_a_ln_g, m_even_a_ln_b, m_even_a_ws, m_even_a_bs, m_even_b_sinks, m_even_w_out, m_even_ln_g, m_even_ln_b, m_odd_w_in, m_odd_conv_w, m_odd_conv_b, m_odd_w_a, m_odd_b_a, m_odd_w_x, m_odd_b_x, m_odd_lam, m_odd_w_pool, m_odd_d_scale, m_odd_w_out, m_odd_ln_g, m_odd_ln_b, v_even_w_in, v_even_a_ln_g, v_even_a_ln_b, v_even_a_ws, v_even_a_bs, v_even_b_sinks, v_even_w_out, v_even_ln_g, v_even_ln_b, v_odd_w_in, v_odd_conv_w, v_odd_conv_b, v_odd_w_a, v_odd_b_a, v_odd_w_x, v_odd_b_x, v_odd_lam, v_odd_w_pool, v_odd_d_scale, v_odd_w_out, v_odd_ln_g, v_odd_ln_b):
    given = dict(x=x, positions=positions, even_w_in=even_w_in, even_a_ln_g=even_a_ln_g, even_a_ln_b=even_a_ln_b, even_a_ws=even_a_ws, even_a_bs=even_a_bs, even_b_sinks=even_b_sinks, even_w_out=even_w_out, even_ln_g=even_ln_g, even_ln_b=even_ln_b, odd_w_in=odd_w_in, odd_conv_w=odd_conv_w, odd_conv_b=odd_conv_b, odd_w_a=odd_w_a, odd_b_a=odd_b_a, odd_w_x=odd_w_x, odd_b_x=odd_b_x, odd_lam=odd_lam, odd_w_pool=odd_w_pool, odd_d_scale=odd_d_scale, odd_w_out=odd_w_out, odd_ln_g=odd_ln_g, odd_ln_b=odd_ln_b, loss_target=loss_target, m_even_w_in=m_even_w_in, m_even_a_ln_g=m_even_a_ln_g, m_even_a_ln_b=m_even_a_ln_b, m_even_a_ws=m_even_a_ws, m_even_a_bs=m_even_a_bs, m_even_b_sinks=m_even_b_sinks, m_even_w_out=m_even_w_out, m_even_ln_g=m_even_ln_g, m_even_ln_b=m_even_ln_b, m_odd_w_in=m_odd_w_in, m_odd_conv_w=m_odd_conv_w, m_odd_conv_b=m_odd_conv_b, m_odd_w_a=m_odd_w_a, m_odd_b_a=m_odd_b_a, m_odd_w_x=m_odd_w_x, m_odd_b_x=m_odd_b_x, m_odd_lam=m_odd_lam, m_odd_w_pool=m_odd_w_pool, m_odd_d_scale=m_odd_d_scale, m_odd_w_out=m_odd_w_out, m_odd_ln_g=m_odd_ln_g, m_odd_ln_b=m_odd_ln_b, v_even_w_in=v_even_w_in, v_even_a_ln_g=v_even_a_ln_g, v_even_a_ln_b=v_even_a_ln_b, v_even_a_ws=v_even_a_ws, v_even_a_bs=v_even_a_bs, v_even_b_sinks=v_even_b_sinks, v_even_w_out=v_even_w_out, v_even_ln_g=v_even_ln_g, v_even_ln_b=v_even_ln_b, v_odd_w_in=v_odd_w_in, v_odd_conv_w=v_odd_conv_w, v_odd_conv_b=v_odd_conv_b, v_odd_w_a=v_odd_w_a, v_odd_b_a=v_odd_b_a, v_odd_w_x=v_odd_w_x, v_odd_b_x=v_odd_b_x, v_odd_lam=v_odd_lam, v_odd_w_pool=v_odd_w_pool, v_odd_d_scale=v_odd_d_scale, v_odd_w_out=v_odd_w_out, v_odd_ln_g=v_odd_ln_g, v_odd_ln_b=v_odd_ln_b)
    weights = {n: given[n] for n in TWIN_WEIGHTS}
    shared = {n: given[n] for n in SHARED_INPUTS}
    per_example = {n: given[n] for n in ['x', 'positions']}
    grad_fn = _jax.value_and_grad(_loss, argnums=(0, 1))

    def one_microbatch(ex, loss_target):
        ex = dict(ex)
        diff = ex.pop(TWIN_DIFF_INPUT)
        return grad_fn(weights, diff, {**shared, **ex}, loss_target)

    if N_MICROBATCH == 1:
        loss, (grad_w, grad_x) = one_microbatch(per_example, given["loss_target"])
    else:
        def body(carry, xs):
            loss_sum, grad_sum = carry
            l_k, (gw_k, gx_k) = one_microbatch(xs[0], xs[1])
            with _jax.named_scope("update"):
                return (loss_sum + l_k, _jax.tree.map(_jnp.add, grad_sum, gw_k)), gx_k

        init = (_jnp.zeros((), _jnp.float32), _jax.tree.map(_jnp.zeros_like, weights))
        (loss, grad_w), grad_x = _jax.lax.scan(body, init, (per_example, given["loss_target"]))
    with _jax.named_scope("update"):
        delta_w, new_m, new_v = {}, {}, {}
        for n in TWIN_WEIGHTS:
            delta_w[n], new_m[n], new_v[n] = _adamw(weights[n], grad_w[n], given["m_" + n], given["v_" + n])
    return (loss, grad_x, *[grad_w[n] for n in TWIN_WEIGHTS], *[delta_w[n] for n in TWIN_WEIGHTS],
            *[new_m[n] for n in TWIN_WEIGHTS], *[new_v[n] for n in TWIN_WEIGHTS])
```

```python
import functools
import math

import jax
import jax.numpy as jnp
from jax import lax
from jax.experimental import pallas as pl
from jax.experimental.pallas import tpu as pltpu

F32 = jnp.float32
BF16 = jnp.bfloat16
MXU_DTYPE = jnp.bfloat16

D_MODEL = 2048
DEPTH = 4
A_WIDTH = 1024
A_GROUPS = 8
CHUNK = 128
B_HEAD_DIM = 64
B_Q_HEADS = 16
B_KV_HEADS = 2
WINDOW = 128
ROT_DIM = 16
ROPE_THETA = 500000.0
C_WIDTH = 1024
C_HEADS = 8
CONV_WIDTH = 4
LRU_C = 8.0
D_WIDTH = 1024
D_GROUPS = 4
D_GROUP_DIM = 256
EVEN_IN = 5376
ODD_IN = 4096
DN_ALPHA = (2 * DEPTH) ** 0.25
LN_EPS = 1e-5
ATTN_SCALE = B_HEAD_DIM ** -0.5
NEG_BIG = -1e30

ADAM_LR = 0.001
ADAM_B1 = 0.9
ADAM_B2 = 0.999
ADAM_EPS = 1e-08
ADAM_WD = 0.01
ADAM_STEP = 10

LANES = 128
VMEM_LIMIT = 56 * 1024 * 1024
N_CHIPS = 4
MESH = pl.DeviceIdType.MESH

WEIGHT_NAMES = ['even_w_in', 'even_a_ln_g', 'even_a_ln_b', 'even_a_ws', 'even_a_bs', 'even_b_sinks', 'even_w_out',
                'even_ln_g', 'even_ln_b', 'odd_w_in', 'odd_conv_w', 'odd_conv_b', 'odd_w_a', 'odd_b_a', 'odd_w_x',
                'odd_b_x', 'odd_lam', 'odd_w_pool', 'odd_d_scale', 'odd_w_out', 'odd_ln_g', 'odd_ln_b']
BIG_SHARD_AXIS = {'even_w_in': 2, 'even_w_out': 1, 'odd_w_in': 2, 'odd_w_out': 1}
SMALL_SHARD_AXIS = {'odd_conv_w': 2, 'odd_conv_b': 1, 'odd_b_a': 1, 'odd_b_x': 1, 'odd_lam': 1, 'odd_w_pool': 2,
                    'odd_d_scale': 1, 'odd_ln_g': 1, 'odd_ln_b': 1}
FULL_SHAPES = {
    'even_w_in': (2, 2048, 5376), 'even_a_ln_g': (2, 1024), 'even_a_ln_b': (2, 1024), 'even_a_ws': (2, 8, 128, 128),
    'even_a_bs': (2, 8, 128), 'even_b_sinks': (2, 16), 'even_w_out': (2, 2048, 2048), 'even_ln_g': (2, 2048),
    'even_ln_b': (2, 2048), 'odd_w_in': (2, 2048, 4096), 'odd_conv_w': (2, 4, 1024), 'odd_conv_b': (2, 1024),
    'odd_w_a': (2, 8, 128, 128), 'odd_b_a': (2, 1024), 'odd_w_x': (2, 8, 128, 128), 'odd_b_x': (2, 1024),
    'odd_lam': (2, 1024), 'odd_w_pool': (2, 4, 256, 256), 'odd_d_scale': (2, 1024), 'odd_w_out': (2, 2048, 2048),
    'odd_ln_g': (2, 2048), 'odd_ln_b': (2, 2048)}
BIG_NAMES = ['even_w_in', 'even_w_out', 'odd_w_in', 'odd_w_out']
SMALL_SHARDED = ['odd_conv_w', 'odd_conv_b', 'odd_b_a', 'odd_b_x', 'odd_lam', 'odd_d_scale', 'odd_ln_g', 'odd_ln_b',
                 'odd_w_pool']
SMALL_NAMES = [n for n in WEIGHT_NAMES if n not in BIG_NAMES]
PACK_COLS = 1024


def _params(sem):
    return pltpu.CompilerParams(dimension_semantics=sem, vmem_limit_bytes=VMEM_LIMIT)


def _dot(a, b):
    return lax.dot_general(a, b, (((1,), (0,)), ((), ())), preferred_element_type=F32)


def _dot_nt(a, b):
    return lax.dot_general(a, b, (((1,), (1,)), ((), ())), preferred_element_type=F32)


def _dot_tn(a, b):
    return lax.dot_general(a, b, (((0,), (0,)), ((), ())), preferred_element_type=F32)


def _sigmoid(x):
    return 1.0 / (1.0 + jnp.exp(-x))


def _silu_and_grad(x):
    s = _sigmoid(x)
    return x * s, s * (1.0 + x * (1.0 - s))


def _matmul(a, b, *, mode, out_dtype, tm, tn, name, res=None, res_scale=1.0):
    if mode == 'nn':
        (m, k), (k2, n) = a.shape, b.shape
    elif mode == 'nt':
        (m, k), (n, k2) = a.shape, b.shape
    else:
        (k, m), (k2, n) = a.shape, b.shape
    assert k == k2, (a.shape, b.shape, mode)
    tm, tn = min(tm, m), min(tn, n)
    assert m % tm == 0 and n % tn == 0, (m, n, tm, tn)
    dot = {'nn': _dot, 'nt': _dot_nt, 'tn': _dot_tn}[mode]

    def body(*refs):
        if res is None:
            a_ref, b_ref, o_ref = refs
        else:
            a_ref, b_ref, r_ref, o_ref = refs
        acc = dot(a_ref[...].astype(MXU_DTYPE), b_ref[...].astype(MXU_DTYPE))
        if res is not None:
            acc = acc + res_scale * r_ref[...]
        o_ref[...] = acc.astype(out_dtype)

    a_spec = pl.BlockSpec((k, tm), lambda i, j: (0, i)) if mode == 'tn' else pl.BlockSpec((tm, k), lambda i, j: (i, 0))
    b_spec = pl.BlockSpec((tn, k), lambda i, j: (j, 0)) if mode == 'nt' else pl.BlockSpec((k, tn), lambda i, j: (0, j))
    in_specs = [a_spec, b_spec]
    args = [a, b]
    if res is not None:
        in_specs.append(pl.BlockSpec((tm, tn), lambda i, j: (i, j)))
        args.append(res)
    return pl.pallas_call(
        body, name=name, grid=(m // tm, n // tn), in_specs=in_specs,
        out_specs=pl.BlockSpec((tm, tn), lambda i, j: (i, j)),
        out_shape=jax.ShapeDtypeStruct((m, n), out_dtype),
        compiler_params=_params(("parallel", "parallel")),
    )(*args)


def _ln_fwd(x, y, g, b):
    s, d = x.shape
    tr = min(256, s)

    def body(x_ref, y_ref, g_ref, b_ref, z_ref, o_ref, ob_ref):
        z = DN_ALPHA * x_ref[...] + y_ref[...]
        mu = jnp.mean(z, axis=1, keepdims=True)
        zc = z - mu
        var = jnp.mean(zc * zc, axis=1, keepdims=True)
        o = zc * lax.rsqrt(var + LN_EPS) * g_ref[...] + b_ref[...]
        z_ref[...] = z
        o_ref[...] = o
        ob_ref[...] = o.astype(MXU_DTYPE)

    row = pl.BlockSpec((tr, d), lambda i: (i, 0))
    vec = pl.BlockSpec((1, d), lambda i: (0, 0))
    return pl.pallas_call(
        body, name="ln_fwd", grid=(s // tr,), in_specs=[row, row, vec, vec], out_specs=[row, row, row],
        out_shape=[jax.ShapeDtypeStruct((s, d), F32), jax.ShapeDtypeStruct((s, d), F32),
                   jax.ShapeDtypeStruct((s, d), MXU_DTYPE)],
        compiler_params=_params(("parallel",)),
    )(x, y, g.reshape(1, d), b.reshape(1, d))


def _ln_bwd(dout, z, g):
    s, d = z.shape
    tr = min(256, s)

    def body(do_ref, z_ref, g_ref, dz_ref, dzb_ref, dg_ref, db_ref):
        @pl.when(pl.program_id(0) == 0)
        def _():
            dg_ref[...] = jnp.zeros_like(dg_ref)
            db_ref[...] = jnp.zeros_like(db_ref)

        zz = z_ref[...]
        do = do_ref[...]
        mu = jnp.mean(zz, axis=1, keepdims=True)
        zc = zz - mu
        var = jnp.mean(zc * zc, axis=1, keepdims=True)
        rstd = lax.rsqrt(var + LN_EPS)
        zh = zc * rstd
        dg_ref[...] += jnp.sum(do * zh, axis=0, keepdims=True)
        db_ref[...] += jnp.sum(do, axis=0, keepdims=True)
        dzh = do * g_ref[...]
        dz = rstd * (dzh - jnp.mean(dzh, axis=1, keepdims=True) - zh * jnp.mean(dzh * zh, axis=1, keepdims=True))
        dz_ref[...] = dz
        dzb_ref[...] = dz.astype(MXU_DTYPE)

    row = pl.BlockSpec((tr, d), lambda i: (i, 0))
    vec = pl.BlockSpec((1, d), lambda i: (0, 0))
    return pl.pallas_call(
        body, name="ln_bwd", grid=(s // tr,), in_specs=[row, row, vec], out_specs=[row, row, vec, vec],
        out_shape=[jax.ShapeDtypeStruct((s, d), F32), jax.ShapeDtypeStruct((s, d), MXU_DTYPE),
                   jax.ShapeDtypeStruct((1, d), F32), jax.ShapeDtypeStruct((1, d), F32)],
        compiler_params=_params(("arbitrary",)),
    )(dout, z, g.reshape(1, d))


def _loss_and_grad(out, tgt):
    s, d = out.shape
    tr = min(256, s)

    def body(o_ref, t_ref, l_ref, do_ref):
        @pl.when(pl.program_id(0) == 0)
        def _():
            l_ref[...] = jnp.zeros_like(l_ref)

        e = o_ref[...] - t_ref[...]
        do_ref[...] = e * (1.0 / d)
        rows = jnp.sum(e * e, axis=1, keepdims=True)
        l_ref[...] += (0.5 / d) * jnp.sum(rows, axis=0, keepdims=True)

    row = pl.BlockSpec((tr, d), lambda i: (i, 0))
    return pl.pallas_call(
        body, name="loss", grid=(s // tr,), in_specs=[row, row],
        out_specs=[pl.BlockSpec((1, 1), lambda i: (0, 0)), row],
        out_shape=[jax.ShapeDtypeStruct((1, 1), F32), jax.ShapeDtypeStruct((s, d), F32)],
        compiler_params=_params(("arbitrary",)),
    )(out, tgt)


Q0, K0, V0, BG0 = 3072, 4096, 4224, 4352
T = CHUNK


def _lane_ids(width):
    return lax.broadcasted_iota(jnp.int32, (T, width), 1)


def _rope_swap(x):
    w = x.shape[1]
    l64 = _lane_ids(w) % B_HEAD_DIM
    half = ROT_DIM // 2
    return jnp.where(l64 < half, pltpu.roll(x, w - half, 1), jnp.where(l64 < ROT_DIM, pltpu.roll(x, half, 1), 0.0))


def _rope(x, c, s):
    return x * c + _rope_swap(x) * s


def _rope_bwd(d, c, s):
    return d * c + _rope_swap(d * s)


def _tril():
    return lax.broadcasted_iota(jnp.int32, (T, T), 0) >= lax.broadcasted_iota(jnp.int32, (T, T), 1)


def _band_mask(has_prev):
    qi = lax.broadcasted_iota(jnp.int32, (T, 2 * T), 0)
    kj = lax.broadcasted_iota(jnp.int32, (T, 2 * T), 1)
    return (kj > qi) & (kj <= qi + WINDOW) & ((kj >= WINDOW) | has_prev)


def _a_norm(v, g, b):
    mu = jnp.mean(v, axis=1, keepdims=True)
    vc = v - mu
    var = jnp.mean(vc * vc, axis=1, keepdims=True)
    rstd = lax.rsqrt(var + LN_EPS)
    vh = vc * rstd
    return vh, rstd, vh * g + b


def _kv_operands(hk, k_rot, kp_rot, vv, vp):
    lo = _lane_ids(LANES) < B_HEAD_DIM
    mine = lo if hk == 0 else jnp.logical_not(lo)
    kc = jnp.where(mine, k_rot, 0.0)
    kp = jnp.where(mine, kp_rot, 0.0)
    k2 = jnp.concatenate([kp + pltpu.roll(kp, B_HEAD_DIM, 1), kc + pltpu.roll(kc, B_HEAD_DIM, 1)], axis=0)
    vm = jnp.concatenate([jnp.where(mine, vp, 0.0), jnp.where(mine, vv, 0.0)], axis=0)
    vs = pltpu.roll(vm, B_HEAD_DIM, 1)
    v_lo, v_hi = (vm, vs) if hk == 0 else (vs, vm)
    return k2.astype(MXU_DTYPE), v_lo.astype(MXU_DTYPE), v_hi.astype(MXU_DTYPE), (vm + vs).astype(MXU_DTYPE)


def _softmax_with_sink(qm, k2, valid, sink):
    s = _dot_nt(qm, k2) * ATTN_SCALE
    s = jnp.where(valid, s, NEG_BIG)
    m = jnp.maximum(jnp.max(s, axis=1, keepdims=True), sink)
    ex = jnp.exp(s - m)
    es = jnp.exp(sink - m)
    inv = 1.0 / (jnp.sum(ex, axis=1, keepdims=True) + es)
    return ex * inv, es * inv


def _even_specs(s):
    nb = s // T
    return nb, {
        'h': lambda r: pl.BlockSpec((T, EVEN_IN), lambda i: (r(i), 0)),
        'kprev': lambda r: pl.BlockSpec((T, LANES), lambda i: (jnp.maximum(r(i) - 1, 0), K0 // LANES)),
        'vprev': lambda r: pl.BlockSpec((T, LANES), lambda i: (jnp.maximum(r(i) - 1, 0), V0 // LANES)),
        'tab': lambda r: pl.BlockSpec((T, A_WIDTH), lambda i: (r(i), 0)),
        'tabprev': lambda r: pl.BlockSpec((T, LANES), lambda i: (jnp.maximum(r(i) - 1, 0), 0)),
    }


def _full(shape):
    nd = len(shape)
    return pl.BlockSpec(shape, lambda i: (0,) * nd)


def _even_fwd(h, cq, sq, ln_g, ln_b, ws, bs_t, sinks):
    s = h.shape[0]
    nb, sp = _even_specs(s)

    def body(h_ref, kp_ref, vp_ref, cq_ref, sq_ref, ckp_ref, skp_ref, g_ref, b_ref, ws_ref, bst_ref, sink_ref, mix_ref):
        i = pl.program_id(0)
        tril = _tril()
        _, _, vn = _a_norm(h_ref[:, 1024:2048], g_ref[...], b_ref[...])
        vnb = vn.astype(MXU_DTYPE)
        for g in range(A_GROUPS):
            c0, c1 = g * LANES, (g + 1) * LANES
            wg = jnp.where(tril, ws_ref[g], 0.0).astype(MXU_DTYPE)
            mixed = _dot(wg, vnb[:, c0:c1]) + bst_ref[:, g:g + 1]
            sil, _ = _silu_and_grad(h_ref[:, 2048 + c0:2048 + c1])
            mix_ref[:, c0:c1] = (h_ref[:, c0:c1] * mixed * sil).astype(mix_ref.dtype)

        k_rot = _rope(h_ref[:, K0:K0 + LANES], cq_ref[:, 0:LANES], sq_ref[:, 0:LANES])
        kp_rot = _rope(kp_ref[...], ckp_ref[...], skp_ref[...])
        valid = _band_mask(i > 0)
        lo = _lane_ids(LANES) < B_HEAD_DIM
        for hk in range(B_KV_HEADS):
            k2, v_lo, v_hi, _ = _kv_operands(hk, k_rot, kp_rot, h_ref[:, V0:V0 + LANES], vp_ref[...])
            for pp in range(4):
                pair = hk * 4 + pp
                c0, c1 = pair * LANES, (pair + 1) * LANES
                qp = _rope(h_ref[:, Q0 + c0:Q0 + c1], cq_ref[:, c0:c1], sq_ref[:, c0:c1])
                acc = jnp.zeros((T, LANES), F32)
                for e in range(2):
                    head = 2 * pair + e
                    qm = jnp.where(lo if e == 0 else jnp.logical_not(lo), qp, 0.0).astype(MXU_DTYPE)
                    p, _ = _softmax_with_sink(qm, k2, valid, sink_ref[:, head:head + 1])
                    acc = acc + _dot(p.astype(MXU_DTYPE), v_lo if e == 0 else v_hi)
                sil, _ = _silu_and_grad(h_ref[:, BG0 + c0:BG0 + c1])
                mix_ref[:, A_WIDTH + c0:A_WIDTH + c1] = (acc * sil).astype(mix_ref.dtype)

    ident = lambda i: i
    return pl.pallas_call(
        body, name="even_fwd", grid=(nb,),
        in_specs=[sp['h'](ident), sp['kprev'](ident), sp['vprev'](ident), sp['tab'](ident), sp['tab'](ident),
                  sp['tabprev'](ident), sp['tabprev'](ident), _full((1, A_WIDTH)), _full((1, A_WIDTH)),
                  _full((A_GROUPS, T, T)), _full((T, A_GROUPS)), _full((1, B_Q_HEADS))],
        out_specs=pl.BlockSpec((T, 2 * A_WIDTH), lambda i: (i, 0)),
        out_shape=jax.ShapeDtypeStruct((s, 2 * A_WIDTH), MXU_DTYPE),
        compiler_params=_params(("parallel",)),
    )(h, h, h, cq, sq, cq, sq, ln_g.reshape(1, -1), ln_b.reshape(1, -1), ws, bs_t, sinks.reshape(1, -1))


def _even_bwd(h, dmix, cq, sq, ln_g, ln_b, ws, bs_t, sinks):
    s = h.shape[0]
    nb, sp = _even_specs(s)

    def body(h_ref, kp_ref, vp_ref, cq_ref, sq_ref, ckp_ref, skp_ref, dm_ref, g_ref, b_ref, ws_ref, bst_ref, sink_ref,
             dh_ref, dg_ref, db_ref, dws_ref, dbst_ref, dsink_ref, dvn_scr, ck_scr, cv_scr):
        i = pl.program_id(0)

        @pl.when(i == 0)
        def _():
            for ref in (dg_ref, db_ref, dws_ref, dbst_ref, dsink_ref, ck_scr, cv_scr):
                ref[...] = jnp.zeros_like(ref)

        tril = _tril()
        vh, rstd, vn = _a_norm(h_ref[:, 1024:2048], g_ref[...], b_ref[...])
        vnb = vn.astype(MXU_DTYPE)
        for g in range(A_GROUPS):
            c0, c1 = g * LANES, (g + 1) * LANES
            wg = jnp.where(tril, ws_ref[g], 0.0).astype(MXU_DTYPE)
            mixed = _dot(wg, vnb[:, c0:c1]) + bst_ref[:, g:g + 1]
            ag = h_ref[:, 2048 + c0:2048 + c1]
            u = h_ref[:, c0:c1]
            sil, dsil = _silu_and_grad(ag)
            da = dm_ref[:, c0:c1]
            dh_ref[:, c0:c1] = (da * mixed * sil).astype(dh_ref.dtype)
            dh_ref[:, 2048 + c0:2048 + c1] = (da * u * mixed * dsil).astype(dh_ref.dtype)
            dmx = da * u * sil
            dmb = dmx.astype(MXU_DTYPE)
            dvn_scr[:, c0:c1] = _dot_tn(wg, dmb)
            dws_ref[g] += jnp.where(tril, _dot_nt(dmb, vnb[:, c0:c1]), 0.0)
            dbst_ref[:, g:g + 1] += jnp.sum(dmx, axis=1, keepdims=True)
        dvn = dvn_scr[...]
        dg_ref[...] += jnp.sum(dvn * vh, axis=0, keepdims=True)
        db_ref[...] += jnp.sum(dvn, axis=0, keepdims=True)
        dvh = dvn * g_ref[...]
        dv = rstd * (dvh - jnp.mean(dvh, axis=1, keepdims=True) - vh * jnp.mean(dvh * vh, axis=1, keepdims=True))
        dh_ref[:, 1024:2048] = dv.astype(dh_ref.dtype)

        ck_cur, sk_cur = cq_ref[:, 0:LANES], sq_ref[:, 0:LANES]
        k_rot = _rope(h_ref[:, K0:K0 + LANES], ck_cur, sk_cur)
        kp_rot = _rope(kp_ref[...], ckp_ref[...], skp_ref[...])
        valid = _band_mask(i < nb - 1)
        lo = _lane_ids(LANES) < B_HEAD_DIM
        hi = jnp.logical_not(lo)
        dk_all = jnp.zeros((2 * T, LANES), F32)
        dv_all = jnp.zeros((2 * T, LANES), F32)
        for hk in range(B_KV_HEADS):
            k2, v_lo, v_hi, v2 = _kv_operands(hk, k_rot, kp_rot, h_ref[:, V0:V0 + LANES], vp_ref[...])
            acc_k = jnp.zeros((2 * T, LANES), F32)
            acc_v = jnp.zeros((2 * T, LANES), F32)
            for pp in range(4):
                pair = hk * 4 + pp
                c0, c1 = pair * LANES, (pair + 1) * LANES
                cqp, sqp = cq_ref[:, c0:c1], sq_ref[:, c0:c1]
                qp = _rope(h_ref[:, Q0 + c0:Q0 + c1], cqp, sqp)
                qms, ps, pss = [], [], []
                ob = jnp.zeros((T, LANES), F32)
                for e in range(2):
                    head = 2 * pair + e
                    qm = jnp.where(lo if e == 0 else hi, qp, 0.0).astype(MXU_DTYPE)
                    p, psink = _softmax_with_sink(qm, k2, valid, sink_ref[:, head:head + 1])
                    ob = ob + _dot(p.astype(MXU_DTYPE), v_lo if e == 0 else v_hi)
                    qms.append(qm)
                    ps.append(p)
                    pss.append(psink)
                sil, dsil = _silu_and_grad(h_ref[:, BG0 + c0:BG0 + c1])
                dbo = dm_ref[:, A_WIDTH + c0:A_WIDTH + c1]
                dh_ref[:, BG0 + c0:BG0 + c1] = (dbo * ob * dsil).astype(dh_ref.dtype)
                dob = dbo * sil
                dq_rot = jnp.zeros((T, LANES), F32)
                for e in range(2):
                    head = 2 * pair + e
                    sel = lo if e == 0 else hi
                    dobm = jnp.where(sel, dob, 0.0).astype(MXU_DTYPE)
                    dp = _dot_nt(dobm, v2)
                    rs = jnp.sum(ps[e] * dp, axis=1, keepdims=True)
                    ds = ps[e] * (dp - rs) * ATTN_SCALE
                    dsink_ref[:, head:head + 1] += -jnp.sum(pss[e] * rs, axis=0, keepdims=True)
                    dsb = ds.astype(MXU_DTYPE)
                    dq_rot = dq_rot + jnp.where(sel, _dot(dsb, k2), 0.0)
                    acc_k = acc_k + _dot_tn(dsb, qms[e])
                    acc_v = acc_v + _dot_tn(ps[e].astype(MXU_DTYPE), dobm)
                dh_ref[:, Q0 + c0:Q0 + c1] = _rope_bwd(dq_rot, cqp, sqp).astype(dh_ref.dtype)
            lo2 = lax.broadcasted_iota(jnp.int32, (2 * T, LANES), 1) < B_HEAD_DIM
            mine = lo2 if hk == 0 else jnp.logical_not(lo2)
            dk_all = dk_all + jnp.where(mine, acc_k + pltpu.roll(acc_k, B_HEAD_DIM, 1), 0.0)
            dv_all = dv_all + jnp.where(mine, acc_v + pltpu.roll(acc_v, B_HEAD_DIM, 1), 0.0)
        dk_rot = dk_all[T:2 * T] + ck_scr[...]
        dvv = dv_all[T:2 * T] + cv_scr[...]
        ck_scr[...] = dk_all[0:T]
        cv_scr[...] = dv_all[0:T]
        dh_ref[:, K0:K0 + LANES] = _rope_bwd(dk_rot, ck_cur, sk_cur).astype(dh_ref.dtype)
        dh_ref[:, V0:V0 + LANES] = dvv.astype(dh_ref.dtype)

    rev = lambda i: nb - 1 - i
    return pl.pallas_call(
        body, name="even_bwd", grid=(nb,),
        in_specs=[sp['h'](rev), sp['kprev'](rev), sp['vprev'](rev), sp['tab'](rev), sp['tab'](rev),
                  sp['tabprev'](rev), sp['tabprev'](rev), pl.BlockSpec((T, 2 * A_WIDTH), lambda i: (rev(i), 0)),
                  _full((1, A_WIDTH)), _full((1, A_WIDTH)), _full((A_GROUPS, T, T)), _full((T, A_GROUPS)),
                  _full((1, B_Q_HEADS))],
        out_specs=[pl.BlockSpec((T, EVEN_IN), lambda i: (rev(i), 0)), _full((1, A_WIDTH)), _full((1, A_WIDTH)),
                   _full((A_GROUPS, T, T)), _full((T, A_GROUPS)), _full((1, B_Q_HEADS))],
        out_shape=[jax.ShapeDtypeStruct((s, EVEN_IN), MXU_DTYPE), jax.ShapeDtypeStruct((1, A_WIDTH), F32),
                   jax.ShapeDtypeStruct((1, A_WIDTH), F32), jax.ShapeDtypeStruct((A_GROUPS, T, T), F32),
                   jax.ShapeDtypeStruct((T, A_GROUPS), F32), jax.ShapeDtypeStruct((1, B_Q_HEADS), F32)],
        scratch_shapes=[pltpu.VMEM((T, A_WIDTH), F32), pltpu.VMEM((T, LANES), F32), pltpu.VMEM((T, LANES), F32)],
        compiler_params=_params(("arbitrary",)),
    )(h, h, h, cq, sq, cq, sq, dmix, ln_g.reshape(1, -1), ln_b.reshape(1, -1), ws, bs_t, sinks.reshape(1, -1))


def _shift_down(x, d, fill=0.0):
    rows = lax.broadcasted_iota(jnp.int32, x.shape, 0)
    return jnp.where(rows >= d, pltpu.roll(x, d, 0), fill)


def _shift_up(x, d, fill=0.0):
    n = x.shape[0]
    rows = lax.broadcasted_iota(jnp.int32, x.shape, 0)
    return jnp.where(rows < n - d, pltpu.roll(x, n - d, 0), fill)


def _scan(a, b, shift):
    n = a.shape[0]
    d = 1
    while d < n:
        b = a * shift(b, d) + b
        if 2 * d < n:
            a = a * shift(a, d, 1.0)
        d *= 2
    return b


def _neg_expm1(y):
    u = jnp.exp(y)
    um1 = u - 1.0
    safe = jnp.where(um1 == 0.0, 1.0, jnp.log(u))
    near = jnp.where(um1 == 0.0, y, um1 * y / safe)
    return -jnp.where(y > -0.5, near, um1)


def _softplus(x):
    return jnp.maximum(x, 0.0) + jnp.log1p(jnp.exp(-jnp.abs(x)))


def _lru_forward(xc, cw, cb, wa, ba, wx, bx, lam):
    shifted = [_shift_down(xc, CONV_WIDTH - 1 - j) if j < CONV_WIDTH - 1 else xc for j in range(CONV_WIDTH)]
    xconv = cb
    for j in range(CONV_WIDTH):
        xconv = xconv + cw[j:j + 1, :] * shifted[j]
    xb = xconv.astype(MXU_DTYPE)
    r = _sigmoid(_dot(xb, wa.astype(MXU_DTYPE)) + ba)
    ig = _sigmoid(_dot(xb, wx.astype(MXU_DTYPE)) + bx)
    sp = _softplus(-lam)
    log_a = -LRU_C * r * sp
    a = jnp.exp(log_a)
    mult = jnp.sqrt(_neg_expm1(2.0 * log_a))
    hs = _scan(a, mult * ig * xconv, _shift_down)
    return dict(shifted=shifted, xconv=xconv, xb=xb, r=r, ig=ig, sp=sp, a=a, mult=mult, hs=hs)


def _col(width, off):
    return lambda s: pl.BlockSpec((s, width), lambda i: (0, off + i))


def _lru_specs(s):
    w = LANES
    return [_col(w, 0)(s), _col(w, C_WIDTH // w)(s), pl.BlockSpec((CONV_WIDTH, w), lambda i: (0, i)),
            pl.BlockSpec((1, w), lambda i: (0, i)), pl.BlockSpec((None, w, w), lambda i: (i, 0, 0)),
            pl.BlockSpec((1, w), lambda i: (0, i)), pl.BlockSpec((None, w, w), lambda i: (i, 0, 0)),
            pl.BlockSpec((1, w), lambda i: (0, i)), pl.BlockSpec((1, w), lambda i: (0, i))]


def _lru_fwd(h, cw, cb, wa, ba, wx, bx, lam):
    s = h.shape[0]

    def body(xc_ref, cg_ref, cw_ref, cb_ref, wa_ref, ba_ref, wx_ref, bx_ref, lam_ref, o_ref):
        f = _lru_forward(xc_ref[...], cw_ref[...], cb_ref[...], wa_ref[...], ba_ref[...], wx_ref[...], bx_ref[...],
                         lam_ref[...])
        sil, _ = _silu_and_grad(cg_ref[...])
        o_ref[...] = (f['hs'] * sil).astype(o_ref.dtype)

    return pl.pallas_call(
        body, name="lru_fwd", grid=(C_HEADS,), in_specs=_lru_specs(s), out_specs=_col(LANES, 0)(s),
        out_shape=jax.ShapeDtypeStruct((s, C_WIDTH), MXU_DTYPE), compiler_params=_params(("parallel",)),
    )(h, h, cw, cb.reshape(1, -1), wa, ba.reshape(1, -1), wx, bx.reshape(1, -1), lam.reshape(1, -1))


def _lru_bwd(h, dmix, cw, cb, wa, ba, wx, bx, lam):
    s = h.shape[0]

    def body(xc_ref, cg_ref, cw_ref, cb_ref, wa_ref, ba_ref, wx_ref, bx_ref, lam_ref, dco_ref,
             dxc_ref, dcg_ref, dcw_ref, dcb_ref, dwa_ref, dba_ref, dwx_ref, dbx_ref, dlam_ref):
        cw, lam = cw_ref[...], lam_ref[...]
        wab, wxb = wa_ref[...].astype(MXU_DTYPE), wx_ref[...].astype(MXU_DTYPE)
        f = _lru_forward(xc_ref[...], cw, cb_ref[...], wa_ref[...], ba_ref[...], wx_ref[...], bx_ref[...], lam)
        sil, dsil = _silu_and_grad(cg_ref[...])
        dco = dco_ref[...]
        hs, a, mult, ig, r, xconv = f['hs'], f['a'], f['mult'], f['ig'], f['r'], f['xconv']
        dcg_ref[...] = (dco * hs * dsil).astype(dcg_ref.dtype)
        lamb = _scan(_shift_up(a, 1), dco * sil, _shift_up)
        da = lamb * _shift_down(hs, 1)
        d_ig = lamb * mult * xconv
        d_mult = lamb * ig * xconv
        dxconv = lamb * mult * ig
        d_log_a = da * a - d_mult * (a * a) / mult
        dlam_ref[...] = jnp.sum(d_log_a * r, axis=0, keepdims=True) * LRU_C * _sigmoid(-lam)
        dpa = d_log_a * (-LRU_C * f['sp']) * r * (1.0 - r)
        dpx = d_ig * ig * (1.0 - ig)
        dba_ref[...] = jnp.sum(dpa, axis=0, keepdims=True)
        dbx_ref[...] = jnp.sum(dpx, axis=0, keepdims=True)
        dpab, dpxb = dpa.astype(MXU_DTYPE), dpx.astype(MXU_DTYPE)
        dwa_ref[...] = _dot_tn(f['xb'], dpab)
        dwx_ref[...] = _dot_tn(f['xb'], dpxb)
        dxconv = dxconv + _dot_nt(dpab, wab) + _dot_nt(dpxb, wxb)
        dcb_ref[...] = jnp.sum(dxconv, axis=0, keepdims=True)
        dxc = jnp.zeros_like(dxconv)
        for j in range(CONV_WIDTH):
            dcw_ref[j:j + 1, :] = jnp.sum(dxconv * f['shifted'][j], axis=0, keepdims=True)
            back = _shift_up(dxconv, CONV_WIDTH - 1 - j) if j < CONV_WIDTH - 1 else dxconv
            dxc = dxc + cw[j:j + 1, :] * back
        dxc_ref[...] = dxc.astype(dxc_ref.dtype)

    w = LANES
    vec = pl.BlockSpec((1, w), lambda i: (0, i))
    mat = pl.BlockSpec((None, w, w), lambda i: (i, 0, 0))
    vshape = jax.ShapeDtypeStruct((1, C_WIDTH), F32)
    mshape = jax.ShapeDtypeStruct((C_HEADS, w, w), F32)
    return pl.pallas_call(
        body, name="lru_bwd", grid=(C_HEADS,), in_specs=_lru_specs(s) + [_col(w, 0)(s)],
        out_specs=[_col(w, 0)(s), _col(w, 0)(s), pl.BlockSpec((CONV_WIDTH, w), lambda i: (0, i)), vec, mat, vec, mat,
                   vec, vec],
        out_shape=[jax.ShapeDtypeStruct((s, C_WIDTH), MXU_DTYPE), jax.ShapeDtypeStruct((s, C_WIDTH), MXU_DTYPE),
                   jax.ShapeDtypeStruct((CONV_WIDTH, C_WIDTH), F32), vshape, mshape, vshape, mshape, vshape, vshape],
        compiler_params=_params(("parallel",)),
    )(h, h, cw, cb.reshape(1, -1), wa, ba.reshape(1, -1), wx, bx.reshape(1, -1), lam.reshape(1, -1), dmix)


def _pool_window(g):
    return jnp.left_shift(2, g).astype(F32)


def _select_window(g, sums):
    out = sums[-1]
    for k in range(len(sums) - 2, -1, -1):
        out = jnp.where(g == k, sums[k], out)
    return out


def _window_sums(x, shift):
    sums, cur = [], x
    for k in range(D_GROUPS):
        cur = cur + shift(cur, 2 ** k)
        sums.append(cur)
    return sums


def _pool_specs(s):
    w = D_GROUP_DIM
    return [_col(w, 2 * C_WIDTH // w)(s), _col(w, 3 * C_WIDTH // w)(s),
            pl.BlockSpec((None, w, w), lambda i: (i, 0, 0)), pl.BlockSpec((1, w), lambda i: (0, i))]


def _pool_forward(xd, g):
    rows = lax.broadcasted_iota(jnp.int32, xd.shape, 0).astype(F32) + 1.0
    cnt = jnp.minimum(rows, _pool_window(g))
    pooled = _select_window(g, _window_sums(xd, _shift_down)) / cnt - xd
    return pooled, cnt


def _pool_fwd(h, w_pool, d_scale):
    s = h.shape[0]

    def body(xd_ref, dg_ref, wp_ref, sc_ref, o_ref):
        pooled, _ = _pool_forward(xd_ref[...], pl.program_id(0))
        mixed = _dot(pooled.astype(MXU_DTYPE), wp_ref[...].astype(MXU_DTYPE))
        sil, _ = _silu_and_grad(dg_ref[...])
        o_ref[...] = (mixed * sc_ref[...] * sil).astype(o_ref.dtype)

    return pl.pallas_call(
        body, name="pool_fwd", grid=(D_GROUPS,), in_specs=_pool_specs(s), out_specs=_col(D_GROUP_DIM, 0)(s),
        out_shape=jax.ShapeDtypeStruct((s, D_WIDTH), MXU_DTYPE), compiler_params=_params(("parallel",)),
    )(h, h, w_pool, d_scale.reshape(1, -1))


def _pool_bwd(h, dmix, w_pool, d_scale):
    s = h.shape[0]
    w = D_GROUP_DIM

    def body(xd_ref, dg_ref, wp_ref, sc_ref, ddo_ref, dxd_ref, ddg_ref, dwp_ref, dsc_ref):
        g = pl.program_id(0)
        pooled, cnt = _pool_forward(xd_ref[...], g)
        pb = pooled.astype(MXU_DTYPE)
        wpb = wp_ref[...].astype(MXU_DTYPE)
        mixed = _dot(pb, wpb)
        sil, dsil = _silu_and_grad(dg_ref[...])
        ddo = ddo_ref[...]
        sc = sc_ref[...]
        ddg_ref[...] = (ddo * mixed * sc * dsil).astype(ddg_ref.dtype)
        dms = ddo * sil
        dsc_ref[...] = jnp.sum(dms * mixed, axis=0, keepdims=True)
        dmb = (dms * sc).astype(MXU_DTYPE)
        dwp_ref[...] = _dot_tn(pb, dmb)
        dpooled = _dot_nt(dmb, wpb)
        dxd = _select_window(g, _window_sums(dpooled / cnt, _shift_up)) - dpooled
        dxd_ref[...] = dxd.astype(dxd_ref.dtype)

    return pl.pallas_call(
        body, name="pool_bwd", grid=(D_GROUPS,), in_specs=_pool_specs(s) + [_col(w, C_WIDTH // w)(s)],
        out_specs=[_col(w, 0)(s), _col(w, 0)(s), pl.BlockSpec((None, w, w), lambda i: (i, 0, 0)),
                   pl.BlockSpec((1, w), lambda i: (0, i))],
        out_shape=[jax.ShapeDtypeStruct((s, D_WIDTH), MXU_DTYPE), jax.ShapeDtypeStruct((s, D_WIDTH), MXU_DTYPE),
                   jax.ShapeDtypeStruct((D_GROUPS, w, w), F32), jax.ShapeDtypeStruct((1, D_WIDTH), F32)],
        compiler_params=_params(("parallel",)),
    )(h, h, w_pool, d_scale.reshape(1, -1), dmix)


def _rope_tables(positions):
    s = positions.shape[0]
    inv_freq = ROPE_THETA ** (-jnp.arange(0, ROT_DIM, 2, dtype=F32) / ROT_DIM)
    ang = positions.astype(F32)[:, None] * inv_freq
    c, sn = jnp.cos(ang), jnp.sin(ang)
    rest = B_HEAD_DIM - ROT_DIM
    c64 = jnp.concatenate([c, c, jnp.ones((s, rest), F32)], axis=1)
    s64 = jnp.concatenate([-sn, sn, jnp.zeros((s, rest), F32)], axis=1)
    return jnp.tile(c64, (1, B_Q_HEADS)), jnp.tile(s64, (1, B_Q_HEADS))


def _local_step(x, positions, target, w):
    cq, sq = _rope_tables(positions)
    saved = []
    xin, xin_b = x, x
    for layer in range(DEPTH):
        j = layer // 2
        if layer % 2 == 0:
            h = _matmul(xin_b, w['even_w_in'][j], mode='nn', out_dtype=F32, tm=1024, tn=768, name="mm_even_in")
            small = (w['even_a_ln_g'][j], w['even_a_ln_b'][j], w['even_a_ws'][j], w['even_a_bs'][j].T,
                     w['even_b_sinks'][j])
            mix = _even_fwd(h, cq, sq, *small)
            y = _matmul(mix, w['even_w_out'][j], mode='nn', out_dtype=F32, tm=1024, tn=1024, name="mm_out")
            z, out, out_b = _ln_fwd(xin, y, w['even_ln_g'][j], w['even_ln_b'][j])
        else:
            h = _matmul(xin_b, w['odd_w_in'][j], mode='nn', out_dtype=F32, tm=1024, tn=1024, name="mm_odd_in")
            c_out = _lru_fwd(h, w['odd_conv_w'][j], w['odd_conv_b'][j], w['odd_w_a'][j], w['odd_b_a'][j],
                             w['odd_w_x'][j], w['odd_b_x'][j], w['odd_lam'][j])
            d_out = _pool_fwd(h, w['odd_w_pool'][j], w['odd_d_scale'][j])
            mix = jnp.concatenate([c_out, d_out], axis=1)
            y = _matmul(mix, w['odd_w_out'][j], mode='nn', out_dtype=F32, tm=1024, tn=1024, name="mm_out")
            z, out, out_b = _ln_fwd(xin, y, w['odd_ln_g'][j], w['odd_ln_b'][j])
        saved.append((xin_b, h, mix, z))
        xin, xin_b = out, out_b

    loss, dout = _loss_and_grad(xin, target)
    per_layer = {n: [None, None] for n in WEIGHT_NAMES}
    for layer in reversed(range(DEPTH)):
        j = layer // 2
        kind = 'even' if layer % 2 == 0 else 'odd'
        xin_b, h, mix, z = saved[layer]
        dz, dz_b, dg, db = _ln_bwd(dout, z, w[kind + '_ln_g'][j])
        per_layer[kind + '_ln_g'][j], per_layer[kind + '_ln_b'][j] = dg[0], db[0]
        dmix = _matmul(dz_b, w[kind + '_w_out'][j], mode='nt', out_dtype=F32, tm=1024, tn=1024, name="mm_dmix")
        per_layer[kind + '_w_out'][j] = _matmul(mix, dz_b, mode='tn', out_dtype=MXU_DTYPE, tm=1024, tn=1024,
                                                name="mm_dw_out")
        if kind == 'even':
            dh, dlg, dlb, dws, dbst, dsink = _even_bwd(
                h, dmix, cq, sq, w['even_a_ln_g'][j], w['even_a_ln_b'][j], w['even_a_ws'][j], w['even_a_bs'][j].T,
                w['even_b_sinks'][j])
            for n, val in (('even_a_ln_g', dlg[0]), ('even_a_ln_b', dlb[0]), ('even_a_ws', dws),
                           ('even_a_bs', dbst.T), ('even_b_sinks', dsink[0])):
                per_layer[n][j] = val
            tn_in = 768
        else:
            dxc, dcg, dcw, dcb, dwa, dba, dwx, dbx, dlam = _lru_bwd(
                h, dmix, w['odd_conv_w'][j], w['odd_conv_b'][j], w['odd_w_a'][j], w['odd_b_a'][j], w['odd_w_x'][j],
                w['odd_b_x'][j], w['odd_lam'][j])
            dxd, ddg, dwp, dsc = _pool_bwd(h, dmix, w['odd_w_pool'][j], w['odd_d_scale'][j])
            dh = jnp.concatenate([dxc, dcg, dxd, ddg], axis=1)
            for n, val in (('odd_conv_w', dcw), ('odd_conv_b', dcb[0]), ('odd_w_a', dwa), ('odd_b_a', dba[0]),
                           ('odd_w_x', dwx), ('odd_b_x', dbx[0]), ('odd_lam', dlam[0]), ('odd_w_pool', dwp),
                           ('odd_d_scale', dsc[0])):
                per_layer[n][j] = val
            tn_in = 1024
        dout = _matmul(dh, w[kind + '_w_in'][j], mode='nt', out_dtype=F32, tm=512, tn=512, name="mm_dx_" + kind,
                       res=dz, res_scale=DN_ALPHA)
        per_layer[kind + '_w_in'][j] = _matmul(xin_b, dh, mode='tn', out_dtype=MXU_DTYPE, tm=1024, tn=tn_in,
                                               name="mm_dw_in_" + kind)
    grads = {n: jnp.stack(v) for n, v in per_layer.items()}
    return loss, dout, grads


ANY = pl.BlockSpec(memory_space=pl.ANY)


def _where_am_i():
    return lax.axis_index("x"), lax.axis_index("y"), lax.axis_index("c")


def _other_chips(x, y):
    return [(1 - x, y), (x, 1 - y), (1 - x, 1 - y)]


def _two_level_gather(buf, name):
    r, cols = buf.shape
    half = r // 2
    assert r == 2 * half

    def body(x_ref, out_ref, send_sems, recv_sems, local_sem):
        x, y, c = _where_am_i()
        sibling = (x, y, 1 - c)
        chips = _other_chips(x, y)

        def rows(px, py, pc):
            return out_ref.at[2 * px + py, pl.ds(pl.multiple_of(pc * half, 16), half), :]

        def copy(k, block, to, src=None):
            return pltpu.make_async_remote_copy(
                src_ref=rows(*block) if src is None else src, dst_ref=rows(*block), send_sem=send_sems.at[k],
                recv_sem=recv_sems.at[k], device_id=to, device_id_type=MESH)

        mine = pltpu.make_async_copy(x_ref, out_ref.at[2 * x + y], local_sem)
        mine.start()
        my_half = x_ref.at[pl.ds(pl.multiple_of(c * half, 16), half), :]
        first = [copy(j, (x, y, c), (*chip, c), src=my_half) for j, chip in enumerate(chips)]
        for cp in first:
            cp.start()
        passed = [copy(3 + j, (*chip, c), sibling) for j, chip in enumerate(chips)]
        for j, chip in enumerate(chips):
            copy(j, (*chip, c), (x, y, c)).wait_recv()
            passed[j].start()
        for j, chip in enumerate(chips):
            copy(3 + j, (*chip, 1 - c), (x, y, c)).wait_recv()
        for cp in first + passed:
            cp.wait_send()
        mine.wait()

    return pl.pallas_call(
        body, name=name, in_specs=[ANY], out_specs=ANY,
        out_shape=jax.ShapeDtypeStruct((N_CHIPS, r, cols), buf.dtype),
        scratch_shapes=[pltpu.SemaphoreType.DMA((6,)), pltpu.SemaphoreType.DMA((6,)), pltpu.SemaphoreType.DMA],
    )(buf)


def _sibling_swap(buf, name, *, other_half_of_axis1=False, join=False):
    if other_half_of_axis1:
        n, r2, cols = buf.shape
        out_shape = (n, r2 // 2, cols)
    elif join:
        out_shape = (2,) + buf.shape
    else:
        out_shape = buf.shape

    def body(x_ref, out_ref, send_sem, recv_sem, local_sem):
        x, y, c = _where_am_i()
        if other_half_of_axis1:
            h = out_shape[1]
            src = x_ref.at[:, pl.ds(pl.multiple_of((1 - c) * h, 16), h), :]
            dst = out_ref
        elif join:
            src, dst = x_ref, out_ref.at[c]
        else:
            src, dst = x_ref, out_ref
        if join:
            mine = pltpu.make_async_copy(x_ref, out_ref.at[c], local_sem)
            mine.start()
        cp = pltpu.make_async_remote_copy(src_ref=src, dst_ref=dst, send_sem=send_sem, recv_sem=recv_sem,
                                          device_id=(x, y, 1 - c), device_id_type=MESH)
        cp.start()
        if join:
            landing = out_ref.at[1 - c]
            pltpu.make_async_remote_copy(src_ref=src, dst_ref=landing, send_sem=send_sem, recv_sem=recv_sem,
                                         device_id=(x, y, 1 - c), device_id_type=MESH).wait_recv()
            cp.wait_send()
            mine.wait()
        else:
            cp.wait()

    return pl.pallas_call(
        body, name=name, in_specs=[ANY], out_specs=ANY, out_shape=jax.ShapeDtypeStruct(out_shape, buf.dtype),
        scratch_shapes=[pltpu.SemaphoreType.DMA, pltpu.SemaphoreType.DMA, pltpu.SemaphoreType.DMA],
    )(buf)


def _chip_scatter(part, name):
    def body(p_ref, out_ref, send_sems, recv_sems, local_sem):
        x, y, c = _where_am_i()
        me = 2 * x + y
        chips = _other_chips(x, y)
        mine = pltpu.make_async_copy(p_ref.at[me], out_ref.at[me], local_sem)
        mine.start()
        sends = [pltpu.make_async_remote_copy(
            src_ref=p_ref.at[2 * px + py], dst_ref=out_ref.at[me], send_sem=send_sems.at[k], recv_sem=recv_sems.at[k],
            device_id=(px, py, c), device_id_type=MESH) for k, (px, py) in enumerate(chips)]
        for cp in sends:
            cp.start()
        for k, (px, py) in enumerate(chips):
            pltpu.make_async_remote_copy(
                src_ref=p_ref.at[me], dst_ref=out_ref.at[2 * px + py], send_sem=send_sems.at[k],
                recv_sem=recv_sems.at[k], device_id=(px, py, c), device_id_type=MESH).wait_recv()
        for cp in sends:
            cp.wait_send()
        mine.wait()

    return pl.pallas_call(
        body, name=name, in_specs=[ANY], out_specs=ANY, out_shape=jax.ShapeDtypeStruct(part.shape, part.dtype),
        scratch_shapes=[pltpu.SemaphoreType.DMA((3,)), pltpu.SemaphoreType.DMA((3,)), pltpu.SemaphoreType.DMA],
    )(part)


def _col_tile(rows, cols, itemsize, n_bufs, budget=20 * 1024 * 1024):
    best = LANES
    for t in range(LANES, cols + 1, LANES):
        if cols % t == 0 and 2 * n_bufs * rows * t * itemsize <= budget:
            best = t
    return best


def _add_halves(full, recv, core, out_dtype, name):
    n, h, cols = recv.shape
    tc = _col_tile(h, cols, 4, 3)

    def body(core_ref, a_ref, b_ref, o_ref):
        o_ref[...] = (a_ref[...].astype(F32) + b_ref[...].astype(F32)).astype(out_dtype)

    blk = (None, h, tc)
    return pl.pallas_call(
        body, name=name,
        grid_spec=pltpu.PrefetchScalarGridSpec(
            num_scalar_prefetch=1, grid=(n, cols // tc),
            in_specs=[pl.BlockSpec(blk, lambda j, i, core_ref: (j, core_ref[0], i)),
                      pl.BlockSpec(blk, lambda j, i, core_ref: (j, 0, i))],
            out_specs=pl.BlockSpec(blk, lambda j, i, core_ref: (j, 0, i))),
        out_shape=jax.ShapeDtypeStruct((n, h, cols), out_dtype),
        compiler_params=_params(("parallel", "parallel")),
    )(core.reshape(1), full, recv)


def _add2(a, b, name):
    r, cols = a.shape
    tc = _col_tile(r, cols, 4, 3)

    def body(a_ref, b_ref, o_ref):
        o_ref[...] = a_ref[...] + b_ref[...]

    blk = pl.BlockSpec((r, tc), lambda i: (0, i))
    return pl.pallas_call(body, name=name, grid=(cols // tc,), in_specs=[blk, blk], out_specs=blk,
                          out_shape=jax.ShapeDtypeStruct(a.shape, a.dtype), compiler_params=_params(("parallel",)))(a, b)


def _sum_slots(parts, name):
    n, h, cols = parts.shape
    tc = _col_tile(h, cols, 4, n + 1)

    def body(p_ref, o_ref):
        acc = p_ref[0].astype(F32)
        for k in range(1, n):
            acc = acc + p_ref[k].astype(F32)
        o_ref[...] = acc

    return pl.pallas_call(
        body, name=name, grid=(cols // tc,), in_specs=[pl.BlockSpec((n, h, tc), lambda i: (0, 0, i))],
        out_specs=pl.BlockSpec((h, tc), lambda i: (0, i)), out_shape=jax.ShapeDtypeStruct((h, cols), F32),
        compiler_params=_params(("parallel",)),
    )(parts)


def _adamw(w, g, m, v, name):
    shape = w.shape
    cols = shape[-1]
    rows = math.prod(shape[:-1])
    tr = rows
    if rows % 8 == 0:
        tr = 8
        for t in range(8, rows + 1, 8):
            if rows % t == 0 and t * cols * 4 <= 1536 * 1024:
                tr = t
    c1 = 1.0 - ADAM_B1 ** ADAM_STEP
    c2 = 1.0 - ADAM_B2 ** ADAM_STEP

    def body(w_ref, g_ref, m_ref, v_ref, d_ref, nm_ref, nv_ref):
        gg = g_ref[...]
        nm = ADAM_B1 * m_ref[...] + (1.0 - ADAM_B1) * gg
        nv = ADAM_B2 * v_ref[...] + (1.0 - ADAM_B2) * (gg * gg)
        d_ref[...] = -ADAM_LR * ((nm / c1) / (jnp.sqrt(nv / c2) + ADAM_EPS) + ADAM_WD * w_ref[...])
        nm_ref[...] = nm
        nv_ref[...] = nv

    blk = pl.BlockSpec((tr, cols), lambda i: (i, 0))
    flat = jax.ShapeDtypeStruct((rows, cols), F32)
    outs = pl.pallas_call(
        body, name=name, grid=(rows // tr,), in_specs=[blk] * 4, out_specs=[blk] * 3, out_shape=[flat] * 3,
        compiler_params=_params(("parallel",)),
    )(*[a.reshape(rows, cols) for a in (w, g, m, v)])
    return [o.reshape(shape) for o in outs]


def _chip_shape(name):
    shape = list(FULL_SHAPES[name])
    axis = BIG_SHARD_AXIS.get(name, SMALL_SHARD_AXIS.get(name))
    if axis is not None:
        shape[axis] //= N_CHIPS
    return tuple(shape)


def _round_up(n, k):
    return -(-n // k) * k


BIG_ROWS = {n: math.prod(_chip_shape(n)) // PACK_COLS for n in BIG_NAMES}
BIG_TOTAL_ROWS = sum(BIG_ROWS.values())
SHARDED_SMALL_WORDS = sum(math.prod(_chip_shape(n)) for n in SMALL_SHARDED)
SHARDED_SMALL_ROWS = _round_up(-(-SHARDED_SMALL_WORDS // PACK_COLS), 32)
SMALL_WORDS = sum(math.prod(FULL_SHAPES[n]) for n in SMALL_NAMES)
SMALL_ROWS = _round_up(-(-SMALL_WORDS // PACK_COLS), 32)


def _pack_rows(flat, rows):
    return jnp.pad(flat, (0, rows * PACK_COLS - flat.shape[0])).reshape(rows, PACK_COLS)


def _pack_for_gather(shards):
    big = jnp.concatenate([shards[n].astype(BF16).reshape(-1) for n in BIG_NAMES])
    small = jnp.concatenate([shards[n].reshape(-1) for n in SMALL_SHARDED])
    return big.reshape(BIG_TOTAL_ROWS, PACK_COLS), _pack_rows(small, SHARDED_SMALL_ROWS)


def _unpack_gathered(gathered, gathered_small):
    out = {}
    r0 = 0
    for n in BIG_NAMES:
        blocks = gathered[:, r0:r0 + BIG_ROWS[n]].reshape((N_CHIPS,) + _chip_shape(n))
        out[n] = jnp.concatenate([blocks[j] for j in range(N_CHIPS)], axis=BIG_SHARD_AXIS[n])
        r0 += BIG_ROWS[n]
    words = gathered_small.reshape(N_CHIPS, -1)
    w0 = 0
    for n in SMALL_SHARDED:
        size = math.prod(_chip_shape(n))
        blocks = words[:, w0:w0 + size].reshape((N_CHIPS,) + _chip_shape(n))
        out[n] = jnp.concatenate([blocks[j] for j in range(N_CHIPS)], axis=SMALL_SHARD_AXIS[n])
        w0 += size
    return out


def _pack_big_grads(grads):
    per_name = []
    for n in BIG_NAMES:
        parts = jnp.split(grads[n], N_CHIPS, axis=BIG_SHARD_AXIS[n])
        per_name.append(jnp.stack([p.reshape(BIG_ROWS[n], PACK_COLS) for p in parts]))
    return jnp.concatenate(per_name, axis=1)


def _unpack_big_grads(total):
    out = {}
    r0 = 0
    for n in BIG_NAMES:
        out[n] = total[r0:r0 + BIG_ROWS[n]].reshape(_chip_shape(n))
        r0 += BIG_ROWS[n]
    return out


def _pack_small_grads(grads):
    return _pack_rows(jnp.concatenate([grads[n].astype(F32).reshape(-1) for n in SMALL_NAMES]), SMALL_ROWS)


def _unpack_small_grads(total, chip):
    flat = total.reshape(-1)
    out = {}
    w0 = 0
    for n in SMALL_NAMES:
        size = math.prod(FULL_SHAPES[n])
        full = flat[w0:w0 + size].reshape(FULL_SHAPES[n])
        if n in SMALL_SHARD_AXIS:
            axis = SMALL_SHARD_AXIS[n]
            width = FULL_SHAPES[n][axis] // N_CHIPS
            full = lax.dynamic_slice_in_dim(full, chip * width, width, axis)
        out[n] = full
        w0 += size
    return out


def kernel(x, positions, even_w_in, even_a_ln_g, even_a_ln_b, even_a_ws, even_a_bs, even_b_sinks, even_w_out, even_ln_g, even_ln_b, odd_w_in, odd_conv_w, odd_conv_b, odd_w_a, odd_b_a, odd_w_x, odd_b_x, odd_lam, odd_w_pool, odd_d_scale, odd_w_out, odd_ln_g, odd_ln_b, loss_target, m_even_w_in, m_even_a_ln_g, m_even_a_ln_b, m_even_a_ws, m_even_a_bs, m_even_b_sinks, m_even_w_out, m_even_ln_g, m_even_ln_b, m_odd_w_in, m_odd_conv_w, m_odd_conv_b, m_odd_w_a, m_odd_b_a, m_odd_w_x, m_odd_b_x, m_odd_lam, m_odd_w_pool, m_odd_d_scale, m_odd_w_out, m_odd_ln_g, m_odd_ln_b, v_even_w_in, v_even_a_ln_g, v_even_a_ln_b, v_even_a_ws, v_even_a_bs, v_even_b_sinks, v_even_w_out, v_even_ln_g, v_even_ln_b, v_odd_w_in, v_odd_conv_w, v_odd_conv_b, v_odd_w_a, v_odd_b_a, v_odd_w_x, v_odd_b_x, v_odd_lam, v_odd_w_pool, v_odd_d_scale, v_odd_w_out, v_odd_ln_g, v_odd_ln_b):
    local = dict(zip(WEIGHT_NAMES, (even_w_in, even_a_ln_g, even_a_ln_b, even_a_ws, even_a_bs, even_b_sinks, even_w_out, even_ln_g, even_ln_b, odd_w_in, odd_conv_w, odd_conv_b, odd_w_a, odd_b_a, odd_w_x, odd_b_x, odd_lam, odd_w_pool, odd_d_scale, odd_w_out, odd_ln_g, odd_ln_b)))
    mom = dict(zip(WEIGHT_NAMES, (m_even_w_in, m_even_a_ln_g, m_even_a_ln_b, m_even_a_ws, m_even_a_bs, m_even_b_sinks, m_even_w_out, m_even_ln_g, m_even_ln_b, m_odd_w_in, m_odd_conv_w, m_odd_conv_b, m_odd_w_a, m_odd_b_a, m_odd_w_x, m_odd_b_x, m_odd_lam, m_odd_w_pool, m_odd_d_scale, m_odd_w_out, m_odd_ln_g, m_odd_ln_b)))
    vel = dict(zip(WEIGHT_NAMES, (v_even_w_in, v_even_a_ln_g, v_even_a_ln_b, v_even_a_ws, v_even_a_bs, v_even_b_sinks, v_even_w_out, v_even_ln_g, v_even_ln_b, v_odd_w_in, v_odd_conv_w, v_odd_conv_b, v_odd_w_a, v_odd_b_a, v_odd_w_x, v_odd_b_x, v_odd_lam, v_odd_w_pool, v_odd_d_scale, v_odd_w_out, v_odd_ln_g, v_odd_ln_b)))
    mx, my, core = _where_am_i()
    chip = 2 * mx + my

    packed_big, packed_small = _pack_for_gather(local)
    weights = dict(local)
    weights.update(_unpack_gathered(_two_level_gather(packed_big, "gather_weights"),
                                    _two_level_gather(packed_small, "gather_small_weights")))

    loss, grad_x, grads = _local_step(x[0], positions[0], loss_target[0], weights)

    packed = _pack_big_grads(grads)
    from_sibling = _sibling_swap(packed, "big_presum_swap", other_half_of_axis1=True)
    chip_part = _add_halves(packed, from_sibling, core, BF16, "big_presum_add")
    arrived = _chip_scatter(chip_part, "big_scatter")
    my_half = _sum_slots(arrived, "big_sum")
    big_total = _sibling_swap(my_half, "big_join", join=True).reshape(BIG_TOTAL_ROWS, PACK_COLS)
    reduced = _unpack_big_grads(big_total)

    small = _pack_small_grads(grads)
    chip_sum = _add2(small, _sibling_swap(small, "small_swap"), "small_presum_add")
    small_total = _sum_slots(_two_level_gather(chip_sum, "small_gather"), "small_sum")
    reduced.update(_unpack_small_grads(small_total, chip))

    deltas, new_m, new_v = {}, {}, {}
    for n in WEIGHT_NAMES:
        deltas[n], new_m[n], new_v[n] = _adamw(local[n], reduced[n], mom[n], vel[n], "adamw_" + n)

    total_loss = lax.psum(loss[0, 0], ("x", "y", "c"))
    return (total_loss, grad_x[None], *[reduced[n] for n in WEIGHT_NAMES], *[deltas[n] for n in WEIGHT_NAMES],
            *[new_m[n] for n in WEIGHT_NAMES], *[new_v[n] for n in WEIGHT_NAMES])
```

```python
import functools
import math

import jax
import jax.numpy as jnp
from jax import lax
from jax.experimental import pallas as pl
from jax.experimental.pallas import tpu as pltpu

F32 = jnp.float32
BF16 = jnp.bfloat16
MXU_DTYPE = jnp.bfloat16

D_MODEL = 2048
DEPTH = 4
A_WIDTH = 1024
A_GROUPS = 8
CHUNK = 128
B_HEAD_DIM = 64
B_Q_HEADS = 16
B_KV_HEADS = 2
WINDOW = 128
ROT_DIM = 16
ROPE_THETA = 500000.0
C_WIDTH = 1024
C_HEADS = 8
CONV_WIDTH = 4
LRU_C = 8.0
D_WIDTH = 1024
D_GROUPS = 4
D_GROUP_DIM = 256
EVEN_IN = 5376
ODD_IN = 4096
DN_ALPHA = (2 * DEPTH) ** 0.25
LN_EPS = 1e-5
ATTN_SCALE = B_HEAD_DIM ** -0.5
NEG_BIG = -1e30

ADAM_LR = 0.001
ADAM_B1 = 0.9
ADAM_B2 = 0.999
ADAM_EPS = 1e-08
ADAM_WD = 0.01
ADAM_STEP = 10

LANES = 128
VMEM_LIMIT = 56 * 1024 * 1024
N_CHIPS = 4
MESH = pl.DeviceIdType.MESH

WEIGHT_NAMES = ['even_w_in', 'even_a_ln_g', 'even_a_ln_b', 'even_a_ws', 'even_a_bs', 'even_b_sinks', 'even_w_out',
                'even_ln_g', 'even_ln_b', 'odd_w_in', 'odd_conv_w', 'odd_conv_b', 'odd_w_a', 'odd_b_a', 'odd_w_x',
                'odd_b_x', 'odd_lam', 'odd_w_pool', 'odd_d_scale', 'odd_w_out', 'odd_ln_g', 'odd_ln_b']
BIG_SHARD_AXIS = {'even_w_in': 2, 'even_w_out': 1, 'odd_w_in': 2, 'odd_w_out': 1}
SMALL_SHARD_AXIS = {'odd_conv_w': 2, 'odd_conv_b': 1, 'odd_b_a': 1, 'odd_b_x': 1, 'odd_lam': 1, 'odd_w_pool': 2,
                    'odd_d_scale': 1, 'odd_ln_g': 1, 'odd_ln_b': 1}
FULL_SHAPES = {
    'even_w_in': (2, 2048, 5376), 'even_a_ln_g': (2, 1024), 'even_a_ln_b': (2, 1024), 'even_a_ws': (2, 8, 128, 128),
    'even_a_bs': (2, 8, 128), 'even_b_sinks': (2, 16), 'even_w_out': (2, 2048, 2048), 'even_ln_g': (2, 2048),
    'even_ln_b': (2, 2048), 'odd_w_in': (2, 2048, 4096), 'odd_conv_w': (2, 4, 1024), 'odd_conv_b': (2, 1024),
    'odd_w_a': (2, 8, 128, 128), 'odd_b_a': (2, 1024), 'odd_w_x': (2, 8, 128, 128), 'odd_b_x': (2, 1024),
    'odd_lam': (2, 1024), 'odd_w_pool': (2, 4, 256, 256), 'odd_d_scale': (2, 1024), 'odd_w_out': (2, 2048, 2048),
    'odd_ln_g': (2, 2048), 'odd_ln_b': (2, 2048)}
BIG_NAMES = ['even_w_in', 'even_w_out', 'odd_w_in', 'odd_w_out']
SMALL_SHARDED = ['odd_conv_w', 'odd_conv_b', 'odd_b_a', 'odd_b_x', 'odd_lam', 'odd_d_scale', 'odd_ln_g', 'odd_ln_b',
                 'odd_w_pool']
SMALL_NAMES = [n for n in WEIGHT_NAMES if n not in BIG_NAMES]
PACK_COLS = 1024


def _params(sem):
    return pltpu.CompilerParams(dimension_semantics=sem, vmem_limit_bytes=VMEM_LIMIT)


def _dot(a, b):
    return lax.dot_general(a, b, (((1,), (0,)), ((), ())), preferred_element_type=F32)


def _dot_nt(a, b):
    return lax.dot_general(a, b, (((1,), (1,)), ((), ())), preferred_element_type=F32)


def _dot_tn(a, b):
    return lax.dot_general(a, b, (((0,), (0,)), ((), ())), preferred_element_type=F32)


def _sigmoid(x):
    return 1.0 / (1.0 + jnp.exp(-x))


def _silu_and_grad(x):
    s = _sigmoid(x)
    return x * s, s * (1.0 + x * (1.0 - s))


def _matmul(a, b, *, mode, out_dtype, tm, tn, name, res=None, res_scale=1.0):
    if mode == 'nn':
        (m, k), (k2, n) = a.shape, b.shape
    elif mode == 'nt':
        (m, k), (n, k2) = a.shape, b.shape
    else:
        (k, m), (k2, n) = a.shape, b.shape
    assert k == k2, (a.shape, b.shape, mode)
    tm, tn = min(tm, m), min(tn, n)
    assert m % tm == 0 and n % tn == 0, (m, n, tm, tn)
    dot = {'nn': _dot, 'nt': _dot_nt, 'tn': _dot_tn}[mode]

    def body(*refs):
        if res is None:
            a_ref, b_ref, o_ref = refs
        else:
            a_ref, b_ref, r_ref, o_ref = refs
        acc = dot(a_ref[...].astype(MXU_DTYPE), b_ref[...].astype(MXU_DTYPE))
        if res is not None:
            acc = acc + res_scale * r_ref[...]
        o_ref[...] = acc.astype(out_dtype)

    a_spec = pl.BlockSpec((k, tm), lambda i, j: (0, i)) if mode == 'tn' else pl.BlockSpec((tm, k), lambda i, j: (i, 0))
    b_spec = pl.BlockSpec((tn, k), lambda i, j: (j, 0)) if mode == 'nt' else pl.BlockSpec((k, tn), lambda i, j: (0, j))
    in_specs = [a_spec, b_spec]
    args = [a, b]
    if res is not None:
        in_specs.append(pl.BlockSpec((tm, tn), lambda i, j: (i, j)))
        args.append(res)
    return pl.pallas_call(
        body, name=name, grid=(m // tm, n // tn), in_specs=in_specs,
        out_specs=pl.BlockSpec((tm, tn), lambda i, j: (i, j)),
        out_shape=jax.ShapeDtypeStruct((m, n), out_dtype),
        compiler_params=_params(("parallel", "parallel")),
    )(*args)


def _ln_fwd(x, y, g, b):
    s, d = x.shape
    tr = min(256, s)

    def body(x_ref, y_ref, g_ref, b_ref, z_ref, o_ref, ob_ref):
        z = DN_ALPHA * x_ref[...] + y_ref[...]
        mu = jnp.mean(z, axis=1, keepdims=True)
        zc = z - mu
        var = jnp.mean(zc * zc, axis=1, keepdims=True)
        o = zc * lax.rsqrt(var + LN_EPS) * g_ref[...] + b_ref[...]
        z_ref[...] = z
        o_ref[...] = o
        ob_ref[...] = o.astype(MXU_DTYPE)

    row = pl.BlockSpec((tr, d), lambda i: (i, 0))
    vec = pl.BlockSpec((1, d), lambda i: (0, 0))
    return pl.pallas_call(
        body, name="ln_fwd", grid=(s // tr,), in_specs=[row, row, vec, vec], out_specs=[row, row, row],
        out_shape=[jax.ShapeDtypeStruct((s, d), F32), jax.ShapeDtypeStruct((s, d), F32),
                   jax.ShapeDtypeStruct((s, d), MXU_DTYPE)],
        compiler_params=_params(("parallel",)),
    )(x, y, g.reshape(1, d), b.reshape(1, d))


def _ln_bwd(dout, z, g):
    s, d = z.shape
    tr = min(256, s)

    def body(do_ref, z_ref, g_ref, dz_ref, dzb_ref, dg_ref, db_ref):
        @pl.when(pl.program_id(0) == 0)
        def _():
            dg_ref[...] = jnp.zeros_like(dg_ref)
            db_ref[...] = jnp.zeros_like(db_ref)

        zz = z_ref[...]
        do = do_ref[...]
        mu = jnp.mean(zz, axis=1, keepdims=True)
        zc = zz - mu
        var = jnp.mean(zc * zc, axis=1, keepdims=True)
        rstd = lax.rsqrt(var + LN_EPS)
        zh = zc * rstd
        dg_ref[...] += jnp.sum(do * zh, axis=0, keepdims=True)
        db_ref[...] += jnp.sum(do, axis=0, keepdims=True)
        dzh = do * g_ref[...]
        dz = rstd * (dzh - jnp.mean(dzh, axis=1, keepdims=True) - zh * jnp.mean(dzh * zh, axis=1, keepdims=True))
        dz_ref[...] = dz
        dzb_ref[...] = dz.astype(MXU_DTYPE)

    row = pl.BlockSpec((tr, d), lambda i: (i, 0))
    vec = pl.BlockSpec((1, d), lambda i: (0, 0))
    return pl.pallas_call(
        body, name="ln_bwd", grid=(s // tr,), in_specs=[row, row, vec], out_specs=[row, row, vec, vec],
        out_shape=[jax.ShapeDtypeStruct((s, d), F32), jax.ShapeDtypeStruct((s, d), MXU_DTYPE),
                   jax.ShapeDtypeStruct((1, d), F32), jax.ShapeDtypeStruct((1, d), F32)],
        compiler_params=_params(("arbitrary",)),
    )(dout, z, g.reshape(1, d))


def _loss_and_grad(out, tgt):
    s, d = out.shape
    tr = min(256, s)

    def body(o_ref, t_ref, l_ref, do_ref):
        @pl.when(pl.program_id(0) == 0)
        def _():
            l_ref[...] = jnp.zeros_like(l_ref)

        e = o_ref[...] - t_ref[...]
        do_ref[...] = e * (1.0 / d)
        rows = jnp.sum(e * e, axis=1, keepdims=True)
        l_ref[...] += (0.5 / d) * jnp.sum(rows, axis=0, keepdims=True)

    row = pl.BlockSpec((tr, d), lambda i: (i, 0))
    return pl.pallas_call(
        body, name="loss", grid=(s // tr,), in_specs=[row, row],
        out_specs=[pl.BlockSpec((1, 1), lambda i: (0, 0)), row],
        out_shape=[jax.ShapeDtypeStruct((1, 1), F32), jax.ShapeDtypeStruct((s, d), F32)],
        compiler_params=_params(("arbitrary",)),
    )(out, tgt)


Q0, K0, V0, BG0 = 3072, 4096, 4224, 4352
T = CHUNK


def _lane_ids(width):
    return lax.broadcasted_iota(jnp.int32, (T, width), 1)


def _rope_swap(x):
    w = x.shape[1]
    l64 = _lane_ids(w) % B_HEAD_DIM
    half = ROT_DIM // 2
    return jnp.where(l64 < half, pltpu.roll(x, w - half, 1), jnp.where(l64 < ROT_DIM, pltpu.roll(x, half, 1), 0.0))


def _rope(x, c, s):
    return x * c + _rope_swap(x) * s


def _rope_bwd(d, c, s):
    return d * c + _rope_swap(d * s)


def _tril():
    return lax.broadcasted_iota(jnp.int32, (T, T), 0) >= lax.broadcasted_iota(jnp.int32, (T, T), 1)


def _band_mask(has_prev):
    qi = lax.broadcasted_iota(jnp.int32, (T, 2 * T), 0)
    kj = lax.broadcasted_iota(jnp.int32, (T, 2 * T), 1)
    return (kj > qi) & (kj <= qi + WINDOW) & ((kj >= WINDOW) | has_prev)


def _a_norm(v, g, b):
    mu = jnp.mean(v, axis=1, keepdims=True)
    vc = v - mu
    var = jnp.mean(vc * vc, axis=1, keepdims=True)
    rstd = lax.rsqrt(var + LN_EPS)
    vh = vc * rstd
    return vh, rstd, vh * g + b


def _kv_operands(hk, k_rot, kp_rot, vv, vp):
    lo = _lane_ids(LANES) < B_HEAD_DIM
    mine = lo if hk == 0 else jnp.logical_not(lo)
    kc = jnp.where(mine, k_rot, 0.0)
    kp = jnp.where(mine, kp_rot, 0.0)
    k2 = jnp.concatenate([kp + pltpu.roll(kp, B_HEAD_DIM, 1), kc + pltpu.roll(kc, B_HEAD_DIM, 1)], axis=0)
    vm = jnp.concatenate([jnp.where(mine, vp, 0.0), jnp.where(mine, vv, 0.0)], axis=0)
    vs = pltpu.roll(vm, B_HEAD_DIM, 1)
    v_lo, v_hi = (vm, vs) if hk == 0 else (vs, vm)
    return k2.astype(MXU_DTYPE), v_lo.astype(MXU_DTYPE), v_hi.astype(MXU_DTYPE), (vm + vs).astype(MXU_DTYPE)


def _softmax_with_sink(qm, k2, valid, sink):
    s = _dot_nt(qm, k2) * ATTN_SCALE
    s = jnp.where(valid, s, NEG_BIG)
    m = jnp.maximum(jnp.max(s, axis=1, keepdims=True), sink)
    ex = jnp.exp(s - m)
    es = jnp.exp(sink - m)
    inv = 1.0 / (jnp.sum(ex, axis=1, keepdims=True) + es)
    return ex * inv, es * inv


def _even_specs(s):
    nb = s // T
    return nb, {
        'h': lambda r: pl.BlockSpec((T, EVEN_IN), lambda i: (r(i), 0)),
        'kprev': lambda r: pl.BlockSpec((T, LANES), lambda i: (jnp.maximum(r(i) - 1, 0), K0 // LANES)),
        'vprev': lambda r: pl.BlockSpec((T, LANES), lambda i: (jnp.maximum(r(i) - 1, 0), V0 // LANES)),
        'tab': lambda r: pl.BlockSpec((T, A_WIDTH), lambda i: (r(i), 0)),
        'tabprev': lambda r: pl.BlockSpec((T, LANES), lambda i: (jnp.maximum(r(i) - 1, 0), 0)),
    }


def _full(shape):
    nd = len(shape)
    return pl.BlockSpec(shape, lambda i: (0,) * nd)


def _even_fwd(h, cq, sq, ln_g, ln_b, ws, bs_t, sinks):
    s = h.shape[0]
    nb, sp = _even_specs(s)

    def body(h_ref, kp_ref, vp_ref, cq_ref, sq_ref, ckp_ref, skp_ref, g_ref, b_ref, ws_ref, bst_ref, sink_ref, mix_ref):
        i = pl.program_id(0)
        tril = _tril()
        _, _, vn = _a_norm(h_ref[:, 1024:2048], g_ref[...], b_ref[...])
        vnb = vn.astype(MXU_DTYPE)
        for g in range(A_GROUPS):
            c0, c1 = g * LANES, (g + 1) * LANES
            wg = jnp.where(tril, ws_ref[g], 0.0).astype(MXU_DTYPE)
            mixed = _dot(wg, vnb[:, c0:c1]) + bst_ref[:, g:g + 1]
            sil, _ = _silu_and_grad(h_ref[:, 2048 + c0:2048 + c1])
            mix_ref[:, c0:c1] = (h_ref[:, c0:c1] * mixed * sil).astype(mix_ref.dtype)

        k_rot = _rope(h_ref[:, K0:K0 + LANES], cq_ref[:, 0:LANES], sq_ref[:, 0:LANES])
        kp_rot = _rope(kp_ref[...], ckp_ref[...], skp_ref[...])
        valid = _band_mask(i > 0)
        lo = _lane_ids(LANES) < B_HEAD_DIM
        for hk in range(B_KV_HEADS):
            k2, v_lo, v_hi, _ = _kv_operands(hk, k_rot, kp_rot, h_ref[:, V0:V0 + LANES], vp_ref[...])
            for pp in range(4):
                pair = hk * 4 + pp
                c0, c1 = pair * LANES, (pair + 1) * LANES
                qp = _rope(h_ref[:, Q0 + c0:Q0 + c1], cq_ref[:, c0:c1], sq_ref[:, c0:c1])
                acc = jnp.zeros((T, LANES), F32)
                for e in range(2):
                    head = 2 * pair + e
                    qm = jnp.where(lo if e == 0 else jnp.logical_not(lo), qp, 0.0).astype(MXU_DTYPE)
                    p, _ = _softmax_with_sink(qm, k2, valid, sink_ref[:, head:head + 1])
                    acc = acc + _dot(p.astype(MXU_DTYPE), v_lo if e == 0 else v_hi)
                sil, _ = _silu_and_grad(h_ref[:, BG0 + c0:BG0 + c1])
                mix_ref[:, A_WIDTH + c0:A_WIDTH + c1] = (acc * sil).astype(mix_ref.dtype)

    ident = lambda i: i
    return pl.pallas_call(
        body, name="even_fwd", grid=(nb,),
        in_specs=[sp['h'](ident), sp['kprev'](ident), sp['vprev'](ident), sp['tab'](ident), sp['tab'](ident),
                  sp['tabprev'](ident), sp['tabprev'](ident), _full((1, A_WIDTH)), _full((1, A_WIDTH)),
                  _full((A_GROUPS, T, T)), _full((T, A_GROUPS)), _full((1, B_Q_HEADS))],
        out_specs=pl.BlockSpec((T, 2 * A_WIDTH), lambda i: (i, 0)),
        out_shape=jax.ShapeDtypeStruct((s, 2 * A_WIDTH), MXU_DTYPE),
        compiler_params=_params(("parallel",)),
    )(h, h, h, cq, sq, cq, sq, ln_g.reshape(1, -1), ln_b.reshape(1, -1), ws, bs_t, sinks.reshape(1, -1))


def _even_bwd(h, dmix, cq, sq, ln_g, ln_b, ws, bs_t, sinks):
    s = h.shape[0]
    nb, sp = _even_specs(s)

    def body(h_ref, kp_ref, vp_ref, cq_ref, sq_ref, ckp_ref, skp_ref, dm_ref, g_ref, b_ref, ws_ref, bst_ref, sink_ref,
             dh_ref, dg_ref, db_ref, dws_ref, dbst_ref, dsink_ref, dvn_scr, ck_scr, cv_scr):
        i = pl.program_id(0)

        @pl.when(i == 0)
        def _():
            for ref in (dg_ref, db_ref, dws_ref, dbst_ref, dsink_ref, ck_scr, cv_scr):
                ref[...] = jnp.zeros_like(ref)

        tril = _tril()
        vh, rstd, vn = _a_norm(h_ref[:, 1024:2048], g_ref[...], b_ref[...])
        vnb = vn.astype(MXU_DTYPE)
        for g in range(A_GROUPS):
            c0, c1 = g * LANES, (g + 1) * LANES
            wg = jnp.where(tril, ws_ref[g], 0.0).astype(MXU_DTYPE)
            mixed = _dot(wg, vnb[:, c0:c1]) + bst_ref[:, g:g + 1]
            ag = h_ref[:, 2048 + c0:2048 + c1]
            u = h_ref[:, c0:c1]
            sil, dsil = _silu_and_grad(ag)
            da = dm_ref[:, c0:c1]
            dh_ref[:, c0:c1] = (da * mixed * sil).astype(dh_ref.dtype)
            dh_ref[:, 2048 + c0:2048 + c1] = (da * u * mixed * dsil).astype(dh_ref.dtype)
            dmx = da * u * sil
            dmb = dmx.astype(MXU_DTYPE)
            dvn_scr[:, c0:c1] = _dot_tn(wg, dmb)
            dws_ref[g] += jnp.where(tril, _dot_nt(dmb, vnb[:, c0:c1]), 0.0)
            dbst_ref[:, g:g + 1] += jnp.sum(dmx, axis=1, keepdims=True)
        dvn = dvn_scr[...]
        dg_ref[...] += jnp.sum(dvn * vh, axis=0, keepdims=True)
        db_ref[...] += jnp.sum(dvn, axis=0, keepdims=True)
        dvh = dvn * g_ref[...]
        dv = rstd * (dvh - jnp.mean(dvh, axis=1, keepdims=True) - vh * jnp.mean(dvh * vh, axis=1, keepdims=True))
        dh_ref[:, 1024:2048] = dv.astype(dh_ref.dtype)

        ck_cur, sk_cur = cq_ref[:, 0:LANES], sq_ref[:, 0:LANES]
        k_rot = _rope(h_ref[:, K0:K0 + LANES], ck_cur, sk_cur)
        kp_rot = _rope(kp_ref[...], ckp_ref[...], skp_ref[...])
        valid = _band_mask(i < nb - 1)
        lo = _lane_ids(LANES) < B_HEAD_DIM
        hi = jnp.logical_not(lo)
        dk_all = jnp.zeros((2 * T, LANES), F32)
        dv_all = jnp.zeros((2 * T, LANES), F32)
        for hk in range(B_KV_HEADS):
            k2, v_lo, v_hi, v2 = _kv_operands(hk, k_rot, kp_rot, h_ref[:, V0:V0 + LANES], vp_ref[...])
            acc_k = jnp.zeros((2 * T, LANES), F32)
            acc_v = jnp.zeros((2 * T, LANES), F32)
            for pp in range(4):
                pair = hk * 4 + pp
                c0, c1 = pair * LANES, (pair + 1) * LANES
                cqp, sqp = cq_ref[:, c0:c1], sq_ref[:, c0:c1]
                qp = _rope(h_ref[:, Q0 + c0:Q0 + c1], cqp, sqp)
                qms, ps, pss = [], [], []
                ob = jnp.zeros((T, LANES), F32)
                for e in range(2):
                    head = 2 * pair + e
                    qm = jnp.where(lo if e == 0 else hi, qp, 0.0).astype(MXU_DTYPE)
                    p, psink = _softmax_with_sink(qm, k2, valid, sink_ref[:, head:head + 1])
                    ob = ob + _dot(p.astype(MXU_DTYPE), v_lo if e == 0 else v_hi)
                    qms.append(qm)
                    ps.append(p)
                    pss.append(psink)
                sil, dsil = _silu_and_grad(h_ref[:, BG0 + c0:BG0 + c1])
                dbo = dm_ref[:, A_WIDTH + c0:A_WIDTH + c1]
                dh_ref[:, BG0 + c0:BG0 + c1] = (dbo * ob * dsil).astype(dh_ref.dtype)
                dob = dbo * sil
                dq_rot = jnp.zeros((T, LANES), F32)
                for e in range(2):
                    head = 2 * pair + e
                    sel = lo if e == 0 else hi
                    dobm = jnp.where(sel, dob, 0.0).astype(MXU_DTYPE)
                    dp = _dot_nt(dobm, v2)
                    rs = jnp.sum(ps[e] * dp, axis=1, keepdims=True)
                    ds = ps[e] * (dp - rs) * ATTN_SCALE
                    dsink_ref[:, head:head + 1] += -jnp.sum(pss[e] * rs, axis=0, keepdims=True)
                    dsb = ds.astype(MXU_DTYPE)
                    dq_rot = dq_rot + jnp.where(sel, _dot(dsb, k2), 0.0)
                    acc_k = acc_k + _dot_tn(dsb, qms[e])
                    acc_v = acc_v + _dot_tn(ps[e].astype(MXU_DTYPE), dobm)
                dh_ref[:, Q0 + c0:Q0 + c1] = _rope_bwd(dq_rot, cqp, sqp).astype(dh_ref.dtype)
            lo2 = lax.broadcasted_iota(jnp.int32, (2 * T, LANES), 1) < B_HEAD_DIM
            mine = lo2 if hk == 0 else jnp.logical_not(lo2)
            dk_all = dk_all + jnp.where(mine, acc_k + pltpu.roll(acc_k, B_HEAD_DIM, 1), 0.0)
            dv_all = dv_all + jnp.where(mine, acc_v + pltpu.roll(acc_v, B_HEAD_DIM, 1), 0.0)
        dk_rot = dk_all[T:2 * T] + ck_scr[...]
        dvv = dv_all[T:2 * T] + cv_scr[...]
        ck_scr[...] = dk_all[0:T]
        cv_scr[...] = dv_all[0:T]
        dh_ref[:, K0:K0 + LANES] = _rope_bwd(dk_rot, ck_cur, sk_cur).astype(dh_ref.dtype)
        dh_ref[:, V0:V0 + LANES] = dvv.astype(dh_ref.dtype)

    rev = lambda i: nb - 1 - i
    return pl.pallas_call(
        body, name="even_bwd", grid=(nb,),
        in_specs=[sp['h'](rev), sp['kprev'](rev), sp['vprev'](rev), sp['tab'](rev), sp['tab'](rev),
                  sp['tabprev'](rev), sp['tabprev'](rev), pl.BlockSpec((T, 2 * A_WIDTH), lambda i: (rev(i), 0)),
                  _full((1, A_WIDTH)), _full((1, A_WIDTH)), _full((A_GROUPS, T, T)), _full((T, A_GROUPS)),
                  _full((1, B_Q_HEADS))],
        out_specs=[pl.BlockSpec((T, EVEN_IN), lambda i: (rev(i), 0)), _full((1, A_WIDTH)), _full((1, A_WIDTH)),
                   _full((A_GROUPS, T, T)), _full((T, A_GROUPS)), _full((1, B_Q_HEADS))],
        out_shape=[jax.ShapeDtypeStruct((s, EVEN_IN), MXU_DTYPE), jax.ShapeDtypeStruct((1, A_WIDTH), F32),
                   jax.ShapeDtypeStruct((1, A_WIDTH), F32), jax.ShapeDtypeStruct((A_GROUPS, T, T), F32),
                   jax.ShapeDtypeStruct((T, A_GROUPS), F32), jax.ShapeDtypeStruct((1, B_Q_HEADS), F32)],
        scratch_shapes=[pltpu.VMEM((T, A_WIDTH), F32), pltpu.VMEM((T, LANES), F32), pltpu.VMEM((T, LANES), F32)],
        compiler_params=_params(("arbitrary",)),
    )(h, h, h, cq, sq, cq, sq, dmix, ln_g.reshape(1, -1), ln_b.reshape(1, -1), ws, bs_t, sinks.reshape(1, -1))


def _shift_down(x, d, fill=0.0):
    rows = lax.broadcasted_iota(jnp.int32, x.shape, 0)
    return jnp.where(rows >= d, pltpu.roll(x, d, 0), fill)


def _shift_up(x, d, fill=0.0):
    n = x.shape[0]
    rows = lax.broadcasted_iota(jnp.int32, x.shape, 0)
    return jnp.where(rows < n - d, pltpu.roll(x, n - d, 0), fill)


def _scan(a, b, shift):
    n = a.shape[0]
    d = 1
    while d < n:
        b = a * shift(b, d) + b
        if 2 * d < n:
            a = a * shift(a, d, 1.0)
        d *= 2
    return b


def _neg_expm1(y):
    u = jnp.exp(y)
    um1 = u - 1.0
    safe = jnp.where(um1 == 0.0, 1.0, jnp.log(u))
    near = jnp.where(um1 == 0.0, y, um1 * y / safe)
    return -jnp.where(y > -0.5, near, um1)


def _softplus(x):
    return jnp.maximum(x, 0.0) + jnp.log1p(jnp.exp(-jnp.abs(x)))


def _lru_forward(xc, cw, cb, wa, ba, wx, bx, lam):
    shifted = [_shift_down(xc, CONV_WIDTH - 1 - j) if j < CONV_WIDTH - 1 else xc for j in range(CONV_WIDTH)]
    xconv = cb
    for j in range(CONV_WIDTH):
        xconv = xconv + cw[j:j + 1, :] * shifted[j]
    xb = xconv.astype(MXU_DTYPE)
    r = _sigmoid(_dot(xb, wa.astype(MXU_DTYPE)) + ba)
    ig = _sigmoid(_dot(xb, wx.astype(MXU_DTYPE)) + bx)
    sp = _softplus(-lam)
    log_a = -LRU_C * r * sp
    a = jnp.exp(log_a)
    mult = jnp.sqrt(_neg_expm1(2.0 * log_a))
    hs = _scan(a, mult * ig * xconv, _shift_down)
    return dict(shifted=shifted, xconv=xconv, xb=xb, r=r, ig=ig, sp=sp, a=a, mult=mult, hs=hs)


def _col(width, off):
    return lambda s: pl.BlockSpec((s, width), lambda i: (0, off + i))


def _lru_specs(s):
    w = LANES
    return [_col(w, 0)(s), _col(w, C_WIDTH // w)(s), pl.BlockSpec((CONV_WIDTH, w), lambda i: (0, i)),
            pl.BlockSpec((1, w), lambda i: (0, i)), pl.BlockSpec((None, w, w), lambda i: (i, 0, 0)),
            pl.BlockSpec((1, w), lambda i: (0, i)), pl.BlockSpec((None, w, w), lambda i: (i, 0, 0)),
            pl.BlockSpec((1, w), lambda i: (0, i)), pl.BlockSpec((1, w), lambda i: (0, i))]


def _lru_fwd(h, cw, cb, wa, ba, wx, bx, lam):
    s = h.shape[0]

    def body(xc_ref, cg_ref, cw_ref, cb_ref, wa_ref, ba_ref, wx_ref, bx_ref, lam_ref, o_ref):
        f = _lru_forward(xc_ref[...], cw_ref[...], cb_ref[...], wa_ref[...], ba_ref[...], wx_ref[...], bx_ref[...],
                         lam_ref[...])
        sil, _ = _silu_and_grad(cg_ref[...])
        o_ref[...] = (f['hs'] * sil).astype(o_ref.dtype)

    return pl.pallas_call(
        body, name="lru_fwd", grid=(C_HEADS,), in_specs=_lru_specs(s), out_specs=_col(LANES, 0)(s),
        out_shape=jax.ShapeDtypeStruct((s, C_WIDTH), MXU_DTYPE), compiler_params=_params(("parallel",)),
    )(h, h, cw, cb.reshape(1, -1), wa, ba.reshape(1, -1), wx, bx.reshape(1, -1), lam.reshape(1, -1))


def _lru_bwd(h, dmix, cw, cb, wa, ba, wx, bx, lam):
    s = h.shape[0]

    def body(xc_ref, cg_ref, cw_ref, cb_ref, wa_ref, ba_ref, wx_ref, bx_ref, lam_ref, dco_ref,
             dxc_ref, dcg_ref, dcw_ref, dcb_ref, dwa_ref, dba_ref, dwx_ref, dbx_ref, dlam_ref):
        cw, lam = cw_ref[...], lam_ref[...]
        wab, wxb = wa_ref[...].astype(MXU_DTYPE), wx_ref[...].astype(MXU_DTYPE)
        f = _lru_forward(xc_ref[...], cw, cb_ref[...], wa_ref[...], ba_ref[...], wx_ref[...], bx_ref[...], lam)
        sil, dsil = _silu_and_grad(cg_ref[...])
        dco = dco_ref[...]
        hs, a, mult, ig, r, xconv = f['hs'], f['a'], f['mult'], f['ig'], f['r'], f['xconv']
        dcg_ref[...] = (dco * hs * dsil).astype(dcg_ref.dtype)
        lamb = _scan(_shift_up(a, 1), dco * sil, _shift_up)
        da = lamb * _shift_down(hs, 1)
        d_ig = lamb * mult * xconv
        d_mult = lamb * ig * xconv
        dxconv = lamb * mult * ig
        d_log_a = da * a - d_mult * (a * a) / mult
        dlam_ref[...] = jnp.sum(d_log_a * r, axis=0, keepdims=True) * LRU_C * _sigmoid(-lam)
        dpa = d_log_a * (-LRU_C * f['sp']) * r * (1.0 - r)
        dpx = d_ig * ig * (1.0 - ig)
        dba_ref[...] = jnp.sum(dpa, axis=0, keepdims=True)
        dbx_ref[...] = jnp.sum(dpx, axis=0, keepdims=True)
        dpab, dpxb = dpa.astype(MXU_DTYPE), dpx.astype(MXU_DTYPE)
        dwa_ref[...] = _dot_tn(f['xb'], dpab)
        dwx_ref[...] = _dot_tn(f['xb'], dpxb)
        dxconv = dxconv + _dot_nt(dpab, wab) + _dot_nt(dpxb, wxb)
        dcb_ref[...] = jnp.sum(dxconv, axis=0, keepdims=True)
        dxc = jnp.zeros_like(dxconv)
        for j in range(CONV_WIDTH):
            dcw_ref[j:j + 1, :] = jnp.sum(dxconv * f['shifted'][j], axis=0, keepdims=True)
            back = _shift_up(dxconv, CONV_WIDTH - 1 - j) if j < CONV_WIDTH - 1 else dxconv
            dxc = dxc + cw[j:j + 1, :] * back
        dxc_ref[...] = dxc.astype(dxc_ref.dtype)

    w = LANES
    vec = pl.BlockSpec((1, w), lambda i: (0, i))
    mat = pl.BlockSpec((None, w, w), lambda i: (i, 0, 0))
    vshape = jax.ShapeDtypeStruct((1, C_WIDTH), F32)
    mshape = jax.ShapeDtypeStruct((C_HEADS, w, w), F32)
    return pl.pallas_call(
        body, name="lru_bwd", grid=(C_HEADS,), in_specs=_lru_specs(s) + [_col(w, 0)(s)],
        out_specs=[_col(w, 0)(s), _col(w, 0)(s), pl.BlockSpec((CONV_WIDTH, w), lambda i: (0, i)), vec, mat, vec, mat,
                   vec, vec],
        out_shape=[jax.ShapeDtypeStruct((s, C_WIDTH), MXU_DTYPE), jax.ShapeDtypeStruct((s, C_WIDTH), MXU_DTYPE),
                   jax.ShapeDtypeStruct((CONV_WIDTH, C_WIDTH), F32), vshape, mshape, vshape, mshape, vshape, vshape],
        compiler_params=_params(("parallel",)),
    )(h, h, cw, cb.reshape(1, -1), wa, ba.reshape(1, -1), wx, bx.reshape(1, -1), lam.reshape(1, -1), dmix)


def _pool_window(g):
    return jnp.left_shift(2, g).astype(F32)


def _select_window(g, sums):
    out = sums[-1]
    for k in range(len(sums) - 2, -1, -1):
        out = jnp.where(g == k, sums[k], out)
    return out


def _window_sums(x, shift):
    sums, cur = [], x
    for k in range(D_GROUPS):
        cur = cur + shift(cur, 2 ** k)
        sums.append(cur)
    return sums


def _pool_specs(s):
    w = D_GROUP_DIM
    return [_col(w, 2 * C_WIDTH // w)(s), _col(w, 3 * C_WIDTH // w)(s),
            pl.BlockSpec((None, w, w), lambda i: (i, 0, 0)), pl.BlockSpec((1, w), lambda i: (0, i))]


def _pool_forward(xd, g):
    rows = lax.broadcasted_iota(jnp.int32, xd.shape, 0).astype(F32) + 1.0
    cnt = jnp.minimum(rows, _pool_window(g))
    pooled = _select_window(g, _window_sums(xd, _shift_down)) / cnt - xd
    return pooled, cnt


def _pool_fwd(h, w_pool, d_scale):
    s = h.shape[0]

    def body(xd_ref, dg_ref, wp_ref, sc_ref, o_ref):
        pooled, _ = _pool_forward(xd_ref[...], pl.program_id(0))
        mixed = _dot(pooled.astype(MXU_DTYPE), wp_ref[...].astype(MXU_DTYPE))
        sil, _ = _silu_and_grad(dg_ref[...])
        o_ref[...] = (mixed * sc_ref[...] * sil).astype(o_ref.dtype)

    return pl.pallas_call(
        body, name="pool_fwd", grid=(D_GROUPS,), in_specs=_pool_specs(s), out_specs=_col(D_GROUP_DIM, 0)(s),
        out_shape=jax.ShapeDtypeStruct((s, D_WIDTH), MXU_DTYPE), compiler_params=_params(("parallel",)),
    )(h, h, w_pool, d_scale.reshape(1, -1))


def _pool_bwd(h, dmix, w_pool, d_scale):
    s = h.shape[0]
    w = D_GROUP_DIM

    def body(xd_ref, dg_ref, wp_ref, sc_ref, ddo_ref, dxd_ref, ddg_ref, dwp_ref, dsc_ref):
        g = pl.program_id(0)
        pooled, cnt = _pool_forward(xd_ref[...], g)
        pb = pooled.astype(MXU_DTYPE)
        wpb = wp_ref[...].astype(MXU_DTYPE)
        mixed = _dot(pb, wpb)
        sil, dsil = _silu_and_grad(dg_ref[...])
        ddo = ddo_ref[...]
        sc = sc_ref[...]
        ddg_ref[...] = (ddo * mixed * sc * dsil).astype(ddg_ref.dtype)
        dms = ddo * sil
        dsc_ref[...] = jnp.sum(dms * mixed, axis=0, keepdims=True)
        dmb = (dms * sc).astype(MXU_DTYPE)
        dwp_ref[...] = _dot_tn(pb, dmb)
        dpooled = _dot_nt(dmb, wpb)
        dxd = _select_window(g, _window_sums(dpooled / cnt, _shift_up)) - dpooled
        dxd_ref[...] = dxd.astype(dxd_ref.dtype)

    return pl.pallas_call(
        body, name="pool_bwd", grid=(D_GROUPS,), in_specs=_pool_specs(s) + [_col(w, C_WIDTH // w)(s)],
        out_specs=[_col(w, 0)(s), _col(w, 0)(s), pl.BlockSpec((None, w, w), lambda i: (i, 0, 0)),
                   pl.BlockSpec((1, w), lambda i: (0, i))],
        out_shape=[jax.ShapeDtypeStruct((s, D_WIDTH), MXU_DTYPE), jax.ShapeDtypeStruct((s, D_WIDTH), MXU_DTYPE),
                   jax.ShapeDtypeStruct((D_GROUPS, w, w), F32), jax.ShapeDtypeStruct((1, D_WIDTH), F32)],
        compiler_params=_params(("parallel",)),
    )(h, h, w_pool, d_scale.reshape(1, -1), dmix)


def _rope_tables(positions):
    s = positions.shape[0]
    inv_freq = ROPE_THETA ** (-jnp.arange(0, ROT_DIM, 2, dtype=F32) / ROT_DIM)
    ang = positions.astype(F32)[:, None] * inv_freq
    c, sn = jnp.cos(ang), jnp.sin(ang)
    rest = B_HEAD_DIM - ROT_DIM
    c64 = jnp.concatenate([c, c, jnp.ones((s, rest), F32)], axis=1)
    s64 = jnp.concatenate([-sn, sn, jnp.zeros((s, rest), F32)], axis=1)
    return jnp.tile(c64, (1, B_Q_HEADS)), jnp.tile(s64, (1, B_Q_HEADS))


def _local_step(x, positions, target, w):
    cq, sq = _rope_tables(positions)
    saved = []
    xin, xin_b = x, x
    for layer in range(DEPTH):
        j = layer // 2
        if layer % 2 == 0:
            h = _matmul(xin_b, w['even_w_in'][j], mode='nn', out_dtype=F32, tm=1024, tn=768, name="mm_even_in")
            small = (w['even_a_ln_g'][j], w['even_a_ln_b'][j], w['even_a_ws'][j], w['even_a_bs'][j].T,
                     w['even_b_sinks'][j])
            mix = _even_fwd(h, cq, sq, *small)
            y = _matmul(mix, w['even_w_out'][j], mode='nn', out_dtype=F32, tm=1024, tn=1024, name="mm_out")
            z, out, out_b = _ln_fwd(xin, y, w['even_ln_g'][j], w['even_ln_b'][j])
        else:
            h = _matmul(xin_b, w['odd_w_in'][j], mode='nn', out_dtype=F32, tm=1024, tn=1024, name="mm_odd_in")
            c_out = _lru_fwd(h, w['odd_conv_w'][j], w['odd_conv_b'][j], w['odd_w_a'][j], w['odd_b_a'][j],
                             w['odd_w_x'][j], w['odd_b_x'][j], w['odd_lam'][j])
            d_out = _pool_fwd(h, w['odd_w_pool'][j], w['odd_d_scale'][j])
            mix = jnp.concatenate([c_out, d_out], axis=1)
            y = _matmul(mix, w['odd_w_out'][j], mode='nn', out_dtype=F32, tm=1024, tn=1024, name="mm_out")
            z, out, out_b = _ln_fwd(xin, y, w['odd_ln_g'][j], w['odd_ln_b'][j])
        saved.append((xin_b, h, mix, z))
        xin, xin_b = out, out_b

    loss, dout = _loss_and_grad(xin, target)
    per_layer = {n: [None, None] for n in WEIGHT_NAMES}
    for layer in reversed(range(DEPTH)):
        j = layer // 2
        kind = 'even' if layer % 2 == 0 else 'odd'
        xin_b, h, mix, z = saved[layer]
        dz, dz_b, dg, db = _ln_bwd(dout, z, w[kind + '_ln_g'][j])
        per_layer[kind + '_ln_g'][j], per_layer[kind + '_ln_b'][j] = dg[0], db[0]
        dmix = _matmul(dz_b, w[kind + '_w_out'][j], mode='nt', out_dtype=F32, tm=1024, tn=1024, name="mm_dmix")
        per_layer[kind + '_w_out'][j] = _matmul(mix, dz_b, mode='tn', out_dtype=MXU_DTYPE, tm=1024, tn=1024,
                                                name="mm_dw_out")
        if kind == 'even':
            dh, dlg, dlb, dws, dbst, dsink = _even_bwd(
                h, dmix, cq, sq, w['even_a_ln_g'][j], w['even_a_ln_b'][j], w['even_a_ws'][j], w['even_a_bs'][j].T,
                w['even_b_sinks'][j])
            for n, val in (('even_a_ln_g', dlg[0]), ('even_a_ln_b', dlb[0]), ('even_a_ws', dws),
                           ('even_a_bs', dbst.T), ('even_b_sinks', dsink[0])):
                per_layer[n][j] = val
            tn_in = 768
        else:
            dxc, dcg, dcw, dcb, dwa, dba, dwx, dbx, dlam = _lru_bwd(
                h, dmix, w['odd_conv_w'][j], w['odd_conv_b'][j], w['odd_w_a'][j], w['odd_b_a'][j], w['odd_w_x'][j],
                w['odd_b_x'][j], w['odd_lam'][j])
            dxd, ddg, dwp, dsc = _pool_bwd(h, dmix, w['odd_w_pool'][j], w['odd_d_scale'][j])
            dh = jnp.concatenate([dxc, dcg, dxd, ddg], axis=1)
            for n, val in (('odd_conv_w', dcw), ('odd_conv_b', dcb[0]), ('odd_w_a', dwa), ('odd_b_a', dba[0]),
                           ('odd_w_x', dwx), ('odd_b_x', dbx[0]), ('odd_lam', dlam[0]), ('odd_w_pool', dwp),
                           ('odd_d_scale', dsc[0])):
                per_layer[n][j] = val
            tn_in = 1024
        dout = _matmul(dh, w[kind + '_w_in'][j], mode='nt', out_dtype=F32, tm=512, tn=512, name="mm_dx_" + kind,
                       res=dz, res_scale=DN_ALPHA)
        per_layer[kind + '_w_in'][j] = _matmul(xin_b, dh, mode='tn', out_dtype=MXU_DTYPE, tm=1024, tn=tn_in,
                                               name="mm_dw_in_" + kind)
    grads = {n: jnp.stack(v) for n, v in per_layer.items()}
    return loss, dout, grads


ANY = pl.BlockSpec(memory_space=pl.ANY)


def _where_am_i():
    return lax.axis_index("x"), lax.axis_index("y"), lax.axis_index("c")


def _other_chips(x, y):
    return [(1 - x, y), (x, 1 - y), (1 - x, 1 - y)]


def _two_level_gather(buf, chip, name):
    r, cols = buf.shape
    half = r // 2
    assert r == 2 * half

    def body(x_ref, out_ref, send_sems, recv_sems):
        x, y, c = _where_am_i()
        sibling = (x, y, 1 - c)
        chips = _other_chips(x, y)

        def rows(px, py, pc):
            return out_ref.at[2 * px + py, pl.ds(pl.multiple_of(pc * half, 16), half), :]

        def copy(k, block, to, src=None):
            return pltpu.make_async_remote_copy(
                src_ref=rows(*block) if src is None else src, dst_ref=rows(*block), send_sem=send_sems.at[k],
                recv_sem=recv_sems.at[k], device_id=to, device_id_type=MESH)

        my_half = x_ref.at[pl.ds(pl.multiple_of(c * half, 16), half), :]
        first = [copy(j, (x, y, c), (*chip, c), src=my_half) for j, chip in enumerate(chips)]
        for cp in first:
            cp.start()
        passed = [copy(3 + j, (*chip, c), sibling) for j, chip in enumerate(chips)]
        for j, chip in enumerate(chips):
            copy(j, (*chip, c), (x, y, c)).wait_recv()
            passed[j].start()
        for j, chip in enumerate(chips):
            copy(3 + j, (*chip, 1 - c), (x, y, c)).wait_recv()
        for cp in first + passed:
            cp.wait_send()

    others = pl.pallas_call(
        body, name=name, in_specs=[ANY], out_specs=ANY,
        out_shape=jax.ShapeDtypeStruct((N_CHIPS, r, cols), buf.dtype),
        scratch_shapes=[pltpu.SemaphoreType.DMA((6,)), pltpu.SemaphoreType.DMA((6,))],
    )(buf)
    return lax.dynamic_update_slice(others, buf[None], (chip, 0, 0))


def _sibling_swap(buf, name, *, other_half_of_axis1=False):
    if other_half_of_axis1:
        n, r2, cols = buf.shape
        out_shape = (n, r2 // 2, cols)
    else:
        out_shape = buf.shape

    def body(x_ref, out_ref, send_sem, recv_sem):
        x, y, c = _where_am_i()
        if other_half_of_axis1:
            h = out_shape[1]
            src = x_ref.at[:, pl.ds(pl.multiple_of((1 - c) * h, 16), h), :]
        else:
            src = x_ref
        cp = pltpu.make_async_remote_copy(src_ref=src, dst_ref=out_ref, send_sem=send_sem, recv_sem=recv_sem,
                                          device_id=(x, y, 1 - c), device_id_type=MESH)
        cp.start()
        cp.wait()

    return pl.pallas_call(
        body, name=name, in_specs=[ANY], out_specs=ANY, out_shape=jax.ShapeDtypeStruct(out_shape, buf.dtype),
        scratch_shapes=[pltpu.SemaphoreType.DMA, pltpu.SemaphoreType.DMA],
    )(buf)


def _chip_scatter(part, chip, name):
    def body(p_ref, out_ref, send_sems, recv_sems):
        x, y, c = _where_am_i()
        me = 2 * x + y
        chips = _other_chips(x, y)
        sends = [pltpu.make_async_remote_copy(
            src_ref=p_ref.at[2 * px + py], dst_ref=out_ref.at[me], send_sem=send_sems.at[k], recv_sem=recv_sems.at[k],
            device_id=(px, py, c), device_id_type=MESH) for k, (px, py) in enumerate(chips)]
        for cp in sends:
            cp.start()
        for k, (px, py) in enumerate(chips):
            pltpu.make_async_remote_copy(
                src_ref=p_ref.at[me], dst_ref=out_ref.at[2 * px + py], send_sem=send_sems.at[k],
                recv_sem=recv_sems.at[k], device_id=(px, py, c), device_id_type=MESH).wait_recv()
        for cp in sends:
            cp.wait_send()

    others = pl.pallas_call(
        body, name=name, in_specs=[ANY], out_specs=ANY, out_shape=jax.ShapeDtypeStruct(part.shape, part.dtype),
        scratch_shapes=[pltpu.SemaphoreType.DMA((3,)), pltpu.SemaphoreType.DMA((3,))],
    )(part)
    own = lax.dynamic_slice_in_dim(part, chip, 1, axis=0)
    return lax.dynamic_update_slice(others, own, (chip, 0, 0))


def _col_tile(rows, cols, itemsize, n_bufs, budget=20 * 1024 * 1024):
    best = LANES
    for t in range(LANES, cols + 1, LANES):
        if cols % t == 0 and 2 * n_bufs * rows * t * itemsize <= budget:
            best = t
    return best


def _add_halves(full, recv, core, out_dtype, name):
    n, h, cols = recv.shape
    tc = _col_tile(h, cols, 4, 3)

    def body(core_ref, a_ref, b_ref, o_ref):
        o_ref[...] = (a_ref[...].astype(F32) + b_ref[...].astype(F32)).astype(out_dtype)

    blk = (None, h, tc)
    return pl.pallas_call(
        body, name=name,
        grid_spec=pltpu.PrefetchScalarGridSpec(
            num_scalar_prefetch=1, grid=(n, cols // tc),
            in_specs=[pl.BlockSpec(blk, lambda j, i, core_ref: (j, core_ref[0], i)),
                      pl.BlockSpec(blk, lambda j, i, core_ref: (j, 0, i))],
            out_specs=pl.BlockSpec(blk, lambda j, i, core_ref: (j, 0, i))),
        out_shape=jax.ShapeDtypeStruct((n, h, cols), out_dtype),
        compiler_params=_params(("parallel", "parallel")),
    )(core.reshape(1), full, recv)


def _add2(a, b, name):
    r, cols = a.shape
    tc = _col_tile(r, cols, 4, 3)

    def body(a_ref, b_ref, o_ref):
        o_ref[...] = a_ref[...] + b_ref[...]

    blk = pl.BlockSpec((r, tc), lambda i: (0, i))
    return pl.pallas_call(body, name=name, grid=(cols // tc,), in_specs=[blk, blk], out_specs=blk,
                          out_shape=jax.ShapeDtypeStruct(a.shape, a.dtype), compiler_params=_params(("parallel",)))(a, b)


def _sum_slots(parts, name):
    n, h, cols = parts.shape
    tc = _col_tile(h, cols, 4, n + 1)

    def body(p_ref, o_ref):
        acc = p_ref[0].astype(F32)
        for k in range(1, n):
            acc = acc + p_ref[k].astype(F32)
        o_ref[...] = acc

    return pl.pallas_call(
        body, name=name, grid=(cols // tc,), in_specs=[pl.BlockSpec((n, h, tc), lambda i: (0, 0, i))],
        out_specs=pl.BlockSpec((h, tc), lambda i: (0, i)), out_shape=jax.ShapeDtypeStruct((h, cols), F32),
        compiler_params=_params(("parallel",)),
    )(parts)


def _adamw(w, g, m, v, name):
    shape = w.shape
    cols = shape[-1]
    rows = math.prod(shape[:-1])
    tr = rows
    if rows % 8 == 0:
        tr = 8
        for t in range(8, rows + 1, 8):
            if rows % t == 0 and t * cols * 4 <= 1536 * 1024:
                tr = t
    c1 = 1.0 - ADAM_B1 ** ADAM_STEP
    c2 = 1.0 - ADAM_B2 ** ADAM_STEP

    def body(w_ref, g_ref, m_ref, v_ref, d_ref, nm_ref, nv_ref):
        gg = g_ref[...]
        nm = ADAM_B1 * m_ref[...] + (1.0 - ADAM_B1) * gg
        nv = ADAM_B2 * v_ref[...] + (1.0 - ADAM_B2) * (gg * gg)
        d_ref[...] = -ADAM_LR * ((nm / c1) / (jnp.sqrt(nv / c2) + ADAM_EPS) + ADAM_WD * w_ref[...])
        nm_ref[...] = nm
        nv_ref[...] = nv

    blk = pl.BlockSpec((tr, cols), lambda i: (i, 0))
    flat = jax.ShapeDtypeStruct((rows, cols), F32)
    outs = pl.pallas_call(
        body, name=name, grid=(rows // tr,), in_specs=[blk] * 4, out_specs=[blk] * 3, out_shape=[flat] * 3,
        compiler_params=_params(("parallel",)),
    )(*[a.reshape(rows, cols) for a in (w, g, m, v)])
    return [o.reshape(shape) for o in outs]


def _chip_shape(name):
    shape = list(FULL_SHAPES[name])
    axis = BIG_SHARD_AXIS.get(name, SMALL_SHARD_AXIS.get(name))
    if axis is not None:
        shape[axis] //= N_CHIPS
    return tuple(shape)


def _round_up(n, k):
    return -(-n // k) * k


BIG_ROWS = {n: math.prod(_chip_shape(n)) // PACK_COLS for n in BIG_NAMES}
BIG_TOTAL_ROWS = sum(BIG_ROWS.values())
SHARDED_SMALL_WORDS = sum(math.prod(_chip_shape(n)) for n in SMALL_SHARDED)
SHARDED_SMALL_ROWS = _round_up(-(-SHARDED_SMALL_WORDS // PACK_COLS), 32)
SMALL_WORDS = sum(math.prod(FULL_SHAPES[n]) for n in SMALL_NAMES)
SMALL_ROWS = _round_up(-(-SMALL_WORDS // PACK_COLS), 32)


def _pack_rows(flat, rows):
    return jnp.pad(flat, (0, rows * PACK_COLS - flat.shape[0])).reshape(rows, PACK_COLS)


def _pack_for_gather(shards):
    big = jnp.concatenate([shards[n].astype(BF16).reshape(-1) for n in BIG_NAMES])
    small = jnp.concatenate([shards[n].reshape(-1) for n in SMALL_SHARDED])
    return big.reshape(BIG_TOTAL_ROWS, PACK_COLS), _pack_rows(small, SHARDED_SMALL_ROWS)


def _unpack_gathered(gathered, gathered_small):
    out = {}
    r0 = 0
    for n in BIG_NAMES:
        blocks = gathered[:, r0:r0 + BIG_ROWS[n]].reshape((N_CHIPS,) + _chip_shape(n))
        out[n] = jnp.concatenate([blocks[j] for j in range(N_CHIPS)], axis=BIG_SHARD_AXIS[n])
        r0 += BIG_ROWS[n]
    words = gathered_small.reshape(N_CHIPS, -1)
    w0 = 0
    for n in SMALL_SHARDED:
        size = math.prod(_chip_shape(n))
        blocks = words[:, w0:w0 + size].reshape((N_CHIPS,) + _chip_shape(n))
        out[n] = jnp.concatenate([blocks[j] for j in range(N_CHIPS)], axis=SMALL_SHARD_AXIS[n])
        w0 += size
    return out


def _pack_big_grads(grads):
    per_name = []
    for n in BIG_NAMES:
        parts = jnp.split(grads[n], N_CHIPS, axis=BIG_SHARD_AXIS[n])
        per_name.append(jnp.stack([p.reshape(BIG_ROWS[n], PACK_COLS) for p in parts]))
    return jnp.concatenate(per_name, axis=1)


def _unpack_big_grads(total):
    out = {}
    r0 = 0
    for n in BIG_NAMES:
        out[n] = total[r0:r0 + BIG_ROWS[n]].reshape(_chip_shape(n))
        r0 += BIG_ROWS[n]
    return out


def _pack_small_grads(grads):
    return _pack_rows(jnp.concatenate([grads[n].astype(F32).reshape(-1) for n in SMALL_NAMES]), SMALL_ROWS)


def _unpack_small_grads(total, chip):
    flat = total.reshape(-1)
    out = {}
    w0 = 0
    for n in SMALL_NAMES:
        size = math.prod(FULL_SHAPES[n])
        full = flat[w0:w0 + size].reshape(FULL_SHAPES[n])
        if n in SMALL_SHARD_AXIS:
            axis = SMALL_SHARD_AXIS[n]
            width = FULL_SHAPES[n][axis] // N_CHIPS
            full = lax.dynamic_slice_in_dim(full, chip * width, width, axis)
        out[n] = full
        w0 += size
    return out


def kernel(x, positions, even_w_in, even_a_ln_g, even_a_ln_b, even_a_ws, even_a_bs, even_b_sinks, even_w_out, even_ln_g, even_ln_b, odd_w_in, odd_conv_w, odd_conv_b, odd_w_a, odd_b_a, odd_w_x, odd_b_x, odd_lam, odd_w_pool, odd_d_scale, odd_w_out, odd_ln_g, odd_ln_b, loss_target, m_even_w_in, m_even_a_ln_g, m_even_a_ln_b, m_even_a_ws, m_even_a_bs, m_even_b_sinks, m_even_w_out, m_even_ln_g, m_even_ln_b, m_odd_w_in, m_odd_conv_w, m_odd_conv_b, m_odd_w_a, m_odd_b_a, m_odd_w_x, m_odd_b_x, m_odd_lam, m_odd_w_pool, m_odd_d_scale, m_odd_w_out, m_odd_ln_g, m_odd_ln_b, v_even_w_in, v_even_a_ln_g, v_even_a_ln_b, v_even_a_ws, v_even_a_bs, v_even_b_sinks, v_even_w_out, v_even_ln_g, v_even_ln_b, v_odd_w_in, v_odd_conv_w, v_odd_conv_b, v_odd_w_a, v_odd_b_a, v_odd_w_x, v_odd_b_x, v_odd_lam, v_odd_w_pool, v_odd_d_scale, v_odd_w_out, v_odd_ln_g, v_odd_ln_b):
    local = dict(zip(WEIGHT_NAMES, (even_w_in, even_a_ln_g, even_a_ln_b, even_a_ws, even_a_bs, even_b_sinks, even_w_out, even_ln_g, even_ln_b, odd_w_in, odd_conv_w, odd_conv_b, odd_w_a, odd_b_a, odd_w_x, odd_b_x, odd_lam, odd_w_pool, odd_d_scale, odd_w_out, odd_ln_g, odd_ln_b)))
    mom = dict(zip(WEIGHT_NAMES, (m_even_w_in, m_even_a_ln_g, m_even_a_ln_b, m_even_a_ws, m_even_a_bs, m_even_b_sinks, m_even_w_out, m_even_ln_g, m_even_ln_b, m_odd_w_in, m_odd_conv_w, m_odd_conv_b, m_odd_w_a, m_odd_b_a, m_odd_w_x, m_odd_b_x, m_odd_lam, m_odd_w_pool, m_odd_d_scale, m_odd_w_out, m_odd_ln_g, m_odd_ln_b)))
    vel = dict(zip(WEIGHT_NAMES, (v_even_w_in, v_even_a_ln_g, v_even_a_ln_b, v_even_a_ws, v_even_a_bs, v_even_b_sinks, v_even_w_out, v_even_ln_g, v_even_ln_b, v_odd_w_in, v_odd_conv_w, v_odd_conv_b, v_odd_w_a, v_odd_b_a, v_odd_w_x, v_odd_b_x, v_odd_lam, v_odd_w_pool, v_odd_d_scale, v_odd_w_out, v_odd_ln_g, v_odd_ln_b)))
    mx, my, core = _where_am_i()
    chip = 2 * mx + my

    packed_big, packed_small = _pack_for_gather(local)
    weights = dict(local)
    weights.update(_unpack_gathered(_two_level_gather(packed_big, chip, "gather_weights"),
                                    _two_level_gather(packed_small, chip, "gather_small_weights")))

    loss, grad_x, grads = _local_step(x[0], positions[0], loss_target[0], weights)

    packed = _pack_big_grads(grads)
    from_sibling = _sibling_swap(packed, "big_presum_swap", other_half_of_axis1=True)
    chip_part = _add_halves(packed, from_sibling, core, BF16, "big_presum_add")
    arrived = _chip_scatter(chip_part, chip, "big_scatter")
    my_half = _sum_slots(arrived, "big_sum")
    both = jnp.stack([my_half, _sibling_swap(my_half, "big_join")])
    big_total = jnp.where(core == 0, both, both[::-1]).reshape(BIG_TOTAL_ROWS, PACK_COLS)
    reduced = _unpack_big_grads(big_total)

    small = _pack_small_grads(grads)
    chip_sum = _add2(small, _sibling_swap(small, "small_swap"), "small_presum_add")
    small_total = _sum_slots(_two_level_gather(chip_sum, chip, "small_gather"), "small_sum")
    reduced.update(_unpack_small_grads(small_total, chip))

    deltas, new_m, new_v = {}, {}, {}
    for n in WEIGHT_NAMES:
        deltas[n], new_m[n], new_v[n] = _adamw(local[n], reduced[n], mom[n], vel[n], "adamw_" + n)

    total_loss = lax.psum(loss[0, 0], ("x", "y", "c"))
    return (total_loss, grad_x[None], *[reduced[n] for n in WEIGHT_NAMES], *[deltas[n] for n in WEIGHT_NAMES],
            *[new_m[n] for n in WEIGHT_NAMES], *[new_v[n] for n in WEIGHT_NAMES])
```

```python
import functools
import math

import jax
import jax.numpy as jnp
from jax import lax
from jax.experimental import pallas as pl
from jax.experimental.pallas import tpu as pltpu

F32 = jnp.float32
BF16 = jnp.bfloat16
MXU_DTYPE = jnp.bfloat16

D_MODEL = 2048
DEPTH = 4
A_WIDTH = 1024
A_GROUPS = 8
CHUNK = 128
B_HEAD_DIM = 64
B_Q_HEADS = 16
B_KV_HEADS = 2
WINDOW = 128
ROT_DIM = 16
ROPE_THETA = 500000.0
C_WIDTH = 1024
C_HEADS = 8
CONV_WIDTH = 4
LRU_C = 8.0
D_WIDTH = 1024
D_GROUPS = 4
D_GROUP_DIM = 256
EVEN_IN = 5376
ODD_IN = 4096
DN_ALPHA = (2 * DEPTH) ** 0.25
LN_EPS = 1e-5
ATTN_SCALE = B_HEAD_DIM ** -0.5
NEG_BIG = -1e30

ADAM_LR = 0.001
ADAM_B1 = 0.9
ADAM_B2 = 0.999
ADAM_EPS = 1e-08
ADAM_WD = 0.01
ADAM_STEP = 10

LANES = 128
VMEM_LIMIT = 56 * 1024 * 1024
N_CHIPS = 4
MESH = pl.DeviceIdType.MESH

WEIGHT_NAMES = ['even_w_in', 'even_a_ln_g', 'even_a_ln_b', 'even_a_ws', 'even_a_bs', 'even_b_sinks', 'even_w_out',
                'even_ln_g', 'even_ln_b', 'odd_w_in', 'odd_conv_w', 'odd_conv_b', 'odd_w_a', 'odd_b_a', 'odd_w_x',
                'odd_b_x', 'odd_lam', 'odd_w_pool', 'odd_d_scale', 'odd_w_out', 'odd_ln_g', 'odd_ln_b']
BIG_SHARD_AXIS = {'even_w_in': 2, 'even_w_out': 1, 'odd_w_in': 2, 'odd_w_out': 1}
SMALL_SHARD_AXIS = {'odd_conv_w': 2, 'odd_conv_b': 1, 'odd_b_a': 1, 'odd_b_x': 1, 'odd_lam': 1, 'odd_w_pool': 2,
                    'odd_d_scale': 1, 'odd_ln_g': 1, 'odd_ln_b': 1}
FULL_SHAPES = {
    'even_w_in': (2, 2048, 5376), 'even_a_ln_g': (2, 1024), 'even_a_ln_b': (2, 1024), 'even_a_ws': (2, 8, 128, 128),
    'even_a_bs': (2, 8, 128), 'even_b_sinks': (2, 16), 'even_w_out': (2, 2048, 2048), 'even_ln_g': (2, 2048),
    'even_ln_b': (2, 2048), 'odd_w_in': (2, 2048, 4096), 'odd_conv_w': (2, 4, 1024), 'odd_conv_b': (2, 1024),
    'odd_w_a': (2, 8, 128, 128), 'odd_b_a': (2, 1024), 'odd_w_x': (2, 8, 128, 128), 'odd_b_x': (2, 1024),
    'odd_lam': (2, 1024), 'odd_w_pool': (2, 4, 256, 256), 'odd_d_scale': (2, 1024), 'odd_w_out': (2, 2048, 2048),
    'odd_ln_g': (2, 2048), 'odd_ln_b': (2, 2048)}
BIG_NAMES = ['even_w_in', 'even_w_out', 'odd_w_in', 'odd_w_out']
SMALL_SHARDED = ['odd_conv_w', 'odd_conv_b', 'odd_b_a', 'odd_b_x', 'odd_lam', 'odd_d_scale', 'odd_ln_g', 'odd_ln_b',
                 'odd_w_pool']
SMALL_NAMES = [n for n in WEIGHT_NAMES if n not in BIG_NAMES]
PACK_COLS = 1024


def _params(sem):
    return pltpu.CompilerParams(dimension_semantics=sem, vmem_limit_bytes=VMEM_LIMIT)


def _dot(a, b):
    return lax.dot_general(a, b, (((1,), (0,)), ((), ())), preferred_element_type=F32)


def _dot_nt(a, b):
    return lax.dot_general(a, b, (((1,), (1,)), ((), ())), preferred_element_type=F32)


def _dot_tn(a, b):
    return lax.dot_general(a, b, (((0,), (0,)), ((), ())), preferred_element_type=F32)


def _sigmoid(x):
    return 1.0 / (1.0 + jnp.exp(-x))


def _silu_and_grad(x):
    s = _sigmoid(x)
    return x * s, s * (1.0 + x * (1.0 - s))


def _mm_call(body, name, grid, in_specs, out_spec, out_shape, args):
    return pl.pallas_call(body, name=name, grid=grid, in_specs=in_specs, out_specs=out_spec, out_shape=out_shape,
                          compiler_params=_params(("parallel",) * len(grid)))(*args)


def _mm_in(x, wg, name):
    s, k = x.shape
    nc = wg.shape[2]
    tm = min(1024, s)

    def body(x_ref, w_ref, o_ref):
        o_ref[...] = _dot(x_ref[...].astype(MXU_DTYPE), w_ref[...])

    return _mm_call(body, name, (s // tm, N_CHIPS),
                    [pl.BlockSpec((tm, k), lambda i, j: (i, 0)), pl.BlockSpec((None, k, nc), lambda i, j: (j, 0, 0))],
                    pl.BlockSpec((None, tm, nc), lambda i, j: (j, i, 0)),
                    jax.ShapeDtypeStruct((N_CHIPS, s, nc), F32), (x, wg))


def _mm_dx(dh, wg, dz, name):
    _, d, nc = wg.shape
    s = dz.shape[0]
    tm, tn = min(512, s), 512

    def body(a_ref, w_ref, r_ref, o_ref):
        acc = DN_ALPHA * r_ref[...]
        for j in range(N_CHIPS):
            acc = acc + _dot_nt(a_ref[j], w_ref[j])
        o_ref[...] = acc

    return _mm_call(body, name, (s // tm, d // tn),
                    [pl.BlockSpec((N_CHIPS, tm, nc), lambda i, j: (0, i, 0)),
                     pl.BlockSpec((N_CHIPS, tn, nc), lambda i, j: (0, j, 0)),
                     pl.BlockSpec((tm, tn), lambda i, j: (i, j))],
                    pl.BlockSpec((tm, tn), lambda i, j: (i, j)), jax.ShapeDtypeStruct((s, d), F32), (dh, wg, dz))


def _mm_dw_in(x, dh, name):
    s, d = x.shape
    nc = dh.shape[2]
    tm = 1024

    def body(x_ref, b_ref, o_ref):
        o_ref[...] = _dot_tn(x_ref[...].astype(MXU_DTYPE), b_ref[...]).astype(o_ref.dtype)

    return _mm_call(body, name, (d // tm, N_CHIPS),
                    [pl.BlockSpec((s, tm), lambda i, j: (0, i)), pl.BlockSpec((None, s, nc), lambda i, j: (j, 0, 0))],
                    pl.BlockSpec((None, tm, nc), lambda i, j: (j, i, 0)),
                    jax.ShapeDtypeStruct((N_CHIPS, d, nc), MXU_DTYPE), (x, dh))


def _mm_out(mix, wg, name):
    s, k = mix.shape
    _, kc, d = wg.shape
    tm, tn = min(1024, s), 1024

    def body(a_ref, w_ref, o_ref):
        o_ref[...] = _dot(a_ref[...], w_ref[...].reshape(N_CHIPS * kc, tn))

    return _mm_call(body, name, (s // tm, d // tn),
                    [pl.BlockSpec((tm, k), lambda i, j: (i, 0)), pl.BlockSpec((N_CHIPS, kc, tn), lambda i, j: (0, 0, j))],
                    pl.BlockSpec((tm, tn), lambda i, j: (i, j)), jax.ShapeDtypeStruct((s, d), F32), (mix, wg))


def _mm_dmix(dz, wg, name):
    s, d = dz.shape
    kc = wg.shape[1]
    tm = min(1024, s)

    def body(a_ref, w_ref, o_ref):
        o_ref[...] = _dot_nt(a_ref[...], w_ref[...])

    return _mm_call(body, name, (s // tm, N_CHIPS),
                    [pl.BlockSpec((tm, d), lambda i, j: (i, 0)), pl.BlockSpec((None, kc, d), lambda i, j: (j, 0, 0))],
                    pl.BlockSpec((tm, kc), lambda i, j: (i, j)), jax.ShapeDtypeStruct((s, N_CHIPS * kc), F32), (dz, wg))


def _mm_dw_out(mix, dz, name):
    s, k = mix.shape
    d = dz.shape[1]
    kc, tn = k // N_CHIPS, 1024

    def body(a_ref, b_ref, o_ref):
        o_ref[...] = _dot_tn(a_ref[...], b_ref[...]).astype(o_ref.dtype)

    return _mm_call(body, name, (N_CHIPS, d // tn),
                    [pl.BlockSpec((s, kc), lambda i, j: (0, i)), pl.BlockSpec((s, tn), lambda i, j: (0, j))],
                    pl.BlockSpec((None, kc, tn), lambda i, j: (i, 0, j)),
                    jax.ShapeDtypeStruct((N_CHIPS, kc, d), MXU_DTYPE), (mix, dz))


def _ln_fwd(x, y, g, b):
    s, d = x.shape
    tr = min(256, s)

    def body(x_ref, y_ref, g_ref, b_ref, z_ref, o_ref, ob_ref):
        z = DN_ALPHA * x_ref[...] + y_ref[...]
        mu = jnp.mean(z, axis=1, keepdims=True)
        zc = z - mu
        var = jnp.mean(zc * zc, axis=1, keepdims=True)
        o = zc * lax.rsqrt(var + LN_EPS) * g_ref[...] + b_ref[...]
        z_ref[...] = z
        o_ref[...] = o
        ob_ref[...] = o.astype(MXU_DTYPE)

    row = pl.BlockSpec((tr, d), lambda i: (i, 0))
    vec = pl.BlockSpec((1, d), lambda i: (0, 0))
    return pl.pallas_call(
        body, name="ln_fwd", grid=(s // tr,), in_specs=[row, row, vec, vec], out_specs=[row, row, row],
        out_shape=[jax.ShapeDtypeStruct((s, d), F32), jax.ShapeDtypeStruct((s, d), F32),
                   jax.ShapeDtypeStruct((s, d), MXU_DTYPE)],
        compiler_params=_params(("parallel",)),
    )(x, y, g.reshape(1, d), b.reshape(1, d))


def _ln_bwd(dout, z, g):
    s, d = z.shape
    tr = min(256, s)

    def body(do_ref, z_ref, g_ref, dz_ref, dzb_ref, dg_ref, db_ref):
        @pl.when(pl.program_id(0) == 0)
        def _():
            dg_ref[...] = jnp.zeros_like(dg_ref)
            db_ref[...] = jnp.zeros_like(db_ref)

        zz = z_ref[...]
        do = do_ref[...]
        mu = jnp.mean(zz, axis=1, keepdims=True)
        zc = zz - mu
        var = jnp.mean(zc * zc, axis=1, keepdims=True)
        rstd = lax.rsqrt(var + LN_EPS)
        zh = zc * rstd
        dg_ref[...] += jnp.sum(do * zh, axis=0, keepdims=True)
        db_ref[...] += jnp.sum(do, axis=0, keepdims=True)
        dzh = do * g_ref[...]
        dz = rstd * (dzh - jnp.mean(dzh, axis=1, keepdims=True) - zh * jnp.mean(dzh * zh, axis=1, keepdims=True))
        dz_ref[...] = dz
        dzb_ref[...] = dz.astype(MXU_DTYPE)

    row = pl.BlockSpec((tr, d), lambda i: (i, 0))
    vec = pl.BlockSpec((1, d), lambda i: (0, 0))
    return pl.pallas_call(
        body, name="ln_bwd", grid=(s // tr,), in_specs=[row, row, vec], out_specs=[row, row, vec, vec],
        out_shape=[jax.ShapeDtypeStruct((s, d), F32), jax.ShapeDtypeStruct((s, d), MXU_DTYPE),
                   jax.ShapeDtypeStruct((1, d), F32), jax.ShapeDtypeStruct((1, d), F32)],
        compiler_params=_params(("arbitrary",)),
    )(dout, z, g.reshape(1, d))


def _loss_and_grad(out, tgt):
    s, d = out.shape
    tr = min(256, s)

    def body(o_ref, t_ref, l_ref, do_ref):
        @pl.when(pl.program_id(0) == 0)
        def _():
            l_ref[...] = jnp.zeros_like(l_ref)

        e = o_ref[...] - t_ref[...]
        do_ref[...] = e * (1.0 / d)
        rows = jnp.sum(e * e, axis=1, keepdims=True)
        l_ref[...] += (0.5 / d) * jnp.sum(rows, axis=0, keepdims=True)

    row = pl.BlockSpec((tr, d), lambda i: (i, 0))
    return pl.pallas_call(
        body, name="loss", grid=(s // tr,), in_specs=[row, row],
        out_specs=[pl.BlockSpec((1, 1), lambda i: (0, 0)), row],
        out_shape=[jax.ShapeDtypeStruct((1, 1), F32), jax.ShapeDtypeStruct((s, d), F32)],
        compiler_params=_params(("arbitrary",)),
    )(out, tgt)


Q0, K0, V0, BG0 = 3072, 4096, 4224, 4352
T = CHUNK


def _lane_ids(width):
    return lax.broadcasted_iota(jnp.int32, (T, width), 1)


def _rope_swap(x):
    w = x.shape[1]
    l64 = _lane_ids(w) % B_HEAD_DIM
    half = ROT_DIM // 2
    return jnp.where(l64 < half, pltpu.roll(x, w - half, 1), jnp.where(l64 < ROT_DIM, pltpu.roll(x, half, 1), 0.0))


def _rope(x, c, s):
    return x * c + _rope_swap(x) * s


def _rope_bwd(d, c, s):
    return d * c + _rope_swap(d * s)


def _tril():
    return lax.broadcasted_iota(jnp.int32, (T, T), 0) >= lax.broadcasted_iota(jnp.int32, (T, T), 1)


def _band_mask(has_prev):
    qi = lax.broadcasted_iota(jnp.int32, (T, 2 * T), 0)
    kj = lax.broadcasted_iota(jnp.int32, (T, 2 * T), 1)
    return (kj > qi) & (kj <= qi + WINDOW) & ((kj >= WINDOW) | has_prev)


def _a_norm(v, g, b):
    mu = jnp.mean(v, axis=1, keepdims=True)
    vc = v - mu
    var = jnp.mean(vc * vc, axis=1, keepdims=True)
    rstd = lax.rsqrt(var + LN_EPS)
    vh = vc * rstd
    return vh, rstd, vh * g + b


def _kv_operands(hk, k_rot, kp_rot, vv, vp):
    lo = _lane_ids(LANES) < B_HEAD_DIM
    mine = lo if hk == 0 else jnp.logical_not(lo)
    kc = jnp.where(mine, k_rot, 0.0)
    kp = jnp.where(mine, kp_rot, 0.0)
    k2 = jnp.concatenate([kp + pltpu.roll(kp, B_HEAD_DIM, 1), kc + pltpu.roll(kc, B_HEAD_DIM, 1)], axis=0)
    vm = jnp.concatenate([jnp.where(mine, vp, 0.0), jnp.where(mine, vv, 0.0)], axis=0)
    vs = pltpu.roll(vm, B_HEAD_DIM, 1)
    v_lo, v_hi = (vm, vs) if hk == 0 else (vs, vm)
    return k2.astype(MXU_DTYPE), v_lo.astype(MXU_DTYPE), v_hi.astype(MXU_DTYPE), (vm + vs).astype(MXU_DTYPE)


def _softmax_with_sink(qm, k2, valid, sink):
    s = _dot_nt(qm, k2) * ATTN_SCALE
    s = jnp.where(valid, s, NEG_BIG)
    m = jnp.maximum(jnp.max(s, axis=1, keepdims=True), sink)
    ex = jnp.exp(s - m)
    es = jnp.exp(sink - m)
    inv = 1.0 / (jnp.sum(ex, axis=1, keepdims=True) + es)
    return ex * inv, es * inv


EVEN_SHARD = EVEN_IN // N_CHIPS
KV_CHIP = K0 // EVEN_SHARD
KV_SPAN = 3 * LANES
KP0, VP0 = K0 - KV_CHIP * EVEN_SHARD, V0 - KV_CHIP * EVEN_SHARD
assert V0 // EVEN_SHARD == KV_CHIP and VP0 + LANES <= KV_SPAN


def _even_specs(s):
    nb = s // T
    return nb, {
        'h': lambda r: pl.BlockSpec((N_CHIPS, T, EVEN_SHARD), lambda i: (0, r(i), 0)),
        'kvprev': lambda r: pl.BlockSpec((None, T, KV_SPAN), lambda i: (KV_CHIP, jnp.maximum(r(i) - 1, 0), 0)),
        'tab': lambda r: pl.BlockSpec((T, A_WIDTH), lambda i: (r(i), 0)),
        'tabprev': lambda r: pl.BlockSpec((T, LANES), lambda i: (jnp.maximum(r(i) - 1, 0), 0)),
    }


def _rows_from_shards(h4_ref, rows_ref):
    for j in range(N_CHIPS):
        rows_ref[:, j * EVEN_SHARD:(j + 1) * EVEN_SHARD] = h4_ref[j]


def _full(shape):
    nd = len(shape)
    return pl.BlockSpec(shape, lambda i: (0,) * nd)


def _even_fwd(h4, cq, sq, ln_g, ln_b, ws, bs_t, sinks):
    s = h4.shape[1]
    nb, sp = _even_specs(s)

    def body(h4_ref, kvp_ref, cq_ref, sq_ref, ckp_ref, skp_ref, g_ref, b_ref, ws_ref, bst_ref, sink_ref, mix_ref, h_ref):
        i = pl.program_id(0)
        _rows_from_shards(h4_ref, h_ref)
        tril = _tril()
        _, _, vn = _a_norm(h_ref[:, 1024:2048], g_ref[...], b_ref[...])
        vnb = vn.astype(MXU_DTYPE)
        for g in range(A_GROUPS):
            c0, c1 = g * LANES, (g + 1) * LANES
            wg = jnp.where(tril, ws_ref[g], 0.0).astype(MXU_DTYPE)
            mixed = _dot(wg, vnb[:, c0:c1]) + bst_ref[:, g:g + 1]
            sil, _ = _silu_and_grad(h_ref[:, 2048 + c0:2048 + c1])
            mix_ref[:, c0:c1] = (h_ref[:, c0:c1] * mixed * sil).astype(mix_ref.dtype)

        k_rot = _rope(h_ref[:, K0:K0 + LANES], cq_ref[:, 0:LANES], sq_ref[:, 0:LANES])
        kp_rot = _rope(kvp_ref[:, KP0:KP0 + LANES], ckp_ref[...], skp_ref[...])
        vp = kvp_ref[:, VP0:VP0 + LANES]
        valid = _band_mask(i > 0)
        lo = _lane_ids(LANES) < B_HEAD_DIM
        for hk in range(B_KV_HEADS):
            k2, v_lo, v_hi, _ = _kv_operands(hk, k_rot, kp_rot, h_ref[:, V0:V0 + LANES], vp)
            for pp in range(4):
                pair = hk * 4 + pp
                c0, c1 = pair * LANES, (pair + 1) * LANES
                qp = _rope(h_ref[:, Q0 + c0:Q0 + c1], cq_ref[:, c0:c1], sq_ref[:, c0:c1])
                acc = jnp.zeros((T, LANES), F32)
                for e in range(2):
                    head = 2 * pair + e
                    qm = jnp.where(lo if e == 0 else jnp.logical_not(lo), qp, 0.0).astype(MXU_DTYPE)
                    p, _ = _softmax_with_sink(qm, k2, valid, sink_ref[:, head:head + 1])
                    acc = acc + _dot(p.astype(MXU_DTYPE), v_lo if e == 0 else v_hi)
                sil, _ = _silu_and_grad(h_ref[:, BG0 + c0:BG0 + c1])
                mix_ref[:, A_WIDTH + c0:A_WIDTH + c1] = (acc * sil).astype(mix_ref.dtype)

    ident = lambda i: i
    return pl.pallas_call(
        body, name="even_fwd", grid=(nb,),
        in_specs=[sp['h'](ident), sp['kvprev'](ident), sp['tab'](ident), sp['tab'](ident),
                  sp['tabprev'](ident), sp['tabprev'](ident), _full((1, A_WIDTH)), _full((1, A_WIDTH)),
                  _full((A_GROUPS, T, T)), _full((T, A_GROUPS)), _full((1, B_Q_HEADS))],
        out_specs=pl.BlockSpec((T, 2 * A_WIDTH), lambda i: (i, 0)),
        out_shape=jax.ShapeDtypeStruct((s, 2 * A_WIDTH), MXU_DTYPE),
        scratch_shapes=[pltpu.VMEM((T, EVEN_IN), F32)],
        compiler_params=_params(("parallel",)),
    )(h4, h4, cq, sq, cq, sq, ln_g.reshape(1, -1), ln_b.reshape(1, -1), ws, bs_t, sinks.reshape(1, -1))


def _even_bwd(h4, dmix, cq, sq, ln_g, ln_b, ws, bs_t, sinks):
    s = h4.shape[1]
    nb, sp = _even_specs(s)

    def body(h4_ref, kvp_ref, cq_ref, sq_ref, ckp_ref, skp_ref, dm_ref, g_ref, b_ref, ws_ref, bst_ref, sink_ref,
             dh4_ref, dg_ref, db_ref, dws_ref, dbst_ref, dsink_ref, h_ref, dh_ref, dvn_scr, ck_scr, cv_scr):
        i = pl.program_id(0)

        @pl.when(i == 0)
        def _():
            for ref in (dg_ref, db_ref, dws_ref, dbst_ref, dsink_ref, ck_scr, cv_scr):
                ref[...] = jnp.zeros_like(ref)

        _rows_from_shards(h4_ref, h_ref)

        tril = _tril()
        vh, rstd, vn = _a_norm(h_ref[:, 1024:2048], g_ref[...], b_ref[...])
        vnb = vn.astype(MXU_DTYPE)
        for g in range(A_GROUPS):
            c0, c1 = g * LANES, (g + 1) * LANES
            wg = jnp.where(tril, ws_ref[g], 0.0).astype(MXU_DTYPE)
            mixed = _dot(wg, vnb[:, c0:c1]) + bst_ref[:, g:g + 1]
            ag = h_ref[:, 2048 + c0:2048 + c1]
            u = h_ref[:, c0:c1]
            sil, dsil = _silu_and_grad(ag)
            da = dm_ref[:, c0:c1]
            dh_ref[:, c0:c1] = (da * mixed * sil).astype(dh_ref.dtype)
            dh_ref[:, 2048 + c0:2048 + c1] = (da * u * mixed * dsil).astype(dh_ref.dtype)
            dmx = da * u * sil
            dmb = dmx.astype(MXU_DTYPE)
            dvn_scr[:, c0:c1] = _dot_tn(wg, dmb)
            dws_ref[g] += jnp.where(tril, _dot_nt(dmb, vnb[:, c0:c1]), 0.0)
            dbst_ref[:, g:g + 1] += jnp.sum(dmx, axis=1, keepdims=True)
        dvn = dvn_scr[...]
        dg_ref[...] += jnp.sum(dvn * vh, axis=0, keepdims=True)
        db_ref[...] += jnp.sum(dvn, axis=0, keepdims=True)
        dvh = dvn * g_ref[...]
        dv = rstd * (dvh - jnp.mean(dvh, axis=1, keepdims=True) - vh * jnp.mean(dvh * vh, axis=1, keepdims=True))
        dh_ref[:, 1024:2048] = dv.astype(dh_ref.dtype)

        ck_cur, sk_cur = cq_ref[:, 0:LANES], sq_ref[:, 0:LANES]
        k_rot = _rope(h_ref[:, K0:K0 + LANES], ck_cur, sk_cur)
        kp_rot = _rope(kvp_ref[:, KP0:KP0 + LANES], ckp_ref[...], skp_ref[...])
        vp = kvp_ref[:, VP0:VP0 + LANES]
        valid = _band_mask(i < nb - 1)
        lo = _lane_ids(LANES) < B_HEAD_DIM
        hi = jnp.logical_not(lo)
        dk_all = jnp.zeros((2 * T, LANES), F32)
        dv_all = jnp.zeros((2 * T, LANES), F32)
        for hk in range(B_KV_HEADS):
            k2, v_lo, v_hi, v2 = _kv_operands(hk, k_rot, kp_rot, h_ref[:, V0:V0 + LANES], vp)
            acc_k = jnp.zeros((2 * T, LANES), F32)
            acc_v = jnp.zeros((2 * T, LANES), F32)
            for pp in range(4):
                pair = hk * 4 + pp
                c0, c1 = pair * LANES, (pair + 1) * LANES
                cqp, sqp = cq_ref[:, c0:c1], sq_ref[:, c0:c1]
                qp = _rope(h_ref[:, Q0 + c0:Q0 + c1], cqp, sqp)
                qms, ps, pss = [], [], []
                ob = jnp.zeros((T, LANES), F32)
                for e in range(2):
                    head = 2 * pair + e
                    qm = jnp.where(lo if e == 0 else hi, qp, 0.0).astype(MXU_DTYPE)
                    p, psink = _softmax_with_sink(qm, k2, valid, sink_ref[:, head:head + 1])
                    ob = ob + _dot(p.astype(MXU_DTYPE), v_lo if e == 0 else v_hi)
                    qms.append(qm)
                    ps.append(p)
                    pss.append(psink)
                sil, dsil = _silu_and_grad(h_ref[:, BG0 + c0:BG0 + c1])
                dbo = dm_ref[:, A_WIDTH + c0:A_WIDTH + c1]
                dh_ref[:, BG0 + c0:BG0 + c1] = (dbo * ob * dsil).astype(dh_ref.dtype)
                dob = dbo * sil
                dq_rot = jnp.zeros((T, LANES), F32)
                for e in range(2):
                    head = 2 * pair + e
                    sel = lo if e == 0 else hi
                    dobm = jnp.where(sel, dob, 0.0).astype(MXU_DTYPE)
                    dp = _dot_nt(dobm, v2)
                    rs = jnp.sum(ps[e] * dp, axis=1, keepdims=True)
                    ds = ps[e] * (dp - rs) * ATTN_SCALE
                    dsink_ref[:, head:head + 1] += -jnp.sum(pss[e] * rs, axis=0, keepdims=True)
                    dsb = ds.astype(MXU_DTYPE)
                    dq_rot = dq_rot + jnp.where(sel, _dot(dsb, k2), 0.0)
                    acc_k = acc_k + _dot_tn(dsb, qms[e])
                    acc_v = acc_v + _dot_tn(ps[e].astype(MXU_DTYPE), dobm)
                dh_ref[:, Q0 + c0:Q0 + c1] = _rope_bwd(dq_rot, cqp, sqp).astype(dh_ref.dtype)
            lo2 = lax.broadcasted_iota(jnp.int32, (2 * T, LANES), 1) < B_HEAD_DIM
            mine = lo2 if hk == 0 else jnp.logical_not(lo2)
            dk_all = dk_all + jnp.where(mine, acc_k + pltpu.roll(acc_k, B_HEAD_DIM, 1), 0.0)
            dv_all = dv_all + jnp.where(mine, acc_v + pltpu.roll(acc_v, B_HEAD_DIM, 1), 0.0)
        dk_rot = dk_all[T:2 * T] + ck_scr[...]
        dvv = dv_all[T:2 * T] + cv_scr[...]
        ck_scr[...] = dk_all[0:T]
        cv_scr[...] = dv_all[0:T]
        dh_ref[:, K0:K0 + LANES] = _rope_bwd(dk_rot, ck_cur, sk_cur).astype(dh_ref.dtype)
        dh_ref[:, V0:V0 + LANES] = dvv.astype(dh_ref.dtype)
        for j in range(N_CHIPS):
            dh4_ref[j] = dh_ref[:, j * EVEN_SHARD:(j + 1) * EVEN_SHARD].astype(dh4_ref.dtype)

    rev = lambda i: nb - 1 - i
    return pl.pallas_call(
        body, name="even_bwd", grid=(nb,),
        in_specs=[sp['h'](rev), sp['kvprev'](rev), sp['tab'](rev), sp['tab'](rev),
                  sp['tabprev'](rev), sp['tabprev'](rev), pl.BlockSpec((T, 2 * A_WIDTH), lambda i: (rev(i), 0)),
                  _full((1, A_WIDTH)), _full((1, A_WIDTH)), _full((A_GROUPS, T, T)), _full((T, A_GROUPS)),
                  _full((1, B_Q_HEADS))],
        out_specs=[sp['h'](rev), _full((1, A_WIDTH)), _full((1, A_WIDTH)),
                   _full((A_GROUPS, T, T)), _full((T, A_GROUPS)), _full((1, B_Q_HEADS))],
        out_shape=[jax.ShapeDtypeStruct((N_CHIPS, s, EVEN_SHARD), MXU_DTYPE), jax.ShapeDtypeStruct((1, A_WIDTH), F32),
                   jax.ShapeDtypeStruct((1, A_WIDTH), F32), jax.ShapeDtypeStruct((A_GROUPS, T, T), F32),
                   jax.ShapeDtypeStruct((T, A_GROUPS), F32), jax.ShapeDtypeStruct((1, B_Q_HEADS), F32)],
        scratch_shapes=[pltpu.VMEM((T, EVEN_IN), F32), pltpu.VMEM((T, EVEN_IN), F32), pltpu.VMEM((T, A_WIDTH), F32),
                        pltpu.VMEM((T, LANES), F32), pltpu.VMEM((T, LANES), F32)],
        compiler_params=_params(("arbitrary",)),
    )(h4, h4, cq, sq, cq, sq, dmix, ln_g.reshape(1, -1), ln_b.reshape(1, -1), ws, bs_t, sinks.reshape(1, -1))


def _shift_down(x, d, fill=0.0):
    rows = lax.broadcasted_iota(jnp.int32, x.shape, 0)
    return jnp.where(rows >= d, pltpu.roll(x, d, 0), fill)


def _shift_up(x, d, fill=0.0):
    n = x.shape[0]
    rows = lax.broadcasted_iota(jnp.int32, x.shape, 0)
    return jnp.where(rows < n - d, pltpu.roll(x, n - d, 0), fill)


def _scan(a, b, shift):
    n = a.shape[0]
    d = 1
    while d < n:
        b = a * shift(b, d) + b
        if 2 * d < n:
            a = a * shift(a, d, 1.0)
        d *= 2
    return b


def _neg_expm1(y):
    u = jnp.exp(y)
    um1 = u - 1.0
    safe = jnp.where(um1 == 0.0, 1.0, jnp.log(u))
    near = jnp.where(um1 == 0.0, y, um1 * y / safe)
    return -jnp.where(y > -0.5, near, um1)


def _softplus(x):
    return jnp.maximum(x, 0.0) + jnp.log1p(jnp.exp(-jnp.abs(x)))


def _lru_forward(xc, cw, cb, wa, ba, wx, bx, lam):
    shifted = [_shift_down(xc, CONV_WIDTH - 1 - j) if j < CONV_WIDTH - 1 else xc for j in range(CONV_WIDTH)]
    xconv = cb
    for j in range(CONV_WIDTH):
        xconv = xconv + cw[j:j + 1, :] * shifted[j]
    xb = xconv.astype(MXU_DTYPE)
    r = _sigmoid(_dot(xb, wa.astype(MXU_DTYPE)) + ba)
    ig = _sigmoid(_dot(xb, wx.astype(MXU_DTYPE)) + bx)
    sp = _softplus(-lam)
    log_a = -LRU_C * r * sp
    a = jnp.exp(log_a)
    mult = jnp.sqrt(_neg_expm1(2.0 * log_a))
    hs = _scan(a, mult * ig * xconv, _shift_down)
    return dict(shifted=shifted, xconv=xconv, xb=xb, r=r, ig=ig, sp=sp, a=a, mult=mult, hs=hs)


def _col(width, off):
    return lambda s: pl.BlockSpec((s, width), lambda i: (0, off + i))


def _part(width, slot):
    return lambda s: pl.BlockSpec((None, s, width), lambda i: (slot, 0, i))


def _lru_specs(s):
    w = LANES
    return [_part(w, 0)(s), _part(w, 1)(s), pl.BlockSpec((CONV_WIDTH, w), lambda i: (0, i)),
            pl.BlockSpec((1, w), lambda i: (0, i)), pl.BlockSpec((None, w, w), lambda i: (i, 0, 0)),
            pl.BlockSpec((1, w), lambda i: (0, i)), pl.BlockSpec((None, w, w), lambda i: (i, 0, 0)),
            pl.BlockSpec((1, w), lambda i: (0, i)), pl.BlockSpec((1, w), lambda i: (0, i))]


def _lru_fwd(h, cw, cb, wa, ba, wx, bx, lam):
    s = h.shape[1]

    def body(xc_ref, cg_ref, cw_ref, cb_ref, wa_ref, ba_ref, wx_ref, bx_ref, lam_ref, o_ref):
        f = _lru_forward(xc_ref[...], cw_ref[...], cb_ref[...], wa_ref[...], ba_ref[...], wx_ref[...], bx_ref[...],
                         lam_ref[...])
        sil, _ = _silu_and_grad(cg_ref[...])
        o_ref[...] = (f['hs'] * sil).astype(o_ref.dtype)

    return pl.pallas_call(
        body, name="lru_fwd", grid=(C_HEADS,), in_specs=_lru_specs(s), out_specs=_col(LANES, 0)(s),
        out_shape=jax.ShapeDtypeStruct((s, C_WIDTH + D_WIDTH), MXU_DTYPE), compiler_params=_params(("parallel",)),
    )(h, h, cw, cb.reshape(1, -1), wa, ba.reshape(1, -1), wx, bx.reshape(1, -1), lam.reshape(1, -1))


def _lru_bwd(h, dmix, cw, cb, wa, ba, wx, bx, lam):
    s = h.shape[1]

    def body(xc_ref, cg_ref, cw_ref, cb_ref, wa_ref, ba_ref, wx_ref, bx_ref, lam_ref, dco_ref,
             dh_ref, dcw_ref, dcb_ref, dwa_ref, dba_ref, dwx_ref, dbx_ref, dlam_ref):
        dxc_ref, dcg_ref = dh_ref.at[0], dh_ref.at[1]
        cw, lam = cw_ref[...], lam_ref[...]
        wab, wxb = wa_ref[...].astype(MXU_DTYPE), wx_ref[...].astype(MXU_DTYPE)
        f = _lru_forward(xc_ref[...], cw, cb_ref[...], wa_ref[...], ba_ref[...], wx_ref[...], bx_ref[...], lam)
        sil, dsil = _silu_and_grad(cg_ref[...])
        dco = dco_ref[...]
        hs, a, mult, ig, r, xconv = f['hs'], f['a'], f['mult'], f['ig'], f['r'], f['xconv']
        dcg_ref[...] = (dco * hs * dsil).astype(dcg_ref.dtype)
        lamb = _scan(_shift_up(a, 1), dco * sil, _shift_up)
        da = lamb * _shift_down(hs, 1)
        d_ig = lamb * mult * xconv
        d_mult = lamb * ig * xconv
        dxconv = lamb * mult * ig
        d_log_a = da * a - d_mult * (a * a) / mult
        dlam_ref[...] = jnp.sum(d_log_a * r, axis=0, keepdims=True) * LRU_C * _sigmoid(-lam)
        dpa = d_log_a * (-LRU_C * f['sp']) * r * (1.0 - r)
        dpx = d_ig * ig * (1.0 - ig)
        dba_ref[...] = jnp.sum(dpa, axis=0, keepdims=True)
        dbx_ref[...] = jnp.sum(dpx, axis=0, keepdims=True)
        dpab, dpxb = dpa.astype(MXU_DTYPE), dpx.astype(MXU_DTYPE)
        dwa_ref[...] = _dot_tn(f['xb'], dpab)
        dwx_ref[...] = _dot_tn(f['xb'], dpxb)
        dxconv = dxconv + _dot_nt(dpab, wab) + _dot_nt(dpxb, wxb)
        dcb_ref[...] = jnp.sum(dxconv, axis=0, keepdims=True)
        dxc = jnp.zeros_like(dxconv)
        for j in range(CONV_WIDTH):
            dcw_ref[j:j + 1, :] = jnp.sum(dxconv * f['shifted'][j], axis=0, keepdims=True)
            back = _shift_up(dxconv, CONV_WIDTH - 1 - j) if j < CONV_WIDTH - 1 else dxconv
            dxc = dxc + cw[j:j + 1, :] * back
        dxc_ref[...] = dxc.astype(dxc_ref.dtype)

    w = LANES
    vec = pl.BlockSpec((1, w), lambda i: (0, i))
    mat = pl.BlockSpec((None, w, w), lambda i: (i, 0, 0))
    vshape = jax.ShapeDtypeStruct((1, C_WIDTH), F32)
    mshape = jax.ShapeDtypeStruct((C_HEADS, w, w), F32)
    return pl.pallas_call(
        body, name="lru_bwd", grid=(C_HEADS,), in_specs=_lru_specs(s) + [_col(w, 0)(s)],
        out_specs=[pl.BlockSpec((2, s, w), lambda i: (0, 0, i)), pl.BlockSpec((CONV_WIDTH, w), lambda i: (0, i)),
                   vec, mat, vec, mat, vec, vec],
        out_shape=[jax.ShapeDtypeStruct((N_CHIPS, s, C_WIDTH), MXU_DTYPE),
                   jax.ShapeDtypeStruct((CONV_WIDTH, C_WIDTH), F32), vshape, mshape, vshape, mshape, vshape, vshape],
        compiler_params=_params(("parallel",)),
    )(h, h, cw, cb.reshape(1, -1), wa, ba.reshape(1, -1), wx, bx.reshape(1, -1), lam.reshape(1, -1), dmix)


def _pool_window(g):
    return jnp.left_shift(2, g).astype(F32)


def _select_window(g, sums):
    out = sums[-1]
    for k in range(len(sums) - 2, -1, -1):
        out = jnp.where(g == k, sums[k], out)
    return out


def _window_sums(x, shift):
    sums, cur = [], x
    for k in range(D_GROUPS):
        cur = cur + shift(cur, 2 ** k)
        sums.append(cur)
    return sums


def _pool_specs(s):
    w = D_GROUP_DIM
    return [_part(w, 2)(s), _part(w, 3)(s),
            pl.BlockSpec((None, w, w), lambda i: (i, 0, 0)), pl.BlockSpec((1, w), lambda i: (0, i))]


def _pool_forward(xd, g):
    rows = lax.broadcasted_iota(jnp.int32, xd.shape, 0).astype(F32) + 1.0
    cnt = jnp.minimum(rows, _pool_window(g))
    pooled = _select_window(g, _window_sums(xd, _shift_down)) / cnt - xd
    return pooled, cnt


def _pool_fwd(h, mix, w_pool, d_scale):
    s = h.shape[1]

    def body(xd_ref, dg_ref, wp_ref, sc_ref, mix_ref, o_ref):
        pooled, _ = _pool_forward(xd_ref[...], pl.program_id(0))
        mixed = _dot(pooled.astype(MXU_DTYPE), wp_ref[...].astype(MXU_DTYPE))
        sil, _ = _silu_and_grad(dg_ref[...])
        o_ref[...] = (mixed * sc_ref[...] * sil).astype(o_ref.dtype)

    return pl.pallas_call(
        body, name="pool_fwd", grid=(D_GROUPS,), in_specs=_pool_specs(s) + [ANY],
        out_specs=_col(D_GROUP_DIM, C_WIDTH // D_GROUP_DIM)(s), out_shape=jax.ShapeDtypeStruct(mix.shape, mix.dtype),
        input_output_aliases={4: 0}, compiler_params=_params(("parallel",)),
    )(h, h, w_pool, d_scale.reshape(1, -1), mix)


def _pool_bwd(h, dmix, dh, w_pool, d_scale):
    s = h.shape[1]
    w = D_GROUP_DIM

    def body(xd_ref, dg_ref, wp_ref, sc_ref, ddo_ref, dh_in_ref, dh_ref, dwp_ref, dsc_ref):
        dxd_ref, ddg_ref = dh_ref.at[0], dh_ref.at[1]
        g = pl.program_id(0)
        pooled, cnt = _pool_forward(xd_ref[...], g)
        pb = pooled.astype(MXU_DTYPE)
        wpb = wp_ref[...].astype(MXU_DTYPE)
        mixed = _dot(pb, wpb)
        sil, dsil = _silu_and_grad(dg_ref[...])
        ddo = ddo_ref[...]
        sc = sc_ref[...]
        ddg_ref[...] = (ddo * mixed * sc * dsil).astype(ddg_ref.dtype)
        dms = ddo * sil
        dsc_ref[...] = jnp.sum(dms * mixed, axis=0, keepdims=True)
        dmb = (dms * sc).astype(MXU_DTYPE)
        dwp_ref[...] = _dot_tn(pb, dmb)
        dpooled = _dot_nt(dmb, wpb)
        dxd = _select_window(g, _window_sums(dpooled / cnt, _shift_up)) - dpooled
        dxd_ref[...] = dxd.astype(dxd_ref.dtype)

    return pl.pallas_call(
        body, name="pool_bwd", grid=(D_GROUPS,), in_specs=_pool_specs(s) + [_col(w, C_WIDTH // w)(s), ANY],
        out_specs=[pl.BlockSpec((2, s, w), lambda i: (1, 0, i)), pl.BlockSpec((None, w, w), lambda i: (i, 0, 0)),
                   pl.BlockSpec((1, w), lambda i: (0, i))],
        out_shape=[jax.ShapeDtypeStruct(dh.shape, dh.dtype),
                   jax.ShapeDtypeStruct((D_GROUPS, w, w), F32), jax.ShapeDtypeStruct((1, D_WIDTH), F32)],
        input_output_aliases={5: 0}, compiler_params=_params(("parallel",)),
    )(h, h, w_pool, d_scale.reshape(1, -1), dmix, dh)


def _rope_tables(positions):
    s = positions.shape[0]
    inv_freq = ROPE_THETA ** (-jnp.arange(0, ROT_DIM, 2, dtype=F32) / ROT_DIM)
    ang = positions.astype(F32)[:, None] * inv_freq
    c, sn = jnp.cos(ang), jnp.sin(ang)
    rest = B_HEAD_DIM - ROT_DIM
    c64 = jnp.concatenate([c, c, jnp.ones((s, rest), F32)], axis=1)
    s64 = jnp.concatenate([-sn, sn, jnp.zeros((s, rest), F32)], axis=1)
    return jnp.tile(c64, (1, B_Q_HEADS)), jnp.tile(s64, (1, B_Q_HEADS))


def _local_step(x, positions, target, w):
    cq, sq = _rope_tables(positions)
    saved = []
    xin, xin_b = x, x
    for layer in range(DEPTH):
        j = layer // 2
        kind = 'even' if layer % 2 == 0 else 'odd'
        h = _mm_in(xin_b, w[kind + '_w_in'][j], "mm_" + kind + "_in")
        if kind == 'even':
            mix = _even_fwd(h, cq, sq, w['even_a_ln_g'][j], w['even_a_ln_b'][j], w['even_a_ws'][j],
                            w['even_a_bs'][j].T, w['even_b_sinks'][j])
        else:
            mix = _lru_fwd(h, w['odd_conv_w'][j], w['odd_conv_b'][j], w['odd_w_a'][j], w['odd_b_a'][j],
                           w['odd_w_x'][j], w['odd_b_x'][j], w['odd_lam'][j])
            mix = _pool_fwd(h, mix, w['odd_w_pool'][j], w['odd_d_scale'][j])
        y = _mm_out(mix, w[kind + '_w_out'][j], "mm_out")
        z, out, out_b = _ln_fwd(xin, y, w[kind + '_ln_g'][j], w[kind + '_ln_b'][j])
        saved.append((xin_b, h, mix, z))
        xin, xin_b = out, out_b

    loss, dout = _loss_and_grad(xin, target)
    per_layer = {n: [None, None] for n in WEIGHT_NAMES}
    for layer in reversed(range(DEPTH)):
        j = layer // 2
        kind = 'even' if layer % 2 == 0 else 'odd'
        xin_b, h, mix, z = saved[layer]
        dz, dz_b, dg, db = _ln_bwd(dout, z, w[kind + '_ln_g'][j])
        per_layer[kind + '_ln_g'][j], per_layer[kind + '_ln_b'][j] = dg[0], db[0]
        dmix = _mm_dmix(dz_b, w[kind + '_w_out'][j], "mm_dmix")
        per_layer[kind + '_w_out'][j] = _mm_dw_out(mix, dz_b, "mm_dw_out")
        if kind == 'even':
            dh, dlg, dlb, dws, dbst, dsink = _even_bwd(
                h, dmix, cq, sq, w['even_a_ln_g'][j], w['even_a_ln_b'][j], w['even_a_ws'][j], w['even_a_bs'][j].T,
                w['even_b_sinks'][j])
            for n, val in (('even_a_ln_g', dlg[0]), ('even_a_ln_b', dlb[0]), ('even_a_ws', dws),
                           ('even_a_bs', dbst.T), ('even_b_sinks', dsink[0])):
                per_layer[n][j] = val
        else:
            dh, dcw, dcb, dwa, dba, dwx, dbx, dlam = _lru_bwd(
                h, dmix, w['odd_conv_w'][j], w['odd_conv_b'][j], w['odd_w_a'][j], w['odd_b_a'][j], w['odd_w_x'][j],
                w['odd_b_x'][j], w['odd_lam'][j])
            dh, dwp, dsc = _pool_bwd(h, dmix, dh, w['odd_w_pool'][j], w['odd_d_scale'][j])
            for n, val in (('odd_conv_w', dcw), ('odd_conv_b', dcb[0]), ('odd_w_a', dwa), ('odd_b_a', dba[0]),
                           ('odd_w_x', dwx), ('odd_b_x', dbx[0]), ('odd_lam', dlam[0]), ('odd_w_pool', dwp),
                           ('odd_d_scale', dsc[0])):
                per_layer[n][j] = val
        dout = _mm_dx(dh, w[kind + '_w_in'][j], dz, "mm_dx_" + kind)
        per_layer[kind + '_w_in'][j] = _mm_dw_in(xin_b, dh, "mm_dw_in_" + kind)
    grads = {n: (v if n in BIG_NAMES else jnp.stack(v)) for n, v in per_layer.items()}
    return loss, dout, grads


ANY = pl.BlockSpec(memory_space=pl.ANY)


def _where_am_i():
    return lax.axis_index("x"), lax.axis_index("y"), lax.axis_index("c")


def _other_chips(x, y):
    return [(1 - x, y), (x, 1 - y), (1 - x, 1 - y)]


def _half_rows(ref_or_array_rows, which):
    half = ref_or_array_rows // 2
    return pl.ds(pl.multiple_of(which * half, 16), half)


def _two_level_gather(bufs, chip, name):
    n = len(bufs)

    def body(*refs):
        x_refs, out_refs, (send_sems, recv_sems) = refs[:n], refs[n:2 * n], refs[2 * n:]
        x, y, c = _where_am_i()
        sibling = (x, y, 1 - c)
        chips = _other_chips(x, y)

        def rows(p, px, py, pc):
            return out_refs[p].at[2 * px + py, _half_rows(bufs[p].shape[0], pc), :]

        def copy(p, k, block, to, src=None):
            return pltpu.make_async_remote_copy(
                src_ref=rows(p, *block) if src is None else src, dst_ref=rows(p, *block),
                send_sem=send_sems.at[6 * p + k], recv_sem=recv_sems.at[6 * p + k], device_id=to, device_id_type=MESH)

        first = [copy(p, j, (x, y, c), (*chip, c), src=x_refs[p].at[_half_rows(bufs[p].shape[0], c), :])
                 for p in range(n) for j, chip in enumerate(chips)]
        for cp in first:
            cp.start()
        passed = []
        for p in range(n):
            for j, chip in enumerate(chips):
                copy(p, j, (*chip, c), (x, y, c)).wait_recv()
                passed.append(copy(p, 3 + j, (*chip, c), sibling))
                passed[-1].start()
        for p in range(n):
            for j, chip in enumerate(chips):
                copy(p, 3 + j, (*chip, 1 - c), (x, y, c)).wait_recv()
        for cp in first + passed:
            cp.wait_send()

    others = pl.pallas_call(
        body, name=name, in_specs=[ANY] * n, out_specs=[ANY] * n,
        out_shape=[jax.ShapeDtypeStruct((N_CHIPS,) + b.shape, b.dtype) for b in bufs],
        scratch_shapes=[pltpu.SemaphoreType.DMA((6 * n,)), pltpu.SemaphoreType.DMA((6 * n,))],
    )(*bufs)
    return [lax.dynamic_update_slice(o, b[None], (chip, 0, 0)) for o, b in zip(others, bufs)]


def _sibling_swap(bufs, name, *, other_half_of_axis1=False):
    n = len(bufs)
    out_shapes = [(b.shape[0], b.shape[1] // 2, b.shape[2]) if other_half_of_axis1 else b.shape for b in bufs]

    def body(*refs):
        x_refs, out_refs, (send_sems, recv_sems) = refs[:n], refs[n:2 * n], refs[2 * n:]
        x, y, c = _where_am_i()
        copies = []
        for p in range(n):
            src = x_refs[p].at[:, _half_rows(bufs[p].shape[1], 1 - c), :] if other_half_of_axis1 else x_refs[p]
            copies.append(pltpu.make_async_remote_copy(
                src_ref=src, dst_ref=out_refs[p], send_sem=send_sems.at[p], recv_sem=recv_sems.at[p],
                device_id=(x, y, 1 - c), device_id_type=MESH))
            copies[-1].start()
        for cp in copies:
            cp.wait()

    return pl.pallas_call(
        body, name=name, in_specs=[ANY] * n, out_specs=[ANY] * n,
        out_shape=[jax.ShapeDtypeStruct(s, b.dtype) for s, b in zip(out_shapes, bufs)],
        scratch_shapes=[pltpu.SemaphoreType.DMA((n,)), pltpu.SemaphoreType.DMA((n,))],
    )(*bufs)


def _chip_scatter(parts, chip, name):
    n = len(parts)

    def body(*refs):
        p_refs, out_refs, (send_sems, recv_sems) = refs[:n], refs[n:2 * n], refs[2 * n:]
        x, y, c = _where_am_i()
        me = 2 * x + y
        chips = _other_chips(x, y)
        sends = [pltpu.make_async_remote_copy(
            src_ref=p_refs[p].at[2 * px + py], dst_ref=out_refs[p].at[me], send_sem=send_sems.at[3 * p + k],
            recv_sem=recv_sems.at[3 * p + k], device_id=(px, py, c), device_id_type=MESH)
            for p in range(n) for k, (px, py) in enumerate(chips)]
        for cp in sends:
            cp.start()
        for p in range(n):
            for k, (px, py) in enumerate(chips):
                pltpu.make_async_remote_copy(
                    src_ref=p_refs[p].at[me], dst_ref=out_refs[p].at[2 * px + py], send_sem=send_sems.at[3 * p + k],
                    recv_sem=recv_sems.at[3 * p + k], device_id=(px, py, c), device_id_type=MESH).wait_recv()
        for cp in sends:
            cp.wait_send()

    others = pl.pallas_call(
        body, name=name, in_specs=[ANY] * n, out_specs=[ANY] * n,
        out_shape=[jax.ShapeDtypeStruct(p.shape, p.dtype) for p in parts],
        scratch_shapes=[pltpu.SemaphoreType.DMA((3 * n,)), pltpu.SemaphoreType.DMA((3 * n,))],
    )(*parts)
    return [lax.dynamic_update_slice(o, lax.dynamic_slice_in_dim(p, chip, 1, axis=0), (chip, 0, 0))
            for o, p in zip(others, parts)]


def _row_tile(rows, row_bytes, n_bufs, budget=24 * 1024 * 1024):
    best = None
    for t in range(16, rows + 1, 16):
        if rows % t == 0 and 2 * n_bufs * t * row_bytes <= budget:
            best = t
    assert best is not None, (rows, row_bytes)
    return best


def _add_halves(full, recv, core, out_dtype, name):
    n, h, cols = recv.shape
    tr = _row_tile(h, cols * 4, 3)
    nt = h // tr

    def body(core_ref, a_ref, b_ref, o_ref):
        o_ref[...] = (a_ref[...].astype(F32) + b_ref[...].astype(F32)).astype(out_dtype)

    blk = (None, tr, cols)
    return pl.pallas_call(
        body, name=name,
        grid_spec=pltpu.PrefetchScalarGridSpec(
            num_scalar_prefetch=1, grid=(n, nt),
            in_specs=[pl.BlockSpec(blk, lambda j, i, core_ref: (j, core_ref[0] * nt + i, 0)),
                      pl.BlockSpec(blk, lambda j, i, core_ref: (j, i, 0))],
            out_specs=pl.BlockSpec(blk, lambda j, i, core_ref: (j, i, 0))),
        out_shape=jax.ShapeDtypeStruct((n, h, cols), out_dtype),
        compiler_params=_params(("parallel", "parallel")),
    )(core.reshape(1), full, recv)


def _add2(a, b, name):
    r, cols = a.shape
    tr = _row_tile(r, cols * 4, 3)

    def body(a_ref, b_ref, o_ref):
        o_ref[...] = a_ref[...] + b_ref[...]

    blk = pl.BlockSpec((tr, cols), lambda i: (i, 0))
    return pl.pallas_call(body, name=name, grid=(r // tr,), in_specs=[blk, blk], out_specs=blk,
                          out_shape=jax.ShapeDtypeStruct(a.shape, a.dtype), compiler_params=_params(("parallel",)))(a, b)


def _sum_slots(parts, name):
    n, h, cols = parts.shape
    tr = _row_tile(h, cols * 4, n + 1)

    def body(p_ref, o_ref):
        acc = p_ref[0].astype(F32)
        for k in range(1, n):
            acc = acc + p_ref[k].astype(F32)
        o_ref[...] = acc

    return pl.pallas_call(
        body, name=name, grid=(h // tr,), in_specs=[pl.BlockSpec((n, tr, cols), lambda i: (0, i, 0))],
        out_specs=pl.BlockSpec((tr, cols), lambda i: (i, 0)), out_shape=jax.ShapeDtypeStruct((h, cols), F32),
        compiler_params=_params(("parallel",)),
    )(parts)


def _adamw(w, g, m, v, name):
    shape = w.shape
    cols = shape[-1]
    rows = math.prod(shape[:-1])
    tr = rows
    if rows % 8 == 0:
        tr = 8
        for t in range(8, rows + 1, 8):
            if rows % t == 0 and t * cols * 4 <= 1536 * 1024:
                tr = t
    c1 = 1.0 - ADAM_B1 ** ADAM_STEP
    c2 = 1.0 - ADAM_B2 ** ADAM_STEP

    def body(w_ref, g_ref, m_ref, v_ref, d_ref, nm_ref, nv_ref):
        gg = g_ref[...]
        nm = ADAM_B1 * m_ref[...] + (1.0 - ADAM_B1) * gg
        nv = ADAM_B2 * v_ref[...] + (1.0 - ADAM_B2) * (gg * gg)
        d_ref[...] = -ADAM_LR * ((nm / c1) / (jnp.sqrt(nv / c2) + ADAM_EPS) + ADAM_WD * w_ref[...])
        nm_ref[...] = nm
        nv_ref[...] = nv

    blk = pl.BlockSpec((tr, cols), lambda i: (i, 0))
    flat = jax.ShapeDtypeStruct((rows, cols), F32)
    outs = pl.pallas_call(
        body, name=name, grid=(rows // tr,), in_specs=[blk] * 4, out_specs=[blk] * 3, out_shape=[flat] * 3,
        compiler_params=_params(("parallel",)),
    )(*[a.reshape(rows, cols) for a in (w, g, m, v)])
    return [o.reshape(shape) for o in outs]


def _adamw_update(w, g, m, v):
    c1 = 1.0 - ADAM_B1 ** ADAM_STEP
    c2 = 1.0 - ADAM_B2 ** ADAM_STEP
    nm = ADAM_B1 * m + (1.0 - ADAM_B1) * g
    nv = ADAM_B2 * v + (1.0 - ADAM_B2) * (g * g)
    return -ADAM_LR * ((nm / c1) / (jnp.sqrt(nv / c2) + ADAM_EPS) + ADAM_WD * w), nm, nv


def _adamw_projection(w, m, v, mine, theirs, core, name):
    _, r, cols = w.shape
    h = r // 2
    tr = _row_tile(h, cols * 4, 12, budget=40 * 1024 * 1024)
    nt = h // tr

    def body(core_ref, w_ref, m_ref, v_ref, a0, b0, a1, b1, g_ref, d_ref, nm_ref, nv_ref):
        layer, half = pl.program_id(0), pl.program_id(1)
        own = half == core_ref[0]
        g = jnp.where(layer == 0, jnp.where(own, a0[...], b0[...]), jnp.where(own, a1[...], b1[...]))
        g_ref[...] = g
        d_ref[...], nm_ref[...], nv_ref[...] = _adamw_update(w_ref[...], g, m_ref[...], v_ref[...])

    def piece(layer, own):
        def index(l, hh, i, core_ref):
            used = (l == layer) & ((hh == core_ref[0]) == own)
            return (jnp.where(used, i, 0), 0)
        return pl.BlockSpec((tr, cols), index)

    whole = pl.BlockSpec((None, tr, cols), lambda l, hh, i, core_ref: (l, hh * nt + i, 0))
    return pl.pallas_call(
        body, name=name,
        grid_spec=pltpu.PrefetchScalarGridSpec(
            num_scalar_prefetch=1, grid=(2, 2, nt),
            in_specs=[whole] * 3 + [piece(0, True), piece(0, False), piece(1, True), piece(1, False)],
            out_specs=[whole] * 4),
        out_shape=[jax.ShapeDtypeStruct(w.shape, F32)] * 4,
        compiler_params=_params(("arbitrary", "arbitrary", "arbitrary")),
    )(core.reshape(1), w, m, v, mine[0], theirs[0], mine[1], theirs[1])


def _chip_shape(name):
    shape = list(FULL_SHAPES[name])
    axis = BIG_SHARD_AXIS.get(name, SMALL_SHARD_AXIS.get(name))
    if axis is not None:
        shape[axis] //= N_CHIPS
    return tuple(shape)


def _round_up(n, k):
    return -(-n // k) * k


SHARDED_SMALL_WORDS = sum(math.prod(_chip_shape(n)) for n in SMALL_SHARDED)
SHARDED_SMALL_ROWS = _round_up(-(-SHARDED_SMALL_WORDS // PACK_COLS), 32)
SMALL_WORDS = sum(math.prod(FULL_SHAPES[n]) for n in SMALL_NAMES)
SMALL_ROWS = _round_up(-(-SMALL_WORDS // PACK_COLS), 32)


def _pack_rows(flat, rows):
    return jnp.pad(flat, (0, rows * PACK_COLS - flat.shape[0])).reshape(rows, PACK_COLS)


def _pack_small_sharded(shards):
    return _pack_rows(jnp.concatenate([shards[n].reshape(-1) for n in SMALL_SHARDED]), SHARDED_SMALL_ROWS)


def _unpack_small_sharded(gathered_small):
    out = {}
    words = gathered_small.reshape(N_CHIPS, -1)
    w0 = 0
    for n in SMALL_SHARDED:
        size = math.prod(_chip_shape(n))
        blocks = words[:, w0:w0 + size].reshape((N_CHIPS,) + _chip_shape(n))
        out[n] = jnp.concatenate([blocks[j] for j in range(N_CHIPS)], axis=SMALL_SHARD_AXIS[n])
        w0 += size
    return out


def _pack_small_grads(grads):
    return _pack_rows(jnp.concatenate([grads[n].astype(F32).reshape(-1) for n in SMALL_NAMES]), SMALL_ROWS)


def _unpack_small_grads(total, chip):
    flat = total.reshape(-1)
    out = {}
    w0 = 0
    for n in SMALL_NAMES:
        size = math.prod(FULL_SHAPES[n])
        full = flat[w0:w0 + size].reshape(FULL_SHAPES[n])
        if n in SMALL_SHARD_AXIS:
            axis = SMALL_SHARD_AXIS[n]
            width = FULL_SHAPES[n][axis] // N_CHIPS
            full = lax.dynamic_slice_in_dim(full, chip * width, width, axis)
        out[n] = full
        w0 += size
    return out


def kernel(x, positions, even_w_in, even_a_ln_g, even_a_ln_b, even_a_ws, even_a_bs, even_b_sinks, even_w_out, even_ln_g, even_ln_b, odd_w_in, odd_conv_w, odd_conv_b, odd_w_a, odd_b_a, odd_w_x, odd_b_x, odd_lam, odd_w_pool, odd_d_scale, odd_w_out, odd_ln_g, odd_ln_b, loss_target, m_even_w_in, m_even_a_ln_g, m_even_a_ln_b, m_even_a_ws, m_even_a_bs, m_even_b_sinks, m_even_w_out, m_even_ln_g, m_even_ln_b, m_odd_w_in, m_odd_conv_w, m_odd_conv_b, m_odd_w_a, m_odd_b_a, m_odd_w_x, m_odd_b_x, m_odd_lam, m_odd_w_pool, m_odd_d_scale, m_odd_w_out, m_odd_ln_g, m_odd_ln_b, v_even_w_in, v_even_a_ln_g, v_even_a_ln_b, v_even_a_ws, v_even_a_bs, v_even_b_sinks, v_even_w_out, v_even_ln_g, v_even_ln_b, v_odd_w_in, v_odd_conv_w, v_odd_conv_b, v_odd_w_a, v_odd_b_a, v_odd_w_x, v_odd_b_x, v_odd_lam, v_odd_w_pool, v_odd_d_scale, v_odd_w_out, v_odd_ln_g, v_odd_ln_b):
    local = dict(zip(WEIGHT_NAMES, (even_w_in, even_a_ln_g, even_a_ln_b, even_a_ws, even_a_bs, even_b_sinks, even_w_out, even_ln_g, even_ln_b, odd_w_in, odd_conv_w, odd_conv_b, odd_w_a, odd_b_a, odd_w_x, odd_b_x, odd_lam, odd_w_pool, odd_d_scale, odd_w_out, odd_ln_g, odd_ln_b)))
    mom = dict(zip(WEIGHT_NAMES, (m_even_w_in, m_even_a_ln_g, m_even_a_ln_b, m_even_a_ws, m_even_a_bs, m_even_b_sinks, m_even_w_out, m_even_ln_g, m_even_ln_b, m_odd_w_in, m_odd_conv_w, m_odd_conv_b, m_odd_w_a, m_odd_b_a, m_odd_w_x, m_odd_b_x, m_odd_lam, m_odd_w_pool, m_odd_d_scale, m_odd_w_out, m_odd_ln_g, m_odd_ln_b)))
    vel = dict(zip(WEIGHT_NAMES, (v_even_w_in, v_even_a_ln_g, v_even_a_ln_b, v_even_a_ws, v_even_a_bs, v_even_b_sinks, v_even_w_out, v_even_ln_g, v_even_ln_b, v_odd_w_in, v_odd_conv_w, v_odd_conv_b, v_odd_w_a, v_odd_b_a, v_odd_w_x, v_odd_b_x, v_odd_lam, v_odd_w_pool, v_odd_d_scale, v_odd_w_out, v_odd_ln_g, v_odd_ln_b)))
    mx, my, core = _where_am_i()
    chip = 2 * mx + my

    pieces = [(n, l) for l in range(2) for n in BIG_NAMES]
    gathered = _two_level_gather([local[n][l].astype(MXU_DTYPE) for n, l in pieces] + [_pack_small_sharded(local)],
                                 chip, "gather_weights")
    weights = dict(local)
    for n in BIG_NAMES:
        weights[n] = [gathered[pieces.index((n, l))] for l in range(2)]
    weights.update(_unpack_small_sharded(gathered[-1]))

    loss, grad_x, grads = _local_step(x[0], positions[0], loss_target[0], weights)

    partial = [grads[n][l] for n, l in pieces]
    from_sibling = _sibling_swap(partial, "big_presum_swap", other_half_of_axis1=True)
    chip_part = [_add_halves(g, r, core, MXU_DTYPE, "big_presum_add_%s_%d" % p)
                 for g, r, p in zip(partial, from_sibling, pieces)]
    arrived = _chip_scatter(chip_part, chip, "big_scatter")
    mine = [_sum_slots(a, "big_sum_%s_%d" % p) for a, p in zip(arrived, pieces)]
    theirs = _sibling_swap(mine, "big_join")

    small = _pack_small_grads(grads)
    chip_sum = _add2(small, _sibling_swap([small], "small_swap")[0], "small_presum_add")
    small_total = _sum_slots(_two_level_gather([chip_sum], chip, "small_gather")[0], "small_sum")
    reduced = _unpack_small_grads(small_total, chip)

    deltas, new_m, new_v = {}, {}, {}
    for n in SMALL_NAMES:
        deltas[n], new_m[n], new_v[n] = _adamw(local[n], reduced[n], mom[n], vel[n], "adamw_" + n)
    for n in BIG_NAMES:
        of_layer = lambda halves: [halves[pieces.index((n, l))] for l in range(2)]
        reduced[n], deltas[n], new_m[n], new_v[n] = _adamw_projection(
            local[n], mom[n], vel[n], of_layer(mine), of_layer(theirs), core, "adamw_" + n)

    total_loss = lax.psum(loss[0, 0], ("x", "y", "c"))
    return (total_loss, grad_x[None], *[reduced[n] for n in WEIGHT_NAMES], *[deltas[n] for n in WEIGHT_NAMES],
            *[new_m[n] for n in WEIGHT_NAMES], *[new_v[n] for n in WEIGHT_NAMES])
```

```python
import functools
import math

import jax
import jax.numpy as jnp
from jax import lax
from jax.experimental import pallas as pl
from jax.experimental.pallas import tpu as pltpu

F32 = jnp.float32
BF16 = jnp.bfloat16
MXU_DTYPE = jnp.bfloat16

D_MODEL = 2048
DEPTH = 4
A_WIDTH = 1024
A_GROUPS = 8
CHUNK = 128
B_HEAD_DIM = 64
B_Q_HEADS = 16
B_KV_HEADS = 2
WINDOW = 128
ROT_DIM = 16
ROPE_THETA = 500000.0
C_WIDTH = 1024
C_HEADS = 8
CONV_WIDTH = 4
LRU_C = 8.0
D_WIDTH = 1024
D_GROUPS = 4
D_GROUP_DIM = 256
EVEN_IN = 5376
ODD_IN = 4096
DN_ALPHA = (2 * DEPTH) ** 0.25
LN_EPS = 1e-5
ATTN_SCALE = B_HEAD_DIM ** -0.5
NEG_BIG = -1e30

ADAM_LR = 0.001
ADAM_B1 = 0.9
ADAM_B2 = 0.999
ADAM_EPS = 1e-08
ADAM_WD = 0.01
ADAM_STEP = 10

LANES = 128
VMEM_LIMIT = 56 * 1024 * 1024
N_CHIPS = 4
MESH = pl.DeviceIdType.MESH

WEIGHT_NAMES = ['even_w_in', 'even_a_ln_g', 'even_a_ln_b', 'even_a_ws', 'even_a_bs', 'even_b_sinks', 'even_w_out',
                'even_ln_g', 'even_ln_b', 'odd_w_in', 'odd_conv_w', 'odd_conv_b', 'odd_w_a', 'odd_b_a', 'odd_w_x',
                'odd_b_x', 'odd_lam', 'odd_w_pool', 'odd_d_scale', 'odd_w_out', 'odd_ln_g', 'odd_ln_b']
BIG_SHARD_AXIS = {'even_w_in': 2, 'even_w_out': 1, 'odd_w_in': 2, 'odd_w_out': 1}
SMALL_SHARD_AXIS = {'odd_conv_w': 2, 'odd_conv_b': 1, 'odd_b_a': 1, 'odd_b_x': 1, 'odd_lam': 1, 'odd_w_pool': 2,
                    'odd_d_scale': 1, 'odd_ln_g': 1, 'odd_ln_b': 1}
FULL_SHAPES = {
    'even_w_in': (2, 2048, 5376), 'even_a_ln_g': (2, 1024), 'even_a_ln_b': (2, 1024), 'even_a_ws': (2, 8, 128, 128),
    'even_a_bs': (2, 8, 128), 'even_b_sinks': (2, 16), 'even_w_out': (2, 2048, 2048), 'even_ln_g': (2, 2048),
    'even_ln_b': (2, 2048), 'odd_w_in': (2, 2048, 4096), 'odd_conv_w': (2, 4, 1024), 'odd_conv_b': (2, 1024),
    'odd_w_a': (2, 8, 128, 128), 'odd_b_a': (2, 1024), 'odd_w_x': (2, 8, 128, 128), 'odd_b_x': (2, 1024),
    'odd_lam': (2, 1024), 'odd_w_pool': (2, 4, 256, 256), 'odd_d_scale': (2, 1024), 'odd_w_out': (2, 2048, 2048),
    'odd_ln_g': (2, 2048), 'odd_ln_b': (2, 2048)}
BIG_NAMES = ['even_w_in', 'even_w_out', 'odd_w_in', 'odd_w_out']
SMALL_SHARDED = ['odd_conv_w', 'odd_conv_b', 'odd_b_a', 'odd_b_x', 'odd_lam', 'odd_d_scale', 'odd_ln_g', 'odd_ln_b',
                 'odd_w_pool']
SMALL_NAMES = [n for n in WEIGHT_NAMES if n not in BIG_NAMES]
W_IN_TRANSPOSED = {'even': True, 'odd': False}
PACK_COLS = 1024


def _params(sem):
    return pltpu.CompilerParams(dimension_semantics=sem, vmem_limit_bytes=VMEM_LIMIT)


def _dot(a, b):
    return lax.dot_general(a, b, (((1,), (0,)), ((), ())), preferred_element_type=F32)


def _dot_nt(a, b):
    return lax.dot_general(a, b, (((1,), (1,)), ((), ())), preferred_element_type=F32)


def _dot_tn(a, b):
    return lax.dot_general(a, b, (((0,), (0,)), ((), ())), preferred_element_type=F32)


def _sigmoid(x):
    return 1.0 / (1.0 + jnp.exp(-x))


def _silu_and_grad(x):
    s = _sigmoid(x)
    return x * s, s * (1.0 + x * (1.0 - s))


def _mm_call(body, name, grid, in_specs, out_spec, out_shape, args):
    return pl.pallas_call(body, name=name, grid=grid, in_specs=in_specs, out_specs=out_spec, out_shape=out_shape,
                          compiler_params=_params(("parallel",) * len(grid)))(*args)


def _mm_in(x, wg, transposed, name):
    s, k = x.shape
    nc = wg.shape[1] if transposed else wg.shape[2]
    tm = min(1024, s)

    def body(x_ref, w_ref, o_ref):
        o_ref[...] = (_dot_nt if transposed else _dot)(x_ref[...].astype(MXU_DTYPE), w_ref[...])

    return _mm_call(body, name, (s // tm, N_CHIPS),
                    [pl.BlockSpec((tm, k), lambda i, j: (i, 0)),
                     pl.BlockSpec((None,) + wg.shape[1:], lambda i, j: (j, 0, 0))],
                    pl.BlockSpec((None, tm, nc), lambda i, j: (j, i, 0)),
                    jax.ShapeDtypeStruct((N_CHIPS, s, nc), F32), (x, wg))


def _mm_dx(dh, wg, dz, transposed, name):
    s, d = dz.shape
    nc = dh.shape[2]
    tm, tn = min(512, s), 512

    def body(a_ref, w_ref, r_ref, o_ref):
        acc = DN_ALPHA * r_ref[...]
        for j in range(N_CHIPS):
            acc = acc + (_dot if transposed else _dot_nt)(a_ref[j], w_ref[j])
        o_ref[...] = acc

    w_spec = (pl.BlockSpec((N_CHIPS, nc, tn), lambda i, j: (0, 0, j)) if transposed
              else pl.BlockSpec((N_CHIPS, tn, nc), lambda i, j: (0, j, 0)))
    return _mm_call(body, name, (s // tm, d // tn),
                    [pl.BlockSpec((N_CHIPS, tm, nc), lambda i, j: (0, i, 0)), w_spec,
                     pl.BlockSpec((tm, tn), lambda i, j: (i, j))],
                    pl.BlockSpec((tm, tn), lambda i, j: (i, j)), jax.ShapeDtypeStruct((s, d), F32), (dh, wg, dz))


def _mm_dw_in(x, dh, transposed, name):
    s, d = x.shape
    nc = dh.shape[2]
    tm = 1024

    def body(x_ref, b_ref, o_ref):
        xb = x_ref[...].astype(MXU_DTYPE)
        acc = _dot_tn(b_ref[...], xb) if transposed else _dot_tn(xb, b_ref[...])
        o_ref[...] = acc.astype(o_ref.dtype)

    if transposed:
        out_spec, shape = pl.BlockSpec((None, nc, tm), lambda i, j: (j, 0, i)), (N_CHIPS, nc, d)
    else:
        out_spec, shape = pl.BlockSpec((None, tm, nc), lambda i, j: (j, i, 0)), (N_CHIPS, d, nc)
    return _mm_call(body, name, (d // tm, N_CHIPS),
                    [pl.BlockSpec((s, tm), lambda i, j: (0, i)), pl.BlockSpec((None, s, nc), lambda i, j: (j, 0, 0))],
                    out_spec, jax.ShapeDtypeStruct(shape, MXU_DTYPE), (x, dh))


def _mm_out(mix, wg, name):
    s, k = mix.shape
    _, kc, d = wg.shape
    tm, tn = min(1024, s), 1024

    def body(a_ref, w_ref, o_ref):
        o_ref[...] = _dot(a_ref[...], w_ref[...].reshape(N_CHIPS * kc, tn))

    return _mm_call(body, name, (s // tm, d // tn),
                    [pl.BlockSpec((tm, k), lambda i, j: (i, 0)), pl.BlockSpec((N_CHIPS, kc, tn), lambda i, j: (0, 0, j))],
                    pl.BlockSpec((tm, tn), lambda i, j: (i, j)), jax.ShapeDtypeStruct((s, d), F32), (mix, wg))


def _mm_dmix(dz, wg, name):
    s, d = dz.shape
    kc = wg.shape[1]
    tm = min(1024, s)

    def body(a_ref, w_ref, o_ref):
        o_ref[...] = _dot_nt(a_ref[...], w_ref[...])

    return _mm_call(body, name, (s // tm, N_CHIPS),
                    [pl.BlockSpec((tm, d), lambda i, j: (i, 0)), pl.BlockSpec((None, kc, d), lambda i, j: (j, 0, 0))],
                    pl.BlockSpec((tm, kc), lambda i, j: (i, j)), jax.ShapeDtypeStruct((s, N_CHIPS * kc), F32), (dz, wg))


def _mm_dw_out(mix, dz, name):
    s, k = mix.shape
    d = dz.shape[1]
    kc, tn = k // N_CHIPS, 1024

    def body(a_ref, b_ref, o_ref):
        o_ref[...] = _dot_tn(a_ref[...], b_ref[...]).astype(o_ref.dtype)

    return _mm_call(body, name, (N_CHIPS, d // tn),
                    [pl.BlockSpec((s, kc), lambda i, j: (0, i)), pl.BlockSpec((s, tn), lambda i, j: (0, j))],
                    pl.BlockSpec((None, kc, tn), lambda i, j: (i, 0, j)),
                    jax.ShapeDtypeStruct((N_CHIPS, kc, d), MXU_DTYPE), (mix, dz))


def _ln_fwd(x, y, g, b):
    s, d = x.shape
    tr = min(256, s)

    def body(x_ref, y_ref, g_ref, b_ref, z_ref, o_ref, ob_ref):
        z = DN_ALPHA * x_ref[...] + y_ref[...]
        mu = jnp.mean(z, axis=1, keepdims=True)
        zc = z - mu
        var = jnp.mean(zc * zc, axis=1, keepdims=True)
        o = zc * lax.rsqrt(var + LN_EPS) * g_ref[...] + b_ref[...]
        z_ref[...] = z
        o_ref[...] = o
        ob_ref[...] = o.astype(MXU_DTYPE)

    row = pl.BlockSpec((tr, d), lambda i: (i, 0))
    vec = pl.BlockSpec((1, d), lambda i: (0, 0))
    return pl.pallas_call(
        body, name="ln_fwd", grid=(s // tr,), in_specs=[row, row, vec, vec], out_specs=[row, row, row],
        out_shape=[jax.ShapeDtypeStruct((s, d), F32), jax.ShapeDtypeStruct((s, d), F32),
                   jax.ShapeDtypeStruct((s, d), MXU_DTYPE)],
        compiler_params=_params(("parallel",)),
    )(x, y, g.reshape(1, d), b.reshape(1, d))


def _ln_bwd(dout, z, g, after):
    s, d = z.shape
    tr = min(256, s)

    def body(do_ref, z_ref, g_ref, after_ref, dz_ref, dzb_ref, dg_ref, db_ref):
        @pl.when(pl.program_id(0) == 0)
        def _():
            dg_ref[...] = jnp.zeros_like(dg_ref)
            db_ref[...] = jnp.zeros_like(db_ref)

        zz = z_ref[...]
        do = do_ref[...]
        mu = jnp.mean(zz, axis=1, keepdims=True)
        zc = zz - mu
        var = jnp.mean(zc * zc, axis=1, keepdims=True)
        rstd = lax.rsqrt(var + LN_EPS)
        zh = zc * rstd
        dg_ref[...] += jnp.sum(do * zh, axis=0, keepdims=True)
        db_ref[...] += jnp.sum(do, axis=0, keepdims=True)
        dzh = do * g_ref[...]
        dz = rstd * (dzh - jnp.mean(dzh, axis=1, keepdims=True) - zh * jnp.mean(dzh * zh, axis=1, keepdims=True))
        dz_ref[...] = dz
        dzb_ref[...] = dz.astype(MXU_DTYPE)

    row = pl.BlockSpec((tr, d), lambda i: (i, 0))
    vec = pl.BlockSpec((1, d), lambda i: (0, 0))
    return pl.pallas_call(
        body, name="ln_bwd", grid=(s // tr,), in_specs=[row, row, vec, ANY], out_specs=[row, row, vec, vec],
        out_shape=[jax.ShapeDtypeStruct((s, d), F32), jax.ShapeDtypeStruct((s, d), MXU_DTYPE),
                   jax.ShapeDtypeStruct((1, d), F32), jax.ShapeDtypeStruct((1, d), F32)],
        compiler_params=_params(("arbitrary",)),
    )(dout, z, g.reshape(1, d), after)


def _loss_and_grad(out, tgt):
    s, d = out.shape
    tr = min(256, s)

    def body(o_ref, t_ref, l_ref, do_ref):
        @pl.when(pl.program_id(0) == 0)
        def _():
            l_ref[...] = jnp.zeros_like(l_ref)

        e = o_ref[...] - t_ref[...]
        do_ref[...] = e * (1.0 / d)
        rows = jnp.sum(e * e, axis=1, keepdims=True)
        l_ref[...] += (0.5 / d) * jnp.sum(rows, axis=0, keepdims=True)

    row = pl.BlockSpec((tr, d), lambda i: (i, 0))
    return pl.pallas_call(
        body, name="loss", grid=(s // tr,), in_specs=[row, row],
        out_specs=[pl.BlockSpec((1, 1), lambda i: (0, 0)), row],
        out_shape=[jax.ShapeDtypeStruct((1, 1), F32), jax.ShapeDtypeStruct((s, d), F32)],
        compiler_params=_params(("arbitrary",)),
    )(out, tgt)


Q0, K0, V0, BG0 = 3072, 4096, 4224, 4352
T = CHUNK


def _lane_ids(width):
    return lax.broadcasted_iota(jnp.int32, (T, width), 1)


def _rope_swap(x):
    w = x.shape[1]
    l64 = _lane_ids(w) % B_HEAD_DIM
    half = ROT_DIM // 2
    return jnp.where(l64 < half, pltpu.roll(x, w - half, 1), jnp.where(l64 < ROT_DIM, pltpu.roll(x, half, 1), 0.0))


def _rope(x, c, s):
    return x * c + _rope_swap(x) * s


def _rope_bwd(d, c, s):
    return d * c + _rope_swap(d * s)


def _tril():
    return lax.broadcasted_iota(jnp.int32, (T, T), 0) >= lax.broadcasted_iota(jnp.int32, (T, T), 1)


def _band_mask(has_prev):
    qi = lax.broadcasted_iota(jnp.int32, (T, 2 * T), 0)
    kj = lax.broadcasted_iota(jnp.int32, (T, 2 * T), 1)
    return (kj > qi) & (kj <= qi + WINDOW) & ((kj >= WINDOW) | has_prev)


def _a_norm(v, g, b):
    mu = jnp.mean(v, axis=1, keepdims=True)
    vc = v - mu
    var = jnp.mean(vc * vc, axis=1, keepdims=True)
    rstd = lax.rsqrt(var + LN_EPS)
    vh = vc * rstd
    return vh, rstd, vh * g + b


def _kv_operands(hk, k_rot, kp_rot, vv, vp):
    lo = _lane_ids(LANES) < B_HEAD_DIM
    mine = lo if hk == 0 else jnp.logical_not(lo)
    kc = jnp.where(mine, k_rot, 0.0)
    kp = jnp.where(mine, kp_rot, 0.0)
    k2 = jnp.concatenate([kp + pltpu.roll(kp, B_HEAD_DIM, 1), kc + pltpu.roll(kc, B_HEAD_DIM, 1)], axis=0)
    vm = jnp.concatenate([jnp.where(mine, vp, 0.0), jnp.where(mine, vv, 0.0)], axis=0)
    vs = pltpu.roll(vm, B_HEAD_DIM, 1)
    v_lo, v_hi = (vm, vs) if hk == 0 else (vs, vm)
    return k2.astype(MXU_DTYPE), v_lo.astype(MXU_DTYPE), v_hi.astype(MXU_DTYPE), (vm + vs).astype(MXU_DTYPE)


def _softmax_with_sink(qm, k2, valid, sink):
    s = _dot_nt(qm, k2) * ATTN_SCALE
    s = jnp.where(valid, s, NEG_BIG)
    m = jnp.maximum(jnp.max(s, axis=1, keepdims=True), sink)
    ex = jnp.exp(s - m)
    es = jnp.exp(sink - m)
    inv = 1.0 / (jnp.sum(ex, axis=1, keepdims=True) + es)
    return ex * inv, es * inv


EVEN_SHARD = EVEN_IN // N_CHIPS
KV_CHIP = K0 // EVEN_SHARD
KV_SPAN = 3 * LANES
KP0, VP0 = K0 - KV_CHIP * EVEN_SHARD, V0 - KV_CHIP * EVEN_SHARD
assert V0 // EVEN_SHARD == KV_CHIP and VP0 + LANES <= KV_SPAN


def _even_specs(s):
    nb = s // T
    return nb, {
        'h': lambda r: pl.BlockSpec((N_CHIPS, T, EVEN_SHARD), lambda i: (0, r(i), 0)),
        'kvprev': lambda r: pl.BlockSpec((None, T, KV_SPAN), lambda i: (KV_CHIP, jnp.maximum(r(i) - 1, 0), 0)),
        'tab': lambda r: pl.BlockSpec((T, A_WIDTH), lambda i: (r(i), 0)),
        'tabprev': lambda r: pl.BlockSpec((T, LANES), lambda i: (jnp.maximum(r(i) - 1, 0), 0)),
    }


def _rows_from_shards(h4_ref, rows_ref):
    for j in range(N_CHIPS):
        rows_ref[:, j * EVEN_SHARD:(j + 1) * EVEN_SHARD] = h4_ref[j]


def _full(shape):
    nd = len(shape)
    return pl.BlockSpec(shape, lambda i: (0,) * nd)


def _even_fwd(h4, cq, sq, ln_g, ln_b, ws, bs_t, sinks):
    s = h4.shape[1]
    nb, sp = _even_specs(s)

    def body(h4_ref, kvp_ref, cq_ref, sq_ref, ckp_ref, skp_ref, g_ref, b_ref, ws_ref, bst_ref, sink_ref, mix_ref, h_ref):
        i = pl.program_id(0)
        _rows_from_shards(h4_ref, h_ref)
        tril = _tril()
        _, _, vn = _a_norm(h_ref[:, 1024:2048], g_ref[...], b_ref[...])
        vnb = vn.astype(MXU_DTYPE)
        for g in range(A_GROUPS):
            c0, c1 = g * LANES, (g + 1) * LANES
            wg = jnp.where(tril, ws_ref[g], 0.0).astype(MXU_DTYPE)
            mixed = _dot(wg, vnb[:, c0:c1]) + bst_ref[:, g:g + 1]
            sil, _ = _silu_and_grad(h_ref[:, 2048 + c0:2048 + c1])
            mix_ref[:, c0:c1] = (h_ref[:, c0:c1] * mixed * sil).astype(mix_ref.dtype)

        k_rot = _rope(h_ref[:, K0:K0 + LANES], cq_ref[:, 0:LANES], sq_ref[:, 0:LANES])
        kp_rot = _rope(kvp_ref[:, KP0:KP0 + LANES], ckp_ref[...], skp_ref[...])
        vp = kvp_ref[:, VP0:VP0 + LANES]
        valid = _band_mask(i > 0)
        lo = _lane_ids(LANES) < B_HEAD_DIM
        for hk in range(B_KV_HEADS):
            k2, v_lo, v_hi, _ = _kv_operands(hk, k_rot, kp_rot, h_ref[:, V0:V0 + LANES], vp)
            for pp in range(4):
                pair = hk * 4 + pp
                c0, c1 = pair * LANES, (pair + 1) * LANES
                qp = _rope(h_ref[:, Q0 + c0:Q0 + c1], cq_ref[:, c0:c1], sq_ref[:, c0:c1])
                acc = jnp.zeros((T, LANES), F32)
                for e in range(2):
                    head = 2 * pair + e
                    qm = jnp.where(lo if e == 0 else jnp.logical_not(lo), qp, 0.0).astype(MXU_DTYPE)
                    p, _ = _softmax_with_sink(qm, k2, valid, sink_ref[:, head:head + 1])
                    acc = acc + _dot(p.astype(MXU_DTYPE), v_lo if e == 0 else v_hi)
                sil, _ = _silu_and_grad(h_ref[:, BG0 + c0:BG0 + c1])
                mix_ref[:, A_WIDTH + c0:A_WIDTH + c1] = (acc * sil).astype(mix_ref.dtype)

    ident = lambda i: i
    return pl.pallas_call(
        body, name="even_fwd", grid=(nb,),
        in_specs=[sp['h'](ident), sp['kvprev'](ident), sp['tab'](ident), sp['tab'](ident),
                  sp['tabprev'](ident), sp['tabprev'](ident), _full((1, A_WIDTH)), _full((1, A_WIDTH)),
                  _full((A_GROUPS, T, T)), _full((T, A_GROUPS)), _full((1, B_Q_HEADS))],
        out_specs=pl.BlockSpec((T, 2 * A_WIDTH), lambda i: (i, 0)),
        out_shape=jax.ShapeDtypeStruct((s, 2 * A_WIDTH), MXU_DTYPE),
        scratch_shapes=[pltpu.VMEM((T, EVEN_IN), F32)],
        compiler_params=_params(("parallel",)),
    )(h4, h4, cq, sq, cq, sq, ln_g.reshape(1, -1), ln_b.reshape(1, -1), ws, bs_t, sinks.reshape(1, -1))


def _even_bwd(h4, dmix, cq, sq, ln_g, ln_b, ws, bs_t, sinks):
    s = h4.shape[1]
    nb, sp = _even_specs(s)

    def body(h4_ref, kvp_ref, cq_ref, sq_ref, ckp_ref, skp_ref, dm_ref, g_ref, b_ref, ws_ref, bst_ref, sink_ref,
             dh4_ref, dg_ref, db_ref, dws_ref, dbst_ref, dsink_ref, h_ref, dh_ref, dvn_scr, ck_scr, cv_scr):
        i = pl.program_id(0)

        @pl.when(i == 0)
        def _():
            for ref in (dg_ref, db_ref, dws_ref, dbst_ref, dsink_ref, ck_scr, cv_scr):
                ref[...] = jnp.zeros_like(ref)

        _rows_from_shards(h4_ref, h_ref)

        tril = _tril()
        vh, rstd, vn = _a_norm(h_ref[:, 1024:2048], g_ref[...], b_ref[...])
        vnb = vn.astype(MXU_DTYPE)
        for g in range(A_GROUPS):
            c0, c1 = g * LANES, (g + 1) * LANES
            wg = jnp.where(tril, ws_ref[g], 0.0).astype(MXU_DTYPE)
            mixed = _dot(wg, vnb[:, c0:c1]) + bst_ref[:, g:g + 1]
            ag = h_ref[:, 2048 + c0:2048 + c1]
            u = h_ref[:, c0:c1]
            sil, dsil = _silu_and_grad(ag)
            da = dm_ref[:, c0:c1]
            dh_ref[:, c0:c1] = (da * mixed * sil).astype(dh_ref.dtype)
            dh_ref[:, 2048 + c0:2048 + c1] = (da * u * mixed * dsil).astype(dh_ref.dtype)
            dmx = da * u * sil
            dmb = dmx.astype(MXU_DTYPE)
            dvn_scr[:, c0:c1] = _dot_tn(wg, dmb)
            dws_ref[g] += jnp.where(tril, _dot_nt(dmb, vnb[:, c0:c1]), 0.0)
            dbst_ref[:, g:g + 1] += jnp.sum(dmx, axis=1, keepdims=True)
        dvn = dvn_scr[...]
        dg_ref[...] += jnp.sum(dvn * vh, axis=0, keepdims=True)
        db_ref[...] += jnp.sum(dvn, axis=0, keepdims=True)
        dvh = dvn * g_ref[...]
        dv = rstd * (dvh - jnp.mean(dvh, axis=1, keepdims=True) - vh * jnp.mean(dvh * vh, axis=1, keepdims=True))
        dh_ref[:, 1024:2048] = dv.astype(dh_ref.dtype)

        ck_cur, sk_cur = cq_ref[:, 0:LANES], sq_ref[:, 0:LANES]
        k_rot = _rope(h_ref[:, K0:K0 + LANES], ck_cur, sk_cur)
        kp_rot = _rope(kvp_ref[:, KP0:KP0 + LANES], ckp_ref[...], skp_ref[...])
        vp = kvp_ref[:, VP0:VP0 + LANES]
        valid = _band_mask(i < nb - 1)
        lo = _lane_ids(LANES) < B_HEAD_DIM
        hi = jnp.logical_not(lo)
        dk_all = jnp.zeros((2 * T, LANES), F32)
        dv_all = jnp.zeros((2 * T, LANES), F32)
        for hk in range(B_KV_HEADS):
            k2, v_lo, v_hi, v2 = _kv_operands(hk, k_rot, kp_rot, h_ref[:, V0:V0 + LANES], vp)
            acc_k = jnp.zeros((2 * T, LANES), F32)
            acc_v = jnp.zeros((2 * T, LANES), F32)
            for pp in range(4):
                pair = hk * 4 + pp
                c0, c1 = pair * LANES, (pair + 1) * LANES
                cqp, sqp = cq_ref[:, c0:c1], sq_ref[:, c0:c1]
                qp = _rope(h_ref[:, Q0 + c0:Q0 + c1], cqp, sqp)
                qms, ps, pss = [], [], []
                ob = jnp.zeros((T, LANES), F32)
                for e in range(2):
                    head = 2 * pair + e
                    qm = jnp.where(lo if e == 0 else hi, qp, 0.0).astype(MXU_DTYPE)
                    p, psink = _softmax_with_sink(qm, k2, valid, sink_ref[:, head:head + 1])
                    ob = ob + _dot(p.astype(MXU_DTYPE), v_lo if e == 0 else v_hi)
                    qms.append(qm)
                    ps.append(p)
                    pss.append(psink)
                sil, dsil = _silu_and_grad(h_ref[:, BG0 + c0:BG0 + c1])
                dbo = dm_ref[:, A_WIDTH + c0:A_WIDTH + c1]
                dh_ref[:, BG0 + c0:BG0 + c1] = (dbo * ob * dsil).astype(dh_ref.dtype)
                dob = dbo * sil
                dq_rot = jnp.zeros((T, LANES), F32)
                for e in range(2):
                    head = 2 * pair + e
                    sel = lo if e == 0 else hi
                    dobm = jnp.where(sel, dob, 0.0).astype(MXU_DTYPE)
                    dp = _dot_nt(dobm, v2)
                    rs = jnp.sum(ps[e] * dp, axis=1, keepdims=True)
                    ds = ps[e] * (dp - rs) * ATTN_SCALE
                    dsink_ref[:, head:head + 1] += -jnp.sum(pss[e] * rs, axis=0, keepdims=True)
                    dsb = ds.astype(MXU_DTYPE)
                    dq_rot = dq_rot + jnp.where(sel, _dot(dsb, k2), 0.0)
                    acc_k = acc_k + _dot_tn(dsb, qms[e])
                    acc_v = acc_v + _dot_tn(ps[e].astype(MXU_DTYPE), dobm)
                dh_ref[:, Q0 + c0:Q0 + c1] = _rope_bwd(dq_rot, cqp, sqp).astype(dh_ref.dtype)
            lo2 = lax.broadcasted_iota(jnp.int32, (2 * T, LANES), 1) < B_HEAD_DIM
            mine = lo2 if hk == 0 else jnp.logical_not(lo2)
            dk_all = dk_all + jnp.where(mine, acc_k + pltpu.roll(acc_k, B_HEAD_DIM, 1), 0.0)
            dv_all = dv_all + jnp.where(mine, acc_v + pltpu.roll(acc_v, B_HEAD_DIM, 1), 0.0)
        dk_rot = dk_all[T:2 * T] + ck_scr[...]
        dvv = dv_all[T:2 * T] + cv_scr[...]
        ck_scr[...] = dk_all[0:T]
        cv_scr[...] = dv_all[0:T]
        dh_ref[:, K0:K0 + LANES] = _rope_bwd(dk_rot, ck_cur, sk_cur).astype(dh_ref.dtype)
        dh_ref[:, V0:V0 + LANES] = dvv.astype(dh_ref.dtype)
        for j in range(N_CHIPS):
            dh4_ref[j] = dh_ref[:, j * EVEN_SHARD:(j + 1) * EVEN_SHARD].astype(dh4_ref.dtype)

    rev = lambda i: nb - 1 - i
    return pl.pallas_call(
        body, name="even_bwd", grid=(nb,),
        in_specs=[sp['h'](rev), sp['kvprev'](rev), sp['tab'](rev), sp['tab'](rev),
                  sp['tabprev'](rev), sp['tabprev'](rev), pl.BlockSpec((T, 2 * A_WIDTH), lambda i: (rev(i), 0)),
                  _full((1, A_WIDTH)), _full((1, A_WIDTH)), _full((A_GROUPS, T, T)), _full((T, A_GROUPS)),
                  _full((1, B_Q_HEADS))],
        out_specs=[sp['h'](rev), _full((1, A_WIDTH)), _full((1, A_WIDTH)),
                   _full((A_GROUPS, T, T)), _full((T, A_GROUPS)), _full((1, B_Q_HEADS))],
        out_shape=[jax.ShapeDtypeStruct((N_CHIPS, s, EVEN_SHARD), MXU_DTYPE), jax.ShapeDtypeStruct((1, A_WIDTH), F32),
                   jax.ShapeDtypeStruct((1, A_WIDTH), F32), jax.ShapeDtypeStruct((A_GROUPS, T, T), F32),
                   jax.ShapeDtypeStruct((T, A_GROUPS), F32), jax.ShapeDtypeStruct((1, B_Q_HEADS), F32)],
        scratch_shapes=[pltpu.VMEM((T, EVEN_IN), F32), pltpu.VMEM((T, EVEN_IN), F32), pltpu.VMEM((T, A_WIDTH), F32),
                        pltpu.VMEM((T, LANES), F32), pltpu.VMEM((T, LANES), F32)],
        compiler_params=_params(("arbitrary",)),
    )(h4, h4, cq, sq, cq, sq, dmix, ln_g.reshape(1, -1), ln_b.reshape(1, -1), ws, bs_t, sinks.reshape(1, -1))


def _shift_down(x, d, fill=0.0):
    rows = lax.broadcasted_iota(jnp.int32, x.shape, 0)
    return jnp.where(rows >= d, pltpu.roll(x, d, 0), fill)


def _shift_up(x, d, fill=0.0):
    n = x.shape[0]
    rows = lax.broadcasted_iota(jnp.int32, x.shape, 0)
    return jnp.where(rows < n - d, pltpu.roll(x, n - d, 0), fill)


def _scan(a, b, shift):
    n = a.shape[0]
    d = 1
    while d < n:
        b = a * shift(b, d) + b
        if 2 * d < n:
            a = a * shift(a, d, 1.0)
        d *= 2
    return b


def _neg_expm1(y):
    u = jnp.exp(y)
    um1 = u - 1.0
    safe = jnp.where(um1 == 0.0, 1.0, jnp.log(u))
    near = jnp.where(um1 == 0.0, y, um1 * y / safe)
    return -jnp.where(y > -0.5, near, um1)


def _softplus(x):
    return jnp.maximum(x, 0.0) + jnp.log1p(jnp.exp(-jnp.abs(x)))


def _lru_forward(xc, cw, cb, wa, ba, wx, bx, lam):
    shifted = [_shift_down(xc, CONV_WIDTH - 1 - j) if j < CONV_WIDTH - 1 else xc for j in range(CONV_WIDTH)]
    xconv = cb
    for j in range(CONV_WIDTH):
        xconv = xconv + cw[j:j + 1, :] * shifted[j]
    xb = xconv.astype(MXU_DTYPE)
    r = _sigmoid(_dot(xb, wa.astype(MXU_DTYPE)) + ba)
    ig = _sigmoid(_dot(xb, wx.astype(MXU_DTYPE)) + bx)
    sp = _softplus(-lam)
    log_a = -LRU_C * r * sp
    a = jnp.exp(log_a)
    mult = jnp.sqrt(_neg_expm1(2.0 * log_a))
    hs = _scan(a, mult * ig * xconv, _shift_down)
    return dict(shifted=shifted, xconv=xconv, xb=xb, r=r, ig=ig, sp=sp, a=a, mult=mult, hs=hs)


def _col(width, off):
    return lambda s: pl.BlockSpec((s, width), lambda i: (0, off + i))


def _part(width, slot):
    return lambda s: pl.BlockSpec((None, s, width), lambda i: (slot, 0, i))


def _lru_specs(s):
    w = LANES
    return [_part(w, 0)(s), _part(w, 1)(s), pl.BlockSpec((CONV_WIDTH, w), lambda i: (0, i)),
            pl.BlockSpec((1, w), lambda i: (0, i)), pl.BlockSpec((None, w, w), lambda i: (i, 0, 0)),
            pl.BlockSpec((1, w), lambda i: (0, i)), pl.BlockSpec((None, w, w), lambda i: (i, 0, 0)),
            pl.BlockSpec((1, w), lambda i: (0, i)), pl.BlockSpec((1, w), lambda i: (0, i))]


def _lru_fwd(h, cw, cb, wa, ba, wx, bx, lam):
    s = h.shape[1]

    def body(xc_ref, cg_ref, cw_ref, cb_ref, wa_ref, ba_ref, wx_ref, bx_ref, lam_ref, o_ref):
        f = _lru_forward(xc_ref[...], cw_ref[...], cb_ref[...], wa_ref[...], ba_ref[...], wx_ref[...], bx_ref[...],
                         lam_ref[...])
        sil, _ = _silu_and_grad(cg_ref[...])
        o_ref[...] = (f['hs'] * sil).astype(o_ref.dtype)

    return pl.pallas_call(
        body, name="lru_fwd", grid=(C_HEADS,), in_specs=_lru_specs(s), out_specs=_col(LANES, 0)(s),
        out_shape=jax.ShapeDtypeStruct((s, C_WIDTH + D_WIDTH), MXU_DTYPE), compiler_params=_params(("parallel",)),
    )(h, h, cw, cb.reshape(1, -1), wa, ba.reshape(1, -1), wx, bx.reshape(1, -1), lam.reshape(1, -1))


def _lru_bwd(h, dmix, cw, cb, wa, ba, wx, bx, lam):
    s = h.shape[1]

    def body(xc_ref, cg_ref, cw_ref, cb_ref, wa_ref, ba_ref, wx_ref, bx_ref, lam_ref, dco_ref,
             dh_ref, dcw_ref, dcb_ref, dwa_ref, dba_ref, dwx_ref, dbx_ref, dlam_ref):
        dxc_ref, dcg_ref = dh_ref.at[0], dh_ref.at[1]
        cw, lam = cw_ref[...], lam_ref[...]
        wab, wxb = wa_ref[...].astype(MXU_DTYPE), wx_ref[...].astype(MXU_DTYPE)
        f = _lru_forward(xc_ref[...], cw, cb_ref[...], wa_ref[...], ba_ref[...], wx_ref[...], bx_ref[...], lam)
        sil, dsil = _silu_and_grad(cg_ref[...])
        dco = dco_ref[...]
        hs, a, mult, ig, r, xconv = f['hs'], f['a'], f['mult'], f['ig'], f['r'], f['xconv']
        dcg_ref[...] = (dco * hs * dsil).astype(dcg_ref.dtype)
        lamb = _scan(_shift_up(a, 1), dco * sil, _shift_up)
        da = lamb * _shift_down(hs, 1)
        d_ig = lamb * mult * xconv
        d_mult = lamb * ig * xconv
        dxconv = lamb * mult * ig
        d_log_a = da * a - d_mult * (a * a) / mult
        dlam_ref[...] = jnp.sum(d_log_a * r, axis=0, keepdims=True) * LRU_C * _sigmoid(-lam)
        dpa = d_log_a * (-LRU_C * f['sp']) * r * (1.0 - r)
        dpx = d_ig * ig * (1.0 - ig)
        dba_ref[...] = jnp.sum(dpa, axis=0, keepdims=True)
        dbx_ref[...] = jnp.sum(dpx, axis=0, keepdims=True)
        dpab, dpxb = dpa.astype(MXU_DTYPE), dpx.astype(MXU_DTYPE)
        dwa_ref[...] = _dot_tn(f['xb'], dpab)
        dwx_ref[...] = _dot_tn(f['xb'], dpxb)
        dxconv = dxconv + _dot_nt(dpab, wab) + _dot_nt(dpxb, wxb)
        dcb_ref[...] = jnp.sum(dxconv, axis=0, keepdims=True)
        dxc = jnp.zeros_like(dxconv)
        for j in range(CONV_WIDTH):
            dcw_ref[j:j + 1, :] = jnp.sum(dxconv * f['shifted'][j], axis=0, keepdims=True)
            back = _shift_up(dxconv, CONV_WIDTH - 1 - j) if j < CONV_WIDTH - 1 else dxconv
            dxc = dxc + cw[j:j + 1, :] * back
        dxc_ref[...] = dxc.astype(dxc_ref.dtype)

    w = LANES
    vec = pl.BlockSpec((1, w), lambda i: (0, i))
    mat = pl.BlockSpec((None, w, w), lambda i: (i, 0, 0))
    vshape = jax.ShapeDtypeStruct((1, C_WIDTH), F32)
    mshape = jax.ShapeDtypeStruct((C_HEADS, w, w), F32)
    return pl.pallas_call(
        body, name="lru_bwd", grid=(C_HEADS,), in_specs=_lru_specs(s) + [_col(w, 0)(s)],
        out_specs=[pl.BlockSpec((2, s, w), lambda i: (0, 0, i)), pl.BlockSpec((CONV_WIDTH, w), lambda i: (0, i)),
                   vec, mat, vec, mat, vec, vec],
        out_shape=[jax.ShapeDtypeStruct((N_CHIPS, s, C_WIDTH), MXU_DTYPE),
                   jax.ShapeDtypeStruct((CONV_WIDTH, C_WIDTH), F32), vshape, mshape, vshape, mshape, vshape, vshape],
        compiler_params=_params(("parallel",)),
    )(h, h, cw, cb.reshape(1, -1), wa, ba.reshape(1, -1), wx, bx.reshape(1, -1), lam.reshape(1, -1), dmix)


def _pool_window(g):
    return jnp.left_shift(2, g).astype(F32)


def _select_window(g, sums):
    out = sums[-1]
    for k in range(len(sums) - 2, -1, -1):
        out = jnp.where(g == k, sums[k], out)
    return out


def _window_sums(x, shift):
    sums, cur = [], x
    for k in range(D_GROUPS):
        cur = cur + shift(cur, 2 ** k)
        sums.append(cur)
    return sums


def _pool_specs(s):
    w = D_GROUP_DIM
    return [_part(w, 2)(s), _part(w, 3)(s),
            pl.BlockSpec((None, w, w), lambda i: (i, 0, 0)), pl.BlockSpec((1, w), lambda i: (0, i))]


def _pool_forward(xd, g):
    rows = lax.broadcasted_iota(jnp.int32, xd.shape, 0).astype(F32) + 1.0
    cnt = jnp.minimum(rows, _pool_window(g))
    pooled = _select_window(g, _window_sums(xd, _shift_down)) / cnt - xd
    return pooled, cnt


def _pool_fwd(h, mix, w_pool, d_scale):
    s = h.shape[1]

    def body(xd_ref, dg_ref, wp_ref, sc_ref, mix_ref, o_ref):
        pooled, _ = _pool_forward(xd_ref[...], pl.program_id(0))
        mixed = _dot(pooled.astype(MXU_DTYPE), wp_ref[...].astype(MXU_DTYPE))
        sil, _ = _silu_and_grad(dg_ref[...])
        o_ref[...] = (mixed * sc_ref[...] * sil).astype(o_ref.dtype)

    return pl.pallas_call(
        body, name="pool_fwd", grid=(D_GROUPS,), in_specs=_pool_specs(s) + [ANY],
        out_specs=_col(D_GROUP_DIM, C_WIDTH // D_GROUP_DIM)(s), out_shape=jax.ShapeDtypeStruct(mix.shape, mix.dtype),
        input_output_aliases={4: 0}, compiler_params=_params(("parallel",)),
    )(h, h, w_pool, d_scale.reshape(1, -1), mix)


def _pool_bwd(h, dmix, dh, w_pool, d_scale):
    s = h.shape[1]
    w = D_GROUP_DIM

    def body(xd_ref, dg_ref, wp_ref, sc_ref, ddo_ref, dh_in_ref, dh_ref, dwp_ref, dsc_ref):
        dxd_ref, ddg_ref = dh_ref.at[0], dh_ref.at[1]
        g = pl.program_id(0)
        pooled, cnt = _pool_forward(xd_ref[...], g)
        pb = pooled.astype(MXU_DTYPE)
        wpb = wp_ref[...].astype(MXU_DTYPE)
        mixed = _dot(pb, wpb)
        sil, dsil = _silu_and_grad(dg_ref[...])
        ddo = ddo_ref[...]
        sc = sc_ref[...]
        ddg_ref[...] = (ddo * mixed * sc * dsil).astype(ddg_ref.dtype)
        dms = ddo * sil
        dsc_ref[...] = jnp.sum(dms * mixed, axis=0, keepdims=True)
        dmb = (dms * sc).astype(MXU_DTYPE)
        dwp_ref[...] = _dot_tn(pb, dmb)
        dpooled = _dot_nt(dmb, wpb)
        dxd = _select_window(g, _window_sums(dpooled / cnt, _shift_up)) - dpooled
        dxd_ref[...] = dxd.astype(dxd_ref.dtype)

    return pl.pallas_call(
        body, name="pool_bwd", grid=(D_GROUPS,), in_specs=_pool_specs(s) + [_col(w, C_WIDTH // w)(s), ANY],
        out_specs=[pl.BlockSpec((2, s, w), lambda i: (1, 0, i)), pl.BlockSpec((None, w, w), lambda i: (i, 0, 0)),
                   pl.BlockSpec((1, w), lambda i: (0, i))],
        out_shape=[jax.ShapeDtypeStruct(dh.shape, dh.dtype),
                   jax.ShapeDtypeStruct((D_GROUPS, w, w), F32), jax.ShapeDtypeStruct((1, D_WIDTH), F32)],
        input_output_aliases={5: 0}, compiler_params=_params(("parallel",)),
    )(h, h, w_pool, d_scale.reshape(1, -1), dmix, dh)


def _rope_tables(positions):
    s = positions.shape[0]
    inv_freq = ROPE_THETA ** (-jnp.arange(0, ROT_DIM, 2, dtype=F32) / ROT_DIM)
    ang = positions.astype(F32)[:, None] * inv_freq
    c, sn = jnp.cos(ang), jnp.sin(ang)
    rest = B_HEAD_DIM - ROT_DIM
    c64 = jnp.concatenate([c, c, jnp.ones((s, rest), F32)], axis=1)
    s64 = jnp.concatenate([-sn, sn, jnp.zeros((s, rest), F32)], axis=1)
    return jnp.tile(c64, (1, B_Q_HEADS)), jnp.tile(s64, (1, B_Q_HEADS))


def _local_step(x, positions, target, w, projections_of, projection_grads):
    cq, sq = _rope_tables(positions)
    saved = []
    xin, xin_b = x, x
    for layer in range(DEPTH):
        j = layer // 2
        kind = 'even' if layer % 2 == 0 else 'odd'
        w_in, w_out = projections_of(layer, xin_b)
        h = _mm_in(xin_b, w_in, W_IN_TRANSPOSED[kind], "mm_" + kind + "_in")
        if kind == 'even':
            mix = _even_fwd(h, cq, sq, w['even_a_ln_g'][j], w['even_a_ln_b'][j], w['even_a_ws'][j],
                            w['even_a_bs'][j].T, w['even_b_sinks'][j])
        else:
            mix = _lru_fwd(h, w['odd_conv_w'][j], w['odd_conv_b'][j], w['odd_w_a'][j], w['odd_b_a'][j],
                           w['odd_w_x'][j], w['odd_b_x'][j], w['odd_lam'][j])
            mix = _pool_fwd(h, mix, w['odd_w_pool'][j], w['odd_d_scale'][j])
        y = _mm_out(mix, w_out, "mm_out")
        z, out, out_b = _ln_fwd(xin, y, w[kind + '_ln_g'][j], w[kind + '_ln_b'][j])
        saved.append((xin_b, h, mix, z, w_in, w_out))
        xin, xin_b = out, out_b

    loss, dout = _loss_and_grad(xin, target)
    per_layer = {n: [None, None] for n in SMALL_NAMES}
    behind = loss
    for layer in reversed(range(DEPTH)):
        j = layer // 2
        kind = 'even' if layer % 2 == 0 else 'odd'
        xin_b, h, mix, z, w_in, w_out = saved[layer]
        dz, dz_b, dg, db = _ln_bwd(dout, z, w[kind + '_ln_g'][j], behind)
        per_layer[kind + '_ln_g'][j], per_layer[kind + '_ln_b'][j] = dg[0], db[0]
        dmix = _mm_dmix(dz_b, w_out, "mm_dmix")
        dw_out = _mm_dw_out(mix, dz_b, "mm_dw_out")
        if kind == 'even':
            dh, dlg, dlb, dws, dbst, dsink = _even_bwd(
                h, dmix, cq, sq, w['even_a_ln_g'][j], w['even_a_ln_b'][j], w['even_a_ws'][j], w['even_a_bs'][j].T,
                w['even_b_sinks'][j])
            for n, val in (('even_a_ln_g', dlg[0]), ('even_a_ln_b', dlb[0]), ('even_a_ws', dws),
                           ('even_a_bs', dbst.T), ('even_b_sinks', dsink[0])):
                per_layer[n][j] = val
        else:
            dh, dcw, dcb, dwa, dba, dwx, dbx, dlam = _lru_bwd(
                h, dmix, w['odd_conv_w'][j], w['odd_conv_b'][j], w['odd_w_a'][j], w['odd_b_a'][j], w['odd_w_x'][j],
                w['odd_b_x'][j], w['odd_lam'][j])
            dh, dwp, dsc = _pool_bwd(h, dmix, dh, w['odd_w_pool'][j], w['odd_d_scale'][j])
            for n, val in (('odd_conv_w', dcw), ('odd_conv_b', dcb[0]), ('odd_w_a', dwa), ('odd_b_a', dba[0]),
                           ('odd_w_x', dwx), ('odd_b_x', dbx[0]), ('odd_lam', dlam[0]), ('odd_w_pool', dwp),
                           ('odd_d_scale', dsc[0])):
                per_layer[n][j] = val
        dout = _mm_dx(dh, w_in, dz, W_IN_TRANSPOSED[kind], "mm_dx_" + kind)
        dw_in = _mm_dw_in(xin_b, dh, W_IN_TRANSPOSED[kind], "mm_dw_in_" + kind)
        behind = projection_grads(layer, dw_in, dw_out)
    return loss, dout, {n: jnp.stack(v) for n, v in per_layer.items()}


ANY = pl.BlockSpec(memory_space=pl.ANY)


def _where_am_i():
    return lax.axis_index("x"), lax.axis_index("y"), lax.axis_index("c")


def _other_chips(x, y):
    return [(1 - x, y), (x, 1 - y), (1 - x, 1 - y)]


def _half_rows(ref_or_array_rows, which):
    half = ref_or_array_rows // 2
    return pl.ds(pl.multiple_of(which * half, 16), half)


def _two_level_gather(bufs, chip, name):
    n = len(bufs)

    def body(*refs):
        x_refs, out_refs, (send_sems, recv_sems) = refs[:n], refs[n:2 * n], refs[2 * n:]
        x, y, c = _where_am_i()
        sibling = (x, y, 1 - c)
        chips = _other_chips(x, y)

        def rows(p, px, py, pc):
            return out_refs[p].at[2 * px + py, _half_rows(bufs[p].shape[0], pc), :]

        def copy(p, k, block, to, src=None):
            return pltpu.make_async_remote_copy(
                src_ref=rows(p, *block) if src is None else src, dst_ref=rows(p, *block),
                send_sem=send_sems.at[6 * p + k], recv_sem=recv_sems.at[6 * p + k], device_id=to, device_id_type=MESH)

        first = [copy(p, j, (x, y, c), (*chip, c), src=x_refs[p].at[_half_rows(bufs[p].shape[0], c), :])
                 for p in range(n) for j, chip in enumerate(chips)]
        for cp in first:
            cp.start()
        passed = []
        for p in range(n):
            for j, chip in enumerate(chips):
                copy(p, j, (*chip, c), (x, y, c)).wait_recv()
                passed.append(copy(p, 3 + j, (*chip, c), sibling))
                passed[-1].start()
        for p in range(n):
            for j, chip in enumerate(chips):
                copy(p, 3 + j, (*chip, 1 - c), (x, y, c)).wait_recv()
        for cp in first + passed:
            cp.wait_send()

    others = pl.pallas_call(
        body, name=name, in_specs=[ANY] * n, out_specs=[ANY] * n,
        out_shape=[jax.ShapeDtypeStruct((N_CHIPS,) + b.shape, b.dtype) for b in bufs],
        scratch_shapes=[pltpu.SemaphoreType.DMA((6 * n,)), pltpu.SemaphoreType.DMA((6 * n,))],
    )(*bufs)
    return [lax.dynamic_update_slice(o, b[None], (chip, 0, 0)) for o, b in zip(others, bufs)]


def _sibling_swap(bufs, name, *, other_half_of_axis1=False):
    n = len(bufs)
    out_shapes = [(b.shape[0], b.shape[1] // 2, b.shape[2]) if other_half_of_axis1 else b.shape for b in bufs]

    def body(*refs):
        x_refs, out_refs, (send_sems, recv_sems) = refs[:n], refs[n:2 * n], refs[2 * n:]
        x, y, c = _where_am_i()
        copies = []
        for p in range(n):
            src = x_refs[p].at[:, _half_rows(bufs[p].shape[1], 1 - c), :] if other_half_of_axis1 else x_refs[p]
            copies.append(pltpu.make_async_remote_copy(
                src_ref=src, dst_ref=out_refs[p], send_sem=send_sems.at[p], recv_sem=recv_sems.at[p],
                device_id=(x, y, 1 - c), device_id_type=MESH))
            copies[-1].start()
        for cp in copies:
            cp.wait()

    return pl.pallas_call(
        body, name=name, in_specs=[ANY] * n, out_specs=[ANY] * n,
        out_shape=[jax.ShapeDtypeStruct(s, b.dtype) for s, b in zip(out_shapes, bufs)],
        scratch_shapes=[pltpu.SemaphoreType.DMA((n,)), pltpu.SemaphoreType.DMA((n,))],
    )(*bufs)


HBM = pl.BlockSpec(memory_space=pltpu.HBM)
SEM = pl.BlockSpec(memory_space=pltpu.SEMAPHORE)
IN_FLIGHT = pltpu.CompilerParams(has_side_effects=pltpu.SideEffectType.DATAFLOW_SIDE_EFFECTING)


def _ici_copy(kind, src_ref, land_ref, send_sems, recv_sems, k, peer, x, y, c):
    px, py = peer
    me = 2 * x + y
    if kind == 'gather':
        half = _half_rows(src_ref.shape[0], c)
        src, dst = src_ref.at[half, :], land_ref.at[me, half, :]
    else:
        src, dst = src_ref.at[2 * px + py], land_ref.at[me]
    return pltpu.make_async_remote_copy(src_ref=src, dst_ref=dst, send_sem=send_sems.at[k], recv_sem=recv_sems.at[k],
                                        device_id=(px, py, c), device_id_type=MESH)


def _ici_start(kind, groups, name):
    flat = [a for g in groups for a in g]
    n, ng = len(flat), len(groups)
    land_shapes = [((N_CHIPS,) + a.shape) if kind == 'gather' else a.shape for a in flat]

    def body(*refs):
        srcs, lands, sems, token = refs[:n], refs[n:2 * n], refs[2 * n:2 * n + 2 * ng], refs[-1]
        x, y, c = _where_am_i()
        p = 0
        for gi, g in enumerate(groups):
            for q in range(len(g)):
                for k, peer in enumerate(_other_chips(x, y)):
                    _ici_copy(kind, srcs[p], lands[p], sems[2 * gi], sems[2 * gi + 1], 3 * q + k, peer, x, y, c).start()
                p += 1
        token[...] = jnp.zeros_like(token)

    sem_shapes = [pltpu.SemaphoreType.DMA((3 * len(g),)) for g in groups for _ in range(2)]
    outs = pl.pallas_call(
        body, name=name, in_specs=[HBM] * (2 * n),
        out_specs=[SEM] * (2 * ng) + [HBM] * (2 * n) + [pl.BlockSpec(memory_space=pltpu.VMEM)],
        out_shape=sem_shapes + [pltpu.HBM(a.shape, a.dtype) for a in flat]
        + [pltpu.HBM(s, a.dtype) for s, a in zip(land_shapes, flat)] + [jax.ShapeDtypeStruct((8, LANES), F32)],
        input_output_aliases={i: 2 * ng + i for i in range(2 * n)}, compiler_params=IN_FLIGHT,
    )(*[pltpu.with_memory_space_constraint(a, pltpu.HBM) for a in flat],
      *[pltpu.with_memory_space_constraint(lax.empty(s, a.dtype), pltpu.HBM) for s, a in zip(land_shapes, flat)])
    sems, srcs, lands, token = outs[:2 * ng], outs[2 * ng:2 * ng + n], outs[2 * ng + n:2 * ng + 2 * n], outs[-1]
    started, p = [], 0
    for gi, g in enumerate(groups):
        started.append((sems[2 * gi], sems[2 * gi + 1], srcs[p:p + len(g)], lands[p:p + len(g)]))
        p += len(g)
    return started, token


def _ici_wait(kind, started, after, name):
    send_sems, recv_sems, srcs, lands = started
    n = len(srcs)

    def body(*refs):
        src_refs, land_refs, send_ref, recv_ref = refs[:n], refs[n:2 * n], refs[2 * n], refs[2 * n + 1]
        x, y, c = _where_am_i()
        for q in range(n):
            for k, peer in enumerate(_other_chips(x, y)):
                cp = _ici_copy(kind, src_refs[q], land_refs[q], send_ref, recv_ref, 3 * q + k, peer, x, y, c)
                cp.wait_send()
                cp.wait_recv()

    outs = pl.pallas_call(
        body, name=name, in_specs=[HBM] * (2 * n) + [SEM, SEM, ANY], out_specs=[HBM] * (2 * n),
        out_shape=[pltpu.HBM(a.shape, a.dtype) for a in list(srcs) + list(lands)],
        input_output_aliases={i: i for i in range(2 * n)}, compiler_params=IN_FLIGHT,
    )(*srcs, *lands, send_sems, recv_sems, after)
    return outs[:n], outs[n:]


def _forward_to_sibling(lands, name):
    n = len(lands)

    def body(*refs):
        out_refs, (send_sems, recv_sems) = refs[n:2 * n], refs[2 * n:]
        x, y, c = _where_am_i()

        def copy(q, k, peer, half_of):
            rows = out_refs[q].at[2 * peer[0] + peer[1], _half_rows(lands[q].shape[1], half_of), :]
            return pltpu.make_async_remote_copy(
                src_ref=rows, dst_ref=rows, send_sem=send_sems.at[3 * q + k], recv_sem=recv_sems.at[3 * q + k],
                device_id=(x, y, 1 - c), device_id_type=MESH)

        sends = [copy(q, k, peer, c) for q in range(n) for k, peer in enumerate(_other_chips(x, y))]
        for cp in sends:
            cp.start()
        for q in range(n):
            for k, peer in enumerate(_other_chips(x, y)):
                copy(q, k, peer, 1 - c).wait_recv()
        for cp in sends:
            cp.wait_send()

    return pl.pallas_call(
        body, name=name, in_specs=[ANY] * n, out_specs=[ANY] * n,
        out_shape=[jax.ShapeDtypeStruct(a.shape, a.dtype) for a in lands],
        input_output_aliases={i: i for i in range(n)},
        scratch_shapes=[pltpu.SemaphoreType.DMA((3 * n,)), pltpu.SemaphoreType.DMA((3 * n,))],
    )(*lands)


def _with_own_slot(lands, own, chip):
    out = []
    for land, mine in zip(lands, own):
        mine = lax.dynamic_slice_in_dim(mine, chip, 1, axis=0) if mine.ndim == land.ndim else mine[None]
        out.append(lax.dynamic_update_slice(land, mine, (chip,) + (0,) * (land.ndim - 1)))
    return out


def _row_tile(rows, row_bytes, n_bufs, budget=24 * 1024 * 1024):
    best = None
    for t in range(16, rows + 1, 16):
        if rows % t == 0 and 2 * n_bufs * t * row_bytes <= budget:
            best = t
    assert best is not None, (rows, row_bytes)
    return best


def _add_halves(full, recv, core, out_dtype, name):
    n, h, cols = recv.shape
    tr = _row_tile(h, cols * 4, 3)
    nt = h // tr

    def body(core_ref, a_ref, b_ref, o_ref):
        o_ref[...] = (a_ref[...].astype(F32) + b_ref[...].astype(F32)).astype(out_dtype)

    blk = (None, tr, cols)
    return pl.pallas_call(
        body, name=name,
        grid_spec=pltpu.PrefetchScalarGridSpec(
            num_scalar_prefetch=1, grid=(n, nt),
            in_specs=[pl.BlockSpec(blk, lambda j, i, core_ref: (j, core_ref[0] * nt + i, 0)),
                      pl.BlockSpec(blk, lambda j, i, core_ref: (j, i, 0))],
            out_specs=pl.BlockSpec(blk, lambda j, i, core_ref: (j, i, 0))),
        out_shape=jax.ShapeDtypeStruct((n, h, cols), out_dtype),
        compiler_params=_params(("parallel", "parallel")),
    )(core.reshape(1), full, recv)


def _add2(a, b, name):
    r, cols = a.shape
    tr = _row_tile(r, cols * 4, 3)

    def body(a_ref, b_ref, o_ref):
        o_ref[...] = a_ref[...] + b_ref[...]

    blk = pl.BlockSpec((tr, cols), lambda i: (i, 0))
    return pl.pallas_call(body, name=name, grid=(r // tr,), in_specs=[blk, blk], out_specs=blk,
                          out_shape=jax.ShapeDtypeStruct(a.shape, a.dtype), compiler_params=_params(("parallel",)))(a, b)


def _sum_slots(parts, name):
    n, h, cols = parts.shape
    tr = _row_tile(h, cols * 4, n + 1)

    def body(p_ref, o_ref):
        acc = p_ref[0].astype(F32)
        for k in range(1, n):
            acc = acc + p_ref[k].astype(F32)
        o_ref[...] = acc

    return pl.pallas_call(
        body, name=name, grid=(h // tr,), in_specs=[pl.BlockSpec((n, tr, cols), lambda i: (0, i, 0))],
        out_specs=pl.BlockSpec((tr, cols), lambda i: (i, 0)), out_shape=jax.ShapeDtypeStruct((h, cols), F32),
        compiler_params=_params(("parallel",)),
    )(parts)


def _adamw(w, g, m, v, name):
    shape = w.shape
    cols = shape[-1]
    rows = math.prod(shape[:-1])
    tr = rows
    if rows % 8 == 0:
        tr = 8
        for t in range(8, rows + 1, 8):
            if rows % t == 0 and t * cols * 4 <= 1536 * 1024:
                tr = t
    c1 = 1.0 - ADAM_B1 ** ADAM_STEP
    c2 = 1.0 - ADAM_B2 ** ADAM_STEP

    def body(w_ref, g_ref, m_ref, v_ref, d_ref, nm_ref, nv_ref):
        gg = g_ref[...]
        nm = ADAM_B1 * m_ref[...] + (1.0 - ADAM_B1) * gg
        nv = ADAM_B2 * v_ref[...] + (1.0 - ADAM_B2) * (gg * gg)
        d_ref[...] = -ADAM_LR * ((nm / c1) / (jnp.sqrt(nv / c2) + ADAM_EPS) + ADAM_WD * w_ref[...])
        nm_ref[...] = nm
        nv_ref[...] = nv

    blk = pl.BlockSpec((tr, cols), lambda i: (i, 0))
    flat = jax.ShapeDtypeStruct((rows, cols), F32)
    outs = pl.pallas_call(
        body, name=name, grid=(rows // tr,), in_specs=[blk] * 4, out_specs=[blk] * 3, out_shape=[flat] * 3,
        compiler_params=_params(("parallel",)),
    )(*[a.reshape(rows, cols) for a in (w, g, m, v)])
    return [o.reshape(shape) for o in outs]


def _adamw_update(w, g, m, v):
    c1 = 1.0 - ADAM_B1 ** ADAM_STEP
    c2 = 1.0 - ADAM_B2 ** ADAM_STEP
    nm = ADAM_B1 * m + (1.0 - ADAM_B1) * g
    nv = ADAM_B2 * v + (1.0 - ADAM_B2) * (g * g)
    return -ADAM_LR * ((nm / c1) / (jnp.sqrt(nv / c2) + ADAM_EPS) + ADAM_WD * w), nm, nv


def _adamw_projection(w, m, v, mine, theirs, core, name):
    _, r, cols = w.shape
    h = r // 2
    tr = _row_tile(h, cols * 4, 12, budget=40 * 1024 * 1024)
    nt = h // tr

    def body(core_ref, w_ref, m_ref, v_ref, a0, b0, a1, b1, g_ref, d_ref, nm_ref, nv_ref):
        layer, half = pl.program_id(0), pl.program_id(1)
        own = half == core_ref[0]
        g = jnp.where(layer == 0, jnp.where(own, a0[...], b0[...]), jnp.where(own, a1[...], b1[...]))
        g_ref[...] = g
        d_ref[...], nm_ref[...], nv_ref[...] = _adamw_update(w_ref[...], g, m_ref[...], v_ref[...])

    def piece(layer, own):
        def index(l, hh, i, core_ref):
            used = (l == layer) & ((hh == core_ref[0]) == own)
            return (jnp.where(used, i, 0), 0)
        return pl.BlockSpec((tr, cols), index)

    whole = pl.BlockSpec((None, tr, cols), lambda l, hh, i, core_ref: (l, hh * nt + i, 0))
    return pl.pallas_call(
        body, name=name,
        grid_spec=pltpu.PrefetchScalarGridSpec(
            num_scalar_prefetch=1, grid=(2, 2, nt),
            in_specs=[whole] * 3 + [piece(0, True), piece(0, False), piece(1, True), piece(1, False)],
            out_specs=[whole] * 4),
        out_shape=[jax.ShapeDtypeStruct(w.shape, F32)] * 4,
        compiler_params=_params(("arbitrary", "arbitrary", "arbitrary")),
    )(core.reshape(1), w, m, v, mine[0], theirs[0], mine[1], theirs[1])


def _chip_shape(name):
    shape = list(FULL_SHAPES[name])
    axis = BIG_SHARD_AXIS.get(name, SMALL_SHARD_AXIS.get(name))
    if axis is not None:
        shape[axis] //= N_CHIPS
    return tuple(shape)


def _round_up(n, k):
    return -(-n // k) * k


SHARDED_SMALL_WORDS = sum(math.prod(_chip_shape(n)) for n in SMALL_SHARDED)
SHARDED_SMALL_ROWS = _round_up(-(-SHARDED_SMALL_WORDS // PACK_COLS), 32)
SMALL_WORDS = sum(math.prod(FULL_SHAPES[n]) for n in SMALL_NAMES)
SMALL_ROWS = _round_up(-(-SMALL_WORDS // PACK_COLS), 32)


def _pack_rows(flat, rows):
    return jnp.pad(flat, (0, rows * PACK_COLS - flat.shape[0])).reshape(rows, PACK_COLS)


def _pack_small_sharded(shards):
    return _pack_rows(jnp.concatenate([shards[n].reshape(-1) for n in SMALL_SHARDED]), SHARDED_SMALL_ROWS)


def _unpack_small_sharded(gathered_small):
    out = {}
    words = gathered_small.reshape(N_CHIPS, -1)
    w0 = 0
    for n in SMALL_SHARDED:
        size = math.prod(_chip_shape(n))
        blocks = words[:, w0:w0 + size].reshape((N_CHIPS,) + _chip_shape(n))
        out[n] = jnp.concatenate([blocks[j] for j in range(N_CHIPS)], axis=SMALL_SHARD_AXIS[n])
        w0 += size
    return out


def _pack_small_grads(grads):
    return _pack_rows(jnp.concatenate([grads[n].astype(F32).reshape(-1) for n in SMALL_NAMES]), SMALL_ROWS)


def _unpack_small_grads(total, chip):
    flat = total.reshape(-1)
    out = {}
    w0 = 0
    for n in SMALL_NAMES:
        size = math.prod(FULL_SHAPES[n])
        full = flat[w0:w0 + size].reshape(FULL_SHAPES[n])
        if n in SMALL_SHARD_AXIS:
            axis = SMALL_SHARD_AXIS[n]
            width = FULL_SHAPES[n][axis] // N_CHIPS
            full = lax.dynamic_slice_in_dim(full, chip * width, width, axis)
        out[n] = full
        w0 += size
    return out


def kernel(x, positions, even_w_in, even_a_ln_g, even_a_ln_b, even_a_ws, even_a_bs, even_b_sinks, even_w_out, even_ln_g, even_ln_b, odd_w_in, odd_conv_w, odd_conv_b, odd_w_a, odd_b_a, odd_w_x, odd_b_x, odd_lam, odd_w_pool, odd_d_scale, odd_w_out, odd_ln_g, odd_ln_b, loss_target, m_even_w_in, m_even_a_ln_g, m_even_a_ln_b, m_even_a_ws, m_even_a_bs, m_even_b_sinks, m_even_w_out, m_even_ln_g, m_even_ln_b, m_odd_w_in, m_odd_conv_w, m_odd_conv_b, m_odd_w_a, m_odd_b_a, m_odd_w_x, m_odd_b_x, m_odd_lam, m_odd_w_pool, m_odd_d_scale, m_odd_w_out, m_odd_ln_g, m_odd_ln_b, v_even_w_in, v_even_a_ln_g, v_even_a_ln_b, v_even_a_ws, v_even_a_bs, v_even_b_sinks, v_even_w_out, v_even_ln_g, v_even_ln_b, v_odd_w_in, v_odd_conv_w, v_odd_conv_b, v_odd_w_a, v_odd_b_a, v_odd_w_x, v_odd_b_x, v_odd_lam, v_odd_w_pool, v_odd_d_scale, v_odd_w_out, v_odd_ln_g, v_odd_ln_b):
    local = dict(zip(WEIGHT_NAMES, (even_w_in, even_a_ln_g, even_a_ln_b, even_a_ws, even_a_bs, even_b_sinks, even_w_out, even_ln_g, even_ln_b, odd_w_in, odd_conv_w, odd_conv_b, odd_w_a, odd_b_a, odd_w_x, odd_b_x, odd_lam, odd_w_pool, odd_d_scale, odd_w_out, odd_ln_g, odd_ln_b)))
    mom = dict(zip(WEIGHT_NAMES, (m_even_w_in, m_even_a_ln_g, m_even_a_ln_b, m_even_a_ws, m_even_a_bs, m_even_b_sinks, m_even_w_out, m_even_ln_g, m_even_ln_b, m_odd_w_in, m_odd_conv_w, m_odd_conv_b, m_odd_w_a, m_odd_b_a, m_odd_w_x, m_odd_b_x, m_odd_lam, m_odd_w_pool, m_odd_d_scale, m_odd_w_out, m_odd_ln_g, m_odd_ln_b)))
    vel = dict(zip(WEIGHT_NAMES, (v_even_w_in, v_even_a_ln_g, v_even_a_ln_b, v_even_a_ws, v_even_a_bs, v_even_b_sinks, v_even_w_out, v_even_ln_g, v_even_ln_b, v_odd_w_in, v_odd_conv_w, v_odd_conv_b, v_odd_w_a, v_odd_b_a, v_odd_w_x, v_odd_b_x, v_odd_lam, v_odd_w_pool, v_odd_d_scale, v_odd_w_out, v_odd_ln_g, v_odd_ln_b)))
    mx, my, core = _where_am_i()
    chip = 2 * mx + my

    stored = lambda n, a: jnp.swapaxes(a, 1, 2) if n == 'even_w_in' and W_IN_TRANSPOSED['even'] else a
    names_of = lambda layer: [('even' if layer % 2 == 0 else 'odd') + s for s in ('_w_in', '_w_out')]

    shards = [[stored(n, local[n])[layer // 2].astype(MXU_DTYPE) for n in names_of(layer)] for layer in range(DEPTH)]
    gathers, gather_token = _ici_start('gather', shards, "gather_start")
    weights = dict(local)
    weights.update(_unpack_small_sharded(
        _two_level_gather([_pack_small_sharded(local)], chip, "gather_small_weights")[0]))

    def projections_of(layer, after):
        own, lands = _ici_wait('gather', gathers[layer], gather_token if layer == 0 else after,
                               "gather_wait_%d" % layer)
        return _with_own_slot(_forward_to_sibling(lands, "gather_forward_%d" % layer), own, chip)

    in_flight, mine = [], {}

    def land(after):
        layer, scatter = in_flight.pop()
        parts, lands = _ici_wait('scatter', scatter, after, "scatter_wait_%d" % layer)
        for n, arrived in zip(names_of(layer), _with_own_slot(lands, parts, chip)):
            mine[n, layer // 2] = _sum_slots(arrived, "big_sum_%s_%d" % (n, layer // 2))

    def projection_grads(layer, dw_in, dw_out):
        if in_flight:
            land(dw_in)
        partial = [dw_in, dw_out]
        from_sibling = _sibling_swap(partial, "big_presum_swap_%d" % layer, other_half_of_axis1=True)
        chip_part = [_add_halves(g, r, core, MXU_DTYPE, "big_presum_add_%s_%d" % (n, layer // 2))
                     for g, r, n in zip(partial, from_sibling, names_of(layer))]
        (scatter,), token = _ici_start('scatter', [chip_part], "scatter_start_%d" % layer)
        in_flight.append((layer, scatter))
        return token

    loss, grad_x, grads = _local_step(x[0], positions[0], loss_target[0], weights, projections_of, projection_grads)
    land(grad_x)
    pieces = [(n, l) for n in BIG_NAMES for l in range(2)]
    theirs = dict(zip(pieces, _sibling_swap([mine[p] for p in pieces], "big_join")))

    small = _pack_small_grads(grads)
    chip_sum = _add2(small, _sibling_swap([small], "small_swap")[0], "small_presum_add")
    small_total = _sum_slots(_two_level_gather([chip_sum], chip, "small_gather")[0], "small_sum")
    reduced = _unpack_small_grads(small_total, chip)

    deltas, new_m, new_v = {}, {}, {}
    for n in SMALL_NAMES:
        deltas[n], new_m[n], new_v[n] = _adamw(local[n], reduced[n], mom[n], vel[n], "adamw_" + n)
    for n in BIG_NAMES:
        outs = _adamw_projection(stored(n, local[n]), stored(n, mom[n]), stored(n, vel[n]),
                                 [mine[n, l] for l in range(2)], [theirs[n, l] for l in range(2)], core, "adamw_" + n)
        reduced[n], deltas[n], new_m[n], new_v[n] = [stored(n, o) for o in outs]

    total_loss = lax.psum(loss[0, 0], ("x", "y", "c"))
    return (total_loss, grad_x[None], *[reduced[n] for n in WEIGHT_NAMES], *[deltas[n] for n in WEIGHT_NAMES],
            *[new_m[n] for n in WEIGHT_NAMES], *[new_v[n] for n in WEIGHT_NAMES])
```

```python
import functools
import math

import jax
import jax.numpy as jnp
from jax import lax
from jax.experimental import pallas as pl
from jax.experimental.pallas import tpu as pltpu

F32 = jnp.float32
BF16 = jnp.bfloat16
MXU_DTYPE = jnp.bfloat16

D_MODEL = 2048
DEPTH = 4
A_WIDTH = 1024
A_GROUPS = 8
CHUNK = 128
B_HEAD_DIM = 64
B_Q_HEADS = 16
B_KV_HEADS = 2
WINDOW = 128
ROT_DIM = 16
ROPE_THETA = 500000.0
C_WIDTH = 1024
C_HEADS = 8
CONV_WIDTH = 4
LRU_C = 8.0
D_WIDTH = 1024
D_GROUPS = 4
D_GROUP_DIM = 256
EVEN_IN = 5376
ODD_IN = 4096
DN_ALPHA = (2 * DEPTH) ** 0.25
LN_EPS = 1e-5
ATTN_SCALE = B_HEAD_DIM ** -0.5
NEG_BIG = -1e30

ADAM_LR = 0.001
ADAM_B1 = 0.9
ADAM_B2 = 0.999
ADAM_EPS = 1e-08
ADAM_WD = 0.01
ADAM_STEP = 10

LANES = 128
VMEM_LIMIT = 56 * 1024 * 1024
N_CHIPS = 4
MESH = pl.DeviceIdType.MESH

WEIGHT_NAMES = ['even_w_in', 'even_a_ln_g', 'even_a_ln_b', 'even_a_ws', 'even_a_bs', 'even_b_sinks', 'even_w_out',
                'even_ln_g', 'even_ln_b', 'odd_w_in', 'odd_conv_w', 'odd_conv_b', 'odd_w_a', 'odd_b_a', 'odd_w_x',
                'odd_b_x', 'odd_lam', 'odd_w_pool', 'odd_d_scale', 'odd_w_out', 'odd_ln_g', 'odd_ln_b']
BIG_SHARD_AXIS = {'even_w_in': 2, 'even_w_out': 1, 'odd_w_in': 2, 'odd_w_out': 1}
SMALL_SHARD_AXIS = {'odd_conv_w': 2, 'odd_conv_b': 1, 'odd_b_a': 1, 'odd_b_x': 1, 'odd_lam': 1, 'odd_w_pool': 2,
                    'odd_d_scale': 1, 'odd_ln_g': 1, 'odd_ln_b': 1}
FULL_SHAPES = {
    'even_w_in': (2, 2048, 5376), 'even_a_ln_g': (2, 1024), 'even_a_ln_b': (2, 1024), 'even_a_ws': (2, 8, 128, 128),
    'even_a_bs': (2, 8, 128), 'even_b_sinks': (2, 16), 'even_w_out': (2, 2048, 2048), 'even_ln_g': (2, 2048),
    'even_ln_b': (2, 2048), 'odd_w_in': (2, 2048, 4096), 'odd_conv_w': (2, 4, 1024), 'odd_conv_b': (2, 1024),
    'odd_w_a': (2, 8, 128, 128), 'odd_b_a': (2, 1024), 'odd_w_x': (2, 8, 128, 128), 'odd_b_x': (2, 1024),
    'odd_lam': (2, 1024), 'odd_w_pool': (2, 4, 256, 256), 'odd_d_scale': (2, 1024), 'odd_w_out': (2, 2048, 2048),
    'odd_ln_g': (2, 2048), 'odd_ln_b': (2, 2048)}
BIG_NAMES = ['even_w_in', 'even_w_out', 'odd_w_in', 'odd_w_out']
SMALL_SHARDED = ['odd_conv_w', 'odd_conv_b', 'odd_b_a', 'odd_b_x', 'odd_lam', 'odd_d_scale', 'odd_ln_g', 'odd_ln_b',
                 'odd_w_pool']
SMALL_NAMES = [n for n in WEIGHT_NAMES if n not in BIG_NAMES]
W_IN_TRANSPOSED = {'even': True, 'odd': False}
PACK_COLS = 1024


def _params(sem):
    return pltpu.CompilerParams(dimension_semantics=sem, vmem_limit_bytes=VMEM_LIMIT)


def _dot(a, b):
    return lax.dot_general(a, b, (((1,), (0,)), ((), ())), preferred_element_type=F32)


def _dot_nt(a, b):
    return lax.dot_general(a, b, (((1,), (1,)), ((), ())), preferred_element_type=F32)


def _dot_tn(a, b):
    return lax.dot_general(a, b, (((0,), (0,)), ((), ())), preferred_element_type=F32)


def _sigmoid(x):
    return 1.0 / (1.0 + jnp.exp(-x))


def _silu_and_grad(x):
    s = _sigmoid(x)
    return x * s, s * (1.0 + x * (1.0 - s))


def _mm_call(body, name, grid, in_specs, out_spec, out_shape, args):
    return pl.pallas_call(body, name=name, grid=grid, in_specs=in_specs, out_specs=out_spec, out_shape=out_shape,
                          compiler_params=_params(("parallel",) * len(grid)))(*args)


def _mm_in(x, wg, transposed, name):
    s, k = x.shape
    nc = wg.shape[1] if transposed else wg.shape[2]
    tm = min(1024, s)

    def body(x_ref, w_ref, o_ref):
        o_ref[...] = (_dot_nt if transposed else _dot)(x_ref[...].astype(MXU_DTYPE), w_ref[...])

    return _mm_call(body, name, (s // tm, N_CHIPS),
                    [pl.BlockSpec((tm, k), lambda i, j: (i, 0)),
                     pl.BlockSpec((None,) + wg.shape[1:], lambda i, j: (j, 0, 0))],
                    pl.BlockSpec((None, tm, nc), lambda i, j: (j, i, 0)),
                    jax.ShapeDtypeStruct((N_CHIPS, s, nc), F32), (x, wg))


def _mm_dx(dh, wg, dz, transposed, name):
    s, d = dz.shape
    nc = dh.shape[2]
    tm, tn = min(512, s), 512

    def body(a_ref, w_ref, r_ref, o_ref):
        acc = DN_ALPHA * r_ref[...]
        for j in range(N_CHIPS):
            acc = acc + (_dot if transposed else _dot_nt)(a_ref[j], w_ref[j])
        o_ref[...] = acc

    w_spec = (pl.BlockSpec((N_CHIPS, nc, tn), lambda i, j: (0, 0, j)) if transposed
              else pl.BlockSpec((N_CHIPS, tn, nc), lambda i, j: (0, j, 0)))
    return _mm_call(body, name, (s // tm, d // tn),
                    [pl.BlockSpec((N_CHIPS, tm, nc), lambda i, j: (0, i, 0)), w_spec,
                     pl.BlockSpec((tm, tn), lambda i, j: (i, j))],
                    pl.BlockSpec((tm, tn), lambda i, j: (i, j)), jax.ShapeDtypeStruct((s, d), F32), (dh, wg, dz))


def _mm_dw_in(x, dh, transposed, name):
    s, d = x.shape
    nc = dh.shape[2]
    tm = 1024

    def body(x_ref, b_ref, o_ref):
        xb = x_ref[...].astype(MXU_DTYPE)
        acc = _dot_tn(b_ref[...], xb) if transposed else _dot_tn(xb, b_ref[...])
        o_ref[...] = acc.astype(o_ref.dtype)

    if transposed:
        out_spec, shape = pl.BlockSpec((None, nc, tm), lambda i, j: (j, 0, i)), (N_CHIPS, nc, d)
    else:
        out_spec, shape = pl.BlockSpec((None, tm, nc), lambda i, j: (j, i, 0)), (N_CHIPS, d, nc)
    return _mm_call(body, name, (d // tm, N_CHIPS),
                    [pl.BlockSpec((s, tm), lambda i, j: (0, i)), pl.BlockSpec((None, s, nc), lambda i, j: (j, 0, 0))],
                    out_spec, jax.ShapeDtypeStruct(shape, MXU_DTYPE), (x, dh))


def _mm_out(mix, wg, name):
    s, k = mix.shape
    _, kc, d = wg.shape
    tm, tn = min(1024, s), 1024

    def body(a_ref, w_ref, o_ref):
        o_ref[...] = _dot(a_ref[...], w_ref[...].reshape(N_CHIPS * kc, tn))

    return _mm_call(body, name, (s // tm, d // tn),
                    [pl.BlockSpec((tm, k), lambda i, j: (i, 0)), pl.BlockSpec((N_CHIPS, kc, tn), lambda i, j: (0, 0, j))],
                    pl.BlockSpec((tm, tn), lambda i, j: (i, j)), jax.ShapeDtypeStruct((s, d), F32), (mix, wg))


def _mm_dmix(dz, wg, name):
    s, d = dz.shape
    kc = wg.shape[1]
    tm = min(1024, s)

    def body(a_ref, w_ref, o_ref):
        o_ref[...] = _dot_nt(a_ref[...], w_ref[...])

    return _mm_call(body, name, (s // tm, N_CHIPS),
                    [pl.BlockSpec((tm, d), lambda i, j: (i, 0)), pl.BlockSpec((None, kc, d), lambda i, j: (j, 0, 0))],
                    pl.BlockSpec((tm, kc), lambda i, j: (i, j)), jax.ShapeDtypeStruct((s, N_CHIPS * kc), F32), (dz, wg))


def _mm_dw_out(mix, dz, name):
    s, k = mix.shape
    d = dz.shape[1]
    kc, tn = k // N_CHIPS, 1024

    def body(a_ref, b_ref, o_ref):
        o_ref[...] = _dot_tn(a_ref[...], b_ref[...]).astype(o_ref.dtype)

    return _mm_call(body, name, (N_CHIPS, d // tn),
                    [pl.BlockSpec((s, kc), lambda i, j: (0, i)), pl.BlockSpec((s, tn), lambda i, j: (0, j))],
                    pl.BlockSpec((None, kc, tn), lambda i, j: (i, 0, j)),
                    jax.ShapeDtypeStruct((N_CHIPS, kc, d), MXU_DTYPE), (mix, dz))


def _ln_fwd(x, y, g, b):
    s, d = x.shape
    tr = min(256, s)

    def body(x_ref, y_ref, g_ref, b_ref, z_ref, o_ref, ob_ref):
        z = DN_ALPHA * x_ref[...] + y_ref[...]
        mu = jnp.mean(z, axis=1, keepdims=True)
        zc = z - mu
        var = jnp.mean(zc * zc, axis=1, keepdims=True)
        o = zc * lax.rsqrt(var + LN_EPS) * g_ref[...] + b_ref[...]
        z_ref[...] = z
        o_ref[...] = o
        ob_ref[...] = o.astype(MXU_DTYPE)

    row = pl.BlockSpec((tr, d), lambda i: (i, 0))
    vec = pl.BlockSpec((1, d), lambda i: (0, 0))
    return pl.pallas_call(
        body, name="ln_fwd", grid=(s // tr,), in_specs=[row, row, vec, vec], out_specs=[row, row, row],
        out_shape=[jax.ShapeDtypeStruct((s, d), F32), jax.ShapeDtypeStruct((s, d), F32),
                   jax.ShapeDtypeStruct((s, d), MXU_DTYPE)],
        compiler_params=_params(("parallel",)),
    )(x, y, g.reshape(1, d), b.reshape(1, d))


def _ln_bwd(dout, z, g, after):
    s, d = z.shape
    tr = min(256, s)

    def body(do_ref, z_ref, g_ref, after_ref, dz_ref, dzb_ref, dg_ref, db_ref):
        @pl.when(pl.program_id(0) == 0)
        def _():
            dg_ref[...] = jnp.zeros_like(dg_ref)
            db_ref[...] = jnp.zeros_like(db_ref)

        zz = z_ref[...]
        do = do_ref[...]
        mu = jnp.mean(zz, axis=1, keepdims=True)
        zc = zz - mu
        var = jnp.mean(zc * zc, axis=1, keepdims=True)
        rstd = lax.rsqrt(var + LN_EPS)
        zh = zc * rstd
        dg_ref[...] += jnp.sum(do * zh, axis=0, keepdims=True)
        db_ref[...] += jnp.sum(do, axis=0, keepdims=True)
        dzh = do * g_ref[...]
        dz = rstd * (dzh - jnp.mean(dzh, axis=1, keepdims=True) - zh * jnp.mean(dzh * zh, axis=1, keepdims=True))
        dz_ref[...] = dz
        dzb_ref[...] = dz.astype(MXU_DTYPE)

    row = pl.BlockSpec((tr, d), lambda i: (i, 0))
    vec = pl.BlockSpec((1, d), lambda i: (0, 0))
    return pl.pallas_call(
        body, name="ln_bwd", grid=(s // tr,), in_specs=[row, row, vec, ANY], out_specs=[row, row, vec, vec],
        out_shape=[jax.ShapeDtypeStruct((s, d), F32), jax.ShapeDtypeStruct((s, d), MXU_DTYPE),
                   jax.ShapeDtypeStruct((1, d), F32), jax.ShapeDtypeStruct((1, d), F32)],
        compiler_params=_params(("arbitrary",)),
    )(dout, z, g.reshape(1, d), after)


def _loss_and_grad(out, tgt):
    s, d = out.shape
    tr = min(256, s)

    def body(o_ref, t_ref, l_ref, do_ref):
        @pl.when(pl.program_id(0) == 0)
        def _():
            l_ref[...] = jnp.zeros_like(l_ref)

        e = o_ref[...] - t_ref[...]
        do_ref[...] = e * (1.0 / d)
        rows = jnp.sum(e * e, axis=1, keepdims=True)
        l_ref[...] += (0.5 / d) * jnp.sum(rows, axis=0, keepdims=True)

    row = pl.BlockSpec((tr, d), lambda i: (i, 0))
    return pl.pallas_call(
        body, name="loss", grid=(s // tr,), in_specs=[row, row],
        out_specs=[pl.BlockSpec((1, 1), lambda i: (0, 0)), row],
        out_shape=[jax.ShapeDtypeStruct((1, 1), F32), jax.ShapeDtypeStruct((s, d), F32)],
        compiler_params=_params(("arbitrary",)),
    )(out, tgt)


Q0, K0, V0, BG0 = 3072, 4096, 4224, 4352
T = CHUNK


def _lane_ids(width):
    return lax.broadcasted_iota(jnp.int32, (T, width), 1)


def _rope_swap(x):
    w = x.shape[1]
    l64 = _lane_ids(w) % B_HEAD_DIM
    half = ROT_DIM // 2
    return jnp.where(l64 < half, pltpu.roll(x, w - half, 1), jnp.where(l64 < ROT_DIM, pltpu.roll(x, half, 1), 0.0))


def _rope(x, c, s):
    return x * c + _rope_swap(x) * s


def _rope_bwd(d, c, s):
    return d * c + _rope_swap(d * s)


def _tril():
    return lax.broadcasted_iota(jnp.int32, (T, T), 0) >= lax.broadcasted_iota(jnp.int32, (T, T), 1)


def _band_mask(has_prev):
    qi = lax.broadcasted_iota(jnp.int32, (T, 2 * T), 0)
    kj = lax.broadcasted_iota(jnp.int32, (T, 2 * T), 1)
    return (kj > qi) & (kj <= qi + WINDOW) & ((kj >= WINDOW) | has_prev)


def _a_norm(v, g, b):
    mu = jnp.mean(v, axis=1, keepdims=True)
    vc = v - mu
    var = jnp.mean(vc * vc, axis=1, keepdims=True)
    rstd = lax.rsqrt(var + LN_EPS)
    vh = vc * rstd
    return vh, rstd, vh * g + b


def _kv_operands(hk, k_rot, kp_rot, vv, vp):
    lo = _lane_ids(LANES) < B_HEAD_DIM
    mine = lo if hk == 0 else jnp.logical_not(lo)
    kc = jnp.where(mine, k_rot, 0.0)
    kp = jnp.where(mine, kp_rot, 0.0)
    k2 = jnp.concatenate([kp + pltpu.roll(kp, B_HEAD_DIM, 1), kc + pltpu.roll(kc, B_HEAD_DIM, 1)], axis=0)
    vm = jnp.concatenate([jnp.where(mine, vp, 0.0), jnp.where(mine, vv, 0.0)], axis=0)
    vs = pltpu.roll(vm, B_HEAD_DIM, 1)
    v_lo, v_hi = (vm, vs) if hk == 0 else (vs, vm)
    return k2.astype(MXU_DTYPE), v_lo.astype(MXU_DTYPE), v_hi.astype(MXU_DTYPE), (vm + vs).astype(MXU_DTYPE)


def _softmax_with_sink(qm, k2, valid, sink):
    s = _dot_nt(qm, k2) * ATTN_SCALE
    s = jnp.where(valid, s, NEG_BIG)
    m = jnp.maximum(jnp.max(s, axis=1, keepdims=True), sink)
    ex = jnp.exp(s - m)
    es = jnp.exp(sink - m)
    inv = 1.0 / (jnp.sum(ex, axis=1, keepdims=True) + es)
    return ex * inv, es * inv


EVEN_SHARD = EVEN_IN // N_CHIPS
KV_CHIP = K0 // EVEN_SHARD
KV_SPAN = 3 * LANES
KP0, VP0 = K0 - KV_CHIP * EVEN_SHARD, V0 - KV_CHIP * EVEN_SHARD
assert V0 // EVEN_SHARD == KV_CHIP and VP0 + LANES <= KV_SPAN


def _even_specs(s):
    nb = s // T
    return nb, {
        'h': lambda r: pl.BlockSpec((N_CHIPS, T, EVEN_SHARD), lambda i: (0, r(i), 0)),
        'kvprev': lambda r: pl.BlockSpec((None, T, KV_SPAN), lambda i: (KV_CHIP, jnp.maximum(r(i) - 1, 0), 0)),
        'tab': lambda r: pl.BlockSpec((T, A_WIDTH), lambda i: (r(i), 0)),
        'tabprev': lambda r: pl.BlockSpec((T, LANES), lambda i: (jnp.maximum(r(i) - 1, 0), 0)),
    }


def _rows_from_shards(h4_ref, rows_ref):
    for j in range(N_CHIPS):
        rows_ref[:, j * EVEN_SHARD:(j + 1) * EVEN_SHARD] = h4_ref[j]


def _full(shape):
    nd = len(shape)
    return pl.BlockSpec(shape, lambda i: (0,) * nd)


def _even_fwd(h4, cq, sq, ln_g, ln_b, ws, bs_t, sinks):
    s = h4.shape[1]
    nb, sp = _even_specs(s)

    def body(h4_ref, kvp_ref, cq_ref, sq_ref, ckp_ref, skp_ref, g_ref, b_ref, ws_ref, bst_ref, sink_ref, mix_ref, h_ref):
        i = pl.program_id(0)
        _rows_from_shards(h4_ref, h_ref)
        tril = _tril()
        _, _, vn = _a_norm(h_ref[:, 1024:2048], g_ref[...], b_ref[...])
        vnb = vn.astype(MXU_DTYPE)
        for g in range(A_GROUPS):
            c0, c1 = g * LANES, (g + 1) * LANES
            wg = jnp.where(tril, ws_ref[g], 0.0).astype(MXU_DTYPE)
            mixed = _dot(wg, vnb[:, c0:c1]) + bst_ref[:, g:g + 1]
            sil, _ = _silu_and_grad(h_ref[:, 2048 + c0:2048 + c1])
            mix_ref[:, c0:c1] = (h_ref[:, c0:c1] * mixed * sil).astype(mix_ref.dtype)

        k_rot = _rope(h_ref[:, K0:K0 + LANES], cq_ref[:, 0:LANES], sq_ref[:, 0:LANES])
        kp_rot = _rope(kvp_ref[:, KP0:KP0 + LANES], ckp_ref[...], skp_ref[...])
        vp = kvp_ref[:, VP0:VP0 + LANES]
        valid = _band_mask(i > 0)
        lo = _lane_ids(LANES) < B_HEAD_DIM
        for hk in range(B_KV_HEADS):
            k2, v_lo, v_hi, _ = _kv_operands(hk, k_rot, kp_rot, h_ref[:, V0:V0 + LANES], vp)
            for pp in range(4):
                pair = hk * 4 + pp
                c0, c1 = pair * LANES, (pair + 1) * LANES
                qp = _rope(h_ref[:, Q0 + c0:Q0 + c1], cq_ref[:, c0:c1], sq_ref[:, c0:c1])
                acc = jnp.zeros((T, LANES), F32)
                for e in range(2):
                    head = 2 * pair + e
                    qm = jnp.where(lo if e == 0 else jnp.logical_not(lo), qp, 0.0).astype(MXU_DTYPE)
                    p, _ = _softmax_with_sink(qm, k2, valid, sink_ref[:, head:head + 1])
                    acc = acc + _dot(p.astype(MXU_DTYPE), v_lo if e == 0 else v_hi)
                sil, _ = _silu_and_grad(h_ref[:, BG0 + c0:BG0 + c1])
                mix_ref[:, A_WIDTH + c0:A_WIDTH + c1] = (acc * sil).astype(mix_ref.dtype)

    ident = lambda i: i
    return pl.pallas_call(
        body, name="even_fwd", grid=(nb,),
        in_specs=[sp['h'](ident), sp['kvprev'](ident), sp['tab'](ident), sp['tab'](ident),
                  sp['tabprev'](ident), sp['tabprev'](ident), _full((1, A_WIDTH)), _full((1, A_WIDTH)),
                  _full((A_GROUPS, T, T)), _full((T, A_GROUPS)), _full((1, B_Q_HEADS))],
        out_specs=pl.BlockSpec((T, 2 * A_WIDTH), lambda i: (i, 0)),
        out_shape=jax.ShapeDtypeStruct((s, 2 * A_WIDTH), MXU_DTYPE),
        scratch_shapes=[pltpu.VMEM((T, EVEN_IN), F32)],
        compiler_params=_params(("parallel",)),
    )(h4, h4, cq, sq, cq, sq, ln_g.reshape(1, -1), ln_b.reshape(1, -1), ws, bs_t, sinks.reshape(1, -1))


def _even_bwd(h4, dmix, cq, sq, ln_g, ln_b, ws, bs_t, sinks):
    s = h4.shape[1]
    nb, sp = _even_specs(s)

    def body(h4_ref, kvp_ref, cq_ref, sq_ref, ckp_ref, skp_ref, dm_ref, g_ref, b_ref, ws_ref, bst_ref, sink_ref,
             dh4_ref, dg_ref, db_ref, dws_ref, dbst_ref, dsink_ref, h_ref, dh_ref, dvn_scr, ck_scr, cv_scr):
        i = pl.program_id(0)

        @pl.when(i == 0)
        def _():
            for ref in (dg_ref, db_ref, dws_ref, dbst_ref, dsink_ref, ck_scr, cv_scr):
                ref[...] = jnp.zeros_like(ref)

        _rows_from_shards(h4_ref, h_ref)

        tril = _tril()
        vh, rstd, vn = _a_norm(h_ref[:, 1024:2048], g_ref[...], b_ref[...])
        vnb = vn.astype(MXU_DTYPE)
        for g in range(A_GROUPS):
            c0, c1 = g * LANES, (g + 1) * LANES
            wg = jnp.where(tril, ws_ref[g], 0.0).astype(MXU_DTYPE)
            mixed = _dot(wg, vnb[:, c0:c1]) + bst_ref[:, g:g + 1]
            ag = h_ref[:, 2048 + c0:2048 + c1]
            u = h_ref[:, c0:c1]
            sil, dsil = _silu_and_grad(ag)
            da = dm_ref[:, c0:c1]
            dh_ref[:, c0:c1] = (da * mixed * sil).astype(dh_ref.dtype)
            dh_ref[:, 2048 + c0:2048 + c1] = (da * u * mixed * dsil).astype(dh_ref.dtype)
            dmx = da * u * sil
            dmb = dmx.astype(MXU_DTYPE)
            dvn_scr[:, c0:c1] = _dot_tn(wg, dmb)
            dws_ref[g] += jnp.where(tril, _dot_nt(dmb, vnb[:, c0:c1]), 0.0)
            dbst_ref[:, g:g + 1] += jnp.sum(dmx, axis=1, keepdims=True)
        dvn = dvn_scr[...]
        dg_ref[...] += jnp.sum(dvn * vh, axis=0, keepdims=True)
        db_ref[...] += jnp.sum(dvn, axis=0, keepdims=True)
        dvh = dvn * g_ref[...]
        dv = rstd * (dvh - jnp.mean(dvh, axis=1, keepdims=True) - vh * jnp.mean(dvh * vh, axis=1, keepdims=True))
        dh_ref[:, 1024:2048] = dv.astype(dh_ref.dtype)

        ck_cur, sk_cur = cq_ref[:, 0:LANES], sq_ref[:, 0:LANES]
        k_rot = _rope(h_ref[:, K0:K0 + LANES], ck_cur, sk_cur)
        kp_rot = _rope(kvp_ref[:, KP0:KP0 + LANES], ckp_ref[...], skp_ref[...])
        vp = kvp_ref[:, VP0:VP0 + LANES]
        valid = _band_mask(i < nb - 1)
        lo = _lane_ids(LANES) < B_HEAD_DIM
        hi = jnp.logical_not(lo)
        dk_all = jnp.zeros((2 * T, LANES), F32)
        dv_all = jnp.zeros((2 * T, LANES), F32)
        for hk in range(B_KV_HEADS):
            k2, v_lo, v_hi, v2 = _kv_operands(hk, k_rot, kp_rot, h_ref[:, V0:V0 + LANES], vp)
            acc_k = jnp.zeros((2 * T, LANES), F32)
            acc_v = jnp.zeros((2 * T, LANES), F32)
            for pp in range(4):
                pair = hk * 4 + pp
                c0, c1 = pair * LANES, (pair + 1) * LANES
                cqp, sqp = cq_ref[:, c0:c1], sq_ref[:, c0:c1]
                qp = _rope(h_ref[:, Q0 + c0:Q0 + c1], cqp, sqp)
                qms, ps, pss = [], [], []
                ob = jnp.zeros((T, LANES), F32)
                for e in range(2):
                    head = 2 * pair + e
                    qm = jnp.where(lo if e == 0 else hi, qp, 0.0).astype(MXU_DTYPE)
                    p, psink = _softmax_with_sink(qm, k2, valid, sink_ref[:, head:head + 1])
                    ob = ob + _dot(p.astype(MXU_DTYPE), v_lo if e == 0 else v_hi)
                    qms.append(qm)
                    ps.append(p)
                    pss.append(psink)
                sil, dsil = _silu_and_grad(h_ref[:, BG0 + c0:BG0 + c1])
                dbo = dm_ref[:, A_WIDTH + c0:A_WIDTH + c1]
                dh_ref[:, BG0 + c0:BG0 + c1] = (dbo * ob * dsil).astype(dh_ref.dtype)
                dob = dbo * sil
                dq_rot = jnp.zeros((T, LANES), F32)
                for e in range(2):
                    head = 2 * pair + e
                    sel = lo if e == 0 else hi
                    dobm = jnp.where(sel, dob, 0.0).astype(MXU_DTYPE)
                    dp = _dot_nt(dobm, v2)
                    rs = jnp.sum(ps[e] * dp, axis=1, keepdims=True)
                    ds = ps[e] * (dp - rs) * ATTN_SCALE
                    dsink_ref[:, head:head + 1] += -jnp.sum(pss[e] * rs, axis=0, keepdims=True)
                    dsb = ds.astype(MXU_DTYPE)
                    dq_rot = dq_rot + jnp.where(sel, _dot(dsb, k2), 0.0)
                    acc_k = acc_k + _dot_tn(dsb, qms[e])
                    acc_v = acc_v + _dot_tn(ps[e].astype(MXU_DTYPE), dobm)
                dh_ref[:, Q0 + c0:Q0 + c1] = _rope_bwd(dq_rot, cqp, sqp).astype(dh_ref.dtype)
            lo2 = lax.broadcasted_iota(jnp.int32, (2 * T, LANES), 1) < B_HEAD_DIM
            mine = lo2 if hk == 0 else jnp.logical_not(lo2)
            dk_all = dk_all + jnp.where(mine, acc_k + pltpu.roll(acc_k, B_HEAD_DIM, 1), 0.0)
            dv_all = dv_all + jnp.where(mine, acc_v + pltpu.roll(acc_v, B_HEAD_DIM, 1), 0.0)
        dk_rot = dk_all[T:2 * T] + ck_scr[...]
        dvv = dv_all[T:2 * T] + cv_scr[...]
        ck_scr[...] = dk_all[0:T]
        cv_scr[...] = dv_all[0:T]
        dh_ref[:, K0:K0 + LANES] = _rope_bwd(dk_rot, ck_cur, sk_cur).astype(dh_ref.dtype)
        dh_ref[:, V0:V0 + LANES] = dvv.astype(dh_ref.dtype)
        for j in range(N_CHIPS):
            dh4_ref[j] = dh_ref[:, j * EVEN_SHARD:(j + 1) * EVEN_SHARD].astype(dh4_ref.dtype)

    rev = lambda i: nb - 1 - i
    return pl.pallas_call(
        body, name="even_bwd", grid=(nb,),
        in_specs=[sp['h'](rev), sp['kvprev'](rev), sp['tab'](rev), sp['tab'](rev),
                  sp['tabprev'](rev), sp['tabprev'](rev), pl.BlockSpec((T, 2 * A_WIDTH), lambda i: (rev(i), 0)),
                  _full((1, A_WIDTH)), _full((1, A_WIDTH)), _full((A_GROUPS, T, T)), _full((T, A_GROUPS)),
                  _full((1, B_Q_HEADS))],
        out_specs=[sp['h'](rev), _full((1, A_WIDTH)), _full((1, A_WIDTH)),
                   _full((A_GROUPS, T, T)), _full((T, A_GROUPS)), _full((1, B_Q_HEADS))],
        out_shape=[jax.ShapeDtypeStruct((N_CHIPS, s, EVEN_SHARD), MXU_DTYPE), jax.ShapeDtypeStruct((1, A_WIDTH), F32),
                   jax.ShapeDtypeStruct((1, A_WIDTH), F32), jax.ShapeDtypeStruct((A_GROUPS, T, T), F32),
                   jax.ShapeDtypeStruct((T, A_GROUPS), F32), jax.ShapeDtypeStruct((1, B_Q_HEADS), F32)],
        scratch_shapes=[pltpu.VMEM((T, EVEN_IN), F32), pltpu.VMEM((T, EVEN_IN), F32), pltpu.VMEM((T, A_WIDTH), F32),
                        pltpu.VMEM((T, LANES), F32), pltpu.VMEM((T, LANES), F32)],
        compiler_params=_params(("arbitrary",)),
    )(h4, h4, cq, sq, cq, sq, dmix, ln_g.reshape(1, -1), ln_b.reshape(1, -1), ws, bs_t, sinks.reshape(1, -1))


def _shift_down(x, d, fill=0.0):
    rows = lax.broadcasted_iota(jnp.int32, x.shape, 0)
    return jnp.where(rows >= d, pltpu.roll(x, d, 0), fill)


def _shift_up(x, d, fill=0.0):
    n = x.shape[0]
    rows = lax.broadcasted_iota(jnp.int32, x.shape, 0)
    return jnp.where(rows < n - d, pltpu.roll(x, n - d, 0), fill)


def _scan(a, b, shift):
    n = a.shape[0]
    d = 1
    while d < n:
        b = a * shift(b, d) + b
        if 2 * d < n:
            a = a * shift(a, d, 1.0)
        d *= 2
    return b


def _neg_expm1(y):
    u = jnp.exp(y)
    um1 = u - 1.0
    safe = jnp.where(um1 == 0.0, 1.0, jnp.log(u))
    near = jnp.where(um1 == 0.0, y, um1 * y / safe)
    return -jnp.where(y > -0.5, near, um1)


def _softplus(x):
    return jnp.maximum(x, 0.0) + jnp.log1p(jnp.exp(-jnp.abs(x)))


def _lru_forward(xc, cw, cb, wa, ba, wx, bx, lam):
    shifted = [_shift_down(xc, CONV_WIDTH - 1 - j) if j < CONV_WIDTH - 1 else xc for j in range(CONV_WIDTH)]
    xconv = cb
    for j in range(CONV_WIDTH):
        xconv = xconv + cw[j:j + 1, :] * shifted[j]
    xb = xconv.astype(MXU_DTYPE)
    r = _sigmoid(_dot(xb, wa.astype(MXU_DTYPE)) + ba)
    ig = _sigmoid(_dot(xb, wx.astype(MXU_DTYPE)) + bx)
    sp = _softplus(-lam)
    log_a = -LRU_C * r * sp
    a = jnp.exp(log_a)
    mult = jnp.sqrt(_neg_expm1(2.0 * log_a))
    hs = _scan(a, mult * ig * xconv, _shift_down)
    return dict(shifted=shifted, xconv=xconv, xb=xb, r=r, ig=ig, sp=sp, a=a, mult=mult, hs=hs)


def _col(width, off):
    return lambda s: pl.BlockSpec((s, width), lambda i: (0, off + i))


def _part(width, slot):
    return lambda s: pl.BlockSpec((None, s, width), lambda i: (slot, 0, i))


def _lru_specs(s):
    w = LANES
    return [_part(w, 0)(s), _part(w, 1)(s), pl.BlockSpec((CONV_WIDTH, w), lambda i: (0, i)),
            pl.BlockSpec((1, w), lambda i: (0, i)), pl.BlockSpec((None, w, w), lambda i: (i, 0, 0)),
            pl.BlockSpec((1, w), lambda i: (0, i)), pl.BlockSpec((None, w, w), lambda i: (i, 0, 0)),
            pl.BlockSpec((1, w), lambda i: (0, i)), pl.BlockSpec((1, w), lambda i: (0, i))]


def _lru_fwd(h, cw, cb, wa, ba, wx, bx, lam):
    s = h.shape[1]

    def body(xc_ref, cg_ref, cw_ref, cb_ref, wa_ref, ba_ref, wx_ref, bx_ref, lam_ref, o_ref):
        f = _lru_forward(xc_ref[...], cw_ref[...], cb_ref[...], wa_ref[...], ba_ref[...], wx_ref[...], bx_ref[...],
                         lam_ref[...])
        sil, _ = _silu_and_grad(cg_ref[...])
        o_ref[...] = (f['hs'] * sil).astype(o_ref.dtype)

    return pl.pallas_call(
        body, name="lru_fwd", grid=(C_HEADS,), in_specs=_lru_specs(s), out_specs=_col(LANES, 0)(s),
        out_shape=jax.ShapeDtypeStruct((s, C_WIDTH + D_WIDTH), MXU_DTYPE), compiler_params=_params(("parallel",)),
    )(h, h, cw, cb.reshape(1, -1), wa, ba.reshape(1, -1), wx, bx.reshape(1, -1), lam.reshape(1, -1))


def _lru_bwd(h, dmix, cw, cb, wa, ba, wx, bx, lam):
    s = h.shape[1]

    def body(xc_ref, cg_ref, cw_ref, cb_ref, wa_ref, ba_ref, wx_ref, bx_ref, lam_ref, dco_ref,
             dh_ref, dcw_ref, dcb_ref, dwa_ref, dba_ref, dwx_ref, dbx_ref, dlam_ref):
        dxc_ref, dcg_ref = dh_ref.at[0], dh_ref.at[1]
        cw, lam = cw_ref[...], lam_ref[...]
        wab, wxb = wa_ref[...].astype(MXU_DTYPE), wx_ref[...].astype(MXU_DTYPE)
        f = _lru_forward(xc_ref[...], cw, cb_ref[...], wa_ref[...], ba_ref[...], wx_ref[...], bx_ref[...], lam)
        sil, dsil = _silu_and_grad(cg_ref[...])
        dco = dco_ref[...]
        hs, a, mult, ig, r, xconv = f['hs'], f['a'], f['mult'], f['ig'], f['r'], f['xconv']
        dcg_ref[...] = (dco * hs * dsil).astype(dcg_ref.dtype)
        lamb = _scan(_shift_up(a, 1), dco * sil, _shift_up)
        da = lamb * _shift_down(hs, 1)
        d_ig = lamb * mult * xconv
        d_mult = lamb * ig * xconv
        dxconv = lamb * mult * ig
        d_log_a = da * a - d_mult * (a * a) / mult
        dlam_ref[...] = jnp.sum(d_log_a * r, axis=0, keepdims=True) * LRU_C * _sigmoid(-lam)
        dpa = d_log_a * (-LRU_C * f['sp']) * r * (1.0 - r)
        dpx = d_ig * ig * (1.0 - ig)
        dba_ref[...] = jnp.sum(dpa, axis=0, keepdims=True)
        dbx_ref[...] = jnp.sum(dpx, axis=0, keepdims=True)
        dpab, dpxb = dpa.astype(MXU_DTYPE), dpx.astype(MXU_DTYPE)
        dwa_ref[...] = _dot_tn(f['xb'], dpab)
        dwx_ref[...] = _dot_tn(f['xb'], dpxb)
        dxconv = dxconv + _dot_nt(dpab, wab) + _dot_nt(dpxb, wxb)
        dcb_ref[...] = jnp.sum(dxconv, axis=0, keepdims=True)
        dxc = jnp.zeros_like(dxconv)
        for j in range(CONV_WIDTH):
            dcw_ref[j:j + 1, :] = jnp.sum(dxconv * f['shifted'][j], axis=0, keepdims=True)
            back = _shift_up(dxconv, CONV_WIDTH - 1 - j) if j < CONV_WIDTH - 1 else dxconv
            dxc = dxc + cw[j:j + 1, :] * back
        dxc_ref[...] = dxc.astype(dxc_ref.dtype)

    w = LANES
    vec = pl.BlockSpec((1, w), lambda i: (0, i))
    mat = pl.BlockSpec((None, w, w), lambda i: (i, 0, 0))
    vshape = jax.ShapeDtypeStruct((1, C_WIDTH), F32)
    mshape = jax.ShapeDtypeStruct((C_HEADS, w, w), F32)
    return pl.pallas_call(
        body, name="lru_bwd", grid=(C_HEADS,), in_specs=_lru_specs(s) + [_col(w, 0)(s)],
        out_specs=[pl.BlockSpec((2, s, w), lambda i: (0, 0, i)), pl.BlockSpec((CONV_WIDTH, w), lambda i: (0, i)),
                   vec, mat, vec, mat, vec, vec],
        out_shape=[jax.ShapeDtypeStruct((N_CHIPS, s, C_WIDTH), MXU_DTYPE),
                   jax.ShapeDtypeStruct((CONV_WIDTH, C_WIDTH), F32), vshape, mshape, vshape, mshape, vshape, vshape],
        compiler_params=_params(("parallel",)),
    )(h, h, cw, cb.reshape(1, -1), wa, ba.reshape(1, -1), wx, bx.reshape(1, -1), lam.reshape(1, -1), dmix)


def _pool_window(g):
    return jnp.left_shift(2, g).astype(F32)


def _select_window(g, sums):
    out = sums[-1]
    for k in range(len(sums) - 2, -1, -1):
        out = jnp.where(g == k, sums[k], out)
    return out


def _window_sums(x, shift):
    sums, cur = [], x
    for k in range(D_GROUPS):
        cur = cur + shift(cur, 2 ** k)
        sums.append(cur)
    return sums


def _pool_specs(s):
    w = D_GROUP_DIM
    return [_part(w, 2)(s), _part(w, 3)(s),
            pl.BlockSpec((None, w, w), lambda i: (i, 0, 0)), pl.BlockSpec((1, w), lambda i: (0, i))]


def _pool_forward(xd, g):
    rows = lax.broadcasted_iota(jnp.int32, xd.shape, 0).astype(F32) + 1.0
    cnt = jnp.minimum(rows, _pool_window(g))
    pooled = _select_window(g, _window_sums(xd, _shift_down)) / cnt - xd
    return pooled, cnt


def _pool_fwd(h, mix, w_pool, d_scale):
    s = h.shape[1]

    def body(xd_ref, dg_ref, wp_ref, sc_ref, mix_ref, o_ref):
        pooled, _ = _pool_forward(xd_ref[...], pl.program_id(0))
        mixed = _dot(pooled.astype(MXU_DTYPE), wp_ref[...].astype(MXU_DTYPE))
        sil, _ = _silu_and_grad(dg_ref[...])
        o_ref[...] = (mixed * sc_ref[...] * sil).astype(o_ref.dtype)

    return pl.pallas_call(
        body, name="pool_fwd", grid=(D_GROUPS,), in_specs=_pool_specs(s) + [ANY],
        out_specs=_col(D_GROUP_DIM, C_WIDTH // D_GROUP_DIM)(s), out_shape=jax.ShapeDtypeStruct(mix.shape, mix.dtype),
        input_output_aliases={4: 0}, compiler_params=_params(("parallel",)),
    )(h, h, w_pool, d_scale.reshape(1, -1), mix)


def _pool_bwd(h, dmix, dh, w_pool, d_scale):
    s = h.shape[1]
    w = D_GROUP_DIM

    def body(xd_ref, dg_ref, wp_ref, sc_ref, ddo_ref, dh_in_ref, dh_ref, dwp_ref, dsc_ref):
        dxd_ref, ddg_ref = dh_ref.at[0], dh_ref.at[1]
        g = pl.program_id(0)
        pooled, cnt = _pool_forward(xd_ref[...], g)
        pb = pooled.astype(MXU_DTYPE)
        wpb = wp_ref[...].astype(MXU_DTYPE)
        mixed = _dot(pb, wpb)
        sil, dsil = _silu_and_grad(dg_ref[...])
        ddo = ddo_ref[...]
        sc = sc_ref[...]
        ddg_ref[...] = (ddo * mixed * sc * dsil).astype(ddg_ref.dtype)
        dms = ddo * sil
        dsc_ref[...] = jnp.sum(dms * mixed, axis=0, keepdims=True)
        dmb = (dms * sc).astype(MXU_DTYPE)
        dwp_ref[...] = _dot_tn(pb, dmb)
        dpooled = _dot_nt(dmb, wpb)
        dxd = _select_window(g, _window_sums(dpooled / cnt, _shift_up)) - dpooled
        dxd_ref[...] = dxd.astype(dxd_ref.dtype)

    return pl.pallas_call(
        body, name="pool_bwd", grid=(D_GROUPS,), in_specs=_pool_specs(s) + [_col(w, C_WIDTH // w)(s), ANY],
        out_specs=[pl.BlockSpec((2, s, w), lambda i: (1, 0, i)), pl.BlockSpec((None, w, w), lambda i: (i, 0, 0)),
                   pl.BlockSpec((1, w), lambda i: (0, i))],
        out_shape=[jax.ShapeDtypeStruct(dh.shape, dh.dtype),
                   jax.ShapeDtypeStruct((D_GROUPS, w, w), F32), jax.ShapeDtypeStruct((1, D_WIDTH), F32)],
        input_output_aliases={5: 0}, compiler_params=_params(("parallel",)),
    )(h, h, w_pool, d_scale.reshape(1, -1), dmix, dh)


def _rope_tables(positions):
    s = positions.shape[0]
    inv_freq = ROPE_THETA ** (-jnp.arange(0, ROT_DIM, 2, dtype=F32) / ROT_DIM)
    ang = positions.astype(F32)[:, None] * inv_freq
    c, sn = jnp.cos(ang), jnp.sin(ang)
    rest = B_HEAD_DIM - ROT_DIM
    c64 = jnp.concatenate([c, c, jnp.ones((s, rest), F32)], axis=1)
    s64 = jnp.concatenate([-sn, sn, jnp.zeros((s, rest), F32)], axis=1)
    return jnp.tile(c64, (1, B_Q_HEADS)), jnp.tile(s64, (1, B_Q_HEADS))


def _local_step(x, positions, target, w, projections_of, projection_grads):
    cq, sq = _rope_tables(positions)
    saved = []
    xin, xin_b = x, x
    for layer in range(DEPTH):
        j = layer // 2
        kind = 'even' if layer % 2 == 0 else 'odd'
        w_in, w_out = projections_of(layer, xin_b)
        h = _mm_in(xin_b, w_in, W_IN_TRANSPOSED[kind], "mm_" + kind + "_in")
        if kind == 'even':
            mix = _even_fwd(h, cq, sq, w['even_a_ln_g'][j], w['even_a_ln_b'][j], w['even_a_ws'][j],
                            w['even_a_bs'][j].T, w['even_b_sinks'][j])
        else:
            mix = _lru_fwd(h, w['odd_conv_w'][j], w['odd_conv_b'][j], w['odd_w_a'][j], w['odd_b_a'][j],
                           w['odd_w_x'][j], w['odd_b_x'][j], w['odd_lam'][j])
            mix = _pool_fwd(h, mix, w['odd_w_pool'][j], w['odd_d_scale'][j])
        y = _mm_out(mix, w_out, "mm_out")
        z, out, out_b = _ln_fwd(xin, y, w[kind + '_ln_g'][j], w[kind + '_ln_b'][j])
        saved.append((xin_b, h, mix, z, w_in, w_out))
        xin, xin_b = out, out_b

    loss, dout = _loss_and_grad(xin, target)
    per_layer = {n: [None, None] for n in SMALL_NAMES}
    behind = loss
    for layer in reversed(range(DEPTH)):
        j = layer // 2
        kind = 'even' if layer % 2 == 0 else 'odd'
        xin_b, h, mix, z, w_in, w_out = saved[layer]
        dz, dz_b, dg, db = _ln_bwd(dout, z, w[kind + '_ln_g'][j], behind)
        per_layer[kind + '_ln_g'][j], per_layer[kind + '_ln_b'][j] = dg[0], db[0]
        dmix = _mm_dmix(dz_b, w_out, "mm_dmix")
        dw_out = _mm_dw_out(mix, dz_b, "mm_dw_out")
        if kind == 'even':
            dh, dlg, dlb, dws, dbst, dsink = _even_bwd(
                h, dmix, cq, sq, w['even_a_ln_g'][j], w['even_a_ln_b'][j], w['even_a_ws'][j], w['even_a_bs'][j].T,
                w['even_b_sinks'][j])
            for n, val in (('even_a_ln_g', dlg[0]), ('even_a_ln_b', dlb[0]), ('even_a_ws', dws),
                           ('even_a_bs', dbst.T), ('even_b_sinks', dsink[0])):
                per_layer[n][j] = val
        else:
            dh, dcw, dcb, dwa, dba, dwx, dbx, dlam = _lru_bwd(
                h, dmix, w['odd_conv_w'][j], w['odd_conv_b'][j], w['odd_w_a'][j], w['odd_b_a'][j], w['odd_w_x'][j],
                w['odd_b_x'][j], w['odd_lam'][j])
            dh, dwp, dsc = _pool_bwd(h, dmix, dh, w['odd_w_pool'][j], w['odd_d_scale'][j])
            for n, val in (('odd_conv_w', dcw), ('odd_conv_b', dcb[0]), ('odd_w_a', dwa), ('odd_b_a', dba[0]),
                           ('odd_w_x', dwx), ('odd_b_x', dbx[0]), ('odd_lam', dlam[0]), ('odd_w_pool', dwp),
                           ('odd_d_scale', dsc[0])):
                per_layer[n][j] = val
        dout = _mm_dx(dh, w_in, dz, W_IN_TRANSPOSED[kind], "mm_dx_" + kind)
        dw_in = _mm_dw_in(xin_b, dh, W_IN_TRANSPOSED[kind], "mm_dw_in_" + kind)
        behind = projection_grads(layer, dw_in, dw_out)
    return loss, dout, {n: jnp.stack(v) for n, v in per_layer.items()}


ANY = pl.BlockSpec(memory_space=pl.ANY)


def _where_am_i():
    return lax.axis_index("x"), lax.axis_index("y"), lax.axis_index("c")


def _other_chips(x, y):
    return [(1 - x, y), (x, 1 - y), (1 - x, 1 - y)]


def _half_rows(ref_or_array_rows, which):
    half = ref_or_array_rows // 2
    return pl.ds(pl.multiple_of(which * half, 16), half)


def _two_level_gather(bufs, chip, name):
    n = len(bufs)

    def body(*refs):
        x_refs, out_refs, (send_sems, recv_sems) = refs[:n], refs[n:2 * n], refs[2 * n:]
        x, y, c = _where_am_i()
        sibling = (x, y, 1 - c)
        chips = _other_chips(x, y)

        def rows(p, px, py, pc):
            return out_refs[p].at[2 * px + py, _half_rows(bufs[p].shape[0], pc), :]

        def copy(p, k, block, to, src=None):
            return pltpu.make_async_remote_copy(
                src_ref=rows(p, *block) if src is None else src, dst_ref=rows(p, *block),
                send_sem=send_sems.at[6 * p + k], recv_sem=recv_sems.at[6 * p + k], device_id=to, device_id_type=MESH)

        first = [copy(p, j, (x, y, c), (*chip, c), src=x_refs[p].at[_half_rows(bufs[p].shape[0], c), :])
                 for p in range(n) for j, chip in enumerate(chips)]
        for cp in first:
            cp.start()
        passed = []
        for p in range(n):
            for j, chip in enumerate(chips):
                copy(p, j, (*chip, c), (x, y, c)).wait_recv()
                passed.append(copy(p, 3 + j, (*chip, c), sibling))
                passed[-1].start()
        for p in range(n):
            for j, chip in enumerate(chips):
                copy(p, 3 + j, (*chip, 1 - c), (x, y, c)).wait_recv()
        for cp in first + passed:
            cp.wait_send()

    others = pl.pallas_call(
        body, name=name, in_specs=[ANY] * n, out_specs=[ANY] * n,
        out_shape=[jax.ShapeDtypeStruct((N_CHIPS,) + b.shape, b.dtype) for b in bufs],
        scratch_shapes=[pltpu.SemaphoreType.DMA((6 * n,)), pltpu.SemaphoreType.DMA((6 * n,))],
    )(*bufs)
    return [lax.dynamic_update_slice(o, b[None], (chip, 0, 0)) for o, b in zip(others, bufs)]


def _sibling_swap(bufs, name, *, other_half_of_axis1=False, after=()):
    n, na = len(bufs), len(after)
    out_shapes = [(b.shape[0], b.shape[1] // 2, b.shape[2]) if other_half_of_axis1 else b.shape for b in bufs]

    def body(*refs):
        x_refs, out_refs, (send_sems, recv_sems) = refs[:n], refs[n + na:2 * n + na], refs[2 * n + na:]
        x, y, c = _where_am_i()
        copies = []
        for p in range(n):
            src = x_refs[p].at[:, _half_rows(bufs[p].shape[1], 1 - c), :] if other_half_of_axis1 else x_refs[p]
            copies.append(pltpu.make_async_remote_copy(
                src_ref=src, dst_ref=out_refs[p], send_sem=send_sems.at[p], recv_sem=recv_sems.at[p],
                device_id=(x, y, 1 - c), device_id_type=MESH))
            copies[-1].start()
        for cp in copies:
            cp.wait()

    return pl.pallas_call(
        body, name=name, in_specs=[ANY] * (n + na), out_specs=[ANY] * n,
        out_shape=[jax.ShapeDtypeStruct(s, b.dtype) for s, b in zip(out_shapes, bufs)],
        scratch_shapes=[pltpu.SemaphoreType.DMA((n,)), pltpu.SemaphoreType.DMA((n,))],
    )(*bufs, *after)


HBM = pl.BlockSpec(memory_space=pltpu.HBM)
SEM = pl.BlockSpec(memory_space=pltpu.SEMAPHORE)
IN_FLIGHT = pltpu.CompilerParams(has_side_effects=pltpu.SideEffectType.DATAFLOW_SIDE_EFFECTING)


def _ici_copy(kind, src_ref, land_ref, send_sems, recv_sems, k, peer, x, y, c):
    px, py = peer
    me = 2 * x + y
    if kind == 'gather':
        half = _half_rows(src_ref.shape[0], c)
        src, dst = src_ref.at[half, :], land_ref.at[me, half, :]
    else:
        src, dst = src_ref.at[2 * px + py], land_ref.at[me]
    return pltpu.make_async_remote_copy(src_ref=src, dst_ref=dst, send_sem=send_sems.at[k], recv_sem=recv_sems.at[k],
                                        device_id=(px, py, c), device_id_type=MESH)


def _ici_start(kind, groups, name):
    flat = [a for g in groups for a in g]
    n, ng = len(flat), len(groups)
    land_shapes = [((N_CHIPS,) + a.shape) if kind == 'gather' else a.shape for a in flat]

    def body(*refs):
        srcs, lands, sems, token = refs[:n], refs[n:2 * n], refs[2 * n:2 * n + 2 * ng], refs[-1]
        x, y, c = _where_am_i()
        p = 0
        for gi, g in enumerate(groups):
            for q in range(len(g)):
                for k, peer in enumerate(_other_chips(x, y)):
                    _ici_copy(kind, srcs[p], lands[p], sems[2 * gi], sems[2 * gi + 1], 3 * q + k, peer, x, y, c).start()
                p += 1
        token[...] = jnp.zeros_like(token)

    sem_shapes = [pltpu.SemaphoreType.DMA((3 * len(g),)) for g in groups for _ in range(2)]
    outs = pl.pallas_call(
        body, name=name, in_specs=[HBM] * (2 * n),
        out_specs=[SEM] * (2 * ng) + [HBM] * (2 * n) + [pl.BlockSpec(memory_space=pltpu.VMEM)],
        out_shape=sem_shapes + [pltpu.HBM(a.shape, a.dtype) for a in flat]
        + [pltpu.HBM(s, a.dtype) for s, a in zip(land_shapes, flat)] + [jax.ShapeDtypeStruct((8, LANES), F32)],
        input_output_aliases={i: 2 * ng + i for i in range(2 * n)}, compiler_params=IN_FLIGHT,
    )(*[pltpu.with_memory_space_constraint(a, pltpu.HBM) for a in flat],
      *[pltpu.with_memory_space_constraint(lax.empty(s, a.dtype), pltpu.HBM) for s, a in zip(land_shapes, flat)])
    sems, srcs, lands, token = outs[:2 * ng], outs[2 * ng:2 * ng + n], outs[2 * ng + n:2 * ng + 2 * n], outs[-1]
    started, p = [], 0
    for gi, g in enumerate(groups):
        started.append((sems[2 * gi], sems[2 * gi + 1], srcs[p:p + len(g)], lands[p:p + len(g)]))
        p += len(g)
    return started, token


def _ici_wait(kind, started, after, name):
    send_sems, recv_sems, srcs, lands = started
    n = len(srcs)
    after = list(after)

    def body(*refs):
        src_refs, land_refs, send_ref, recv_ref = refs[:n], refs[n:2 * n], refs[2 * n], refs[2 * n + 1]
        x, y, c = _where_am_i()
        for q in range(n):
            for k, peer in enumerate(_other_chips(x, y)):
                cp = _ici_copy(kind, src_refs[q], land_refs[q], send_ref, recv_ref, 3 * q + k, peer, x, y, c)
                cp.wait_send()
                cp.wait_recv()

    outs = pl.pallas_call(
        body, name=name, in_specs=[HBM] * (2 * n) + [SEM, SEM] + [ANY] * len(after), out_specs=[HBM] * (2 * n),
        out_shape=[pltpu.HBM(a.shape, a.dtype) for a in list(srcs) + list(lands)],
        input_output_aliases={i: i for i in range(2 * n)}, compiler_params=IN_FLIGHT,
    )(*srcs, *lands, send_sems, recv_sems, *after)
    return outs[:n], outs[n:]


def _forward_to_sibling(lands, name):
    n = len(lands)

    def body(*refs):
        out_refs, (send_sems, recv_sems) = refs[n:2 * n], refs[2 * n:]
        x, y, c = _where_am_i()

        def copy(q, k, peer, half_of):
            rows = out_refs[q].at[2 * peer[0] + peer[1], _half_rows(lands[q].shape[1], half_of), :]
            return pltpu.make_async_remote_copy(
                src_ref=rows, dst_ref=rows, send_sem=send_sems.at[3 * q + k], recv_sem=recv_sems.at[3 * q + k],
                device_id=(x, y, 1 - c), device_id_type=MESH)

        sends = [copy(q, k, peer, c) for q in range(n) for k, peer in enumerate(_other_chips(x, y))]
        for cp in sends:
            cp.start()
        for q in range(n):
            for k, peer in enumerate(_other_chips(x, y)):
                copy(q, k, peer, 1 - c).wait_recv()
        for cp in sends:
            cp.wait_send()

    return pl.pallas_call(
        body, name=name, in_specs=[ANY] * n, out_specs=[ANY] * n,
        out_shape=[jax.ShapeDtypeStruct(a.shape, a.dtype) for a in lands],
        input_output_aliases={i: i for i in range(n)},
        scratch_shapes=[pltpu.SemaphoreType.DMA((3 * n,)), pltpu.SemaphoreType.DMA((3 * n,))],
    )(*lands)


def _with_own_slot(lands, own, chip):
    out = []
    for land, mine in zip(lands, own):
        mine = lax.dynamic_slice_in_dim(mine, chip, 1, axis=0) if mine.ndim == land.ndim else mine[None]
        out.append(lax.dynamic_update_slice(land, mine, (chip,) + (0,) * (land.ndim - 1)))
    return out


def _row_tile(rows, row_bytes, n_bufs, budget=24 * 1024 * 1024):
    best = None
    for t in range(16, rows + 1, 16):
        if rows % t == 0 and 2 * n_bufs * t * row_bytes <= budget:
            best = t
    assert best is not None, (rows, row_bytes)
    return best


def _add_halves(full, recv, core, out_dtype, name):
    n, h, cols = recv.shape
    tr = _row_tile(h, cols * 4, 3)
    nt = h // tr

    def body(core_ref, a_ref, b_ref, o_ref):
        o_ref[...] = (a_ref[...].astype(F32) + b_ref[...].astype(F32)).astype(out_dtype)

    blk = (None, tr, cols)
    return pl.pallas_call(
        body, name=name,
        grid_spec=pltpu.PrefetchScalarGridSpec(
            num_scalar_prefetch=1, grid=(n, nt),
            in_specs=[pl.BlockSpec(blk, lambda j, i, core_ref: (j, core_ref[0] * nt + i, 0)),
                      pl.BlockSpec(blk, lambda j, i, core_ref: (j, i, 0))],
            out_specs=pl.BlockSpec(blk, lambda j, i, core_ref: (j, i, 0))),
        out_shape=jax.ShapeDtypeStruct((n, h, cols), out_dtype),
        compiler_params=_params(("parallel", "parallel")),
    )(core.reshape(1), full, recv)


def _add2(a, b, name):
    r, cols = a.shape
    tr = _row_tile(r, cols * 4, 3)

    def body(a_ref, b_ref, o_ref):
        o_ref[...] = a_ref[...] + b_ref[...]

    blk = pl.BlockSpec((tr, cols), lambda i: (i, 0))
    return pl.pallas_call(body, name=name, grid=(r // tr,), in_specs=[blk, blk], out_specs=blk,
                          out_shape=jax.ShapeDtypeStruct(a.shape, a.dtype), compiler_params=_params(("parallel",)))(a, b)


def _sum_slots(parts, name):
    n, h, cols = parts.shape
    tr = _row_tile(h, cols * 4, n + 1)

    def body(p_ref, o_ref):
        acc = p_ref[0].astype(F32)
        for k in range(1, n):
            acc = acc + p_ref[k].astype(F32)
        o_ref[...] = acc

    return pl.pallas_call(
        body, name=name, grid=(h // tr,), in_specs=[pl.BlockSpec((n, tr, cols), lambda i: (0, i, 0))],
        out_specs=pl.BlockSpec((tr, cols), lambda i: (i, 0)), out_shape=jax.ShapeDtypeStruct((h, cols), F32),
        compiler_params=_params(("parallel",)),
    )(parts)


def _adamw(w, g, m, v, name):
    shape = w.shape
    cols = shape[-1]
    rows = math.prod(shape[:-1])
    tr = rows
    if rows % 8 == 0:
        tr = 8
        for t in range(8, rows + 1, 8):
            if rows % t == 0 and t * cols * 4 <= 1536 * 1024:
                tr = t
    c1 = 1.0 - ADAM_B1 ** ADAM_STEP
    c2 = 1.0 - ADAM_B2 ** ADAM_STEP

    def body(w_ref, g_ref, m_ref, v_ref, d_ref, nm_ref, nv_ref):
        gg = g_ref[...]
        nm = ADAM_B1 * m_ref[...] + (1.0 - ADAM_B1) * gg
        nv = ADAM_B2 * v_ref[...] + (1.0 - ADAM_B2) * (gg * gg)
        d_ref[...] = -ADAM_LR * ((nm / c1) / (jnp.sqrt(nv / c2) + ADAM_EPS) + ADAM_WD * w_ref[...])
        nm_ref[...] = nm
        nv_ref[...] = nv

    blk = pl.BlockSpec((tr, cols), lambda i: (i, 0))
    flat = jax.ShapeDtypeStruct((rows, cols), F32)
    outs = pl.pallas_call(
        body, name=name, grid=(rows // tr,), in_specs=[blk] * 4, out_specs=[blk] * 3, out_shape=[flat] * 3,
        compiler_params=_params(("parallel",)),
    )(*[a.reshape(rows, cols) for a in (w, g, m, v)])
    return [o.reshape(shape) for o in outs]


def _adamw_update(w, g, m, v):
    c1 = 1.0 - ADAM_B1 ** ADAM_STEP
    c2 = 1.0 - ADAM_B2 ** ADAM_STEP
    nm = ADAM_B1 * m + (1.0 - ADAM_B1) * g
    nv = ADAM_B2 * v + (1.0 - ADAM_B2) * (g * g)
    return -ADAM_LR * ((nm / c1) / (jnp.sqrt(nv / c2) + ADAM_EPS) + ADAM_WD * w), nm, nv


def _adamw_projection(w, m, v, mine, theirs, core, name):
    _, r, cols = w.shape
    h = r // 2
    tr = _row_tile(h, cols * 4, 12, budget=40 * 1024 * 1024)
    nt = h // tr

    def body(core_ref, w_ref, m_ref, v_ref, a0, b0, a1, b1, g_ref, d_ref, nm_ref, nv_ref):
        layer, half = pl.program_id(0), pl.program_id(1)
        own = half == core_ref[0]
        g = jnp.where(layer == 0, jnp.where(own, a0[...], b0[...]), jnp.where(own, a1[...], b1[...]))
        g_ref[...] = g
        d_ref[...], nm_ref[...], nv_ref[...] = _adamw_update(w_ref[...], g, m_ref[...], v_ref[...])

    def piece(layer, own):
        def index(l, hh, i, core_ref):
            used = (l == layer) & ((hh == core_ref[0]) == own)
            return (jnp.where(used, i, 0), 0)
        return pl.BlockSpec((tr, cols), index)

    whole = pl.BlockSpec((None, tr, cols), lambda l, hh, i, core_ref: (l, hh * nt + i, 0))
    return pl.pallas_call(
        body, name=name,
        grid_spec=pltpu.PrefetchScalarGridSpec(
            num_scalar_prefetch=1, grid=(2, 2, nt),
            in_specs=[whole] * 3 + [piece(0, True), piece(0, False), piece(1, True), piece(1, False)],
            out_specs=[whole] * 4),
        out_shape=[jax.ShapeDtypeStruct(w.shape, F32)] * 4,
        compiler_params=_params(("arbitrary", "arbitrary", "arbitrary")),
    )(core.reshape(1), w, m, v, mine[0], theirs[0], mine[1], theirs[1])


def _chip_shape(name):
    shape = list(FULL_SHAPES[name])
    axis = BIG_SHARD_AXIS.get(name, SMALL_SHARD_AXIS.get(name))
    if axis is not None:
        shape[axis] //= N_CHIPS
    return tuple(shape)


def _round_up(n, k):
    return -(-n // k) * k


SHARDED_SMALL_WORDS = sum(math.prod(_chip_shape(n)) for n in SMALL_SHARDED)
SHARDED_SMALL_ROWS = _round_up(-(-SHARDED_SMALL_WORDS // PACK_COLS), 32)
SMALL_WORDS = sum(math.prod(FULL_SHAPES[n]) for n in SMALL_NAMES)
SMALL_ROWS = _round_up(-(-SMALL_WORDS // PACK_COLS), 32)


def _pack_rows(flat, rows):
    return jnp.pad(flat, (0, rows * PACK_COLS - flat.shape[0])).reshape(rows, PACK_COLS)


def _pack_small_sharded(shards):
    return _pack_rows(jnp.concatenate([shards[n].reshape(-1) for n in SMALL_SHARDED]), SHARDED_SMALL_ROWS)


def _unpack_small_sharded(gathered_small):
    out = {}
    words = gathered_small.reshape(N_CHIPS, -1)
    w0 = 0
    for n in SMALL_SHARDED:
        size = math.prod(_chip_shape(n))
        blocks = words[:, w0:w0 + size].reshape((N_CHIPS,) + _chip_shape(n))
        out[n] = jnp.concatenate([blocks[j] for j in range(N_CHIPS)], axis=SMALL_SHARD_AXIS[n])
        w0 += size
    return out


def _pack_small_grads(grads):
    return _pack_rows(jnp.concatenate([grads[n].astype(F32).reshape(-1) for n in SMALL_NAMES]), SMALL_ROWS)


def _unpack_small_grads(total, chip):
    flat = total.reshape(-1)
    out = {}
    w0 = 0
    for n in SMALL_NAMES:
        size = math.prod(FULL_SHAPES[n])
        full = flat[w0:w0 + size].reshape(FULL_SHAPES[n])
        if n in SMALL_SHARD_AXIS:
            axis = SMALL_SHARD_AXIS[n]
            width = FULL_SHAPES[n][axis] // N_CHIPS
            full = lax.dynamic_slice_in_dim(full, chip * width, width, axis)
        out[n] = full
        w0 += size
    return out


def kernel(x, positions, even_w_in, even_a_ln_g, even_a_ln_b, even_a_ws, even_a_bs, even_b_sinks, even_w_out, even_ln_g, even_ln_b, odd_w_in, odd_conv_w, odd_conv_b, odd_w_a, odd_b_a, odd_w_x, odd_b_x, odd_lam, odd_w_pool, odd_d_scale, odd_w_out, odd_ln_g, odd_ln_b, loss_target, m_even_w_in, m_even_a_ln_g, m_even_a_ln_b, m_even_a_ws, m_even_a_bs, m_even_b_sinks, m_even_w_out, m_even_ln_g, m_even_ln_b, m_odd_w_in, m_odd_conv_w, m_odd_conv_b, m_odd_w_a, m_odd_b_a, m_odd_w_x, m_odd_b_x, m_odd_lam, m_odd_w_pool, m_odd_d_scale, m_odd_w_out, m_odd_ln_g, m_odd_ln_b, v_even_w_in, v_even_a_ln_g, v_even_a_ln_b, v_even_a_ws, v_even_a_bs, v_even_b_sinks, v_even_w_out, v_even_ln_g, v_even_ln_b, v_odd_w_in, v_odd_conv_w, v_odd_conv_b, v_odd_w_a, v_odd_b_a, v_odd_w_x, v_odd_b_x, v_odd_lam, v_odd_w_pool, v_odd_d_scale, v_odd_w_out, v_odd_ln_g, v_odd_ln_b):
    local = dict(zip(WEIGHT_NAMES, (even_w_in, even_a_ln_g, even_a_ln_b, even_a_ws, even_a_bs, even_b_sinks, even_w_out, even_ln_g, even_ln_b, odd_w_in, odd_conv_w, odd_conv_b, odd_w_a, odd_b_a, odd_w_x, odd_b_x, odd_lam, odd_w_pool, odd_d_scale, odd_w_out, odd_ln_g, odd_ln_b)))
    mom = dict(zip(WEIGHT_NAMES, (m_even_w_in, m_even_a_ln_g, m_even_a_ln_b, m_even_a_ws, m_even_a_bs, m_even_b_sinks, m_even_w_out, m_even_ln_g, m_even_ln_b, m_odd_w_in, m_odd_conv_w, m_odd_conv_b, m_odd_w_a, m_odd_b_a, m_odd_w_x, m_odd_b_x, m_odd_lam, m_odd_w_pool, m_odd_d_scale, m_odd_w_out, m_odd_ln_g, m_odd_ln_b)))
    vel = dict(zip(WEIGHT_NAMES, (v_even_w_in, v_even_a_ln_g, v_even_a_ln_b, v_even_a_ws, v_even_a_bs, v_even_b_sinks, v_even_w_out, v_even_ln_g, v_even_ln_b, v_odd_w_in, v_odd_conv_w, v_odd_conv_b, v_odd_w_a, v_odd_b_a, v_odd_w_x, v_odd_b_x, v_odd_lam, v_odd_w_pool, v_odd_d_scale, v_odd_w_out, v_odd_ln_g, v_odd_ln_b)))
    mx, my, core = _where_am_i()
    chip = 2 * mx + my

    stored = lambda n, a: jnp.swapaxes(a, 1, 2) if n == 'even_w_in' and W_IN_TRANSPOSED['even'] else a
    names_of = lambda layer: [('even' if layer % 2 == 0 else 'odd') + s for s in ('_w_in', '_w_out')]

    shards = [[stored(n, local[n])[layer // 2].astype(MXU_DTYPE) for n in names_of(layer)] for layer in range(DEPTH)]
    shards[0].insert(0, _pack_small_sharded(local))
    gathers, gather_token = _ici_start('gather', shards, "gather_start")

    def gathered(layer, after):
        own, lands = _ici_wait('gather', gathers[layer], [after], "gather_wait_%d" % layer)
        return _with_own_slot(_forward_to_sibling(lands, "gather_forward_%d" % layer), own, chip)

    first = gathered(0, gather_token)
    weights = dict(local)
    weights.update(_unpack_small_sharded(first[0]))
    projections_of = lambda layer, after: first[1:] if layer == 0 else gathered(layer, after)

    in_flight, mine = [], {}

    def land(*after):
        layer, scatter = in_flight.pop()
        parts, lands = _ici_wait('scatter', scatter, after, "scatter_wait_%d" % layer)
        for n, arrived in zip(names_of(layer), _with_own_slot(lands, parts, chip)):
            mine[n, layer // 2] = _sum_slots(arrived, "big_sum_%s_%d" % (n, layer // 2))

    def projection_grads(layer, dw_in, dw_out):
        if in_flight:
            land(dw_in)
        partial = [dw_in, dw_out]
        from_sibling = _sibling_swap(partial, "big_presum_swap_%d" % layer, other_half_of_axis1=True)
        chip_part = [_add_halves(g, r, core, MXU_DTYPE, "big_presum_add_%s_%d" % (n, layer // 2))
                     for g, r, n in zip(partial, from_sibling, names_of(layer))]
        (scatter,), started[layer] = _ici_start('scatter', [chip_part], "scatter_start_%d" % layer)
        in_flight.append((layer, scatter))
        return started[layer]

    started = {}
    loss, grad_x, grads = _local_step(x[0], positions[0], loss_target[0], weights, projections_of, projection_grads)

    small = _pack_small_grads(grads)
    chip_sum = _add2(small, _sibling_swap([small], "small_swap", after=[started[0]])[0], "small_presum_add")
    (small_gather,), small_token = _ici_start('gather', [[chip_sum]], "small_gather_start")

    reduced, deltas, new_m, new_v = {}, {}, {}, {}

    def adamw_projections(kind, after):
        names = [kind + '_w_in', kind + '_w_out']
        pieces = [(n, l) for n in names for l in range(2)]
        theirs = dict(zip(pieces, _sibling_swap([mine[p] for p in pieces], "big_join_" + kind, after=after)))
        done = []
        for n in names:
            outs = _adamw_projection(stored(n, local[n]), stored(n, mom[n]), stored(n, vel[n]),
                                     [mine[n, l] for l in range(2)], [theirs[n, l] for l in range(2)], core,
                                     "adamw_" + n)
            reduced[n], deltas[n], new_m[n], new_v[n] = [stored(n, o) for o in outs]
            done.append(outs[-1])
        return done

    done = adamw_projections('odd', [small_token])
    land(grad_x, *done)
    done = adamw_projections('even', [])

    own, lands = _ici_wait('gather', small_gather, done, "small_gather_wait")
    chip_sums = _with_own_slot(_forward_to_sibling(lands, "small_gather_forward"), own, chip)[0]
    reduced.update(_unpack_small_grads(_sum_slots(chip_sums, "small_sum"), chip))
    for n in SMALL_NAMES:
        deltas[n], new_m[n], new_v[n] = _adamw(local[n], reduced[n], mom[n], vel[n], "adamw_" + n)

    total_loss = lax.psum(loss[0, 0], ("x", "y", "c"))
    return (total_loss, grad_x[None], *[reduced[n] for n in WEIGHT_NAMES], *[deltas[n] for n in WEIGHT_NAMES],
            *[new_m[n] for n in WEIGHT_NAMES], *[new_v[n] for n in WEIGHT_NAMES])
```

```python
import functools
import math

import jax
import jax.numpy as jnp
from jax import lax
from jax.experimental import pallas as pl
from jax.experimental.pallas import tpu as pltpu

F32 = jnp.float32
BF16 = jnp.bfloat16
MXU_DTYPE = jnp.bfloat16

D_MODEL = 2048
DEPTH = 4
A_WIDTH = 1024
A_GROUPS = 8
CHUNK = 128
B_HEAD_DIM = 64
B_Q_HEADS = 16
B_KV_HEADS = 2
WINDOW = 128
ROT_DIM = 16
ROPE_THETA = 500000.0
C_WIDTH = 1024
C_HEADS = 8
CONV_WIDTH = 4
LRU_C = 8.0
D_WIDTH = 1024
D_GROUPS = 4
D_GROUP_DIM = 256
EVEN_IN = 5376
ODD_IN = 4096
DN_ALPHA = (2 * DEPTH) ** 0.25
LN_EPS = 1e-5
ATTN_SCALE = B_HEAD_DIM ** -0.5
NEG_BIG = -1e30

ADAM_LR = 0.001
ADAM_B1 = 0.9
ADAM_B2 = 0.999
ADAM_EPS = 1e-08
ADAM_WD = 0.01
ADAM_STEP = 10

LANES = 128
VMEM_LIMIT = 56 * 1024 * 1024
N_CHIPS = 4
MESH = pl.DeviceIdType.MESH

WEIGHT_NAMES = ['even_w_in', 'even_a_ln_g', 'even_a_ln_b', 'even_a_ws', 'even_a_bs', 'even_b_sinks', 'even_w_out',
                'even_ln_g', 'even_ln_b', 'odd_w_in', 'odd_conv_w', 'odd_conv_b', 'odd_w_a', 'odd_b_a', 'odd_w_x',
                'odd_b_x', 'odd_lam', 'odd_w_pool', 'odd_d_scale', 'odd_w_out', 'odd_ln_g', 'odd_ln_b']
BIG_SHARD_AXIS = {'even_w_in': 2, 'even_w_out': 1, 'odd_w_in': 2, 'odd_w_out': 1}
SMALL_SHARD_AXIS = {'odd_conv_w': 2, 'odd_conv_b': 1, 'odd_b_a': 1, 'odd_b_x': 1, 'odd_lam': 1, 'odd_w_pool': 2,
                    'odd_d_scale': 1, 'odd_ln_g': 1, 'odd_ln_b': 1}
FULL_SHAPES = {
    'even_w_in': (2, 2048, 5376), 'even_a_ln_g': (2, 1024), 'even_a_ln_b': (2, 1024), 'even_a_ws': (2, 8, 128, 128),
    'even_a_bs': (2, 8, 128), 'even_b_sinks': (2, 16), 'even_w_out': (2, 2048, 2048), 'even_ln_g': (2, 2048),
    'even_ln_b': (2, 2048), 'odd_w_in': (2, 2048, 4096), 'odd_conv_w': (2, 4, 1024), 'odd_conv_b': (2, 1024),
    'odd_w_a': (2, 8, 128, 128), 'odd_b_a': (2, 1024), 'odd_w_x': (2, 8, 128, 128), 'odd_b_x': (2, 1024),
    'odd_lam': (2, 1024), 'odd_w_pool': (2, 4, 256, 256), 'odd_d_scale': (2, 1024), 'odd_w_out': (2, 2048, 2048),
    'odd_ln_g': (2, 2048), 'odd_ln_b': (2, 2048)}
BIG_NAMES = ['even_w_in', 'even_w_out', 'odd_w_in', 'odd_w_out']
SMALL_SHARDED = ['odd_conv_w', 'odd_conv_b', 'odd_b_a', 'odd_b_x', 'odd_lam', 'odd_d_scale', 'odd_ln_g', 'odd_ln_b',
                 'odd_w_pool']
SMALL_NAMES = [n for n in WEIGHT_NAMES if n not in BIG_NAMES]
W_IN_TRANSPOSED = {'even': True, 'odd': False}
PACK_COLS = 1024


def _params(sem):
    return pltpu.CompilerParams(dimension_semantics=sem, vmem_limit_bytes=VMEM_LIMIT)


def _dot(a, b):
    return lax.dot_general(a, b, (((1,), (0,)), ((), ())), preferred_element_type=F32)


def _dot_nt(a, b):
    return lax.dot_general(a, b, (((1,), (1,)), ((), ())), preferred_element_type=F32)


def _dot_tn(a, b):
    return lax.dot_general(a, b, (((0,), (0,)), ((), ())), preferred_element_type=F32)


def _sigmoid(x):
    return 1.0 / (1.0 + jnp.exp(-x))


def _silu_and_grad(x):
    s = _sigmoid(x)
    return x * s, s * (1.0 + x * (1.0 - s))


def _mm_call(body, name, grid, in_specs, out_spec, out_shape, args):
    return pl.pallas_call(body, name=name, grid=grid, in_specs=in_specs, out_specs=out_spec, out_shape=out_shape,
                          compiler_params=_params(("parallel",) * len(grid)))(*args)


def _mm_in(x, wg, transposed, name):
    s, k = x.shape
    nc = wg.shape[1] if transposed else wg.shape[2]
    tm = min(1024, s)

    def body(x_ref, w_ref, o_ref):
        o_ref[...] = (_dot_nt if transposed else _dot)(x_ref[...].astype(MXU_DTYPE), w_ref[...])

    return _mm_call(body, name, (s // tm, N_CHIPS),
                    [pl.BlockSpec((tm, k), lambda i, j: (i, 0)),
                     pl.BlockSpec((None,) + wg.shape[1:], lambda i, j: (j, 0, 0))],
                    pl.BlockSpec((None, tm, nc), lambda i, j: (j, i, 0)),
                    jax.ShapeDtypeStruct((N_CHIPS, s, nc), F32), (x, wg))


def _mm_dx(dh, wg, dz, transposed, name):
    s, d = dz.shape
    nc = dh.shape[2]
    tm, tn = min(512, s), 512

    def body(a_ref, w_ref, r_ref, o_ref):
        acc = DN_ALPHA * r_ref[...]
        for j in range(N_CHIPS):
            acc = acc + (_dot if transposed else _dot_nt)(a_ref[j], w_ref[j])
        o_ref[...] = acc

    w_spec = (pl.BlockSpec((N_CHIPS, nc, tn), lambda i, j: (0, 0, j)) if transposed
              else pl.BlockSpec((N_CHIPS, tn, nc), lambda i, j: (0, j, 0)))
    return _mm_call(body, name, (s // tm, d // tn),
                    [pl.BlockSpec((N_CHIPS, tm, nc), lambda i, j: (0, i, 0)), w_spec,
                     pl.BlockSpec((tm, tn), lambda i, j: (i, j))],
                    pl.BlockSpec((tm, tn), lambda i, j: (i, j)), jax.ShapeDtypeStruct((s, d), F32), (dh, wg, dz))


def _mm_dw_in(x, dh, transposed, name):
    s, d = x.shape
    nc = dh.shape[2]
    tm = 1024

    def body(x_ref, b_ref, o_ref):
        xb = x_ref[...].astype(MXU_DTYPE)
        acc = _dot_tn(b_ref[...], xb) if transposed else _dot_tn(xb, b_ref[...])
        o_ref[...] = acc.astype(o_ref.dtype)

    if transposed:
        out_spec, shape = pl.BlockSpec((None, nc, tm), lambda i, j: (j, 0, i)), (N_CHIPS, nc, d)
    else:
        out_spec, shape = pl.BlockSpec((None, tm, nc), lambda i, j: (j, i, 0)), (N_CHIPS, d, nc)
    return _mm_call(body, name, (d // tm, N_CHIPS),
                    [pl.BlockSpec((s, tm), lambda i, j: (0, i)), pl.BlockSpec((None, s, nc), lambda i, j: (j, 0, 0))],
                    out_spec, jax.ShapeDtypeStruct(shape, MXU_DTYPE), (x, dh))


def _mm_out(mix, wg, name):
    s, k = mix.shape
    _, kc, d = wg.shape
    tm, tn = min(1024, s), 1024

    def body(a_ref, w_ref, o_ref):
        o_ref[...] = _dot(a_ref[...], w_ref[...].reshape(N_CHIPS * kc, tn))

    return _mm_call(body, name, (s // tm, d // tn),
                    [pl.BlockSpec((tm, k), lambda i, j: (i, 0)), pl.BlockSpec((N_CHIPS, kc, tn), lambda i, j: (0, 0, j))],
                    pl.BlockSpec((tm, tn), lambda i, j: (i, j)), jax.ShapeDtypeStruct((s, d), F32), (mix, wg))


def _mm_dmix(dz, wg, name):
    s, d = dz.shape
    kc = wg.shape[1]
    tm = min(1024, s)

    def body(a_ref, w_ref, o_ref):
        o_ref[...] = _dot_nt(a_ref[...], w_ref[...])

    return _mm_call(body, name, (s // tm, N_CHIPS),
                    [pl.BlockSpec((tm, d), lambda i, j: (i, 0)), pl.BlockSpec((None, kc, d), lambda i, j: (j, 0, 0))],
                    pl.BlockSpec((tm, kc), lambda i, j: (i, j)), jax.ShapeDtypeStruct((s, N_CHIPS * kc), F32), (dz, wg))


def _mm_dw_out(mix, dz, name):
    s, k = mix.shape
    d = dz.shape[1]
    kc, tn = k // N_CHIPS, 1024

    def body(a_ref, b_ref, o_ref):
        o_ref[...] = _dot_tn(a_ref[...], b_ref[...]).astype(o_ref.dtype)

    return _mm_call(body, name, (N_CHIPS, d // tn),
                    [pl.BlockSpec((s, kc), lambda i, j: (0, i)), pl.BlockSpec((s, tn), lambda i, j: (0, j))],
                    pl.BlockSpec((None, kc, tn), lambda i, j: (i, 0, j)),
                    jax.ShapeDtypeStruct((N_CHIPS, kc, d), MXU_DTYPE), (mix, dz))


def _ln_fwd(x, y, g, b):
    s, d = x.shape
    tr = min(256, s)

    def body(x_ref, y_ref, g_ref, b_ref, z_ref, o_ref, ob_ref):
        z = DN_ALPHA * x_ref[...] + y_ref[...]
        mu = jnp.mean(z, axis=1, keepdims=True)
        zc = z - mu
        var = jnp.mean(zc * zc, axis=1, keepdims=True)
        o = zc * lax.rsqrt(var + LN_EPS) * g_ref[...] + b_ref[...]
        z_ref[...] = z
        o_ref[...] = o
        ob_ref[...] = o.astype(MXU_DTYPE)

    row = pl.BlockSpec((tr, d), lambda i: (i, 0))
    vec = pl.BlockSpec((1, d), lambda i: (0, 0))
    return pl.pallas_call(
        body, name="ln_fwd", grid=(s // tr,), in_specs=[row, row, vec, vec], out_specs=[row, row, row],
        out_shape=[jax.ShapeDtypeStruct((s, d), F32), jax.ShapeDtypeStruct((s, d), F32),
                   jax.ShapeDtypeStruct((s, d), MXU_DTYPE)],
        compiler_params=_params(("parallel",)),
    )(x, y, g.reshape(1, d), b.reshape(1, d))


def _ln_bwd(dout, z, g, after):
    s, d = z.shape
    tr = min(256, s)

    def body(do_ref, z_ref, g_ref, after_ref, dz_ref, dzb_ref, dg_ref, db_ref):
        @pl.when(pl.program_id(0) == 0)
        def _():
            dg_ref[...] = jnp.zeros_like(dg_ref)
            db_ref[...] = jnp.zeros_like(db_ref)

        zz = z_ref[...]
        do = do_ref[...]
        mu = jnp.mean(zz, axis=1, keepdims=True)
        zc = zz - mu
        var = jnp.mean(zc * zc, axis=1, keepdims=True)
        rstd = lax.rsqrt(var + LN_EPS)
        zh = zc * rstd
        dg_ref[...] += jnp.sum(do * zh, axis=0, keepdims=True)
        db_ref[...] += jnp.sum(do, axis=0, keepdims=True)
        dzh = do * g_ref[...]
        dz = rstd * (dzh - jnp.mean(dzh, axis=1, keepdims=True) - zh * jnp.mean(dzh * zh, axis=1, keepdims=True))
        dz_ref[...] = dz
        dzb_ref[...] = dz.astype(MXU_DTYPE)

    row = pl.BlockSpec((tr, d), lambda i: (i, 0))
    vec = pl.BlockSpec((1, d), lambda i: (0, 0))
    return pl.pallas_call(
        body, name="ln_bwd", grid=(s // tr,), in_specs=[row, row, vec, ANY], out_specs=[row, row, vec, vec],
        out_shape=[jax.ShapeDtypeStruct((s, d), F32), jax.ShapeDtypeStruct((s, d), MXU_DTYPE),
                   jax.ShapeDtypeStruct((1, d), F32), jax.ShapeDtypeStruct((1, d), F32)],
        compiler_params=_params(("arbitrary",)),
    )(dout, z, g.reshape(1, d), after)


def _loss_and_grad(out, tgt):
    s, d = out.shape
    tr = min(256, s)

    def body(o_ref, t_ref, l_ref, do_ref):
        @pl.when(pl.program_id(0) == 0)
        def _():
            l_ref[...] = jnp.zeros_like(l_ref)

        e = o_ref[...] - t_ref[...]
        do_ref[...] = e * (1.0 / d)
        rows = jnp.sum(e * e, axis=1, keepdims=True)
        l_ref[...] += (0.5 / d) * jnp.sum(rows, axis=0, keepdims=True)

    row = pl.BlockSpec((tr, d), lambda i: (i, 0))
    return pl.pallas_call(
        body, name="loss", grid=(s // tr,), in_specs=[row, row],
        out_specs=[pl.BlockSpec((1, 1), lambda i: (0, 0)), row],
        out_shape=[jax.ShapeDtypeStruct((1, 1), F32), jax.ShapeDtypeStruct((s, d), F32)],
        compiler_params=_params(("arbitrary",)),
    )(out, tgt)


Q0, K0, V0, BG0 = 3072, 4096, 4224, 4352
T = CHUNK


def _lane_ids(width):
    return lax.broadcasted_iota(jnp.int32, (T, width), 1)


def _rope_swap(x):
    w = x.shape[1]
    l64 = _lane_ids(w) % B_HEAD_DIM
    half = ROT_DIM // 2
    return jnp.where(l64 < half, pltpu.roll(x, w - half, 1), jnp.where(l64 < ROT_DIM, pltpu.roll(x, half, 1), 0.0))


def _rope(x, c, s):
    return x * c + _rope_swap(x) * s


def _rope_bwd(d, c, s):
    return d * c + _rope_swap(d * s)


def _tril():
    return lax.broadcasted_iota(jnp.int32, (T, T), 0) >= lax.broadcasted_iota(jnp.int32, (T, T), 1)


def _band_mask(has_prev):
    qi = lax.broadcasted_iota(jnp.int32, (T, 2 * T), 0)
    kj = lax.broadcasted_iota(jnp.int32, (T, 2 * T), 1)
    return (kj > qi) & (kj <= qi + WINDOW) & ((kj >= WINDOW) | has_prev)


def _a_norm(v, g, b):
    mu = jnp.mean(v, axis=1, keepdims=True)
    vc = v - mu
    var = jnp.mean(vc * vc, axis=1, keepdims=True)
    rstd = lax.rsqrt(var + LN_EPS)
    vh = vc * rstd
    return vh, rstd, vh * g + b


def _kv_operands(hk, k_rot, kp_rot, vv, vp):
    lo = _lane_ids(LANES) < B_HEAD_DIM
    mine = lo if hk == 0 else jnp.logical_not(lo)
    kc = jnp.where(mine, k_rot, 0.0)
    kp = jnp.where(mine, kp_rot, 0.0)
    k2 = jnp.concatenate([kp + pltpu.roll(kp, B_HEAD_DIM, 1), kc + pltpu.roll(kc, B_HEAD_DIM, 1)], axis=0)
    vm = jnp.concatenate([jnp.where(mine, vp, 0.0), jnp.where(mine, vv, 0.0)], axis=0)
    vs = pltpu.roll(vm, B_HEAD_DIM, 1)
    v_lo, v_hi = (vm, vs) if hk == 0 else (vs, vm)
    return k2.astype(MXU_DTYPE), v_lo.astype(MXU_DTYPE), v_hi.astype(MXU_DTYPE), (vm + vs).astype(MXU_DTYPE)


GROUP_HEADS = B_Q_HEADS // B_KV_HEADS


def _pair_columns(hk):
    return [(hk * GROUP_HEADS // 2 + pp) * LANES for pp in range(GROUP_HEADS // 2)]


def _stacked_heads(pairs):
    lo = _lane_ids(LANES) < B_HEAD_DIM
    rows = []
    for tile in pairs:
        rows += [jnp.where(lo, tile, 0.0), jnp.where(lo, 0.0, tile)]
    return jnp.concatenate(rows, axis=0).astype(MXU_DTYPE)


def _pair_of(stacked, pp):
    lo = _lane_ids(LANES) < B_HEAD_DIM
    return jnp.where(lo, stacked[2 * pp * T:(2 * pp + 1) * T], stacked[(2 * pp + 1) * T:(2 * pp + 2) * T])


def _softmax_with_sink(qm, k2, valid, sink_ref, head0):
    scores = (_dot_nt(qm, k2) * ATTN_SCALE).reshape(GROUP_HEADS, T, 2 * T)
    s = jnp.where(valid[None], scores, NEG_BIG).reshape(GROUP_HEADS * T, 2 * T)
    sink = jnp.concatenate([jnp.broadcast_to(sink_ref[:, head0 + h:head0 + h + 1], (T, 1))
                            for h in range(GROUP_HEADS)], axis=0)
    m = jnp.maximum(jnp.max(s, axis=1, keepdims=True), sink)
    ex = jnp.exp(s - m)
    es = jnp.exp(sink - m)
    inv = 1.0 / (jnp.sum(ex, axis=1, keepdims=True) + es)
    return ex * inv, es * inv


EVEN_SHARD = EVEN_IN // N_CHIPS
KV_CHIP = K0 // EVEN_SHARD
KV_SPAN = 3 * LANES
KP0, VP0 = K0 - KV_CHIP * EVEN_SHARD, V0 - KV_CHIP * EVEN_SHARD
assert V0 // EVEN_SHARD == KV_CHIP and VP0 + LANES <= KV_SPAN


def _even_specs(s):
    nb = s // T
    return nb, {
        'h': lambda r: pl.BlockSpec((N_CHIPS, T, EVEN_SHARD), lambda i: (0, r(i), 0)),
        'kvprev': lambda r: pl.BlockSpec((None, T, KV_SPAN), lambda i: (KV_CHIP, jnp.maximum(r(i) - 1, 0), 0)),
        'tab': lambda r: pl.BlockSpec((T, A_WIDTH), lambda i: (r(i), 0)),
        'tabprev': lambda r: pl.BlockSpec((T, LANES), lambda i: (jnp.maximum(r(i) - 1, 0), 0)),
    }


def _rows_from_shards(h4_ref, rows_ref):
    for j in range(N_CHIPS):
        rows_ref[:, j * EVEN_SHARD:(j + 1) * EVEN_SHARD] = h4_ref[j]


def _full(shape):
    nd = len(shape)
    return pl.BlockSpec(shape, lambda i: (0,) * nd)


def _even_fwd(h4, cq, sq, ln_g, ln_b, ws, bs_t, sinks):
    s = h4.shape[1]
    nb, sp = _even_specs(s)

    def body(h4_ref, kvp_ref, cq_ref, sq_ref, ckp_ref, skp_ref, g_ref, b_ref, ws_ref, bst_ref, sink_ref, mix_ref, h_ref):
        i = pl.program_id(0)
        _rows_from_shards(h4_ref, h_ref)
        tril = _tril()
        _, _, vn = _a_norm(h_ref[:, 1024:2048], g_ref[...], b_ref[...])
        vnb = vn.astype(MXU_DTYPE)
        for g in range(A_GROUPS):
            c0, c1 = g * LANES, (g + 1) * LANES
            wg = jnp.where(tril, ws_ref[g], 0.0).astype(MXU_DTYPE)
            mixed = _dot(wg, vnb[:, c0:c1]) + bst_ref[:, g:g + 1]
            sil, _ = _silu_and_grad(h_ref[:, 2048 + c0:2048 + c1])
            mix_ref[:, c0:c1] = (h_ref[:, c0:c1] * mixed * sil).astype(mix_ref.dtype)

        k_rot = _rope(h_ref[:, K0:K0 + LANES], cq_ref[:, 0:LANES], sq_ref[:, 0:LANES])
        kp_rot = _rope(kvp_ref[:, KP0:KP0 + LANES], ckp_ref[...], skp_ref[...])
        vp = kvp_ref[:, VP0:VP0 + LANES]
        valid = _band_mask(i > 0)
        for hk in range(B_KV_HEADS):
            k2, _, _, v2 = _kv_operands(hk, k_rot, kp_rot, h_ref[:, V0:V0 + LANES], vp)
            qm = _stacked_heads([_rope(h_ref[:, Q0 + c0:Q0 + c0 + LANES], cq_ref[:, c0:c0 + LANES],
                                       sq_ref[:, c0:c0 + LANES]) for c0 in _pair_columns(hk)])
            p, _ = _softmax_with_sink(qm, k2, valid, sink_ref, GROUP_HEADS * hk)
            out = _dot(p.astype(MXU_DTYPE), v2)
            for pp, c0 in enumerate(_pair_columns(hk)):
                sil, _ = _silu_and_grad(h_ref[:, BG0 + c0:BG0 + c0 + LANES])
                mix_ref[:, A_WIDTH + c0:A_WIDTH + c0 + LANES] = (_pair_of(out, pp) * sil).astype(mix_ref.dtype)

    ident = lambda i: i
    return pl.pallas_call(
        body, name="even_fwd", grid=(nb,),
        in_specs=[sp['h'](ident), sp['kvprev'](ident), sp['tab'](ident), sp['tab'](ident),
                  sp['tabprev'](ident), sp['tabprev'](ident), _full((1, A_WIDTH)), _full((1, A_WIDTH)),
                  _full((A_GROUPS, T, T)), _full((T, A_GROUPS)), _full((1, B_Q_HEADS))],
        out_specs=pl.BlockSpec((T, 2 * A_WIDTH), lambda i: (i, 0)),
        out_shape=jax.ShapeDtypeStruct((s, 2 * A_WIDTH), MXU_DTYPE),
        scratch_shapes=[pltpu.VMEM((T, EVEN_IN), F32)],
        compiler_params=_params(("parallel",)),
    )(h4, h4, cq, sq, cq, sq, ln_g.reshape(1, -1), ln_b.reshape(1, -1), ws, bs_t, sinks.reshape(1, -1))


def _even_bwd(h4, dmix, cq, sq, ln_g, ln_b, ws, bs_t, sinks):
    s = h4.shape[1]
    nb, sp = _even_specs(s)

    def body(h4_ref, kvp_ref, cq_ref, sq_ref, ckp_ref, skp_ref, dm_ref, g_ref, b_ref, ws_ref, bst_ref, sink_ref,
             dh4_ref, dg_ref, db_ref, dws_ref, dbst_ref, dsink_ref, h_ref, dh_ref, dvn_scr, ck_scr, cv_scr):
        i = pl.program_id(0)

        @pl.when(i == 0)
        def _():
            for ref in (dg_ref, db_ref, dws_ref, dbst_ref, dsink_ref, ck_scr, cv_scr):
                ref[...] = jnp.zeros_like(ref)

        _rows_from_shards(h4_ref, h_ref)

        tril = _tril()
        vh, rstd, vn = _a_norm(h_ref[:, 1024:2048], g_ref[...], b_ref[...])
        vnb = vn.astype(MXU_DTYPE)
        for g in range(A_GROUPS):
            c0, c1 = g * LANES, (g + 1) * LANES
            wg = jnp.where(tril, ws_ref[g], 0.0).astype(MXU_DTYPE)
            mixed = _dot(wg, vnb[:, c0:c1]) + bst_ref[:, g:g + 1]
            ag = h_ref[:, 2048 + c0:2048 + c1]
            u = h_ref[:, c0:c1]
            sil, dsil = _silu_and_grad(ag)
            da = dm_ref[:, c0:c1]
            dh_ref[:, c0:c1] = (da * mixed * sil).astype(dh_ref.dtype)
            dh_ref[:, 2048 + c0:2048 + c1] = (da * u * mixed * dsil).astype(dh_ref.dtype)
            dmx = da * u * sil
            dmb = dmx.astype(MXU_DTYPE)
            dvn_scr[:, c0:c1] = _dot_tn(wg, dmb)
            dws_ref[g] += jnp.where(tril, _dot_nt(dmb, vnb[:, c0:c1]), 0.0)
            dbst_ref[:, g:g + 1] += jnp.sum(dmx, axis=1, keepdims=True)
        dvn = dvn_scr[...]
        dg_ref[...] += jnp.sum(dvn * vh, axis=0, keepdims=True)
        db_ref[...] += jnp.sum(dvn, axis=0, keepdims=True)
        dvh = dvn * g_ref[...]
        dv = rstd * (dvh - jnp.mean(dvh, axis=1, keepdims=True) - vh * jnp.mean(dvh * vh, axis=1, keepdims=True))
        dh_ref[:, 1024:2048] = dv.astype(dh_ref.dtype)

        ck_cur, sk_cur = cq_ref[:, 0:LANES], sq_ref[:, 0:LANES]
        k_rot = _rope(h_ref[:, K0:K0 + LANES], ck_cur, sk_cur)
        kp_rot = _rope(kvp_ref[:, KP0:KP0 + LANES], ckp_ref[...], skp_ref[...])
        vp = kvp_ref[:, VP0:VP0 + LANES]
        valid = _band_mask(i < nb - 1)
        dk_all = jnp.zeros((2 * T, LANES), F32)
        dv_all = jnp.zeros((2 * T, LANES), F32)
        for hk in range(B_KV_HEADS):
            k2, _, _, v2 = _kv_operands(hk, k_rot, kp_rot, h_ref[:, V0:V0 + LANES], vp)
            cols = _pair_columns(hk)
            qm = _stacked_heads([_rope(h_ref[:, Q0 + c0:Q0 + c0 + LANES], cq_ref[:, c0:c0 + LANES],
                                       sq_ref[:, c0:c0 + LANES]) for c0 in cols])
            p, psink = _softmax_with_sink(qm, k2, valid, sink_ref, GROUP_HEADS * hk)
            pb = p.astype(MXU_DTYPE)
            out = _dot(pb, v2)
            dobs = []
            for pp, c0 in enumerate(cols):
                sil, dsil = _silu_and_grad(h_ref[:, BG0 + c0:BG0 + c0 + LANES])
                dbo = dm_ref[:, A_WIDTH + c0:A_WIDTH + c0 + LANES]
                dh_ref[:, BG0 + c0:BG0 + c0 + LANES] = dbo * _pair_of(out, pp) * dsil
                dobs.append(dbo * sil)
            dobm = _stacked_heads(dobs)
            dp = _dot_nt(dobm, v2)
            rs = jnp.sum(p * dp, axis=1, keepdims=True)
            dsb = (p * (dp - rs) * ATTN_SCALE).astype(MXU_DTYPE)
            to_sink = psink * rs
            for h in range(GROUP_HEADS):
                head = GROUP_HEADS * hk + h
                dsink_ref[:, head:head + 1] += -jnp.sum(to_sink[h * T:(h + 1) * T], axis=0, keepdims=True)
            dq = _dot(dsb, k2)
            for pp, c0 in enumerate(cols):
                dh_ref[:, Q0 + c0:Q0 + c0 + LANES] = _rope_bwd(_pair_of(dq, pp), cq_ref[:, c0:c0 + LANES],
                                                               sq_ref[:, c0:c0 + LANES])
            acc_k = _dot_tn(dsb, qm)
            acc_v = _dot_tn(pb, dobm)
            lo2 = lax.broadcasted_iota(jnp.int32, (2 * T, LANES), 1) < B_HEAD_DIM
            mine = lo2 if hk == 0 else jnp.logical_not(lo2)
            dk_all = dk_all + jnp.where(mine, acc_k + pltpu.roll(acc_k, B_HEAD_DIM, 1), 0.0)
            dv_all = dv_all + jnp.where(mine, acc_v + pltpu.roll(acc_v, B_HEAD_DIM, 1), 0.0)
        dk_rot = dk_all[T:2 * T] + ck_scr[...]
        dvv = dv_all[T:2 * T] + cv_scr[...]
        ck_scr[...] = dk_all[0:T]
        cv_scr[...] = dv_all[0:T]
        dh_ref[:, K0:K0 + LANES] = _rope_bwd(dk_rot, ck_cur, sk_cur).astype(dh_ref.dtype)
        dh_ref[:, V0:V0 + LANES] = dvv.astype(dh_ref.dtype)
        for j in range(N_CHIPS):
            dh4_ref[j] = dh_ref[:, j * EVEN_SHARD:(j + 1) * EVEN_SHARD].astype(dh4_ref.dtype)

    rev = lambda i: nb - 1 - i
    return pl.pallas_call(
        body, name="even_bwd", grid=(nb,),
        in_specs=[sp['h'](rev), sp['kvprev'](rev), sp['tab'](rev), sp['tab'](rev),
                  sp['tabprev'](rev), sp['tabprev'](rev), pl.BlockSpec((T, 2 * A_WIDTH), lambda i: (rev(i), 0)),
                  _full((1, A_WIDTH)), _full((1, A_WIDTH)), _full((A_GROUPS, T, T)), _full((T, A_GROUPS)),
                  _full((1, B_Q_HEADS))],
        out_specs=[sp['h'](rev), _full((1, A_WIDTH)), _full((1, A_WIDTH)),
                   _full((A_GROUPS, T, T)), _full((T, A_GROUPS)), _full((1, B_Q_HEADS))],
        out_shape=[jax.ShapeDtypeStruct((N_CHIPS, s, EVEN_SHARD), MXU_DTYPE), jax.ShapeDtypeStruct((1, A_WIDTH), F32),
                   jax.ShapeDtypeStruct((1, A_WIDTH), F32), jax.ShapeDtypeStruct((A_GROUPS, T, T), F32),
                   jax.ShapeDtypeStruct((T, A_GROUPS), F32), jax.ShapeDtypeStruct((1, B_Q_HEADS), F32)],
        scratch_shapes=[pltpu.VMEM((T, EVEN_IN), F32), pltpu.VMEM((T, EVEN_IN), F32), pltpu.VMEM((T, A_WIDTH), F32),
                        pltpu.VMEM((T, LANES), F32), pltpu.VMEM((T, LANES), F32)],
        compiler_params=_params(("arbitrary",)),
    )(h4, h4, cq, sq, cq, sq, dmix, ln_g.reshape(1, -1), ln_b.reshape(1, -1), ws, bs_t, sinks.reshape(1, -1))


def _shift_down(x, d, fill=0.0):
    rows = lax.broadcasted_iota(jnp.int32, x.shape, 0)
    return jnp.where(rows >= d, pltpu.roll(x, d, 0), fill)


def _shift_up(x, d, fill=0.0):
    n = x.shape[0]
    rows = lax.broadcasted_iota(jnp.int32, x.shape, 0)
    return jnp.where(rows < n - d, pltpu.roll(x, n - d, 0), fill)


SCAN_BLOCK = 128
SUBLANES = 8


def _scan_block(a, b, up):
    n = a.shape[0]
    d = 1
    while d < n:
        if d < SUBLANES:
            shift = _shift_up if up else _shift_down
            a, b = a * shift(a, d, 1.0), a * shift(b, d) + b
        elif up:
            a, b = (jnp.concatenate([a[:n - d] * a[d:], a[n - d:]], axis=0),
                    jnp.concatenate([a[:n - d] * b[d:] + b[:n - d], b[n - d:]], axis=0))
        else:
            a, b = (jnp.concatenate([a[:d], a[d:] * a[:n - d]], axis=0),
                    jnp.concatenate([b[:d], a[d:] * b[:n - d] + b[d:]], axis=0))
        d *= 2
    return a, b


def _scan(a, b, up):
    n = a.shape[0]
    size = min(SCAN_BLOCK, n)
    blocks = n // size
    out, carry = [None] * blocks, None
    for k in (reversed(range(blocks)) if up else range(blocks)):
        prod, h = _scan_block(a[k * size:(k + 1) * size], b[k * size:(k + 1) * size], up)
        if carry is not None:
            h = h + prod * carry
        carry = h[0:1] if up else h[size - 1:size]
        out[k] = h
    return jnp.concatenate(out, axis=0)


def _neg_expm1(y):
    u = jnp.exp(y)
    um1 = u - 1.0
    safe = jnp.where(um1 == 0.0, 1.0, jnp.log(u))
    near = jnp.where(um1 == 0.0, y, um1 * y / safe)
    return -jnp.where(y > -0.5, near, um1)


def _softplus(x):
    return jnp.maximum(x, 0.0) + jnp.log1p(jnp.exp(-jnp.abs(x)))


def _lru_forward(xc, cw, cb, wa, ba, wx, bx, lam):
    shifted = [_shift_down(xc, CONV_WIDTH - 1 - j) if j < CONV_WIDTH - 1 else xc for j in range(CONV_WIDTH)]
    xconv = cb
    for j in range(CONV_WIDTH):
        xconv = xconv + cw[j:j + 1, :] * shifted[j]
    xb = xconv.astype(MXU_DTYPE)
    r = _sigmoid(_dot(xb, wa.astype(MXU_DTYPE)) + ba)
    ig = _sigmoid(_dot(xb, wx.astype(MXU_DTYPE)) + bx)
    sp = _softplus(-lam)
    log_a = -LRU_C * r * sp
    a = jnp.exp(log_a)
    mult = jnp.sqrt(_neg_expm1(2.0 * log_a))
    hs = _scan(a, mult * ig * xconv, up=False)
    return dict(shifted=shifted, xconv=xconv, xb=xb, r=r, ig=ig, sp=sp, a=a, mult=mult, hs=hs)


def _col(width, off):
    return lambda s: pl.BlockSpec((s, width), lambda i: (0, off + i))


def _part(width, slot):
    return lambda s: pl.BlockSpec((None, s, width), lambda i: (slot, 0, i))


def _lru_specs(s):
    w = LANES
    return [_part(w, 0)(s), _part(w, 1)(s), pl.BlockSpec((CONV_WIDTH, w), lambda i: (0, i)),
            pl.BlockSpec((1, w), lambda i: (0, i)), pl.BlockSpec((None, w, w), lambda i: (i, 0, 0)),
            pl.BlockSpec((1, w), lambda i: (0, i)), pl.BlockSpec((None, w, w), lambda i: (i, 0, 0)),
            pl.BlockSpec((1, w), lambda i: (0, i)), pl.BlockSpec((1, w), lambda i: (0, i))]


def _lru_fwd(h, cw, cb, wa, ba, wx, bx, lam):
    s = h.shape[1]

    def body(xc_ref, cg_ref, cw_ref, cb_ref, wa_ref, ba_ref, wx_ref, bx_ref, lam_ref, o_ref):
        f = _lru_forward(xc_ref[...], cw_ref[...], cb_ref[...], wa_ref[...], ba_ref[...], wx_ref[...], bx_ref[...],
                         lam_ref[...])
        sil, _ = _silu_and_grad(cg_ref[...])
        o_ref[...] = (f['hs'] * sil).astype(o_ref.dtype)

    return pl.pallas_call(
        body, name="lru_fwd", grid=(C_HEADS,), in_specs=_lru_specs(s), out_specs=_col(LANES, 0)(s),
        out_shape=jax.ShapeDtypeStruct((s, C_WIDTH + D_WIDTH), MXU_DTYPE), compiler_params=_params(("parallel",)),
    )(h, h, cw, cb.reshape(1, -1), wa, ba.reshape(1, -1), wx, bx.reshape(1, -1), lam.reshape(1, -1))


def _lru_bwd(h, dmix, cw, cb, wa, ba, wx, bx, lam):
    s = h.shape[1]

    def body(xc_ref, cg_ref, cw_ref, cb_ref, wa_ref, ba_ref, wx_ref, bx_ref, lam_ref, dco_ref,
             dh_ref, dcw_ref, dcb_ref, dwa_ref, dba_ref, dwx_ref, dbx_ref, dlam_ref):
        dxc_ref, dcg_ref = dh_ref.at[0], dh_ref.at[1]
        cw, lam = cw_ref[...], lam_ref[...]
        wab, wxb = wa_ref[...].astype(MXU_DTYPE), wx_ref[...].astype(MXU_DTYPE)
        f = _lru_forward(xc_ref[...], cw, cb_ref[...], wa_ref[...], ba_ref[...], wx_ref[...], bx_ref[...], lam)
        sil, dsil = _silu_and_grad(cg_ref[...])
        dco = dco_ref[...]
        hs, a, mult, ig, r, xconv = f['hs'], f['a'], f['mult'], f['ig'], f['r'], f['xconv']
        dcg_ref[...] = (dco * hs * dsil).astype(dcg_ref.dtype)
        lamb = _scan(_shift_up(a, 1), dco * sil, up=True)
        da = lamb * _shift_down(hs, 1)
        d_ig = lamb * mult * xconv
        d_mult = lamb * ig * xconv
        dxconv = lamb * mult * ig
        d_log_a = da * a - d_mult * (a * a) / mult
        dlam_ref[...] = jnp.sum(d_log_a * r, axis=0, keepdims=True) * LRU_C * _sigmoid(-lam)
        dpa = d_log_a * (-LRU_C * f['sp']) * r * (1.0 - r)
        dpx = d_ig * ig * (1.0 - ig)
        dba_ref[...] = jnp.sum(dpa, axis=0, keepdims=True)
        dbx_ref[...] = jnp.sum(dpx, axis=0, keepdims=True)
        dpab, dpxb = dpa.astype(MXU_DTYPE), dpx.astype(MXU_DTYPE)
        dwa_ref[...] = _dot_tn(f['xb'], dpab)
        dwx_ref[...] = _dot_tn(f['xb'], dpxb)
        dxconv = dxconv + _dot_nt(dpab, wab) + _dot_nt(dpxb, wxb)
        dcb_ref[...] = jnp.sum(dxconv, axis=0, keepdims=True)
        dxc = jnp.zeros_like(dxconv)
        for j in range(CONV_WIDTH):
            dcw_ref[j:j + 1, :] = jnp.sum(dxconv * f['shifted'][j], axis=0, keepdims=True)
            back = _shift_up(dxconv, CONV_WIDTH - 1 - j) if j < CONV_WIDTH - 1 else dxconv
            dxc = dxc + cw[j:j + 1, :] * back
        dxc_ref[...] = dxc.astype(dxc_ref.dtype)

    w = LANES
    vec = pl.BlockSpec((1, w), lambda i: (0, i))
    mat = pl.BlockSpec((None, w, w), lambda i: (i, 0, 0))
    vshape = jax.ShapeDtypeStruct((1, C_WIDTH), F32)
    mshape = jax.ShapeDtypeStruct((C_HEADS, w, w), F32)
    return pl.pallas_call(
        body, name="lru_bwd", grid=(C_HEADS,), in_specs=_lru_specs(s) + [_col(w, 0)(s)],
        out_specs=[pl.BlockSpec((2, s, w), lambda i: (0, 0, i)), pl.BlockSpec((CONV_WIDTH, w), lambda i: (0, i)),
                   vec, mat, vec, mat, vec, vec],
        out_shape=[jax.ShapeDtypeStruct((N_CHIPS, s, C_WIDTH), MXU_DTYPE),
                   jax.ShapeDtypeStruct((CONV_WIDTH, C_WIDTH), F32), vshape, mshape, vshape, mshape, vshape, vshape],
        compiler_params=_params(("parallel",)),
    )(h, h, cw, cb.reshape(1, -1), wa, ba.reshape(1, -1), wx, bx.reshape(1, -1), lam.reshape(1, -1), dmix)


def _pool_window(g):
    return jnp.left_shift(2, g).astype(F32)


def _select_window(g, sums):
    out = sums[-1]
    for k in range(len(sums) - 2, -1, -1):
        out = jnp.where(g == k, sums[k], out)
    return out


def _window_sums(x, shift):
    sums, cur = [], x
    for k in range(D_GROUPS):
        cur = cur + shift(cur, 2 ** k)
        sums.append(cur)
    return sums


def _pool_specs(s):
    w = D_GROUP_DIM
    return [_part(w, 2)(s), _part(w, 3)(s),
            pl.BlockSpec((None, w, w), lambda i: (i, 0, 0)), pl.BlockSpec((1, w), lambda i: (0, i))]


def _pool_forward(xd, g):
    rows = lax.broadcasted_iota(jnp.int32, xd.shape, 0).astype(F32) + 1.0
    cnt = jnp.minimum(rows, _pool_window(g))
    pooled = _select_window(g, _window_sums(xd, _shift_down)) / cnt - xd
    return pooled, cnt


def _pool_fwd(h, mix, w_pool, d_scale):
    s = h.shape[1]

    def body(xd_ref, dg_ref, wp_ref, sc_ref, mix_ref, o_ref):
        pooled, _ = _pool_forward(xd_ref[...], pl.program_id(0))
        mixed = _dot(pooled.astype(MXU_DTYPE), wp_ref[...].astype(MXU_DTYPE))
        sil, _ = _silu_and_grad(dg_ref[...])
        o_ref[...] = (mixed * sc_ref[...] * sil).astype(o_ref.dtype)

    return pl.pallas_call(
        body, name="pool_fwd", grid=(D_GROUPS,), in_specs=_pool_specs(s) + [ANY],
        out_specs=_col(D_GROUP_DIM, C_WIDTH // D_GROUP_DIM)(s), out_shape=jax.ShapeDtypeStruct(mix.shape, mix.dtype),
        input_output_aliases={4: 0}, compiler_params=_params(("parallel",)),
    )(h, h, w_pool, d_scale.reshape(1, -1), mix)


def _pool_bwd(h, dmix, dh, w_pool, d_scale):
    s = h.shape[1]
    w = D_GROUP_DIM

    def body(xd_ref, dg_ref, wp_ref, sc_ref, ddo_ref, dh_in_ref, dh_ref, dwp_ref, dsc_ref):
        dxd_ref, ddg_ref = dh_ref.at[0], dh_ref.at[1]
        g = pl.program_id(0)
        pooled, cnt = _pool_forward(xd_ref[...], g)
        pb = pooled.astype(MXU_DTYPE)
        wpb = wp_ref[...].astype(MXU_DTYPE)
        mixed = _dot(pb, wpb)
        sil, dsil = _silu_and_grad(dg_ref[...])
        ddo = ddo_ref[...]
        sc = sc_ref[...]
        ddg_ref[...] = (ddo * mixed * sc * dsil).astype(ddg_ref.dtype)
        dms = ddo * sil
        dsc_ref[...] = jnp.sum(dms * mixed, axis=0, keepdims=True)
        dmb = (dms * sc).astype(MXU_DTYPE)
        dwp_ref[...] = _dot_tn(pb, dmb)
        dpooled = _dot_nt(dmb, wpb)
        dxd = _select_window(g, _window_sums(dpooled / cnt, _shift_up)) - dpooled
        dxd_ref[...] = dxd.astype(dxd_ref.dtype)

    return pl.pallas_call(
        body, name="pool_bwd", grid=(D_GROUPS,), in_specs=_pool_specs(s) + [_col(w, C_WIDTH // w)(s), ANY],
        out_specs=[pl.BlockSpec((2, s, w), lambda i: (1, 0, i)), pl.BlockSpec((None, w, w), lambda i: (i, 0, 0)),
                   pl.BlockSpec((1, w), lambda i: (0, i))],
        out_shape=[jax.ShapeDtypeStruct(dh.shape, dh.dtype),
                   jax.ShapeDtypeStruct((D_GROUPS, w, w), F32), jax.ShapeDtypeStruct((1, D_WIDTH), F32)],
        input_output_aliases={5: 0}, compiler_params=_params(("parallel",)),
    )(h, h, w_pool, d_scale.reshape(1, -1), dmix, dh)


def _rope_tables(positions):
    s = positions.shape[0]
    inv_freq = ROPE_THETA ** (-jnp.arange(0, ROT_DIM, 2, dtype=F32) / ROT_DIM)
    ang = positions.astype(F32)[:, None] * inv_freq
    c, sn = jnp.cos(ang), jnp.sin(ang)
    rest = B_HEAD_DIM - ROT_DIM
    c64 = jnp.concatenate([c, c, jnp.ones((s, rest), F32)], axis=1)
    s64 = jnp.concatenate([-sn, sn, jnp.zeros((s, rest), F32)], axis=1)
    return jnp.tile(c64, (1, B_Q_HEADS)), jnp.tile(s64, (1, B_Q_HEADS))


def _local_step(x, positions, target, w, projections_of, projection_grads):
    cq, sq = _rope_tables(positions)
    saved = []
    xin, xin_b = x, x
    for layer in range(DEPTH):
        j = layer // 2
        kind = 'even' if layer % 2 == 0 else 'odd'
        w_in, w_out = projections_of(layer, xin_b)
        h = _mm_in(xin_b, w_in, W_IN_TRANSPOSED[kind], "mm_" + kind + "_in")
        if kind == 'even':
            mix = _even_fwd(h, cq, sq, w['even_a_ln_g'][j], w['even_a_ln_b'][j], w['even_a_ws'][j],
                            w['even_a_bs'][j].T, w['even_b_sinks'][j])
        else:
            mix = _lru_fwd(h, w['odd_conv_w'][j], w['odd_conv_b'][j], w['odd_w_a'][j], w['odd_b_a'][j],
                           w['odd_w_x'][j], w['odd_b_x'][j], w['odd_lam'][j])
            mix = _pool_fwd(h, mix, w['odd_w_pool'][j], w['odd_d_scale'][j])
        y = _mm_out(mix, w_out, "mm_out")
        z, out, out_b = _ln_fwd(xin, y, w[kind + '_ln_g'][j], w[kind + '_ln_b'][j])
        saved.append((xin_b, h, mix, z, w_in, w_out))
        xin, xin_b = out, out_b

    loss, dout = _loss_and_grad(xin, target)
    per_layer = {n: [None, None] for n in SMALL_NAMES}
    behind = loss
    for layer in reversed(range(DEPTH)):
        j = layer // 2
        kind = 'even' if layer % 2 == 0 else 'odd'
        xin_b, h, mix, z, w_in, w_out = saved[layer]
        dz, dz_b, dg, db = _ln_bwd(dout, z, w[kind + '_ln_g'][j], behind)
        per_layer[kind + '_ln_g'][j], per_layer[kind + '_ln_b'][j] = dg[0], db[0]
        dmix = _mm_dmix(dz_b, w_out, "mm_dmix")
        dw_out = _mm_dw_out(mix, dz_b, "mm_dw_out")
        if kind == 'even':
            dh, dlg, dlb, dws, dbst, dsink = _even_bwd(
                h, dmix, cq, sq, w['even_a_ln_g'][j], w['even_a_ln_b'][j], w['even_a_ws'][j], w['even_a_bs'][j].T,
                w['even_b_sinks'][j])
            for n, val in (('even_a_ln_g', dlg[0]), ('even_a_ln_b', dlb[0]), ('even_a_ws', dws),
                           ('even_a_bs', dbst.T), ('even_b_sinks', dsink[0])):
                per_layer[n][j] = val
        else:
            dh, dcw, dcb, dwa, dba, dwx, dbx, dlam = _lru_bwd(
                h, dmix, w['odd_conv_w'][j], w['odd_conv_b'][j], w['odd_w_a'][j], w['odd_b_a'][j], w['odd_w_x'][j],
                w['odd_b_x'][j], w['odd_lam'][j])
            dh, dwp, dsc = _pool_bwd(h, dmix, dh, w['odd_w_pool'][j], w['odd_d_scale'][j])
            for n, val in (('odd_conv_w', dcw), ('odd_conv_b', dcb[0]), ('odd_w_a', dwa), ('odd_b_a', dba[0]),
                           ('odd_w_x', dwx), ('odd_b_x', dbx[0]), ('odd_lam', dlam[0]), ('odd_w_pool', dwp),
                           ('odd_d_scale', dsc[0])):
                per_layer[n][j] = val
        dout = _mm_dx(dh, w_in, dz, W_IN_TRANSPOSED[kind], "mm_dx_" + kind)
        dw_in = _mm_dw_in(xin_b, dh, W_IN_TRANSPOSED[kind], "mm_dw_in_" + kind)
        behind = projection_grads(layer, dw_in, dw_out)
    return loss, dout, {n: jnp.stack(v) for n, v in per_layer.items()}


ANY = pl.BlockSpec(memory_space=pl.ANY)


def _where_am_i():
    return lax.axis_index("x"), lax.axis_index("y"), lax.axis_index("c")


def _other_chips(x, y):
    return [(1 - x, y), (x, 1 - y), (1 - x, 1 - y)]


def _half_rows(ref_or_array_rows, which):
    half = ref_or_array_rows // 2
    return pl.ds(pl.multiple_of(which * half, 16), half)


def _two_level_gather(bufs, chip, name):
    n = len(bufs)

    def body(*refs):
        x_refs, out_refs, (send_sems, recv_sems) = refs[:n], refs[n:2 * n], refs[2 * n:]
        x, y, c = _where_am_i()
        sibling = (x, y, 1 - c)
        chips = _other_chips(x, y)

        def rows(p, px, py, pc):
            return out_refs[p].at[2 * px + py, _half_rows(bufs[p].shape[0], pc), :]

        def copy(p, k, block, to, src=None):
            return pltpu.make_async_remote_copy(
                src_ref=rows(p, *block) if src is None else src, dst_ref=rows(p, *block),
                send_sem=send_sems.at[6 * p + k], recv_sem=recv_sems.at[6 * p + k], device_id=to, device_id_type=MESH)

        first = [copy(p, j, (x, y, c), (*chip, c), src=x_refs[p].at[_half_rows(bufs[p].shape[0], c), :])
                 for p in range(n) for j, chip in enumerate(chips)]
        for cp in first:
            cp.start()
        passed = []
        for p in range(n):
            for j, chip in enumerate(chips):
                copy(p, j, (*chip, c), (x, y, c)).wait_recv()
                passed.append(copy(p, 3 + j, (*chip, c), sibling))
                passed[-1].start()
        for p in range(n):
            for j, chip in enumerate(chips):
                copy(p, 3 + j, (*chip, 1 - c), (x, y, c)).wait_recv()
        for cp in first + passed:
            cp.wait_send()

    others = pl.pallas_call(
        body, name=name, in_specs=[ANY] * n, out_specs=[ANY] * n,
        out_shape=[jax.ShapeDtypeStruct((N_CHIPS,) + b.shape, b.dtype) for b in bufs],
        scratch_shapes=[pltpu.SemaphoreType.DMA((6 * n,)), pltpu.SemaphoreType.DMA((6 * n,))],
    )(*bufs)
    return [lax.dynamic_update_slice(o, b[None], (chip, 0, 0)) for o, b in zip(others, bufs)]


def _sibling_swap(bufs, name, *, other_half_of_axis1=False, after=()):
    n, na = len(bufs), len(after)
    out_shapes = [(b.shape[0], b.shape[1] // 2, b.shape[2]) if other_half_of_axis1 else b.shape for b in bufs]

    def body(*refs):
        x_refs, out_refs, (send_sems, recv_sems) = refs[:n], refs[n + na:2 * n + na], refs[2 * n + na:]
        x, y, c = _where_am_i()
        copies = []
        for p in range(n):
            src = x_refs[p].at[:, _half_rows(bufs[p].shape[1], 1 - c), :] if other_half_of_axis1 else x_refs[p]
            copies.append(pltpu.make_async_remote_copy(
                src_ref=src, dst_ref=out_refs[p], send_sem=send_sems.at[p], recv_sem=recv_sems.at[p],
                device_id=(x, y, 1 - c), device_id_type=MESH))
            copies[-1].start()
        for cp in copies:
            cp.wait()

    return pl.pallas_call(
        body, name=name, in_specs=[ANY] * (n + na), out_specs=[ANY] * n,
        out_shape=[jax.ShapeDtypeStruct(s, b.dtype) for s, b in zip(out_shapes, bufs)],
        scratch_shapes=[pltpu.SemaphoreType.DMA((n,)), pltpu.SemaphoreType.DMA((n,))],
    )(*bufs, *after)


HBM = pl.BlockSpec(memory_space=pltpu.HBM)
SEM = pl.BlockSpec(memory_space=pltpu.SEMAPHORE)
IN_FLIGHT = pltpu.CompilerParams(has_side_effects=pltpu.SideEffectType.DATAFLOW_SIDE_EFFECTING)


def _ici_copy(kind, src_ref, land_ref, send_sems, recv_sems, k, peer, x, y, c):
    px, py = peer
    me = 2 * x + y
    if kind == 'gather':
        half = _half_rows(src_ref.shape[0], c)
        src, dst = src_ref.at[half, :], land_ref.at[me, half, :]
    else:
        src, dst = src_ref.at[2 * px + py], land_ref.at[me]
    return pltpu.make_async_remote_copy(src_ref=src, dst_ref=dst, send_sem=send_sems.at[k], recv_sem=recv_sems.at[k],
                                        device_id=(px, py, c), device_id_type=MESH)


def _ici_start(kind, groups, name):
    flat = [a for g in groups for a in g]
    n, ng = len(flat), len(groups)
    land_shapes = [((N_CHIPS,) + a.shape) if kind == 'gather' else a.shape for a in flat]

    def body(*refs):
        srcs, lands, sems, token = refs[:n], refs[n:2 * n], refs[2 * n:2 * n + 2 * ng], refs[-1]
        x, y, c = _where_am_i()
        p = 0
        for gi, g in enumerate(groups):
            for q in range(len(g)):
                for k, peer in enumerate(_other_chips(x, y)):
                    _ici_copy(kind, srcs[p], lands[p], sems[2 * gi], sems[2 * gi + 1], 3 * q + k, peer, x, y, c).start()
                p += 1
        token[...] = jnp.zeros_like(token)

    sem_shapes = [pltpu.SemaphoreType.DMA((3 * len(g),)) for g in groups for _ in range(2)]
    outs = pl.pallas_call(
        body, name=name, in_specs=[HBM] * (2 * n),
        out_specs=[SEM] * (2 * ng) + [HBM] * (2 * n) + [pl.BlockSpec(memory_space=pltpu.VMEM)],
        out_shape=sem_shapes + [pltpu.HBM(a.shape, a.dtype) for a in flat]
        + [pltpu.HBM(s, a.dtype) for s, a in zip(land_shapes, flat)] + [jax.ShapeDtypeStruct((8, LANES), F32)],
        input_output_aliases={i: 2 * ng + i for i in range(2 * n)}, compiler_params=IN_FLIGHT,
    )(*[pltpu.with_memory_space_constraint(a, pltpu.HBM) for a in flat],
      *[pltpu.with_memory_space_constraint(lax.empty(s, a.dtype), pltpu.HBM) for s, a in zip(land_shapes, flat)])
    sems, srcs, lands, token = outs[:2 * ng], outs[2 * ng:2 * ng + n], outs[2 * ng + n:2 * ng + 2 * n], outs[-1]
    started, p = [], 0
    for gi, g in enumerate(groups):
        started.append((sems[2 * gi], sems[2 * gi + 1], srcs[p:p + len(g)], lands[p:p + len(g)]))
        p += len(g)
    return started, token


def _ici_wait(kind, started, after, name):
    send_sems, recv_sems, srcs, lands = started
    n = len(srcs)
    after = list(after)

    def body(*refs):
        src_refs, land_refs, send_ref, recv_ref = refs[:n], refs[n:2 * n], refs[2 * n], refs[2 * n + 1]
        x, y, c = _where_am_i()
        for q in range(n):
            for k, peer in enumerate(_other_chips(x, y)):
                cp = _ici_copy(kind, src_refs[q], land_refs[q], send_ref, recv_ref, 3 * q + k, peer, x, y, c)
                cp.wait_send()
                cp.wait_recv()

    outs = pl.pallas_call(
        body, name=name, in_specs=[HBM] * (2 * n) + [SEM, SEM] + [ANY] * len(after), out_specs=[HBM] * (2 * n),
        out_shape=[pltpu.HBM(a.shape, a.dtype) for a in list(srcs) + list(lands)],
        input_output_aliases={i: i for i in range(2 * n)}, compiler_params=IN_FLIGHT,
    )(*srcs, *lands, send_sems, recv_sems, *after)
    return outs[:n], outs[n:]


def _forward_to_sibling(lands, name):
    n = len(lands)

    def body(*refs):
        out_refs, (send_sems, recv_sems) = refs[n:2 * n], refs[2 * n:]
        x, y, c = _where_am_i()

        def copy(q, k, peer, half_of):
            rows = out_refs[q].at[2 * peer[0] + peer[1], _half_rows(lands[q].shape[1], half_of), :]
            return pltpu.make_async_remote_copy(
                src_ref=rows, dst_ref=rows, send_sem=send_sems.at[3 * q + k], recv_sem=recv_sems.at[3 * q + k],
                device_id=(x, y, 1 - c), device_id_type=MESH)

        sends = [copy(q, k, peer, c) for q in range(n) for k, peer in enumerate(_other_chips(x, y))]
        for cp in sends:
            cp.start()
        for q in range(n):
            for k, peer in enumerate(_other_chips(x, y)):
                copy(q, k, peer, 1 - c).wait_recv()
        for cp in sends:
            cp.wait_send()

    return pl.pallas_call(
        body, name=name, in_specs=[ANY] * n, out_specs=[ANY] * n,
        out_shape=[jax.ShapeDtypeStruct(a.shape, a.dtype) for a in lands],
        input_output_aliases={i: i for i in range(n)},
        scratch_shapes=[pltpu.SemaphoreType.DMA((3 * n,)), pltpu.SemaphoreType.DMA((3 * n,))],
    )(*lands)


def _with_own_slot(lands, own, chip):
    out = []
    for land, mine in zip(lands, own):
        mine = lax.dynamic_slice_in_dim(mine, chip, 1, axis=0) if mine.ndim == land.ndim else mine[None]
        out.append(lax.dynamic_update_slice(land, mine, (chip,) + (0,) * (land.ndim - 1)))
    return out


def _row_tile(rows, row_bytes, n_bufs, budget=24 * 1024 * 1024):
    best = None
    for t in range(16, rows + 1, 16):
        if rows % t == 0 and 2 * n_bufs * t * row_bytes <= budget:
            best = t
    assert best is not None, (rows, row_bytes)
    return best


def _add_halves(full, recv, core, out_dtype, name):
    n, h, cols = recv.shape
    tr = _row_tile(h, cols * 4, 3)
    nt = h // tr

    def body(core_ref, a_ref, b_ref, o_ref):
        o_ref[...] = (a_ref[...].astype(F32) + b_ref[...].astype(F32)).astype(out_dtype)

    blk = (None, tr, cols)
    return pl.pallas_call(
        body, name=name,
        grid_spec=pltpu.PrefetchScalarGridSpec(
            num_scalar_prefetch=1, grid=(n, nt),
            in_specs=[pl.BlockSpec(blk, lambda j, i, core_ref: (j, core_ref[0] * nt + i, 0)),
                      pl.BlockSpec(blk, lambda j, i, core_ref: (j, i, 0))],
            out_specs=pl.BlockSpec(blk, lambda j, i, core_ref: (j, i, 0))),
        out_shape=jax.ShapeDtypeStruct((n, h, cols), out_dtype),
        compiler_params=_params(("parallel", "parallel")),
    )(core.reshape(1), full, recv)


def _add2(a, b, name):
    r, cols = a.shape
    tr = _row_tile(r, cols * 4, 3)

    def body(a_ref, b_ref, o_ref):
        o_ref[...] = a_ref[...] + b_ref[...]

    blk = pl.BlockSpec((tr, cols), lambda i: (i, 0))
    return pl.pallas_call(body, name=name, grid=(r // tr,), in_specs=[blk, blk], out_specs=blk,
                          out_shape=jax.ShapeDtypeStruct(a.shape, a.dtype), compiler_params=_params(("parallel",)))(a, b)


def _sum_slots(parts, name):
    n, h, cols = parts.shape
    tr = _row_tile(h, cols * 4, n + 1)

    def body(p_ref, o_ref):
        acc = p_ref[0].astype(F32)
        for k in range(1, n):
            acc = acc + p_ref[k].astype(F32)
        o_ref[...] = acc

    return pl.pallas_call(
        body, name=name, grid=(h // tr,), in_specs=[pl.BlockSpec((n, tr, cols), lambda i: (0, i, 0))],
        out_specs=pl.BlockSpec((tr, cols), lambda i: (i, 0)), out_shape=jax.ShapeDtypeStruct((h, cols), F32),
        compiler_params=_params(("parallel",)),
    )(parts)


def _adamw(w, g, m, v, name):
    shape = w.shape
    cols = shape[-1]
    rows = math.prod(shape[:-1])
    tr = rows
    if rows % 8 == 0:
        tr = 8
        for t in range(8, rows + 1, 8):
            if rows % t == 0 and t * cols * 4 <= 1536 * 1024:
                tr = t
    c1 = 1.0 - ADAM_B1 ** ADAM_STEP
    c2 = 1.0 - ADAM_B2 ** ADAM_STEP

    def body(w_ref, g_ref, m_ref, v_ref, d_ref, nm_ref, nv_ref):
        gg = g_ref[...]
        nm = ADAM_B1 * m_ref[...] + (1.0 - ADAM_B1) * gg
        nv = ADAM_B2 * v_ref[...] + (1.0 - ADAM_B2) * (gg * gg)
        d_ref[...] = -ADAM_LR * ((nm / c1) / (jnp.sqrt(nv / c2) + ADAM_EPS) + ADAM_WD * w_ref[...])
        nm_ref[...] = nm
        nv_ref[...] = nv

    blk = pl.BlockSpec((tr, cols), lambda i: (i, 0))
    flat = jax.ShapeDtypeStruct((rows, cols), F32)
    outs = pl.pallas_call(
        body, name=name, grid=(rows // tr,), in_specs=[blk] * 4, out_specs=[blk] * 3, out_shape=[flat] * 3,
        compiler_params=_params(("parallel",)),
    )(*[a.reshape(rows, cols) for a in (w, g, m, v)])
    return [o.reshape(shape) for o in outs]


def _adamw_update(w, g, m, v):
    c1 = 1.0 - ADAM_B1 ** ADAM_STEP
    c2 = 1.0 - ADAM_B2 ** ADAM_STEP
    nm = ADAM_B1 * m + (1.0 - ADAM_B1) * g
    nv = ADAM_B2 * v + (1.0 - ADAM_B2) * (g * g)
    return -ADAM_LR * ((nm / c1) / (jnp.sqrt(nv / c2) + ADAM_EPS) + ADAM_WD * w), nm, nv


def _adamw_projection(w, m, v, mine, theirs, core, name):
    _, r, cols = w.shape
    h = r // 2
    tr = _row_tile(h, cols * 4, 12, budget=40 * 1024 * 1024)
    nt = h // tr

    def body(core_ref, w_ref, m_ref, v_ref, a0, b0, a1, b1, g_ref, d_ref, nm_ref, nv_ref):
        layer, half = pl.program_id(0), pl.program_id(1)
        own = half == core_ref[0]
        g = jnp.where(layer == 0, jnp.where(own, a0[...], b0[...]), jnp.where(own, a1[...], b1[...]))
        g_ref[...] = g
        d_ref[...], nm_ref[...], nv_ref[...] = _adamw_update(w_ref[...], g, m_ref[...], v_ref[...])

    def piece(layer, own):
        def index(l, hh, i, core_ref):
            used = (l == layer) & ((hh == core_ref[0]) == own)
            return (jnp.where(used, i, 0), 0)
        return pl.BlockSpec((tr, cols), index)

    whole = pl.BlockSpec((None, tr, cols), lambda l, hh, i, core_ref: (l, hh * nt + i, 0))
    return pl.pallas_call(
        body, name=name,
        grid_spec=pltpu.PrefetchScalarGridSpec(
            num_scalar_prefetch=1, grid=(2, 2, nt),
            in_specs=[whole] * 3 + [piece(0, True), piece(0, False), piece(1, True), piece(1, False)],
            out_specs=[whole] * 4),
        out_shape=[jax.ShapeDtypeStruct(w.shape, F32)] * 4,
        compiler_params=_params(("arbitrary", "arbitrary", "arbitrary")),
    )(core.reshape(1), w, m, v, mine[0], theirs[0], mine[1], theirs[1])


def _chip_shape(name):
    shape = list(FULL_SHAPES[name])
    axis = BIG_SHARD_AXIS.get(name, SMALL_SHARD_AXIS.get(name))
    if axis is not None:
        shape[axis] //= N_CHIPS
    return tuple(shape)


def _round_up(n, k):
    return -(-n // k) * k


SHARDED_SMALL_WORDS = sum(math.prod(_chip_shape(n)) for n in SMALL_SHARDED)
SHARDED_SMALL_ROWS = _round_up(-(-SHARDED_SMALL_WORDS // PACK_COLS), 32)
SMALL_WORDS = sum(math.prod(FULL_SHAPES[n]) for n in SMALL_NAMES)
SMALL_ROWS = _round_up(-(-SMALL_WORDS // PACK_COLS), 32)


def _pack_rows(flat, rows):
    return jnp.pad(flat, (0, rows * PACK_COLS - flat.shape[0])).reshape(rows, PACK_COLS)


def _pack_small_sharded(shards):
    return _pack_rows(jnp.concatenate([shards[n].reshape(-1) for n in SMALL_SHARDED]), SHARDED_SMALL_ROWS)


def _unpack_small_sharded(gathered_small):
    out = {}
    words = gathered_small.reshape(N_CHIPS, -1)
    w0 = 0
    for n in SMALL_SHARDED:
        size = math.prod(_chip_shape(n))
        blocks = words[:, w0:w0 + size].reshape((N_CHIPS,) + _chip_shape(n))
        out[n] = jnp.concatenate([blocks[j] for j in range(N_CHIPS)], axis=SMALL_SHARD_AXIS[n])
        w0 += size
    return out


def _pack_small_grads(grads):
    return _pack_rows(jnp.concatenate([grads[n].astype(F32).reshape(-1) for n in SMALL_NAMES]), SMALL_ROWS)


def _unpack_small_grads(total, chip):
    flat = total.reshape(-1)
    out = {}
    w0 = 0
    for n in SMALL_NAMES:
        size = math.prod(FULL_SHAPES[n])
        full = flat[w0:w0 + size].reshape(FULL_SHAPES[n])
        if n in SMALL_SHARD_AXIS:
            axis = SMALL_SHARD_AXIS[n]
            width = FULL_SHAPES[n][axis] // N_CHIPS
            full = lax.dynamic_slice_in_dim(full, chip * width, width, axis)
        out[n] = full
        w0 += size
    return out


def kernel(x, positions, even_w_in, even_a_ln_g, even_a_ln_b, even_a_ws, even_a_bs, even_b_sinks, even_w_out, even_ln_g, even_ln_b, odd_w_in, odd_conv_w, odd_conv_b, odd_w_a, odd_b_a, odd_w_x, odd_b_x, odd_lam, odd_w_pool, odd_d_scale, odd_w_out, odd_ln_g, odd_ln_b, loss_target, m_even_w_in, m_even_a_ln_g, m_even_a_ln_b, m_even_a_ws, m_even_a_bs, m_even_b_sinks, m_even_w_out, m_even_ln_g, m_even_ln_b, m_odd_w_in, m_odd_conv_w, m_odd_conv_b, m_odd_w_a, m_odd_b_a, m_odd_w_x, m_odd_b_x, m_odd_lam, m_odd_w_pool, m_odd_d_scale, m_odd_w_out, m_odd_ln_g, m_odd_ln_b, v_even_w_in, v_even_a_ln_g, v_even_a_ln_b, v_even_a_ws, v_even_a_bs, v_even_b_sinks, v_even_w_out, v_even_ln_g, v_even_ln_b, v_odd_w_in, v_odd_conv_w, v_odd_conv_b, v_odd_w_a, v_odd_b_a, v_odd_w_x, v_odd_b_x, v_odd_lam, v_odd_w_pool, v_odd_d_scale, v_odd_w_out, v_odd_ln_g, v_odd_ln_b):
    local = dict(zip(WEIGHT_NAMES, (even_w_in, even_a_ln_g, even_a_ln_b, even_a_ws, even_a_bs, even_b_sinks, even_w_out, even_ln_g, even_ln_b, odd_w_in, odd_conv_w, odd_conv_b, odd_w_a, odd_b_a, odd_w_x, odd_b_x, odd_lam, odd_w_pool, odd_d_scale, odd_w_out, odd_ln_g, odd_ln_b)))
    mom = dict(zip(WEIGHT_NAMES, (m_even_w_in, m_even_a_ln_g, m_even_a_ln_b, m_even_a_ws, m_even_a_bs, m_even_b_sinks, m_even_w_out, m_even_ln_g, m_even_ln_b, m_odd_w_in, m_odd_conv_w, m_odd_conv_b, m_odd_w_a, m_odd_b_a, m_odd_w_x, m_odd_b_x, m_odd_lam, m_odd_w_pool, m_odd_d_scale, m_odd_w_out, m_odd_ln_g, m_odd_ln_b)))
    vel = dict(zip(WEIGHT_NAMES, (v_even_w_in, v_even_a_ln_g, v_even_a_ln_b, v_even_a_ws, v_even_a_bs, v_even_b_sinks, v_even_w_out, v_even_ln_g, v_even_ln_b, v_odd_w_in, v_odd_conv_w, v_odd_conv_b, v_odd_w_a, v_odd_b_a, v_odd_w_x, v_odd_b_x, v_odd_lam, v_odd_w_pool, v_odd_d_scale, v_odd_w_out, v_odd_ln_g, v_odd_ln_b)))
    mx, my, core = _where_am_i()
    chip = 2 * mx + my

    stored = lambda n, a: jnp.swapaxes(a, 1, 2) if n == 'even_w_in' and W_IN_TRANSPOSED['even'] else a
    names_of = lambda layer: [('even' if layer % 2 == 0 else 'odd') + s for s in ('_w_in', '_w_out')]

    shards = [[stored(n, local[n])[layer // 2].astype(MXU_DTYPE) for n in names_of(layer)] for layer in range(DEPTH)]
    shards[0].insert(0, _pack_small_sharded(local))
    gathers, gather_token = _ici_start('gather', shards, "gather_start")

    def gathered(layer, after):
        own, lands = _ici_wait('gather', gathers[layer], [after], "gather_wait_%d" % layer)
        return _with_own_slot(_forward_to_sibling(lands, "gather_forward_%d" % layer), own, chip)

    first = gathered(0, gather_token)
    weights = dict(local)
    weights.update(_unpack_small_sharded(first[0]))
    projections_of = lambda layer, after: first[1:] if layer == 0 else gathered(layer, after)

    in_flight, mine = [], {}

    def land(*after):
        layer, scatter = in_flight.pop()
        parts, lands = _ici_wait('scatter', scatter, after, "scatter_wait_%d" % layer)
        for n, arrived in zip(names_of(layer), _with_own_slot(lands, parts, chip)):
            mine[n, layer // 2] = _sum_slots(arrived, "big_sum_%s_%d" % (n, layer // 2))

    def projection_grads(layer, dw_in, dw_out):
        if in_flight:
            land(dw_in)
        partial = [dw_in, dw_out]
        from_sibling = _sibling_swap(partial, "big_presum_swap_%d" % layer, other_half_of_axis1=True)
        chip_part = [_add_halves(g, r, core, MXU_DTYPE, "big_presum_add_%s_%d" % (n, layer // 2))
                     for g, r, n in zip(partial, from_sibling, names_of(layer))]
        (scatter,), started[layer] = _ici_start('scatter', [chip_part], "scatter_start_%d" % layer)
        in_flight.append((layer, scatter))
        return started[layer]

    started = {}
    loss, grad_x, grads = _local_step(x[0], positions[0], loss_target[0], weights, projections_of, projection_grads)

    small = _pack_small_grads(grads)
    chip_sum = _add2(small, _sibling_swap([small], "small_swap", after=[started[0]])[0], "small_presum_add")
    (small_gather,), small_token = _ici_start('gather', [[chip_sum]], "small_gather_start")

    reduced, deltas, new_m, new_v = {}, {}, {}, {}

    def adamw_projections(kind, after):
        names = [kind + '_w_in', kind + '_w_out']
        pieces = [(n, l) for n in names for l in range(2)]
        theirs = dict(zip(pieces, _sibling_swap([mine[p] for p in pieces], "big_join_" + kind, after=after)))
        done = []
        for n in names:
            outs = _adamw_projection(stored(n, local[n]), stored(n, mom[n]), stored(n, vel[n]),
                                     [mine[n, l] for l in range(2)], [theirs[n, l] for l in range(2)], core,
                                     "adamw_" + n)
            reduced[n], deltas[n], new_m[n], new_v[n] = [stored(n, o) for o in outs]
            done.append(outs[-1])
        return done

    done = adamw_projections('odd', [small_token])
    land(grad_x, *done)
    done = adamw_projections('even', [])

    own, lands = _ici_wait('gather', small_gather, done, "small_gather_wait")
    chip_sums = _with_own_slot(_forward_to_sibling(lands, "small_gather_forward"), own, chip)[0]
    reduced.update(_unpack_small_grads(_sum_slots(chip_sums, "small_sum"), chip))
    for n in SMALL_NAMES:
        deltas[n], new_m[n], new_v[n] = _adamw(local[n], reduced[n], mom[n], vel[n], "adamw_" + n)

    total_loss = lax.psum(loss[0, 0], ("x", "y", "c"))
    return (total_loss, grad_x[None], *[reduced[n] for n in WEIGHT_NAMES], *[deltas[n] for n in WEIGHT_NAMES],
            *[new_m[n] for n in WEIGHT_NAMES], *[new_v[n] for n in WEIGHT_NAMES])
```

```python
import math

import jax
import jax.numpy as jnp
from jax import lax
from jax.experimental import pallas as pl
from jax.experimental.pallas import tpu as pltpu

F32 = jnp.float32
BF16 = jnp.bfloat16
MXU_DTYPE = jnp.bfloat16

D_MODEL = 2048
DEPTH = 4
A_WIDTH = 1024
A_GROUPS = 8
CHUNK = 128
B_HEAD_DIM = 64
B_Q_HEADS = 16
B_KV_HEADS = 2
WINDOW = 128
ROT_DIM = 16
ROPE_THETA = 500000.0
C_WIDTH = 1024
C_HEADS = 8
CONV_WIDTH = 4
LRU_C = 8.0
D_WIDTH = 1024
D_GROUPS = 4
D_GROUP_DIM = 256
EVEN_IN = 5376
ODD_IN = 4096
DN_ALPHA = (2 * DEPTH) ** 0.25
LN_EPS = 1e-5
ATTN_SCALE = B_HEAD_DIM ** -0.5
NEG_BIG = -1e30

ADAM_LR = 0.001
ADAM_B1 = 0.9
ADAM_B2 = 0.999
ADAM_EPS = 1e-08
ADAM_WD = 0.01
ADAM_STEP = 10

LANES = 128
VMEM_LIMIT = 56 * 1024 * 1024
N_CHIPS = 4
MESH = pl.DeviceIdType.MESH

WEIGHT_NAMES = ['even_w_in', 'even_a_ln_g', 'even_a_ln_b', 'even_a_ws', 'even_a_bs', 'even_b_sinks', 'even_w_out',
                'even_ln_g', 'even_ln_b', 'odd_w_in', 'odd_conv_w', 'odd_conv_b', 'odd_w_a', 'odd_b_a', 'odd_w_x',
                'odd_b_x', 'odd_lam', 'odd_w_pool', 'odd_d_scale', 'odd_w_out', 'odd_ln_g', 'odd_ln_b']
BIG_SHARD_AXIS = {'even_w_in': 2, 'even_w_out': 1, 'odd_w_in': 2, 'odd_w_out': 1}
SMALL_SHARD_AXIS = {'odd_conv_w': 2, 'odd_conv_b': 1, 'odd_b_a': 1, 'odd_b_x': 1, 'odd_lam': 1, 'odd_w_pool': 2,
                    'odd_d_scale': 1, 'odd_ln_g': 1, 'odd_ln_b': 1}
FULL_SHAPES = {
    'even_w_in': (2, 2048, 5376), 'even_a_ln_g': (2, 1024), 'even_a_ln_b': (2, 1024), 'even_a_ws': (2, 8, 128, 128),
    'even_a_bs': (2, 8, 128), 'even_b_sinks': (2, 16), 'even_w_out': (2, 2048, 2048), 'even_ln_g': (2, 2048),
    'even_ln_b': (2, 2048), 'odd_w_in': (2, 2048, 4096), 'odd_conv_w': (2, 4, 1024), 'odd_conv_b': (2, 1024),
    'odd_w_a': (2, 8, 128, 128), 'odd_b_a': (2, 1024), 'odd_w_x': (2, 8, 128, 128), 'odd_b_x': (2, 1024),
    'odd_lam': (2, 1024), 'odd_w_pool': (2, 4, 256, 256), 'odd_d_scale': (2, 1024), 'odd_w_out': (2, 2048, 2048),
    'odd_ln_g': (2, 2048), 'odd_ln_b': (2, 2048)}
BIG_NAMES = ['even_w_in', 'even_w_out', 'odd_w_in', 'odd_w_out']
SMALL_SHARDED = ['odd_conv_w', 'odd_conv_b', 'odd_b_a', 'odd_b_x', 'odd_lam', 'odd_d_scale', 'odd_ln_g', 'odd_ln_b',
                 'odd_w_pool']
SMALL_NAMES = [n for n in WEIGHT_NAMES if n not in BIG_NAMES]
W_IN_TRANSPOSED = {'even': True, 'odd': False}
PACK_COLS = 1024


def _params(sem):
    return pltpu.CompilerParams(dimension_semantics=sem, vmem_limit_bytes=VMEM_LIMIT)


def _dot(a, b):
    return lax.dot_general(a, b, (((1,), (0,)), ((), ())), preferred_element_type=F32)


def _dot_nt(a, b):
    return lax.dot_general(a, b, (((1,), (1,)), ((), ())), preferred_element_type=F32)


def _dot_tn(a, b):
    return lax.dot_general(a, b, (((0,), (0,)), ((), ())), preferred_element_type=F32)


def _sigmoid(x):
    return 1.0 / (1.0 + jnp.exp(-x))


def _silu_and_grad(x):
    s = _sigmoid(x)
    return x * s, s * (1.0 + x * (1.0 - s))


def _mm_call(body, name, grid, in_specs, out_spec, out_shape, args):
    return pl.pallas_call(body, name=name, grid=grid, in_specs=in_specs, out_specs=out_spec, out_shape=out_shape,
                          compiler_params=_params(("parallel",) * len(grid)))(*args)


def _mm_in(x, wg, transposed, name):
    s, k = x.shape
    nc = wg.shape[1] if transposed else wg.shape[2]
    tm = min(1024, s)

    def body(x_ref, w_ref, o_ref):
        o_ref[...] = (_dot_nt if transposed else _dot)(x_ref[...].astype(MXU_DTYPE), w_ref[...])

    return _mm_call(body, name, (s // tm, N_CHIPS),
                    [pl.BlockSpec((tm, k), lambda i, j: (i, 0)),
                     pl.BlockSpec((None,) + wg.shape[1:], lambda i, j: (j, 0, 0))],
                    pl.BlockSpec((None, tm, nc), lambda i, j: (j, i, 0)),
                    jax.ShapeDtypeStruct((N_CHIPS, s, nc), F32), (x, wg))


def _mm_dx(dh, wg, dz, transposed, name):
    s, d = dz.shape
    nc = dh.shape[2]
    tm, tn = min(1024, s), 512

    def body(a_ref, w_ref, r_ref, o_ref):
        acc = DN_ALPHA * r_ref[...]
        for j in range(N_CHIPS):
            acc = acc + (_dot if transposed else _dot_nt)(a_ref[j], w_ref[j])
        o_ref[...] = acc

    w_spec = (pl.BlockSpec((N_CHIPS, nc, tn), lambda i, j: (0, 0, j)) if transposed
              else pl.BlockSpec((N_CHIPS, tn, nc), lambda i, j: (0, j, 0)))
    return _mm_call(body, name, (s // tm, d // tn),
                    [pl.BlockSpec((N_CHIPS, tm, nc), lambda i, j: (0, i, 0)), w_spec,
                     pl.BlockSpec((tm, tn), lambda i, j: (i, j))],
                    pl.BlockSpec((tm, tn), lambda i, j: (i, j)), jax.ShapeDtypeStruct((s, d), F32), (dh, wg, dz))


def _mm_dw_in(x, dh, transposed, name):
    s, d = x.shape
    nc = dh.shape[2]
    tm = 1024

    def body(x_ref, b_ref, o_ref):
        xb = x_ref[...].astype(MXU_DTYPE)
        acc = _dot_tn(b_ref[...], xb) if transposed else _dot_tn(xb, b_ref[...])
        o_ref[...] = acc.astype(o_ref.dtype)

    if transposed:
        out_spec, shape = pl.BlockSpec((None, nc, tm), lambda i, j: (j, 0, i)), (N_CHIPS, nc, d)
    else:
        out_spec, shape = pl.BlockSpec((None, tm, nc), lambda i, j: (j, i, 0)), (N_CHIPS, d, nc)
    return _mm_call(body, name, (d // tm, N_CHIPS),
                    [pl.BlockSpec((s, tm), lambda i, j: (0, i)), pl.BlockSpec((None, s, nc), lambda i, j: (j, 0, 0))],
                    out_spec, jax.ShapeDtypeStruct(shape, MXU_DTYPE), (x, dh))


def _mm_out_norm(mix, wg, x, g, b, name):
    s, k = mix.shape
    _, kc, d = wg.shape
    tm = min(512, s)

    def body(a_ref, w_ref, x_ref, g_ref, b_ref, z_ref, o_ref, ob_ref):
        z = DN_ALPHA * x_ref[...] + _dot(a_ref[...], w_ref[...].reshape(N_CHIPS * kc, d))
        mu = jnp.mean(z, axis=1, keepdims=True)
        zc = z - mu
        var = jnp.mean(zc * zc, axis=1, keepdims=True)
        o = zc * lax.rsqrt(var + LN_EPS) * g_ref[...] + b_ref[...]
        z_ref[...] = z
        o_ref[...] = o
        ob_ref[...] = o.astype(ob_ref.dtype)

    row = pl.BlockSpec((tm, d), lambda i: (i, 0))
    vec = pl.BlockSpec((1, d), lambda i: (0, 0))
    return pl.pallas_call(
        body, name=name, grid=(s // tm,),
        in_specs=[pl.BlockSpec((tm, k), lambda i: (i, 0)),
                  pl.BlockSpec((N_CHIPS, kc, d), lambda i: (0, 0, 0), pipeline_mode=pl.Buffered(1)), row, vec, vec],
        out_specs=[row, row, row],
        out_shape=[jax.ShapeDtypeStruct((s, d), F32), jax.ShapeDtypeStruct((s, d), F32),
                   jax.ShapeDtypeStruct((s, d), MXU_DTYPE)],
        compiler_params=_params(("parallel",)),
    )(mix, wg, x, g.reshape(1, d), b.reshape(1, d))


def _mm_dmix(dz, wg, name):
    s, d = dz.shape
    kc = wg.shape[1]
    tm = min(1024, s)

    def body(a_ref, w_ref, o_ref):
        o_ref[...] = _dot_nt(a_ref[...], w_ref[...])

    return _mm_call(body, name, (s // tm, N_CHIPS),
                    [pl.BlockSpec((tm, d), lambda i, j: (i, 0)), pl.BlockSpec((None, kc, d), lambda i, j: (j, 0, 0))],
                    pl.BlockSpec((tm, kc), lambda i, j: (i, j)), jax.ShapeDtypeStruct((s, N_CHIPS * kc), F32), (dz, wg))


def _mm_dw_out(mix, dz, name):
    s, k = mix.shape
    d = dz.shape[1]
    kc, tn = k // N_CHIPS, 1024

    def body(a_ref, b_ref, o_ref):
        o_ref[...] = _dot_tn(a_ref[...], b_ref[...]).astype(o_ref.dtype)

    return _mm_call(body, name, (N_CHIPS, d // tn),
                    [pl.BlockSpec((s, kc), lambda i, j: (0, i)), pl.BlockSpec((s, tn), lambda i, j: (0, j))],
                    pl.BlockSpec((None, kc, tn), lambda i, j: (i, 0, j)),
                    jax.ShapeDtypeStruct((N_CHIPS, kc, d), MXU_DTYPE), (mix, dz))


def _ln_bwd(dout, z, g, after):
    s, d = z.shape
    tr = min(256, s)

    def body(do_ref, z_ref, g_ref, after_ref, dz_ref, dzb_ref, dg_ref, db_ref):
        @pl.when(pl.program_id(0) == 0)
        def _():
            dg_ref[...] = jnp.zeros_like(dg_ref)
            db_ref[...] = jnp.zeros_like(db_ref)

        zz = z_ref[...]
        do = do_ref[...]
        mu = jnp.mean(zz, axis=1, keepdims=True)
        zc = zz - mu
        var = jnp.mean(zc * zc, axis=1, keepdims=True)
        rstd = lax.rsqrt(var + LN_EPS)
        zh = zc * rstd
        dg_ref[...] += jnp.sum(do * zh, axis=0, keepdims=True)
        db_ref[...] += jnp.sum(do, axis=0, keepdims=True)
        dzh = do * g_ref[...]
        dz = rstd * (dzh - jnp.mean(dzh, axis=1, keepdims=True) - zh * jnp.mean(dzh * zh, axis=1, keepdims=True))
        dz_ref[...] = dz
        dzb_ref[...] = dz.astype(MXU_DTYPE)

    row = pl.BlockSpec((tr, d), lambda i: (i, 0))
    vec = pl.BlockSpec((1, d), lambda i: (0, 0))
    return pl.pallas_call(
        body, name="ln_bwd", grid=(s // tr,), in_specs=[row, row, vec, ANY], out_specs=[row, row, vec, vec],
        out_shape=[jax.ShapeDtypeStruct((s, d), F32), jax.ShapeDtypeStruct((s, d), MXU_DTYPE),
                   jax.ShapeDtypeStruct((1, d), F32), jax.ShapeDtypeStruct((1, d), F32)],
        compiler_params=_params(("arbitrary",)),
    )(dout, z, g.reshape(1, d), after)


def _loss_and_grad(out, tgt):
    s, d = out.shape
    tr = min(256, s)

    def body(o_ref, t_ref, l_ref, do_ref):
        @pl.when(pl.program_id(0) == 0)
        def _():
            l_ref[...] = jnp.zeros_like(l_ref)

        e = o_ref[...] - t_ref[...]
        do_ref[...] = e * (1.0 / d)
        rows = jnp.sum(e * e, axis=1, keepdims=True)
        l_ref[...] += (0.5 / d) * jnp.sum(rows, axis=0, keepdims=True)

    row = pl.BlockSpec((tr, d), lambda i: (i, 0))
    return pl.pallas_call(
        body, name="loss", grid=(s // tr,), in_specs=[row, row],
        out_specs=[pl.BlockSpec((1, 1), lambda i: (0, 0)), row],
        out_shape=[jax.ShapeDtypeStruct((1, 1), F32), jax.ShapeDtypeStruct((s, d), F32)],
        compiler_params=_params(("arbitrary",)),
    )(out, tgt)


Q0, K0, V0, BG0 = 3072, 4096, 4224, 4352
T = CHUNK


def _lane_ids(width):
    return lax.broadcasted_iota(jnp.int32, (T, width), 1)


def _rope_swap(x):
    w = x.shape[1]
    l64 = _lane_ids(w) % B_HEAD_DIM
    half = ROT_DIM // 2
    return jnp.where(l64 < half, pltpu.roll(x, w - half, 1), jnp.where(l64 < ROT_DIM, pltpu.roll(x, half, 1), 0.0))


def _rope(x, c, s):
    return x * c + _rope_swap(x) * s


def _rope_bwd(d, c, s):
    return d * c + _rope_swap(d * s)


def _tril():
    return lax.broadcasted_iota(jnp.int32, (T, T), 0) >= lax.broadcasted_iota(jnp.int32, (T, T), 1)


def _band_mask(has_prev):
    qi = lax.broadcasted_iota(jnp.int32, (T, 2 * T), 0)
    kj = lax.broadcasted_iota(jnp.int32, (T, 2 * T), 1)
    return (kj > qi) & (kj <= qi + WINDOW) & ((kj >= WINDOW) | has_prev)


def _a_norm(v, g, b):
    mu = jnp.mean(v, axis=1, keepdims=True)
    vc = v - mu
    var = jnp.mean(vc * vc, axis=1, keepdims=True)
    rstd = lax.rsqrt(var + LN_EPS)
    vh = vc * rstd
    return vh, rstd, vh * g + b


def _kv_operands(hk, k_rot, kp_rot, vv, vp):
    lo = _lane_ids(LANES) < B_HEAD_DIM
    mine = lo if hk == 0 else jnp.logical_not(lo)
    both = lambda t: jnp.where(mine, t, 0.0) + pltpu.roll(jnp.where(mine, t, 0.0), B_HEAD_DIM, 1)
    k2 = jnp.concatenate([both(kp_rot), both(k_rot)], axis=0)
    v2 = jnp.concatenate([both(vp), both(vv)], axis=0)
    return k2.astype(MXU_DTYPE), v2.astype(MXU_DTYPE)


GROUP_HEADS = B_Q_HEADS // B_KV_HEADS


def _pair_columns(hk):
    return [(hk * GROUP_HEADS // 2 + pp) * LANES for pp in range(GROUP_HEADS // 2)]


def _stacked_heads(pairs):
    lo = _lane_ids(LANES) < B_HEAD_DIM
    rows = []
    for tile in pairs:
        rows += [jnp.where(lo, tile, 0.0), jnp.where(lo, 0.0, tile)]
    return jnp.concatenate(rows, axis=0).astype(MXU_DTYPE)


def _pair_of(stacked, pp):
    lo = _lane_ids(LANES) < B_HEAD_DIM
    return jnp.where(lo, stacked[2 * pp * T:(2 * pp + 1) * T], stacked[(2 * pp + 1) * T:(2 * pp + 2) * T])


def _softmax_with_sink(qm, k2, valid, sink_ref, head0):
    scores = (_dot_nt(qm, k2) * ATTN_SCALE).reshape(GROUP_HEADS, T, 2 * T)
    s = jnp.where(valid[None], scores, NEG_BIG).reshape(GROUP_HEADS * T, 2 * T)
    sink = jnp.concatenate([jnp.broadcast_to(sink_ref[:, head0 + h:head0 + h + 1], (T, 1))
                            for h in range(GROUP_HEADS)], axis=0)
    m = jnp.maximum(jnp.max(s, axis=1, keepdims=True), sink)
    ex = jnp.exp(s - m)
    es = jnp.exp(sink - m)
    inv = 1.0 / (jnp.sum(ex, axis=1, keepdims=True) + es)
    return ex * inv, es * inv


EVEN_SHARD = EVEN_IN // N_CHIPS
KV_CHIP = K0 // EVEN_SHARD
KV_SPAN = 3 * LANES
KP0, VP0 = K0 - KV_CHIP * EVEN_SHARD, V0 - KV_CHIP * EVEN_SHARD
assert V0 // EVEN_SHARD == KV_CHIP and VP0 + LANES <= KV_SPAN


def _even_specs(s):
    nb = s // T
    return nb, {
        'h': lambda r: pl.BlockSpec((N_CHIPS, T, EVEN_SHARD), lambda i: (0, r(i), 0)),
        'kvprev': lambda r: pl.BlockSpec((None, T, KV_SPAN), lambda i: (KV_CHIP, jnp.maximum(r(i) - 1, 0), 0)),
        'tab': lambda r: pl.BlockSpec((T, A_WIDTH), lambda i: (r(i), 0)),
        'tabprev': lambda r: pl.BlockSpec((T, LANES), lambda i: (jnp.maximum(r(i) - 1, 0), 0)),
    }


def _rows_from_shards(h4_ref, rows_ref):
    for j in range(N_CHIPS):
        rows_ref[:, j * EVEN_SHARD:(j + 1) * EVEN_SHARD] = h4_ref[j]


def _full(shape):
    nd = len(shape)
    return pl.BlockSpec(shape, lambda i: (0,) * nd)


def _even_fwd(h4, cq, sq, ln_g, ln_b, ws, bs_t, sinks):
    s = h4.shape[1]
    nb, sp = _even_specs(s)

    def body(h4_ref, kvp_ref, cq_ref, sq_ref, ckp_ref, skp_ref, g_ref, b_ref, ws_ref, bst_ref, sink_ref, mix_ref, h_ref):
        i = pl.program_id(0)
        _rows_from_shards(h4_ref, h_ref)
        tril = _tril()
        _, _, vn = _a_norm(h_ref[:, 1024:2048], g_ref[...], b_ref[...])
        vnb = vn.astype(MXU_DTYPE)
        for g in range(A_GROUPS):
            c0, c1 = g * LANES, (g + 1) * LANES
            wg = jnp.where(tril, ws_ref[g], 0.0).astype(MXU_DTYPE)
            mixed = _dot(wg, vnb[:, c0:c1]) + bst_ref[:, g:g + 1]
            sil, _ = _silu_and_grad(h_ref[:, 2048 + c0:2048 + c1])
            mix_ref[:, c0:c1] = (h_ref[:, c0:c1] * mixed * sil).astype(mix_ref.dtype)

        k_rot = _rope(h_ref[:, K0:K0 + LANES], cq_ref[:, 0:LANES], sq_ref[:, 0:LANES])
        kp_rot = _rope(kvp_ref[:, KP0:KP0 + LANES], ckp_ref[...], skp_ref[...])
        vp = kvp_ref[:, VP0:VP0 + LANES]
        valid = _band_mask(i > 0)
        for hk in range(B_KV_HEADS):
            k2, v2 = _kv_operands(hk, k_rot, kp_rot, h_ref[:, V0:V0 + LANES], vp)
            qm = _stacked_heads([_rope(h_ref[:, Q0 + c0:Q0 + c0 + LANES], cq_ref[:, c0:c0 + LANES],
                                       sq_ref[:, c0:c0 + LANES]) for c0 in _pair_columns(hk)])
            p, _ = _softmax_with_sink(qm, k2, valid, sink_ref, GROUP_HEADS * hk)
            out = _dot(p.astype(MXU_DTYPE), v2)
            for pp, c0 in enumerate(_pair_columns(hk)):
                sil, _ = _silu_and_grad(h_ref[:, BG0 + c0:BG0 + c0 + LANES])
                mix_ref[:, A_WIDTH + c0:A_WIDTH + c0 + LANES] = (_pair_of(out, pp) * sil).astype(mix_ref.dtype)

    ident = lambda i: i
    return pl.pallas_call(
        body, name="even_fwd", grid=(nb,),
        in_specs=[sp['h'](ident), sp['kvprev'](ident), sp['tab'](ident), sp['tab'](ident),
                  sp['tabprev'](ident), sp['tabprev'](ident), _full((1, A_WIDTH)), _full((1, A_WIDTH)),
                  _full((A_GROUPS, T, T)), _full((T, A_GROUPS)), _full((1, B_Q_HEADS))],
        out_specs=pl.BlockSpec((T, 2 * A_WIDTH), lambda i: (i, 0)),
        out_shape=jax.ShapeDtypeStruct((s, 2 * A_WIDTH), MXU_DTYPE),
        scratch_shapes=[pltpu.VMEM((T, EVEN_IN), F32)],
        compiler_params=_params(("parallel",)),
    )(h4, h4, cq, sq, cq, sq, ln_g.reshape(1, -1), ln_b.reshape(1, -1), ws, bs_t, sinks.reshape(1, -1))


def _even_bwd(h4, dmix, cq, sq, ln_g, ln_b, ws, bs_t, sinks):
    s = h4.shape[1]
    nb, sp = _even_specs(s)

    def body(h4_ref, kvp_ref, cq_ref, sq_ref, ckp_ref, skp_ref, dm_ref, g_ref, b_ref, ws_ref, bst_ref, sink_ref,
             dh4_ref, dg_ref, db_ref, dws_ref, dbst_ref, dsink_ref, h_ref, dh_ref, dvn_scr, ck_scr, cv_scr):
        i = pl.program_id(0)

        @pl.when(i == 0)
        def _():
            for ref in (dg_ref, db_ref, dws_ref, dbst_ref, dsink_ref, ck_scr, cv_scr):
                ref[...] = jnp.zeros_like(ref)

        _rows_from_shards(h4_ref, h_ref)

        tril = _tril()
        vh, rstd, vn = _a_norm(h_ref[:, 1024:2048], g_ref[...], b_ref[...])
        vnb = vn.astype(MXU_DTYPE)
        for g in range(A_GROUPS):
            c0, c1 = g * LANES, (g + 1) * LANES
            wg = jnp.where(tril, ws_ref[g], 0.0).astype(MXU_DTYPE)
            mixed = _dot(wg, vnb[:, c0:c1]) + bst_ref[:, g:g + 1]
            ag = h_ref[:, 2048 + c0:2048 + c1]
            u = h_ref[:, c0:c1]
            sil, dsil = _silu_and_grad(ag)
            da = dm_ref[:, c0:c1]
            dh_ref[:, c0:c1] = (da * mixed * sil).astype(dh_ref.dtype)
            dh_ref[:, 2048 + c0:2048 + c1] = (da * u * mixed * dsil).astype(dh_ref.dtype)
            dmx = da * u * sil
            dmb = dmx.astype(MXU_DTYPE)
            dvn_scr[:, c0:c1] = _dot_tn(wg, dmb)
            dws_ref[g] += jnp.where(tril, _dot_nt(dmb, vnb[:, c0:c1]), 0.0)
            dbst_ref[:, g:g + 1] += jnp.sum(dmx, axis=1, keepdims=True)
        dvn = dvn_scr[...]
        dg_ref[...] += jnp.sum(dvn * vh, axis=0, keepdims=True)
        db_ref[...] += jnp.sum(dvn, axis=0, keepdims=True)
        dvh = dvn * g_ref[...]
        dv = rstd * (dvh - jnp.mean(dvh, axis=1, keepdims=True) - vh * jnp.mean(dvh * vh, axis=1, keepdims=True))
        dh_ref[:, 1024:2048] = dv.astype(dh_ref.dtype)

        ck_cur, sk_cur = cq_ref[:, 0:LANES], sq_ref[:, 0:LANES]
        k_rot = _rope(h_ref[:, K0:K0 + LANES], ck_cur, sk_cur)
        kp_rot = _rope(kvp_ref[:, KP0:KP0 + LANES], ckp_ref[...], skp_ref[...])
        vp = kvp_ref[:, VP0:VP0 + LANES]
        valid = _band_mask(i < nb - 1)
        dk_all = jnp.zeros((2 * T, LANES), F32)
        dv_all = jnp.zeros((2 * T, LANES), F32)
        for hk in range(B_KV_HEADS):
            k2, v2 = _kv_operands(hk, k_rot, kp_rot, h_ref[:, V0:V0 + LANES], vp)
            cols = _pair_columns(hk)
            qm = _stacked_heads([_rope(h_ref[:, Q0 + c0:Q0 + c0 + LANES], cq_ref[:, c0:c0 + LANES],
                                       sq_ref[:, c0:c0 + LANES]) for c0 in cols])
            p, psink = _softmax_with_sink(qm, k2, valid, sink_ref, GROUP_HEADS * hk)
            pb = p.astype(MXU_DTYPE)
            out = _dot(pb, v2)
            dobs = []
            for pp, c0 in enumerate(cols):
                sil, dsil = _silu_and_grad(h_ref[:, BG0 + c0:BG0 + c0 + LANES])
                dbo = dm_ref[:, A_WIDTH + c0:A_WIDTH + c0 + LANES]
                dh_ref[:, BG0 + c0:BG0 + c0 + LANES] = dbo * _pair_of(out, pp) * dsil
                dobs.append(dbo * sil)
            dobm = _stacked_heads(dobs)
            dp = _dot_nt(dobm, v2)
            rs = jnp.sum(p * dp, axis=1, keepdims=True)
            dsb = (p * (dp - rs) * ATTN_SCALE).astype(MXU_DTYPE)
            to_sink = psink * rs
            for h in range(GROUP_HEADS):
                head = GROUP_HEADS * hk + h
                dsink_ref[:, head:head + 1] += -jnp.sum(to_sink[h * T:(h + 1) * T], axis=0, keepdims=True)
            dq = _dot(dsb, k2)
            for pp, c0 in enumerate(cols):
                dh_ref[:, Q0 + c0:Q0 + c0 + LANES] = _rope_bwd(_pair_of(dq, pp), cq_ref[:, c0:c0 + LANES],
                                                               sq_ref[:, c0:c0 + LANES])
            acc_k = _dot_tn(dsb, qm)
            acc_v = _dot_tn(pb, dobm)
            lo2 = lax.broadcasted_iota(jnp.int32, (2 * T, LANES), 1) < B_HEAD_DIM
            mine = lo2 if hk == 0 else jnp.logical_not(lo2)
            dk_all = dk_all + jnp.where(mine, acc_k + pltpu.roll(acc_k, B_HEAD_DIM, 1), 0.0)
            dv_all = dv_all + jnp.where(mine, acc_v + pltpu.roll(acc_v, B_HEAD_DIM, 1), 0.0)
        dk_rot = dk_all[T:2 * T] + ck_scr[...]
        dvv = dv_all[T:2 * T] + cv_scr[...]
        ck_scr[...] = dk_all[0:T]
        cv_scr[...] = dv_all[0:T]
        dh_ref[:, K0:K0 + LANES] = _rope_bwd(dk_rot, ck_cur, sk_cur).astype(dh_ref.dtype)
        dh_ref[:, V0:V0 + LANES] = dvv.astype(dh_ref.dtype)
        for j in range(N_CHIPS):
            dh4_ref[j] = dh_ref[:, j * EVEN_SHARD:(j + 1) * EVEN_SHARD].astype(dh4_ref.dtype)

    rev = lambda i: nb - 1 - i
    return pl.pallas_call(
        body, name="even_bwd", grid=(nb,),
        in_specs=[sp['h'](rev), sp['kvprev'](rev), sp['tab'](rev), sp['tab'](rev),
                  sp['tabprev'](rev), sp['tabprev'](rev), pl.BlockSpec((T, 2 * A_WIDTH), lambda i: (rev(i), 0)),
                  _full((1, A_WIDTH)), _full((1, A_WIDTH)), _full((A_GROUPS, T, T)), _full((T, A_GROUPS)),
                  _full((1, B_Q_HEADS))],
        out_specs=[sp['h'](rev), _full((1, A_WIDTH)), _full((1, A_WIDTH)),
                   _full((A_GROUPS, T, T)), _full((T, A_GROUPS)), _full((1, B_Q_HEADS))],
        out_shape=[jax.ShapeDtypeStruct((N_CHIPS, s, EVEN_SHARD), MXU_DTYPE), jax.ShapeDtypeStruct((1, A_WIDTH), F32),
                   jax.ShapeDtypeStruct((1, A_WIDTH), F32), jax.ShapeDtypeStruct((A_GROUPS, T, T), F32),
                   jax.ShapeDtypeStruct((T, A_GROUPS), F32), jax.ShapeDtypeStruct((1, B_Q_HEADS), F32)],
        scratch_shapes=[pltpu.VMEM((T, EVEN_IN), F32), pltpu.VMEM((T, EVEN_IN), F32), pltpu.VMEM((T, A_WIDTH), F32),
                        pltpu.VMEM((T, LANES), F32), pltpu.VMEM((T, LANES), F32)],
        compiler_params=_params(("arbitrary",)),
    )(h4, h4, cq, sq, cq, sq, dmix, ln_g.reshape(1, -1), ln_b.reshape(1, -1), ws, bs_t, sinks.reshape(1, -1))


def _shift_down(x, d, fill=0.0):
    rows = lax.broadcasted_iota(jnp.int32, x.shape, 0)
    return jnp.where(rows >= d, pltpu.roll(x, d, 0), fill)


def _shift_up(x, d, fill=0.0):
    n = x.shape[0]
    rows = lax.broadcasted_iota(jnp.int32, x.shape, 0)
    return jnp.where(rows < n - d, pltpu.roll(x, n - d, 0), fill)


SCAN_BLOCK = 128
SUBLANES = 8


def _scan_block(a, b, up):
    n = a.shape[0]
    d = 1
    while d < n:
        if d < SUBLANES:
            shift = _shift_up if up else _shift_down
            a, b = a * shift(a, d, 1.0), a * shift(b, d) + b
        elif up:
            a, b = (jnp.concatenate([a[:n - d] * a[d:], a[n - d:]], axis=0),
                    jnp.concatenate([a[:n - d] * b[d:] + b[:n - d], b[n - d:]], axis=0))
        else:
            a, b = (jnp.concatenate([a[:d], a[d:] * a[:n - d]], axis=0),
                    jnp.concatenate([b[:d], a[d:] * b[:n - d] + b[d:]], axis=0))
        d *= 2
    return a, b


def _scan(a, b, up):
    n = a.shape[0]
    size = min(SCAN_BLOCK, n)
    blocks = n // size
    out, carry = [None] * blocks, None
    for k in (reversed(range(blocks)) if up else range(blocks)):
        prod, h = _scan_block(a[k * size:(k + 1) * size], b[k * size:(k + 1) * size], up)
        if carry is not None:
            h = h + prod * carry
        carry = h[0:1] if up else h[size - 1:size]
        out[k] = h
    return jnp.concatenate(out, axis=0)


def _neg_expm1(y):
    u = jnp.exp(y)
    um1 = u - 1.0
    safe = jnp.where(um1 == 0.0, 1.0, jnp.log(u))
    near = jnp.where(um1 == 0.0, y, um1 * y / safe)
    return -jnp.where(y > -0.5, near, um1)


def _softplus(x):
    return jnp.maximum(x, 0.0) + jnp.log1p(jnp.exp(-jnp.abs(x)))


def _lru_forward(xc, cw, cb, wa, ba, wx, bx, lam):
    shifted = [_shift_down(xc, CONV_WIDTH - 1 - j) if j < CONV_WIDTH - 1 else xc for j in range(CONV_WIDTH)]
    xconv = cb
    for j in range(CONV_WIDTH):
        xconv = xconv + cw[j:j + 1, :] * shifted[j]
    xb = xconv.astype(MXU_DTYPE)
    r = _sigmoid(_dot(xb, wa.astype(MXU_DTYPE)) + ba)
    ig = _sigmoid(_dot(xb, wx.astype(MXU_DTYPE)) + bx)
    sp = _softplus(-lam)
    log_a = -LRU_C * r * sp
    a = jnp.exp(log_a)
    mult = jnp.sqrt(_neg_expm1(2.0 * log_a))
    hs = _scan(a, mult * ig * xconv, up=False)
    return dict(shifted=shifted, xconv=xconv, xb=xb, r=r, ig=ig, sp=sp, a=a, mult=mult, hs=hs)


def _col(width, off):
    return lambda s: pl.BlockSpec((s, width), lambda i: (0, off + i))


def _part(width, slot):
    return lambda s: pl.BlockSpec((None, s, width), lambda i: (slot, 0, i))


def _lru_specs(s):
    w = LANES
    return [_part(w, 0)(s), _part(w, 1)(s), pl.BlockSpec((CONV_WIDTH, w), lambda i: (0, i)),
            pl.BlockSpec((1, w), lambda i: (0, i)), pl.BlockSpec((None, w, w), lambda i: (i, 0, 0)),
            pl.BlockSpec((1, w), lambda i: (0, i)), pl.BlockSpec((None, w, w), lambda i: (i, 0, 0)),
            pl.BlockSpec((1, w), lambda i: (0, i)), pl.BlockSpec((1, w), lambda i: (0, i))]


def _lru_fwd(h, cw, cb, wa, ba, wx, bx, lam):
    s = h.shape[1]

    def body(xc_ref, cg_ref, cw_ref, cb_ref, wa_ref, ba_ref, wx_ref, bx_ref, lam_ref, o_ref):
        f = _lru_forward(xc_ref[...], cw_ref[...], cb_ref[...], wa_ref[...], ba_ref[...], wx_ref[...], bx_ref[...],
                         lam_ref[...])
        sil, _ = _silu_and_grad(cg_ref[...])
        o_ref[...] = (f['hs'] * sil).astype(o_ref.dtype)

    return pl.pallas_call(
        body, name="lru_fwd", grid=(C_HEADS,), in_specs=_lru_specs(s), out_specs=_col(LANES, 0)(s),
        out_shape=jax.ShapeDtypeStruct((s, C_WIDTH + D_WIDTH), MXU_DTYPE), compiler_params=_params(("parallel",)),
    )(h, h, cw, cb.reshape(1, -1), wa, ba.reshape(1, -1), wx, bx.reshape(1, -1), lam.reshape(1, -1))


def _lru_bwd(h, dmix, cw, cb, wa, ba, wx, bx, lam):
    s = h.shape[1]

    def body(xc_ref, cg_ref, cw_ref, cb_ref, wa_ref, ba_ref, wx_ref, bx_ref, lam_ref, dco_ref,
             dh_ref, dcw_ref, dcb_ref, dwa_ref, dba_ref, dwx_ref, dbx_ref, dlam_ref):
        dxc_ref, dcg_ref = dh_ref.at[0], dh_ref.at[1]
        cw, lam = cw_ref[...], lam_ref[...]
        wab, wxb = wa_ref[...].astype(MXU_DTYPE), wx_ref[...].astype(MXU_DTYPE)
        f = _lru_forward(xc_ref[...], cw, cb_ref[...], wa_ref[...], ba_ref[...], wx_ref[...], bx_ref[...], lam)
        sil, dsil = _silu_and_grad(cg_ref[...])
        dco = dco_ref[...]
        hs, a, mult, ig, r, xconv = f['hs'], f['a'], f['mult'], f['ig'], f['r'], f['xconv']
        dcg_ref[...] = (dco * hs * dsil).astype(dcg_ref.dtype)
        lamb = _scan(_shift_up(a, 1), dco * sil, up=True)
        da = lamb * _shift_down(hs, 1)
        d_ig = lamb * mult * xconv
        d_mult = lamb * ig * xconv
        dxconv = lamb * mult * ig
        d_log_a = da * a - d_mult * (a * a) / mult
        dlam_ref[...] = jnp.sum(d_log_a * r, axis=0, keepdims=True) * LRU_C * _sigmoid(-lam)
        dpa = d_log_a * (-LRU_C * f['sp']) * r * (1.0 - r)
        dpx = d_ig * ig * (1.0 - ig)
        dba_ref[...] = jnp.sum(dpa, axis=0, keepdims=True)
        dbx_ref[...] = jnp.sum(dpx, axis=0, keepdims=True)
        dpab, dpxb = dpa.astype(MXU_DTYPE), dpx.astype(MXU_DTYPE)
        dwa_ref[...] = _dot_tn(f['xb'], dpab)
        dwx_ref[...] = _dot_tn(f['xb'], dpxb)
        dxconv = dxconv + _dot_nt(dpab, wab) + _dot_nt(dpxb, wxb)
        dcb_ref[...] = jnp.sum(dxconv, axis=0, keepdims=True)
        dxc = jnp.zeros_like(dxconv)
        for j in range(CONV_WIDTH):
            dcw_ref[j:j + 1, :] = jnp.sum(dxconv * f['shifted'][j], axis=0, keepdims=True)
            back = _shift_up(dxconv, CONV_WIDTH - 1 - j) if j < CONV_WIDTH - 1 else dxconv
            dxc = dxc + cw[j:j + 1, :] * back
        dxc_ref[...] = dxc.astype(dxc_ref.dtype)

    w = LANES
    vec = pl.BlockSpec((1, w), lambda i: (0, i))
    mat = pl.BlockSpec((None, w, w), lambda i: (i, 0, 0))
    vshape = jax.ShapeDtypeStruct((1, C_WIDTH), F32)
    mshape = jax.ShapeDtypeStruct((C_HEADS, w, w), F32)
    return pl.pallas_call(
        body, name="lru_bwd", grid=(C_HEADS,), in_specs=_lru_specs(s) + [_col(w, 0)(s)],
        out_specs=[pl.BlockSpec((2, s, w), lambda i: (0, 0, i)), pl.BlockSpec((CONV_WIDTH, w), lambda i: (0, i)),
                   vec, mat, vec, mat, vec, vec],
        out_shape=[jax.ShapeDtypeStruct((N_CHIPS, s, C_WIDTH), MXU_DTYPE),
                   jax.ShapeDtypeStruct((CONV_WIDTH, C_WIDTH), F32), vshape, mshape, vshape, mshape, vshape, vshape],
        compiler_params=_params(("parallel",)),
    )(h, h, cw, cb.reshape(1, -1), wa, ba.reshape(1, -1), wx, bx.reshape(1, -1), lam.reshape(1, -1), dmix)


def _pool_window(g):
    return jnp.left_shift(2, g).astype(F32)


def _select_window(g, sums):
    out = sums[-1]
    for k in range(len(sums) - 2, -1, -1):
        out = jnp.where(g == k, sums[k], out)
    return out


def _window_sums(x, shift):
    sums, cur = [], x
    for k in range(D_GROUPS):
        cur = cur + shift(cur, 2 ** k)
        sums.append(cur)
    return sums


def _pool_specs(s):
    w = D_GROUP_DIM
    return [_part(w, 2)(s), _part(w, 3)(s),
            pl.BlockSpec((None, w, w), lambda i: (i, 0, 0)), pl.BlockSpec((1, w), lambda i: (0, i))]


def _pool_forward(xd, g):
    rows = lax.broadcasted_iota(jnp.int32, xd.shape, 0).astype(F32) + 1.0
    cnt = jnp.minimum(rows, _pool_window(g))
    pooled = _select_window(g, _window_sums(xd, _shift_down)) / cnt - xd
    return pooled, cnt


def _pool_fwd(h, mix, w_pool, d_scale):
    s = h.shape[1]

    def body(xd_ref, dg_ref, wp_ref, sc_ref, mix_ref, o_ref):
        pooled, _ = _pool_forward(xd_ref[...], pl.program_id(0))
        mixed = _dot(pooled.astype(MXU_DTYPE), wp_ref[...].astype(MXU_DTYPE))
        sil, _ = _silu_and_grad(dg_ref[...])
        o_ref[...] = (mixed * sc_ref[...] * sil).astype(o_ref.dtype)

    return pl.pallas_call(
        body, name="pool_fwd", grid=(D_GROUPS,), in_specs=_pool_specs(s) + [ANY],
        out_specs=_col(D_GROUP_DIM, C_WIDTH // D_GROUP_DIM)(s), out_shape=jax.ShapeDtypeStruct(mix.shape, mix.dtype),
        input_output_aliases={4: 0}, compiler_params=_params(("parallel",)),
    )(h, h, w_pool, d_scale.reshape(1, -1), mix)


def _pool_bwd(h, dmix, dh, w_pool, d_scale):
    s = h.shape[1]
    w = D_GROUP_DIM

    def body(xd_ref, dg_ref, wp_ref, sc_ref, ddo_ref, dh_in_ref, dh_ref, dwp_ref, dsc_ref):
        dxd_ref, ddg_ref = dh_ref.at[0], dh_ref.at[1]
        g = pl.program_id(0)
        pooled, cnt = _pool_forward(xd_ref[...], g)
        pb = pooled.astype(MXU_DTYPE)
        wpb = wp_ref[...].astype(MXU_DTYPE)
        mixed = _dot(pb, wpb)
        sil, dsil = _silu_and_grad(dg_ref[...])
        ddo = ddo_ref[...]
        sc = sc_ref[...]
        ddg_ref[...] = (ddo * mixed * sc * dsil).astype(ddg_ref.dtype)
        dms = ddo * sil
        dsc_ref[...] = jnp.sum(dms * mixed, axis=0, keepdims=True)
        dmb = (dms * sc).astype(MXU_DTYPE)
        dwp_ref[...] = _dot_tn(pb, dmb)
        dpooled = _dot_nt(dmb, wpb)
        dxd = _select_window(g, _window_sums(dpooled / cnt, _shift_up)) - dpooled
        dxd_ref[...] = dxd.astype(dxd_ref.dtype)

    return pl.pallas_call(
        body, name="pool_bwd", grid=(D_GROUPS,), in_specs=_pool_specs(s) + [_col(w, C_WIDTH // w)(s), ANY],
        out_specs=[pl.BlockSpec((2, s, w), lambda i: (1, 0, i)), pl.BlockSpec((None, w, w), lambda i: (i, 0, 0)),
                   pl.BlockSpec((1, w), lambda i: (0, i))],
        out_shape=[jax.ShapeDtypeStruct(dh.shape, dh.dtype),
                   jax.ShapeDtypeStruct((D_GROUPS, w, w), F32), jax.ShapeDtypeStruct((1, D_WIDTH), F32)],
        input_output_aliases={5: 0}, compiler_params=_params(("parallel",)),
    )(h, h, w_pool, d_scale.reshape(1, -1), dmix, dh)


def _rope_tables(positions):
    s = positions.shape[0]
    inv_freq = ROPE_THETA ** (-jnp.arange(0, ROT_DIM, 2, dtype=F32) / ROT_DIM)
    ang = positions.astype(F32)[:, None] * inv_freq
    c, sn = jnp.cos(ang), jnp.sin(ang)
    rest = B_HEAD_DIM - ROT_DIM
    c64 = jnp.concatenate([c, c, jnp.ones((s, rest), F32)], axis=1)
    s64 = jnp.concatenate([-sn, sn, jnp.zeros((s, rest), F32)], axis=1)
    return jnp.tile(c64, (1, B_Q_HEADS)), jnp.tile(s64, (1, B_Q_HEADS))


def _local_step(x, positions, target, w, projections_of, projection_grads):
    cq, sq = _rope_tables(positions)
    saved = []
    xin, xin_b = x, x
    for layer in range(DEPTH):
        j = layer // 2
        kind = 'even' if layer % 2 == 0 else 'odd'
        w_in, w_out = projections_of(layer, xin_b)
        h = _mm_in(xin_b, w_in, W_IN_TRANSPOSED[kind], "mm_" + kind + "_in")
        if kind == 'even':
            mix = _even_fwd(h, cq, sq, w['even_a_ln_g'][j], w['even_a_ln_b'][j], w['even_a_ws'][j],
                            w['even_a_bs'][j].T, w['even_b_sinks'][j])
        else:
            mix = _lru_fwd(h, w['odd_conv_w'][j], w['odd_conv_b'][j], w['odd_w_a'][j], w['odd_b_a'][j],
                           w['odd_w_x'][j], w['odd_b_x'][j], w['odd_lam'][j])
            mix = _pool_fwd(h, mix, w['odd_w_pool'][j], w['odd_d_scale'][j])
        z, out, out_b = _mm_out_norm(mix, w_out, xin, w[kind + '_ln_g'][j], w[kind + '_ln_b'][j], "mm_out_norm")
        saved.append((xin_b, h, mix, z, w_in, w_out))
        xin, xin_b = out, out_b

    loss, dout = _loss_and_grad(xin, target)
    per_layer = {n: [None, None] for n in SMALL_NAMES}
    behind = loss
    for layer in reversed(range(DEPTH)):
        j = layer // 2
        kind = 'even' if layer % 2 == 0 else 'odd'
        xin_b, h, mix, z, w_in, w_out = saved[layer]
        dz, dz_b, dg, db = _ln_bwd(dout, z, w[kind + '_ln_g'][j], behind)
        per_layer[kind + '_ln_g'][j], per_layer[kind + '_ln_b'][j] = dg[0], db[0]
        dmix = _mm_dmix(dz_b, w_out, "mm_dmix")
        dw_out = _mm_dw_out(mix, dz_b, "mm_dw_out")
        if kind == 'even':
            dh, dlg, dlb, dws, dbst, dsink = _even_bwd(
                h, dmix, cq, sq, w['even_a_ln_g'][j], w['even_a_ln_b'][j], w['even_a_ws'][j], w['even_a_bs'][j].T,
                w['even_b_sinks'][j])
            for n, val in (('even_a_ln_g', dlg[0]), ('even_a_ln_b', dlb[0]), ('even_a_ws', dws),
                           ('even_a_bs', dbst.T), ('even_b_sinks', dsink[0])):
                per_layer[n][j] = val
        else:
            dh, dcw, dcb, dwa, dba, dwx, dbx, dlam = _lru_bwd(
                h, dmix, w['odd_conv_w'][j], w['odd_conv_b'][j], w['odd_w_a'][j], w['odd_b_a'][j], w['odd_w_x'][j],
                w['odd_b_x'][j], w['odd_lam'][j])
            dh, dwp, dsc = _pool_bwd(h, dmix, dh, w['odd_w_pool'][j], w['odd_d_scale'][j])
            for n, val in (('odd_conv_w', dcw), ('odd_conv_b', dcb[0]), ('odd_w_a', dwa), ('odd_b_a', dba[0]),
                           ('odd_w_x', dwx), ('odd_b_x', dbx[0]), ('odd_lam', dlam[0]), ('odd_w_pool', dwp),
                           ('odd_d_scale', dsc[0])):
                per_layer[n][j] = val
        dout = _mm_dx(dh, w_in, dz, W_IN_TRANSPOSED[kind], "mm_dx_" + kind)
        dw_in = _mm_dw_in(xin_b, dh, W_IN_TRANSPOSED[kind], "mm_dw_in_" + kind)
        behind = projection_grads(layer, dw_in, dw_out)
    return loss, dout, {n: jnp.stack(v) for n, v in per_layer.items()}


ANY = pl.BlockSpec(memory_space=pl.ANY)


def _where_am_i():
    return lax.axis_index("x"), lax.axis_index("y"), lax.axis_index("c")


def _other_chips(x, y):
    return [(1 - x, y), (x, 1 - y), (1 - x, 1 - y)]


PACKED_ROWS = 16


def _half_rows(rows, which):
    half = rows // 2
    assert half % PACKED_ROWS == 0, rows
    return pl.ds(pl.multiple_of(which * half, PACKED_ROWS), half)


def _sibling_swap(bufs, name, *, other_half_of_axis1=False, after=()):
    n, na = len(bufs), len(after)
    out_shapes = [(b.shape[0], b.shape[1] // 2, b.shape[2]) if other_half_of_axis1 else b.shape for b in bufs]

    def body(*refs):
        x_refs, out_refs, (send_sems, recv_sems) = refs[:n], refs[n + na:2 * n + na], refs[2 * n + na:]
        x, y, c = _where_am_i()
        copies = []
        for p in range(n):
            src = x_refs[p].at[:, _half_rows(bufs[p].shape[1], 1 - c), :] if other_half_of_axis1 else x_refs[p]
            copies.append(pltpu.make_async_remote_copy(
                src_ref=src, dst_ref=out_refs[p], send_sem=send_sems.at[p], recv_sem=recv_sems.at[p],
                device_id=(x, y, 1 - c), device_id_type=MESH))
            copies[-1].start()
        for cp in copies:
            cp.wait()

    return pl.pallas_call(
        body, name=name, in_specs=[ANY] * (n + na), out_specs=[ANY] * n,
        out_shape=[jax.ShapeDtypeStruct(s, b.dtype) for s, b in zip(out_shapes, bufs)],
        scratch_shapes=[pltpu.SemaphoreType.DMA((n,)), pltpu.SemaphoreType.DMA((n,))],
    )(*bufs, *after)


HBM = pl.BlockSpec(memory_space=pltpu.HBM)
SEM = pl.BlockSpec(memory_space=pltpu.SEMAPHORE)
IN_FLIGHT = pltpu.CompilerParams(has_side_effects=pltpu.SideEffectType.DATAFLOW_SIDE_EFFECTING)


def _ici_copy(kind, src_ref, land_ref, send_sems, recv_sems, k, peer, x, y, c):
    px, py = peer
    me = 2 * x + y
    if kind == 'gather':
        half = _half_rows(src_ref.shape[0], c)
        src, dst = src_ref.at[half, :], land_ref.at[me, half, :]
    else:
        src, dst = src_ref.at[2 * px + py], land_ref.at[me]
    return pltpu.make_async_remote_copy(src_ref=src, dst_ref=dst, send_sem=send_sems.at[k], recv_sem=recv_sems.at[k],
                                        device_id=(px, py, c), device_id_type=MESH)


def _ici_start(kind, groups, name):
    flat = [a for g in groups for a in g]
    n, ng = len(flat), len(groups)
    land_shapes = [((N_CHIPS,) + a.shape) if kind == 'gather' else a.shape for a in flat]

    def body(*refs):
        srcs, lands, sems, token = refs[:n], refs[n:2 * n], refs[2 * n:2 * n + 2 * ng], refs[-1]
        x, y, c = _where_am_i()
        p = 0
        for gi, g in enumerate(groups):
            for q in range(len(g)):
                for k, peer in enumerate(_other_chips(x, y)):
                    _ici_copy(kind, srcs[p], lands[p], sems[2 * gi], sems[2 * gi + 1], 3 * q + k, peer, x, y, c).start()
                p += 1
        token[...] = jnp.zeros_like(token)

    sem_shapes = [pltpu.SemaphoreType.DMA((3 * len(g),)) for g in groups for _ in range(2)]
    outs = pl.pallas_call(
        body, name=name, in_specs=[HBM] * (2 * n),
        out_specs=[SEM] * (2 * ng) + [HBM] * (2 * n) + [pl.BlockSpec(memory_space=pltpu.VMEM)],
        out_shape=sem_shapes + [pltpu.HBM(a.shape, a.dtype) for a in flat]
        + [pltpu.HBM(s, a.dtype) for s, a in zip(land_shapes, flat)] + [jax.ShapeDtypeStruct((8, LANES), F32)],
        input_output_aliases={i: 2 * ng + i for i in range(2 * n)}, compiler_params=IN_FLIGHT,
    )(*[pltpu.with_memory_space_constraint(a, pltpu.HBM) for a in flat],
      *[pltpu.with_memory_space_constraint(lax.empty(s, a.dtype), pltpu.HBM) for s, a in zip(land_shapes, flat)])
    sems, srcs, lands, token = outs[:2 * ng], outs[2 * ng:2 * ng + n], outs[2 * ng + n:2 * ng + 2 * n], outs[-1]
    started, p = [], 0
    for gi, g in enumerate(groups):
        started.append((sems[2 * gi], sems[2 * gi + 1], srcs[p:p + len(g)], lands[p:p + len(g)]))
        p += len(g)
    return started, token


def _ici_wait(kind, started, after, name):
    send_sems, recv_sems, srcs, lands = started
    n = len(srcs)
    after = list(after)

    def body(*refs):
        src_refs, land_refs, send_ref, recv_ref = refs[:n], refs[n:2 * n], refs[2 * n], refs[2 * n + 1]
        x, y, c = _where_am_i()
        for q in range(n):
            for k, peer in enumerate(_other_chips(x, y)):
                cp = _ici_copy(kind, src_refs[q], land_refs[q], send_ref, recv_ref, 3 * q + k, peer, x, y, c)
                cp.wait_send()
                cp.wait_recv()

    outs = pl.pallas_call(
        body, name=name, in_specs=[HBM] * (2 * n) + [SEM, SEM] + [ANY] * len(after), out_specs=[HBM] * (2 * n),
        out_shape=[pltpu.HBM(a.shape, a.dtype) for a in list(srcs) + list(lands)],
        input_output_aliases={i: i for i in range(2 * n)}, compiler_params=IN_FLIGHT,
    )(*srcs, *lands, send_sems, recv_sems, *after)
    return outs[:n], outs[n:]


def _forward_to_sibling(lands, name):
    n = len(lands)

    def body(*refs):
        out_refs, (send_sems, recv_sems) = refs[n:2 * n], refs[2 * n:]
        x, y, c = _where_am_i()

        def copy(q, k, peer, half_of):
            rows = out_refs[q].at[2 * peer[0] + peer[1], _half_rows(lands[q].shape[1], half_of), :]
            return pltpu.make_async_remote_copy(
                src_ref=rows, dst_ref=rows, send_sem=send_sems.at[3 * q + k], recv_sem=recv_sems.at[3 * q + k],
                device_id=(x, y, 1 - c), device_id_type=MESH)

        sends = [copy(q, k, peer, c) for q in range(n) for k, peer in enumerate(_other_chips(x, y))]
        for cp in sends:
            cp.start()
        for q in range(n):
            for k, peer in enumerate(_other_chips(x, y)):
                copy(q, k, peer, 1 - c).wait_recv()
        for cp in sends:
            cp.wait_send()

    return pl.pallas_call(
        body, name=name, in_specs=[ANY] * n, out_specs=[ANY] * n,
        out_shape=[jax.ShapeDtypeStruct(a.shape, a.dtype) for a in lands],
        input_output_aliases={i: i for i in range(n)},
        scratch_shapes=[pltpu.SemaphoreType.DMA((3 * n,)), pltpu.SemaphoreType.DMA((3 * n,))],
    )(*lands)


def _with_own_slot(lands, own, chip):
    out = []
    for land, mine in zip(lands, own):
        mine = lax.dynamic_slice_in_dim(mine, chip, 1, axis=0) if mine.ndim == land.ndim else mine[None]
        out.append(lax.dynamic_update_slice(land, mine, (chip,) + (0,) * (land.ndim - 1)))
    return out


def _row_tile(rows, row_bytes, n_bufs, budget=24 * 1024 * 1024):
    best = None
    for t in range(PACKED_ROWS, rows + 1, PACKED_ROWS):
        if rows % t == 0 and 2 * n_bufs * t * row_bytes <= budget:
            best = t
    assert best is not None, (rows, row_bytes)
    return best


def _add_halves(full, recv, core, out_dtype, name):
    n, h, cols = recv.shape
    tr = _row_tile(h, cols * 4, 3)
    nt = h // tr

    def body(core_ref, a_ref, b_ref, o_ref):
        o_ref[...] = (a_ref[...].astype(F32) + b_ref[...].astype(F32)).astype(out_dtype)

    blk = (None, tr, cols)
    return pl.pallas_call(
        body, name=name,
        grid_spec=pltpu.PrefetchScalarGridSpec(
            num_scalar_prefetch=1, grid=(n, nt),
            in_specs=[pl.BlockSpec(blk, lambda j, i, core_ref: (j, core_ref[0] * nt + i, 0)),
                      pl.BlockSpec(blk, lambda j, i, core_ref: (j, i, 0))],
            out_specs=pl.BlockSpec(blk, lambda j, i, core_ref: (j, i, 0))),
        out_shape=jax.ShapeDtypeStruct((n, h, cols), out_dtype),
        compiler_params=_params(("parallel", "parallel")),
    )(core.reshape(1), full, recv)


def _add2(a, b, name):
    r, cols = a.shape
    tr = _row_tile(r, cols * 4, 3)

    def body(a_ref, b_ref, o_ref):
        o_ref[...] = a_ref[...] + b_ref[...]

    blk = pl.BlockSpec((tr, cols), lambda i: (i, 0))
    return pl.pallas_call(body, name=name, grid=(r // tr,), in_specs=[blk, blk], out_specs=blk,
                          out_shape=jax.ShapeDtypeStruct(a.shape, a.dtype), compiler_params=_params(("parallel",)))(a, b)


def _sum_slots(parts, name):
    n, h, cols = parts.shape
    tr = _row_tile(h, cols * 4, n + 1)

    def body(p_ref, o_ref):
        acc = p_ref[0].astype(F32)
        for k in range(1, n):
            acc = acc + p_ref[k].astype(F32)
        o_ref[...] = acc

    return pl.pallas_call(
        body, name=name, grid=(h // tr,), in_specs=[pl.BlockSpec((n, tr, cols), lambda i: (0, i, 0))],
        out_specs=pl.BlockSpec((tr, cols), lambda i: (i, 0)), out_shape=jax.ShapeDtypeStruct((h, cols), F32),
        compiler_params=_params(("parallel",)),
    )(parts)


def _adamw(w, g, m, v, name):
    shape = w.shape
    cols = shape[-1]
    rows = math.prod(shape[:-1])
    tr = rows
    if rows % 8 == 0:
        tr = 8
        for t in range(8, rows + 1, 8):
            if rows % t == 0 and t * cols * 4 <= 1536 * 1024:
                tr = t
    c1 = 1.0 - ADAM_B1 ** ADAM_STEP
    c2 = 1.0 - ADAM_B2 ** ADAM_STEP

    def body(w_ref, g_ref, m_ref, v_ref, d_ref, nm_ref, nv_ref):
        gg = g_ref[...]
        nm = ADAM_B1 * m_ref[...] + (1.0 - ADAM_B1) * gg
        nv = ADAM_B2 * v_ref[...] + (1.0 - ADAM_B2) * (gg * gg)
        d_ref[...] = -ADAM_LR * ((nm / c1) / (jnp.sqrt(nv / c2) + ADAM_EPS) + ADAM_WD * w_ref[...])
        nm_ref[...] = nm
        nv_ref[...] = nv

    blk = pl.BlockSpec((tr, cols), lambda i: (i, 0))
    flat = jax.ShapeDtypeStruct((rows, cols), F32)
    outs = pl.pallas_call(
        body, name=name, grid=(rows // tr,), in_specs=[blk] * 4, out_specs=[blk] * 3, out_shape=[flat] * 3,
        compiler_params=_params(("parallel",)),
    )(*[a.reshape(rows, cols) for a in (w, g, m, v)])
    return [o.reshape(shape) for o in outs]


def _adamw_update(w, g, m, v):
    c1 = 1.0 - ADAM_B1 ** ADAM_STEP
    c2 = 1.0 - ADAM_B2 ** ADAM_STEP
    nm = ADAM_B1 * m + (1.0 - ADAM_B1) * g
    nv = ADAM_B2 * v + (1.0 - ADAM_B2) * (g * g)
    return -ADAM_LR * ((nm / c1) / (jnp.sqrt(nv / c2) + ADAM_EPS) + ADAM_WD * w), nm, nv


def _adamw_projection(w, m, v, mine, theirs, core, name):
    _, r, cols = w.shape
    h = r // 2
    tr = _row_tile(h, cols * 4, 12, budget=40 * 1024 * 1024)
    nt = h // tr

    def body(core_ref, w_ref, m_ref, v_ref, a0, b0, a1, b1, g_ref, d_ref, nm_ref, nv_ref):
        layer, half = pl.program_id(0), pl.program_id(1)
        own = half == core_ref[0]
        g = jnp.where(layer == 0, jnp.where(own, a0[...], b0[...]), jnp.where(own, a1[...], b1[...]))
        g_ref[...] = g
        d_ref[...], nm_ref[...], nv_ref[...] = _adamw_update(w_ref[...], g, m_ref[...], v_ref[...])

    def piece(layer, own):
        def index(l, hh, i, core_ref):
            used = (l == layer) & ((hh == core_ref[0]) == own)
            return (jnp.where(used, i, 0), 0)
        return pl.BlockSpec((tr, cols), index)

    whole = pl.BlockSpec((None, tr, cols), lambda l, hh, i, core_ref: (l, hh * nt + i, 0))
    return pl.pallas_call(
        body, name=name,
        grid_spec=pltpu.PrefetchScalarGridSpec(
            num_scalar_prefetch=1, grid=(2, 2, nt),
            in_specs=[whole] * 3 + [piece(0, True), piece(0, False), piece(1, True), piece(1, False)],
            out_specs=[whole] * 4),
        out_shape=[jax.ShapeDtypeStruct(w.shape, F32)] * 4,
        compiler_params=_params(("arbitrary", "arbitrary", "arbitrary")),
    )(core.reshape(1), w, m, v, mine[0], theirs[0], mine[1], theirs[1])


def _chip_shape(name):
    shape = list(FULL_SHAPES[name])
    axis = BIG_SHARD_AXIS.get(name, SMALL_SHARD_AXIS.get(name))
    if axis is not None:
        shape[axis] //= N_CHIPS
    return tuple(shape)


def _round_up(n, k):
    return -(-n // k) * k


SHARDED_SMALL_WORDS = sum(math.prod(_chip_shape(n)) for n in SMALL_SHARDED)
SHARDED_SMALL_ROWS = _round_up(-(-SHARDED_SMALL_WORDS // PACK_COLS), 32)
SMALL_MATRICES = ['even_a_ws', 'odd_w_a', 'odd_w_x', 'odd_w_pool']
SMALL_VECTORS = [n for n in SMALL_NAMES if n not in SMALL_MATRICES]
VECTOR_ROWS = _round_up(-(-(sum(math.prod(FULL_SHAPES[n]) for n in SMALL_VECTORS) + 1) // PACK_COLS), 32)


def _pack_rows(flat, rows):
    return jnp.pad(flat, (0, rows * PACK_COLS - flat.shape[0])).reshape(rows, PACK_COLS)


def _pack_small_sharded(shards):
    return _pack_rows(jnp.concatenate([shards[n].reshape(-1) for n in SMALL_SHARDED]), SHARDED_SMALL_ROWS)


def _unpack_small_sharded(gathered_small):
    out = {}
    words = gathered_small.reshape(N_CHIPS, -1)
    w0 = 0
    for n in SMALL_SHARDED:
        size = math.prod(_chip_shape(n))
        blocks = words[:, w0:w0 + size].reshape((N_CHIPS,) + _chip_shape(n))
        out[n] = jnp.concatenate([blocks[j] for j in range(N_CHIPS)], axis=SMALL_SHARD_AXIS[n])
        w0 += size
    return out


def _small_grad_parts(grads, loss):
    words = [grads[n].reshape(-1) for n in SMALL_VECTORS] + [loss.reshape(-1)]
    return ([_pack_rows(jnp.concatenate(words), VECTOR_ROWS)]
            + [grads[n].reshape(-1, FULL_SHAPES[n][-1]) for n in SMALL_MATRICES])


def _own_block(name, full, chip):
    if name not in SMALL_SHARD_AXIS:
        return full
    axis = SMALL_SHARD_AXIS[name]
    width = FULL_SHAPES[name][axis] // N_CHIPS
    return lax.dynamic_slice_in_dim(full, chip * width, width, axis)


def _from_small_grad_parts(totals, chip):
    flat = totals[0].reshape(-1)
    out = {}
    w0 = 0
    for n in SMALL_VECTORS:
        size = math.prod(FULL_SHAPES[n])
        out[n] = _own_block(n, flat[w0:w0 + size].reshape(FULL_SHAPES[n]), chip)
        w0 += size
    for n, total in zip(SMALL_MATRICES, totals[1:]):
        out[n] = _own_block(n, total.reshape(FULL_SHAPES[n]), chip)
    return out, flat[w0]


def kernel(x, positions, even_w_in, even_a_ln_g, even_a_ln_b, even_a_ws, even_a_bs, even_b_sinks, even_w_out, even_ln_g, even_ln_b, odd_w_in, odd_conv_w, odd_conv_b, odd_w_a, odd_b_a, odd_w_x, odd_b_x, odd_lam, odd_w_pool, odd_d_scale, odd_w_out, odd_ln_g, odd_ln_b, loss_target, m_even_w_in, m_even_a_ln_g, m_even_a_ln_b, m_even_a_ws, m_even_a_bs, m_even_b_sinks, m_even_w_out, m_even_ln_g, m_even_ln_b, m_odd_w_in, m_odd_conv_w, m_odd_conv_b, m_odd_w_a, m_odd_b_a, m_odd_w_x, m_odd_b_x, m_odd_lam, m_odd_w_pool, m_odd_d_scale, m_odd_w_out, m_odd_ln_g, m_odd_ln_b, v_even_w_in, v_even_a_ln_g, v_even_a_ln_b, v_even_a_ws, v_even_a_bs, v_even_b_sinks, v_even_w_out, v_even_ln_g, v_even_ln_b, v_odd_w_in, v_odd_conv_w, v_odd_conv_b, v_odd_w_a, v_odd_b_a, v_odd_w_x, v_odd_b_x, v_odd_lam, v_odd_w_pool, v_odd_d_scale, v_odd_w_out, v_odd_ln_g, v_odd_ln_b):
    local = dict(zip(WEIGHT_NAMES, (even_w_in, even_a_ln_g, even_a_ln_b, even_a_ws, even_a_bs, even_b_sinks, even_w_out, even_ln_g, even_ln_b, odd_w_in, odd_conv_w, odd_conv_b, odd_w_a, odd_b_a, odd_w_x, odd_b_x, odd_lam, odd_w_pool, odd_d_scale, odd_w_out, odd_ln_g, odd_ln_b)))
    mom = dict(zip(WEIGHT_NAMES, (m_even_w_in, m_even_a_ln_g, m_even_a_ln_b, m_even_a_ws, m_even_a_bs, m_even_b_sinks, m_even_w_out, m_even_ln_g, m_even_ln_b, m_odd_w_in, m_odd_conv_w, m_odd_conv_b, m_odd_w_a, m_odd_b_a, m_odd_w_x, m_odd_b_x, m_odd_lam, m_odd_w_pool, m_odd_d_scale, m_odd_w_out, m_odd_ln_g, m_odd_ln_b)))
    vel = dict(zip(WEIGHT_NAMES, (v_even_w_in, v_even_a_ln_g, v_even_a_ln_b, v_even_a_ws, v_even_a_bs, v_even_b_sinks, v_even_w_out, v_even_ln_g, v_even_ln_b, v_odd_w_in, v_odd_conv_w, v_odd_conv_b, v_odd_w_a, v_odd_b_a, v_odd_w_x, v_odd_b_x, v_odd_lam, v_odd_w_pool, v_odd_d_scale, v_odd_w_out, v_odd_ln_g, v_odd_ln_b)))
    mx, my, core = _where_am_i()
    chip = 2 * mx + my

    stored = lambda n, a: jnp.swapaxes(a, 1, 2) if n == 'even_w_in' and W_IN_TRANSPOSED['even'] else a
    names_of = lambda layer: [('even' if layer % 2 == 0 else 'odd') + s for s in ('_w_in', '_w_out')]

    shards = [[stored(n, local[n])[layer // 2].astype(MXU_DTYPE) for n in names_of(layer)] for layer in range(DEPTH)]
    shards[0].insert(0, _pack_small_sharded(local))
    gathers, gather_token = _ici_start('gather', shards, "gather_start")

    def gathered(layer, after):
        own, lands = _ici_wait('gather', gathers[layer], [after], "gather_wait_%d" % layer)
        return _with_own_slot(_forward_to_sibling(lands, "gather_forward_%d" % layer), own, chip)

    first = gathered(0, gather_token)
    weights = dict(local)
    weights.update(_unpack_small_sharded(first[0]))
    projections_of = lambda layer, after: first[1:] if layer == 0 else gathered(layer, after)

    in_flight, mine = [], {}

    def land(*after):
        layer, scatter = in_flight.pop()
        parts, lands = _ici_wait('scatter', scatter, after, "scatter_wait_%d" % layer)
        for n, arrived in zip(names_of(layer), _with_own_slot(lands, parts, chip)):
            mine[n, layer // 2] = _sum_slots(arrived, "big_sum_%s_%d" % (n, layer // 2))

    def projection_grads(layer, dw_in, dw_out):
        if in_flight:
            land(dw_in)
        partial = [dw_in, dw_out]
        from_sibling = _sibling_swap(partial, "big_presum_swap_%d" % layer, other_half_of_axis1=True)
        chip_part = [_add_halves(g, r, core, MXU_DTYPE, "big_presum_add_%s_%d" % (n, layer // 2))
                     for g, r, n in zip(partial, from_sibling, names_of(layer))]
        (scatter,), started[layer] = _ici_start('scatter', [chip_part], "scatter_start_%d" % layer)
        in_flight.append((layer, scatter))
        return started[layer]

    started = {}
    loss, grad_x, grads = _local_step(x[0], positions[0], loss_target[0], weights, projections_of, projection_grads)

    small = _small_grad_parts(grads, loss)
    chip_sums = [_add2(a, b, "small_presum_add_%d" % i)
                 for i, (a, b) in enumerate(zip(small, _sibling_swap(small, "small_swap", after=[started[0]])))]
    (small_gather,), small_token = _ici_start('gather', [chip_sums], "small_gather_start")

    reduced, deltas, new_m, new_v = {}, {}, {}, {}

    def adamw_projections(kind, after):
        names = [kind + '_w_in', kind + '_w_out']
        pieces = [(n, l) for n in names for l in range(2)]
        theirs = dict(zip(pieces, _sibling_swap([mine[p] for p in pieces], "big_join_" + kind, after=after)))
        done = []
        for n in names:
            outs = _adamw_projection(stored(n, local[n]), stored(n, mom[n]), stored(n, vel[n]),
                                     [mine[n, l] for l in range(2)], [theirs[n, l] for l in range(2)], core,
                                     "adamw_" + n)
            reduced[n], deltas[n], new_m[n], new_v[n] = [stored(n, o) for o in outs]
            done.append(outs[-1])
        return done

    done = adamw_projections('odd', [small_token])
    land(grad_x, *done)
    done = adamw_projections('even', [])

    own, lands = _ici_wait('gather', small_gather, done, "small_gather_wait")
    of_chips = _with_own_slot(_forward_to_sibling(lands, "small_gather_forward"), own, chip)
    small_totals, total_loss = _from_small_grad_parts(
        [_sum_slots(g, "small_sum_%d" % i) for i, g in enumerate(of_chips)], chip)
    reduced.update(small_totals)
    for n in SMALL_NAMES:
        deltas[n], new_m[n], new_v[n] = _adamw(local[n], reduced[n], mom[n], vel[n], "adamw_" + n)

    return (total_loss, grad_x[None], *[reduced[n] for n in WEIGHT_NAMES], *[deltas[n] for n in WEIGHT_NAMES],
            *[new_m[n] for n in WEIGHT_NAMES], *[new_v[n] for n in WEIGHT_NAMES])
```

```python
import math

import jax
import jax.numpy as jnp
from jax import lax
from jax.experimental import pallas as pl
from jax.experimental.pallas import tpu as pltpu

F32 = jnp.float32
BF16 = jnp.bfloat16
MXU_DTYPE = jnp.bfloat16

D_MODEL = 2048
DEPTH = 4
A_WIDTH = 1024
A_GROUPS = 8
CHUNK = 128
B_HEAD_DIM = 64
B_Q_HEADS = 16
B_KV_HEADS = 2
WINDOW = 128
ROT_DIM = 16
ROPE_THETA = 500000.0
C_WIDTH = 1024
C_HEADS = 8
CONV_WIDTH = 4
LRU_C = 8.0
D_WIDTH = 1024
D_GROUPS = 4
D_GROUP_DIM = 256
EVEN_IN = 5376
ODD_IN = 4096
DN_ALPHA = (2 * DEPTH) ** 0.25
LN_EPS = 1e-5
ATTN_SCALE = B_HEAD_DIM ** -0.5
NEG_BIG = -1e30

ADAM_LR = 0.001
ADAM_B1 = 0.9
ADAM_B2 = 0.999
ADAM_EPS = 1e-08
ADAM_WD = 0.01
ADAM_STEP = 10

LANES = 128
VMEM_LIMIT = 56 * 1024 * 1024
N_CHIPS = 4
MESH = pl.DeviceIdType.MESH

WEIGHT_NAMES = ['even_w_in', 'even_a_ln_g', 'even_a_ln_b', 'even_a_ws', 'even_a_bs', 'even_b_sinks', 'even_w_out',
                'even_ln_g', 'even_ln_b', 'odd_w_in', 'odd_conv_w', 'odd_conv_b', 'odd_w_a', 'odd_b_a', 'odd_w_x',
                'odd_b_x', 'odd_lam', 'odd_w_pool', 'odd_d_scale', 'odd_w_out', 'odd_ln_g', 'odd_ln_b']
BIG_SHARD_AXIS = {'even_w_in': 2, 'even_w_out': 1, 'odd_w_in': 2, 'odd_w_out': 1}
SMALL_SHARD_AXIS = {'odd_conv_w': 2, 'odd_conv_b': 1, 'odd_b_a': 1, 'odd_b_x': 1, 'odd_lam': 1, 'odd_w_pool': 2,
                    'odd_d_scale': 1, 'odd_ln_g': 1, 'odd_ln_b': 1}
FULL_SHAPES = {
    'even_w_in': (2, 2048, 5376), 'even_a_ln_g': (2, 1024), 'even_a_ln_b': (2, 1024), 'even_a_ws': (2, 8, 128, 128),
    'even_a_bs': (2, 8, 128), 'even_b_sinks': (2, 16), 'even_w_out': (2, 2048, 2048), 'even_ln_g': (2, 2048),
    'even_ln_b': (2, 2048), 'odd_w_in': (2, 2048, 4096), 'odd_conv_w': (2, 4, 1024), 'odd_conv_b': (2, 1024),
    'odd_w_a': (2, 8, 128, 128), 'odd_b_a': (2, 1024), 'odd_w_x': (2, 8, 128, 128), 'odd_b_x': (2, 1024),
    'odd_lam': (2, 1024), 'odd_w_pool': (2, 4, 256, 256), 'odd_d_scale': (2, 1024), 'odd_w_out': (2, 2048, 2048),
    'odd_ln_g': (2, 2048), 'odd_ln_b': (2, 2048)}
BIG_NAMES = ['even_w_in', 'even_w_out', 'odd_w_in', 'odd_w_out']
SMALL_SHARDED = ['odd_conv_w', 'odd_conv_b', 'odd_b_a', 'odd_b_x', 'odd_lam', 'odd_d_scale', 'odd_ln_g', 'odd_ln_b',
                 'odd_w_pool']
SMALL_NAMES = [n for n in WEIGHT_NAMES if n not in BIG_NAMES]
W_IN_TRANSPOSED = {'even': True, 'odd': False}
PACK_COLS = 1024


def _params(sem):
    return pltpu.CompilerParams(dimension_semantics=sem, vmem_limit_bytes=VMEM_LIMIT)


def _dot(a, b):
    return lax.dot_general(a, b, (((1,), (0,)), ((), ())), preferred_element_type=F32)


def _dot_nt(a, b):
    return lax.dot_general(a, b, (((1,), (1,)), ((), ())), preferred_element_type=F32)


def _dot_tn(a, b):
    return lax.dot_general(a, b, (((0,), (0,)), ((), ())), preferred_element_type=F32)


def _sigmoid(x):
    return 1.0 / (1.0 + jnp.exp(-x))


def _silu_and_grad(x):
    s = _sigmoid(x)
    return x * s, s * (1.0 + x * (1.0 - s))


def _mm_call(body, name, grid, in_specs, out_spec, out_shape, args):
    return pl.pallas_call(body, name=name, grid=grid, in_specs=in_specs, out_specs=out_spec, out_shape=out_shape,
                          compiler_params=_params(("parallel",) * len(grid)))(*args)


def _resident(block, index_map):
    return pl.BlockSpec(block, index_map, pipeline_mode=pl.Buffered(1))


def _rows_per_step(x, rows):
    return rows if x.dtype.itemsize == 2 else min(1024, rows)


def _mm_in(x, wg, transposed, name):
    s, k = x.shape
    nc = wg.shape[1] if transposed else wg.shape[2]
    tm = _rows_per_step(x, s)

    def body(x_ref, w_ref, o_ref):
        o_ref[...] = (_dot_nt if transposed else _dot)(x_ref[...].astype(MXU_DTYPE), w_ref[...])

    x_spec = _resident((tm, k), lambda i, j: (0, 0)) if tm == s else pl.BlockSpec((tm, k), lambda i, j: (i, 0))
    return _mm_call(body, name, (s // tm, N_CHIPS),
                    [x_spec, pl.BlockSpec((None,) + wg.shape[1:], lambda i, j: (j, 0, 0))],
                    pl.BlockSpec((None, tm, nc), lambda i, j: (j, i, 0)),
                    jax.ShapeDtypeStruct((N_CHIPS, s, nc), F32), (x, wg))


def _mm_dx(dh, wg, dz, transposed, name):
    s, d = dz.shape
    nc = dh.shape[2]
    tm, tn = min(1024, s), 512

    def body(a_ref, w_ref, r_ref, o_ref):
        acc = DN_ALPHA * r_ref[...]
        for j in range(N_CHIPS):
            acc = acc + (_dot if transposed else _dot_nt)(a_ref[j], w_ref[j])
        o_ref[...] = acc

    w_spec = (pl.BlockSpec((N_CHIPS, nc, tn), lambda i, j: (0, 0, j)) if transposed
              else pl.BlockSpec((N_CHIPS, tn, nc), lambda i, j: (0, j, 0)))
    return _mm_call(body, name, (s // tm, d // tn),
                    [pl.BlockSpec((N_CHIPS, tm, nc), lambda i, j: (0, i, 0)), w_spec,
                     pl.BlockSpec((tm, tn), lambda i, j: (i, j))],
                    pl.BlockSpec((tm, tn), lambda i, j: (i, j)), jax.ShapeDtypeStruct((s, d), F32), (dh, wg, dz))


def _mm_dw_in(x, dh, transposed, name):
    s, d = x.shape
    nc = dh.shape[2]
    tm = _rows_per_step(x, d)

    def body(x_ref, b_ref, o_ref):
        xb = x_ref[...].astype(MXU_DTYPE)
        acc = _dot_tn(b_ref[...], xb) if transposed else _dot_tn(xb, b_ref[...])
        o_ref[...] = acc.astype(o_ref.dtype)

    if transposed:
        out_spec, shape = pl.BlockSpec((None, nc, tm), lambda i, j: (j, 0, i)), (N_CHIPS, nc, d)
    else:
        out_spec, shape = pl.BlockSpec((None, tm, nc), lambda i, j: (j, i, 0)), (N_CHIPS, d, nc)
    x_spec = _resident((s, tm), lambda i, j: (0, 0)) if tm == d else pl.BlockSpec((s, tm), lambda i, j: (0, i))
    return _mm_call(body, name, (d // tm, N_CHIPS),
                    [x_spec, pl.BlockSpec((None, s, nc), lambda i, j: (j, 0, 0))],
                    out_spec, jax.ShapeDtypeStruct(shape, MXU_DTYPE), (x, dh))


def _mm_out_norm(mix, wg, x, g, b, name):
    s, k = mix.shape
    _, kc, d = wg.shape
    tm = min(512, s)

    def body(a_ref, w_ref, x_ref, g_ref, b_ref, z_ref, o_ref, ob_ref):
        z = DN_ALPHA * x_ref[...] + _dot(a_ref[...], w_ref[...].reshape(N_CHIPS * kc, d))
        mu = jnp.mean(z, axis=1, keepdims=True)
        zc = z - mu
        var = jnp.mean(zc * zc, axis=1, keepdims=True)
        o = zc * lax.rsqrt(var + LN_EPS) * g_ref[...] + b_ref[...]
        z_ref[...] = z
        o_ref[...] = o
        ob_ref[...] = o.astype(ob_ref.dtype)

    row = pl.BlockSpec((tm, d), lambda i: (i, 0))
    vec = pl.BlockSpec((1, d), lambda i: (0, 0))
    return pl.pallas_call(
        body, name=name, grid=(s // tm,),
        in_specs=[pl.BlockSpec((tm, k), lambda i: (i, 0)), _resident((N_CHIPS, kc, d), lambda i: (0, 0, 0)), row, vec,
                  vec],
        out_specs=[row, row, row],
        out_shape=[jax.ShapeDtypeStruct((s, d), F32), jax.ShapeDtypeStruct((s, d), F32),
                   jax.ShapeDtypeStruct((s, d), MXU_DTYPE)],
        compiler_params=_params(("parallel",)),
    )(mix, wg, x, g.reshape(1, d), b.reshape(1, d))


def _mm_dmix(dz, wg, name):
    s, d = dz.shape
    kc = wg.shape[1]

    def body(a_ref, w_ref, o_ref):
        o_ref[...] = _dot_nt(a_ref[...], w_ref[...])

    return _mm_call(body, name, (N_CHIPS,),
                    [_resident((s, d), lambda j: (0, 0)), pl.BlockSpec((None, kc, d), lambda j: (j, 0, 0))],
                    pl.BlockSpec((s, kc), lambda j: (0, j)), jax.ShapeDtypeStruct((s, N_CHIPS * kc), F32), (dz, wg))


def _mm_dw_out(mix, dz, name):
    s, k = mix.shape
    d = dz.shape[1]
    kc = k // N_CHIPS

    def body(a_ref, b_ref, o_ref):
        o_ref[...] = _dot_tn(a_ref[...], b_ref[...]).astype(o_ref.dtype)

    return _mm_call(body, name, (N_CHIPS,),
                    [pl.BlockSpec((s, kc), lambda i: (0, i)), _resident((s, d), lambda i: (0, 0))],
                    pl.BlockSpec((None, kc, d), lambda i: (i, 0, 0)),
                    jax.ShapeDtypeStruct((N_CHIPS, kc, d), MXU_DTYPE), (mix, dz))


def _ln_bwd(dout, z, g, after):
    s, d = z.shape
    tr = min(256, s)

    def body(do_ref, z_ref, g_ref, after_ref, dz_ref, dzb_ref, dg_ref, db_ref):
        @pl.when(pl.program_id(0) == 0)
        def _():
            dg_ref[...] = jnp.zeros_like(dg_ref)
            db_ref[...] = jnp.zeros_like(db_ref)

        zz = z_ref[...]
        do = do_ref[...]
        mu = jnp.mean(zz, axis=1, keepdims=True)
        zc = zz - mu
        var = jnp.mean(zc * zc, axis=1, keepdims=True)
        rstd = lax.rsqrt(var + LN_EPS)
        zh = zc * rstd
        dg_ref[...] += jnp.sum(do * zh, axis=0, keepdims=True)
        db_ref[...] += jnp.sum(do, axis=0, keepdims=True)
        dzh = do * g_ref[...]
        dz = rstd * (dzh - jnp.mean(dzh, axis=1, keepdims=True) - zh * jnp.mean(dzh * zh, axis=1, keepdims=True))
        dz_ref[...] = dz
        dzb_ref[...] = dz.astype(MXU_DTYPE)

    row = pl.BlockSpec((tr, d), lambda i: (i, 0))
    vec = pl.BlockSpec((1, d), lambda i: (0, 0))
    return pl.pallas_call(
        body, name="ln_bwd", grid=(s // tr,), in_specs=[row, row, vec, ANY], out_specs=[row, row, vec, vec],
        out_shape=[jax.ShapeDtypeStruct((s, d), F32), jax.ShapeDtypeStruct((s, d), MXU_DTYPE),
                   jax.ShapeDtypeStruct((1, d), F32), jax.ShapeDtypeStruct((1, d), F32)],
        compiler_params=_params(("arbitrary",)),
    )(dout, z, g.reshape(1, d), after)


def _loss_and_grad(out, tgt):
    s, d = out.shape
    tr = min(256, s)

    def body(o_ref, t_ref, l_ref, do_ref):
        @pl.when(pl.program_id(0) == 0)
        def _():
            l_ref[...] = jnp.zeros_like(l_ref)

        e = o_ref[...] - t_ref[...]
        do_ref[...] = e * (1.0 / d)
        rows = jnp.sum(e * e, axis=1, keepdims=True)
        l_ref[...] += (0.5 / d) * jnp.sum(rows, axis=0, keepdims=True)

    row = pl.BlockSpec((tr, d), lambda i: (i, 0))
    return pl.pallas_call(
        body, name="loss", grid=(s // tr,), in_specs=[row, row],
        out_specs=[pl.BlockSpec((1, 1), lambda i: (0, 0)), row],
        out_shape=[jax.ShapeDtypeStruct((1, 1), F32), jax.ShapeDtypeStruct((s, d), F32)],
        compiler_params=_params(("arbitrary",)),
    )(out, tgt)


Q0, K0, V0, BG0 = 3072, 4096, 4224, 4352
T = CHUNK


def _lane_ids(width):
    return lax.broadcasted_iota(jnp.int32, (T, width), 1)


def _rope_swap(x):
    w = x.shape[1]
    l64 = _lane_ids(w) % B_HEAD_DIM
    half = ROT_DIM // 2
    return jnp.where(l64 < half, pltpu.roll(x, w - half, 1), jnp.where(l64 < ROT_DIM, pltpu.roll(x, half, 1), 0.0))


def _rope(x, c, s):
    return x * c + _rope_swap(x) * s


def _rope_bwd(d, c, s):
    return d * c + _rope_swap(d * s)


def _tril():
    return lax.broadcasted_iota(jnp.int32, (T, T), 0) >= lax.broadcasted_iota(jnp.int32, (T, T), 1)


def _band_mask(has_prev):
    qi = lax.broadcasted_iota(jnp.int32, (T, 2 * T), 0)
    kj = lax.broadcasted_iota(jnp.int32, (T, 2 * T), 1)
    return (kj > qi) & (kj <= qi + WINDOW) & ((kj >= WINDOW) | has_prev)


def _a_norm(v, g, b):
    mu = jnp.mean(v, axis=1, keepdims=True)
    vc = v - mu
    var = jnp.mean(vc * vc, axis=1, keepdims=True)
    rstd = lax.rsqrt(var + LN_EPS)
    vh = vc * rstd
    return vh, rstd, vh * g + b


def _kv_operands(hk, k_rot, kp_rot, vv, vp):
    lo = _lane_ids(LANES) < B_HEAD_DIM
    mine = lo if hk == 0 else jnp.logical_not(lo)
    both = lambda t: jnp.where(mine, t, 0.0) + pltpu.roll(jnp.where(mine, t, 0.0), B_HEAD_DIM, 1)
    k2 = jnp.concatenate([both(kp_rot), both(k_rot)], axis=0)
    v2 = jnp.concatenate([both(vp), both(vv)], axis=0)
    return k2.astype(MXU_DTYPE), v2.astype(MXU_DTYPE)


GROUP_HEADS = B_Q_HEADS // B_KV_HEADS


def _pair_columns(hk):
    return [(hk * GROUP_HEADS // 2 + pp) * LANES for pp in range(GROUP_HEADS // 2)]


def _stacked_heads(pairs):
    lo = _lane_ids(LANES) < B_HEAD_DIM
    rows = []
    for tile in pairs:
        rows += [jnp.where(lo, tile, 0.0), jnp.where(lo, 0.0, tile)]
    return jnp.concatenate(rows, axis=0).astype(MXU_DTYPE)


def _pair_of(stacked, pp):
    lo = _lane_ids(LANES) < B_HEAD_DIM
    return jnp.where(lo, stacked[2 * pp * T:(2 * pp + 1) * T], stacked[(2 * pp + 1) * T:(2 * pp + 2) * T])


def _softmax_with_sink(qm, k2, valid, sink_ref, head0):
    scores = (_dot_nt(qm, k2) * ATTN_SCALE).reshape(GROUP_HEADS, T, 2 * T)
    s = jnp.where(valid[None], scores, NEG_BIG).reshape(GROUP_HEADS * T, 2 * T)
    sink = jnp.concatenate([jnp.broadcast_to(sink_ref[:, head0 + h:head0 + h + 1], (T, 1))
                            for h in range(GROUP_HEADS)], axis=0)
    m = jnp.maximum(jnp.max(s, axis=1, keepdims=True), sink)
    ex = jnp.exp(s - m)
    es = jnp.exp(sink - m)
    inv = 1.0 / (jnp.sum(ex, axis=1, keepdims=True) + es)
    return ex * inv, es * inv


EVEN_SHARD = EVEN_IN // N_CHIPS
KV_CHIP = K0 // EVEN_SHARD
KV_SPAN = 3 * LANES
KP0, VP0 = K0 - KV_CHIP * EVEN_SHARD, V0 - KV_CHIP * EVEN_SHARD
assert V0 // EVEN_SHARD == KV_CHIP and VP0 + LANES <= KV_SPAN


def _even_specs(s):
    nb = s // T
    return nb, {
        'h': lambda r: pl.BlockSpec((N_CHIPS, T, EVEN_SHARD), lambda i: (0, r(i), 0)),
        'kvprev': lambda r: pl.BlockSpec((None, T, KV_SPAN), lambda i: (KV_CHIP, jnp.maximum(r(i) - 1, 0), 0)),
        'tab': lambda r: pl.BlockSpec((T, A_WIDTH), lambda i: (r(i), 0)),
        'tabprev': lambda r: pl.BlockSpec((T, LANES), lambda i: (jnp.maximum(r(i) - 1, 0), 0)),
    }


def _rows_from_shards(h4_ref, rows_ref):
    for j in range(N_CHIPS):
        rows_ref[:, j * EVEN_SHARD:(j + 1) * EVEN_SHARD] = h4_ref[j]


def _full(shape):
    nd = len(shape)
    return pl.BlockSpec(shape, lambda i: (0,) * nd)


def _even_fwd(h4, cq, sq, ln_g, ln_b, ws, bs_t, sinks):
    s = h4.shape[1]
    nb, sp = _even_specs(s)

    def body(h4_ref, kvp_ref, cq_ref, sq_ref, ckp_ref, skp_ref, g_ref, b_ref, ws_ref, bst_ref, sink_ref, mix_ref, h_ref):
        i = pl.program_id(0)
        _rows_from_shards(h4_ref, h_ref)
        tril = _tril()
        _, _, vn = _a_norm(h_ref[:, 1024:2048], g_ref[...], b_ref[...])
        vnb = vn.astype(MXU_DTYPE)
        for g in range(A_GROUPS):
            c0, c1 = g * LANES, (g + 1) * LANES
            wg = jnp.where(tril, ws_ref[g], 0.0).astype(MXU_DTYPE)
            mixed = _dot(wg, vnb[:, c0:c1]) + bst_ref[:, g:g + 1]
            sil, _ = _silu_and_grad(h_ref[:, 2048 + c0:2048 + c1])
            mix_ref[:, c0:c1] = (h_ref[:, c0:c1] * mixed * sil).astype(mix_ref.dtype)

        k_rot = _rope(h_ref[:, K0:K0 + LANES], cq_ref[:, 0:LANES], sq_ref[:, 0:LANES])
        kp_rot = _rope(kvp_ref[:, KP0:KP0 + LANES], ckp_ref[...], skp_ref[...])
        vp = kvp_ref[:, VP0:VP0 + LANES]
        valid = _band_mask(i > 0)
        for hk in range(B_KV_HEADS):
            k2, v2 = _kv_operands(hk, k_rot, kp_rot, h_ref[:, V0:V0 + LANES], vp)
            qm = _stacked_heads([_rope(h_ref[:, Q0 + c0:Q0 + c0 + LANES], cq_ref[:, c0:c0 + LANES],
                                       sq_ref[:, c0:c0 + LANES]) for c0 in _pair_columns(hk)])
            p, _ = _softmax_with_sink(qm, k2, valid, sink_ref, GROUP_HEADS * hk)
            out = _dot(p.astype(MXU_DTYPE), v2)
            for pp, c0 in enumerate(_pair_columns(hk)):
                sil, _ = _silu_and_grad(h_ref[:, BG0 + c0:BG0 + c0 + LANES])
                mix_ref[:, A_WIDTH + c0:A_WIDTH + c0 + LANES] = (_pair_of(out, pp) * sil).astype(mix_ref.dtype)

    ident = lambda i: i
    return pl.pallas_call(
        body, name="even_fwd", grid=(nb,),
        in_specs=[sp['h'](ident), sp['kvprev'](ident), sp['tab'](ident), sp['tab'](ident),
                  sp['tabprev'](ident), sp['tabprev'](ident), _full((1, A_WIDTH)), _full((1, A_WIDTH)),
                  _full((A_GROUPS, T, T)), _full((T, A_GROUPS)), _full((1, B_Q_HEADS))],
        out_specs=pl.BlockSpec((T, 2 * A_WIDTH), lambda i: (i, 0)),
        out_shape=jax.ShapeDtypeStruct((s, 2 * A_WIDTH), MXU_DTYPE),
        scratch_shapes=[pltpu.VMEM((T, EVEN_IN), F32)],
        compiler_params=_params(("parallel",)),
    )(h4, h4, cq, sq, cq, sq, ln_g.reshape(1, -1), ln_b.reshape(1, -1), ws, bs_t, sinks.reshape(1, -1))


def _even_bwd(h4, dmix, cq, sq, ln_g, ln_b, ws, bs_t, sinks):
    s = h4.shape[1]
    nb, sp = _even_specs(s)

    def body(h4_ref, kvp_ref, cq_ref, sq_ref, ckp_ref, skp_ref, dm_ref, g_ref, b_ref, ws_ref, bst_ref, sink_ref,
             dh4_ref, dg_ref, db_ref, dws_ref, dbst_ref, dsink_ref, h_ref, dh_ref, dvn_scr, ck_scr, cv_scr):
        i = pl.program_id(0)

        @pl.when(i == 0)
        def _():
            for ref in (dg_ref, db_ref, dws_ref, dbst_ref, dsink_ref, ck_scr, cv_scr):
                ref[...] = jnp.zeros_like(ref)

        _rows_from_shards(h4_ref, h_ref)

        tril = _tril()
        vh, rstd, vn = _a_norm(h_ref[:, 1024:2048], g_ref[...], b_ref[...])
        vnb = vn.astype(MXU_DTYPE)
        for g in range(A_GROUPS):
            c0, c1 = g * LANES, (g + 1) * LANES
            wg = jnp.where(tril, ws_ref[g], 0.0).astype(MXU_DTYPE)
            mixed = _dot(wg, vnb[:, c0:c1]) + bst_ref[:, g:g + 1]
            ag = h_ref[:, 2048 + c0:2048 + c1]
            u = h_ref[:, c0:c1]
            sil, dsil = _silu_and_grad(ag)
            da = dm_ref[:, c0:c1]
            dh_ref[:, c0:c1] = (da * mixed * sil).astype(dh_ref.dtype)
            dh_ref[:, 2048 + c0:2048 + c1] = (da * u * mixed * dsil).astype(dh_ref.dtype)
            dmx = da * u * sil
            dmb = dmx.astype(MXU_DTYPE)
            dvn_scr[:, c0:c1] = _dot_tn(wg, dmb)
            dws_ref[g] += jnp.where(tril, _dot_nt(dmb, vnb[:, c0:c1]), 0.0)
            dbst_ref[:, g:g + 1] += jnp.sum(dmx, axis=1, keepdims=True)
        dvn = dvn_scr[...]
        dg_ref[...] += jnp.sum(dvn * vh, axis=0, keepdims=True)
        db_ref[...] += jnp.sum(dvn, axis=0, keepdims=True)
        dvh = dvn * g_ref[...]
        dv = rstd * (dvh - jnp.mean(dvh, axis=1, keepdims=True) - vh * jnp.mean(dvh * vh, axis=1, keepdims=True))
        dh_ref[:, 1024:2048] = dv.astype(dh_ref.dtype)

        ck_cur, sk_cur = cq_ref[:, 0:LANES], sq_ref[:, 0:LANES]
        k_rot = _rope(h_ref[:, K0:K0 + LANES], ck_cur, sk_cur)
        kp_rot = _rope(kvp_ref[:, KP0:KP0 + LANES], ckp_ref[...], skp_ref[...])
        vp = kvp_ref[:, VP0:VP0 + LANES]
        valid = _band_mask(i < nb - 1)
        dk_all = jnp.zeros((2 * T, LANES), F32)
        dv_all = jnp.zeros((2 * T, LANES), F32)
        for hk in range(B_KV_HEADS):
            k2, v2 = _kv_operands(hk, k_rot, kp_rot, h_ref[:, V0:V0 + LANES], vp)
            cols = _pair_columns(hk)
            qm = _stacked_heads([_rope(h_ref[:, Q0 + c0:Q0 + c0 + LANES], cq_ref[:, c0:c0 + LANES],
                                       sq_ref[:, c0:c0 + LANES]) for c0 in cols])
            p, psink = _softmax_with_sink(qm, k2, valid, sink_ref, GROUP_HEADS * hk)
            pb = p.astype(MXU_DTYPE)
            out = _dot(pb, v2)
            dobs = []
            for pp, c0 in enumerate(cols):
                sil, dsil = _silu_and_grad(h_ref[:, BG0 + c0:BG0 + c0 + LANES])
                dbo = dm_ref[:, A_WIDTH + c0:A_WIDTH + c0 + LANES]
                dh_ref[:, BG0 + c0:BG0 + c0 + LANES] = dbo * _pair_of(out, pp) * dsil
                dobs.append(dbo * sil)
            dobm = _stacked_heads(dobs)
            dp = _dot_nt(dobm, v2)
            rs = jnp.sum(p * dp, axis=1, keepdims=True)
            dsb = (p * (dp - rs) * ATTN_SCALE).astype(MXU_DTYPE)
            to_sink = psink * rs
            for h in range(GROUP_HEADS):
                head = GROUP_HEADS * hk + h
                dsink_ref[:, head:head + 1] += -jnp.sum(to_sink[h * T:(h + 1) * T], axis=0, keepdims=True)
            dq = _dot(dsb, k2)
            for pp, c0 in enumerate(cols):
                dh_ref[:, Q0 + c0:Q0 + c0 + LANES] = _rope_bwd(_pair_of(dq, pp), cq_ref[:, c0:c0 + LANES],
                                                               sq_ref[:, c0:c0 + LANES])
            acc_k = _dot_tn(dsb, qm)
            acc_v = _dot_tn(pb, dobm)
            lo2 = lax.broadcasted_iota(jnp.int32, (2 * T, LANES), 1) < B_HEAD_DIM
            mine = lo2 if hk == 0 else jnp.logical_not(lo2)
            dk_all = dk_all + jnp.where(mine, acc_k + pltpu.roll(acc_k, B_HEAD_DIM, 1), 0.0)
            dv_all = dv_all + jnp.where(mine, acc_v + pltpu.roll(acc_v, B_HEAD_DIM, 1), 0.0)
        dk_rot = dk_all[T:2 * T] + ck_scr[...]
        dvv = dv_all[T:2 * T] + cv_scr[...]
        ck_scr[...] = dk_all[0:T]
        cv_scr[...] = dv_all[0:T]
        dh_ref[:, K0:K0 + LANES] = _rope_bwd(dk_rot, ck_cur, sk_cur).astype(dh_ref.dtype)
        dh_ref[:, V0:V0 + LANES] = dvv.astype(dh_ref.dtype)
        for j in range(N_CHIPS):
            dh4_ref[j] = dh_ref[:, j * EVEN_SHARD:(j + 1) * EVEN_SHARD].astype(dh4_ref.dtype)

    rev = lambda i: nb - 1 - i
    return pl.pallas_call(
        body, name="even_bwd", grid=(nb,),
        in_specs=[sp['h'](rev), sp['kvprev'](rev), sp['tab'](rev), sp['tab'](rev),
                  sp['tabprev'](rev), sp['tabprev'](rev), pl.BlockSpec((T, 2 * A_WIDTH), lambda i: (rev(i), 0)),
                  _full((1, A_WIDTH)), _full((1, A_WIDTH)), _full((A_GROUPS, T, T)), _full((T, A_GROUPS)),
                  _full((1, B_Q_HEADS))],
        out_specs=[sp['h'](rev), _full((1, A_WIDTH)), _full((1, A_WIDTH)),
                   _full((A_GROUPS, T, T)), _full((T, A_GROUPS)), _full((1, B_Q_HEADS))],
        out_shape=[jax.ShapeDtypeStruct((N_CHIPS, s, EVEN_SHARD), MXU_DTYPE), jax.ShapeDtypeStruct((1, A_WIDTH), F32),
                   jax.ShapeDtypeStruct((1, A_WIDTH), F32), jax.ShapeDtypeStruct((A_GROUPS, T, T), F32),
                   jax.ShapeDtypeStruct((T, A_GROUPS), F32), jax.ShapeDtypeStruct((1, B_Q_HEADS), F32)],
        scratch_shapes=[pltpu.VMEM((T, EVEN_IN), F32), pltpu.VMEM((T, EVEN_IN), F32), pltpu.VMEM((T, A_WIDTH), F32),
                        pltpu.VMEM((T, LANES), F32), pltpu.VMEM((T, LANES), F32)],
        compiler_params=_params(("arbitrary",)),
    )(h4, h4, cq, sq, cq, sq, dmix, ln_g.reshape(1, -1), ln_b.reshape(1, -1), ws, bs_t, sinks.reshape(1, -1))


def _shift_down(x, d, fill=0.0):
    rows = lax.broadcasted_iota(jnp.int32, x.shape, 0)
    return jnp.where(rows >= d, pltpu.roll(x, d, 0), fill)


def _shift_up(x, d, fill=0.0):
    n = x.shape[0]
    rows = lax.broadcasted_iota(jnp.int32, x.shape, 0)
    return jnp.where(rows < n - d, pltpu.roll(x, n - d, 0), fill)


SCAN_BLOCK = 128
SUBLANES = 8


def _scan_block(a, b, up):
    n = a.shape[0]
    d = 1
    while d < n:
        if d < SUBLANES:
            shift = _shift_up if up else _shift_down
            a, b = a * shift(a, d, 1.0), a * shift(b, d) + b
        elif up:
            a, b = (jnp.concatenate([a[:n - d] * a[d:], a[n - d:]], axis=0),
                    jnp.concatenate([a[:n - d] * b[d:] + b[:n - d], b[n - d:]], axis=0))
        else:
            a, b = (jnp.concatenate([a[:d], a[d:] * a[:n - d]], axis=0),
                    jnp.concatenate([b[:d], a[d:] * b[:n - d] + b[d:]], axis=0))
        d *= 2
    return a, b


def _scan(a, b, up):
    n = a.shape[0]
    size = min(SCAN_BLOCK, n)
    blocks = n // size
    out, carry = [None] * blocks, None
    for k in (reversed(range(blocks)) if up else range(blocks)):
        prod, h = _scan_block(a[k * size:(k + 1) * size], b[k * size:(k + 1) * size], up)
        if carry is not None:
            h = h + prod * carry
        carry = h[0:1] if up else h[size - 1:size]
        out[k] = h
    return jnp.concatenate(out, axis=0)


def _neg_expm1(y):
    u = jnp.exp(y)
    um1 = u - 1.0
    safe = jnp.where(um1 == 0.0, 1.0, jnp.log(u))
    near = jnp.where(um1 == 0.0, y, um1 * y / safe)
    return -jnp.where(y > -0.5, near, um1)


def _softplus(x):
    return jnp.maximum(x, 0.0) + jnp.log1p(jnp.exp(-jnp.abs(x)))


def _lru_forward(xc, cw, cb, wa, ba, wx, bx, lam):
    shifted = [_shift_down(xc, CONV_WIDTH - 1 - j) if j < CONV_WIDTH - 1 else xc for j in range(CONV_WIDTH)]
    xconv = cb
    for j in range(CONV_WIDTH):
        xconv = xconv + cw[j:j + 1, :] * shifted[j]
    xb = xconv.astype(MXU_DTYPE)
    r = _sigmoid(_dot(xb, wa.astype(MXU_DTYPE)) + ba)
    ig = _sigmoid(_dot(xb, wx.astype(MXU_DTYPE)) + bx)
    sp = _softplus(-lam)
    log_a = -LRU_C * r * sp
    a = jnp.exp(log_a)
    mult = jnp.sqrt(_neg_expm1(2.0 * log_a))
    hs = _scan(a, mult * ig * xconv, up=False)
    return dict(shifted=shifted, xconv=xconv, xb=xb, r=r, ig=ig, sp=sp, a=a, mult=mult, hs=hs)


def _col(width, off):
    return lambda s: pl.BlockSpec((s, width), lambda i: (0, off + i))


def _part(width, slot):
    return lambda s: pl.BlockSpec((None, s, width), lambda i: (slot, 0, i))


def _lru_specs(s):
    w = LANES
    return [_part(w, 0)(s), _part(w, 1)(s), pl.BlockSpec((CONV_WIDTH, w), lambda i: (0, i)),
            pl.BlockSpec((1, w), lambda i: (0, i)), pl.BlockSpec((None, w, w), lambda i: (i, 0, 0)),
            pl.BlockSpec((1, w), lambda i: (0, i)), pl.BlockSpec((None, w, w), lambda i: (i, 0, 0)),
            pl.BlockSpec((1, w), lambda i: (0, i)), pl.BlockSpec((1, w), lambda i: (0, i))]


def _lru_fwd(h, cw, cb, wa, ba, wx, bx, lam):
    s = h.shape[1]

    def body(xc_ref, cg_ref, cw_ref, cb_ref, wa_ref, ba_ref, wx_ref, bx_ref, lam_ref, o_ref):
        f = _lru_forward(xc_ref[...], cw_ref[...], cb_ref[...], wa_ref[...], ba_ref[...], wx_ref[...], bx_ref[...],
                         lam_ref[...])
        sil, _ = _silu_and_grad(cg_ref[...])
        o_ref[...] = (f['hs'] * sil).astype(o_ref.dtype)

    return pl.pallas_call(
        body, name="lru_fwd", grid=(C_HEADS,), in_specs=_lru_specs(s), out_specs=_col(LANES, 0)(s),
        out_shape=jax.ShapeDtypeStruct((s, C_WIDTH + D_WIDTH), MXU_DTYPE), compiler_params=_params(("parallel",)),
    )(h, h, cw, cb.reshape(1, -1), wa, ba.reshape(1, -1), wx, bx.reshape(1, -1), lam.reshape(1, -1))


def _lru_bwd(h, dmix, cw, cb, wa, ba, wx, bx, lam):
    s = h.shape[1]

    def body(xc_ref, cg_ref, cw_ref, cb_ref, wa_ref, ba_ref, wx_ref, bx_ref, lam_ref, dco_ref,
             dh_ref, dcw_ref, dcb_ref, dwa_ref, dba_ref, dwx_ref, dbx_ref, dlam_ref):
        dxc_ref, dcg_ref = dh_ref.at[0], dh_ref.at[1]
        cw, lam = cw_ref[...], lam_ref[...]
        wab, wxb = wa_ref[...].astype(MXU_DTYPE), wx_ref[...].astype(MXU_DTYPE)
        f = _lru_forward(xc_ref[...], cw, cb_ref[...], wa_ref[...], ba_ref[...], wx_ref[...], bx_ref[...], lam)
        sil, dsil = _silu_and_grad(cg_ref[...])
        dco = dco_ref[...]
        hs, a, mult, ig, r, xconv = f['hs'], f['a'], f['mult'], f['ig'], f['r'], f['xconv']
        dcg_ref[...] = (dco * hs * dsil).astype(dcg_ref.dtype)
        lamb = _scan(_shift_up(a, 1), dco * sil, up=True)
        da = lamb * _shift_down(hs, 1)
        d_ig = lamb * mult * xconv
        d_mult = lamb * ig * xconv
        dxconv = lamb * mult * ig
        d_log_a = da * a - d_mult * (a * a) / mult
        dlam_ref[...] = jnp.sum(d_log_a * r, axis=0, keepdims=True) * LRU_C * _sigmoid(-lam)
        dpa = d_log_a * (-LRU_C * f['sp']) * r * (1.0 - r)
        dpx = d_ig * ig * (1.0 - ig)
        dba_ref[...] = jnp.sum(dpa, axis=0, keepdims=True)
        dbx_ref[...] = jnp.sum(dpx, axis=0, keepdims=True)
        dpab, dpxb = dpa.astype(MXU_DTYPE), dpx.astype(MXU_DTYPE)
        dwa_ref[...] = _dot_tn(f['xb'], dpab)
        dwx_ref[...] = _dot_tn(f['xb'], dpxb)
        dxconv = dxconv + _dot_nt(dpab, wab) + _dot_nt(dpxb, wxb)
        dcb_ref[...] = jnp.sum(dxconv, axis=0, keepdims=True)
        dxc = jnp.zeros_like(dxconv)
        for j in range(CONV_WIDTH):
            dcw_ref[j:j + 1, :] = jnp.sum(dxconv * f['shifted'][j], axis=0, keepdims=True)
            back = _shift_up(dxconv, CONV_WIDTH - 1 - j) if j < CONV_WIDTH - 1 else dxconv
            dxc = dxc + cw[j:j + 1, :] * back
        dxc_ref[...] = dxc.astype(dxc_ref.dtype)

    w = LANES
    vec = pl.BlockSpec((1, w), lambda i: (0, i))
    mat = pl.BlockSpec((None, w, w), lambda i: (i, 0, 0))
    vshape = jax.ShapeDtypeStruct((1, C_WIDTH), F32)
    mshape = jax.ShapeDtypeStruct((C_HEADS, w, w), F32)
    return pl.pallas_call(
        body, name="lru_bwd", grid=(C_HEADS,), in_specs=_lru_specs(s) + [_col(w, 0)(s)],
        out_specs=[pl.BlockSpec((2, s, w), lambda i: (0, 0, i)), pl.BlockSpec((CONV_WIDTH, w), lambda i: (0, i)),
                   vec, mat, vec, mat, vec, vec],
        out_shape=[jax.ShapeDtypeStruct((N_CHIPS, s, C_WIDTH), MXU_DTYPE),
                   jax.ShapeDtypeStruct((CONV_WIDTH, C_WIDTH), F32), vshape, mshape, vshape, mshape, vshape, vshape],
        compiler_params=_params(("parallel",)),
    )(h, h, cw, cb.reshape(1, -1), wa, ba.reshape(1, -1), wx, bx.reshape(1, -1), lam.reshape(1, -1), dmix)


def _pool_window(g):
    return jnp.left_shift(2, g).astype(F32)


def _select_window(g, sums):
    out = sums[-1]
    for k in range(len(sums) - 2, -1, -1):
        out = jnp.where(g == k, sums[k], out)
    return out


def _window_sums(x, shift):
    sums, cur = [], x
    for k in range(D_GROUPS):
        cur = cur + shift(cur, 2 ** k)
        sums.append(cur)
    return sums


def _pool_specs(s):
    w = D_GROUP_DIM
    return [_part(w, 2)(s), _part(w, 3)(s),
            pl.BlockSpec((None, w, w), lambda i: (i, 0, 0)), pl.BlockSpec((1, w), lambda i: (0, i))]


def _pool_forward(xd, g):
    rows = lax.broadcasted_iota(jnp.int32, xd.shape, 0).astype(F32) + 1.0
    cnt = jnp.minimum(rows, _pool_window(g))
    pooled = _select_window(g, _window_sums(xd, _shift_down)) / cnt - xd
    return pooled, cnt


def _pool_fwd(h, mix, w_pool, d_scale):
    s = h.shape[1]

    def body(xd_ref, dg_ref, wp_ref, sc_ref, mix_ref, o_ref):
        pooled, _ = _pool_forward(xd_ref[...], pl.program_id(0))
        mixed = _dot(pooled.astype(MXU_DTYPE), wp_ref[...].astype(MXU_DTYPE))
        sil, _ = _silu_and_grad(dg_ref[...])
        o_ref[...] = (mixed * sc_ref[...] * sil).astype(o_ref.dtype)

    return pl.pallas_call(
        body, name="pool_fwd", grid=(D_GROUPS,), in_specs=_pool_specs(s) + [ANY],
        out_specs=_col(D_GROUP_DIM, C_WIDTH // D_GROUP_DIM)(s), out_shape=jax.ShapeDtypeStruct(mix.shape, mix.dtype),
        input_output_aliases={4: 0}, compiler_params=_params(("parallel",)),
    )(h, h, w_pool, d_scale.reshape(1, -1), mix)


def _pool_bwd(h, dmix, dh, w_pool, d_scale):
    s = h.shape[1]
    w = D_GROUP_DIM

    def body(xd_ref, dg_ref, wp_ref, sc_ref, ddo_ref, dh_in_ref, dh_ref, dwp_ref, dsc_ref):
        dxd_ref, ddg_ref = dh_ref.at[0], dh_ref.at[1]
        g = pl.program_id(0)
        pooled, cnt = _pool_forward(xd_ref[...], g)
        pb = pooled.astype(MXU_DTYPE)
        wpb = wp_ref[...].astype(MXU_DTYPE)
        mixed = _dot(pb, wpb)
        sil, dsil = _silu_and_grad(dg_ref[...])
        ddo = ddo_ref[...]
        sc = sc_ref[...]
        ddg_ref[...] = (ddo * mixed * sc * dsil).astype(ddg_ref.dtype)
        dms = ddo * sil
        dsc_ref[...] = jnp.sum(dms * mixed, axis=0, keepdims=True)
        dmb = (dms * sc).astype(MXU_DTYPE)
        dwp_ref[...] = _dot_tn(pb, dmb)
        dpooled = _dot_nt(dmb, wpb)
        dxd = _select_window(g, _window_sums(dpooled / cnt, _shift_up)) - dpooled
        dxd_ref[...] = dxd.astype(dxd_ref.dtype)

    return pl.pallas_call(
        body, name="pool_bwd", grid=(D_GROUPS,), in_specs=_pool_specs(s) + [_col(w, C_WIDTH // w)(s), ANY],
        out_specs=[pl.BlockSpec((2, s, w), lambda i: (1, 0, i)), pl.BlockSpec((None, w, w), lambda i: (i, 0, 0)),
                   pl.BlockSpec((1, w), lambda i: (0, i))],
        out_shape=[jax.ShapeDtypeStruct(dh.shape, dh.dtype),
                   jax.ShapeDtypeStruct((D_GROUPS, w, w), F32), jax.ShapeDtypeStruct((1, D_WIDTH), F32)],
        input_output_aliases={5: 0}, compiler_params=_params(("parallel",)),
    )(h, h, w_pool, d_scale.reshape(1, -1), dmix, dh)


def _rope_tables(positions):
    s = positions.shape[0]
    inv_freq = ROPE_THETA ** (-jnp.arange(0, ROT_DIM, 2, dtype=F32) / ROT_DIM)
    ang = positions.astype(F32)[:, None] * inv_freq
    c, sn = jnp.cos(ang), jnp.sin(ang)
    rest = B_HEAD_DIM - ROT_DIM
    c64 = jnp.concatenate([c, c, jnp.ones((s, rest), F32)], axis=1)
    s64 = jnp.concatenate([-sn, sn, jnp.zeros((s, rest), F32)], axis=1)
    return jnp.tile(c64, (1, B_Q_HEADS)), jnp.tile(s64, (1, B_Q_HEADS))


def _local_step(x, positions, target, w, projections_of, projection_grads):
    cq, sq = _rope_tables(positions)
    saved = []
    xin, xin_b = x, x
    for layer in range(DEPTH):
        j = layer // 2
        kind = 'even' if layer % 2 == 0 else 'odd'
        w_in, w_out = projections_of(layer, xin_b)
        h = _mm_in(xin_b, w_in, W_IN_TRANSPOSED[kind], "mm_" + kind + "_in")
        if kind == 'even':
            mix = _even_fwd(h, cq, sq, w['even_a_ln_g'][j], w['even_a_ln_b'][j], w['even_a_ws'][j],
                            w['even_a_bs'][j].T, w['even_b_sinks'][j])
        else:
            mix = _lru_fwd(h, w['odd_conv_w'][j], w['odd_conv_b'][j], w['odd_w_a'][j], w['odd_b_a'][j],
                           w['odd_w_x'][j], w['odd_b_x'][j], w['odd_lam'][j])
            mix = _pool_fwd(h, mix, w['odd_w_pool'][j], w['odd_d_scale'][j])
        z, out, out_b = _mm_out_norm(mix, w_out, xin, w[kind + '_ln_g'][j], w[kind + '_ln_b'][j], "mm_out_norm")
        saved.append((xin_b, h, mix, z, w_in, w_out))
        xin, xin_b = out, out_b

    loss, dout = _loss_and_grad(xin, target)
    per_layer = {n: [None, None] for n in SMALL_NAMES}
    behind = loss
    for layer in reversed(range(DEPTH)):
        j = layer // 2
        kind = 'even' if layer % 2 == 0 else 'odd'
        xin_b, h, mix, z, w_in, w_out = saved[layer]
        dz, dz_b, dg, db = _ln_bwd(dout, z, w[kind + '_ln_g'][j], behind)
        per_layer[kind + '_ln_g'][j], per_layer[kind + '_ln_b'][j] = dg[0], db[0]
        dmix = _mm_dmix(dz_b, w_out, "mm_dmix")
        dw_out = _mm_dw_out(mix, dz_b, "mm_dw_out")
        if kind == 'even':
            dh, dlg, dlb, dws, dbst, dsink = _even_bwd(
                h, dmix, cq, sq, w['even_a_ln_g'][j], w['even_a_ln_b'][j], w['even_a_ws'][j], w['even_a_bs'][j].T,
                w['even_b_sinks'][j])
            for n, val in (('even_a_ln_g', dlg[0]), ('even_a_ln_b', dlb[0]), ('even_a_ws', dws),
                           ('even_a_bs', dbst.T), ('even_b_sinks', dsink[0])):
                per_layer[n][j] = val
        else:
            dh, dcw, dcb, dwa, dba, dwx, dbx, dlam = _lru_bwd(
                h, dmix, w['odd_conv_w'][j], w['odd_conv_b'][j], w['odd_w_a'][j], w['odd_b_a'][j], w['odd_w_x'][j],
                w['odd_b_x'][j], w['odd_lam'][j])
            dh, dwp, dsc = _pool_bwd(h, dmix, dh, w['odd_w_pool'][j], w['odd_d_scale'][j])
            for n, val in (('odd_conv_w', dcw), ('odd_conv_b', dcb[0]), ('odd_w_a', dwa), ('odd_b_a', dba[0]),
                           ('odd_w_x', dwx), ('odd_b_x', dbx[0]), ('odd_lam', dlam[0]), ('odd_w_pool', dwp),
                           ('odd_d_scale', dsc[0])):
                per_layer[n][j] = val
        dout = _mm_dx(dh, w_in, dz, W_IN_TRANSPOSED[kind], "mm_dx_" + kind)
        dw_in = _mm_dw_in(xin_b, dh, W_IN_TRANSPOSED[kind], "mm_dw_in_" + kind)
        behind = projection_grads(layer, dw_in, dw_out)
    return loss, dout, {n: jnp.stack(v) for n, v in per_layer.items()}


ANY = pl.BlockSpec(memory_space=pl.ANY)


def _where_am_i():
    return lax.axis_index("x"), lax.axis_index("y"), lax.axis_index("c")


def _other_chips(x, y):
    return [(1 - x, y), (x, 1 - y), (1 - x, 1 - y)]


PACKED_ROWS = 16


def _half_rows(rows, which):
    half = rows // 2
    assert half % PACKED_ROWS == 0, rows
    return pl.ds(pl.multiple_of(which * half, PACKED_ROWS), half)


def _sibling_swap(bufs, name, *, other_half_of_axis1=False, after=()):
    n, na = len(bufs), len(after)
    out_shapes = [(b.shape[0], b.shape[1] // 2, b.shape[2]) if other_half_of_axis1 else b.shape for b in bufs]

    def body(*refs):
        x_refs, out_refs, (send_sems, recv_sems) = refs[:n], refs[n + na:2 * n + na], refs[2 * n + na:]
        x, y, c = _where_am_i()
        copies = []
        for p in range(n):
            src = x_refs[p].at[:, _half_rows(bufs[p].shape[1], 1 - c), :] if other_half_of_axis1 else x_refs[p]
            copies.append(pltpu.make_async_remote_copy(
                src_ref=src, dst_ref=out_refs[p], send_sem=send_sems.at[p], recv_sem=recv_sems.at[p],
                device_id=(x, y, 1 - c), device_id_type=MESH))
            copies[-1].start()
        for cp in copies:
            cp.wait()

    return pl.pallas_call(
        body, name=name, in_specs=[ANY] * (n + na), out_specs=[ANY] * n,
        out_shape=[jax.ShapeDtypeStruct(s, b.dtype) for s, b in zip(out_shapes, bufs)],
        scratch_shapes=[pltpu.SemaphoreType.DMA((n,)), pltpu.SemaphoreType.DMA((n,))],
    )(*bufs, *after)


HBM = pl.BlockSpec(memory_space=pltpu.HBM)
SEM = pl.BlockSpec(memory_space=pltpu.SEMAPHORE)
IN_FLIGHT = pltpu.CompilerParams(has_side_effects=pltpu.SideEffectType.DATAFLOW_SIDE_EFFECTING)


def _ici_copy(kind, src_ref, land_ref, send_sems, recv_sems, k, peer, x, y, c):
    px, py = peer
    me = 2 * x + y
    if kind == 'gather':
        half = _half_rows(src_ref.shape[0], c)
        src, dst = src_ref.at[half, :], land_ref.at[me, half, :]
    else:
        src, dst = src_ref.at[2 * px + py], land_ref.at[me]
    return pltpu.make_async_remote_copy(src_ref=src, dst_ref=dst, send_sem=send_sems.at[k], recv_sem=recv_sems.at[k],
                                        device_id=(px, py, c), device_id_type=MESH)


def _ici_start(kind, groups, name):
    flat = [a for g in groups for a in g]
    n, ng = len(flat), len(groups)
    land_shapes = [((N_CHIPS,) + a.shape) if kind == 'gather' else a.shape for a in flat]

    def body(*refs):
        srcs, lands, sems, token = refs[:n], refs[n:2 * n], refs[2 * n:2 * n + 2 * ng], refs[-1]
        x, y, c = _where_am_i()
        p = 0
        for gi, g in enumerate(groups):
            for q in range(len(g)):
                for k, peer in enumerate(_other_chips(x, y)):
                    _ici_copy(kind, srcs[p], lands[p], sems[2 * gi], sems[2 * gi + 1], 3 * q + k, peer, x, y, c).start()
                p += 1
        token[...] = jnp.zeros_like(token)

    sem_shapes = [pltpu.SemaphoreType.DMA((3 * len(g),)) for g in groups for _ in range(2)]
    outs = pl.pallas_call(
        body, name=name, in_specs=[HBM] * (2 * n),
        out_specs=[SEM] * (2 * ng) + [HBM] * (2 * n) + [pl.BlockSpec(memory_space=pltpu.VMEM)],
        out_shape=sem_shapes + [pltpu.HBM(a.shape, a.dtype) for a in flat]
        + [pltpu.HBM(s, a.dtype) for s, a in zip(land_shapes, flat)] + [jax.ShapeDtypeStruct((8, LANES), F32)],
        input_output_aliases={i: 2 * ng + i for i in range(2 * n)}, compiler_params=IN_FLIGHT,
    )(*[pltpu.with_memory_space_constraint(a, pltpu.HBM) for a in flat],
      *[pltpu.with_memory_space_constraint(lax.empty(s, a.dtype), pltpu.HBM) for s, a in zip(land_shapes, flat)])
    sems, srcs, lands, token = outs[:2 * ng], outs[2 * ng:2 * ng + n], outs[2 * ng + n:2 * ng + 2 * n], outs[-1]
    started, p = [], 0
    for gi, g in enumerate(groups):
        started.append((sems[2 * gi], sems[2 * gi + 1], srcs[p:p + len(g)], lands[p:p + len(g)]))
        p += len(g)
    return started, token


def _ici_wait(kind, started, after, name):
    send_sems, recv_sems, srcs, lands = started
    n = len(srcs)
    after = list(after)

    def body(*refs):
        src_refs, land_refs, send_ref, recv_ref = refs[:n], refs[n:2 * n], refs[2 * n], refs[2 * n + 1]
        x, y, c = _where_am_i()
        for q in range(n):
            for k, peer in enumerate(_other_chips(x, y)):
                cp = _ici_copy(kind, src_refs[q], land_refs[q], send_ref, recv_ref, 3 * q + k, peer, x, y, c)
                cp.wait_send()
                cp.wait_recv()

    outs = pl.pallas_call(
        body, name=name, in_specs=[HBM] * (2 * n) + [SEM, SEM] + [ANY] * len(after), out_specs=[HBM] * (2 * n),
        out_shape=[pltpu.HBM(a.shape, a.dtype) for a in list(srcs) + list(lands)],
        input_output_aliases={i: i for i in range(2 * n)}, compiler_params=IN_FLIGHT,
    )(*srcs, *lands, send_sems, recv_sems, *after)
    return outs[:n], outs[n:]


def _forward_to_sibling(lands, name):
    n = len(lands)

    def body(*refs):
        out_refs, (send_sems, recv_sems) = refs[n:2 * n], refs[2 * n:]
        x, y, c = _where_am_i()

        def copy(q, k, peer, half_of):
            rows = out_refs[q].at[2 * peer[0] + peer[1], _half_rows(lands[q].shape[1], half_of), :]
            return pltpu.make_async_remote_copy(
                src_ref=rows, dst_ref=rows, send_sem=send_sems.at[3 * q + k], recv_sem=recv_sems.at[3 * q + k],
                device_id=(x, y, 1 - c), device_id_type=MESH)

        sends = [copy(q, k, peer, c) for q in range(n) for k, peer in enumerate(_other_chips(x, y))]
        for cp in sends:
            cp.start()
        for q in range(n):
            for k, peer in enumerate(_other_chips(x, y)):
                copy(q, k, peer, 1 - c).wait_recv()
        for cp in sends:
            cp.wait_send()

    return pl.pallas_call(
        body, name=name, in_specs=[ANY] * n, out_specs=[ANY] * n,
        out_shape=[jax.ShapeDtypeStruct(a.shape, a.dtype) for a in lands],
        input_output_aliases={i: i for i in range(n)},
        scratch_shapes=[pltpu.SemaphoreType.DMA((3 * n,)), pltpu.SemaphoreType.DMA((3 * n,))],
    )(*lands)


def _with_own_slot(lands, own, chip):
    out = []
    for land, mine in zip(lands, own):
        mine = lax.dynamic_slice_in_dim(mine, chip, 1, axis=0) if mine.ndim == land.ndim else mine[None]
        out.append(lax.dynamic_update_slice(land, mine, (chip,) + (0,) * (land.ndim - 1)))
    return out


def _row_tile(rows, row_bytes, n_bufs, budget=24 * 1024 * 1024):
    best = None
    for t in range(PACKED_ROWS, rows + 1, PACKED_ROWS):
        if rows % t == 0 and 2 * n_bufs * t * row_bytes <= budget:
            best = t
    assert best is not None, (rows, row_bytes)
    return best


def _add_halves(full, recv, core, out_dtype, name):
    n, h, cols = recv.shape
    tr = _row_tile(h, cols * 4, 3)
    nt = h // tr

    def body(core_ref, a_ref, b_ref, o_ref):
        o_ref[...] = (a_ref[...].astype(F32) + b_ref[...].astype(F32)).astype(out_dtype)

    blk = (None, tr, cols)
    return pl.pallas_call(
        body, name=name,
        grid_spec=pltpu.PrefetchScalarGridSpec(
            num_scalar_prefetch=1, grid=(n, nt),
            in_specs=[pl.BlockSpec(blk, lambda j, i, core_ref: (j, core_ref[0] * nt + i, 0)),
                      pl.BlockSpec(blk, lambda j, i, core_ref: (j, i, 0))],
            out_specs=pl.BlockSpec(blk, lambda j, i, core_ref: (j, i, 0))),
        out_shape=jax.ShapeDtypeStruct((n, h, cols), out_dtype),
        compiler_params=_params(("parallel", "parallel")),
    )(core.reshape(1), full, recv)


def _add2(a, b, name):
    r, cols = a.shape
    tr = _row_tile(r, cols * 4, 3)

    def body(a_ref, b_ref, o_ref):
        o_ref[...] = a_ref[...] + b_ref[...]

    blk = pl.BlockSpec((tr, cols), lambda i: (i, 0))
    return pl.pallas_call(body, name=name, grid=(r // tr,), in_specs=[blk, blk], out_specs=blk,
                          out_shape=jax.ShapeDtypeStruct(a.shape, a.dtype), compiler_params=_params(("parallel",)))(a, b)


def _sum_slots(parts, name):
    n, h, cols = parts.shape
    tr = _row_tile(h, cols * 4, n + 1)

    def body(p_ref, o_ref):
        acc = p_ref[0].astype(F32)
        for k in range(1, n):
            acc = acc + p_ref[k].astype(F32)
        o_ref[...] = acc

    return pl.pallas_call(
        body, name=name, grid=(h // tr,), in_specs=[pl.BlockSpec((n, tr, cols), lambda i: (0, i, 0))],
        out_specs=pl.BlockSpec((tr, cols), lambda i: (i, 0)), out_shape=jax.ShapeDtypeStruct((h, cols), F32),
        compiler_params=_params(("parallel",)),
    )(parts)


def _adamw(w, g, m, v, name):
    shape = w.shape
    cols = shape[-1]
    rows = math.prod(shape[:-1])
    tr = rows
    if rows % 8 == 0:
        tr = 8
        for t in range(8, rows + 1, 8):
            if rows % t == 0 and t * cols * 4 <= 1536 * 1024:
                tr = t
    c1 = 1.0 - ADAM_B1 ** ADAM_STEP
    c2 = 1.0 - ADAM_B2 ** ADAM_STEP

    def body(w_ref, g_ref, m_ref, v_ref, d_ref, nm_ref, nv_ref):
        gg = g_ref[...]
        nm = ADAM_B1 * m_ref[...] + (1.0 - ADAM_B1) * gg
        nv = ADAM_B2 * v_ref[...] + (1.0 - ADAM_B2) * (gg * gg)
        d_ref[...] = -ADAM_LR * ((nm / c1) / (jnp.sqrt(nv / c2) + ADAM_EPS) + ADAM_WD * w_ref[...])
        nm_ref[...] = nm
        nv_ref[...] = nv

    blk = pl.BlockSpec((tr, cols), lambda i: (i, 0))
    flat = jax.ShapeDtypeStruct((rows, cols), F32)
    outs = pl.pallas_call(
        body, name=name, grid=(rows // tr,), in_specs=[blk] * 4, out_specs=[blk] * 3, out_shape=[flat] * 3,
        compiler_params=_params(("parallel",)),
    )(*[a.reshape(rows, cols) for a in (w, g, m, v)])
    return [o.reshape(shape) for o in outs]


def _adamw_update(w, g, m, v):
    c1 = 1.0 - ADAM_B1 ** ADAM_STEP
    c2 = 1.0 - ADAM_B2 ** ADAM_STEP
    nm = ADAM_B1 * m + (1.0 - ADAM_B1) * g
    nv = ADAM_B2 * v + (1.0 - ADAM_B2) * (g * g)
    return -ADAM_LR * ((nm / c1) / (jnp.sqrt(nv / c2) + ADAM_EPS) + ADAM_WD * w), nm, nv


def _adamw_projection(w, m, v, mine, theirs, core, name):
    _, r, cols = w.shape
    h = r // 2
    tr = _row_tile(h, cols * 4, 12, budget=40 * 1024 * 1024)
    nt = h // tr

    def body(core_ref, w_ref, m_ref, v_ref, a0, b0, a1, b1, g_ref, d_ref, nm_ref, nv_ref):
        layer, half = pl.program_id(0), pl.program_id(1)
        own = half == core_ref[0]
        g = jnp.where(layer == 0, jnp.where(own, a0[...], b0[...]), jnp.where(own, a1[...], b1[...]))
        g_ref[...] = g
        d_ref[...], nm_ref[...], nv_ref[...] = _adamw_update(w_ref[...], g, m_ref[...], v_ref[...])

    def piece(layer, own):
        def index(l, hh, i, core_ref):
            used = (l == layer) & ((hh == core_ref[0]) == own)
            return (jnp.where(used, i, 0), 0)
        return pl.BlockSpec((tr, cols), index)

    whole = pl.BlockSpec((None, tr, cols), lambda l, hh, i, core_ref: (l, hh * nt + i, 0))
    return pl.pallas_call(
        body, name=name,
        grid_spec=pltpu.PrefetchScalarGridSpec(
            num_scalar_prefetch=1, grid=(2, 2, nt),
            in_specs=[whole] * 3 + [piece(0, True), piece(0, False), piece(1, True), piece(1, False)],
            out_specs=[whole] * 4),
        out_shape=[jax.ShapeDtypeStruct(w.shape, F32)] * 4,
        compiler_params=_params(("arbitrary", "arbitrary", "arbitrary")),
    )(core.reshape(1), w, m, v, mine[0], theirs[0], mine[1], theirs[1])


def _chip_shape(name):
    shape = list(FULL_SHAPES[name])
    axis = BIG_SHARD_AXIS.get(name, SMALL_SHARD_AXIS.get(name))
    if axis is not None:
        shape[axis] //= N_CHIPS
    return tuple(shape)


def _round_up(n, k):
    return -(-n // k) * k


SHARDED_SMALL_WORDS = sum(math.prod(_chip_shape(n)) for n in SMALL_SHARDED)
SHARDED_SMALL_ROWS = _round_up(-(-SHARDED_SMALL_WORDS // PACK_COLS), 32)
SMALL_MATRICES = ['even_a_ws', 'odd_w_a', 'odd_w_x', 'odd_w_pool']
SMALL_VECTORS = [n for n in SMALL_NAMES if n not in SMALL_MATRICES]
VECTOR_ROWS = _round_up(-(-(sum(math.prod(FULL_SHAPES[n]) for n in SMALL_VECTORS) + 1) // PACK_COLS), 32)


def _pack_rows(flat, rows):
    return jnp.pad(flat, (0, rows * PACK_COLS - flat.shape[0])).reshape(rows, PACK_COLS)


def _pack_small_sharded(shards):
    return _pack_rows(jnp.concatenate([shards[n].reshape(-1) for n in SMALL_SHARDED]), SHARDED_SMALL_ROWS)


def _unpack_small_sharded(gathered_small):
    out = {}
    words = gathered_small.reshape(N_CHIPS, -1)
    w0 = 0
    for n in SMALL_SHARDED:
        size = math.prod(_chip_shape(n))
        blocks = words[:, w0:w0 + size].reshape((N_CHIPS,) + _chip_shape(n))
        out[n] = jnp.concatenate([blocks[j] for j in range(N_CHIPS)], axis=SMALL_SHARD_AXIS[n])
        w0 += size
    return out


def _small_grad_parts(grads, loss):
    words = [grads[n].reshape(-1) for n in SMALL_VECTORS] + [loss.reshape(-1)]
    return ([_pack_rows(jnp.concatenate(words), VECTOR_ROWS)]
            + [grads[n].reshape(-1, FULL_SHAPES[n][-1]) for n in SMALL_MATRICES])


def _own_block(name, full, chip):
    if name not in SMALL_SHARD_AXIS:
        return full
    axis = SMALL_SHARD_AXIS[name]
    width = FULL_SHAPES[name][axis] // N_CHIPS
    return lax.dynamic_slice_in_dim(full, chip * width, width, axis)


def _from_small_grad_parts(totals, chip):
    flat = totals[0].reshape(-1)
    out = {}
    w0 = 0
    for n in SMALL_VECTORS:
        size = math.prod(FULL_SHAPES[n])
        out[n] = _own_block(n, flat[w0:w0 + size].reshape(FULL_SHAPES[n]), chip)
        w0 += size
    for n, total in zip(SMALL_MATRICES, totals[1:]):
        out[n] = _own_block(n, total.reshape(FULL_SHAPES[n]), chip)
    return out, flat[w0]


def kernel(x, positions, even_w_in, even_a_ln_g, even_a_ln_b, even_a_ws, even_a_bs, even_b_sinks, even_w_out, even_ln_g, even_ln_b, odd_w_in, odd_conv_w, odd_conv_b, odd_w_a, odd_b_a, odd_w_x, odd_b_x, odd_lam, odd_w_pool, odd_d_scale, odd_w_out, odd_ln_g, odd_ln_b, loss_target, m_even_w_in, m_even_a_ln_g, m_even_a_ln_b, m_even_a_ws, m_even_a_bs, m_even_b_sinks, m_even_w_out, m_even_ln_g, m_even_ln_b, m_odd_w_in, m_odd_conv_w, m_odd_conv_b, m_odd_w_a, m_odd_b_a, m_odd_w_x, m_odd_b_x, m_odd_lam, m_odd_w_pool, m_odd_d_scale, m_odd_w_out, m_odd_ln_g, m_odd_ln_b, v_even_w_in, v_even_a_ln_g, v_even_a_ln_b, v_even_a_ws, v_even_a_bs, v_even_b_sinks, v_even_w_out, v_even_ln_g, v_even_ln_b, v_odd_w_in, v_odd_conv_w, v_odd_conv_b, v_odd_w_a, v_odd_b_a, v_odd_w_x, v_odd_b_x, v_odd_lam, v_odd_w_pool, v_odd_d_scale, v_odd_w_out, v_odd_ln_g, v_odd_ln_b):
    local = dict(zip(WEIGHT_NAMES, (even_w_in, even_a_ln_g, even_a_ln_b, even_a_ws, even_a_bs, even_b_sinks, even_w_out, even_ln_g, even_ln_b, odd_w_in, odd_conv_w, odd_conv_b, odd_w_a, odd_b_a, odd_w_x, odd_b_x, odd_lam, odd_w_pool, odd_d_scale, odd_w_out, odd_ln_g, odd_ln_b)))
    mom = dict(zip(WEIGHT_NAMES, (m_even_w_in, m_even_a_ln_g, m_even_a_ln_b, m_even_a_ws, m_even_a_bs, m_even_b_sinks, m_even_w_out, m_even_ln_g, m_even_ln_b, m_odd_w_in, m_odd_conv_w, m_odd_conv_b, m_odd_w_a, m_odd_b_a, m_odd_w_x, m_odd_b_x, m_odd_lam, m_odd_w_pool, m_odd_d_scale, m_odd_w_out, m_odd_ln_g, m_odd_ln_b)))
    vel = dict(zip(WEIGHT_NAMES, (v_even_w_in, v_even_a_ln_g, v_even_a_ln_b, v_even_a_ws, v_even_a_bs, v_even_b_sinks, v_even_w_out, v_even_ln_g, v_even_ln_b, v_odd_w_in, v_odd_conv_w, v_odd_conv_b, v_odd_w_a, v_odd_b_a, v_odd_w_x, v_odd_b_x, v_odd_lam, v_odd_w_pool, v_odd_d_scale, v_odd_w_out, v_odd_ln_g, v_odd_ln_b)))
    mx, my, core = _where_am_i()
    chip = 2 * mx + my

    stored = lambda n, a: jnp.swapaxes(a, 1, 2) if n == 'even_w_in' and W_IN_TRANSPOSED['even'] else a
    names_of = lambda layer: [('even' if layer % 2 == 0 else 'odd') + s for s in ('_w_in', '_w_out')]

    shards = [[stored(n, local[n])[layer // 2].astype(MXU_DTYPE) for n in names_of(layer)] for layer in range(DEPTH)]
    shards[0].insert(0, _pack_small_sharded(local))
    gathers, gather_token = _ici_start('gather', shards, "gather_start")

    def gathered(layer, after):
        own, lands = _ici_wait('gather', gathers[layer], [after], "gather_wait_%d" % layer)
        return _with_own_slot(_forward_to_sibling(lands, "gather_forward_%d" % layer), own, chip)

    first = gathered(0, gather_token)
    weights = dict(local)
    weights.update(_unpack_small_sharded(first[0]))
    projections_of = lambda layer, after: first[1:] if layer == 0 else gathered(layer, after)

    in_flight, mine = [], {}

    def land(*after):
        layer, scatter = in_flight.pop()
        parts, lands = _ici_wait('scatter', scatter, after, "scatter_wait_%d" % layer)
        for n, arrived in zip(names_of(layer), _with_own_slot(lands, parts, chip)):
            mine[n, layer // 2] = _sum_slots(arrived, "big_sum_%s_%d" % (n, layer // 2))

    def projection_grads(layer, dw_in, dw_out):
        if in_flight:
            land(dw_in)
        partial = [dw_in, dw_out]
        from_sibling = _sibling_swap(partial, "big_presum_swap_%d" % layer, other_half_of_axis1=True)
        chip_part = [_add_halves(g, r, core, MXU_DTYPE, "big_presum_add_%s_%d" % (n, layer // 2))
                     for g, r, n in zip(partial, from_sibling, names_of(layer))]
        (scatter,), started[layer] = _ici_start('scatter', [chip_part], "scatter_start_%d" % layer)
        in_flight.append((layer, scatter))
        return started[layer]

    started = {}
    loss, grad_x, grads = _local_step(x[0], positions[0], loss_target[0], weights, projections_of, projection_grads)

    small = _small_grad_parts(grads, loss)
    chip_sums = [_add2(a, b, "small_presum_add_%d" % i)
                 for i, (a, b) in enumerate(zip(small, _sibling_swap(small, "small_swap", after=[started[0]])))]
    (small_gather,), small_token = _ici_start('gather', [chip_sums], "small_gather_start")

    reduced, deltas, new_m, new_v = {}, {}, {}, {}

    def adamw_projections(kind, after):
        names = [kind + '_w_in', kind + '_w_out']
        pieces = [(n, l) for n in names for l in range(2)]
        theirs = dict(zip(pieces, _sibling_swap([mine[p] for p in pieces], "big_join_" + kind, after=after)))
        done = []
        for n in names:
            outs = _adamw_projection(stored(n, local[n]), stored(n, mom[n]), stored(n, vel[n]),
                                     [mine[n, l] for l in range(2)], [theirs[n, l] for l in range(2)], core,
                                     "adamw_" + n)
            reduced[n], deltas[n], new_m[n], new_v[n] = [stored(n, o) for o in outs]
            done.append(outs[-1])
        return done

    done = adamw_projections('odd', [small_token])
    land(grad_x, *done)
    done = adamw_projections('even', [])

    own, lands = _ici_wait('gather', small_gather, done, "small_gather_wait")
    of_chips = _with_own_slot(_forward_to_sibling(lands, "small_gather_forward"), own, chip)
    small_totals, total_loss = _from_small_grad_parts(
        [_sum_slots(g, "small_sum_%d" % i) for i, g in enumerate(of_chips)], chip)
    reduced.update(small_totals)
    for n in SMALL_NAMES:
        deltas[n], new_m[n], new_v[n] = _adamw(local[n], reduced[n], mom[n], vel[n], "adamw_" + n)

    return (total_loss, grad_x[None], *[reduced[n] for n in WEIGHT_NAMES], *[deltas[n] for n in WEIGHT_NAMES],
            *[new_m[n] for n in WEIGHT_NAMES], *[new_v[n] for n in WEIGHT_NAMES])
```

```python
import math

import jax
import jax.numpy as jnp
from jax import lax
from jax.experimental import pallas as pl
from jax.experimental.pallas import tpu as pltpu

F32 = jnp.float32
BF16 = jnp.bfloat16
MXU_DTYPE = jnp.bfloat16

D_MODEL = 2048
DEPTH = 4
A_WIDTH = 1024
A_GROUPS = 8
CHUNK = 128
B_HEAD_DIM = 64
B_Q_HEADS = 16
B_KV_HEADS = 2
WINDOW = 128
ROT_DIM = 16
ROPE_THETA = 500000.0
C_WIDTH = 1024
C_HEADS = 8
CONV_WIDTH = 4
LRU_C = 8.0
D_WIDTH = 1024
D_GROUPS = 4
D_GROUP_DIM = 256
EVEN_IN = 5376
ODD_IN = 4096
DN_ALPHA = (2 * DEPTH) ** 0.25
LN_EPS = 1e-5
ATTN_SCALE = B_HEAD_DIM ** -0.5
NEG_BIG = -1e30

ADAM_LR = 0.001
ADAM_B1 = 0.9
ADAM_B2 = 0.999
ADAM_EPS = 1e-08
ADAM_WD = 0.01
ADAM_STEP = 10

LANES = 128
VMEM_LIMIT = 56 * 1024 * 1024
N_CHIPS = 4
MESH = pl.DeviceIdType.MESH

WEIGHT_NAMES = ['even_w_in', 'even_a_ln_g', 'even_a_ln_b', 'even_a_ws', 'even_a_bs', 'even_b_sinks', 'even_w_out',
                'even_ln_g', 'even_ln_b', 'odd_w_in', 'odd_conv_w', 'odd_conv_b', 'odd_w_a', 'odd_b_a', 'odd_w_x',
                'odd_b_x', 'odd_lam', 'odd_w_pool', 'odd_d_scale', 'odd_w_out', 'odd_ln_g', 'odd_ln_b']
BIG_SHARD_AXIS = {'even_w_in': 2, 'even_w_out': 1, 'odd_w_in': 2, 'odd_w_out': 1}
SMALL_SHARD_AXIS = {'odd_conv_w': 2, 'odd_conv_b': 1, 'odd_b_a': 1, 'odd_b_x': 1, 'odd_lam': 1, 'odd_w_pool': 2,
                    'odd_d_scale': 1, 'odd_ln_g': 1, 'odd_ln_b': 1}
FULL_SHAPES = {
    'even_w_in': (2, 2048, 5376), 'even_a_ln_g': (2, 1024), 'even_a_ln_b': (2, 1024), 'even_a_ws': (2, 8, 128, 128),
    'even_a_bs': (2, 8, 128), 'even_b_sinks': (2, 16), 'even_w_out': (2, 2048, 2048), 'even_ln_g': (2, 2048),
    'even_ln_b': (2, 2048), 'odd_w_in': (2, 2048, 4096), 'odd_conv_w': (2, 4, 1024), 'odd_conv_b': (2, 1024),
    'odd_w_a': (2, 8, 128, 128), 'odd_b_a': (2, 1024), 'odd_w_x': (2, 8, 128, 128), 'odd_b_x': (2, 1024),
    'odd_lam': (2, 1024), 'odd_w_pool': (2, 4, 256, 256), 'odd_d_scale': (2, 1024), 'odd_w_out': (2, 2048, 2048),
    'odd_ln_g': (2, 2048), 'odd_ln_b': (2, 2048)}
BIG_NAMES = ['even_w_in', 'even_w_out', 'odd_w_in', 'odd_w_out']
SMALL_SHARDED = ['odd_conv_w', 'odd_conv_b', 'odd_b_a', 'odd_b_x', 'odd_lam', 'odd_d_scale', 'odd_ln_g', 'odd_ln_b',
                 'odd_w_pool']
SMALL_NAMES = [n for n in WEIGHT_NAMES if n not in BIG_NAMES]
W_IN_TRANSPOSED = {'even': True, 'odd': False}
PACK_COLS = 1024


def _params(sem):
    return pltpu.CompilerParams(dimension_semantics=sem, vmem_limit_bytes=VMEM_LIMIT)


def _dot(a, b):
    return lax.dot_general(a, b, (((1,), (0,)), ((), ())), preferred_element_type=F32)


def _dot_nt(a, b):
    return lax.dot_general(a, b, (((1,), (1,)), ((), ())), preferred_element_type=F32)


def _dot_tn(a, b):
    return lax.dot_general(a, b, (((0,), (0,)), ((), ())), preferred_element_type=F32)


def _sigmoid(x):
    return 1.0 / (1.0 + jnp.exp(-x))


def _silu_and_grad(x):
    s = _sigmoid(x)
    return x * s, s * (1.0 + x * (1.0 - s))


def _mm_call(body, name, grid, in_specs, out_spec, out_shape, args):
    return pl.pallas_call(body, name=name, grid=grid, in_specs=in_specs, out_specs=out_spec, out_shape=out_shape,
                          compiler_params=_params(("parallel",) * len(grid)))(*args)


def _resident(block, index_map):
    return pl.BlockSpec(block, index_map, pipeline_mode=pl.Buffered(1))


def _rows_per_step(x, rows):
    return rows if x.dtype.itemsize == 2 else min(1024, rows)


def _mm_in(x, wg, transposed, name):
    s, k = x.shape
    nc = wg.shape[1] if transposed else wg.shape[2]
    tm = _rows_per_step(x, s)

    def body(x_ref, w_ref, o_ref):
        o_ref[...] = (_dot_nt if transposed else _dot)(x_ref[...].astype(MXU_DTYPE), w_ref[...])

    x_spec = _resident((tm, k), lambda i, j: (0, 0)) if tm == s else pl.BlockSpec((tm, k), lambda i, j: (i, 0))
    return _mm_call(body, name, (s // tm, N_CHIPS),
                    [x_spec, pl.BlockSpec((None,) + wg.shape[1:], lambda i, j: (j, 0, 0))],
                    pl.BlockSpec((None, tm, nc), lambda i, j: (j, i, 0)),
                    jax.ShapeDtypeStruct((N_CHIPS, s, nc), F32), (x, wg))


def _mm_dx(dh, wg, dz, transposed, name):
    s, d = dz.shape
    nc = dh.shape[2]
    tm, tn = min(1024, s), 512

    def body(a_ref, w_ref, r_ref, o_ref):
        acc = DN_ALPHA * r_ref[...]
        for j in range(N_CHIPS):
            acc = acc + (_dot if transposed else _dot_nt)(a_ref[j], w_ref[j])
        o_ref[...] = acc

    w_spec = (pl.BlockSpec((N_CHIPS, nc, tn), lambda i, j: (0, 0, j)) if transposed
              else pl.BlockSpec((N_CHIPS, tn, nc), lambda i, j: (0, j, 0)))
    return _mm_call(body, name, (s // tm, d // tn),
                    [pl.BlockSpec((N_CHIPS, tm, nc), lambda i, j: (0, i, 0)), w_spec,
                     pl.BlockSpec((tm, tn), lambda i, j: (i, j))],
                    pl.BlockSpec((tm, tn), lambda i, j: (i, j)), jax.ShapeDtypeStruct((s, d), F32), (dh, wg, dz))


def _mm_dw_in(x, dh, transposed, name):
    s, d = x.shape
    nc = dh.shape[2]
    tm = _rows_per_step(x, d)

    def body(x_ref, b_ref, o_ref):
        xb = x_ref[...].astype(MXU_DTYPE)
        acc = _dot_tn(b_ref[...], xb) if transposed else _dot_tn(xb, b_ref[...])
        o_ref[...] = acc.astype(o_ref.dtype)

    if transposed:
        out_spec, shape = pl.BlockSpec((None, nc, tm), lambda i, j: (j, 0, i)), (N_CHIPS, nc, d)
    else:
        out_spec, shape = pl.BlockSpec((None, tm, nc), lambda i, j: (j, i, 0)), (N_CHIPS, d, nc)
    x_spec = _resident((s, tm), lambda i, j: (0, 0)) if tm == d else pl.BlockSpec((s, tm), lambda i, j: (0, i))
    return _mm_call(body, name, (d // tm, N_CHIPS),
                    [x_spec, pl.BlockSpec((None, s, nc), lambda i, j: (j, 0, 0))],
                    out_spec, jax.ShapeDtypeStruct(shape, MXU_DTYPE), (x, dh))


def _mm_out_norm(mix, wg, x, g, b, name):
    s, k = mix.shape
    _, kc, d = wg.shape
    tm = min(512, s)

    def body(a_ref, w_ref, x_ref, g_ref, b_ref, z_ref, o_ref, ob_ref):
        z = DN_ALPHA * x_ref[...] + _dot(a_ref[...], w_ref[...].reshape(N_CHIPS * kc, d))
        mu = jnp.mean(z, axis=1, keepdims=True)
        zc = z - mu
        var = jnp.mean(zc * zc, axis=1, keepdims=True)
        o = zc * lax.rsqrt(var + LN_EPS) * g_ref[...] + b_ref[...]
        z_ref[...] = z
        o_ref[...] = o
        ob_ref[...] = o.astype(ob_ref.dtype)

    row = pl.BlockSpec((tm, d), lambda i: (i, 0))
    vec = pl.BlockSpec((1, d), lambda i: (0, 0))
    return pl.pallas_call(
        body, name=name, grid=(s // tm,),
        in_specs=[pl.BlockSpec((tm, k), lambda i: (i, 0)), _resident((N_CHIPS, kc, d), lambda i: (0, 0, 0)), row, vec,
                  vec],
        out_specs=[row, row, row],
        out_shape=[jax.ShapeDtypeStruct((s, d), F32), jax.ShapeDtypeStruct((s, d), F32),
                   jax.ShapeDtypeStruct((s, d), MXU_DTYPE)],
        compiler_params=_params(("parallel",)),
    )(mix, wg, x, g.reshape(1, d), b.reshape(1, d))


def _mm_dmix(dz, wg, after, name):
    s, d = dz.shape
    kc = wg.shape[1]

    def body(a_ref, w_ref, after_ref, o_ref):
        o_ref[...] = _dot_nt(a_ref[...], w_ref[...])

    return _mm_call(body, name, (N_CHIPS,),
                    [_resident((s, d), lambda j: (0, 0)), pl.BlockSpec((None, kc, d), lambda j: (j, 0, 0)), ANY],
                    pl.BlockSpec((s, kc), lambda j: (0, j)), jax.ShapeDtypeStruct((s, N_CHIPS * kc), F32),
                    (dz, wg, after))


def _mm_dw_out(mix, dz, name):
    s, k = mix.shape
    d = dz.shape[1]
    kc = k // N_CHIPS

    def body(a_ref, b_ref, o_ref):
        o_ref[...] = _dot_tn(a_ref[...], b_ref[...]).astype(o_ref.dtype)

    return _mm_call(body, name, (N_CHIPS,),
                    [pl.BlockSpec((s, kc), lambda i: (0, i)), _resident((s, d), lambda i: (0, 0))],
                    pl.BlockSpec((None, kc, d), lambda i: (i, 0, 0)),
                    jax.ShapeDtypeStruct((N_CHIPS, kc, d), MXU_DTYPE), (mix, dz))


def _ln_bwd(dout, z, g, after):
    s, d = z.shape
    tr = min(256, s)

    def body(do_ref, z_ref, g_ref, after_ref, dz_ref, dzb_ref, dg_ref, db_ref):
        @pl.when(pl.program_id(0) == 0)
        def _():
            dg_ref[...] = jnp.zeros_like(dg_ref)
            db_ref[...] = jnp.zeros_like(db_ref)

        zz = z_ref[...]
        do = do_ref[...]
        mu = jnp.mean(zz, axis=1, keepdims=True)
        zc = zz - mu
        var = jnp.mean(zc * zc, axis=1, keepdims=True)
        rstd = lax.rsqrt(var + LN_EPS)
        zh = zc * rstd
        dg_ref[...] += jnp.sum(do * zh, axis=0, keepdims=True)
        db_ref[...] += jnp.sum(do, axis=0, keepdims=True)
        dzh = do * g_ref[...]
        dz = rstd * (dzh - jnp.mean(dzh, axis=1, keepdims=True) - zh * jnp.mean(dzh * zh, axis=1, keepdims=True))
        dz_ref[...] = dz
        dzb_ref[...] = dz.astype(MXU_DTYPE)

    row = pl.BlockSpec((tr, d), lambda i: (i, 0))
    vec = pl.BlockSpec((1, d), lambda i: (0, 0))
    return pl.pallas_call(
        body, name="ln_bwd", grid=(s // tr,), in_specs=[row, row, vec, ANY], out_specs=[row, row, vec, vec],
        out_shape=[jax.ShapeDtypeStruct((s, d), F32), jax.ShapeDtypeStruct((s, d), MXU_DTYPE),
                   jax.ShapeDtypeStruct((1, d), F32), jax.ShapeDtypeStruct((1, d), F32)],
        compiler_params=_params(("arbitrary",)),
    )(dout, z, g.reshape(1, d), after)


def _loss_and_grad(out, tgt):
    s, d = out.shape
    tr = min(256, s)

    def body(o_ref, t_ref, l_ref, do_ref):
        @pl.when(pl.program_id(0) == 0)
        def _():
            l_ref[...] = jnp.zeros_like(l_ref)

        e = o_ref[...] - t_ref[...]
        do_ref[...] = e * (1.0 / d)
        rows = jnp.sum(e * e, axis=1, keepdims=True)
        l_ref[...] += (0.5 / d) * jnp.sum(rows, axis=0, keepdims=True)

    row = pl.BlockSpec((tr, d), lambda i: (i, 0))
    return pl.pallas_call(
        body, name="loss", grid=(s // tr,), in_specs=[row, row],
        out_specs=[pl.BlockSpec((1, 1), lambda i: (0, 0)), row],
        out_shape=[jax.ShapeDtypeStruct((1, 1), F32), jax.ShapeDtypeStruct((s, d), F32)],
        compiler_params=_params(("arbitrary",)),
    )(out, tgt)


Q0, K0, V0, BG0 = 3072, 4096, 4224, 4352
T = CHUNK


def _lane_ids(width):
    return lax.broadcasted_iota(jnp.int32, (T, width), 1)


def _rope_swap(x):
    w = x.shape[1]
    l64 = _lane_ids(w) % B_HEAD_DIM
    half = ROT_DIM // 2
    return jnp.where(l64 < half, pltpu.roll(x, w - half, 1), jnp.where(l64 < ROT_DIM, pltpu.roll(x, half, 1), 0.0))


def _rope(x, c, s):
    return x * c + _rope_swap(x) * s


def _rope_bwd(d, c, s):
    return d * c + _rope_swap(d * s)


def _tril():
    return lax.broadcasted_iota(jnp.int32, (T, T), 0) >= lax.broadcasted_iota(jnp.int32, (T, T), 1)


def _band_mask(has_prev):
    qi = lax.broadcasted_iota(jnp.int32, (T, 2 * T), 0)
    kj = lax.broadcasted_iota(jnp.int32, (T, 2 * T), 1)
    return (kj > qi) & (kj <= qi + WINDOW) & ((kj >= WINDOW) | has_prev)


def _a_norm(v, g, b):
    mu = jnp.mean(v, axis=1, keepdims=True)
    vc = v - mu
    var = jnp.mean(vc * vc, axis=1, keepdims=True)
    rstd = lax.rsqrt(var + LN_EPS)
    vh = vc * rstd
    return vh, rstd, vh * g + b


def _kv_operands(hk, k_rot, kp_rot, vv, vp):
    lo = _lane_ids(LANES) < B_HEAD_DIM
    mine = lo if hk == 0 else jnp.logical_not(lo)
    both = lambda t: jnp.where(mine, t, 0.0) + pltpu.roll(jnp.where(mine, t, 0.0), B_HEAD_DIM, 1)
    k2 = jnp.concatenate([both(kp_rot), both(k_rot)], axis=0)
    v2 = jnp.concatenate([both(vp), both(vv)], axis=0)
    return k2.astype(MXU_DTYPE), v2.astype(MXU_DTYPE)


GROUP_HEADS = B_Q_HEADS // B_KV_HEADS


def _pair_columns(hk):
    return [(hk * GROUP_HEADS // 2 + pp) * LANES for pp in range(GROUP_HEADS // 2)]


def _stacked_heads(pairs):
    lo = _lane_ids(LANES) < B_HEAD_DIM
    rows = []
    for tile in pairs:
        rows += [jnp.where(lo, tile, 0.0), jnp.where(lo, 0.0, tile)]
    return jnp.concatenate(rows, axis=0).astype(MXU_DTYPE)


def _pair_of(stacked, pp):
    lo = _lane_ids(LANES) < B_HEAD_DIM
    return jnp.where(lo, stacked[2 * pp * T:(2 * pp + 1) * T], stacked[(2 * pp + 1) * T:(2 * pp + 2) * T])


def _softmax_with_sink(qm, k2, valid, sink_ref, head0):
    scores = (_dot_nt(qm, k2) * ATTN_SCALE).reshape(GROUP_HEADS, T, 2 * T)
    s = jnp.where(valid[None], scores, NEG_BIG).reshape(GROUP_HEADS * T, 2 * T)
    sink = jnp.concatenate([jnp.broadcast_to(sink_ref[:, head0 + h:head0 + h + 1], (T, 1))
                            for h in range(GROUP_HEADS)], axis=0)
    m = jnp.maximum(jnp.max(s, axis=1, keepdims=True), sink)
    ex = jnp.exp(s - m)
    es = jnp.exp(sink - m)
    inv = 1.0 / (jnp.sum(ex, axis=1, keepdims=True) + es)
    return ex * inv, es * inv


EVEN_SHARD = EVEN_IN // N_CHIPS
KV_CHIP = K0 // EVEN_SHARD
KV_SPAN = 3 * LANES
KP0, VP0 = K0 - KV_CHIP * EVEN_SHARD, V0 - KV_CHIP * EVEN_SHARD
assert V0 // EVEN_SHARD == KV_CHIP and VP0 + LANES <= KV_SPAN


def _even_specs(s):
    nb = s // T
    return nb, {
        'h': lambda r: pl.BlockSpec((N_CHIPS, T, EVEN_SHARD), lambda i: (0, r(i), 0)),
        'kvprev': lambda r: pl.BlockSpec((None, T, KV_SPAN), lambda i: (KV_CHIP, jnp.maximum(r(i) - 1, 0), 0)),
        'tab': lambda r: pl.BlockSpec((T, A_WIDTH), lambda i: (r(i), 0)),
        'tabprev': lambda r: pl.BlockSpec((T, LANES), lambda i: (jnp.maximum(r(i) - 1, 0), 0)),
    }


def _rows_from_shards(h4_ref, rows_ref):
    for j in range(N_CHIPS):
        rows_ref[:, j * EVEN_SHARD:(j + 1) * EVEN_SHARD] = h4_ref[j]


def _full(shape):
    nd = len(shape)
    return pl.BlockSpec(shape, lambda i: (0,) * nd)


def _even_fwd(h4, cq, sq, ln_g, ln_b, ws, bs_t, sinks):
    s = h4.shape[1]
    nb, sp = _even_specs(s)

    def body(h4_ref, kvp_ref, cq_ref, sq_ref, ckp_ref, skp_ref, g_ref, b_ref, ws_ref, bst_ref, sink_ref, mix_ref, h_ref):
        i = pl.program_id(0)
        _rows_from_shards(h4_ref, h_ref)
        tril = _tril()
        _, _, vn = _a_norm(h_ref[:, 1024:2048], g_ref[...], b_ref[...])
        vnb = vn.astype(MXU_DTYPE)
        for g in range(A_GROUPS):
            c0, c1 = g * LANES, (g + 1) * LANES
            wg = jnp.where(tril, ws_ref[g], 0.0).astype(MXU_DTYPE)
            mixed = _dot(wg, vnb[:, c0:c1]) + bst_ref[:, g:g + 1]
            sil, _ = _silu_and_grad(h_ref[:, 2048 + c0:2048 + c1])
            mix_ref[:, c0:c1] = (h_ref[:, c0:c1] * mixed * sil).astype(mix_ref.dtype)

        k_rot = _rope(h_ref[:, K0:K0 + LANES], cq_ref[:, 0:LANES], sq_ref[:, 0:LANES])
        kp_rot = _rope(kvp_ref[:, KP0:KP0 + LANES], ckp_ref[...], skp_ref[...])
        vp = kvp_ref[:, VP0:VP0 + LANES]
        valid = _band_mask(i > 0)
        for hk in range(B_KV_HEADS):
            k2, v2 = _kv_operands(hk, k_rot, kp_rot, h_ref[:, V0:V0 + LANES], vp)
            qm = _stacked_heads([_rope(h_ref[:, Q0 + c0:Q0 + c0 + LANES], cq_ref[:, c0:c0 + LANES],
                                       sq_ref[:, c0:c0 + LANES]) for c0 in _pair_columns(hk)])
            p, _ = _softmax_with_sink(qm, k2, valid, sink_ref, GROUP_HEADS * hk)
            out = _dot(p.astype(MXU_DTYPE), v2)
            for pp, c0 in enumerate(_pair_columns(hk)):
                sil, _ = _silu_and_grad(h_ref[:, BG0 + c0:BG0 + c0 + LANES])
                mix_ref[:, A_WIDTH + c0:A_WIDTH + c0 + LANES] = (_pair_of(out, pp) * sil).astype(mix_ref.dtype)

    ident = lambda i: i
    return pl.pallas_call(
        body, name="even_fwd", grid=(nb,),
        in_specs=[sp['h'](ident), sp['kvprev'](ident), sp['tab'](ident), sp['tab'](ident),
                  sp['tabprev'](ident), sp['tabprev'](ident), _full((1, A_WIDTH)), _full((1, A_WIDTH)),
                  _full((A_GROUPS, T, T)), _full((T, A_GROUPS)), _full((1, B_Q_HEADS))],
        out_specs=pl.BlockSpec((T, 2 * A_WIDTH), lambda i: (i, 0)),
        out_shape=jax.ShapeDtypeStruct((s, 2 * A_WIDTH), MXU_DTYPE),
        scratch_shapes=[pltpu.VMEM((T, EVEN_IN), F32)],
        compiler_params=_params(("parallel",)),
    )(h4, h4, cq, sq, cq, sq, ln_g.reshape(1, -1), ln_b.reshape(1, -1), ws, bs_t, sinks.reshape(1, -1))


def _even_bwd(h4, dmix, cq, sq, ln_g, ln_b, ws, bs_t, sinks):
    s = h4.shape[1]
    nb, sp = _even_specs(s)

    def body(h4_ref, kvp_ref, cq_ref, sq_ref, ckp_ref, skp_ref, dm_ref, g_ref, b_ref, ws_ref, bst_ref, sink_ref,
             dh4_ref, dg_ref, db_ref, dws_ref, dbst_ref, dsink_ref, h_ref, dh_ref, dvn_scr, ck_scr, cv_scr):
        i = pl.program_id(0)

        @pl.when(i == 0)
        def _():
            for ref in (dg_ref, db_ref, dws_ref, dbst_ref, dsink_ref, ck_scr, cv_scr):
                ref[...] = jnp.zeros_like(ref)

        _rows_from_shards(h4_ref, h_ref)

        tril = _tril()
        vh, rstd, vn = _a_norm(h_ref[:, 1024:2048], g_ref[...], b_ref[...])
        vnb = vn.astype(MXU_DTYPE)
        for g in range(A_GROUPS):
            c0, c1 = g * LANES, (g + 1) * LANES
            wg = jnp.where(tril, ws_ref[g], 0.0).astype(MXU_DTYPE)
            mixed = _dot(wg, vnb[:, c0:c1]) + bst_ref[:, g:g + 1]
            ag = h_ref[:, 2048 + c0:2048 + c1]
            u = h_ref[:, c0:c1]
            sil, dsil = _silu_and_grad(ag)
            da = dm_ref[:, c0:c1]
            dh_ref[:, c0:c1] = (da * mixed * sil).astype(dh_ref.dtype)
            dh_ref[:, 2048 + c0:2048 + c1] = (da * u * mixed * dsil).astype(dh_ref.dtype)
            dmx = da * u * sil
            dmb = dmx.astype(MXU_DTYPE)
            dvn_scr[:, c0:c1] = _dot_tn(wg, dmb)
            dws_ref[g] += jnp.where(tril, _dot_nt(dmb, vnb[:, c0:c1]), 0.0)
            dbst_ref[:, g:g + 1] += jnp.sum(dmx, axis=1, keepdims=True)
        dvn = dvn_scr[...]
        dg_ref[...] += jnp.sum(dvn * vh, axis=0, keepdims=True)
        db_ref[...] += jnp.sum(dvn, axis=0, keepdims=True)
        dvh = dvn * g_ref[...]
        dv = rstd * (dvh - jnp.mean(dvh, axis=1, keepdims=True) - vh * jnp.mean(dvh * vh, axis=1, keepdims=True))
        dh_ref[:, 1024:2048] = dv.astype(dh_ref.dtype)

        ck_cur, sk_cur = cq_ref[:, 0:LANES], sq_ref[:, 0:LANES]
        k_rot = _rope(h_ref[:, K0:K0 + LANES], ck_cur, sk_cur)
        kp_rot = _rope(kvp_ref[:, KP0:KP0 + LANES], ckp_ref[...], skp_ref[...])
        vp = kvp_ref[:, VP0:VP0 + LANES]
        valid = _band_mask(i < nb - 1)
        dk_all = jnp.zeros((2 * T, LANES), F32)
        dv_all = jnp.zeros((2 * T, LANES), F32)
        for hk in range(B_KV_HEADS):
            k2, v2 = _kv_operands(hk, k_rot, kp_rot, h_ref[:, V0:V0 + LANES], vp)
            cols = _pair_columns(hk)
            qm = _stacked_heads([_rope(h_ref[:, Q0 + c0:Q0 + c0 + LANES], cq_ref[:, c0:c0 + LANES],
                                       sq_ref[:, c0:c0 + LANES]) for c0 in cols])
            p, psink = _softmax_with_sink(qm, k2, valid, sink_ref, GROUP_HEADS * hk)
            pb = p.astype(MXU_DTYPE)
            out = _dot(pb, v2)
            dobs = []
            for pp, c0 in enumerate(cols):
                sil, dsil = _silu_and_grad(h_ref[:, BG0 + c0:BG0 + c0 + LANES])
                dbo = dm_ref[:, A_WIDTH + c0:A_WIDTH + c0 + LANES]
                dh_ref[:, BG0 + c0:BG0 + c0 + LANES] = dbo * _pair_of(out, pp) * dsil
                dobs.append(dbo * sil)
            dobm = _stacked_heads(dobs)
            dp = _dot_nt(dobm, v2)
            rs = jnp.sum(p * dp, axis=1, keepdims=True)
            dsb = (p * (dp - rs) * ATTN_SCALE).astype(MXU_DTYPE)
            to_sink = psink * rs
            for h in range(GROUP_HEADS):
                head = GROUP_HEADS * hk + h
                dsink_ref[:, head:head + 1] += -jnp.sum(to_sink[h * T:(h + 1) * T], axis=0, keepdims=True)
            dq = _dot(dsb, k2)
            for pp, c0 in enumerate(cols):
                dh_ref[:, Q0 + c0:Q0 + c0 + LANES] = _rope_bwd(_pair_of(dq, pp), cq_ref[:, c0:c0 + LANES],
                                                               sq_ref[:, c0:c0 + LANES])
            acc_k = _dot_tn(dsb, qm)
            acc_v = _dot_tn(pb, dobm)
            lo2 = lax.broadcasted_iota(jnp.int32, (2 * T, LANES), 1) < B_HEAD_DIM
            mine = lo2 if hk == 0 else jnp.logical_not(lo2)
            dk_all = dk_all + jnp.where(mine, acc_k + pltpu.roll(acc_k, B_HEAD_DIM, 1), 0.0)
            dv_all = dv_all + jnp.where(mine, acc_v + pltpu.roll(acc_v, B_HEAD_DIM, 1), 0.0)
        dk_rot = dk_all[T:2 * T] + ck_scr[...]
        dvv = dv_all[T:2 * T] + cv_scr[...]
        ck_scr[...] = dk_all[0:T]
        cv_scr[...] = dv_all[0:T]
        dh_ref[:, K0:K0 + LANES] = _rope_bwd(dk_rot, ck_cur, sk_cur).astype(dh_ref.dtype)
        dh_ref[:, V0:V0 + LANES] = dvv.astype(dh_ref.dtype)
        for j in range(N_CHIPS):
            dh4_ref[j] = dh_ref[:, j * EVEN_SHARD:(j + 1) * EVEN_SHARD].astype(dh4_ref.dtype)

    rev = lambda i: nb - 1 - i
    return pl.pallas_call(
        body, name="even_bwd", grid=(nb,),
        in_specs=[sp['h'](rev), sp['kvprev'](rev), sp['tab'](rev), sp['tab'](rev),
                  sp['tabprev'](rev), sp['tabprev'](rev), pl.BlockSpec((T, 2 * A_WIDTH), lambda i: (rev(i), 0)),
                  _full((1, A_WIDTH)), _full((1, A_WIDTH)), _full((A_GROUPS, T, T)), _full((T, A_GROUPS)),
                  _full((1, B_Q_HEADS))],
        out_specs=[sp['h'](rev), _full((1, A_WIDTH)), _full((1, A_WIDTH)),
                   _full((A_GROUPS, T, T)), _full((T, A_GROUPS)), _full((1, B_Q_HEADS))],
        out_shape=[jax.ShapeDtypeStruct((N_CHIPS, s, EVEN_SHARD), MXU_DTYPE), jax.ShapeDtypeStruct((1, A_WIDTH), F32),
                   jax.ShapeDtypeStruct((1, A_WIDTH), F32), jax.ShapeDtypeStruct((A_GROUPS, T, T), F32),
                   jax.ShapeDtypeStruct((T, A_GROUPS), F32), jax.ShapeDtypeStruct((1, B_Q_HEADS), F32)],
        scratch_shapes=[pltpu.VMEM((T, EVEN_IN), F32), pltpu.VMEM((T, EVEN_IN), F32), pltpu.VMEM((T, A_WIDTH), F32),
                        pltpu.VMEM((T, LANES), F32), pltpu.VMEM((T, LANES), F32)],
        compiler_params=_params(("arbitrary",)),
    )(h4, h4, cq, sq, cq, sq, dmix, ln_g.reshape(1, -1), ln_b.reshape(1, -1), ws, bs_t, sinks.reshape(1, -1))


def _shift_down(x, d, fill=0.0):
    rows = lax.broadcasted_iota(jnp.int32, x.shape, 0)
    return jnp.where(rows >= d, pltpu.roll(x, d, 0), fill)


def _shift_up(x, d, fill=0.0):
    n = x.shape[0]
    rows = lax.broadcasted_iota(jnp.int32, x.shape, 0)
    return jnp.where(rows < n - d, pltpu.roll(x, n - d, 0), fill)


SCAN_BLOCK = 128
SUBLANES = 8


def _scan_block(a, b, up):
    n = a.shape[0]
    d = 1
    while d < n:
        if d < SUBLANES:
            shift = _shift_up if up else _shift_down
            a, b = a * shift(a, d, 1.0), a * shift(b, d) + b
        elif up:
            a, b = (jnp.concatenate([a[:n - d] * a[d:], a[n - d:]], axis=0),
                    jnp.concatenate([a[:n - d] * b[d:] + b[:n - d], b[n - d:]], axis=0))
        else:
            a, b = (jnp.concatenate([a[:d], a[d:] * a[:n - d]], axis=0),
                    jnp.concatenate([b[:d], a[d:] * b[:n - d] + b[d:]], axis=0))
        d *= 2
    return a, b


def _scan(a, b, up):
    n = a.shape[0]
    size = min(SCAN_BLOCK, n)
    blocks = n // size
    out, carry = [None] * blocks, None
    for k in (reversed(range(blocks)) if up else range(blocks)):
        prod, h = _scan_block(a[k * size:(k + 1) * size], b[k * size:(k + 1) * size], up)
        if carry is not None:
            h = h + prod * carry
        carry = h[0:1] if up else h[size - 1:size]
        out[k] = h
    return jnp.concatenate(out, axis=0)


def _neg_expm1(y):
    u = jnp.exp(y)
    um1 = u - 1.0
    safe = jnp.where(um1 == 0.0, 1.0, jnp.log(u))
    near = jnp.where(um1 == 0.0, y, um1 * y / safe)
    return -jnp.where(y > -0.5, near, um1)


def _softplus(x):
    return jnp.maximum(x, 0.0) + jnp.log1p(jnp.exp(-jnp.abs(x)))


def _lru_forward(xc, cw, cb, wa, ba, wx, bx, lam):
    shifted = [_shift_down(xc, CONV_WIDTH - 1 - j) if j < CONV_WIDTH - 1 else xc for j in range(CONV_WIDTH)]
    xconv = cb
    for j in range(CONV_WIDTH):
        xconv = xconv + cw[j:j + 1, :] * shifted[j]
    xb = xconv.astype(MXU_DTYPE)
    r = _sigmoid(_dot(xb, wa.astype(MXU_DTYPE)) + ba)
    ig = _sigmoid(_dot(xb, wx.astype(MXU_DTYPE)) + bx)
    sp = _softplus(-lam)
    log_a = -LRU_C * r * sp
    a = jnp.exp(log_a)
    mult = jnp.sqrt(_neg_expm1(2.0 * log_a))
    hs = _scan(a, mult * ig * xconv, up=False)
    return dict(shifted=shifted, xconv=xconv, xb=xb, r=r, ig=ig, sp=sp, a=a, mult=mult, hs=hs)


def _col(width, off):
    return lambda s: pl.BlockSpec((s, width), lambda i: (0, off + i))


def _part(width, slot):
    return lambda s: pl.BlockSpec((None, s, width), lambda i: (slot, 0, i))


def _lru_specs(s):
    w = LANES
    return [_part(w, 0)(s), _part(w, 1)(s), pl.BlockSpec((CONV_WIDTH, w), lambda i: (0, i)),
            pl.BlockSpec((1, w), lambda i: (0, i)), pl.BlockSpec((None, w, w), lambda i: (i, 0, 0)),
            pl.BlockSpec((1, w), lambda i: (0, i)), pl.BlockSpec((None, w, w), lambda i: (i, 0, 0)),
            pl.BlockSpec((1, w), lambda i: (0, i)), pl.BlockSpec((1, w), lambda i: (0, i))]


def _lru_fwd(h, cw, cb, wa, ba, wx, bx, lam):
    s = h.shape[1]

    def body(xc_ref, cg_ref, cw_ref, cb_ref, wa_ref, ba_ref, wx_ref, bx_ref, lam_ref, o_ref):
        f = _lru_forward(xc_ref[...], cw_ref[...], cb_ref[...], wa_ref[...], ba_ref[...], wx_ref[...], bx_ref[...],
                         lam_ref[...])
        sil, _ = _silu_and_grad(cg_ref[...])
        o_ref[...] = (f['hs'] * sil).astype(o_ref.dtype)

    return pl.pallas_call(
        body, name="lru_fwd", grid=(C_HEADS,), in_specs=_lru_specs(s), out_specs=_col(LANES, 0)(s),
        out_shape=jax.ShapeDtypeStruct((s, C_WIDTH + D_WIDTH), MXU_DTYPE), compiler_params=_params(("parallel",)),
    )(h, h, cw, cb.reshape(1, -1), wa, ba.reshape(1, -1), wx, bx.reshape(1, -1), lam.reshape(1, -1))


def _lru_bwd(h, dmix, cw, cb, wa, ba, wx, bx, lam):
    s = h.shape[1]

    def body(xc_ref, cg_ref, cw_ref, cb_ref, wa_ref, ba_ref, wx_ref, bx_ref, lam_ref, dco_ref,
             dh_ref, dcw_ref, dcb_ref, dwa_ref, dba_ref, dwx_ref, dbx_ref, dlam_ref):
        dxc_ref, dcg_ref = dh_ref.at[0], dh_ref.at[1]
        cw, lam = cw_ref[...], lam_ref[...]
        wab, wxb = wa_ref[...].astype(MXU_DTYPE), wx_ref[...].astype(MXU_DTYPE)
        f = _lru_forward(xc_ref[...], cw, cb_ref[...], wa_ref[...], ba_ref[...], wx_ref[...], bx_ref[...], lam)
        sil, dsil = _silu_and_grad(cg_ref[...])
        dco = dco_ref[...]
        hs, a, mult, ig, r, xconv = f['hs'], f['a'], f['mult'], f['ig'], f['r'], f['xconv']
        dcg_ref[...] = (dco * hs * dsil).astype(dcg_ref.dtype)
        lamb = _scan(_shift_up(a, 1), dco * sil, up=True)
        da = lamb * _shift_down(hs, 1)
        d_ig = lamb * mult * xconv
        d_mult = lamb * ig * xconv
        dxconv = lamb * mult * ig
        d_log_a = da * a - d_mult * (a * a) / mult
        dlam_ref[...] = jnp.sum(d_log_a * r, axis=0, keepdims=True) * LRU_C * _sigmoid(-lam)
        dpa = d_log_a * (-LRU_C * f['sp']) * r * (1.0 - r)
        dpx = d_ig * ig * (1.0 - ig)
        dba_ref[...] = jnp.sum(dpa, axis=0, keepdims=True)
        dbx_ref[...] = jnp.sum(dpx, axis=0, keepdims=True)
        dpab, dpxb = dpa.astype(MXU_DTYPE), dpx.astype(MXU_DTYPE)
        dwa_ref[...] = _dot_tn(f['xb'], dpab)
        dwx_ref[...] = _dot_tn(f['xb'], dpxb)
        dxconv = dxconv + _dot_nt(dpab, wab) + _dot_nt(dpxb, wxb)
        dcb_ref[...] = jnp.sum(dxconv, axis=0, keepdims=True)
        dxc = jnp.zeros_like(dxconv)
        for j in range(CONV_WIDTH):
            dcw_ref[j:j + 1, :] = jnp.sum(dxconv * f['shifted'][j], axis=0, keepdims=True)
            back = _shift_up(dxconv, CONV_WIDTH - 1 - j) if j < CONV_WIDTH - 1 else dxconv
            dxc = dxc + cw[j:j + 1, :] * back
        dxc_ref[...] = dxc.astype(dxc_ref.dtype)

    w = LANES
    vec = pl.BlockSpec((1, w), lambda i: (0, i))
    mat = pl.BlockSpec((None, w, w), lambda i: (i, 0, 0))
    vshape = jax.ShapeDtypeStruct((1, C_WIDTH), F32)
    mshape = jax.ShapeDtypeStruct((C_HEADS, w, w), F32)
    return pl.pallas_call(
        body, name="lru_bwd", grid=(C_HEADS,), in_specs=_lru_specs(s) + [_col(w, 0)(s)],
        out_specs=[pl.BlockSpec((2, s, w), lambda i: (0, 0, i)), pl.BlockSpec((CONV_WIDTH, w), lambda i: (0, i)),
                   vec, mat, vec, mat, vec, vec],
        out_shape=[jax.ShapeDtypeStruct((N_CHIPS, s, C_WIDTH), MXU_DTYPE),
                   jax.ShapeDtypeStruct((CONV_WIDTH, C_WIDTH), F32), vshape, mshape, vshape, mshape, vshape, vshape],
        compiler_params=_params(("parallel",)),
    )(h, h, cw, cb.reshape(1, -1), wa, ba.reshape(1, -1), wx, bx.reshape(1, -1), lam.reshape(1, -1), dmix)


def _pool_window(g):
    return jnp.left_shift(2, g).astype(F32)


def _select_window(g, sums):
    out = sums[-1]
    for k in range(len(sums) - 2, -1, -1):
        out = jnp.where(g == k, sums[k], out)
    return out


def _window_sums(x, shift):
    sums, cur = [], x
    for k in range(D_GROUPS):
        cur = cur + shift(cur, 2 ** k)
        sums.append(cur)
    return sums


def _pool_specs(s):
    w = D_GROUP_DIM
    return [_part(w, 2)(s), _part(w, 3)(s),
            pl.BlockSpec((None, w, w), lambda i: (i, 0, 0)), pl.BlockSpec((1, w), lambda i: (0, i))]


def _pool_forward(xd, g):
    rows = lax.broadcasted_iota(jnp.int32, xd.shape, 0).astype(F32) + 1.0
    cnt = jnp.minimum(rows, _pool_window(g))
    pooled = _select_window(g, _window_sums(xd, _shift_down)) / cnt - xd
    return pooled, cnt


def _pool_fwd(h, mix, w_pool, d_scale):
    s = h.shape[1]

    def body(xd_ref, dg_ref, wp_ref, sc_ref, mix_ref, o_ref):
        pooled, _ = _pool_forward(xd_ref[...], pl.program_id(0))
        mixed = _dot(pooled.astype(MXU_DTYPE), wp_ref[...].astype(MXU_DTYPE))
        sil, _ = _silu_and_grad(dg_ref[...])
        o_ref[...] = (mixed * sc_ref[...] * sil).astype(o_ref.dtype)

    return pl.pallas_call(
        body, name="pool_fwd", grid=(D_GROUPS,), in_specs=_pool_specs(s) + [ANY],
        out_specs=_col(D_GROUP_DIM, C_WIDTH // D_GROUP_DIM)(s), out_shape=jax.ShapeDtypeStruct(mix.shape, mix.dtype),
        input_output_aliases={4: 0}, compiler_params=_params(("parallel",)),
    )(h, h, w_pool, d_scale.reshape(1, -1), mix)


def _pool_bwd(h, dmix, dh, w_pool, d_scale):
    s = h.shape[1]
    w = D_GROUP_DIM

    def body(xd_ref, dg_ref, wp_ref, sc_ref, ddo_ref, dh_in_ref, dh_ref, dwp_ref, dsc_ref):
        dxd_ref, ddg_ref = dh_ref.at[0], dh_ref.at[1]
        g = pl.program_id(0)
        pooled, cnt = _pool_forward(xd_ref[...], g)
        pb = pooled.astype(MXU_DTYPE)
        wpb = wp_ref[...].astype(MXU_DTYPE)
        mixed = _dot(pb, wpb)
        sil, dsil = _silu_and_grad(dg_ref[...])
        ddo = ddo_ref[...]
        sc = sc_ref[...]
        ddg_ref[...] = (ddo * mixed * sc * dsil).astype(ddg_ref.dtype)
        dms = ddo * sil
        dsc_ref[...] = jnp.sum(dms * mixed, axis=0, keepdims=True)
        dmb = (dms * sc).astype(MXU_DTYPE)
        dwp_ref[...] = _dot_tn(pb, dmb)
        dpooled = _dot_nt(dmb, wpb)
        dxd = _select_window(g, _window_sums(dpooled / cnt, _shift_up)) - dpooled
        dxd_ref[...] = dxd.astype(dxd_ref.dtype)

    return pl.pallas_call(
        body, name="pool_bwd", grid=(D_GROUPS,), in_specs=_pool_specs(s) + [_col(w, C_WIDTH // w)(s), ANY],
        out_specs=[pl.BlockSpec((2, s, w), lambda i: (1, 0, i)), pl.BlockSpec((None, w, w), lambda i: (i, 0, 0)),
                   pl.BlockSpec((1, w), lambda i: (0, i))],
        out_shape=[jax.ShapeDtypeStruct(dh.shape, dh.dtype),
                   jax.ShapeDtypeStruct((D_GROUPS, w, w), F32), jax.ShapeDtypeStruct((1, D_WIDTH), F32)],
        input_output_aliases={5: 0}, compiler_params=_params(("parallel",)),
    )(h, h, w_pool, d_scale.reshape(1, -1), dmix, dh)


def _rope_tables(positions):
    s = positions.shape[0]
    inv_freq = ROPE_THETA ** (-jnp.arange(0, ROT_DIM, 2, dtype=F32) / ROT_DIM)
    ang = positions.astype(F32)[:, None] * inv_freq
    c, sn = jnp.cos(ang), jnp.sin(ang)
    rest = B_HEAD_DIM - ROT_DIM
    c64 = jnp.concatenate([c, c, jnp.ones((s, rest), F32)], axis=1)
    s64 = jnp.concatenate([-sn, sn, jnp.zeros((s, rest), F32)], axis=1)
    return jnp.tile(c64, (1, B_Q_HEADS)), jnp.tile(s64, (1, B_Q_HEADS))


def _local_step(x, positions, target, w, projections_of, projection_grads, norm_grads_done):
    cq, sq = _rope_tables(positions)
    saved = []
    xin, xin_b = x, x
    for layer in range(DEPTH):
        j = layer // 2
        kind = 'even' if layer % 2 == 0 else 'odd'
        w_in, w_out = projections_of(layer, xin_b)
        h = _mm_in(xin_b, w_in, W_IN_TRANSPOSED[kind], "mm_" + kind + "_in")
        if kind == 'even':
            mix = _even_fwd(h, cq, sq, w['even_a_ln_g'][j], w['even_a_ln_b'][j], w['even_a_ws'][j],
                            w['even_a_bs'][j].T, w['even_b_sinks'][j])
        else:
            mix = _lru_fwd(h, w['odd_conv_w'][j], w['odd_conv_b'][j], w['odd_w_a'][j], w['odd_b_a'][j],
                           w['odd_w_x'][j], w['odd_b_x'][j], w['odd_lam'][j])
            mix = _pool_fwd(h, mix, w['odd_w_pool'][j], w['odd_d_scale'][j])
        z, out, out_b = _mm_out_norm(mix, w_out, xin, w[kind + '_ln_g'][j], w[kind + '_ln_b'][j], "mm_out_norm")
        saved.append((xin_b, h, mix, z, w_in, w_out))
        xin, xin_b = out, out_b

    loss, dout = _loss_and_grad(xin, target)
    per_layer = {n: [None, None] for n in SMALL_NAMES}
    behind = loss
    for layer in reversed(range(DEPTH)):
        j = layer // 2
        kind = 'even' if layer % 2 == 0 else 'odd'
        xin_b, h, mix, z, w_in, w_out = saved[layer]
        dz, dz_b, dg, db = _ln_bwd(dout, z, w[kind + '_ln_g'][j], behind)
        per_layer[kind + '_ln_g'][j], per_layer[kind + '_ln_b'][j] = dg[0], db[0]
        dmix = _mm_dmix(dz_b, w_out, norm_grads_done(layer, dz_b), "mm_dmix")
        dw_out = _mm_dw_out(mix, dz_b, "mm_dw_out")
        if kind == 'even':
            dh, dlg, dlb, dws, dbst, dsink = _even_bwd(
                h, dmix, cq, sq, w['even_a_ln_g'][j], w['even_a_ln_b'][j], w['even_a_ws'][j], w['even_a_bs'][j].T,
                w['even_b_sinks'][j])
            for n, val in (('even_a_ln_g', dlg[0]), ('even_a_ln_b', dlb[0]), ('even_a_ws', dws),
                           ('even_a_bs', dbst.T), ('even_b_sinks', dsink[0])):
                per_layer[n][j] = val
        else:
            dh, dcw, dcb, dwa, dba, dwx, dbx, dlam = _lru_bwd(
                h, dmix, w['odd_conv_w'][j], w['odd_conv_b'][j], w['odd_w_a'][j], w['odd_b_a'][j], w['odd_w_x'][j],
                w['odd_b_x'][j], w['odd_lam'][j])
            dh, dwp, dsc = _pool_bwd(h, dmix, dh, w['odd_w_pool'][j], w['odd_d_scale'][j])
            for n, val in (('odd_conv_w', dcw), ('odd_conv_b', dcb[0]), ('odd_w_a', dwa), ('odd_b_a', dba[0]),
                           ('odd_w_x', dwx), ('odd_b_x', dbx[0]), ('odd_lam', dlam[0]), ('odd_w_pool', dwp),
                           ('odd_d_scale', dsc[0])):
                per_layer[n][j] = val
        dout = _mm_dx(dh, w_in, dz, W_IN_TRANSPOSED[kind], "mm_dx_" + kind)
        dw_in = _mm_dw_in(xin_b, dh, W_IN_TRANSPOSED[kind], "mm_dw_in_" + kind)
        behind = projection_grads(layer, dw_in, dw_out)
    return loss, dout, {n: jnp.stack(v) for n, v in per_layer.items()}


ANY = pl.BlockSpec(memory_space=pl.ANY)


def _where_am_i():
    return lax.axis_index("x"), lax.axis_index("y"), lax.axis_index("c")


def _other_chips(x, y):
    return [(1 - x, y), (x, 1 - y), (1 - x, 1 - y)]


PACKED_ROWS = 16


def _half_rows(rows, which):
    half = rows // 2
    assert half % PACKED_ROWS == 0, rows
    return pl.ds(pl.multiple_of(which * half, PACKED_ROWS), half)


def _sibling_swap(bufs, name, *, other_half_of_axis1=False, after=()):
    n, na = len(bufs), len(after)
    out_shapes = [(b.shape[0], b.shape[1] // 2, b.shape[2]) if other_half_of_axis1 else b.shape for b in bufs]

    def body(*refs):
        x_refs, out_refs, (send_sems, recv_sems) = refs[:n], refs[n + na:2 * n + na], refs[2 * n + na:]
        x, y, c = _where_am_i()
        copies = []
        for p in range(n):
            src = x_refs[p].at[:, _half_rows(bufs[p].shape[1], 1 - c), :] if other_half_of_axis1 else x_refs[p]
            copies.append(pltpu.make_async_remote_copy(
                src_ref=src, dst_ref=out_refs[p], send_sem=send_sems.at[p], recv_sem=recv_sems.at[p],
                device_id=(x, y, 1 - c), device_id_type=MESH))
            copies[-1].start()
        for cp in copies:
            cp.wait()

    return pl.pallas_call(
        body, name=name, in_specs=[ANY] * (n + na), out_specs=[ANY] * n,
        out_shape=[jax.ShapeDtypeStruct(s, b.dtype) for s, b in zip(out_shapes, bufs)],
        scratch_shapes=[pltpu.SemaphoreType.DMA((n,)), pltpu.SemaphoreType.DMA((n,))],
    )(*bufs, *after)


HBM = pl.BlockSpec(memory_space=pltpu.HBM)
SEM = pl.BlockSpec(memory_space=pltpu.SEMAPHORE)
IN_FLIGHT = pltpu.CompilerParams(has_side_effects=pltpu.SideEffectType.DATAFLOW_SIDE_EFFECTING)


def _peers(kind, x, y, c):
    return [(x, y, 1 - c)] if kind == 'swap' else [(px, py, c) for px, py in _other_chips(x, y)]


def _n_peers(kind):
    return 1 if kind == 'swap' else N_CHIPS - 1


def _land_shape(kind, a):
    if kind == 'gather':
        return (N_CHIPS,) + a.shape
    return (a.shape[0], a.shape[1] // 2, a.shape[2]) if kind == 'swap' else a.shape


def _ici_copy(kind, src_ref, land_ref, send_sems, recv_sems, k, peer, x, y, c):
    me = 2 * x + y
    if kind == 'gather':
        half = _half_rows(src_ref.shape[0], c)
        src, dst = src_ref.at[half, :], land_ref.at[me, half, :]
    elif kind == 'scatter':
        src, dst = src_ref.at[2 * peer[0] + peer[1]], land_ref.at[me]
    else:
        src, dst = src_ref.at[:, _half_rows(src_ref.shape[1], 1 - c), :], land_ref
    return pltpu.make_async_remote_copy(src_ref=src, dst_ref=dst, send_sem=send_sems.at[k], recv_sem=recv_sems.at[k],
                                        device_id=peer, device_id_type=MESH)


def _ici_start(kind, groups, name, after=()):
    flat = [a for g in groups for a in g]
    n, ng, np_, na = len(flat), len(groups), _n_peers(kind), len(after)
    land_shapes = [_land_shape(kind, a) for a in flat]

    def body(*refs):
        srcs, lands, sems, token = refs[:n], refs[n:2 * n], refs[2 * n + na:2 * n + na + 2 * ng], refs[-1]
        x, y, c = _where_am_i()
        p = 0
        for gi, g in enumerate(groups):
            for q in range(len(g)):
                for k, peer in enumerate(_peers(kind, x, y, c)):
                    _ici_copy(kind, srcs[p], lands[p], sems[2 * gi], sems[2 * gi + 1], np_ * q + k, peer, x, y,
                              c).start()
                p += 1
        token[...] = jnp.zeros_like(token)

    sem_shapes = [pltpu.SemaphoreType.DMA((np_ * len(g),)) for g in groups for _ in range(2)]
    outs = pl.pallas_call(
        body, name=name, in_specs=[HBM] * (2 * n) + [ANY] * na,
        out_specs=[SEM] * (2 * ng) + [HBM] * (2 * n) + [pl.BlockSpec(memory_space=pltpu.VMEM)],
        out_shape=sem_shapes + [pltpu.HBM(a.shape, a.dtype) for a in flat]
        + [pltpu.HBM(s, a.dtype) for s, a in zip(land_shapes, flat)] + [jax.ShapeDtypeStruct((8, LANES), F32)],
        input_output_aliases={i: 2 * ng + i for i in range(2 * n)}, compiler_params=IN_FLIGHT,
    )(*[pltpu.with_memory_space_constraint(a, pltpu.HBM) for a in flat],
      *[pltpu.with_memory_space_constraint(lax.empty(s, a.dtype), pltpu.HBM) for s, a in zip(land_shapes, flat)],
      *after)
    sems, srcs, lands, token = outs[:2 * ng], outs[2 * ng:2 * ng + n], outs[2 * ng + n:2 * ng + 2 * n], outs[-1]
    started, p = [], 0
    for gi, g in enumerate(groups):
        started.append((sems[2 * gi], sems[2 * gi + 1], srcs[p:p + len(g)], lands[p:p + len(g)]))
        p += len(g)
    return started, token


def _ici_wait(kind, started, after, name):
    send_sems, recv_sems, srcs, lands = started
    n, np_ = len(srcs), _n_peers(kind)
    after = list(after)

    def body(*refs):
        src_refs, land_refs, send_ref, recv_ref = refs[:n], refs[n:2 * n], refs[2 * n], refs[2 * n + 1]
        x, y, c = _where_am_i()
        for q in range(n):
            for k, peer in enumerate(_peers(kind, x, y, c)):
                cp = _ici_copy(kind, src_refs[q], land_refs[q], send_ref, recv_ref, np_ * q + k, peer, x, y, c)
                cp.wait_send()
                cp.wait_recv()

    outs = pl.pallas_call(
        body, name=name, in_specs=[HBM] * (2 * n) + [SEM, SEM] + [ANY] * len(after), out_specs=[HBM] * (2 * n),
        out_shape=[pltpu.HBM(a.shape, a.dtype) for a in list(srcs) + list(lands)],
        input_output_aliases={i: i for i in range(2 * n)}, compiler_params=IN_FLIGHT,
    )(*srcs, *lands, send_sems, recv_sems, *after)
    return outs[:n], outs[n:]


def _forward_to_sibling(lands, name):
    n = len(lands)

    def body(*refs):
        out_refs, (send_sems, recv_sems) = refs[n:2 * n], refs[2 * n:]
        x, y, c = _where_am_i()

        def copy(q, k, peer, half_of):
            rows = out_refs[q].at[2 * peer[0] + peer[1], _half_rows(lands[q].shape[1], half_of), :]
            return pltpu.make_async_remote_copy(
                src_ref=rows, dst_ref=rows, send_sem=send_sems.at[3 * q + k], recv_sem=recv_sems.at[3 * q + k],
                device_id=(x, y, 1 - c), device_id_type=MESH)

        sends = [copy(q, k, peer, c) for q in range(n) for k, peer in enumerate(_other_chips(x, y))]
        for cp in sends:
            cp.start()
        for q in range(n):
            for k, peer in enumerate(_other_chips(x, y)):
                copy(q, k, peer, 1 - c).wait_recv()
        for cp in sends:
            cp.wait_send()

    return pl.pallas_call(
        body, name=name, in_specs=[ANY] * n, out_specs=[ANY] * n,
        out_shape=[jax.ShapeDtypeStruct(a.shape, a.dtype) for a in lands],
        input_output_aliases={i: i for i in range(n)},
        scratch_shapes=[pltpu.SemaphoreType.DMA((3 * n,)), pltpu.SemaphoreType.DMA((3 * n,))],
    )(*lands)


def _with_own_slot(lands, own, chip):
    out = []
    for land, mine in zip(lands, own):
        mine = lax.dynamic_slice_in_dim(mine, chip, 1, axis=0) if mine.ndim == land.ndim else mine[None]
        out.append(lax.dynamic_update_slice(land, mine, (chip,) + (0,) * (land.ndim - 1)))
    return out


def _row_tile(rows, row_bytes, n_bufs, budget=24 * 1024 * 1024):
    best = None
    for t in range(PACKED_ROWS, rows + 1, PACKED_ROWS):
        if rows % t == 0 and 2 * n_bufs * t * row_bytes <= budget:
            best = t
    assert best is not None, (rows, row_bytes)
    return best


def _add_halves(full, recv, core, out_dtype, name):
    n, h, cols = recv.shape
    tr = _row_tile(h, cols * 4, 3)
    nt = h // tr

    def body(core_ref, a_ref, b_ref, o_ref):
        o_ref[...] = (a_ref[...].astype(F32) + b_ref[...].astype(F32)).astype(out_dtype)

    blk = (None, tr, cols)
    return pl.pallas_call(
        body, name=name,
        grid_spec=pltpu.PrefetchScalarGridSpec(
            num_scalar_prefetch=1, grid=(n, nt),
            in_specs=[pl.BlockSpec(blk, lambda j, i, core_ref: (j, core_ref[0] * nt + i, 0)),
                      pl.BlockSpec(blk, lambda j, i, core_ref: (j, i, 0))],
            out_specs=pl.BlockSpec(blk, lambda j, i, core_ref: (j, i, 0))),
        out_shape=jax.ShapeDtypeStruct((n, h, cols), out_dtype),
        compiler_params=_params(("parallel", "parallel")),
    )(core.reshape(1), full, recv)


def _add2(a, b, name):
    r, cols = a.shape
    tr = _row_tile(r, cols * 4, 3)

    def body(a_ref, b_ref, o_ref):
        o_ref[...] = a_ref[...] + b_ref[...]

    blk = pl.BlockSpec((tr, cols), lambda i: (i, 0))
    return pl.pallas_call(body, name=name, grid=(r // tr,), in_specs=[blk, blk], out_specs=blk,
                          out_shape=jax.ShapeDtypeStruct(a.shape, a.dtype), compiler_params=_params(("parallel",)))(a, b)


def _sum_slots(parts, name):
    n, h, cols = parts.shape
    tr = _row_tile(h, cols * 4, n + 1)

    def body(p_ref, o_ref):
        acc = p_ref[0].astype(F32)
        for k in range(1, n):
            acc = acc + p_ref[k].astype(F32)
        o_ref[...] = acc

    return pl.pallas_call(
        body, name=name, grid=(h // tr,), in_specs=[pl.BlockSpec((n, tr, cols), lambda i: (0, i, 0))],
        out_specs=pl.BlockSpec((tr, cols), lambda i: (i, 0)), out_shape=jax.ShapeDtypeStruct((h, cols), F32),
        compiler_params=_params(("parallel",)),
    )(parts)


def _adamw(w, g, m, v, name):
    shape = w.shape
    cols = shape[-1]
    rows = math.prod(shape[:-1])
    tr = rows
    if rows % 8 == 0:
        tr = 8
        for t in range(8, rows + 1, 8):
            if rows % t == 0 and t * cols * 4 <= 1536 * 1024:
                tr = t
    c1 = 1.0 - ADAM_B1 ** ADAM_STEP
    c2 = 1.0 - ADAM_B2 ** ADAM_STEP

    def body(w_ref, g_ref, m_ref, v_ref, d_ref, nm_ref, nv_ref):
        gg = g_ref[...]
        nm = ADAM_B1 * m_ref[...] + (1.0 - ADAM_B1) * gg
        nv = ADAM_B2 * v_ref[...] + (1.0 - ADAM_B2) * (gg * gg)
        d_ref[...] = -ADAM_LR * ((nm / c1) / (jnp.sqrt(nv / c2) + ADAM_EPS) + ADAM_WD * w_ref[...])
        nm_ref[...] = nm
        nv_ref[...] = nv

    blk = pl.BlockSpec((tr, cols), lambda i: (i, 0))
    flat = jax.ShapeDtypeStruct((rows, cols), F32)
    outs = pl.pallas_call(
        body, name=name, grid=(rows // tr,), in_specs=[blk] * 4, out_specs=[blk] * 3, out_shape=[flat] * 3,
        compiler_params=_params(("parallel",)),
    )(*[a.reshape(rows, cols) for a in (w, g, m, v)])
    return [o.reshape(shape) for o in outs]


def _adamw_update(w, g, m, v):
    c1 = 1.0 - ADAM_B1 ** ADAM_STEP
    c2 = 1.0 - ADAM_B2 ** ADAM_STEP
    nm = ADAM_B1 * m + (1.0 - ADAM_B1) * g
    nv = ADAM_B2 * v + (1.0 - ADAM_B2) * (g * g)
    return -ADAM_LR * ((nm / c1) / (jnp.sqrt(nv / c2) + ADAM_EPS) + ADAM_WD * w), nm, nv


def _adamw_projection(w, m, v, mine, theirs, core, name):
    _, r, cols = w.shape
    h = r // 2
    tr = _row_tile(h, cols * 4, 12, budget=40 * 1024 * 1024)
    nt = h // tr

    def body(core_ref, w_ref, m_ref, v_ref, a0, b0, a1, b1, g_ref, d_ref, nm_ref, nv_ref):
        layer, half = pl.program_id(0), pl.program_id(1)
        own = half == core_ref[0]
        g = jnp.where(layer == 0, jnp.where(own, a0[...], b0[...]), jnp.where(own, a1[...], b1[...]))
        g_ref[...] = g
        d_ref[...], nm_ref[...], nv_ref[...] = _adamw_update(w_ref[...], g, m_ref[...], v_ref[...])

    def piece(layer, own):
        def index(l, hh, i, core_ref):
            used = (l == layer) & ((hh == core_ref[0]) == own)
            return (jnp.where(used, i, 0), 0)
        return pl.BlockSpec((tr, cols), index)

    whole = pl.BlockSpec((None, tr, cols), lambda l, hh, i, core_ref: (l, hh * nt + i, 0))
    return pl.pallas_call(
        body, name=name,
        grid_spec=pltpu.PrefetchScalarGridSpec(
            num_scalar_prefetch=1, grid=(2, 2, nt),
            in_specs=[whole] * 3 + [piece(0, True), piece(0, False), piece(1, True), piece(1, False)],
            out_specs=[whole] * 4),
        out_shape=[jax.ShapeDtypeStruct(w.shape, F32)] * 4,
        compiler_params=_params(("arbitrary", "arbitrary", "arbitrary")),
    )(core.reshape(1), w, m, v, mine[0], theirs[0], mine[1], theirs[1])


def _chip_shape(name):
    shape = list(FULL_SHAPES[name])
    axis = BIG_SHARD_AXIS.get(name, SMALL_SHARD_AXIS.get(name))
    if axis is not None:
        shape[axis] //= N_CHIPS
    return tuple(shape)


def _round_up(n, k):
    return -(-n // k) * k


SHARDED_SMALL_WORDS = sum(math.prod(_chip_shape(n)) for n in SMALL_SHARDED)
SHARDED_SMALL_ROWS = _round_up(-(-SHARDED_SMALL_WORDS // PACK_COLS), 32)
SMALL_MATRICES = ['even_a_ws', 'odd_w_a', 'odd_w_x', 'odd_w_pool']
SMALL_VECTORS = [n for n in SMALL_NAMES if n not in SMALL_MATRICES]
VECTOR_ROWS = _round_up(-(-(sum(math.prod(FULL_SHAPES[n]) for n in SMALL_VECTORS) + 1) // PACK_COLS), 32)


def _pack_rows(flat, rows):
    return jnp.pad(flat, (0, rows * PACK_COLS - flat.shape[0])).reshape(rows, PACK_COLS)


def _pack_small_sharded(shards):
    return _pack_rows(jnp.concatenate([shards[n].reshape(-1) for n in SMALL_SHARDED]), SHARDED_SMALL_ROWS)


def _unpack_small_sharded(gathered_small):
    out = {}
    words = gathered_small.reshape(N_CHIPS, -1)
    w0 = 0
    for n in SMALL_SHARDED:
        size = math.prod(_chip_shape(n))
        blocks = words[:, w0:w0 + size].reshape((N_CHIPS,) + _chip_shape(n))
        out[n] = jnp.concatenate([blocks[j] for j in range(N_CHIPS)], axis=SMALL_SHARD_AXIS[n])
        w0 += size
    return out


def _small_grad_parts(grads, loss):
    words = [grads[n].reshape(-1) for n in SMALL_VECTORS] + [loss.reshape(-1)]
    return ([_pack_rows(jnp.concatenate(words), VECTOR_ROWS)]
            + [grads[n].reshape(-1, FULL_SHAPES[n][-1]) for n in SMALL_MATRICES])


def _own_block(name, full, chip):
    if name not in SMALL_SHARD_AXIS:
        return full
    axis = SMALL_SHARD_AXIS[name]
    width = FULL_SHAPES[name][axis] // N_CHIPS
    return lax.dynamic_slice_in_dim(full, chip * width, width, axis)


def _from_small_grad_parts(totals, chip):
    flat = totals[0].reshape(-1)
    out = {}
    w0 = 0
    for n in SMALL_VECTORS:
        size = math.prod(FULL_SHAPES[n])
        out[n] = _own_block(n, flat[w0:w0 + size].reshape(FULL_SHAPES[n]), chip)
        w0 += size
    for n, total in zip(SMALL_MATRICES, totals[1:]):
        out[n] = _own_block(n, total.reshape(FULL_SHAPES[n]), chip)
    return out, flat[w0]


def kernel(x, positions, even_w_in, even_a_ln_g, even_a_ln_b, even_a_ws, even_a_bs, even_b_sinks, even_w_out, even_ln_g, even_ln_b, odd_w_in, odd_conv_w, odd_conv_b, odd_w_a, odd_b_a, odd_w_x, odd_b_x, odd_lam, odd_w_pool, odd_d_scale, odd_w_out, odd_ln_g, odd_ln_b, loss_target, m_even_w_in, m_even_a_ln_g, m_even_a_ln_b, m_even_a_ws, m_even_a_bs, m_even_b_sinks, m_even_w_out, m_even_ln_g, m_even_ln_b, m_odd_w_in, m_odd_conv_w, m_odd_conv_b, m_odd_w_a, m_odd_b_a, m_odd_w_x, m_odd_b_x, m_odd_lam, m_odd_w_pool, m_odd_d_scale, m_odd_w_out, m_odd_ln_g, m_odd_ln_b, v_even_w_in, v_even_a_ln_g, v_even_a_ln_b, v_even_a_ws, v_even_a_bs, v_even_b_sinks, v_even_w_out, v_even_ln_g, v_even_ln_b, v_odd_w_in, v_odd_conv_w, v_odd_conv_b, v_odd_w_a, v_odd_b_a, v_odd_w_x, v_odd_b_x, v_odd_lam, v_odd_w_pool, v_odd_d_scale, v_odd_w_out, v_odd_ln_g, v_odd_ln_b):
    local = dict(zip(WEIGHT_NAMES, (even_w_in, even_a_ln_g, even_a_ln_b, even_a_ws, even_a_bs, even_b_sinks, even_w_out, even_ln_g, even_ln_b, odd_w_in, odd_conv_w, odd_conv_b, odd_w_a, odd_b_a, odd_w_x, odd_b_x, odd_lam, odd_w_pool, odd_d_scale, odd_w_out, odd_ln_g, odd_ln_b)))
    mom = dict(zip(WEIGHT_NAMES, (m_even_w_in, m_even_a_ln_g, m_even_a_ln_b, m_even_a_ws, m_even_a_bs, m_even_b_sinks, m_even_w_out, m_even_ln_g, m_even_ln_b, m_odd_w_in, m_odd_conv_w, m_odd_conv_b, m_odd_w_a, m_odd_b_a, m_odd_w_x, m_odd_b_x, m_odd_lam, m_odd_w_pool, m_odd_d_scale, m_odd_w_out, m_odd_ln_g, m_odd_ln_b)))
    vel = dict(zip(WEIGHT_NAMES, (v_even_w_in, v_even_a_ln_g, v_even_a_ln_b, v_even_a_ws, v_even_a_bs, v_even_b_sinks, v_even_w_out, v_even_ln_g, v_even_ln_b, v_odd_w_in, v_odd_conv_w, v_odd_conv_b, v_odd_w_a, v_odd_b_a, v_odd_w_x, v_odd_b_x, v_odd_lam, v_odd_w_pool, v_odd_d_scale, v_odd_w_out, v_odd_ln_g, v_odd_ln_b)))
    mx, my, core = _where_am_i()
    chip = 2 * mx + my

    stored = lambda n, a: jnp.swapaxes(a, 1, 2) if n == 'even_w_in' and W_IN_TRANSPOSED['even'] else a
    names_of = lambda layer: [('even' if layer % 2 == 0 else 'odd') + s for s in ('_w_in', '_w_out')]

    shards = [[stored(n, local[n])[layer // 2].astype(MXU_DTYPE) for n in names_of(layer)] for layer in range(DEPTH)]
    shards[0].insert(0, _pack_small_sharded(local))
    gathers, first_token = _ici_start('gather', shards[:1], "gather_start_first")
    later, gather_token = _ici_start('gather', shards[1:], "gather_start", after=[first_token])
    gathers += later

    def gathered(layer, *after):
        own, lands = _ici_wait('gather', gathers[layer], after, "gather_wait_%d" % layer)
        return _with_own_slot(_forward_to_sibling(lands, "gather_forward_%d" % layer), own, chip)

    first = gathered(0, first_token, gather_token)
    weights = dict(local)
    weights.update(_unpack_small_sharded(first[0]))
    projections_of = lambda layer, after: first[1:] if layer == 0 else gathered(layer, after)

    swapping, in_flight, mine, started = [], [], {}, {}

    def land(*after):
        layer, scatter = in_flight.pop()
        parts, lands = _ici_wait('scatter', scatter, after, "scatter_wait_%d" % layer)
        for n, arrived in zip(names_of(layer), _with_own_slot(lands, parts, chip)):
            mine[n, layer // 2] = _sum_slots(arrived, "big_sum_%s_%d" % (n, layer // 2))

    def projection_grads(layer, dw_in, dw_out):
        if in_flight:
            land(dw_in)
        (swap,), token = _ici_start('swap', [[dw_in, dw_out]], "presum_start_%d" % layer)
        swapping.append((layer, swap))
        return presum_and_scatter(token) if layer == 0 else token

    def presum_and_scatter(*after):
        layer, swap = swapping.pop()
        partial, from_sibling = _ici_wait('swap', swap, after, "presum_wait_%d" % layer)
        chip_part = [_add_halves(g, r, core, MXU_DTYPE, "big_presum_add_%s_%d" % (n, layer // 2))
                     for g, r, n in zip(partial, from_sibling, names_of(layer))]
        (scatter,), started[layer] = _ici_start('scatter', [chip_part], "scatter_start_%d" % layer)
        in_flight.append((layer, scatter))
        return started[layer]

    norm_grads_done = lambda layer, dz: presum_and_scatter(dz) if swapping else dz
    loss, grad_x, grads = _local_step(x[0], positions[0], loss_target[0], weights, projections_of, projection_grads,
                                      norm_grads_done)

    small = _small_grad_parts(grads, loss)
    chip_sums = [_add2(a, b, "small_presum_add_%d" % i)
                 for i, (a, b) in enumerate(zip(small, _sibling_swap(small, "small_swap", after=[started[0]])))]
    (small_gather,), small_token = _ici_start('gather', [chip_sums], "small_gather_start")

    reduced, deltas, new_m, new_v = {}, {}, {}, {}

    def adamw_projections(kind, after):
        names = [kind + '_w_in', kind + '_w_out']
        pieces = [(n, l) for n in names for l in range(2)]
        theirs = dict(zip(pieces, _sibling_swap([mine[p] for p in pieces], "big_join_" + kind, after=after)))
        done = []
        for n in names:
            outs = _adamw_projection(stored(n, local[n]), stored(n, mom[n]), stored(n, vel[n]),
                                     [mine[n, l] for l in range(2)], [theirs[n, l] for l in range(2)], core,
                                     "adamw_" + n)
            reduced[n], deltas[n], new_m[n], new_v[n] = [stored(n, o) for o in outs]
            done.append(outs[-1])
        return done

    done = adamw_projections('odd', [small_token])
    land(grad_x, *done)
    done = adamw_projections('even', [])

    own, lands = _ici_wait('gather', small_gather, done, "small_gather_wait")
    of_chips = _with_own_slot(_forward_to_sibling(lands, "small_gather_forward"), own, chip)
    small_totals, total_loss = _from_small_grad_parts(
        [_sum_slots(g, "small_sum_%d" % i) for i, g in enumerate(of_chips)], chip)
    reduced.update(small_totals)
    for n in SMALL_NAMES:
        deltas[n], new_m[n], new_v[n] = _adamw(local[n], reduced[n], mom[n], vel[n], "adamw_" + n)

    return (total_loss, grad_x[None], *[reduced[n] for n in WEIGHT_NAMES], *[deltas[n] for n in WEIGHT_NAMES],
            *[new_m[n] for n in WEIGHT_NAMES], *[new_v[n] for n in WEIGHT_NAMES])
```

```python
import math

import jax
import jax.numpy as jnp
from jax import lax
from jax.experimental import pallas as pl
from jax.experimental.pallas import tpu as pltpu

F32 = jnp.float32
BF16 = jnp.bfloat16
MXU_DTYPE = jnp.bfloat16

D_MODEL = 2048
DEPTH = 4
A_WIDTH = 1024
A_GROUPS = 8
CHUNK = 128
B_HEAD_DIM = 64
B_Q_HEADS = 16
B_KV_HEADS = 2
WINDOW = 128
ROT_DIM = 16
ROPE_THETA = 500000.0
C_WIDTH = 1024
C_HEADS = 8
CONV_WIDTH = 4
LRU_C = 8.0
D_WIDTH = 1024
D_GROUPS = 4
D_GROUP_DIM = 256
EVEN_IN = 5376
ODD_IN = 4096
DN_ALPHA = (2 * DEPTH) ** 0.25
LN_EPS = 1e-5
ATTN_SCALE = B_HEAD_DIM ** -0.5
NEG_BIG = -1e30

ADAM_LR = 0.001
ADAM_B1 = 0.9
ADAM_B2 = 0.999
ADAM_EPS = 1e-08
ADAM_WD = 0.01
ADAM_STEP = 10

LANES = 128
VMEM_LIMIT = 56 * 1024 * 1024
N_CHIPS = 4
MESH = pl.DeviceIdType.MESH

WEIGHT_NAMES = ['even_w_in', 'even_a_ln_g', 'even_a_ln_b', 'even_a_ws', 'even_a_bs', 'even_b_sinks', 'even_w_out',
                'even_ln_g', 'even_ln_b', 'odd_w_in', 'odd_conv_w', 'odd_conv_b', 'odd_w_a', 'odd_b_a', 'odd_w_x',
                'odd_b_x', 'odd_lam', 'odd_w_pool', 'odd_d_scale', 'odd_w_out', 'odd_ln_g', 'odd_ln_b']
BIG_SHARD_AXIS = {'even_w_in': 2, 'even_w_out': 1, 'odd_w_in': 2, 'odd_w_out': 1}
SMALL_SHARD_AXIS = {'odd_conv_w': 2, 'odd_conv_b': 1, 'odd_b_a': 1, 'odd_b_x': 1, 'odd_lam': 1, 'odd_w_pool': 2,
                    'odd_d_scale': 1, 'odd_ln_g': 1, 'odd_ln_b': 1}
FULL_SHAPES = {
    'even_w_in': (2, 2048, 5376), 'even_a_ln_g': (2, 1024), 'even_a_ln_b': (2, 1024), 'even_a_ws': (2, 8, 128, 128),
    'even_a_bs': (2, 8, 128), 'even_b_sinks': (2, 16), 'even_w_out': (2, 2048, 2048), 'even_ln_g': (2, 2048),
    'even_ln_b': (2, 2048), 'odd_w_in': (2, 2048, 4096), 'odd_conv_w': (2, 4, 1024), 'odd_conv_b': (2, 1024),
    'odd_w_a': (2, 8, 128, 128), 'odd_b_a': (2, 1024), 'odd_w_x': (2, 8, 128, 128), 'odd_b_x': (2, 1024),
    'odd_lam': (2, 1024), 'odd_w_pool': (2, 4, 256, 256), 'odd_d_scale': (2, 1024), 'odd_w_out': (2, 2048, 2048),
    'odd_ln_g': (2, 2048), 'odd_ln_b': (2, 2048)}
BIG_NAMES = ['even_w_in', 'even_w_out', 'odd_w_in', 'odd_w_out']
SMALL_SHARDED = ['odd_conv_w', 'odd_conv_b', 'odd_b_a', 'odd_b_x', 'odd_lam', 'odd_d_scale', 'odd_ln_g', 'odd_ln_b',
                 'odd_w_pool']
SMALL_NAMES = [n for n in WEIGHT_NAMES if n not in BIG_NAMES]
W_IN_TRANSPOSED = {'even': True, 'odd': False}
PACK_COLS = 1024


def _params(sem):
    return pltpu.CompilerParams(dimension_semantics=sem, vmem_limit_bytes=VMEM_LIMIT)


def _dot(a, b):
    return lax.dot_general(a, b, (((1,), (0,)), ((), ())), preferred_element_type=F32)


def _dot_nt(a, b):
    return lax.dot_general(a, b, (((1,), (1,)), ((), ())), preferred_element_type=F32)


def _dot_tn(a, b):
    return lax.dot_general(a, b, (((0,), (0,)), ((), ())), preferred_element_type=F32)


def _sigmoid(x):
    return 1.0 / (1.0 + jnp.exp(-x))


def _silu_and_grad(x):
    s = _sigmoid(x)
    return x * s, s * (1.0 + x * (1.0 - s))


def _mm_call(body, name, grid, in_specs, out_spec, out_shape, args):
    return pl.pallas_call(body, name=name, grid=grid, in_specs=in_specs, out_specs=out_spec, out_shape=out_shape,
                          compiler_params=_params(("parallel",) * len(grid)))(*args)


def _resident(block, index_map):
    return pl.BlockSpec(block, index_map, pipeline_mode=pl.Buffered(1))


def _rows_per_step(x, rows):
    return rows if x.dtype.itemsize == 2 else min(1024, rows)


ROWS_OF_WT = 768


def _mm_in(x, wg, transposed, name):
    s, k = x.shape
    tm = _rows_per_step(x, s)
    x_spec = _resident((tm, k), lambda i, j: (0, 0)) if tm == s else pl.BlockSpec((tm, k), lambda i, j: (i, 0))

    def body(x_ref, w_ref, o_ref):
        o_ref[...] = (_dot_nt if transposed else _dot)(x_ref[...].astype(MXU_DTYPE), w_ref[...])

    if transposed:
        wt = wg.reshape(-1, k)
        n, tn = wt.shape[0], ROWS_OF_WT
        return _mm_call(body, name, (s // tm, n // tn), [x_spec, pl.BlockSpec((tn, k), lambda i, j: (j, 0))],
                        pl.BlockSpec((tm, tn), lambda i, j: (i, j)), jax.ShapeDtypeStruct((s, n), F32), (x, wt))
    nc = wg.shape[2]
    return _mm_call(body, name, (s // tm, N_CHIPS),
                    [x_spec, pl.BlockSpec((None, k, nc), lambda i, j: (j, 0, 0))],
                    pl.BlockSpec((None, tm, nc), lambda i, j: (j, i, 0)),
                    jax.ShapeDtypeStruct((N_CHIPS, s, nc), F32), (x, wg))


def _mm_dx(dh, wg, dz, transposed, name):
    s, d = dz.shape
    tm, tn = min(1024, s), 512
    res = pl.BlockSpec((tm, tn), lambda i, j: (i, j))
    if transposed:
        wt = wg.reshape(-1, d)
        n = wt.shape[0]

        def body(a_ref, w_ref, r_ref, o_ref):
            o_ref[...] = DN_ALPHA * r_ref[...] + _dot(a_ref[...], w_ref[...])

        return _mm_call(body, name, (s // tm, d // tn),
                        [pl.BlockSpec((tm, n), lambda i, j: (i, 0)), pl.BlockSpec((n, tn), lambda i, j: (0, j)), res],
                        res, jax.ShapeDtypeStruct((s, d), F32), (dh, wt, dz))
    nc = dh.shape[2]

    def body(a_ref, w_ref, r_ref, o_ref):
        acc = DN_ALPHA * r_ref[...]
        for j in range(N_CHIPS):
            acc = acc + _dot_nt(a_ref[j], w_ref[j])
        o_ref[...] = acc

    return _mm_call(body, name, (s // tm, d // tn),
                    [pl.BlockSpec((N_CHIPS, tm, nc), lambda i, j: (0, i, 0)),
                     pl.BlockSpec((N_CHIPS, tn, nc), lambda i, j: (0, j, 0)), res],
                    res, jax.ShapeDtypeStruct((s, d), F32), (dh, wg, dz))


def _mm_dw_in(x, dh, transposed, name):
    s, d = x.shape
    tx = _rows_per_step(x, d)
    x_spec = _resident((s, tx), lambda i, j: (0, 0)) if tx == d else pl.BlockSpec((s, tx), lambda i, j: (0, j))

    def body(a_ref, b_ref, o_ref):
        o_ref[...] = _dot_tn(a_ref[...].astype(MXU_DTYPE), b_ref[...].astype(MXU_DTYPE)).astype(o_ref.dtype)

    if transposed:
        n, tn = dh.shape[1], ROWS_OF_WT
        flat = _mm_call(body, name, (n // tn, d // tx), [pl.BlockSpec((s, tn), lambda i, j: (0, i)), x_spec],
                        pl.BlockSpec((tn, tx), lambda i, j: (i, j)), jax.ShapeDtypeStruct((n, d), MXU_DTYPE), (dh, x))
        return flat.reshape(N_CHIPS, n // N_CHIPS, d)
    nc = dh.shape[2]
    x_spec = _resident((s, tx), lambda i, j: (0, 0)) if tx == d else pl.BlockSpec((s, tx), lambda i, j: (0, i))
    return _mm_call(body, name, (d // tx, N_CHIPS), [x_spec, pl.BlockSpec((None, s, nc), lambda i, j: (j, 0, 0))],
                    pl.BlockSpec((None, tx, nc), lambda i, j: (j, i, 0)),
                    jax.ShapeDtypeStruct((N_CHIPS, d, nc), MXU_DTYPE), (x, dh))


def _mm_out_norm(mix, wg, x, g, b, name, target=None):
    s, k = mix.shape
    _, kc, d = wg.shape
    tm = min(512, s)

    def normed(a_ref, w_ref, x_ref, g_ref, b_ref, z_ref):
        z = DN_ALPHA * x_ref[...] + _dot(a_ref[...], w_ref[...].reshape(N_CHIPS * kc, d))
        mu = jnp.mean(z, axis=1, keepdims=True)
        zc = z - mu
        var = jnp.mean(zc * zc, axis=1, keepdims=True)
        z_ref[...] = z
        return zc * lax.rsqrt(var + LN_EPS) * g_ref[...] + b_ref[...]

    def body(a_ref, w_ref, x_ref, g_ref, b_ref, z_ref, o_ref, ob_ref):
        o = normed(a_ref, w_ref, x_ref, g_ref, b_ref, z_ref)
        o_ref[...] = o
        ob_ref[...] = o.astype(ob_ref.dtype)

    def body_with_loss(a_ref, w_ref, x_ref, g_ref, b_ref, t_ref, z_ref, do_ref, l_ref):
        @pl.when(pl.program_id(0) == 0)
        def _():
            l_ref[...] = jnp.zeros_like(l_ref)

        e = normed(a_ref, w_ref, x_ref, g_ref, b_ref, z_ref) - t_ref[...]
        do_ref[...] = e * (1.0 / d)
        l_ref[...] += (0.5 / d) * jnp.sum(jnp.sum(e * e, axis=1, keepdims=True), axis=0, keepdims=True)

    row = pl.BlockSpec((tm, d), lambda i: (i, 0))
    vec = pl.BlockSpec((1, d), lambda i: (0, 0))
    in_specs = [pl.BlockSpec((tm, k), lambda i: (i, 0)), _resident((N_CHIPS, kc, d), lambda i: (0, 0, 0)), row, vec,
                vec]
    args = (mix, wg, x, g.reshape(1, d), b.reshape(1, d))
    f32_rows = jax.ShapeDtypeStruct((s, d), F32)
    if target is None:
        return pl.pallas_call(
            body, name=name, grid=(s // tm,), in_specs=in_specs, out_specs=[row, row, row],
            out_shape=[f32_rows, f32_rows, jax.ShapeDtypeStruct((s, d), MXU_DTYPE)],
            compiler_params=_params(("parallel",)))(*args)
    return pl.pallas_call(
        body_with_loss, name=name, grid=(s // tm,), in_specs=in_specs + [row],
        out_specs=[row, row, pl.BlockSpec((1, 1), lambda i: (0, 0))],
        out_shape=[f32_rows, f32_rows, jax.ShapeDtypeStruct((1, 1), F32)],
        compiler_params=_params(("arbitrary",)))(*args, target)


def _mm_dmix(dz, wg, after, name):
    s, d = dz.shape
    kc = wg.shape[1]

    def body(a_ref, w_ref, after_ref, o_ref):
        o_ref[...] = _dot_nt(a_ref[...], w_ref[...])

    return _mm_call(body, name, (N_CHIPS,),
                    [_resident((s, d), lambda j: (0, 0)), pl.BlockSpec((None, kc, d), lambda j: (j, 0, 0)), ANY],
                    pl.BlockSpec((s, kc), lambda j: (0, j)), jax.ShapeDtypeStruct((s, N_CHIPS * kc), F32),
                    (dz, wg, after))


def _mm_dw_out(mix, dz, name):
    s, k = mix.shape
    d = dz.shape[1]
    kc = k // N_CHIPS

    def body(a_ref, b_ref, o_ref):
        o_ref[...] = _dot_tn(a_ref[...], b_ref[...]).astype(o_ref.dtype)

    return _mm_call(body, name, (N_CHIPS,),
                    [pl.BlockSpec((s, kc), lambda i: (0, i)), _resident((s, d), lambda i: (0, 0))],
                    pl.BlockSpec((None, kc, d), lambda i: (i, 0, 0)),
                    jax.ShapeDtypeStruct((N_CHIPS, kc, d), MXU_DTYPE), (mix, dz))


def _ln_bwd(dout, z, g, after):
    s, d = z.shape
    tr = min(256, s)

    def body(do_ref, z_ref, g_ref, after_ref, dz_ref, dzb_ref, dg_ref, db_ref):
        @pl.when(pl.program_id(0) == 0)
        def _():
            dg_ref[...] = jnp.zeros_like(dg_ref)
            db_ref[...] = jnp.zeros_like(db_ref)

        zz = z_ref[...]
        do = do_ref[...]
        mu = jnp.mean(zz, axis=1, keepdims=True)
        zc = zz - mu
        var = jnp.mean(zc * zc, axis=1, keepdims=True)
        rstd = lax.rsqrt(var + LN_EPS)
        zh = zc * rstd
        dg_ref[...] += jnp.sum(do * zh, axis=0, keepdims=True)
        db_ref[...] += jnp.sum(do, axis=0, keepdims=True)
        dzh = do * g_ref[...]
        dz = rstd * (dzh - jnp.mean(dzh, axis=1, keepdims=True) - zh * jnp.mean(dzh * zh, axis=1, keepdims=True))
        dz_ref[...] = dz
        dzb_ref[...] = dz.astype(MXU_DTYPE)

    row = pl.BlockSpec((tr, d), lambda i: (i, 0))
    vec = pl.BlockSpec((1, d), lambda i: (0, 0))
    return pl.pallas_call(
        body, name="ln_bwd", grid=(s // tr,), in_specs=[row, row, vec, ANY], out_specs=[row, row, vec, vec],
        out_shape=[jax.ShapeDtypeStruct((s, d), F32), jax.ShapeDtypeStruct((s, d), MXU_DTYPE),
                   jax.ShapeDtypeStruct((1, d), F32), jax.ShapeDtypeStruct((1, d), F32)],
        compiler_params=_params(("arbitrary",)),
    )(dout, z, g.reshape(1, d), after)


Q0, K0, V0, BG0 = 3072, 4096, 4224, 4352
T = CHUNK


def _lane_ids(width):
    return lax.broadcasted_iota(jnp.int32, (T, width), 1)


def _rope_swap(x):
    w = x.shape[1]
    l64 = _lane_ids(w) % B_HEAD_DIM
    half = ROT_DIM // 2
    return jnp.where(l64 < half, pltpu.roll(x, w - half, 1), jnp.where(l64 < ROT_DIM, pltpu.roll(x, half, 1), 0.0))


def _rope(x, c, s):
    return x * c + _rope_swap(x) * s


def _rope_bwd(d, c, s):
    return d * c + _rope_swap(d * s)


def _tril():
    return lax.broadcasted_iota(jnp.int32, (T, T), 0) >= lax.broadcasted_iota(jnp.int32, (T, T), 1)


def _band_mask(has_prev):
    qi = lax.broadcasted_iota(jnp.int32, (T, 2 * T), 0)
    kj = lax.broadcasted_iota(jnp.int32, (T, 2 * T), 1)
    return (kj > qi) & (kj <= qi + WINDOW) & ((kj >= WINDOW) | has_prev)


def _a_norm(v, g, b):
    mu = jnp.mean(v, axis=1, keepdims=True)
    vc = v - mu
    var = jnp.mean(vc * vc, axis=1, keepdims=True)
    rstd = lax.rsqrt(var + LN_EPS)
    vh = vc * rstd
    return vh, rstd, vh * g + b


def _kv_operands(hk, k_rot, kp_rot, vv, vp):
    lo = _lane_ids(LANES) < B_HEAD_DIM
    mine = lo if hk == 0 else jnp.logical_not(lo)
    both = lambda t: jnp.where(mine, t, 0.0) + pltpu.roll(jnp.where(mine, t, 0.0), B_HEAD_DIM, 1)
    k2 = jnp.concatenate([both(kp_rot), both(k_rot)], axis=0)
    v2 = jnp.concatenate([both(vp), both(vv)], axis=0)
    return k2.astype(MXU_DTYPE), v2.astype(MXU_DTYPE)


GROUP_HEADS = B_Q_HEADS // B_KV_HEADS


def _pair_columns(hk):
    return [(hk * GROUP_HEADS // 2 + pp) * LANES for pp in range(GROUP_HEADS // 2)]


def _stacked_heads(pairs):
    lo = _lane_ids(LANES) < B_HEAD_DIM
    rows = []
    for tile in pairs:
        rows += [jnp.where(lo, tile, 0.0), jnp.where(lo, 0.0, tile)]
    return jnp.concatenate(rows, axis=0).astype(MXU_DTYPE)


def _pair_of(stacked, pp):
    lo = _lane_ids(LANES) < B_HEAD_DIM
    return jnp.where(lo, stacked[2 * pp * T:(2 * pp + 1) * T], stacked[(2 * pp + 1) * T:(2 * pp + 2) * T])


def _softmax_with_sink(qm, k2, valid, sink_ref, head0):
    scores = (_dot_nt(qm, k2) * ATTN_SCALE).reshape(GROUP_HEADS, T, 2 * T)
    s = jnp.where(valid[None], scores, NEG_BIG).reshape(GROUP_HEADS * T, 2 * T)
    sink = jnp.concatenate([jnp.broadcast_to(sink_ref[:, head0 + h:head0 + h + 1], (T, 1))
                            for h in range(GROUP_HEADS)], axis=0)
    m = jnp.maximum(jnp.max(s, axis=1, keepdims=True), sink)
    ex = jnp.exp(s - m)
    es = jnp.exp(sink - m)
    inv = 1.0 / (jnp.sum(ex, axis=1, keepdims=True) + es)
    return ex * inv, es * inv


def _even_specs(s):
    nb = s // T
    prev = lambda r: lambda i: jnp.maximum(r(i) - 1, 0)
    return nb, {
        'h': lambda r: pl.BlockSpec((T, EVEN_IN), lambda i: (r(i), 0)),
        'kprev': lambda r: pl.BlockSpec((T, LANES), lambda i: (prev(r)(i), K0 // LANES)),
        'vprev': lambda r: pl.BlockSpec((T, LANES), lambda i: (prev(r)(i), V0 // LANES)),
        'tab': lambda r: pl.BlockSpec((T, LANES), lambda i: (r(i), 0)),
        'tabprev': lambda r: pl.BlockSpec((T, LANES), lambda i: (prev(r)(i), 0)),
    }


def _full(shape):
    nd = len(shape)
    return pl.BlockSpec(shape, lambda i: (0,) * nd)


def _even_fwd(h, cq, sq, ln_g, ln_b, ws, bs_t, sinks):
    s = h.shape[0]
    nb, sp = _even_specs(s)

    def body(h_ref, kp_ref, vp_ref, cq_ref, sq_ref, ckp_ref, skp_ref, g_ref, b_ref, ws_ref, bst_ref, sink_ref, mix_ref):
        i = pl.program_id(0)
        tril = _tril()
        _, _, vn = _a_norm(h_ref[:, 1024:2048], g_ref[...], b_ref[...])
        vnb = vn.astype(MXU_DTYPE)
        for g in range(A_GROUPS):
            c0, c1 = g * LANES, (g + 1) * LANES
            wg = jnp.where(tril, ws_ref[g], 0.0).astype(MXU_DTYPE)
            mixed = _dot(wg, vnb[:, c0:c1]) + bst_ref[:, g:g + 1]
            sil, _ = _silu_and_grad(h_ref[:, 2048 + c0:2048 + c1])
            mix_ref[:, c0:c1] = (h_ref[:, c0:c1] * mixed * sil).astype(mix_ref.dtype)

        cq, sq = cq_ref[...], sq_ref[...]
        k_rot = _rope(h_ref[:, K0:K0 + LANES], cq, sq)
        kp_rot = _rope(kp_ref[...], ckp_ref[...], skp_ref[...])
        valid = _band_mask(i > 0)
        for hk in range(B_KV_HEADS):
            k2, v2 = _kv_operands(hk, k_rot, kp_rot, h_ref[:, V0:V0 + LANES], vp_ref[...])
            qm = _stacked_heads([_rope(h_ref[:, Q0 + c0:Q0 + c0 + LANES], cq, sq) for c0 in _pair_columns(hk)])
            p, _ = _softmax_with_sink(qm, k2, valid, sink_ref, GROUP_HEADS * hk)
            out = _dot(p.astype(MXU_DTYPE), v2)
            for pp, c0 in enumerate(_pair_columns(hk)):
                sil, _ = _silu_and_grad(h_ref[:, BG0 + c0:BG0 + c0 + LANES])
                mix_ref[:, A_WIDTH + c0:A_WIDTH + c0 + LANES] = (_pair_of(out, pp) * sil).astype(mix_ref.dtype)

    ident = lambda i: i
    return pl.pallas_call(
        body, name="even_fwd", grid=(nb,),
        in_specs=[sp['h'](ident), sp['kprev'](ident), sp['vprev'](ident), sp['tab'](ident), sp['tab'](ident),
                  sp['tabprev'](ident), sp['tabprev'](ident), _full((1, A_WIDTH)), _full((1, A_WIDTH)),
                  _full((A_GROUPS, T, T)), _full((T, A_GROUPS)), _full((1, B_Q_HEADS))],
        out_specs=pl.BlockSpec((T, 2 * A_WIDTH), lambda i: (i, 0)),
        out_shape=jax.ShapeDtypeStruct((s, 2 * A_WIDTH), MXU_DTYPE),
        compiler_params=_params(("parallel",)),
    )(h, h, h, cq, sq, cq, sq, ln_g.reshape(1, -1), ln_b.reshape(1, -1), ws, bs_t, sinks.reshape(1, -1))


def _even_bwd(h, dmix, cq, sq, ln_g, ln_b, ws, bs_t, sinks):
    s = h.shape[0]
    nb, sp = _even_specs(s)

    def body(h_ref, kp_ref, vp_ref, cq_ref, sq_ref, ckp_ref, skp_ref, dm_ref, g_ref, b_ref, ws_ref, bst_ref, sink_ref,
             dh_ref, dg_ref, db_ref, dws_ref, dbst_ref, dsink_ref, dvn_scr, ck_scr, cv_scr):
        i = pl.program_id(0)

        @pl.when(i == 0)
        def _():
            for ref in (dg_ref, db_ref, dws_ref, dbst_ref, dsink_ref, ck_scr, cv_scr):
                ref[...] = jnp.zeros_like(ref)

        tril = _tril()
        vh, rstd, vn = _a_norm(h_ref[:, 1024:2048], g_ref[...], b_ref[...])
        vnb = vn.astype(MXU_DTYPE)
        for g in range(A_GROUPS):
            c0, c1 = g * LANES, (g + 1) * LANES
            wg = jnp.where(tril, ws_ref[g], 0.0).astype(MXU_DTYPE)
            mixed = _dot(wg, vnb[:, c0:c1]) + bst_ref[:, g:g + 1]
            ag = h_ref[:, 2048 + c0:2048 + c1]
            u = h_ref[:, c0:c1]
            sil, dsil = _silu_and_grad(ag)
            da = dm_ref[:, c0:c1]
            dh_ref[:, c0:c1] = (da * mixed * sil).astype(dh_ref.dtype)
            dh_ref[:, 2048 + c0:2048 + c1] = (da * u * mixed * dsil).astype(dh_ref.dtype)
            dmx = da * u * sil
            dmb = dmx.astype(MXU_DTYPE)
            dvn_scr[:, c0:c1] = _dot_tn(wg, dmb)
            dws_ref[g] += jnp.where(tril, _dot_nt(dmb, vnb[:, c0:c1]), 0.0)
            dbst_ref[:, g:g + 1] += jnp.sum(dmx, axis=1, keepdims=True)
        dvn = dvn_scr[...]
        dg_ref[...] += jnp.sum(dvn * vh, axis=0, keepdims=True)
        db_ref[...] += jnp.sum(dvn, axis=0, keepdims=True)
        dvh = dvn * g_ref[...]
        dv = rstd * (dvh - jnp.mean(dvh, axis=1, keepdims=True) - vh * jnp.mean(dvh * vh, axis=1, keepdims=True))
        dh_ref[:, 1024:2048] = dv.astype(dh_ref.dtype)

        cq, sq = cq_ref[...], sq_ref[...]
        k_rot = _rope(h_ref[:, K0:K0 + LANES], cq, sq)
        kp_rot = _rope(kp_ref[...], ckp_ref[...], skp_ref[...])
        valid = _band_mask(i < nb - 1)
        dk_all = jnp.zeros((2 * T, LANES), F32)
        dv_all = jnp.zeros((2 * T, LANES), F32)
        for hk in range(B_KV_HEADS):
            k2, v2 = _kv_operands(hk, k_rot, kp_rot, h_ref[:, V0:V0 + LANES], vp_ref[...])
            cols = _pair_columns(hk)
            qm = _stacked_heads([_rope(h_ref[:, Q0 + c0:Q0 + c0 + LANES], cq, sq) for c0 in cols])
            p, psink = _softmax_with_sink(qm, k2, valid, sink_ref, GROUP_HEADS * hk)
            pb = p.astype(MXU_DTYPE)
            out = _dot(pb, v2)
            dobs = []
            for pp, c0 in enumerate(cols):
                sil, dsil = _silu_and_grad(h_ref[:, BG0 + c0:BG0 + c0 + LANES])
                dbo = dm_ref[:, A_WIDTH + c0:A_WIDTH + c0 + LANES]
                dh_ref[:, BG0 + c0:BG0 + c0 + LANES] = (dbo * _pair_of(out, pp) * dsil).astype(dh_ref.dtype)
                dobs.append(dbo * sil)
            dobm = _stacked_heads(dobs)
            dp = _dot_nt(dobm, v2)
            rs = jnp.sum(p * dp, axis=1, keepdims=True)
            dsb = (p * (dp - rs) * ATTN_SCALE).astype(MXU_DTYPE)
            to_sink = psink * rs
            for h in range(GROUP_HEADS):
                head = GROUP_HEADS * hk + h
                dsink_ref[:, head:head + 1] += -jnp.sum(to_sink[h * T:(h + 1) * T], axis=0, keepdims=True)
            dq = _dot(dsb, k2)
            for pp, c0 in enumerate(cols):
                dh_ref[:, Q0 + c0:Q0 + c0 + LANES] = _rope_bwd(_pair_of(dq, pp), cq, sq).astype(dh_ref.dtype)
            acc_k = _dot_tn(dsb, qm)
            acc_v = _dot_tn(pb, dobm)
            lo2 = lax.broadcasted_iota(jnp.int32, (2 * T, LANES), 1) < B_HEAD_DIM
            mine = lo2 if hk == 0 else jnp.logical_not(lo2)
            dk_all = dk_all + jnp.where(mine, acc_k + pltpu.roll(acc_k, B_HEAD_DIM, 1), 0.0)
            dv_all = dv_all + jnp.where(mine, acc_v + pltpu.roll(acc_v, B_HEAD_DIM, 1), 0.0)
        dk_rot = dk_all[T:2 * T] + ck_scr[...]
        dvv = dv_all[T:2 * T] + cv_scr[...]
        ck_scr[...] = dk_all[0:T]
        cv_scr[...] = dv_all[0:T]
        dh_ref[:, K0:K0 + LANES] = _rope_bwd(dk_rot, cq, sq).astype(dh_ref.dtype)
        dh_ref[:, V0:V0 + LANES] = dvv.astype(dh_ref.dtype)

    rev = lambda i: nb - 1 - i
    return pl.pallas_call(
        body, name="even_bwd", grid=(nb,),
        in_specs=[sp['h'](rev), sp['kprev'](rev), sp['vprev'](rev), sp['tab'](rev), sp['tab'](rev),
                  sp['tabprev'](rev), sp['tabprev'](rev), pl.BlockSpec((T, 2 * A_WIDTH), lambda i: (rev(i), 0)),
                  _full((1, A_WIDTH)), _full((1, A_WIDTH)), _full((A_GROUPS, T, T)), _full((T, A_GROUPS)),
                  _full((1, B_Q_HEADS))],
        out_specs=[sp['h'](rev), _full((1, A_WIDTH)), _full((1, A_WIDTH)),
                   _full((A_GROUPS, T, T)), _full((T, A_GROUPS)), _full((1, B_Q_HEADS))],
        out_shape=[jax.ShapeDtypeStruct((s, EVEN_IN), MXU_DTYPE), jax.ShapeDtypeStruct((1, A_WIDTH), F32),
                   jax.ShapeDtypeStruct((1, A_WIDTH), F32), jax.ShapeDtypeStruct((A_GROUPS, T, T), F32),
                   jax.ShapeDtypeStruct((T, A_GROUPS), F32), jax.ShapeDtypeStruct((1, B_Q_HEADS), F32)],
        scratch_shapes=[pltpu.VMEM((T, A_WIDTH), F32), pltpu.VMEM((T, LANES), F32), pltpu.VMEM((T, LANES), F32)],
        compiler_params=_params(("arbitrary",)),
    )(h, h, h, cq, sq, cq, sq, dmix, ln_g.reshape(1, -1), ln_b.reshape(1, -1), ws, bs_t, sinks.reshape(1, -1))


def _shift_down(x, d, fill=0.0):
    rows = lax.broadcasted_iota(jnp.int32, x.shape, 0)
    return jnp.where(rows >= d, pltpu.roll(x, d, 0), fill)


def _shift_up(x, d, fill=0.0):
    n = x.shape[0]
    rows = lax.broadcasted_iota(jnp.int32, x.shape, 0)
    return jnp.where(rows < n - d, pltpu.roll(x, n - d, 0), fill)


SCAN_BLOCK = 128
SUBLANES = 8


def _scan_block(a, b, up):
    n = a.shape[0]
    d = 1
    while d < n:
        if d < SUBLANES:
            shift = _shift_up if up else _shift_down
            a, b = a * shift(a, d, 1.0), a * shift(b, d) + b
        elif up:
            a, b = (jnp.concatenate([a[:n - d] * a[d:], a[n - d:]], axis=0),
                    jnp.concatenate([a[:n - d] * b[d:] + b[:n - d], b[n - d:]], axis=0))
        else:
            a, b = (jnp.concatenate([a[:d], a[d:] * a[:n - d]], axis=0),
                    jnp.concatenate([b[:d], a[d:] * b[:n - d] + b[d:]], axis=0))
        d *= 2
    return a, b


def _scan(a, b, up):
    n = a.shape[0]
    size = min(SCAN_BLOCK, n)
    blocks = n // size
    out, carry = [None] * blocks, None
    for k in (reversed(range(blocks)) if up else range(blocks)):
        prod, h = _scan_block(a[k * size:(k + 1) * size], b[k * size:(k + 1) * size], up)
        if carry is not None:
            h = h + prod * carry
        carry = h[0:1] if up else h[size - 1:size]
        out[k] = h
    return jnp.concatenate(out, axis=0)


def _neg_expm1(y):
    u = jnp.exp(y)
    um1 = u - 1.0
    safe = jnp.where(um1 == 0.0, 1.0, jnp.log(u))
    near = jnp.where(um1 == 0.0, y, um1 * y / safe)
    return -jnp.where(y > -0.5, near, um1)


def _softplus(x):
    return jnp.maximum(x, 0.0) + jnp.log1p(jnp.exp(-jnp.abs(x)))


def _lru_forward(xc, cw, cb, wa, ba, wx, bx, lam):
    shifted = [_shift_down(xc, CONV_WIDTH - 1 - j) if j < CONV_WIDTH - 1 else xc for j in range(CONV_WIDTH)]
    xconv = cb
    for j in range(CONV_WIDTH):
        xconv = xconv + cw[j:j + 1, :] * shifted[j]
    xb = xconv.astype(MXU_DTYPE)
    r = _sigmoid(_dot(xb, wa.astype(MXU_DTYPE)) + ba)
    ig = _sigmoid(_dot(xb, wx.astype(MXU_DTYPE)) + bx)
    sp = _softplus(-lam)
    log_a = -LRU_C * r * sp
    a = jnp.exp(log_a)
    mult = jnp.sqrt(_neg_expm1(2.0 * log_a))
    hs = _scan(a, mult * ig * xconv, up=False)
    return dict(shifted=shifted, xconv=xconv, xb=xb, r=r, ig=ig, sp=sp, a=a, mult=mult, hs=hs)


def _col(width, off):
    return lambda s: pl.BlockSpec((s, width), lambda i: (0, off + i))


def _part(width, slot):
    return lambda s: pl.BlockSpec((None, s, width), lambda i: (slot, 0, i))


def _lru_specs(s):
    w = LANES
    return [_part(w, 0)(s), _part(w, 1)(s), pl.BlockSpec((CONV_WIDTH, w), lambda i: (0, i)),
            pl.BlockSpec((1, w), lambda i: (0, i)), pl.BlockSpec((None, w, w), lambda i: (i, 0, 0)),
            pl.BlockSpec((1, w), lambda i: (0, i)), pl.BlockSpec((None, w, w), lambda i: (i, 0, 0)),
            pl.BlockSpec((1, w), lambda i: (0, i)), pl.BlockSpec((1, w), lambda i: (0, i))]


def _lru_fwd(h, cw, cb, wa, ba, wx, bx, lam):
    s = h.shape[1]

    def body(xc_ref, cg_ref, cw_ref, cb_ref, wa_ref, ba_ref, wx_ref, bx_ref, lam_ref, o_ref):
        f = _lru_forward(xc_ref[...], cw_ref[...], cb_ref[...], wa_ref[...], ba_ref[...], wx_ref[...], bx_ref[...],
                         lam_ref[...])
        sil, _ = _silu_and_grad(cg_ref[...])
        o_ref[...] = (f['hs'] * sil).astype(o_ref.dtype)

    return pl.pallas_call(
        body, name="lru_fwd", grid=(C_HEADS,), in_specs=_lru_specs(s), out_specs=_col(LANES, 0)(s),
        out_shape=jax.ShapeDtypeStruct((s, C_WIDTH + D_WIDTH), MXU_DTYPE), compiler_params=_params(("parallel",)),
    )(h, h, cw, cb.reshape(1, -1), wa, ba.reshape(1, -1), wx, bx.reshape(1, -1), lam.reshape(1, -1))


def _lru_bwd(h, dmix, cw, cb, wa, ba, wx, bx, lam):
    s = h.shape[1]

    def body(xc_ref, cg_ref, cw_ref, cb_ref, wa_ref, ba_ref, wx_ref, bx_ref, lam_ref, dco_ref,
             dh_ref, dcw_ref, dcb_ref, dwa_ref, dba_ref, dwx_ref, dbx_ref, dlam_ref):
        dxc_ref, dcg_ref = dh_ref.at[0], dh_ref.at[1]
        cw, lam = cw_ref[...], lam_ref[...]
        wab, wxb = wa_ref[...].astype(MXU_DTYPE), wx_ref[...].astype(MXU_DTYPE)
        f = _lru_forward(xc_ref[...], cw, cb_ref[...], wa_ref[...], ba_ref[...], wx_ref[...], bx_ref[...], lam)
        sil, dsil = _silu_and_grad(cg_ref[...])
        dco = dco_ref[...]
        hs, a, mult, ig, r, xconv = f['hs'], f['a'], f['mult'], f['ig'], f['r'], f['xconv']
        dcg_ref[...] = (dco * hs * dsil).astype(dcg_ref.dtype)
        lamb = _scan(_shift_up(a, 1), dco * sil, up=True)
        da = lamb * _shift_down(hs, 1)
        d_ig = lamb * mult * xconv
        d_mult = lamb * ig * xconv
        dxconv = lamb * mult * ig
        d_log_a = da * a - d_mult * (a * a) / mult
        dlam_ref[...] = jnp.sum(d_log_a * r, axis=0, keepdims=True) * LRU_C * _sigmoid(-lam)
        dpa = d_log_a * (-LRU_C * f['sp']) * r * (1.0 - r)
        dpx = d_ig * ig * (1.0 - ig)
        dba_ref[...] = jnp.sum(dpa, axis=0, keepdims=True)
        dbx_ref[...] = jnp.sum(dpx, axis=0, keepdims=True)
        dpab, dpxb = dpa.astype(MXU_DTYPE), dpx.astype(MXU_DTYPE)
        dwa_ref[...] = _dot_tn(f['xb'], dpab)
        dwx_ref[...] = _dot_tn(f['xb'], dpxb)
        dxconv = dxconv + _dot_nt(dpab, wab) + _dot_nt(dpxb, wxb)
        dcb_ref[...] = jnp.sum(dxconv, axis=0, keepdims=True)
        dxc = jnp.zeros_like(dxconv)
        for j in range(CONV_WIDTH):
            dcw_ref[j:j + 1, :] = jnp.sum(dxconv * f['shifted'][j], axis=0, keepdims=True)
            back = _shift_up(dxconv, CONV_WIDTH - 1 - j) if j < CONV_WIDTH - 1 else dxconv
            dxc = dxc + cw[j:j + 1, :] * back
        dxc_ref[...] = dxc.astype(dxc_ref.dtype)

    w = LANES
    vec = pl.BlockSpec((1, w), lambda i: (0, i))
    mat = pl.BlockSpec((None, w, w), lambda i: (i, 0, 0))
    vshape = jax.ShapeDtypeStruct((1, C_WIDTH), F32)
    mshape = jax.ShapeDtypeStruct((C_HEADS, w, w), F32)
    return pl.pallas_call(
        body, name="lru_bwd", grid=(C_HEADS,), in_specs=_lru_specs(s) + [_col(w, 0)(s)],
        out_specs=[pl.BlockSpec((2, s, w), lambda i: (0, 0, i)), pl.BlockSpec((CONV_WIDTH, w), lambda i: (0, i)),
                   vec, mat, vec, mat, vec, vec],
        out_shape=[jax.ShapeDtypeStruct((N_CHIPS, s, C_WIDTH), MXU_DTYPE),
                   jax.ShapeDtypeStruct((CONV_WIDTH, C_WIDTH), F32), vshape, mshape, vshape, mshape, vshape, vshape],
        compiler_params=_params(("parallel",)),
    )(h, h, cw, cb.reshape(1, -1), wa, ba.reshape(1, -1), wx, bx.reshape(1, -1), lam.reshape(1, -1), dmix)


def _pool_window(g):
    return jnp.left_shift(2, g).astype(F32)


def _select_window(g, sums):
    out = sums[-1]
    for k in range(len(sums) - 2, -1, -1):
        out = jnp.where(g == k, sums[k], out)
    return out


def _window_sums(x, shift):
    sums, cur = [], x
    for k in range(D_GROUPS):
        cur = cur + shift(cur, 2 ** k)
        sums.append(cur)
    return sums


def _pool_specs(s):
    w = D_GROUP_DIM
    return [_part(w, 2)(s), _part(w, 3)(s),
            pl.BlockSpec((None, w, w), lambda i: (i, 0, 0)), pl.BlockSpec((1, w), lambda i: (0, i))]


def _pool_forward(xd, g):
    rows = lax.broadcasted_iota(jnp.int32, xd.shape, 0).astype(F32) + 1.0
    cnt = jnp.minimum(rows, _pool_window(g))
    pooled = _select_window(g, _window_sums(xd, _shift_down)) / cnt - xd
    return pooled, cnt


def _pool_fwd(h, mix, w_pool, d_scale):
    s = h.shape[1]

    def body(xd_ref, dg_ref, wp_ref, sc_ref, mix_ref, o_ref):
        pooled, _ = _pool_forward(xd_ref[...], pl.program_id(0))
        mixed = _dot(pooled.astype(MXU_DTYPE), wp_ref[...].astype(MXU_DTYPE))
        sil, _ = _silu_and_grad(dg_ref[...])
        o_ref[...] = (mixed * sc_ref[...] * sil).astype(o_ref.dtype)

    return pl.pallas_call(
        body, name="pool_fwd", grid=(D_GROUPS,), in_specs=_pool_specs(s) + [ANY],
        out_specs=_col(D_GROUP_DIM, C_WIDTH // D_GROUP_DIM)(s), out_shape=jax.ShapeDtypeStruct(mix.shape, mix.dtype),
        input_output_aliases={4: 0}, compiler_params=_params(("parallel",)),
    )(h, h, w_pool, d_scale.reshape(1, -1), mix)


def _pool_bwd(h, dmix, dh, w_pool, d_scale):
    s = h.shape[1]
    w = D_GROUP_DIM

    def body(xd_ref, dg_ref, wp_ref, sc_ref, ddo_ref, dh_in_ref, dh_ref, dwp_ref, dsc_ref):
        dxd_ref, ddg_ref = dh_ref.at[0], dh_ref.at[1]
        g = pl.program_id(0)
        pooled, cnt = _pool_forward(xd_ref[...], g)
        pb = pooled.astype(MXU_DTYPE)
        wpb = wp_ref[...].astype(MXU_DTYPE)
        mixed = _dot(pb, wpb)
        sil, dsil = _silu_and_grad(dg_ref[...])
        ddo = ddo_ref[...]
        sc = sc_ref[...]
        ddg_ref[...] = (ddo * mixed * sc * dsil).astype(ddg_ref.dtype)
        dms = ddo * sil
        dsc_ref[...] = jnp.sum(dms * mixed, axis=0, keepdims=True)
        dmb = (dms * sc).astype(MXU_DTYPE)
        dwp_ref[...] = _dot_tn(pb, dmb)
        dpooled = _dot_nt(dmb, wpb)
        dxd = _select_window(g, _window_sums(dpooled / cnt, _shift_up)) - dpooled
        dxd_ref[...] = dxd.astype(dxd_ref.dtype)

    return pl.pallas_call(
        body, name="pool_bwd", grid=(D_GROUPS,), in_specs=_pool_specs(s) + [_col(w, C_WIDTH // w)(s), ANY],
        out_specs=[pl.BlockSpec((2, s, w), lambda i: (1, 0, i)), pl.BlockSpec((None, w, w), lambda i: (i, 0, 0)),
                   pl.BlockSpec((1, w), lambda i: (0, i))],
        out_shape=[jax.ShapeDtypeStruct(dh.shape, dh.dtype),
                   jax.ShapeDtypeStruct((D_GROUPS, w, w), F32), jax.ShapeDtypeStruct((1, D_WIDTH), F32)],
        input_output_aliases={5: 0}, compiler_params=_params(("parallel",)),
    )(h, h, w_pool, d_scale.reshape(1, -1), dmix, dh)


def _rope_tables(positions):
    s = positions.shape[0]
    inv_freq = ROPE_THETA ** (-jnp.arange(0, ROT_DIM, 2, dtype=F32) / ROT_DIM)
    ang = positions.astype(F32)[:, None] * inv_freq
    c, sn = jnp.cos(ang), jnp.sin(ang)
    rest = B_HEAD_DIM - ROT_DIM
    c64 = jnp.concatenate([c, c, jnp.ones((s, rest), F32)], axis=1)
    s64 = jnp.concatenate([-sn, sn, jnp.zeros((s, rest), F32)], axis=1)
    return jnp.tile(c64, (1, LANES // B_HEAD_DIM)), jnp.tile(s64, (1, LANES // B_HEAD_DIM))


def _local_step(x, positions, target, w, projections_of, projection_grads, norm_grads_done):
    cq, sq = _rope_tables(positions)
    saved = []
    xin, xin_b = x, x
    for layer in range(DEPTH):
        j = layer // 2
        kind = 'even' if layer % 2 == 0 else 'odd'
        w_in, w_out = projections_of(layer, xin_b)
        h = _mm_in(xin_b, w_in, W_IN_TRANSPOSED[kind], "mm_" + kind + "_in")
        if kind == 'even':
            mix = _even_fwd(h, cq, sq, w['even_a_ln_g'][j], w['even_a_ln_b'][j], w['even_a_ws'][j],
                            w['even_a_bs'][j].T, w['even_b_sinks'][j])
        else:
            mix = _lru_fwd(h, w['odd_conv_w'][j], w['odd_conv_b'][j], w['odd_w_a'][j], w['odd_b_a'][j],
                           w['odd_w_x'][j], w['odd_b_x'][j], w['odd_lam'][j])
            mix = _pool_fwd(h, mix, w['odd_w_pool'][j], w['odd_d_scale'][j])
        last = layer == DEPTH - 1
        z, out, out_b = _mm_out_norm(mix, w_out, xin, w[kind + '_ln_g'][j], w[kind + '_ln_b'][j],
                                     "mm_out_norm_loss" if last else "mm_out_norm", target if last else None)
        saved.append((xin_b, h, mix, z, w_in, w_out))
        xin, xin_b = out, out_b

    dout, loss = xin, xin_b
    per_layer = {n: [None, None] for n in SMALL_NAMES}
    behind = loss
    for layer in reversed(range(DEPTH)):
        j = layer // 2
        kind = 'even' if layer % 2 == 0 else 'odd'
        xin_b, h, mix, z, w_in, w_out = saved[layer]
        dz, dz_b, dg, db = _ln_bwd(dout, z, w[kind + '_ln_g'][j], behind)
        per_layer[kind + '_ln_g'][j], per_layer[kind + '_ln_b'][j] = dg[0], db[0]
        dmix = _mm_dmix(dz_b, w_out, norm_grads_done(layer, dz_b), "mm_dmix")
        dw_out = _mm_dw_out(mix, dz_b, "mm_dw_out")
        if kind == 'even':
            dh, dlg, dlb, dws, dbst, dsink = _even_bwd(
                h, dmix, cq, sq, w['even_a_ln_g'][j], w['even_a_ln_b'][j], w['even_a_ws'][j], w['even_a_bs'][j].T,
                w['even_b_sinks'][j])
            for n, val in (('even_a_ln_g', dlg[0]), ('even_a_ln_b', dlb[0]), ('even_a_ws', dws),
                           ('even_a_bs', dbst.T), ('even_b_sinks', dsink[0])):
                per_layer[n][j] = val
        else:
            dh, dcw, dcb, dwa, dba, dwx, dbx, dlam = _lru_bwd(
                h, dmix, w['odd_conv_w'][j], w['odd_conv_b'][j], w['odd_w_a'][j], w['odd_b_a'][j], w['odd_w_x'][j],
                w['odd_b_x'][j], w['odd_lam'][j])
            dh, dwp, dsc = _pool_bwd(h, dmix, dh, w['odd_w_pool'][j], w['odd_d_scale'][j])
            for n, val in (('odd_conv_w', dcw), ('odd_conv_b', dcb[0]), ('odd_w_a', dwa), ('odd_b_a', dba[0]),
                           ('odd_w_x', dwx), ('odd_b_x', dbx[0]), ('odd_lam', dlam[0]), ('odd_w_pool', dwp),
                           ('odd_d_scale', dsc[0])):
                per_layer[n][j] = val
        dout = _mm_dx(dh, w_in, dz, W_IN_TRANSPOSED[kind], "mm_dx_" + kind)
        dw_in = _mm_dw_in(xin_b, dh, W_IN_TRANSPOSED[kind], "mm_dw_in_" + kind)
        behind = projection_grads(layer, dw_in, dw_out)
    return loss, dout, {n: jnp.stack(v) for n, v in per_layer.items()}


ANY = pl.BlockSpec(memory_space=pl.ANY)


def _where_am_i():
    return lax.axis_index("x"), lax.axis_index("y"), lax.axis_index("c")


def _other_chips(x, y):
    return [(1 - x, y), (x, 1 - y), (1 - x, 1 - y)]


PACKED_ROWS = 16


def _half_rows(rows, which):
    half = rows // 2
    assert half % PACKED_ROWS == 0, rows
    return pl.ds(pl.multiple_of(which * half, PACKED_ROWS), half)


def _sibling_swap(bufs, name, *, other_half_of_axis1=False, after=()):
    n, na = len(bufs), len(after)
    out_shapes = [(b.shape[0], b.shape[1] // 2, b.shape[2]) if other_half_of_axis1 else b.shape for b in bufs]

    def body(*refs):
        x_refs, out_refs, (send_sems, recv_sems) = refs[:n], refs[n + na:2 * n + na], refs[2 * n + na:]
        x, y, c = _where_am_i()
        copies = []
        for p in range(n):
            src = x_refs[p].at[:, _half_rows(bufs[p].shape[1], 1 - c), :] if other_half_of_axis1 else x_refs[p]
            copies.append(pltpu.make_async_remote_copy(
                src_ref=src, dst_ref=out_refs[p], send_sem=send_sems.at[p], recv_sem=recv_sems.at[p],
                device_id=(x, y, 1 - c), device_id_type=MESH))
            copies[-1].start()
        for cp in copies:
            cp.wait()

    return pl.pallas_call(
        body, name=name, in_specs=[ANY] * (n + na), out_specs=[ANY] * n,
        out_shape=[jax.ShapeDtypeStruct(s, b.dtype) for s, b in zip(out_shapes, bufs)],
        scratch_shapes=[pltpu.SemaphoreType.DMA((n,)), pltpu.SemaphoreType.DMA((n,))],
    )(*bufs, *after)


HBM = pl.BlockSpec(memory_space=pltpu.HBM)
SEM = pl.BlockSpec(memory_space=pltpu.SEMAPHORE)
IN_FLIGHT = pltpu.CompilerParams(has_side_effects=pltpu.SideEffectType.DATAFLOW_SIDE_EFFECTING)


def _peers(kind, x, y, c):
    return [(x, y, 1 - c)] if kind == 'swap' else [(px, py, c) for px, py in _other_chips(x, y)]


def _n_peers(kind):
    return 1 if kind == 'swap' else N_CHIPS - 1


def _land_shape(kind, a):
    if kind == 'gather':
        return (N_CHIPS,) + a.shape
    return (a.shape[0], a.shape[1] // 2, a.shape[2]) if kind == 'swap' else a.shape


def _ici_copy(kind, src_ref, land_ref, send_sems, recv_sems, k, peer, x, y, c):
    me = 2 * x + y
    if kind == 'gather':
        half = _half_rows(src_ref.shape[0], c)
        src, dst = src_ref.at[half, :], land_ref.at[me, half, :]
    elif kind == 'scatter':
        src, dst = src_ref.at[2 * peer[0] + peer[1]], land_ref.at[me]
    else:
        src, dst = src_ref.at[:, _half_rows(src_ref.shape[1], 1 - c), :], land_ref
    return pltpu.make_async_remote_copy(src_ref=src, dst_ref=dst, send_sem=send_sems.at[k], recv_sem=recv_sems.at[k],
                                        device_id=peer, device_id_type=MESH)


def _ici_start(kind, groups, name, after=()):
    flat = [a for g in groups for a in g]
    n, ng, np_, na = len(flat), len(groups), _n_peers(kind), len(after)
    land_shapes = [_land_shape(kind, a) for a in flat]

    def body(*refs):
        srcs, lands, sems, token = refs[:n], refs[n:2 * n], refs[2 * n + na:2 * n + na + 2 * ng], refs[-1]
        x, y, c = _where_am_i()
        p = 0
        for gi, g in enumerate(groups):
            for q in range(len(g)):
                for k, peer in enumerate(_peers(kind, x, y, c)):
                    _ici_copy(kind, srcs[p], lands[p], sems[2 * gi], sems[2 * gi + 1], np_ * q + k, peer, x, y,
                              c).start()
                p += 1
        token[...] = jnp.zeros_like(token)

    sem_shapes = [pltpu.SemaphoreType.DMA((np_ * len(g),)) for g in groups for _ in range(2)]
    outs = pl.pallas_call(
        body, name=name, in_specs=[HBM] * (2 * n) + [ANY] * na,
        out_specs=[SEM] * (2 * ng) + [HBM] * (2 * n) + [pl.BlockSpec(memory_space=pltpu.VMEM)],
        out_shape=sem_shapes + [pltpu.HBM(a.shape, a.dtype) for a in flat]
        + [pltpu.HBM(s, a.dtype) for s, a in zip(land_shapes, flat)] + [jax.ShapeDtypeStruct((8, LANES), F32)],
        input_output_aliases={i: 2 * ng + i for i in range(2 * n)}, compiler_params=IN_FLIGHT,
    )(*[pltpu.with_memory_space_constraint(a, pltpu.HBM) for a in flat],
      *[pltpu.with_memory_space_constraint(lax.empty(s, a.dtype), pltpu.HBM) for s, a in zip(land_shapes, flat)],
      *after)
    sems, srcs, lands, token = outs[:2 * ng], outs[2 * ng:2 * ng + n], outs[2 * ng + n:2 * ng + 2 * n], outs[-1]
    started, p = [], 0
    for gi, g in enumerate(groups):
        started.append((sems[2 * gi], sems[2 * gi + 1], srcs[p:p + len(g)], lands[p:p + len(g)]))
        p += len(g)
    return started, token


def _ici_wait(kind, started, after, name):
    send_sems, recv_sems, srcs, lands = started
    n, np_ = len(srcs), _n_peers(kind)
    after = list(after)

    def body(*refs):
        src_refs, land_refs, send_ref, recv_ref = refs[:n], refs[n:2 * n], refs[2 * n], refs[2 * n + 1]
        x, y, c = _where_am_i()
        for q in range(n):
            for k, peer in enumerate(_peers(kind, x, y, c)):
                cp = _ici_copy(kind, src_refs[q], land_refs[q], send_ref, recv_ref, np_ * q + k, peer, x, y, c)
                cp.wait_send()
                cp.wait_recv()

    outs = pl.pallas_call(
        body, name=name, in_specs=[HBM] * (2 * n) + [SEM, SEM] + [ANY] * len(after), out_specs=[HBM] * (2 * n),
        out_shape=[pltpu.HBM(a.shape, a.dtype) for a in list(srcs) + list(lands)],
        input_output_aliases={i: i for i in range(2 * n)}, compiler_params=IN_FLIGHT,
    )(*srcs, *lands, send_sems, recv_sems, *after)
    return outs[:n], outs[n:]


def _forward_to_sibling(lands, name):
    n = len(lands)

    def body(*refs):
        out_refs, (send_sems, recv_sems) = refs[n:2 * n], refs[2 * n:]
        x, y, c = _where_am_i()

        def copy(q, k, peer, half_of):
            rows = out_refs[q].at[2 * peer[0] + peer[1], _half_rows(lands[q].shape[1], half_of), :]
            return pltpu.make_async_remote_copy(
                src_ref=rows, dst_ref=rows, send_sem=send_sems.at[3 * q + k], recv_sem=recv_sems.at[3 * q + k],
                device_id=(x, y, 1 - c), device_id_type=MESH)

        sends = [copy(q, k, peer, c) for q in range(n) for k, peer in enumerate(_other_chips(x, y))]
        for cp in sends:
            cp.start()
        for q in range(n):
            for k, peer in enumerate(_other_chips(x, y)):
                copy(q, k, peer, 1 - c).wait_recv()
        for cp in sends:
            cp.wait_send()

    return pl.pallas_call(
        body, name=name, in_specs=[ANY] * n, out_specs=[ANY] * n,
        out_shape=[jax.ShapeDtypeStruct(a.shape, a.dtype) for a in lands],
        input_output_aliases={i: i for i in range(n)},
        scratch_shapes=[pltpu.SemaphoreType.DMA((3 * n,)), pltpu.SemaphoreType.DMA((3 * n,))],
    )(*lands)


def _with_own_slot(lands, own, chip):
    out = []
    for land, mine in zip(lands, own):
        mine = lax.dynamic_slice_in_dim(mine, chip, 1, axis=0) if mine.ndim == land.ndim else mine[None]
        out.append(lax.dynamic_update_slice(land, mine, (chip,) + (0,) * (land.ndim - 1)))
    return out


def _row_tile(rows, row_bytes, n_bufs, budget=24 * 1024 * 1024):
    best = None
    for t in range(PACKED_ROWS, rows + 1, PACKED_ROWS):
        if rows % t == 0 and 2 * n_bufs * t * row_bytes <= budget:
            best = t
    assert best is not None, (rows, row_bytes)
    return best


def _add_halves(full, recv, core, out_dtype, name):
    n, h, cols = recv.shape
    tr = _row_tile(h, cols * 4, 3)
    nt = h // tr

    def body(core_ref, a_ref, b_ref, o_ref):
        o_ref[...] = (a_ref[...].astype(F32) + b_ref[...].astype(F32)).astype(out_dtype)

    blk = (None, tr, cols)
    return pl.pallas_call(
        body, name=name,
        grid_spec=pltpu.PrefetchScalarGridSpec(
            num_scalar_prefetch=1, grid=(n, nt),
            in_specs=[pl.BlockSpec(blk, lambda j, i, core_ref: (j, core_ref[0] * nt + i, 0)),
                      pl.BlockSpec(blk, lambda j, i, core_ref: (j, i, 0))],
            out_specs=pl.BlockSpec(blk, lambda j, i, core_ref: (j, i, 0))),
        out_shape=jax.ShapeDtypeStruct((n, h, cols), out_dtype),
        compiler_params=_params(("parallel", "parallel")),
    )(core.reshape(1), full, recv)


def _add2(a, b, name):
    r, cols = a.shape
    tr = _row_tile(r, cols * 4, 3)

    def body(a_ref, b_ref, o_ref):
        o_ref[...] = a_ref[...] + b_ref[...]

    blk = pl.BlockSpec((tr, cols), lambda i: (i, 0))
    return pl.pallas_call(body, name=name, grid=(r // tr,), in_specs=[blk, blk], out_specs=blk,
                          out_shape=jax.ShapeDtypeStruct(a.shape, a.dtype), compiler_params=_params(("parallel",)))(a, b)


def _sum_slots(parts, name):
    n, h, cols = parts.shape
    tr = _row_tile(h, cols * 4, n + 1)

    def body(p_ref, o_ref):
        acc = p_ref[0].astype(F32)
        for k in range(1, n):
            acc = acc + p_ref[k].astype(F32)
        o_ref[...] = acc

    return pl.pallas_call(
        body, name=name, grid=(h // tr,), in_specs=[pl.BlockSpec((n, tr, cols), lambda i: (0, i, 0))],
        out_specs=pl.BlockSpec((tr, cols), lambda i: (i, 0)), out_shape=jax.ShapeDtypeStruct((h, cols), F32),
        compiler_params=_params(("parallel",)),
    )(parts)


def _adamw(w, g, m, v, name):
    shape = w.shape
    cols = shape[-1]
    rows = math.prod(shape[:-1])
    tr = rows
    if rows % 8 == 0:
        tr = 8
        for t in range(8, rows + 1, 8):
            if rows % t == 0 and t * cols * 4 <= 1536 * 1024:
                tr = t
    c1 = 1.0 - ADAM_B1 ** ADAM_STEP
    c2 = 1.0 - ADAM_B2 ** ADAM_STEP

    def body(w_ref, g_ref, m_ref, v_ref, d_ref, nm_ref, nv_ref):
        gg = g_ref[...]
        nm = ADAM_B1 * m_ref[...] + (1.0 - ADAM_B1) * gg
        nv = ADAM_B2 * v_ref[...] + (1.0 - ADAM_B2) * (gg * gg)
        d_ref[...] = -ADAM_LR * ((nm / c1) / (jnp.sqrt(nv / c2) + ADAM_EPS) + ADAM_WD * w_ref[...])
        nm_ref[...] = nm
        nv_ref[...] = nv

    blk = pl.BlockSpec((tr, cols), lambda i: (i, 0))
    flat = jax.ShapeDtypeStruct((rows, cols), F32)
    outs = pl.pallas_call(
        body, name=name, grid=(rows // tr,), in_specs=[blk] * 4, out_specs=[blk] * 3, out_shape=[flat] * 3,
        compiler_params=_params(("parallel",)),
    )(*[a.reshape(rows, cols) for a in (w, g, m, v)])
    return [o.reshape(shape) for o in outs]


def _adamw_update(w, g, m, v):
    c1 = 1.0 - ADAM_B1 ** ADAM_STEP
    c2 = 1.0 - ADAM_B2 ** ADAM_STEP
    nm = ADAM_B1 * m + (1.0 - ADAM_B1) * g
    nv = ADAM_B2 * v + (1.0 - ADAM_B2) * (g * g)
    return -ADAM_LR * ((nm / c1) / (jnp.sqrt(nv / c2) + ADAM_EPS) + ADAM_WD * w), nm, nv


def _adamw_projection(w, m, v, mine, theirs, core, name):
    _, r, cols = w.shape
    h = r // 2
    tr = _row_tile(h, cols * 4, 12, budget=40 * 1024 * 1024)
    nt = h // tr

    def body(core_ref, w_ref, m_ref, v_ref, a0, b0, a1, b1, g_ref, d_ref, nm_ref, nv_ref):
        layer, half = pl.program_id(0), pl.program_id(1)
        own = half == core_ref[0]
        g = jnp.where(layer == 0, jnp.where(own, a0[...], b0[...]), jnp.where(own, a1[...], b1[...]))
        g_ref[...] = g
        d_ref[...], nm_ref[...], nv_ref[...] = _adamw_update(w_ref[...], g, m_ref[...], v_ref[...])

    def piece(layer, own):
        def index(l, hh, i, core_ref):
            used = (l == layer) & ((hh == core_ref[0]) == own)
            return (jnp.where(used, i, 0), 0)
        return pl.BlockSpec((tr, cols), index)

    whole = pl.BlockSpec((None, tr, cols), lambda l, hh, i, core_ref: (l, hh * nt + i, 0))
    return pl.pallas_call(
        body, name=name,
        grid_spec=pltpu.PrefetchScalarGridSpec(
            num_scalar_prefetch=1, grid=(2, 2, nt),
            in_specs=[whole] * 3 + [piece(0, True), piece(0, False), piece(1, True), piece(1, False)],
            out_specs=[whole] * 4),
        out_shape=[jax.ShapeDtypeStruct(w.shape, F32)] * 4,
        compiler_params=_params(("arbitrary", "arbitrary", "arbitrary")),
    )(core.reshape(1), w, m, v, mine[0], theirs[0], mine[1], theirs[1])


def _chip_shape(name):
    shape = list(FULL_SHAPES[name])
    axis = BIG_SHARD_AXIS.get(name, SMALL_SHARD_AXIS.get(name))
    if axis is not None:
        shape[axis] //= N_CHIPS
    return tuple(shape)


def _round_up(n, k):
    return -(-n // k) * k


SHARDED_SMALL_WORDS = sum(math.prod(_chip_shape(n)) for n in SMALL_SHARDED)
SHARDED_SMALL_ROWS = _round_up(-(-SHARDED_SMALL_WORDS // PACK_COLS), 32)
SMALL_MATRICES = ['even_a_ws', 'odd_w_a', 'odd_w_x', 'odd_w_pool']
SMALL_VECTORS = [n for n in SMALL_NAMES if n not in SMALL_MATRICES]
VECTOR_ROWS = _round_up(-(-(sum(math.prod(FULL_SHAPES[n]) for n in SMALL_VECTORS) + 1) // PACK_COLS), 32)


def _pack_rows(flat, rows):
    return jnp.pad(flat, (0, rows * PACK_COLS - flat.shape[0])).reshape(rows, PACK_COLS)


def _pack_small_sharded(shards):
    return _pack_rows(jnp.concatenate([shards[n].reshape(-1) for n in SMALL_SHARDED]), SHARDED_SMALL_ROWS)


def _unpack_small_sharded(gathered_small):
    out = {}
    words = gathered_small.reshape(N_CHIPS, -1)
    w0 = 0
    for n in SMALL_SHARDED:
        size = math.prod(_chip_shape(n))
        blocks = words[:, w0:w0 + size].reshape((N_CHIPS,) + _chip_shape(n))
        out[n] = jnp.concatenate([blocks[j] for j in range(N_CHIPS)], axis=SMALL_SHARD_AXIS[n])
        w0 += size
    return out


def _small_grad_parts(grads, loss):
    words = [grads[n].reshape(-1) for n in SMALL_VECTORS] + [loss.reshape(-1)]
    return ([_pack_rows(jnp.concatenate(words), VECTOR_ROWS)]
            + [grads[n].reshape(-1, FULL_SHAPES[n][-1]) for n in SMALL_MATRICES])


def _own_block(name, full, chip):
    if name not in SMALL_SHARD_AXIS:
        return full
    axis = SMALL_SHARD_AXIS[name]
    width = FULL_SHAPES[name][axis] // N_CHIPS
    return lax.dynamic_slice_in_dim(full, chip * width, width, axis)


def _from_small_grad_parts(totals, chip):
    flat = totals[0].reshape(-1)
    out = {}
    w0 = 0
    for n in SMALL_VECTORS:
        size = math.prod(FULL_SHAPES[n])
        out[n] = _own_block(n, flat[w0:w0 + size].reshape(FULL_SHAPES[n]), chip)
        w0 += size
    for n, total in zip(SMALL_MATRICES, totals[1:]):
        out[n] = _own_block(n, total.reshape(FULL_SHAPES[n]), chip)
    return out, flat[w0]


def kernel(x, positions, even_w_in, even_a_ln_g, even_a_ln_b, even_a_ws, even_a_bs, even_b_sinks, even_w_out, even_ln_g, even_ln_b, odd_w_in, odd_conv_w, odd_conv_b, odd_w_a, odd_b_a, odd_w_x, odd_b_x, odd_lam, odd_w_pool, odd_d_scale, odd_w_out, odd_ln_g, odd_ln_b, loss_target, m_even_w_in, m_even_a_ln_g, m_even_a_ln_b, m_even_a_ws, m_even_a_bs, m_even_b_sinks, m_even_w_out, m_even_ln_g, m_even_ln_b, m_odd_w_in, m_odd_conv_w, m_odd_conv_b, m_odd_w_a, m_odd_b_a, m_odd_w_x, m_odd_b_x, m_odd_lam, m_odd_w_pool, m_odd_d_scale, m_odd_w_out, m_odd_ln_g, m_odd_ln_b, v_even_w_in, v_even_a_ln_g, v_even_a_ln_b, v_even_a_ws, v_even_a_bs, v_even_b_sinks, v_even_w_out, v_even_ln_g, v_even_ln_b, v_odd_w_in, v_odd_conv_w, v_odd_conv_b, v_odd_w_a, v_odd_b_a, v_odd_w_x, v_odd_b_x, v_odd_lam, v_odd_w_pool, v_odd_d_scale, v_odd_w_out, v_odd_ln_g, v_odd_ln_b):
    local = dict(zip(WEIGHT_NAMES, (even_w_in, even_a_ln_g, even_a_ln_b, even_a_ws, even_a_bs, even_b_sinks, even_w_out, even_ln_g, even_ln_b, odd_w_in, odd_conv_w, odd_conv_b, odd_w_a, odd_b_a, odd_w_x, odd_b_x, odd_lam, odd_w_pool, odd_d_scale, odd_w_out, odd_ln_g, odd_ln_b)))
    mom = dict(zip(WEIGHT_NAMES, (m_even_w_in, m_even_a_ln_g, m_even_a_ln_b, m_even_a_ws, m_even_a_bs, m_even_b_sinks, m_even_w_out, m_even_ln_g, m_even_ln_b, m_odd_w_in, m_odd_conv_w, m_odd_conv_b, m_odd_w_a, m_odd_b_a, m_odd_w_x, m_odd_b_x, m_odd_lam, m_odd_w_pool, m_odd_d_scale, m_odd_w_out, m_odd_ln_g, m_odd_ln_b)))
    vel = dict(zip(WEIGHT_NAMES, (v_even_w_in, v_even_a_ln_g, v_even_a_ln_b, v_even_a_ws, v_even_a_bs, v_even_b_sinks, v_even_w_out, v_even_ln_g, v_even_ln_b, v_odd_w_in, v_odd_conv_w, v_odd_conv_b, v_odd_w_a, v_odd_b_a, v_odd_w_x, v_odd_b_x, v_odd_lam, v_odd_w_pool, v_odd_d_scale, v_odd_w_out, v_odd_ln_g, v_odd_ln_b)))
    mx, my, core = _where_am_i()
    chip = 2 * mx + my

    stored = lambda n, a: jnp.swapaxes(a, 1, 2) if n == 'even_w_in' and W_IN_TRANSPOSED['even'] else a
    names_of = lambda layer: [('even' if layer % 2 == 0 else 'odd') + s for s in ('_w_in', '_w_out')]

    shards = [[stored(n, local[n])[layer // 2].astype(MXU_DTYPE) for n in names_of(layer)] for layer in range(DEPTH)]
    shards[0].insert(0, _pack_small_sharded(local))
    gathers, first_token = _ici_start('gather', shards[:1], "gather_start_first")
    later, gather_token = _ici_start('gather', shards[1:], "gather_start", after=[first_token])
    gathers += later

    def gathered(layer, *after):
        own, lands = _ici_wait('gather', gathers[layer], after, "gather_wait_%d" % layer)
        return _with_own_slot(_forward_to_sibling(lands, "gather_forward_%d" % layer), own, chip)

    first = gathered(0, first_token, gather_token)
    weights = dict(local)
    weights.update(_unpack_small_sharded(first[0]))
    projections_of = lambda layer, after: first[1:] if layer == 0 else gathered(layer, after)

    swapping, in_flight, mine, started = [], [], {}, {}

    def land(*after):
        layer, scatter = in_flight.pop()
        parts, lands = _ici_wait('scatter', scatter, after, "scatter_wait_%d" % layer)
        for n, arrived in zip(names_of(layer), _with_own_slot(lands, parts, chip)):
            mine[n, layer // 2] = _sum_slots(arrived, "big_sum_%s_%d" % (n, layer // 2))

    def projection_grads(layer, dw_in, dw_out):
        if in_flight:
            land(dw_in)
        (swap,), token = _ici_start('swap', [[dw_in, dw_out]], "presum_start_%d" % layer)
        swapping.append((layer, swap))
        return presum_and_scatter(token) if layer == 0 else token

    def presum_and_scatter(*after):
        layer, swap = swapping.pop()
        partial, from_sibling = _ici_wait('swap', swap, after, "presum_wait_%d" % layer)
        chip_part = [_add_halves(g, r, core, MXU_DTYPE, "big_presum_add_%s_%d" % (n, layer // 2))
                     for g, r, n in zip(partial, from_sibling, names_of(layer))]
        (scatter,), started[layer] = _ici_start('scatter', [chip_part], "scatter_start_%d" % layer)
        in_flight.append((layer, scatter))
        return started[layer]

    norm_grads_done = lambda layer, dz: presum_and_scatter(dz) if swapping else dz
    loss, grad_x, grads = _local_step(x[0], positions[0], loss_target[0], weights, projections_of, projection_grads,
                                      norm_grads_done)

    small = _small_grad_parts(grads, loss)
    chip_sums = [_add2(a, b, "small_presum_add_%d" % i)
                 for i, (a, b) in enumerate(zip(small, _sibling_swap(small, "small_swap", after=[started[0]])))]
    (small_gather,), small_token = _ici_start('gather', [chip_sums], "small_gather_start")

    reduced, deltas, new_m, new_v = {}, {}, {}, {}

    def adamw_projections(kind, after):
        names = [kind + '_w_in', kind + '_w_out']
        pieces = [(n, l) for n in names for l in range(2)]
        theirs = dict(zip(pieces, _sibling_swap([mine[p] for p in pieces], "big_join_" + kind, after=after)))
        done = []
        for n in names:
            outs = _adamw_projection(stored(n, local[n]), stored(n, mom[n]), stored(n, vel[n]),
                                     [mine[n, l] for l in range(2)], [theirs[n, l] for l in range(2)], core,
                                     "adamw_" + n)
            reduced[n], deltas[n], new_m[n], new_v[n] = [stored(n, o) for o in outs]
            done.append(outs[-1])
        return done

    done = adamw_projections('odd', [small_token])
    land(grad_x, *done)
    done = adamw_projections('even', [])

    own, lands = _ici_wait('gather', small_gather, done, "small_gather_wait")
    of_chips = _with_own_slot(_forward_to_sibling(lands, "small_gather_forward"), own, chip)
    small_totals, total_loss = _from_small_grad_parts(
        [_sum_slots(g, "small_sum_%d" % i) for i, g in enumerate(of_chips)], chip)
    reduced.update(small_totals)
    for n in SMALL_NAMES:
        deltas[n], new_m[n], new_v[n] = _adamw(local[n], reduced[n], mom[n], vel[n], "adamw_" + n)

    return (total_loss, grad_x[None], *[reduced[n] for n in WEIGHT_NAMES], *[deltas[n] for n in WEIGHT_NAMES],
            *[new_m[n] for n in WEIGHT_NAMES], *[new_v[n] for n in WEIGHT_NAMES])
```

```python
import math

import jax
import jax.numpy as jnp
from jax import lax
from jax.experimental import pallas as pl
from jax.experimental.pallas import tpu as pltpu

F32 = jnp.float32
BF16 = jnp.bfloat16
MXU_DTYPE = jnp.bfloat16

D_MODEL = 2048
DEPTH = 4
A_WIDTH = 1024
A_GROUPS = 8
CHUNK = 128
B_HEAD_DIM = 64
B_Q_HEADS = 16
B_KV_HEADS = 2
WINDOW = 128
ROT_DIM = 16
ROPE_THETA = 500000.0
C_WIDTH = 1024
C_HEADS = 8
CONV_WIDTH = 4
LRU_C = 8.0
D_WIDTH = 1024
D_GROUPS = 4
D_GROUP_DIM = 256
EVEN_IN = 5376
ODD_IN = 4096
DN_ALPHA = (2 * DEPTH) ** 0.25
LN_EPS = 1e-5
ATTN_SCALE = B_HEAD_DIM ** -0.5
NEG_BIG = -1e30

ADAM_LR = 0.001
ADAM_B1 = 0.9
ADAM_B2 = 0.999
ADAM_EPS = 1e-08
ADAM_WD = 0.01
ADAM_STEP = 10

LANES = 128
VMEM_LIMIT = 56 * 1024 * 1024
N_CHIPS = 4
MESH = pl.DeviceIdType.MESH

WEIGHT_NAMES = ['even_w_in', 'even_a_ln_g', 'even_a_ln_b', 'even_a_ws', 'even_a_bs', 'even_b_sinks', 'even_w_out',
                'even_ln_g', 'even_ln_b', 'odd_w_in', 'odd_conv_w', 'odd_conv_b', 'odd_w_a', 'odd_b_a', 'odd_w_x',
                'odd_b_x', 'odd_lam', 'odd_w_pool', 'odd_d_scale', 'odd_w_out', 'odd_ln_g', 'odd_ln_b']
BIG_SHARD_AXIS = {'even_w_in': 2, 'even_w_out': 1, 'odd_w_in': 2, 'odd_w_out': 1}
SMALL_SHARD_AXIS = {'odd_conv_w': 2, 'odd_conv_b': 1, 'odd_b_a': 1, 'odd_b_x': 1, 'odd_lam': 1, 'odd_w_pool': 2,
                    'odd_d_scale': 1, 'odd_ln_g': 1, 'odd_ln_b': 1}
FULL_SHAPES = {
    'even_w_in': (2, 2048, 5376), 'even_a_ln_g': (2, 1024), 'even_a_ln_b': (2, 1024), 'even_a_ws': (2, 8, 128, 128),
    'even_a_bs': (2, 8, 128), 'even_b_sinks': (2, 16), 'even_w_out': (2, 2048, 2048), 'even_ln_g': (2, 2048),
    'even_ln_b': (2, 2048), 'odd_w_in': (2, 2048, 4096), 'odd_conv_w': (2, 4, 1024), 'odd_conv_b': (2, 1024),
    'odd_w_a': (2, 8, 128, 128), 'odd_b_a': (2, 1024), 'odd_w_x': (2, 8, 128, 128), 'odd_b_x': (2, 1024),
    'odd_lam': (2, 1024), 'odd_w_pool': (2, 4, 256, 256), 'odd_d_scale': (2, 1024), 'odd_w_out': (2, 2048, 2048),
    'odd_ln_g': (2, 2048), 'odd_ln_b': (2, 2048)}
BIG_NAMES = ['even_w_in', 'even_w_out', 'odd_w_in', 'odd_w_out']
SMALL_SHARDED = ['odd_conv_w', 'odd_conv_b', 'odd_b_a', 'odd_b_x', 'odd_lam', 'odd_d_scale', 'odd_ln_g', 'odd_ln_b',
                 'odd_w_pool']
SMALL_NAMES = [n for n in WEIGHT_NAMES if n not in BIG_NAMES]
W_IN_TRANSPOSED = {'even': True, 'odd': False}
PACK_COLS = 1024


def _params(sem):
    return pltpu.CompilerParams(dimension_semantics=sem, vmem_limit_bytes=VMEM_LIMIT)


def _dot(a, b):
    return lax.dot_general(a, b, (((1,), (0,)), ((), ())), preferred_element_type=F32)


def _dot_nt(a, b):
    return lax.dot_general(a, b, (((1,), (1,)), ((), ())), preferred_element_type=F32)


def _dot_tn(a, b):
    return lax.dot_general(a, b, (((0,), (0,)), ((), ())), preferred_element_type=F32)


def _sigmoid(x):
    return 0.5 * jnp.tanh(0.5 * x) + 0.5


def _silu_and_grad(x):
    s = _sigmoid(x)
    return x * s, s * (1.0 + x * (1.0 - s))


def _mm_call(body, name, grid, in_specs, out_spec, out_shape, args):
    return pl.pallas_call(body, name=name, grid=grid, in_specs=in_specs, out_specs=out_spec, out_shape=out_shape,
                          compiler_params=_params(("parallel",) * len(grid)))(*args)


def _resident(block, index_map):
    return pl.BlockSpec(block, index_map, pipeline_mode=pl.Buffered(1))


ROWS_OF_WT = 768


def _mm_in(x, wg, transposed, name):
    s, k = x.shape
    x_spec = _resident((s, k), lambda j: (0, 0))

    def body(x_ref, w_ref, o_ref):
        o_ref[...] = (_dot_nt if transposed else _dot)(x_ref[...], w_ref[...])

    if transposed:
        wt = wg.reshape(-1, k)
        n, tn = wt.shape[0], ROWS_OF_WT
        return _mm_call(body, name, (n // tn,), [x_spec, pl.BlockSpec((tn, k), lambda j: (j, 0))],
                        pl.BlockSpec((s, tn), lambda j: (0, j)), jax.ShapeDtypeStruct((s, n), F32), (x, wt))
    nc = wg.shape[2]
    return _mm_call(body, name, (N_CHIPS,), [x_spec, pl.BlockSpec((None, k, nc), lambda j: (j, 0, 0))],
                    pl.BlockSpec((None, s, nc), lambda j: (j, 0, 0)),
                    jax.ShapeDtypeStruct((N_CHIPS, s, nc), F32), (x, wg))


def _mm_dx(dh, wg, dz, transposed, name):
    s, d = dz.shape
    tm, tn = min(1024, s), 512
    res = pl.BlockSpec((tm, tn), lambda i, j: (i, j))
    if transposed:
        wt = wg.reshape(-1, d)
        n = wt.shape[0]

        def body(a_ref, w_ref, r_ref, o_ref):
            o_ref[...] = DN_ALPHA * r_ref[...] + _dot(a_ref[...], w_ref[...])

        return _mm_call(body, name, (s // tm, d // tn),
                        [pl.BlockSpec((tm, n), lambda i, j: (i, 0)), pl.BlockSpec((n, tn), lambda i, j: (0, j)), res],
                        res, jax.ShapeDtypeStruct((s, d), F32), (dh, wt, dz))
    nc = dh.shape[2]

    def body(a_ref, w_ref, r_ref, o_ref):
        acc = DN_ALPHA * r_ref[...]
        for j in range(N_CHIPS):
            acc = acc + _dot_nt(a_ref[j], w_ref[j])
        o_ref[...] = acc

    return _mm_call(body, name, (s // tm, d // tn),
                    [pl.BlockSpec((N_CHIPS, tm, nc), lambda i, j: (0, i, 0)),
                     pl.BlockSpec((N_CHIPS, tn, nc), lambda i, j: (0, j, 0)), res],
                    res, jax.ShapeDtypeStruct((s, d), F32), (dh, wg, dz))


def _mm_dw_in(x, dh, transposed, name):
    s, d = x.shape
    x_spec = _resident((s, d), lambda j: (0, 0))

    def body(a_ref, b_ref, o_ref):
        o_ref[...] = _dot_tn(a_ref[...], b_ref[...]).astype(o_ref.dtype)

    if transposed:
        n, tn = dh.shape[1], ROWS_OF_WT
        flat = _mm_call(body, name, (n // tn,), [pl.BlockSpec((s, tn), lambda j: (0, j)), x_spec],
                        pl.BlockSpec((tn, d), lambda j: (j, 0)), jax.ShapeDtypeStruct((n, d), MXU_DTYPE), (dh, x))
        return flat.reshape(N_CHIPS, n // N_CHIPS, d)
    nc = dh.shape[2]
    return _mm_call(body, name, (N_CHIPS,), [x_spec, pl.BlockSpec((None, s, nc), lambda j: (j, 0, 0))],
                    pl.BlockSpec((None, d, nc), lambda j: (j, 0, 0)),
                    jax.ShapeDtypeStruct((N_CHIPS, d, nc), MXU_DTYPE), (x, dh))


def _mm_out_norm(mix, wg, x, g, b, name, target=None):
    s, k = mix.shape
    _, kc, d = wg.shape
    tm = min(512, s)

    def normed(a_ref, w_ref, x_ref, g_ref, b_ref, z_ref):
        z = DN_ALPHA * x_ref[...] + _dot(a_ref[...], w_ref[...].reshape(N_CHIPS * kc, d))
        mu = jnp.mean(z, axis=1, keepdims=True)
        zc = z - mu
        var = jnp.mean(zc * zc, axis=1, keepdims=True)
        z_ref[...] = z
        return zc * lax.rsqrt(var + LN_EPS) * g_ref[...] + b_ref[...]

    def body(a_ref, w_ref, x_ref, g_ref, b_ref, z_ref, o_ref, ob_ref):
        o = normed(a_ref, w_ref, x_ref, g_ref, b_ref, z_ref)
        o_ref[...] = o
        ob_ref[...] = o.astype(ob_ref.dtype)

    def body_with_loss(a_ref, w_ref, x_ref, g_ref, b_ref, t_ref, z_ref, do_ref, l_ref):
        @pl.when(pl.program_id(0) == 0)
        def _():
            l_ref[...] = jnp.zeros_like(l_ref)

        e = normed(a_ref, w_ref, x_ref, g_ref, b_ref, z_ref) - t_ref[...]
        do_ref[...] = e * (1.0 / d)
        l_ref[...] += (0.5 / d) * jnp.sum(jnp.sum(e * e, axis=1, keepdims=True), axis=0, keepdims=True)

    row = pl.BlockSpec((tm, d), lambda i: (i, 0))
    vec = pl.BlockSpec((1, d), lambda i: (0, 0))
    in_specs = [pl.BlockSpec((tm, k), lambda i: (i, 0)), _resident((N_CHIPS, kc, d), lambda i: (0, 0, 0)), row, vec,
                vec]
    args = (mix, wg, x, g.reshape(1, d), b.reshape(1, d))
    f32_rows = jax.ShapeDtypeStruct((s, d), F32)
    if target is None:
        return pl.pallas_call(
            body, name=name, grid=(s // tm,), in_specs=in_specs, out_specs=[row, row, row],
            out_shape=[f32_rows, f32_rows, jax.ShapeDtypeStruct((s, d), MXU_DTYPE)],
            compiler_params=_params(("parallel",)))(*args)
    return pl.pallas_call(
        body_with_loss, name=name, grid=(s // tm,), in_specs=in_specs + [row],
        out_specs=[row, row, pl.BlockSpec((1, 1), lambda i: (0, 0))],
        out_shape=[f32_rows, f32_rows, jax.ShapeDtypeStruct((1, 1), F32)],
        compiler_params=_params(("arbitrary",)))(*args, target)


def _mm_dmix(dz, wg, after, name):
    s, d = dz.shape
    kc = wg.shape[1]

    def body(a_ref, w_ref, after_ref, o_ref):
        o_ref[...] = _dot_nt(a_ref[...], w_ref[...])

    return _mm_call(body, name, (N_CHIPS,),
                    [_resident((s, d), lambda j: (0, 0)), pl.BlockSpec((None, kc, d), lambda j: (j, 0, 0)), ANY],
                    pl.BlockSpec((s, kc), lambda j: (0, j)), jax.ShapeDtypeStruct((s, N_CHIPS * kc), F32),
                    (dz, wg, after))


def _mm_dw_out(mix, dz, name):
    s, k = mix.shape
    d = dz.shape[1]
    kc = k // N_CHIPS

    def body(a_ref, b_ref, o_ref):
        o_ref[...] = _dot_tn(a_ref[...], b_ref[...]).astype(o_ref.dtype)

    return _mm_call(body, name, (N_CHIPS,),
                    [pl.BlockSpec((s, kc), lambda i: (0, i)), _resident((s, d), lambda i: (0, 0))],
                    pl.BlockSpec((None, kc, d), lambda i: (i, 0, 0)),
                    jax.ShapeDtypeStruct((N_CHIPS, kc, d), MXU_DTYPE), (mix, dz))


def _ln_bwd(dout, z, g, after):
    s, d = z.shape
    tr = min(256, s)

    def body(do_ref, z_ref, g_ref, after_ref, dz_ref, dzb_ref, dg_ref, db_ref):
        @pl.when(pl.program_id(0) == 0)
        def _():
            dg_ref[...] = jnp.zeros_like(dg_ref)
            db_ref[...] = jnp.zeros_like(db_ref)

        zz = z_ref[...]
        do = do_ref[...]
        mu = jnp.mean(zz, axis=1, keepdims=True)
        zc = zz - mu
        var = jnp.mean(zc * zc, axis=1, keepdims=True)
        rstd = lax.rsqrt(var + LN_EPS)
        zh = zc * rstd
        dg_ref[...] += jnp.sum(do * zh, axis=0, keepdims=True)
        db_ref[...] += jnp.sum(do, axis=0, keepdims=True)
        dzh = do * g_ref[...]
        dz = rstd * (dzh - jnp.mean(dzh, axis=1, keepdims=True) - zh * jnp.mean(dzh * zh, axis=1, keepdims=True))
        dz_ref[...] = dz
        dzb_ref[...] = dz.astype(MXU_DTYPE)

    row = pl.BlockSpec((tr, d), lambda i: (i, 0))
    vec = pl.BlockSpec((1, d), lambda i: (0, 0))
    return pl.pallas_call(
        body, name="ln_bwd", grid=(s // tr,), in_specs=[row, row, vec, ANY], out_specs=[row, row, vec, vec],
        out_shape=[jax.ShapeDtypeStruct((s, d), F32), jax.ShapeDtypeStruct((s, d), MXU_DTYPE),
                   jax.ShapeDtypeStruct((1, d), F32), jax.ShapeDtypeStruct((1, d), F32)],
        compiler_params=_params(("arbitrary",)),
    )(dout, z, g.reshape(1, d), after)


Q0, K0, V0, BG0 = 3072, 4096, 4224, 4352
T = CHUNK


def _lane_ids(width):
    return lax.broadcasted_iota(jnp.int32, (T, width), 1)


def _rope_swap(x):
    w = x.shape[1]
    l64 = _lane_ids(w) % B_HEAD_DIM
    half = ROT_DIM // 2
    return jnp.where(l64 < half, pltpu.roll(x, w - half, 1), jnp.where(l64 < ROT_DIM, pltpu.roll(x, half, 1), 0.0))


def _rope(x, c, s):
    return x * c + _rope_swap(x) * s


def _rope_bwd(d, c, s):
    return d * c + _rope_swap(d * s)


def _tril():
    return lax.broadcasted_iota(jnp.int32, (T, T), 0) >= lax.broadcasted_iota(jnp.int32, (T, T), 1)


def _band_mask(has_prev):
    qi = lax.broadcasted_iota(jnp.int32, (T, 2 * T), 0)
    kj = lax.broadcasted_iota(jnp.int32, (T, 2 * T), 1)
    return (kj > qi) & (kj <= qi + WINDOW) & ((kj >= WINDOW) | has_prev)


def _a_norm(v, g, b):
    mu = jnp.mean(v, axis=1, keepdims=True)
    vc = v - mu
    var = jnp.mean(vc * vc, axis=1, keepdims=True)
    rstd = lax.rsqrt(var + LN_EPS)
    vh = vc * rstd
    return vh, rstd, vh * g + b


def _kv_operands(hk, k_rot, kp_rot, vv, vp):
    lo = _lane_ids(LANES) < B_HEAD_DIM
    mine = lo if hk == 0 else jnp.logical_not(lo)
    both = lambda t: jnp.where(mine, t, 0.0) + pltpu.roll(jnp.where(mine, t, 0.0), B_HEAD_DIM, 1)
    k2 = jnp.concatenate([both(kp_rot), both(k_rot)], axis=0)
    v2 = jnp.concatenate([both(vp), both(vv)], axis=0)
    return k2.astype(MXU_DTYPE), v2.astype(MXU_DTYPE)


GROUP_HEADS = B_Q_HEADS // B_KV_HEADS


def _pair_columns(hk):
    return [(hk * GROUP_HEADS // 2 + pp) * LANES for pp in range(GROUP_HEADS // 2)]


def _stacked_heads(pairs):
    lo = _lane_ids(LANES) < B_HEAD_DIM
    rows = []
    for tile in pairs:
        rows += [jnp.where(lo, tile, 0.0), jnp.where(lo, 0.0, tile)]
    return jnp.concatenate(rows, axis=0).astype(MXU_DTYPE)


def _pair_of(stacked, pp):
    lo = _lane_ids(LANES) < B_HEAD_DIM
    return jnp.where(lo, stacked[2 * pp * T:(2 * pp + 1) * T], stacked[(2 * pp + 1) * T:(2 * pp + 2) * T])


def _softmax_with_sink(qm, k2, valid, sink_ref, head0):
    scores = (_dot_nt(qm, k2) * ATTN_SCALE).reshape(GROUP_HEADS, T, 2 * T)
    s = jnp.where(valid[None], scores, NEG_BIG).reshape(GROUP_HEADS * T, 2 * T)
    sink = jnp.concatenate([jnp.broadcast_to(sink_ref[:, head0 + h:head0 + h + 1], (T, 1))
                            for h in range(GROUP_HEADS)], axis=0)
    m = jnp.maximum(jnp.max(s, axis=1, keepdims=True), sink)
    ex = jnp.exp(s - m)
    es = jnp.exp(sink - m)
    inv = 1.0 / (jnp.sum(ex, axis=1, keepdims=True) + es)
    return ex * inv, es * inv


def _even_specs(s):
    nb = s // T
    prev = lambda r: lambda i: jnp.maximum(r(i) - 1, 0)
    return nb, {
        'h': lambda r: pl.BlockSpec((T, EVEN_IN), lambda i: (r(i), 0)),
        'kprev': lambda r: pl.BlockSpec((T, LANES), lambda i: (prev(r)(i), K0 // LANES)),
        'vprev': lambda r: pl.BlockSpec((T, LANES), lambda i: (prev(r)(i), V0 // LANES)),
        'tab': lambda r: pl.BlockSpec((T, LANES), lambda i: (r(i), 0)),
        'tabprev': lambda r: pl.BlockSpec((T, LANES), lambda i: (prev(r)(i), 0)),
    }


def _full(shape):
    nd = len(shape)
    return pl.BlockSpec(shape, lambda i: (0,) * nd)


def _even_fwd(h, cq, sq, ln_g, ln_b, ws, bs_t, sinks):
    s = h.shape[0]
    nb, sp = _even_specs(s)

    def body(h_ref, kp_ref, vp_ref, cq_ref, sq_ref, ckp_ref, skp_ref, g_ref, b_ref, ws_ref, bst_ref, sink_ref, mix_ref):
        i = pl.program_id(0)
        tril = _tril()
        _, _, vn = _a_norm(h_ref[:, 1024:2048], g_ref[...], b_ref[...])
        vnb = vn.astype(MXU_DTYPE)
        for g in range(A_GROUPS):
            c0, c1 = g * LANES, (g + 1) * LANES
            wg = jnp.where(tril, ws_ref[g], 0.0).astype(MXU_DTYPE)
            mixed = _dot(wg, vnb[:, c0:c1]) + bst_ref[:, g:g + 1]
            sil, _ = _silu_and_grad(h_ref[:, 2048 + c0:2048 + c1])
            mix_ref[:, c0:c1] = (h_ref[:, c0:c1] * mixed * sil).astype(mix_ref.dtype)

        cq, sq = cq_ref[...], sq_ref[...]
        k_rot = _rope(h_ref[:, K0:K0 + LANES], cq, sq)
        kp_rot = _rope(kp_ref[...], ckp_ref[...], skp_ref[...])
        valid = _band_mask(i > 0)
        for hk in range(B_KV_HEADS):
            k2, v2 = _kv_operands(hk, k_rot, kp_rot, h_ref[:, V0:V0 + LANES], vp_ref[...])
            qm = _stacked_heads([_rope(h_ref[:, Q0 + c0:Q0 + c0 + LANES], cq, sq) for c0 in _pair_columns(hk)])
            p, _ = _softmax_with_sink(qm, k2, valid, sink_ref, GROUP_HEADS * hk)
            out = _dot(p.astype(MXU_DTYPE), v2)
            for pp, c0 in enumerate(_pair_columns(hk)):
                sil, _ = _silu_and_grad(h_ref[:, BG0 + c0:BG0 + c0 + LANES])
                mix_ref[:, A_WIDTH + c0:A_WIDTH + c0 + LANES] = (_pair_of(out, pp) * sil).astype(mix_ref.dtype)

    ident = lambda i: i
    return pl.pallas_call(
        body, name="even_fwd", grid=(nb,),
        in_specs=[sp['h'](ident), sp['kprev'](ident), sp['vprev'](ident), sp['tab'](ident), sp['tab'](ident),
                  sp['tabprev'](ident), sp['tabprev'](ident), _full((1, A_WIDTH)), _full((1, A_WIDTH)),
                  _full((A_GROUPS, T, T)), _full((T, A_GROUPS)), _full((1, B_Q_HEADS))],
        out_specs=pl.BlockSpec((T, 2 * A_WIDTH), lambda i: (i, 0)),
        out_shape=jax.ShapeDtypeStruct((s, 2 * A_WIDTH), MXU_DTYPE),
        compiler_params=_params(("parallel",)),
    )(h, h, h, cq, sq, cq, sq, ln_g.reshape(1, -1), ln_b.reshape(1, -1), ws, bs_t, sinks.reshape(1, -1))


def _even_bwd(h, dmix, cq, sq, ln_g, ln_b, ws, bs_t, sinks):
    s = h.shape[0]
    nb, sp = _even_specs(s)

    def body(h_ref, kp_ref, vp_ref, cq_ref, sq_ref, ckp_ref, skp_ref, dm_ref, g_ref, b_ref, ws_ref, bst_ref, sink_ref,
             dh_ref, dg_ref, db_ref, dws_ref, dbst_ref, dsink_ref, dvn_scr, ck_scr, cv_scr):
        i = pl.program_id(0)

        @pl.when(i == 0)
        def _():
            for ref in (dg_ref, db_ref, dws_ref, dbst_ref, dsink_ref, ck_scr, cv_scr):
                ref[...] = jnp.zeros_like(ref)

        tril = _tril()
        vh, rstd, vn = _a_norm(h_ref[:, 1024:2048], g_ref[...], b_ref[...])
        vnb = vn.astype(MXU_DTYPE)
        for g in range(A_GROUPS):
            c0, c1 = g * LANES, (g + 1) * LANES
            wg = jnp.where(tril, ws_ref[g], 0.0).astype(MXU_DTYPE)
            mixed = _dot(wg, vnb[:, c0:c1]) + bst_ref[:, g:g + 1]
            ag = h_ref[:, 2048 + c0:2048 + c1]
            u = h_ref[:, c0:c1]
            sil, dsil = _silu_and_grad(ag)
            da = dm_ref[:, c0:c1]
            dh_ref[:, c0:c1] = (da * mixed * sil).astype(dh_ref.dtype)
            dh_ref[:, 2048 + c0:2048 + c1] = (da * u * mixed * dsil).astype(dh_ref.dtype)
            dmx = da * u * sil
            dmb = dmx.astype(MXU_DTYPE)
            dvn_scr[:, c0:c1] = _dot_tn(wg, dmb)
            dws_ref[g] += jnp.where(tril, _dot_nt(dmb, vnb[:, c0:c1]), 0.0)
            dbst_ref[:, g:g + 1] += jnp.sum(dmx, axis=1, keepdims=True)
        dvn = dvn_scr[...]
        dg_ref[...] += jnp.sum(dvn * vh, axis=0, keepdims=True)
        db_ref[...] += jnp.sum(dvn, axis=0, keepdims=True)
        dvh = dvn * g_ref[...]
        dv = rstd * (dvh - jnp.mean(dvh, axis=1, keepdims=True) - vh * jnp.mean(dvh * vh, axis=1, keepdims=True))
        dh_ref[:, 1024:2048] = dv.astype(dh_ref.dtype)

        cq, sq = cq_ref[...], sq_ref[...]
        k_rot = _rope(h_ref[:, K0:K0 + LANES], cq, sq)
        kp_rot = _rope(kp_ref[...], ckp_ref[...], skp_ref[...])
        valid = _band_mask(i < nb - 1)
        dk_all = jnp.zeros((2 * T, LANES), F32)
        dv_all = jnp.zeros((2 * T, LANES), F32)
        for hk in range(B_KV_HEADS):
            k2, v2 = _kv_operands(hk, k_rot, kp_rot, h_ref[:, V0:V0 + LANES], vp_ref[...])
            cols = _pair_columns(hk)
            qm = _stacked_heads([_rope(h_ref[:, Q0 + c0:Q0 + c0 + LANES], cq, sq) for c0 in cols])
            p, psink = _softmax_with_sink(qm, k2, valid, sink_ref, GROUP_HEADS * hk)
            pb = p.astype(MXU_DTYPE)
            out = _dot(pb, v2)
            dobs = []
            for pp, c0 in enumerate(cols):
                sil, dsil = _silu_and_grad(h_ref[:, BG0 + c0:BG0 + c0 + LANES])
                dbo = dm_ref[:, A_WIDTH + c0:A_WIDTH + c0 + LANES]
                dh_ref[:, BG0 + c0:BG0 + c0 + LANES] = (dbo * _pair_of(out, pp) * dsil).astype(dh_ref.dtype)
                dobs.append(dbo * sil)
            dobm = _stacked_heads(dobs)
            dp = _dot_nt(dobm, v2)
            rs = jnp.sum(p * dp, axis=1, keepdims=True)
            dsb = (p * (dp - rs) * ATTN_SCALE).astype(MXU_DTYPE)
            to_sink = psink * rs
            for h in range(GROUP_HEADS):
                head = GROUP_HEADS * hk + h
                dsink_ref[:, head:head + 1] += -jnp.sum(to_sink[h * T:(h + 1) * T], axis=0, keepdims=True)
            dq = _dot(dsb, k2)
            for pp, c0 in enumerate(cols):
                dh_ref[:, Q0 + c0:Q0 + c0 + LANES] = _rope_bwd(_pair_of(dq, pp), cq, sq).astype(dh_ref.dtype)
            acc_k = _dot_tn(dsb, qm)
            acc_v = _dot_tn(pb, dobm)
            lo2 = lax.broadcasted_iota(jnp.int32, (2 * T, LANES), 1) < B_HEAD_DIM
            mine = lo2 if hk == 0 else jnp.logical_not(lo2)
            dk_all = dk_all + jnp.where(mine, acc_k + pltpu.roll(acc_k, B_HEAD_DIM, 1), 0.0)
            dv_all = dv_all + jnp.where(mine, acc_v + pltpu.roll(acc_v, B_HEAD_DIM, 1), 0.0)
        dk_rot = dk_all[T:2 * T] + ck_scr[...]
        dvv = dv_all[T:2 * T] + cv_scr[...]
        ck_scr[...] = dk_all[0:T]
        cv_scr[...] = dv_all[0:T]
        dh_ref[:, K0:K0 + LANES] = _rope_bwd(dk_rot, cq, sq).astype(dh_ref.dtype)
        dh_ref[:, V0:V0 + LANES] = dvv.astype(dh_ref.dtype)

    rev = lambda i: nb - 1 - i
    return pl.pallas_call(
        body, name="even_bwd", grid=(nb,),
        in_specs=[sp['h'](rev), sp['kprev'](rev), sp['vprev'](rev), sp['tab'](rev), sp['tab'](rev),
                  sp['tabprev'](rev), sp['tabprev'](rev), pl.BlockSpec((T, 2 * A_WIDTH), lambda i: (rev(i), 0)),
                  _full((1, A_WIDTH)), _full((1, A_WIDTH)), _full((A_GROUPS, T, T)), _full((T, A_GROUPS)),
                  _full((1, B_Q_HEADS))],
        out_specs=[sp['h'](rev), _full((1, A_WIDTH)), _full((1, A_WIDTH)),
                   _full((A_GROUPS, T, T)), _full((T, A_GROUPS)), _full((1, B_Q_HEADS))],
        out_shape=[jax.ShapeDtypeStruct((s, EVEN_IN), MXU_DTYPE), jax.ShapeDtypeStruct((1, A_WIDTH), F32),
                   jax.ShapeDtypeStruct((1, A_WIDTH), F32), jax.ShapeDtypeStruct((A_GROUPS, T, T), F32),
                   jax.ShapeDtypeStruct((T, A_GROUPS), F32), jax.ShapeDtypeStruct((1, B_Q_HEADS), F32)],
        scratch_shapes=[pltpu.VMEM((T, A_WIDTH), F32), pltpu.VMEM((T, LANES), F32), pltpu.VMEM((T, LANES), F32)],
        compiler_params=_params(("arbitrary",)),
    )(h, h, h, cq, sq, cq, sq, dmix, ln_g.reshape(1, -1), ln_b.reshape(1, -1), ws, bs_t, sinks.reshape(1, -1))


def _shift_down(x, d, fill=0.0):
    rows = lax.broadcasted_iota(jnp.int32, x.shape, 0)
    return jnp.where(rows >= d, pltpu.roll(x, d, 0), fill)


def _shift_up(x, d, fill=0.0):
    n = x.shape[0]
    rows = lax.broadcasted_iota(jnp.int32, x.shape, 0)
    return jnp.where(rows < n - d, pltpu.roll(x, n - d, 0), fill)


SCAN_BLOCK = 128
SUBLANES = 8


def _scan_block(a, b, up):
    n = a.shape[0]
    d = 1
    while d < n:
        if d < SUBLANES:
            shift = _shift_up if up else _shift_down
            a, b = a * shift(a, d, 1.0), a * shift(b, d) + b
        elif up:
            a, b = (jnp.concatenate([a[:n - d] * a[d:], a[n - d:]], axis=0),
                    jnp.concatenate([a[:n - d] * b[d:] + b[:n - d], b[n - d:]], axis=0))
        else:
            a, b = (jnp.concatenate([a[:d], a[d:] * a[:n - d]], axis=0),
                    jnp.concatenate([b[:d], a[d:] * b[:n - d] + b[d:]], axis=0))
        d *= 2
    return a, b


def _scan(a, b, up):
    n = a.shape[0]
    size = min(SCAN_BLOCK, n)
    blocks = n // size
    out, carry = [None] * blocks, None
    for k in (reversed(range(blocks)) if up else range(blocks)):
        prod, h = _scan_block(a[k * size:(k + 1) * size], b[k * size:(k + 1) * size], up)
        if carry is not None:
            h = h + prod * carry
        carry = h[0:1] if up else h[size - 1:size]
        out[k] = h
    return jnp.concatenate(out, axis=0)


def _neg_expm1(y):
    t = jnp.tanh(-0.5 * y)
    return 2.0 * t / (1.0 + t)


def _softplus(x):
    return jnp.maximum(x, 0.0) + jnp.log1p(jnp.exp(-jnp.abs(x)))


def _lru_forward(xc, cw, cb, wa, ba, wx, bx, lam):
    shifted = [_shift_down(xc, CONV_WIDTH - 1 - j) if j < CONV_WIDTH - 1 else xc for j in range(CONV_WIDTH)]
    xconv = cb
    for j in range(CONV_WIDTH):
        xconv = xconv + cw[j:j + 1, :] * shifted[j]
    xb = xconv.astype(MXU_DTYPE)
    r = _sigmoid(_dot(xb, wa.astype(MXU_DTYPE)) + ba)
    ig = _sigmoid(_dot(xb, wx.astype(MXU_DTYPE)) + bx)
    sp = _softplus(-lam)
    log_a = -LRU_C * r * sp
    a = jnp.exp(log_a)
    mult = jnp.sqrt(_neg_expm1(2.0 * log_a))
    hs = _scan(a, mult * ig * xconv, up=False)
    return dict(shifted=shifted, xconv=xconv, xb=xb, r=r, ig=ig, sp=sp, a=a, mult=mult, hs=hs)


def _col(width, off):
    return lambda s: pl.BlockSpec((s, width), lambda i: (0, off + i))


def _part(width, slot):
    return lambda s: pl.BlockSpec((None, s, width), lambda i: (slot, 0, i))


def _lru_specs(s):
    w = LANES
    return [_part(w, 0)(s), _part(w, 1)(s), pl.BlockSpec((CONV_WIDTH, w), lambda i: (0, i)),
            pl.BlockSpec((1, w), lambda i: (0, i)), pl.BlockSpec((None, w, w), lambda i: (i, 0, 0)),
            pl.BlockSpec((1, w), lambda i: (0, i)), pl.BlockSpec((None, w, w), lambda i: (i, 0, 0)),
            pl.BlockSpec((1, w), lambda i: (0, i)), pl.BlockSpec((1, w), lambda i: (0, i))]


def _lru_fwd(h, cw, cb, wa, ba, wx, bx, lam):
    s = h.shape[1]

    def body(xc_ref, cg_ref, cw_ref, cb_ref, wa_ref, ba_ref, wx_ref, bx_ref, lam_ref, o_ref):
        f = _lru_forward(xc_ref[...], cw_ref[...], cb_ref[...], wa_ref[...], ba_ref[...], wx_ref[...], bx_ref[...],
                         lam_ref[...])
        sil, _ = _silu_and_grad(cg_ref[...])
        o_ref[...] = (f['hs'] * sil).astype(o_ref.dtype)

    return pl.pallas_call(
        body, name="lru_fwd", grid=(C_HEADS,), in_specs=_lru_specs(s), out_specs=_col(LANES, 0)(s),
        out_shape=jax.ShapeDtypeStruct((s, C_WIDTH + D_WIDTH), MXU_DTYPE), compiler_params=_params(("parallel",)),
    )(h, h, cw, cb.reshape(1, -1), wa, ba.reshape(1, -1), wx, bx.reshape(1, -1), lam.reshape(1, -1))


def _lru_bwd(h, dmix, cw, cb, wa, ba, wx, bx, lam):
    s = h.shape[1]

    def body(xc_ref, cg_ref, cw_ref, cb_ref, wa_ref, ba_ref, wx_ref, bx_ref, lam_ref, dco_ref,
             dh_ref, dcw_ref, dcb_ref, dwa_ref, dba_ref, dwx_ref, dbx_ref, dlam_ref):
        dxc_ref, dcg_ref = dh_ref.at[0], dh_ref.at[1]
        cw, lam = cw_ref[...], lam_ref[...]
        wab, wxb = wa_ref[...].astype(MXU_DTYPE), wx_ref[...].astype(MXU_DTYPE)
        f = _lru_forward(xc_ref[...], cw, cb_ref[...], wa_ref[...], ba_ref[...], wx_ref[...], bx_ref[...], lam)
        sil, dsil = _silu_and_grad(cg_ref[...])
        dco = dco_ref[...]
        hs, a, mult, ig, r, xconv = f['hs'], f['a'], f['mult'], f['ig'], f['r'], f['xconv']
        dcg_ref[...] = (dco * hs * dsil).astype(dcg_ref.dtype)
        lamb = _scan(_shift_up(a, 1), dco * sil, up=True)
        da = lamb * _shift_down(hs, 1)
        d_ig = lamb * mult * xconv
        d_mult = lamb * ig * xconv
        dxconv = lamb * mult * ig
        d_log_a = da * a - d_mult * (a * a) / mult
        dlam_ref[...] = jnp.sum(d_log_a * r, axis=0, keepdims=True) * LRU_C * _sigmoid(-lam)
        dpa = d_log_a * (-LRU_C * f['sp']) * r * (1.0 - r)
        dpx = d_ig * ig * (1.0 - ig)
        dba_ref[...] = jnp.sum(dpa, axis=0, keepdims=True)
        dbx_ref[...] = jnp.sum(dpx, axis=0, keepdims=True)
        dpab, dpxb = dpa.astype(MXU_DTYPE), dpx.astype(MXU_DTYPE)
        dwa_ref[...] = _dot_tn(f['xb'], dpab)
        dwx_ref[...] = _dot_tn(f['xb'], dpxb)
        dxconv = dxconv + _dot_nt(dpab, wab) + _dot_nt(dpxb, wxb)
        dcb_ref[...] = jnp.sum(dxconv, axis=0, keepdims=True)
        dxc = jnp.zeros_like(dxconv)
        for j in range(CONV_WIDTH):
            dcw_ref[j:j + 1, :] = jnp.sum(dxconv * f['shifted'][j], axis=0, keepdims=True)
            back = _shift_up(dxconv, CONV_WIDTH - 1 - j) if j < CONV_WIDTH - 1 else dxconv
            dxc = dxc + cw[j:j + 1, :] * back
        dxc_ref[...] = dxc.astype(dxc_ref.dtype)

    w = LANES
    vec = pl.BlockSpec((1, w), lambda i: (0, i))
    mat = pl.BlockSpec((None, w, w), lambda i: (i, 0, 0))
    vshape = jax.ShapeDtypeStruct((1, C_WIDTH), F32)
    mshape = jax.ShapeDtypeStruct((C_HEADS, w, w), F32)
    return pl.pallas_call(
        body, name="lru_bwd", grid=(C_HEADS,), in_specs=_lru_specs(s) + [_col(w, 0)(s)],
        out_specs=[pl.BlockSpec((2, s, w), lambda i: (0, 0, i)), pl.BlockSpec((CONV_WIDTH, w), lambda i: (0, i)),
                   vec, mat, vec, mat, vec, vec],
        out_shape=[jax.ShapeDtypeStruct((N_CHIPS, s, C_WIDTH), MXU_DTYPE),
                   jax.ShapeDtypeStruct((CONV_WIDTH, C_WIDTH), F32), vshape, mshape, vshape, mshape, vshape, vshape],
        compiler_params=_params(("parallel",)),
    )(h, h, cw, cb.reshape(1, -1), wa, ba.reshape(1, -1), wx, bx.reshape(1, -1), lam.reshape(1, -1), dmix)


def _pool_window(g):
    return jnp.left_shift(2, g).astype(F32)


def _select_window(g, sums):
    out = sums[-1]
    for k in range(len(sums) - 2, -1, -1):
        out = jnp.where(g == k, sums[k], out)
    return out


def _window_sums(x, shift):
    sums, cur = [], x
    for k in range(D_GROUPS):
        cur = cur + shift(cur, 2 ** k)
        sums.append(cur)
    return sums


def _pool_specs(s):
    w = D_GROUP_DIM
    return [_part(w, 2)(s), _part(w, 3)(s),
            pl.BlockSpec((None, w, w), lambda i: (i, 0, 0)), pl.BlockSpec((1, w), lambda i: (0, i))]


def _pool_forward(xd, g):
    rows = lax.broadcasted_iota(jnp.int32, xd.shape, 0).astype(F32) + 1.0
    cnt = jnp.minimum(rows, _pool_window(g))
    pooled = _select_window(g, _window_sums(xd, _shift_down)) / cnt - xd
    return pooled, cnt


def _pool_fwd(h, mix, w_pool, d_scale):
    s = h.shape[1]

    def body(xd_ref, dg_ref, wp_ref, sc_ref, mix_ref, o_ref):
        pooled, _ = _pool_forward(xd_ref[...], pl.program_id(0))
        mixed = _dot(pooled.astype(MXU_DTYPE), wp_ref[...].astype(MXU_DTYPE))
        sil, _ = _silu_and_grad(dg_ref[...])
        o_ref[...] = (mixed * sc_ref[...] * sil).astype(o_ref.dtype)

    return pl.pallas_call(
        body, name="pool_fwd", grid=(D_GROUPS,), in_specs=_pool_specs(s) + [ANY],
        out_specs=_col(D_GROUP_DIM, C_WIDTH // D_GROUP_DIM)(s), out_shape=jax.ShapeDtypeStruct(mix.shape, mix.dtype),
        input_output_aliases={4: 0}, compiler_params=_params(("parallel",)),
    )(h, h, w_pool, d_scale.reshape(1, -1), mix)


def _pool_bwd(h, dmix, dh, w_pool, d_scale):
    s = h.shape[1]
    w = D_GROUP_DIM

    def body(xd_ref, dg_ref, wp_ref, sc_ref, ddo_ref, dh_in_ref, dh_ref, dwp_ref, dsc_ref):
        dxd_ref, ddg_ref = dh_ref.at[0], dh_ref.at[1]
        g = pl.program_id(0)
        pooled, cnt = _pool_forward(xd_ref[...], g)
        pb = pooled.astype(MXU_DTYPE)
        wpb = wp_ref[...].astype(MXU_DTYPE)
        mixed = _dot(pb, wpb)
        sil, dsil = _silu_and_grad(dg_ref[...])
        ddo = ddo_ref[...]
        sc = sc_ref[...]
        ddg_ref[...] = (ddo * mixed * sc * dsil).astype(ddg_ref.dtype)
        dms = ddo * sil
        dsc_ref[...] = jnp.sum(dms * mixed, axis=0, keepdims=True)
        dmb = (dms * sc).astype(MXU_DTYPE)
        dwp_ref[...] = _dot_tn(pb, dmb)
        dpooled = _dot_nt(dmb, wpb)
        dxd = _select_window(g, _window_sums(dpooled / cnt, _shift_up)) - dpooled
        dxd_ref[...] = dxd.astype(dxd_ref.dtype)

    return pl.pallas_call(
        body, name="pool_bwd", grid=(D_GROUPS,), in_specs=_pool_specs(s) + [_col(w, C_WIDTH // w)(s), ANY],
        out_specs=[pl.BlockSpec((2, s, w), lambda i: (1, 0, i)), pl.BlockSpec((None, w, w), lambda i: (i, 0, 0)),
                   pl.BlockSpec((1, w), lambda i: (0, i))],
        out_shape=[jax.ShapeDtypeStruct(dh.shape, dh.dtype),
                   jax.ShapeDtypeStruct((D_GROUPS, w, w), F32), jax.ShapeDtypeStruct((1, D_WIDTH), F32)],
        input_output_aliases={5: 0}, compiler_params=_params(("parallel",)),
    )(h, h, w_pool, d_scale.reshape(1, -1), dmix, dh)


def _rope_tables(positions):
    s = positions.shape[0]
    inv_freq = ROPE_THETA ** (-jnp.arange(0, ROT_DIM, 2, dtype=F32) / ROT_DIM)
    ang = positions.astype(F32)[:, None] * inv_freq
    c, sn = jnp.cos(ang), jnp.sin(ang)
    rest = B_HEAD_DIM - ROT_DIM
    c64 = jnp.concatenate([c, c, jnp.ones((s, rest), F32)], axis=1)
    s64 = jnp.concatenate([-sn, sn, jnp.zeros((s, rest), F32)], axis=1)
    return jnp.tile(c64, (1, LANES // B_HEAD_DIM)), jnp.tile(s64, (1, LANES // B_HEAD_DIM))


def _local_step(x, positions, target, w, projections_of, projection_grads, norm_grads_done):
    cq, sq = _rope_tables(positions)
    saved = []
    xin, xin_b = x, x.astype(MXU_DTYPE)
    for layer in range(DEPTH):
        j = layer // 2
        kind = 'even' if layer % 2 == 0 else 'odd'
        w_in, w_out = projections_of(layer, xin_b)
        h = _mm_in(xin_b, w_in, W_IN_TRANSPOSED[kind], "mm_" + kind + "_in")
        if kind == 'even':
            mix = _even_fwd(h, cq, sq, w['even_a_ln_g'][j], w['even_a_ln_b'][j], w['even_a_ws'][j],
                            w['even_a_bs'][j].T, w['even_b_sinks'][j])
        else:
            mix = _lru_fwd(h, w['odd_conv_w'][j], w['odd_conv_b'][j], w['odd_w_a'][j], w['odd_b_a'][j],
                           w['odd_w_x'][j], w['odd_b_x'][j], w['odd_lam'][j])
            mix = _pool_fwd(h, mix, w['odd_w_pool'][j], w['odd_d_scale'][j])
        last = layer == DEPTH - 1
        z, out, out_b = _mm_out_norm(mix, w_out, xin, w[kind + '_ln_g'][j], w[kind + '_ln_b'][j],
                                     "mm_out_norm_loss" if last else "mm_out_norm", target if last else None)
        saved.append((xin_b, h, mix, z, w_in, w_out))
        xin, xin_b = out, out_b

    dout, loss = xin, xin_b
    per_layer = {n: [None, None] for n in SMALL_NAMES}
    behind = loss
    for layer in reversed(range(DEPTH)):
        j = layer // 2
        kind = 'even' if layer % 2 == 0 else 'odd'
        xin_b, h, mix, z, w_in, w_out = saved[layer]
        dz, dz_b, dg, db = _ln_bwd(dout, z, w[kind + '_ln_g'][j], behind)
        per_layer[kind + '_ln_g'][j], per_layer[kind + '_ln_b'][j] = dg[0], db[0]
        dmix = _mm_dmix(dz_b, w_out, norm_grads_done(layer, dz_b), "mm_dmix")
        dw_out = _mm_dw_out(mix, dz_b, "mm_dw_out")
        if kind == 'even':
            dh, dlg, dlb, dws, dbst, dsink = _even_bwd(
                h, dmix, cq, sq, w['even_a_ln_g'][j], w['even_a_ln_b'][j], w['even_a_ws'][j], w['even_a_bs'][j].T,
                w['even_b_sinks'][j])
            for n, val in (('even_a_ln_g', dlg[0]), ('even_a_ln_b', dlb[0]), ('even_a_ws', dws),
                           ('even_a_bs', dbst.T), ('even_b_sinks', dsink[0])):
                per_layer[n][j] = val
        else:
            dh, dcw, dcb, dwa, dba, dwx, dbx, dlam = _lru_bwd(
                h, dmix, w['odd_conv_w'][j], w['odd_conv_b'][j], w['odd_w_a'][j], w['odd_b_a'][j], w['odd_w_x'][j],
                w['odd_b_x'][j], w['odd_lam'][j])
            dh, dwp, dsc = _pool_bwd(h, dmix, dh, w['odd_w_pool'][j], w['odd_d_scale'][j])
            for n, val in (('odd_conv_w', dcw), ('odd_conv_b', dcb[0]), ('odd_w_a', dwa), ('odd_b_a', dba[0]),
                           ('odd_w_x', dwx), ('odd_b_x', dbx[0]), ('odd_lam', dlam[0]), ('odd_w_pool', dwp),
                           ('odd_d_scale', dsc[0])):
                per_layer[n][j] = val
        dout = _mm_dx(dh, w_in, dz, W_IN_TRANSPOSED[kind], "mm_dx_" + kind)
        dw_in = _mm_dw_in(xin_b, dh, W_IN_TRANSPOSED[kind], "mm_dw_in_" + kind)
        behind = projection_grads(layer, dw_in, dw_out)
    return loss, dout, {n: jnp.stack(v) for n, v in per_layer.items()}


ANY = pl.BlockSpec(memory_space=pl.ANY)


def _where_am_i():
    return lax.axis_index("x"), lax.axis_index("y"), lax.axis_index("c")


def _other_chips(x, y):
    return [(1 - x, y), (x, 1 - y), (1 - x, 1 - y)]


PACKED_ROWS = 16


def _half_rows(rows, which):
    half = rows // 2
    assert half % PACKED_ROWS == 0, rows
    return pl.ds(pl.multiple_of(which * half, PACKED_ROWS), half)


def _sibling_swap(bufs, name, *, other_half_of_axis1=False, after=()):
    n, na = len(bufs), len(after)
    out_shapes = [(b.shape[0], b.shape[1] // 2, b.shape[2]) if other_half_of_axis1 else b.shape for b in bufs]

    def body(*refs):
        x_refs, out_refs, (send_sems, recv_sems) = refs[:n], refs[n + na:2 * n + na], refs[2 * n + na:]
        x, y, c = _where_am_i()
        copies = []
        for p in range(n):
            src = x_refs[p].at[:, _half_rows(bufs[p].shape[1], 1 - c), :] if other_half_of_axis1 else x_refs[p]
            copies.append(pltpu.make_async_remote_copy(
                src_ref=src, dst_ref=out_refs[p], send_sem=send_sems.at[p], recv_sem=recv_sems.at[p],
                device_id=(x, y, 1 - c), device_id_type=MESH))
            copies[-1].start()
        for cp in copies:
            cp.wait()

    return pl.pallas_call(
        body, name=name, in_specs=[ANY] * (n + na), out_specs=[ANY] * n,
        out_shape=[jax.ShapeDtypeStruct(s, b.dtype) for s, b in zip(out_shapes, bufs)],
        scratch_shapes=[pltpu.SemaphoreType.DMA((n,)), pltpu.SemaphoreType.DMA((n,))],
    )(*bufs, *after)


HBM = pl.BlockSpec(memory_space=pltpu.HBM)
SEM = pl.BlockSpec(memory_space=pltpu.SEMAPHORE)
IN_FLIGHT = pltpu.CompilerParams(has_side_effects=pltpu.SideEffectType.DATAFLOW_SIDE_EFFECTING)


def _peers(kind, x, y, c):
    return [(x, y, 1 - c)] if kind == 'swap' else [(px, py, c) for px, py in _other_chips(x, y)]


def _n_peers(kind):
    return 1 if kind == 'swap' else N_CHIPS - 1


def _land_shape(kind, a):
    if kind == 'gather':
        return (N_CHIPS,) + a.shape
    return (a.shape[0], a.shape[1] // 2, a.shape[2]) if kind == 'swap' else a.shape


def _ici_copy(kind, src_ref, land_ref, send_sems, recv_sems, k, peer, x, y, c):
    me = 2 * x + y
    if kind == 'gather':
        half = _half_rows(src_ref.shape[0], c)
        src, dst = src_ref.at[half, :], land_ref.at[me, half, :]
    elif kind == 'scatter':
        src, dst = src_ref.at[2 * peer[0] + peer[1]], land_ref.at[me]
    else:
        src, dst = src_ref.at[:, _half_rows(src_ref.shape[1], 1 - c), :], land_ref
    return pltpu.make_async_remote_copy(src_ref=src, dst_ref=dst, send_sem=send_sems.at[k], recv_sem=recv_sems.at[k],
                                        device_id=peer, device_id_type=MESH)


def _ici_start(kind, groups, name, after=()):
    flat = [a for g in groups for a in g]
    n, ng, np_, na = len(flat), len(groups), _n_peers(kind), len(after)
    land_shapes = [_land_shape(kind, a) for a in flat]

    def body(*refs):
        srcs, lands, sems, token = refs[:n], refs[n:2 * n], refs[2 * n + na:2 * n + na + 2 * ng], refs[-1]
        x, y, c = _where_am_i()
        p = 0
        for gi, g in enumerate(groups):
            for q in range(len(g)):
                for k, peer in enumerate(_peers(kind, x, y, c)):
                    _ici_copy(kind, srcs[p], lands[p], sems[2 * gi], sems[2 * gi + 1], np_ * q + k, peer, x, y,
                              c).start()
                p += 1
        token[...] = jnp.zeros_like(token)

    sem_shapes = [pltpu.SemaphoreType.DMA((np_ * len(g),)) for g in groups for _ in range(2)]
    outs = pl.pallas_call(
        body, name=name, in_specs=[HBM] * (2 * n) + [ANY] * na,
        out_specs=[SEM] * (2 * ng) + [HBM] * (2 * n) + [pl.BlockSpec(memory_space=pltpu.VMEM)],
        out_shape=sem_shapes + [pltpu.HBM(a.shape, a.dtype) for a in flat]
        + [pltpu.HBM(s, a.dtype) for s, a in zip(land_shapes, flat)] + [jax.ShapeDtypeStruct((8, LANES), F32)],
        input_output_aliases={i: 2 * ng + i for i in range(2 * n)}, compiler_params=IN_FLIGHT,
    )(*[pltpu.with_memory_space_constraint(a, pltpu.HBM) for a in flat],
      *[pltpu.with_memory_space_constraint(lax.empty(s, a.dtype), pltpu.HBM) for s, a in zip(land_shapes, flat)],
      *after)
    sems, srcs, lands, token = outs[:2 * ng], outs[2 * ng:2 * ng + n], outs[2 * ng + n:2 * ng + 2 * n], outs[-1]
    started, p = [], 0
    for gi, g in enumerate(groups):
        started.append((sems[2 * gi], sems[2 * gi + 1], srcs[p:p + len(g)], lands[p:p + len(g)]))
        p += len(g)
    return started, token


def _ici_wait(kind, started, after, name):
    send_sems, recv_sems, srcs, lands = started
    n, np_ = len(srcs), _n_peers(kind)
    after = list(after)

    def body(*refs):
        src_refs, land_refs, send_ref, recv_ref = refs[:n], refs[n:2 * n], refs[2 * n], refs[2 * n + 1]
        x, y, c = _where_am_i()
        for q in range(n):
            for k, peer in enumerate(_peers(kind, x, y, c)):
                cp = _ici_copy(kind, src_refs[q], land_refs[q], send_ref, recv_ref, np_ * q + k, peer, x, y, c)
                cp.wait_send()
                cp.wait_recv()

    outs = pl.pallas_call(
        body, name=name, in_specs=[HBM] * (2 * n) + [SEM, SEM] + [ANY] * len(after), out_specs=[HBM] * (2 * n),
        out_shape=[pltpu.HBM(a.shape, a.dtype) for a in list(srcs) + list(lands)],
        input_output_aliases={i: i for i in range(2 * n)}, compiler_params=IN_FLIGHT,
    )(*srcs, *lands, send_sems, recv_sems, *after)
    return outs[:n], outs[n:]


def _forward_to_sibling(lands, name):
    n = len(lands)

    def body(*refs):
        out_refs, (send_sems, recv_sems) = refs[n:2 * n], refs[2 * n:]
        x, y, c = _where_am_i()

        def copy(q, k, peer, half_of):
            rows = out_refs[q].at[2 * peer[0] + peer[1], _half_rows(lands[q].shape[1], half_of), :]
            return pltpu.make_async_remote_copy(
                src_ref=rows, dst_ref=rows, send_sem=send_sems.at[3 * q + k], recv_sem=recv_sems.at[3 * q + k],
                device_id=(x, y, 1 - c), device_id_type=MESH)

        sends = [copy(q, k, peer, c) for q in range(n) for k, peer in enumerate(_other_chips(x, y))]
        for cp in sends:
            cp.start()
        for q in range(n):
            for k, peer in enumerate(_other_chips(x, y)):
                copy(q, k, peer, 1 - c).wait_recv()
        for cp in sends:
            cp.wait_send()

    return pl.pallas_call(
        body, name=name, in_specs=[ANY] * n, out_specs=[ANY] * n,
        out_shape=[jax.ShapeDtypeStruct(a.shape, a.dtype) for a in lands],
        input_output_aliases={i: i for i in range(n)},
        scratch_shapes=[pltpu.SemaphoreType.DMA((3 * n,)), pltpu.SemaphoreType.DMA((3 * n,))],
    )(*lands)


def _with_own_slot(lands, own, chip):
    out = []
    for land, mine in zip(lands, own):
        mine = lax.dynamic_slice_in_dim(mine, chip, 1, axis=0) if mine.ndim == land.ndim else mine[None]
        out.append(lax.dynamic_update_slice(land, mine, (chip,) + (0,) * (land.ndim - 1)))
    return out


def _row_tile(rows, row_bytes, n_bufs, budget=24 * 1024 * 1024):
    best = None
    for t in range(PACKED_ROWS, rows + 1, PACKED_ROWS):
        if rows % t == 0 and 2 * n_bufs * t * row_bytes <= budget:
            best = t
    assert best is not None, (rows, row_bytes)
    return best


def _add_halves(full, recv, core, out_dtype, name):
    n, h, cols = recv.shape
    tr = _row_tile(h, cols * 4, 3)
    nt = h // tr

    def body(core_ref, a_ref, b_ref, o_ref):
        o_ref[...] = (a_ref[...].astype(F32) + b_ref[...].astype(F32)).astype(out_dtype)

    blk = (None, tr, cols)
    return pl.pallas_call(
        body, name=name,
        grid_spec=pltpu.PrefetchScalarGridSpec(
            num_scalar_prefetch=1, grid=(n, nt),
            in_specs=[pl.BlockSpec(blk, lambda j, i, core_ref: (j, core_ref[0] * nt + i, 0)),
                      pl.BlockSpec(blk, lambda j, i, core_ref: (j, i, 0))],
            out_specs=pl.BlockSpec(blk, lambda j, i, core_ref: (j, i, 0))),
        out_shape=jax.ShapeDtypeStruct((n, h, cols), out_dtype),
        compiler_params=_params(("parallel", "parallel")),
    )(core.reshape(1), full, recv)


def _add2(a, b, name):
    r, cols = a.shape
    tr = _row_tile(r, cols * 4, 3)

    def body(a_ref, b_ref, o_ref):
        o_ref[...] = a_ref[...] + b_ref[...]

    blk = pl.BlockSpec((tr, cols), lambda i: (i, 0))
    return pl.pallas_call(body, name=name, grid=(r // tr,), in_specs=[blk, blk], out_specs=blk,
                          out_shape=jax.ShapeDtypeStruct(a.shape, a.dtype), compiler_params=_params(("parallel",)))(a, b)


def _sum_slots(parts, name):
    n, h, cols = parts.shape
    tr = _row_tile(h, cols * 4, n + 1)

    def body(p_ref, o_ref):
        acc = p_ref[0].astype(F32)
        for k in range(1, n):
            acc = acc + p_ref[k].astype(F32)
        o_ref[...] = acc

    return pl.pallas_call(
        body, name=name, grid=(h // tr,), in_specs=[pl.BlockSpec((n, tr, cols), lambda i: (0, i, 0))],
        out_specs=pl.BlockSpec((tr, cols), lambda i: (i, 0)), out_shape=jax.ShapeDtypeStruct((h, cols), F32),
        compiler_params=_params(("parallel",)),
    )(parts)


def _adamw(w, g, m, v, name):
    shape = w.shape
    cols = shape[-1]
    rows = math.prod(shape[:-1])
    tr = rows
    if rows % 8 == 0:
        tr = 8
        for t in range(8, rows + 1, 8):
            if rows % t == 0 and t * cols * 4 <= 1536 * 1024:
                tr = t
    c1 = 1.0 - ADAM_B1 ** ADAM_STEP
    c2 = 1.0 - ADAM_B2 ** ADAM_STEP

    def body(w_ref, g_ref, m_ref, v_ref, d_ref, nm_ref, nv_ref):
        gg = g_ref[...]
        nm = ADAM_B1 * m_ref[...] + (1.0 - ADAM_B1) * gg
        nv = ADAM_B2 * v_ref[...] + (1.0 - ADAM_B2) * (gg * gg)
        d_ref[...] = -ADAM_LR * ((nm / c1) / (jnp.sqrt(nv / c2) + ADAM_EPS) + ADAM_WD * w_ref[...])
        nm_ref[...] = nm
        nv_ref[...] = nv

    blk = pl.BlockSpec((tr, cols), lambda i: (i, 0))
    flat = jax.ShapeDtypeStruct((rows, cols), F32)
    outs = pl.pallas_call(
        body, name=name, grid=(rows // tr,), in_specs=[blk] * 4, out_specs=[blk] * 3, out_shape=[flat] * 3,
        compiler_params=_params(("parallel",)),
    )(*[a.reshape(rows, cols) for a in (w, g, m, v)])
    return [o.reshape(shape) for o in outs]


def _adamw_update(w, g, m, v):
    c1 = 1.0 - ADAM_B1 ** ADAM_STEP
    c2 = 1.0 - ADAM_B2 ** ADAM_STEP
    nm = ADAM_B1 * m + (1.0 - ADAM_B1) * g
    nv = ADAM_B2 * v + (1.0 - ADAM_B2) * (g * g)
    return -ADAM_LR * ((nm / c1) / (jnp.sqrt(nv / c2) + ADAM_EPS) + ADAM_WD * w), nm, nv


def _adamw_projection(w, m, v, mine, theirs, core, name):
    _, r, cols = w.shape
    h = r // 2
    tr = _row_tile(h, cols * 4, 12, budget=40 * 1024 * 1024)
    nt = h // tr

    def body(core_ref, w_ref, m_ref, v_ref, a0, b0, a1, b1, g_ref, d_ref, nm_ref, nv_ref):
        layer, half = pl.program_id(0), pl.program_id(1)
        own = half == core_ref[0]
        g = jnp.where(layer == 0, jnp.where(own, a0[...], b0[...]), jnp.where(own, a1[...], b1[...]))
        g_ref[...] = g
        d_ref[...], nm_ref[...], nv_ref[...] = _adamw_update(w_ref[...], g, m_ref[...], v_ref[...])

    def piece(layer, own):
        def index(l, hh, i, core_ref):
            used = (l == layer) & ((hh == core_ref[0]) == own)
            return (jnp.where(used, i, 0), 0)
        return pl.BlockSpec((tr, cols), index)

    whole = pl.BlockSpec((None, tr, cols), lambda l, hh, i, core_ref: (l, hh * nt + i, 0))
    return pl.pallas_call(
        body, name=name,
        grid_spec=pltpu.PrefetchScalarGridSpec(
            num_scalar_prefetch=1, grid=(2, 2, nt),
            in_specs=[whole] * 3 + [piece(0, True), piece(0, False), piece(1, True), piece(1, False)],
            out_specs=[whole] * 4),
        out_shape=[jax.ShapeDtypeStruct(w.shape, F32)] * 4,
        compiler_params=_params(("arbitrary", "arbitrary", "arbitrary")),
    )(core.reshape(1), w, m, v, mine[0], theirs[0], mine[1], theirs[1])


def _chip_shape(name):
    shape = list(FULL_SHAPES[name])
    axis = BIG_SHARD_AXIS.get(name, SMALL_SHARD_AXIS.get(name))
    if axis is not None:
        shape[axis] //= N_CHIPS
    return tuple(shape)


def _round_up(n, k):
    return -(-n // k) * k


SHARDED_SMALL_WORDS = sum(math.prod(_chip_shape(n)) for n in SMALL_SHARDED)
SHARDED_SMALL_ROWS = _round_up(-(-SHARDED_SMALL_WORDS // PACK_COLS), 32)
SMALL_MATRICES = ['even_a_ws', 'odd_w_a', 'odd_w_x', 'odd_w_pool']
SMALL_VECTORS = [n for n in SMALL_NAMES if n not in SMALL_MATRICES]
VECTOR_ROWS = _round_up(-(-(sum(math.prod(FULL_SHAPES[n]) for n in SMALL_VECTORS) + 1) // PACK_COLS), 32)


def _pack_rows(flat, rows):
    return jnp.pad(flat, (0, rows * PACK_COLS - flat.shape[0])).reshape(rows, PACK_COLS)


def _pack_small_sharded(shards):
    return _pack_rows(jnp.concatenate([shards[n].reshape(-1) for n in SMALL_SHARDED]), SHARDED_SMALL_ROWS)


def _unpack_small_sharded(gathered_small):
    out = {}
    words = gathered_small.reshape(N_CHIPS, -1)
    w0 = 0
    for n in SMALL_SHARDED:
        size = math.prod(_chip_shape(n))
        blocks = words[:, w0:w0 + size].reshape((N_CHIPS,) + _chip_shape(n))
        out[n] = jnp.concatenate([blocks[j] for j in range(N_CHIPS)], axis=SMALL_SHARD_AXIS[n])
        w0 += size
    return out


def _small_grad_parts(grads, loss):
    words = [grads[n].reshape(-1) for n in SMALL_VECTORS] + [loss.reshape(-1)]
    return ([_pack_rows(jnp.concatenate(words), VECTOR_ROWS)]
            + [grads[n].reshape(-1, FULL_SHAPES[n][-1]) for n in SMALL_MATRICES])


def _own_block(name, full, chip):
    if name not in SMALL_SHARD_AXIS:
        return full
    axis = SMALL_SHARD_AXIS[name]
    width = FULL_SHAPES[name][axis] // N_CHIPS
    return lax.dynamic_slice_in_dim(full, chip * width, width, axis)


def _from_small_grad_parts(totals, chip):
    flat = totals[0].reshape(-1)
    out = {}
    w0 = 0
    for n in SMALL_VECTORS:
        size = math.prod(FULL_SHAPES[n])
        out[n] = _own_block(n, flat[w0:w0 + size].reshape(FULL_SHAPES[n]), chip)
        w0 += size
    for n, total in zip(SMALL_MATRICES, totals[1:]):
        out[n] = _own_block(n, total.reshape(FULL_SHAPES[n]), chip)
    return out, flat[w0]


def kernel(x, positions, even_w_in, even_a_ln_g, even_a_ln_b, even_a_ws, even_a_bs, even_b_sinks, even_w_out, even_ln_g, even_ln_b, odd_w_in, odd_conv_w, odd_conv_b, odd_w_a, odd_b_a, odd_w_x, odd_b_x, odd_lam, odd_w_pool, odd_d_scale, odd_w_out, odd_ln_g, odd_ln_b, loss_target, m_even_w_in, m_even_a_ln_g, m_even_a_ln_b, m_even_a_ws, m_even_a_bs, m_even_b_sinks, m_even_w_out, m_even_ln_g, m_even_ln_b, m_odd_w_in, m_odd_conv_w, m_odd_conv_b, m_odd_w_a, m_odd_b_a, m_odd_w_x, m_odd_b_x, m_odd_lam, m_odd_w_pool, m_odd_d_scale, m_odd_w_out, m_odd_ln_g, m_odd_ln_b, v_even_w_in, v_even_a_ln_g, v_even_a_ln_b, v_even_a_ws, v_even_a_bs, v_even_b_sinks, v_even_w_out, v_even_ln_g, v_even_ln_b, v_odd_w_in, v_odd_conv_w, v_odd_conv_b, v_odd_w_a, v_odd_b_a, v_odd_w_x, v_odd_b_x, v_odd_lam, v_odd_w_pool, v_odd_d_scale, v_odd_w_out, v_odd_ln_g, v_odd_ln_b):
    local = dict(zip(WEIGHT_NAMES, (even_w_in, even_a_ln_g, even_a_ln_b, even_a_ws, even_a_bs, even_b_sinks, even_w_out, even_ln_g, even_ln_b, odd_w_in, odd_conv_w, odd_conv_b, odd_w_a, odd_b_a, odd_w_x, odd_b_x, odd_lam, odd_w_pool, odd_d_scale, odd_w_out, odd_ln_g, odd_ln_b)))
    mom = dict(zip(WEIGHT_NAMES, (m_even_w_in, m_even_a_ln_g, m_even_a_ln_b, m_even_a_ws, m_even_a_bs, m_even_b_sinks, m_even_w_out, m_even_ln_g, m_even_ln_b, m_odd_w_in, m_odd_conv_w, m_odd_conv_b, m_odd_w_a, m_odd_b_a, m_odd_w_x, m_odd_b_x, m_odd_lam, m_odd_w_pool, m_odd_d_scale, m_odd_w_out, m_odd_ln_g, m_odd_ln_b)))
    vel = dict(zip(WEIGHT_NAMES, (v_even_w_in, v_even_a_ln_g, v_even_a_ln_b, v_even_a_ws, v_even_a_bs, v_even_b_sinks, v_even_w_out, v_even_ln_g, v_even_ln_b, v_odd_w_in, v_odd_conv_w, v_odd_conv_b, v_odd_w_a, v_odd_b_a, v_odd_w_x, v_odd_b_x, v_odd_lam, v_odd_w_pool, v_odd_d_scale, v_odd_w_out, v_odd_ln_g, v_odd_ln_b)))
    mx, my, core = _where_am_i()
    chip = 2 * mx + my

    stored = lambda n, a: jnp.swapaxes(a, 1, 2) if n == 'even_w_in' and W_IN_TRANSPOSED['even'] else a
    names_of = lambda layer: [('even' if layer % 2 == 0 else 'odd') + s for s in ('_w_in', '_w_out')]

    shards = [[stored(n, local[n])[layer // 2].astype(MXU_DTYPE) for n in names_of(layer)] for layer in range(DEPTH)]
    shards[0].insert(0, _pack_small_sharded(local))
    gathers, first_token = _ici_start('gather', shards[:1], "gather_start_first")
    later, gather_token = _ici_start('gather', shards[1:], "gather_start", after=[first_token])
    gathers += later

    def gathered(layer, *after):
        own, lands = _ici_wait('gather', gathers[layer], after, "gather_wait_%d" % layer)
        return _with_own_slot(_forward_to_sibling(lands, "gather_forward_%d" % layer), own, chip)

    first = gathered(0, first_token, gather_token)
    weights = dict(local)
    weights.update(_unpack_small_sharded(first[0]))
    projections_of = lambda layer, after: first[1:] if layer == 0 else gathered(layer, after)

    swapping, in_flight, mine, started = [], [], {}, {}

    def land(*after):
        layer, scatter = in_flight.pop()
        parts, lands = _ici_wait('scatter', scatter, after, "scatter_wait_%d" % layer)
        for n, arrived in zip(names_of(layer), _with_own_slot(lands, parts, chip)):
            mine[n, layer // 2] = _sum_slots(arrived, "big_sum_%s_%d" % (n, layer // 2))

    def projection_grads(layer, dw_in, dw_out):
        if in_flight:
            land(dw_in)
        (swap,), token = _ici_start('swap', [[dw_in, dw_out]], "presum_start_%d" % layer)
        swapping.append((layer, swap))
        return presum_and_scatter(token) if layer == 0 else token

    def presum_and_scatter(*after):
        layer, swap = swapping.pop()
        partial, from_sibling = _ici_wait('swap', swap, after, "presum_wait_%d" % layer)
        chip_part = [_add_halves(g, r, core, MXU_DTYPE, "big_presum_add_%s_%d" % (n, layer // 2))
                     for g, r, n in zip(partial, from_sibling, names_of(layer))]
        (scatter,), started[layer] = _ici_start('scatter', [chip_part], "scatter_start_%d" % layer)
        in_flight.append((layer, scatter))
        return started[layer]

    norm_grads_done = lambda layer, dz: presum_and_scatter(dz) if swapping else dz
    loss, grad_x, grads = _local_step(x[0], positions[0], loss_target[0], weights, projections_of, projection_grads,
                                      norm_grads_done)

    small = _small_grad_parts(grads, loss)
    chip_sums = [_add2(a, b, "small_presum_add_%d" % i)
                 for i, (a, b) in enumerate(zip(small, _sibling_swap(small, "small_swap", after=[started[0]])))]
    (small_gather,), small_token = _ici_start('gather', [chip_sums], "small_gather_start")

    reduced, deltas, new_m, new_v = {}, {}, {}, {}

    def adamw_projections(kind, after):
        names = [kind + '_w_in', kind + '_w_out']
        pieces = [(n, l) for n in names for l in range(2)]
        theirs = dict(zip(pieces, _sibling_swap([mine[p] for p in pieces], "big_join_" + kind, after=after)))
        done = []
        for n in names:
            outs = _adamw_projection(stored(n, local[n]), stored(n, mom[n]), stored(n, vel[n]),
                                     [mine[n, l] for l in range(2)], [theirs[n, l] for l in range(2)], core,
                                     "adamw_" + n)
            reduced[n], deltas[n], new_m[n], new_v[n] = [stored(n, o) for o in outs]
            done.append(outs[-1])
        return done

    done = adamw_projections('odd', [small_token])
    land(grad_x, *done)
    done = adamw_projections('even', [])

    own, lands = _ici_wait('gather', small_gather, done, "small_gather_wait")
    of_chips = _with_own_slot(_forward_to_sibling(lands, "small_gather_forward"), own, chip)
    small_totals, total_loss = _from_small_grad_parts(
        [_sum_slots(g, "small_sum_%d" % i) for i, g in enumerate(of_chips)], chip)
    reduced.update(small_totals)
    for n in SMALL_NAMES:
        deltas[n], new_m[n], new_v[n] = _adamw(local[n], reduced[n], mom[n], vel[n], "adamw_" + n)

    return (total_loss, grad_x[None], *[reduced[n] for n in WEIGHT_NAMES], *[deltas[n] for n in WEIGHT_NAMES],
            *[new_m[n] for n in WEIGHT_NAMES], *[new_v[n] for n in WEIGHT_NAMES])
```

```python
import math

import jax
import jax.numpy as jnp
from jax import lax
from jax.experimental import pallas as pl
from jax.experimental.pallas import tpu as pltpu

F32 = jnp.float32
BF16 = jnp.bfloat16
MXU_DTYPE = jnp.bfloat16

D_MODEL = 2048
DEPTH = 4
A_WIDTH = 1024
A_GROUPS = 8
CHUNK = 128
B_HEAD_DIM = 64
B_Q_HEADS = 16
B_KV_HEADS = 2
WINDOW = 128
ROT_DIM = 16
ROPE_THETA = 500000.0
C_WIDTH = 1024
C_HEADS = 8
CONV_WIDTH = 4
LRU_C = 8.0
D_WIDTH = 1024
D_GROUPS = 4
D_GROUP_DIM = 256
EVEN_IN = 5376
ODD_IN = 4096
DN_ALPHA = (2 * DEPTH) ** 0.25
LN_EPS = 1e-5
ATTN_SCALE = B_HEAD_DIM ** -0.5
NEG_BIG = -1e30

ADAM_LR = 0.001
ADAM_B1 = 0.9
ADAM_B2 = 0.999
ADAM_EPS = 1e-08
ADAM_WD = 0.01
ADAM_STEP = 10

LANES = 128
VMEM_LIMIT = 56 * 1024 * 1024
N_CHIPS = 4
MESH = pl.DeviceIdType.MESH

WEIGHT_NAMES = ['even_w_in', 'even_a_ln_g', 'even_a_ln_b', 'even_a_ws', 'even_a_bs', 'even_b_sinks', 'even_w_out',
                'even_ln_g', 'even_ln_b', 'odd_w_in', 'odd_conv_w', 'odd_conv_b', 'odd_w_a', 'odd_b_a', 'odd_w_x',
                'odd_b_x', 'odd_lam', 'odd_w_pool', 'odd_d_scale', 'odd_w_out', 'odd_ln_g', 'odd_ln_b']
BIG_SHARD_AXIS = {'even_w_in': 2, 'even_w_out': 1, 'odd_w_in': 2, 'odd_w_out': 1}
SMALL_SHARD_AXIS = {'odd_conv_w': 2, 'odd_conv_b': 1, 'odd_b_a': 1, 'odd_b_x': 1, 'odd_lam': 1, 'odd_w_pool': 2,
                    'odd_d_scale': 1, 'odd_ln_g': 1, 'odd_ln_b': 1}
FULL_SHAPES = {
    'even_w_in': (2, 2048, 5376), 'even_a_ln_g': (2, 1024), 'even_a_ln_b': (2, 1024), 'even_a_ws': (2, 8, 128, 128),
    'even_a_bs': (2, 8, 128), 'even_b_sinks': (2, 16), 'even_w_out': (2, 2048, 2048), 'even_ln_g': (2, 2048),
    'even_ln_b': (2, 2048), 'odd_w_in': (2, 2048, 4096), 'odd_conv_w': (2, 4, 1024), 'odd_conv_b': (2, 1024),
    'odd_w_a': (2, 8, 128, 128), 'odd_b_a': (2, 1024), 'odd_w_x': (2, 8, 128, 128), 'odd_b_x': (2, 1024),
    'odd_lam': (2, 1024), 'odd_w_pool': (2, 4, 256, 256), 'odd_d_scale': (2, 1024), 'odd_w_out': (2, 2048, 2048),
    'odd_ln_g': (2, 2048), 'odd_ln_b': (2, 2048)}
BIG_NAMES = ['even_w_in', 'even_w_out', 'odd_w_in', 'odd_w_out']
SMALL_SHARDED = ['odd_conv_w', 'odd_conv_b', 'odd_b_a', 'odd_b_x', 'odd_lam', 'odd_d_scale', 'odd_ln_g', 'odd_ln_b',
                 'odd_w_pool']
SMALL_NAMES = [n for n in WEIGHT_NAMES if n not in BIG_NAMES]
W_IN_TRANSPOSED = {'even': True, 'odd': False}
PACK_COLS = 1024


def _params(sem):
    return pltpu.CompilerParams(dimension_semantics=sem, vmem_limit_bytes=VMEM_LIMIT)


def _dot(a, b):
    return lax.dot_general(a, b, (((1,), (0,)), ((), ())), preferred_element_type=F32)


def _dot_nt(a, b):
    return lax.dot_general(a, b, (((1,), (1,)), ((), ())), preferred_element_type=F32)


def _dot_tn(a, b):
    return lax.dot_general(a, b, (((0,), (0,)), ((), ())), preferred_element_type=F32)


def _sigmoid(x):
    return 0.5 * jnp.tanh(0.5 * x) + 0.5


def _silu_and_grad(x):
    s = _sigmoid(x)
    return x * s, s * (1.0 + x * (1.0 - s))


def _mm_call(body, name, grid, in_specs, out_spec, out_shape, args):
    return pl.pallas_call(body, name=name, grid=grid, in_specs=in_specs, out_specs=out_spec, out_shape=out_shape,
                          compiler_params=_params(("parallel",) * len(grid)))(*args)


def _resident(block, index_map):
    return pl.BlockSpec(block, index_map, pipeline_mode=pl.Buffered(1))


ROWS_OF_WT = 768


def _mm_in(x, wg, transposed, name):
    s, k = x.shape
    x_spec = _resident((s, k), lambda j: (0, 0))

    def body(x_ref, w_ref, o_ref):
        o_ref[...] = (_dot_nt if transposed else _dot)(x_ref[...], w_ref[...])

    if transposed:
        wt = wg.reshape(-1, k)
        n, tn = wt.shape[0], ROWS_OF_WT
        return _mm_call(body, name, (n // tn,), [x_spec, pl.BlockSpec((tn, k), lambda j: (j, 0))],
                        pl.BlockSpec((s, tn), lambda j: (0, j)), jax.ShapeDtypeStruct((s, n), F32), (x, wt))
    nc = wg.shape[2]
    return _mm_call(body, name, (N_CHIPS,), [x_spec, pl.BlockSpec((None, k, nc), lambda j: (j, 0, 0))],
                    pl.BlockSpec((None, s, nc), lambda j: (j, 0, 0)),
                    jax.ShapeDtypeStruct((N_CHIPS, s, nc), F32), (x, wg))


def _mm_dx(dh, wg, dz, transposed, name):
    s, d = dz.shape
    tm, tn = min(1024, s), 512
    res = pl.BlockSpec((tm, tn), lambda i, j: (i, j))
    if transposed:
        wt = wg.reshape(-1, d)
        n = wt.shape[0]

        def body(a_ref, w_ref, r_ref, o_ref):
            o_ref[...] = DN_ALPHA * r_ref[...] + _dot(a_ref[...], w_ref[...])

        return _mm_call(body, name, (s // tm, d // tn),
                        [pl.BlockSpec((tm, n), lambda i, j: (i, 0)), pl.BlockSpec((n, tn), lambda i, j: (0, j)), res],
                        res, jax.ShapeDtypeStruct((s, d), F32), (dh, wt, dz))
    nc = dh.shape[2]

    def body(a_ref, w_ref, r_ref, o_ref):
        acc = DN_ALPHA * r_ref[...]
        for j in range(N_CHIPS):
            acc = acc + _dot_nt(a_ref[j], w_ref[j])
        o_ref[...] = acc

    return _mm_call(body, name, (s // tm, d // tn),
                    [pl.BlockSpec((N_CHIPS, tm, nc), lambda i, j: (0, i, 0)),
                     pl.BlockSpec((N_CHIPS, tn, nc), lambda i, j: (0, j, 0)), res],
                    res, jax.ShapeDtypeStruct((s, d), F32), (dh, wg, dz))


def _mm_dw_in(x, dh, transposed, name):
    s, d = x.shape
    x_spec = _resident((s, d), lambda j: (0, 0))

    def body(a_ref, b_ref, o_ref):
        o_ref[...] = _dot_tn(a_ref[...], b_ref[...]).astype(o_ref.dtype)

    if transposed:
        n, tn = dh.shape[1], ROWS_OF_WT
        flat = _mm_call(body, name, (n // tn,), [pl.BlockSpec((s, tn), lambda j: (0, j)), x_spec],
                        pl.BlockSpec((tn, d), lambda j: (j, 0)), jax.ShapeDtypeStruct((n, d), MXU_DTYPE), (dh, x))
        return flat.reshape(N_CHIPS, n // N_CHIPS, d)
    nc = dh.shape[2]
    return _mm_call(body, name, (N_CHIPS,), [x_spec, pl.BlockSpec((None, s, nc), lambda j: (j, 0, 0))],
                    pl.BlockSpec((None, d, nc), lambda j: (j, 0, 0)),
                    jax.ShapeDtypeStruct((N_CHIPS, d, nc), MXU_DTYPE), (x, dh))


def _mm_out_norm(mix, wg, x, g, b, name, target=None, after=None):
    s, k = mix.shape
    _, kc, d = wg.shape
    tm = min(512, s)

    def normed(a_ref, w_ref, x_ref, g_ref, b_ref, z_ref):
        z = DN_ALPHA * x_ref[...] + _dot(a_ref[...], w_ref[...].reshape(N_CHIPS * kc, d))
        mu = jnp.mean(z, axis=1, keepdims=True)
        zc = z - mu
        var = jnp.mean(zc * zc, axis=1, keepdims=True)
        z_ref[...] = z
        return zc * lax.rsqrt(var + LN_EPS) * g_ref[...] + b_ref[...]

    def body(a_ref, w_ref, x_ref, g_ref, b_ref, after_ref, z_ref, o_ref, ob_ref):
        o = normed(a_ref, w_ref, x_ref, g_ref, b_ref, z_ref)
        o_ref[...] = o
        ob_ref[...] = o.astype(ob_ref.dtype)

    def body_with_loss(a_ref, w_ref, x_ref, g_ref, b_ref, t_ref, z_ref, do_ref, l_ref):
        @pl.when(pl.program_id(0) == 0)
        def _():
            l_ref[...] = jnp.zeros_like(l_ref)

        e = normed(a_ref, w_ref, x_ref, g_ref, b_ref, z_ref) - t_ref[...]
        do_ref[...] = e * (1.0 / d)
        l_ref[...] += (0.5 / d) * jnp.sum(jnp.sum(e * e, axis=1, keepdims=True), axis=0, keepdims=True)

    row = pl.BlockSpec((tm, d), lambda i: (i, 0))
    vec = pl.BlockSpec((1, d), lambda i: (0, 0))
    in_specs = [pl.BlockSpec((tm, k), lambda i: (i, 0)), _resident((N_CHIPS, kc, d), lambda i: (0, 0, 0)), row, vec,
                vec]
    args = (mix, wg, x, g.reshape(1, d), b.reshape(1, d))
    f32_rows = jax.ShapeDtypeStruct((s, d), F32)
    if target is None:
        return pl.pallas_call(
            body, name=name, grid=(s // tm,), in_specs=in_specs + [ANY], out_specs=[row, row, row],
            out_shape=[f32_rows, f32_rows, jax.ShapeDtypeStruct((s, d), MXU_DTYPE)],
            compiler_params=_params(("parallel",)))(*args, after)
    return pl.pallas_call(
        body_with_loss, name=name, grid=(s // tm,), in_specs=in_specs + [row],
        out_specs=[row, row, pl.BlockSpec((1, 1), lambda i: (0, 0))],
        out_shape=[f32_rows, f32_rows, jax.ShapeDtypeStruct((1, 1), F32)],
        compiler_params=_params(("arbitrary",)))(*args, target)


def _mm_dmix(dz, wg, after, name):
    s, d = dz.shape
    kc = wg.shape[1]

    def body(a_ref, w_ref, after_ref, o_ref):
        o_ref[...] = _dot_nt(a_ref[...], w_ref[...])

    return _mm_call(body, name, (N_CHIPS,),
                    [_resident((s, d), lambda j: (0, 0)), pl.BlockSpec((None, kc, d), lambda j: (j, 0, 0)), ANY],
                    pl.BlockSpec((s, kc), lambda j: (0, j)), jax.ShapeDtypeStruct((s, N_CHIPS * kc), F32),
                    (dz, wg, after))


def _mm_dw_out(mix, dz, name):
    s, k = mix.shape
    d = dz.shape[1]
    kc = k // N_CHIPS

    def body(a_ref, b_ref, o_ref):
        o_ref[...] = _dot_tn(a_ref[...], b_ref[...]).astype(o_ref.dtype)

    return _mm_call(body, name, (N_CHIPS,),
                    [pl.BlockSpec((s, kc), lambda i: (0, i)), _resident((s, d), lambda i: (0, 0))],
                    pl.BlockSpec((None, kc, d), lambda i: (i, 0, 0)),
                    jax.ShapeDtypeStruct((N_CHIPS, kc, d), MXU_DTYPE), (mix, dz))


def _ln_bwd(dout, z, g, after):
    s, d = z.shape
    tr = min(256, s)

    def body(do_ref, z_ref, g_ref, after_ref, dz_ref, dzb_ref, dg_ref, db_ref):
        @pl.when(pl.program_id(0) == 0)
        def _():
            dg_ref[...] = jnp.zeros_like(dg_ref)
            db_ref[...] = jnp.zeros_like(db_ref)

        zz = z_ref[...]
        do = do_ref[...]
        mu = jnp.mean(zz, axis=1, keepdims=True)
        zc = zz - mu
        var = jnp.mean(zc * zc, axis=1, keepdims=True)
        rstd = lax.rsqrt(var + LN_EPS)
        zh = zc * rstd
        dg_ref[...] += jnp.sum(do * zh, axis=0, keepdims=True)
        db_ref[...] += jnp.sum(do, axis=0, keepdims=True)
        dzh = do * g_ref[...]
        dz = rstd * (dzh - jnp.mean(dzh, axis=1, keepdims=True) - zh * jnp.mean(dzh * zh, axis=1, keepdims=True))
        dz_ref[...] = dz
        dzb_ref[...] = dz.astype(MXU_DTYPE)

    row = pl.BlockSpec((tr, d), lambda i: (i, 0))
    vec = pl.BlockSpec((1, d), lambda i: (0, 0))
    return pl.pallas_call(
        body, name="ln_bwd", grid=(s // tr,), in_specs=[row, row, vec, ANY], out_specs=[row, row, vec, vec],
        out_shape=[jax.ShapeDtypeStruct((s, d), F32), jax.ShapeDtypeStruct((s, d), MXU_DTYPE),
                   jax.ShapeDtypeStruct((1, d), F32), jax.ShapeDtypeStruct((1, d), F32)],
        compiler_params=_params(("arbitrary",)),
    )(dout, z, g.reshape(1, d), after)


Q0, K0, V0, BG0 = 3072, 4096, 4224, 4352
T = CHUNK


def _lane_ids(width):
    return lax.broadcasted_iota(jnp.int32, (T, width), 1)


def _rope_swap(x):
    w = x.shape[1]
    l64 = _lane_ids(w) % B_HEAD_DIM
    half = ROT_DIM // 2
    return jnp.where(l64 < half, pltpu.roll(x, w - half, 1), jnp.where(l64 < ROT_DIM, pltpu.roll(x, half, 1), 0.0))


def _rope(x, c, s):
    return x * c + _rope_swap(x) * s


def _rope_bwd(d, c, s):
    return d * c + _rope_swap(d * s)


def _tril():
    return lax.broadcasted_iota(jnp.int32, (T, T), 0) >= lax.broadcasted_iota(jnp.int32, (T, T), 1)


def _band_mask(has_prev):
    qi = lax.broadcasted_iota(jnp.int32, (T, 2 * T), 0)
    kj = lax.broadcasted_iota(jnp.int32, (T, 2 * T), 1)
    return (kj > qi) & (kj <= qi + WINDOW) & ((kj >= WINDOW) | has_prev)


def _a_norm(v, g, b):
    mu = jnp.mean(v, axis=1, keepdims=True)
    vc = v - mu
    var = jnp.mean(vc * vc, axis=1, keepdims=True)
    rstd = lax.rsqrt(var + LN_EPS)
    vh = vc * rstd
    return vh, rstd, vh * g + b


def _kv_operands(hk, k_rot, kp_rot, vv, vp):
    lo = _lane_ids(LANES) < B_HEAD_DIM
    mine = lo if hk == 0 else jnp.logical_not(lo)
    both = lambda t: jnp.where(mine, t, 0.0) + pltpu.roll(jnp.where(mine, t, 0.0), B_HEAD_DIM, 1)
    k2 = jnp.concatenate([both(kp_rot), both(k_rot)], axis=0)
    v2 = jnp.concatenate([both(vp), both(vv)], axis=0)
    return k2.astype(MXU_DTYPE), v2.astype(MXU_DTYPE)


GROUP_HEADS = B_Q_HEADS // B_KV_HEADS


def _pair_columns(hk):
    return [(hk * GROUP_HEADS // 2 + pp) * LANES for pp in range(GROUP_HEADS // 2)]


def _stacked_heads(pairs):
    lo = _lane_ids(LANES) < B_HEAD_DIM
    rows = []
    for tile in pairs:
        rows += [jnp.where(lo, tile, 0.0), jnp.where(lo, 0.0, tile)]
    return jnp.concatenate(rows, axis=0).astype(MXU_DTYPE)


def _pair_of(stacked, pp):
    lo = _lane_ids(LANES) < B_HEAD_DIM
    return jnp.where(lo, stacked[2 * pp * T:(2 * pp + 1) * T], stacked[(2 * pp + 1) * T:(2 * pp + 2) * T])


def _softmax_with_sink(qm, k2, valid, sink_ref, head0):
    scores = (_dot_nt(qm, k2) * ATTN_SCALE).reshape(GROUP_HEADS, T, 2 * T)
    s = jnp.where(valid[None], scores, NEG_BIG).reshape(GROUP_HEADS * T, 2 * T)
    sink = jnp.concatenate([jnp.broadcast_to(sink_ref[:, head0 + h:head0 + h + 1], (T, 1))
                            for h in range(GROUP_HEADS)], axis=0)
    m = jnp.maximum(jnp.max(s, axis=1, keepdims=True), sink)
    ex = jnp.exp(s - m)
    es = jnp.exp(sink - m)
    inv = 1.0 / (jnp.sum(ex, axis=1, keepdims=True) + es)
    return ex * inv, es * inv


def _even_specs(s):
    nb = s // T
    prev = lambda r: lambda i: jnp.maximum(r(i) - 1, 0)
    return nb, {
        'h': lambda r: pl.BlockSpec((T, EVEN_IN), lambda i: (r(i), 0)),
        'kprev': lambda r: pl.BlockSpec((T, LANES), lambda i: (prev(r)(i), K0 // LANES)),
        'vprev': lambda r: pl.BlockSpec((T, LANES), lambda i: (prev(r)(i), V0 // LANES)),
        'tab': lambda r: pl.BlockSpec((T, LANES), lambda i: (r(i), 0)),
        'tabprev': lambda r: pl.BlockSpec((T, LANES), lambda i: (prev(r)(i), 0)),
    }


def _full(shape):
    nd = len(shape)
    return pl.BlockSpec(shape, lambda i: (0,) * nd)


def _even_fwd(h, cq, sq, ln_g, ln_b, ws, bs_t, sinks):
    s = h.shape[0]
    nb, sp = _even_specs(s)

    def body(h_ref, kp_ref, vp_ref, cq_ref, sq_ref, ckp_ref, skp_ref, g_ref, b_ref, ws_ref, bst_ref, sink_ref, mix_ref):
        i = pl.program_id(0)
        tril = _tril()
        _, _, vn = _a_norm(h_ref[:, 1024:2048], g_ref[...], b_ref[...])
        vnb = vn.astype(MXU_DTYPE)
        for g in range(A_GROUPS):
            c0, c1 = g * LANES, (g + 1) * LANES
            wg = jnp.where(tril, ws_ref[g], 0.0).astype(MXU_DTYPE)
            mixed = _dot(wg, vnb[:, c0:c1]) + bst_ref[:, g:g + 1]
            sil, _ = _silu_and_grad(h_ref[:, 2048 + c0:2048 + c1])
            mix_ref[:, c0:c1] = (h_ref[:, c0:c1] * mixed * sil).astype(mix_ref.dtype)

        cq, sq = cq_ref[...], sq_ref[...]
        k_rot = _rope(h_ref[:, K0:K0 + LANES], cq, sq)
        kp_rot = _rope(kp_ref[...], ckp_ref[...], skp_ref[...])
        valid = _band_mask(i > 0)
        for hk in range(B_KV_HEADS):
            k2, v2 = _kv_operands(hk, k_rot, kp_rot, h_ref[:, V0:V0 + LANES], vp_ref[...])
            qm = _stacked_heads([_rope(h_ref[:, Q0 + c0:Q0 + c0 + LANES], cq, sq) for c0 in _pair_columns(hk)])
            p, _ = _softmax_with_sink(qm, k2, valid, sink_ref, GROUP_HEADS * hk)
            out = _dot(p.astype(MXU_DTYPE), v2)
            for pp, c0 in enumerate(_pair_columns(hk)):
                sil, _ = _silu_and_grad(h_ref[:, BG0 + c0:BG0 + c0 + LANES])
                mix_ref[:, A_WIDTH + c0:A_WIDTH + c0 + LANES] = (_pair_of(out, pp) * sil).astype(mix_ref.dtype)

    ident = lambda i: i
    return pl.pallas_call(
        body, name="even_fwd", grid=(nb,),
        in_specs=[sp['h'](ident), sp['kprev'](ident), sp['vprev'](ident), sp['tab'](ident), sp['tab'](ident),
                  sp['tabprev'](ident), sp['tabprev'](ident), _full((1, A_WIDTH)), _full((1, A_WIDTH)),
                  _full((A_GROUPS, T, T)), _full((T, A_GROUPS)), _full((1, B_Q_HEADS))],
        out_specs=pl.BlockSpec((T, 2 * A_WIDTH), lambda i: (i, 0)),
        out_shape=jax.ShapeDtypeStruct((s, 2 * A_WIDTH), MXU_DTYPE),
        compiler_params=_params(("parallel",)),
    )(h, h, h, cq, sq, cq, sq, ln_g.reshape(1, -1), ln_b.reshape(1, -1), ws, bs_t, sinks.reshape(1, -1))


def _even_bwd(h, dmix, cq, sq, ln_g, ln_b, ws, bs_t, sinks):
    s = h.shape[0]
    nb, sp = _even_specs(s)

    def body(h_ref, kp_ref, vp_ref, cq_ref, sq_ref, ckp_ref, skp_ref, dm_ref, g_ref, b_ref, ws_ref, bst_ref, sink_ref,
             dh_ref, dg_ref, db_ref, dws_ref, dbst_ref, dsink_ref, dvn_scr, ck_scr, cv_scr):
        i = pl.program_id(0)

        @pl.when(i == 0)
        def _():
            for ref in (dg_ref, db_ref, dws_ref, dbst_ref, dsink_ref, ck_scr, cv_scr):
                ref[...] = jnp.zeros_like(ref)

        tril = _tril()
        vh, rstd, vn = _a_norm(h_ref[:, 1024:2048], g_ref[...], b_ref[...])
        vnb = vn.astype(MXU_DTYPE)
        for g in range(A_GROUPS):
            c0, c1 = g * LANES, (g + 1) * LANES
            wg = jnp.where(tril, ws_ref[g], 0.0).astype(MXU_DTYPE)
            mixed = _dot(wg, vnb[:, c0:c1]) + bst_ref[:, g:g + 1]
            ag = h_ref[:, 2048 + c0:2048 + c1]
            u = h_ref[:, c0:c1]
            sil, dsil = _silu_and_grad(ag)
            da = dm_ref[:, c0:c1]
            dh_ref[:, c0:c1] = (da * mixed * sil).astype(dh_ref.dtype)
            dh_ref[:, 2048 + c0:2048 + c1] = (da * u * mixed * dsil).astype(dh_ref.dtype)
            dmx = da * u * sil
            dmb = dmx.astype(MXU_DTYPE)
            dvn_scr[:, c0:c1] = _dot_tn(wg, dmb)
            dws_ref[g] += jnp.where(tril, _dot_nt(dmb, vnb[:, c0:c1]), 0.0)
            dbst_ref[:, g:g + 1] += jnp.sum(dmx, axis=1, keepdims=True)
        dvn = dvn_scr[...]
        dg_ref[...] += jnp.sum(dvn * vh, axis=0, keepdims=True)
        db_ref[...] += jnp.sum(dvn, axis=0, keepdims=True)
        dvh = dvn * g_ref[...]
        dv = rstd * (dvh - jnp.mean(dvh, axis=1, keepdims=True) - vh * jnp.mean(dvh * vh, axis=1, keepdims=True))
        dh_ref[:, 1024:2048] = dv.astype(dh_ref.dtype)

        cq, sq = cq_ref[...], sq_ref[...]
        k_rot = _rope(h_ref[:, K0:K0 + LANES], cq, sq)
        kp_rot = _rope(kp_ref[...], ckp_ref[...], skp_ref[...])
        valid = _band_mask(i < nb - 1)
        dk_all = jnp.zeros((2 * T, LANES), F32)
        dv_all = jnp.zeros((2 * T, LANES), F32)
        for hk in range(B_KV_HEADS):
            k2, v2 = _kv_operands(hk, k_rot, kp_rot, h_ref[:, V0:V0 + LANES], vp_ref[...])
            cols = _pair_columns(hk)
            qm = _stacked_heads([_rope(h_ref[:, Q0 + c0:Q0 + c0 + LANES], cq, sq) for c0 in cols])
            p, psink = _softmax_with_sink(qm, k2, valid, sink_ref, GROUP_HEADS * hk)
            pb = p.astype(MXU_DTYPE)
            out = _dot(pb, v2)
            dobs = []
            for pp, c0 in enumerate(cols):
                sil, dsil = _silu_and_grad(h_ref[:, BG0 + c0:BG0 + c0 + LANES])
                dbo = dm_ref[:, A_WIDTH + c0:A_WIDTH + c0 + LANES]
                dh_ref[:, BG0 + c0:BG0 + c0 + LANES] = (dbo * _pair_of(out, pp) * dsil).astype(dh_ref.dtype)
                dobs.append(dbo * sil)
            dobm = _stacked_heads(dobs)
            dp = _dot_nt(dobm, v2)
            rs = jnp.sum(p * dp, axis=1, keepdims=True)
            dsb = (p * (dp - rs) * ATTN_SCALE).astype(MXU_DTYPE)
            to_sink = psink * rs
            for h in range(GROUP_HEADS):
                head = GROUP_HEADS * hk + h
                dsink_ref[:, head:head + 1] += -jnp.sum(to_sink[h * T:(h + 1) * T], axis=0, keepdims=True)
            dq = _dot(dsb, k2)
            for pp, c0 in enumerate(cols):
                dh_ref[:, Q0 + c0:Q0 + c0 + LANES] = _rope_bwd(_pair_of(dq, pp), cq, sq).astype(dh_ref.dtype)
            acc_k = _dot_tn(dsb, qm)
            acc_v = _dot_tn(pb, dobm)
            lo2 = lax.broadcasted_iota(jnp.int32, (2 * T, LANES), 1) < B_HEAD_DIM
            mine = lo2 if hk == 0 else jnp.logical_not(lo2)
            dk_all = dk_all + jnp.where(mine, acc_k + pltpu.roll(acc_k, B_HEAD_DIM, 1), 0.0)
            dv_all = dv_all + jnp.where(mine, acc_v + pltpu.roll(acc_v, B_HEAD_DIM, 1), 0.0)
        dk_rot = dk_all[T:2 * T] + ck_scr[...]
        dvv = dv_all[T:2 * T] + cv_scr[...]
        ck_scr[...] = dk_all[0:T]
        cv_scr[...] = dv_all[0:T]
        dh_ref[:, K0:K0 + LANES] = _rope_bwd(dk_rot, cq, sq).astype(dh_ref.dtype)
        dh_ref[:, V0:V0 + LANES] = dvv.astype(dh_ref.dtype)

    rev = lambda i: nb - 1 - i
    return pl.pallas_call(
        body, name="even_bwd", grid=(nb,),
        in_specs=[sp['h'](rev), sp['kprev'](rev), sp['vprev'](rev), sp['tab'](rev), sp['tab'](rev),
                  sp['tabprev'](rev), sp['tabprev'](rev), pl.BlockSpec((T, 2 * A_WIDTH), lambda i: (rev(i), 0)),
                  _full((1, A_WIDTH)), _full((1, A_WIDTH)), _full((A_GROUPS, T, T)), _full((T, A_GROUPS)),
                  _full((1, B_Q_HEADS))],
        out_specs=[sp['h'](rev), _full((1, A_WIDTH)), _full((1, A_WIDTH)),
                   _full((A_GROUPS, T, T)), _full((T, A_GROUPS)), _full((1, B_Q_HEADS))],
        out_shape=[jax.ShapeDtypeStruct((s, EVEN_IN), MXU_DTYPE), jax.ShapeDtypeStruct((1, A_WIDTH), F32),
                   jax.ShapeDtypeStruct((1, A_WIDTH), F32), jax.ShapeDtypeStruct((A_GROUPS, T, T), F32),
                   jax.ShapeDtypeStruct((T, A_GROUPS), F32), jax.ShapeDtypeStruct((1, B_Q_HEADS), F32)],
        scratch_shapes=[pltpu.VMEM((T, A_WIDTH), F32), pltpu.VMEM((T, LANES), F32), pltpu.VMEM((T, LANES), F32)],
        compiler_params=_params(("arbitrary",)),
    )(h, h, h, cq, sq, cq, sq, dmix, ln_g.reshape(1, -1), ln_b.reshape(1, -1), ws, bs_t, sinks.reshape(1, -1))


def _shift_down(x, d, fill=0.0):
    rows = lax.broadcasted_iota(jnp.int32, x.shape, 0)
    return jnp.where(rows >= d, pltpu.roll(x, d, 0), fill)


def _shift_up(x, d, fill=0.0):
    n = x.shape[0]
    rows = lax.broadcasted_iota(jnp.int32, x.shape, 0)
    return jnp.where(rows < n - d, pltpu.roll(x, n - d, 0), fill)


SCAN_BLOCK = 128
SUBLANES = 8


def _scan_block(a, b, up):
    n = a.shape[0]
    d = 1
    while d < n:
        if d < SUBLANES:
            shift = _shift_up if up else _shift_down
            a, b = a * shift(a, d, 1.0), a * shift(b, d) + b
        elif up:
            a, b = (jnp.concatenate([a[:n - d] * a[d:], a[n - d:]], axis=0),
                    jnp.concatenate([a[:n - d] * b[d:] + b[:n - d], b[n - d:]], axis=0))
        else:
            a, b = (jnp.concatenate([a[:d], a[d:] * a[:n - d]], axis=0),
                    jnp.concatenate([b[:d], a[d:] * b[:n - d] + b[d:]], axis=0))
        d *= 2
    return a, b


def _scan(a, b, up):
    n = a.shape[0]
    size = min(SCAN_BLOCK, n)
    blocks = n // size
    out, carry = [None] * blocks, None
    for k in (reversed(range(blocks)) if up else range(blocks)):
        prod, h = _scan_block(a[k * size:(k + 1) * size], b[k * size:(k + 1) * size], up)
        if carry is not None:
            h = h + prod * carry
        carry = h[0:1] if up else h[size - 1:size]
        out[k] = h
    return jnp.concatenate(out, axis=0)


def _neg_expm1(y):
    t = jnp.tanh(-0.5 * y)
    return 2.0 * t / (1.0 + t)


def _softplus(x):
    return jnp.maximum(x, 0.0) + jnp.log1p(jnp.exp(-jnp.abs(x)))


def _lru_forward(xc, cw, cb, wa, ba, wx, bx, lam):
    shifted = [_shift_down(xc, CONV_WIDTH - 1 - j) if j < CONV_WIDTH - 1 else xc for j in range(CONV_WIDTH)]
    xconv = cb
    for j in range(CONV_WIDTH):
        xconv = xconv + cw[j:j + 1, :] * shifted[j]
    xb = xconv.astype(MXU_DTYPE)
    r = _sigmoid(_dot(xb, wa.astype(MXU_DTYPE)) + ba)
    ig = _sigmoid(_dot(xb, wx.astype(MXU_DTYPE)) + bx)
    sp = _softplus(-lam)
    log_a = -LRU_C * r * sp
    a = jnp.exp(log_a)
    mult = jnp.sqrt(_neg_expm1(2.0 * log_a))
    hs = _scan(a, mult * ig * xconv, up=False)
    return dict(shifted=shifted, xconv=xconv, xb=xb, r=r, ig=ig, sp=sp, a=a, mult=mult, hs=hs)


def _col(width, off):
    return lambda s: pl.BlockSpec((s, width), lambda i: (0, off + i))


def _part(width, slot):
    return lambda s: pl.BlockSpec((None, s, width), lambda i: (slot, 0, i))


def _lru_specs(s):
    w = LANES
    return [_part(w, 0)(s), _part(w, 1)(s), pl.BlockSpec((CONV_WIDTH, w), lambda i: (0, i)),
            pl.BlockSpec((1, w), lambda i: (0, i)), pl.BlockSpec((None, w, w), lambda i: (i, 0, 0)),
            pl.BlockSpec((1, w), lambda i: (0, i)), pl.BlockSpec((None, w, w), lambda i: (i, 0, 0)),
            pl.BlockSpec((1, w), lambda i: (0, i)), pl.BlockSpec((1, w), lambda i: (0, i))]


def _lru_fwd(h, cw, cb, wa, ba, wx, bx, lam):
    s = h.shape[1]

    def body(xc_ref, cg_ref, cw_ref, cb_ref, wa_ref, ba_ref, wx_ref, bx_ref, lam_ref, o_ref):
        f = _lru_forward(xc_ref[...], cw_ref[...], cb_ref[...], wa_ref[...], ba_ref[...], wx_ref[...], bx_ref[...],
                         lam_ref[...])
        sil, _ = _silu_and_grad(cg_ref[...])
        o_ref[...] = (f['hs'] * sil).astype(o_ref.dtype)

    return pl.pallas_call(
        body, name="lru_fwd", grid=(C_HEADS,), in_specs=_lru_specs(s), out_specs=_col(LANES, 0)(s),
        out_shape=jax.ShapeDtypeStruct((s, C_WIDTH + D_WIDTH), MXU_DTYPE), compiler_params=_params(("parallel",)),
    )(h, h, cw, cb.reshape(1, -1), wa, ba.reshape(1, -1), wx, bx.reshape(1, -1), lam.reshape(1, -1))


def _lru_bwd(h, dmix, cw, cb, wa, ba, wx, bx, lam):
    s = h.shape[1]

    def body(xc_ref, cg_ref, cw_ref, cb_ref, wa_ref, ba_ref, wx_ref, bx_ref, lam_ref, dco_ref,
             dh_ref, dcw_ref, dcb_ref, dwa_ref, dba_ref, dwx_ref, dbx_ref, dlam_ref):
        dxc_ref, dcg_ref = dh_ref.at[0], dh_ref.at[1]
        cw, lam = cw_ref[...], lam_ref[...]
        wab, wxb = wa_ref[...].astype(MXU_DTYPE), wx_ref[...].astype(MXU_DTYPE)
        f = _lru_forward(xc_ref[...], cw, cb_ref[...], wa_ref[...], ba_ref[...], wx_ref[...], bx_ref[...], lam)
        sil, dsil = _silu_and_grad(cg_ref[...])
        dco = dco_ref[...]
        hs, a, mult, ig, r, xconv = f['hs'], f['a'], f['mult'], f['ig'], f['r'], f['xconv']
        dcg_ref[...] = (dco * hs * dsil).astype(dcg_ref.dtype)
        lamb = _scan(_shift_up(a, 1), dco * sil, up=True)
        da = lamb * _shift_down(hs, 1)
        d_ig = lamb * mult * xconv
        d_mult = lamb * ig * xconv
        dxconv = lamb * mult * ig
        d_log_a = da * a - d_mult * (a * a) / mult
        dlam_ref[...] = jnp.sum(d_log_a * r, axis=0, keepdims=True) * LRU_C * _sigmoid(-lam)
        dpa = d_log_a * (-LRU_C * f['sp']) * r * (1.0 - r)
        dpx = d_ig * ig * (1.0 - ig)
        dba_ref[...] = jnp.sum(dpa, axis=0, keepdims=True)
        dbx_ref[...] = jnp.sum(dpx, axis=0, keepdims=True)
        dpab, dpxb = dpa.astype(MXU_DTYPE), dpx.astype(MXU_DTYPE)
        dwa_ref[...] = _dot_tn(f['xb'], dpab)
        dwx_ref[...] = _dot_tn(f['xb'], dpxb)
        dxconv = dxconv + _dot_nt(dpab, wab) + _dot_nt(dpxb, wxb)
        dcb_ref[...] = jnp.sum(dxconv, axis=0, keepdims=True)
        dxc = jnp.zeros_like(dxconv)
        for j in range(CONV_WIDTH):
            dcw_ref[j:j + 1, :] = jnp.sum(dxconv * f['shifted'][j], axis=0, keepdims=True)
            back = _shift_up(dxconv, CONV_WIDTH - 1 - j) if j < CONV_WIDTH - 1 else dxconv
            dxc = dxc + cw[j:j + 1, :] * back
        dxc_ref[...] = dxc.astype(dxc_ref.dtype)

    w = LANES
    vec = pl.BlockSpec((1, w), lambda i: (0, i))
    mat = pl.BlockSpec((None, w, w), lambda i: (i, 0, 0))
    vshape = jax.ShapeDtypeStruct((1, C_WIDTH), F32)
    mshape = jax.ShapeDtypeStruct((C_HEADS, w, w), F32)
    return pl.pallas_call(
        body, name="lru_bwd", grid=(C_HEADS,), in_specs=_lru_specs(s) + [_col(w, 0)(s)],
        out_specs=[pl.BlockSpec((2, s, w), lambda i: (0, 0, i)), pl.BlockSpec((CONV_WIDTH, w), lambda i: (0, i)),
                   vec, mat, vec, mat, vec, vec],
        out_shape=[jax.ShapeDtypeStruct((N_CHIPS, s, C_WIDTH), MXU_DTYPE),
                   jax.ShapeDtypeStruct((CONV_WIDTH, C_WIDTH), F32), vshape, mshape, vshape, mshape, vshape, vshape],
        compiler_params=_params(("parallel",)),
    )(h, h, cw, cb.reshape(1, -1), wa, ba.reshape(1, -1), wx, bx.reshape(1, -1), lam.reshape(1, -1), dmix)


def _pool_window(g):
    return jnp.left_shift(2, g).astype(F32)


def _select_window(g, sums):
    out = sums[-1]
    for k in range(len(sums) - 2, -1, -1):
        out = jnp.where(g == k, sums[k], out)
    return out


def _window_sums(x, shift):
    sums, cur = [], x
    for k in range(D_GROUPS):
        cur = cur + shift(cur, 2 ** k)
        sums.append(cur)
    return sums


def _pool_specs(s):
    w = D_GROUP_DIM
    return [_part(w, 2)(s), _part(w, 3)(s),
            pl.BlockSpec((None, w, w), lambda i: (i, 0, 0)), pl.BlockSpec((1, w), lambda i: (0, i))]


def _pool_forward(xd, g):
    rows = lax.broadcasted_iota(jnp.int32, xd.shape, 0).astype(F32) + 1.0
    cnt = jnp.minimum(rows, _pool_window(g))
    pooled = _select_window(g, _window_sums(xd, _shift_down)) / cnt - xd
    return pooled, cnt


def _pool_fwd(h, mix, w_pool, d_scale):
    s = h.shape[1]

    def body(xd_ref, dg_ref, wp_ref, sc_ref, mix_ref, o_ref):
        pooled, _ = _pool_forward(xd_ref[...], pl.program_id(0))
        mixed = _dot(pooled.astype(MXU_DTYPE), wp_ref[...].astype(MXU_DTYPE))
        sil, _ = _silu_and_grad(dg_ref[...])
        o_ref[...] = (mixed * sc_ref[...] * sil).astype(o_ref.dtype)

    return pl.pallas_call(
        body, name="pool_fwd", grid=(D_GROUPS,), in_specs=_pool_specs(s) + [ANY],
        out_specs=_col(D_GROUP_DIM, C_WIDTH // D_GROUP_DIM)(s), out_shape=jax.ShapeDtypeStruct(mix.shape, mix.dtype),
        input_output_aliases={4: 0}, compiler_params=_params(("parallel",)),
    )(h, h, w_pool, d_scale.reshape(1, -1), mix)


def _pool_bwd(h, dmix, dh, w_pool, d_scale):
    s = h.shape[1]
    w = D_GROUP_DIM

    def body(xd_ref, dg_ref, wp_ref, sc_ref, ddo_ref, dh_in_ref, dh_ref, dwp_ref, dsc_ref):
        dxd_ref, ddg_ref = dh_ref.at[0], dh_ref.at[1]
        g = pl.program_id(0)
        pooled, cnt = _pool_forward(xd_ref[...], g)
        pb = pooled.astype(MXU_DTYPE)
        wpb = wp_ref[...].astype(MXU_DTYPE)
        mixed = _dot(pb, wpb)
        sil, dsil = _silu_and_grad(dg_ref[...])
        ddo = ddo_ref[...]
        sc = sc_ref[...]
        ddg_ref[...] = (ddo * mixed * sc * dsil).astype(ddg_ref.dtype)
        dms = ddo * sil
        dsc_ref[...] = jnp.sum(dms * mixed, axis=0, keepdims=True)
        dmb = (dms * sc).astype(MXU_DTYPE)
        dwp_ref[...] = _dot_tn(pb, dmb)
        dpooled = _dot_nt(dmb, wpb)
        dxd = _select_window(g, _window_sums(dpooled / cnt, _shift_up)) - dpooled
        dxd_ref[...] = dxd.astype(dxd_ref.dtype)

    return pl.pallas_call(
        body, name="pool_bwd", grid=(D_GROUPS,), in_specs=_pool_specs(s) + [_col(w, C_WIDTH // w)(s), ANY],
        out_specs=[pl.BlockSpec((2, s, w), lambda i: (1, 0, i)), pl.BlockSpec((None, w, w), lambda i: (i, 0, 0)),
                   pl.BlockSpec((1, w), lambda i: (0, i))],
        out_shape=[jax.ShapeDtypeStruct(dh.shape, dh.dtype),
                   jax.ShapeDtypeStruct((D_GROUPS, w, w), F32), jax.ShapeDtypeStruct((1, D_WIDTH), F32)],
        input_output_aliases={5: 0}, compiler_params=_params(("parallel",)),
    )(h, h, w_pool, d_scale.reshape(1, -1), dmix, dh)


def _rope_tables(positions):
    s = positions.shape[0]
    inv_freq = ROPE_THETA ** (-jnp.arange(0, ROT_DIM, 2, dtype=F32) / ROT_DIM)
    ang = positions.astype(F32)[:, None] * inv_freq
    c, sn = jnp.cos(ang), jnp.sin(ang)
    rest = B_HEAD_DIM - ROT_DIM
    c64 = jnp.concatenate([c, c, jnp.ones((s, rest), F32)], axis=1)
    s64 = jnp.concatenate([-sn, sn, jnp.zeros((s, rest), F32)], axis=1)
    return jnp.tile(c64, (1, LANES // B_HEAD_DIM)), jnp.tile(s64, (1, LANES // B_HEAD_DIM))


def _local_step(x, positions, target, w, projections_of, projections_ahead, projection_grads, norm_grads_done):
    cq, sq = _rope_tables(positions)
    saved = []
    xin, xin_b = x, x.astype(MXU_DTYPE)
    for layer in range(DEPTH):
        j = layer // 2
        kind = 'even' if layer % 2 == 0 else 'odd'
        w_in, w_out = projections_of(layer, xin_b)
        h = _mm_in(xin_b, w_in, W_IN_TRANSPOSED[kind], "mm_" + kind + "_in")
        if kind == 'even':
            mix = _even_fwd(h, cq, sq, w['even_a_ln_g'][j], w['even_a_ln_b'][j], w['even_a_ws'][j],
                            w['even_a_bs'][j].T, w['even_b_sinks'][j])
        else:
            mix = _lru_fwd(h, w['odd_conv_w'][j], w['odd_conv_b'][j], w['odd_w_a'][j], w['odd_b_a'][j],
                           w['odd_w_x'][j], w['odd_b_x'][j], w['odd_lam'][j])
            mix = _pool_fwd(h, mix, w['odd_w_pool'][j], w['odd_d_scale'][j])
        if layer == DEPTH - 1:
            z, out, out_b = _mm_out_norm(mix, w_out, xin, w[kind + '_ln_g'][j], w[kind + '_ln_b'][j],
                                         "mm_out_norm_loss", target=target)
        else:
            z, out, out_b = _mm_out_norm(mix, w_out, xin, w[kind + '_ln_g'][j], w[kind + '_ln_b'][j], "mm_out_norm",
                                         after=projections_ahead(layer + 1, mix))
        saved.append((xin_b, h, mix, z, w_in, w_out))
        xin, xin_b = out, out_b

    dout, loss = xin, xin_b
    per_layer = {n: [None, None] for n in SMALL_NAMES}
    behind = loss
    for layer in reversed(range(DEPTH)):
        j = layer // 2
        kind = 'even' if layer % 2 == 0 else 'odd'
        xin_b, h, mix, z, w_in, w_out = saved[layer]
        dz, dz_b, dg, db = _ln_bwd(dout, z, w[kind + '_ln_g'][j], behind)
        per_layer[kind + '_ln_g'][j], per_layer[kind + '_ln_b'][j] = dg[0], db[0]
        dmix = _mm_dmix(dz_b, w_out, norm_grads_done(layer, dz_b), "mm_dmix")
        dw_out = _mm_dw_out(mix, dz_b, "mm_dw_out")
        if kind == 'even':
            dh, dlg, dlb, dws, dbst, dsink = _even_bwd(
                h, dmix, cq, sq, w['even_a_ln_g'][j], w['even_a_ln_b'][j], w['even_a_ws'][j], w['even_a_bs'][j].T,
                w['even_b_sinks'][j])
            for n, val in (('even_a_ln_g', dlg[0]), ('even_a_ln_b', dlb[0]), ('even_a_ws', dws),
                           ('even_a_bs', dbst.T), ('even_b_sinks', dsink[0])):
                per_layer[n][j] = val
        else:
            dh, dcw, dcb, dwa, dba, dwx, dbx, dlam = _lru_bwd(
                h, dmix, w['odd_conv_w'][j], w['odd_conv_b'][j], w['odd_w_a'][j], w['odd_b_a'][j], w['odd_w_x'][j],
                w['odd_b_x'][j], w['odd_lam'][j])
            dh, dwp, dsc = _pool_bwd(h, dmix, dh, w['odd_w_pool'][j], w['odd_d_scale'][j])
            for n, val in (('odd_conv_w', dcw), ('odd_conv_b', dcb[0]), ('odd_w_a', dwa), ('odd_b_a', dba[0]),
                           ('odd_w_x', dwx), ('odd_b_x', dbx[0]), ('odd_lam', dlam[0]), ('odd_w_pool', dwp),
                           ('odd_d_scale', dsc[0])):
                per_layer[n][j] = val
        dout = _mm_dx(dh, w_in, dz, W_IN_TRANSPOSED[kind], "mm_dx_" + kind)
        dw_in = _mm_dw_in(xin_b, dh, W_IN_TRANSPOSED[kind], "mm_dw_in_" + kind)
        behind = projection_grads(layer, dw_in, dw_out)
    return loss, dout, {n: jnp.stack(v) for n, v in per_layer.items()}


ANY = pl.BlockSpec(memory_space=pl.ANY)


def _where_am_i():
    return lax.axis_index("x"), lax.axis_index("y"), lax.axis_index("c")


def _other_chips(x, y):
    return [(1 - x, y), (x, 1 - y), (1 - x, 1 - y)]


PACKED_ROWS = 16


def _half_rows(rows, which):
    half = rows // 2
    assert half % PACKED_ROWS == 0, rows
    return pl.ds(pl.multiple_of(which * half, PACKED_ROWS), half)


def _sibling_swap(bufs, name, *, other_half_of_axis1=False, after=()):
    n, na = len(bufs), len(after)
    out_shapes = [(b.shape[0], b.shape[1] // 2, b.shape[2]) if other_half_of_axis1 else b.shape for b in bufs]

    def body(*refs):
        x_refs, out_refs, (send_sems, recv_sems) = refs[:n], refs[n + na:2 * n + na], refs[2 * n + na:]
        x, y, c = _where_am_i()
        copies = []
        for p in range(n):
            src = x_refs[p].at[:, _half_rows(bufs[p].shape[1], 1 - c), :] if other_half_of_axis1 else x_refs[p]
            copies.append(pltpu.make_async_remote_copy(
                src_ref=src, dst_ref=out_refs[p], send_sem=send_sems.at[p], recv_sem=recv_sems.at[p],
                device_id=(x, y, 1 - c), device_id_type=MESH))
            copies[-1].start()
        for cp in copies:
            cp.wait()

    return pl.pallas_call(
        body, name=name, in_specs=[ANY] * (n + na), out_specs=[ANY] * n,
        out_shape=[jax.ShapeDtypeStruct(s, b.dtype) for s, b in zip(out_shapes, bufs)],
        scratch_shapes=[pltpu.SemaphoreType.DMA((n,)), pltpu.SemaphoreType.DMA((n,))],
    )(*bufs, *after)


HBM = pl.BlockSpec(memory_space=pltpu.HBM)
SEM = pl.BlockSpec(memory_space=pltpu.SEMAPHORE)
IN_FLIGHT = pltpu.CompilerParams(has_side_effects=pltpu.SideEffectType.DATAFLOW_SIDE_EFFECTING)


IN_PLACE = ('forward',)


def _peers(kind, x, y, c):
    if kind in ('swap', 'join'):
        return [(x, y, 1 - c)]
    return [(x, y, 1 - c) if kind == 'forward' else (px, py, c) for px, py in _other_chips(x, y)]


def _n_peers(kind):
    return 1 if kind in ('swap', 'join') else N_CHIPS - 1


def _land_shape(kind, a):
    if kind == 'gather':
        return (N_CHIPS,) + a.shape
    return (a.shape[0], a.shape[1] // 2, a.shape[2]) if kind == 'swap' else a.shape


def _ici_copy(kind, src_ref, land_ref, send_sems, recv_sems, k, peer, x, y, c, arriving=False):
    me = 2 * x + y
    if kind == 'gather':
        half = _half_rows(src_ref.shape[0], c)
        src, dst = src_ref.at[half, :], land_ref.at[me, half, :]
    elif kind == 'scatter':
        src, dst = src_ref.at[2 * peer[0] + peer[1]], land_ref.at[me]
    elif kind == 'swap':
        src, dst = src_ref.at[:, _half_rows(src_ref.shape[1], 1 - c), :], land_ref
    elif kind == 'join':
        src, dst = src_ref, land_ref
    else:
        px, py = _other_chips(x, y)[k % (N_CHIPS - 1)]
        rows = src_ref.at[2 * px + py, _half_rows(src_ref.shape[1], 1 - c if arriving else c), :]
        src, dst = rows, rows
    return pltpu.make_async_remote_copy(src_ref=src, dst_ref=dst, send_sem=send_sems.at[k], recv_sem=recv_sems.at[k],
                                        device_id=peer, device_id_type=MESH)


def _ici_start(kind, groups, name, after=()):
    flat = [a for g in groups for a in g]
    n, ng, np_, na = len(flat), len(groups), _n_peers(kind), len(after)
    nl = 0 if kind in IN_PLACE else n
    land_shapes = [_land_shape(kind, a) for a in flat[:nl]]

    def body(*refs):
        srcs, sems, token = refs[:n], refs[n + nl + na:n + nl + na + 2 * ng], refs[-1]
        lands = refs[n:n + nl] if nl else srcs
        x, y, c = _where_am_i()
        p = 0
        for gi, g in enumerate(groups):
            for q in range(len(g)):
                for k, peer in enumerate(_peers(kind, x, y, c)):
                    _ici_copy(kind, srcs[p], lands[p], sems[2 * gi], sems[2 * gi + 1], np_ * q + k, peer, x, y,
                              c).start()
                p += 1
        token[...] = jnp.zeros_like(token)

    sem_shapes = [pltpu.SemaphoreType.DMA((np_ * len(g),)) for g in groups for _ in range(2)]
    outs = pl.pallas_call(
        body, name=name, in_specs=[HBM] * (n + nl) + [ANY] * na,
        out_specs=[SEM] * (2 * ng) + [HBM] * (n + nl) + [pl.BlockSpec(memory_space=pltpu.VMEM)],
        out_shape=sem_shapes + [pltpu.HBM(a.shape, a.dtype) for a in flat]
        + [pltpu.HBM(s, a.dtype) for s, a in zip(land_shapes, flat)] + [jax.ShapeDtypeStruct((8, LANES), F32)],
        input_output_aliases={i: 2 * ng + i for i in range(n + nl)}, compiler_params=IN_FLIGHT,
    )(*[pltpu.with_memory_space_constraint(a, pltpu.HBM) for a in flat],
      *[pltpu.with_memory_space_constraint(lax.empty(s, a.dtype), pltpu.HBM) for s, a in zip(land_shapes, flat)],
      *after)
    sems, srcs, token = outs[:2 * ng], outs[2 * ng:2 * ng + n], outs[-1]
    lands = outs[2 * ng + n:2 * ng + n + nl] if nl else srcs
    started, p = [], 0
    for gi, g in enumerate(groups):
        started.append((sems[2 * gi], sems[2 * gi + 1], srcs[p:p + len(g)], lands[p:p + len(g)]))
        p += len(g)
    return started, token


def _ici_wait(kind, started, after, name):
    send_sems, recv_sems, srcs, lands = started
    n, np_ = len(srcs), _n_peers(kind)
    nl = 0 if kind in IN_PLACE else n
    after = list(after)

    def body(*refs):
        src_refs, send_ref, recv_ref = refs[:n], refs[n + nl], refs[n + nl + 1]
        land_refs = refs[n:n + nl] if nl else src_refs
        x, y, c = _where_am_i()
        for q in range(n):
            for k, peer in enumerate(_peers(kind, x, y, c)):
                args = (kind, src_refs[q], land_refs[q], send_ref, recv_ref, np_ * q + k, peer, x, y, c)
                _ici_copy(*args).wait_send()
                _ici_copy(*args, arriving=True).wait_recv()

    arrays = list(srcs) + (list(lands) if nl else [])
    outs = pl.pallas_call(
        body, name=name, in_specs=[HBM] * (n + nl) + [SEM, SEM] + [ANY] * len(after), out_specs=[HBM] * (n + nl),
        out_shape=[pltpu.HBM(a.shape, a.dtype) for a in arrays],
        input_output_aliases={i: i for i in range(n + nl)}, compiler_params=IN_FLIGHT,
    )(*arrays, send_sems, recv_sems, *after)
    return outs[:n], (outs[n:] if nl else outs[:n])


def _forward_to_sibling(lands, name):
    n = len(lands)

    def body(*refs):
        out_refs, (send_sems, recv_sems) = refs[n:2 * n], refs[2 * n:]
        x, y, c = _where_am_i()

        def copy(q, k, peer, half_of):
            rows = out_refs[q].at[2 * peer[0] + peer[1], _half_rows(lands[q].shape[1], half_of), :]
            return pltpu.make_async_remote_copy(
                src_ref=rows, dst_ref=rows, send_sem=send_sems.at[3 * q + k], recv_sem=recv_sems.at[3 * q + k],
                device_id=(x, y, 1 - c), device_id_type=MESH)

        sends = [copy(q, k, peer, c) for q in range(n) for k, peer in enumerate(_other_chips(x, y))]
        for cp in sends:
            cp.start()
        for q in range(n):
            for k, peer in enumerate(_other_chips(x, y)):
                copy(q, k, peer, 1 - c).wait_recv()
        for cp in sends:
            cp.wait_send()

    return pl.pallas_call(
        body, name=name, in_specs=[ANY] * n, out_specs=[ANY] * n,
        out_shape=[jax.ShapeDtypeStruct(a.shape, a.dtype) for a in lands],
        input_output_aliases={i: i for i in range(n)},
        scratch_shapes=[pltpu.SemaphoreType.DMA((3 * n,)), pltpu.SemaphoreType.DMA((3 * n,))],
    )(*lands)


def _with_own_slot(lands, own, chip):
    out = []
    for land, mine in zip(lands, own):
        mine = lax.dynamic_slice_in_dim(mine, chip, 1, axis=0) if mine.ndim == land.ndim else mine[None]
        out.append(lax.dynamic_update_slice(land, mine, (chip,) + (0,) * (land.ndim - 1)))
    return out


def _row_tile(rows, row_bytes, n_bufs, budget=24 * 1024 * 1024):
    best = None
    for t in range(PACKED_ROWS, rows + 1, PACKED_ROWS):
        if rows % t == 0 and 2 * n_bufs * t * row_bytes <= budget:
            best = t
    assert best is not None, (rows, row_bytes)
    return best


def _add_halves(full, recv, core, out_dtype, name):
    n, h, cols = recv.shape
    tr = _row_tile(h, cols * 4, 3)
    nt = h // tr

    def body(core_ref, a_ref, b_ref, o_ref):
        o_ref[...] = (a_ref[...].astype(F32) + b_ref[...].astype(F32)).astype(out_dtype)

    blk = (None, tr, cols)
    return pl.pallas_call(
        body, name=name,
        grid_spec=pltpu.PrefetchScalarGridSpec(
            num_scalar_prefetch=1, grid=(n, nt),
            in_specs=[pl.BlockSpec(blk, lambda j, i, core_ref: (j, core_ref[0] * nt + i, 0)),
                      pl.BlockSpec(blk, lambda j, i, core_ref: (j, i, 0))],
            out_specs=pl.BlockSpec(blk, lambda j, i, core_ref: (j, i, 0))),
        out_shape=jax.ShapeDtypeStruct((n, h, cols), out_dtype),
        compiler_params=_params(("parallel", "parallel")),
    )(core.reshape(1), full, recv)


def _add2(a, b, name):
    r, cols = a.shape
    tr = _row_tile(r, cols * 4, 3)

    def body(a_ref, b_ref, o_ref):
        o_ref[...] = a_ref[...] + b_ref[...]

    blk = pl.BlockSpec((tr, cols), lambda i: (i, 0))
    return pl.pallas_call(body, name=name, grid=(r // tr,), in_specs=[blk, blk], out_specs=blk,
                          out_shape=jax.ShapeDtypeStruct(a.shape, a.dtype), compiler_params=_params(("parallel",)))(a, b)


def _sum_slots(parts, name):
    n, h, cols = parts.shape
    tr = _row_tile(h, cols * 4, n + 1)

    def body(p_ref, o_ref):
        acc = p_ref[0].astype(F32)
        for k in range(1, n):
            acc = acc + p_ref[k].astype(F32)
        o_ref[...] = acc

    return pl.pallas_call(
        body, name=name, grid=(h // tr,), in_specs=[pl.BlockSpec((n, tr, cols), lambda i: (0, i, 0))],
        out_specs=pl.BlockSpec((tr, cols), lambda i: (i, 0)), out_shape=jax.ShapeDtypeStruct((h, cols), F32),
        compiler_params=_params(("parallel",)),
    )(parts)


def _adamw(w, g, m, v, name):
    shape = w.shape
    cols = shape[-1]
    rows = math.prod(shape[:-1])
    tr = rows
    if rows % 8 == 0:
        tr = 8
        for t in range(8, rows + 1, 8):
            if rows % t == 0 and t * cols * 4 <= 1536 * 1024:
                tr = t
    c1 = 1.0 - ADAM_B1 ** ADAM_STEP
    c2 = 1.0 - ADAM_B2 ** ADAM_STEP

    def body(w_ref, g_ref, m_ref, v_ref, d_ref, nm_ref, nv_ref):
        gg = g_ref[...]
        nm = ADAM_B1 * m_ref[...] + (1.0 - ADAM_B1) * gg
        nv = ADAM_B2 * v_ref[...] + (1.0 - ADAM_B2) * (gg * gg)
        d_ref[...] = -ADAM_LR * ((nm / c1) / (jnp.sqrt(nv / c2) + ADAM_EPS) + ADAM_WD * w_ref[...])
        nm_ref[...] = nm
        nv_ref[...] = nv

    blk = pl.BlockSpec((tr, cols), lambda i: (i, 0))
    flat = jax.ShapeDtypeStruct((rows, cols), F32)
    outs = pl.pallas_call(
        body, name=name, grid=(rows // tr,), in_specs=[blk] * 4, out_specs=[blk] * 3, out_shape=[flat] * 3,
        compiler_params=_params(("parallel",)),
    )(*[a.reshape(rows, cols) for a in (w, g, m, v)])
    return [o.reshape(shape) for o in outs]


def _adamw_update(w, g, m, v):
    c1 = 1.0 - ADAM_B1 ** ADAM_STEP
    c2 = 1.0 - ADAM_B2 ** ADAM_STEP
    nm = ADAM_B1 * m + (1.0 - ADAM_B1) * g
    nv = ADAM_B2 * v + (1.0 - ADAM_B2) * (g * g)
    return -ADAM_LR * ((nm / c1) / (jnp.sqrt(nv / c2) + ADAM_EPS) + ADAM_WD * w), nm, nv


def _adamw_projection(w, m, v, mine, theirs, core, name):
    _, r, cols = w.shape
    h = r // 2
    tr = _row_tile(h, cols * 4, 12, budget=40 * 1024 * 1024)
    nt = h // tr

    def body(core_ref, w_ref, m_ref, v_ref, a0, b0, a1, b1, g_ref, d_ref, nm_ref, nv_ref):
        layer, half = pl.program_id(0), pl.program_id(1)
        own = half == core_ref[0]
        g = jnp.where(layer == 0, jnp.where(own, a0[...], b0[...]), jnp.where(own, a1[...], b1[...]))
        g_ref[...] = g
        d_ref[...], nm_ref[...], nv_ref[...] = _adamw_update(w_ref[...], g, m_ref[...], v_ref[...])

    def piece(layer, own):
        def index(l, hh, i, core_ref):
            used = (l == layer) & ((hh == core_ref[0]) == own)
            return (jnp.where(used, i, 0), 0)
        return pl.BlockSpec((tr, cols), index)

    whole = pl.BlockSpec((None, tr, cols), lambda l, hh, i, core_ref: (l, hh * nt + i, 0))
    return pl.pallas_call(
        body, name=name,
        grid_spec=pltpu.PrefetchScalarGridSpec(
            num_scalar_prefetch=1, grid=(2, 2, nt),
            in_specs=[whole] * 3 + [piece(0, True), piece(0, False), piece(1, True), piece(1, False)],
            out_specs=[whole] * 4),
        out_shape=[jax.ShapeDtypeStruct(w.shape, F32)] * 4,
        compiler_params=_params(("arbitrary", "arbitrary", "arbitrary")),
    )(core.reshape(1), w, m, v, mine[0], theirs[0], mine[1], theirs[1])


def _chip_shape(name):
    shape = list(FULL_SHAPES[name])
    axis = BIG_SHARD_AXIS.get(name, SMALL_SHARD_AXIS.get(name))
    if axis is not None:
        shape[axis] //= N_CHIPS
    return tuple(shape)


def _round_up(n, k):
    return -(-n // k) * k


SHARDED_SMALL_WORDS = sum(math.prod(_chip_shape(n)) for n in SMALL_SHARDED)
SHARDED_SMALL_ROWS = _round_up(-(-SHARDED_SMALL_WORDS // PACK_COLS), 32)
SMALL_MATRICES = ['even_a_ws', 'odd_w_a', 'odd_w_x', 'odd_w_pool']
SMALL_VECTORS = [n for n in SMALL_NAMES if n not in SMALL_MATRICES]
VECTOR_ROWS = _round_up(-(-(sum(math.prod(FULL_SHAPES[n]) for n in SMALL_VECTORS) + 1) // PACK_COLS), 32)


def _pack_rows(flat, rows):
    return jnp.pad(flat, (0, rows * PACK_COLS - flat.shape[0])).reshape(rows, PACK_COLS)


def _pack_small_sharded(shards):
    return _pack_rows(jnp.concatenate([shards[n].reshape(-1) for n in SMALL_SHARDED]), SHARDED_SMALL_ROWS)


def _unpack_small_sharded(gathered_small):
    out = {}
    words = gathered_small.reshape(N_CHIPS, -1)
    w0 = 0
    for n in SMALL_SHARDED:
        size = math.prod(_chip_shape(n))
        blocks = words[:, w0:w0 + size].reshape((N_CHIPS,) + _chip_shape(n))
        out[n] = jnp.concatenate([blocks[j] for j in range(N_CHIPS)], axis=SMALL_SHARD_AXIS[n])
        w0 += size
    return out


def _small_grad_parts(grads, loss):
    words = [grads[n].reshape(-1) for n in SMALL_VECTORS] + [loss.reshape(-1)]
    return ([_pack_rows(jnp.concatenate(words), VECTOR_ROWS)]
            + [grads[n].reshape(-1, FULL_SHAPES[n][-1]) for n in SMALL_MATRICES])


def _own_block(name, full, chip):
    if name not in SMALL_SHARD_AXIS:
        return full
    axis = SMALL_SHARD_AXIS[name]
    width = FULL_SHAPES[name][axis] // N_CHIPS
    return lax.dynamic_slice_in_dim(full, chip * width, width, axis)


def _from_small_grad_parts(totals, chip):
    flat = totals[0].reshape(-1)
    out = {}
    w0 = 0
    for n in SMALL_VECTORS:
        size = math.prod(FULL_SHAPES[n])
        out[n] = _own_block(n, flat[w0:w0 + size].reshape(FULL_SHAPES[n]), chip)
        w0 += size
    for n, total in zip(SMALL_MATRICES, totals[1:]):
        out[n] = _own_block(n, total.reshape(FULL_SHAPES[n]), chip)
    return out, flat[w0]


def kernel(x, positions, even_w_in, even_a_ln_g, even_a_ln_b, even_a_ws, even_a_bs, even_b_sinks, even_w_out, even_ln_g, even_ln_b, odd_w_in, odd_conv_w, odd_conv_b, odd_w_a, odd_b_a, odd_w_x, odd_b_x, odd_lam, odd_w_pool, odd_d_scale, odd_w_out, odd_ln_g, odd_ln_b, loss_target, m_even_w_in, m_even_a_ln_g, m_even_a_ln_b, m_even_a_ws, m_even_a_bs, m_even_b_sinks, m_even_w_out, m_even_ln_g, m_even_ln_b, m_odd_w_in, m_odd_conv_w, m_odd_conv_b, m_odd_w_a, m_odd_b_a, m_odd_w_x, m_odd_b_x, m_odd_lam, m_odd_w_pool, m_odd_d_scale, m_odd_w_out, m_odd_ln_g, m_odd_ln_b, v_even_w_in, v_even_a_ln_g, v_even_a_ln_b, v_even_a_ws, v_even_a_bs, v_even_b_sinks, v_even_w_out, v_even_ln_g, v_even_ln_b, v_odd_w_in, v_odd_conv_w, v_odd_conv_b, v_odd_w_a, v_odd_b_a, v_odd_w_x, v_odd_b_x, v_odd_lam, v_odd_w_pool, v_odd_d_scale, v_odd_w_out, v_odd_ln_g, v_odd_ln_b):
    local = dict(zip(WEIGHT_NAMES, (even_w_in, even_a_ln_g, even_a_ln_b, even_a_ws, even_a_bs, even_b_sinks, even_w_out, even_ln_g, even_ln_b, odd_w_in, odd_conv_w, odd_conv_b, odd_w_a, odd_b_a, odd_w_x, odd_b_x, odd_lam, odd_w_pool, odd_d_scale, odd_w_out, odd_ln_g, odd_ln_b)))
    mom = dict(zip(WEIGHT_NAMES, (m_even_w_in, m_even_a_ln_g, m_even_a_ln_b, m_even_a_ws, m_even_a_bs, m_even_b_sinks, m_even_w_out, m_even_ln_g, m_even_ln_b, m_odd_w_in, m_odd_conv_w, m_odd_conv_b, m_odd_w_a, m_odd_b_a, m_odd_w_x, m_odd_b_x, m_odd_lam, m_odd_w_pool, m_odd_d_scale, m_odd_w_out, m_odd_ln_g, m_odd_ln_b)))
    vel = dict(zip(WEIGHT_NAMES, (v_even_w_in, v_even_a_ln_g, v_even_a_ln_b, v_even_a_ws, v_even_a_bs, v_even_b_sinks, v_even_w_out, v_even_ln_g, v_even_ln_b, v_odd_w_in, v_odd_conv_w, v_odd_conv_b, v_odd_w_a, v_odd_b_a, v_odd_w_x, v_odd_b_x, v_odd_lam, v_odd_w_pool, v_odd_d_scale, v_odd_w_out, v_odd_ln_g, v_odd_ln_b)))
    mx, my, core = _where_am_i()
    chip = 2 * mx + my

    stored = lambda n, a: jnp.swapaxes(a, 1, 2) if n == 'even_w_in' and W_IN_TRANSPOSED['even'] else a
    names_of = lambda layer: [('even' if layer % 2 == 0 else 'odd') + s for s in ('_w_in', '_w_out')]

    shards = [[stored(n, local[n])[layer // 2].astype(MXU_DTYPE) for n in names_of(layer)] for layer in range(DEPTH)]
    shards[0].insert(0, _pack_small_sharded(local))
    gathers, first_token = _ici_start('gather', shards[:1], "gather_start_first")
    later, gather_token = _ici_start('gather', shards[1:], "gather_start", after=[first_token])
    gathers += later

    own, lands = _ici_wait('gather', gathers[0], [first_token, gather_token], "gather_wait_0")
    first = _with_own_slot(_forward_to_sibling(lands, "gather_forward_0"), own, chip)
    weights = dict(local)
    weights.update(_unpack_small_sharded(first[0]))
    passing = {}

    def projections_ahead(layer, after):
        own, lands = _ici_wait('gather', gathers[layer], [after], "gather_wait_%d" % layer)
        (forward,), token = _ici_start('forward', [lands], "forward_start_%d" % layer)
        passing[layer] = own, forward
        return token

    def projections_of(layer, after):
        if layer == 0:
            return first[1:]
        own, forward = passing.pop(layer)
        lands, _ = _ici_wait('forward', forward, [after], "forward_wait_%d" % layer)
        return _with_own_slot(lands, own, chip)

    swapping, in_flight, mine, started = [], [], {}, {}

    def land(*after):
        layer, scatter = in_flight.pop()
        parts, lands = _ici_wait('scatter', scatter, after, "scatter_wait_%d" % layer)
        for n, arrived in zip(names_of(layer), _with_own_slot(lands, parts, chip)):
            mine[n, layer // 2] = _sum_slots(arrived, "big_sum_%s_%d" % (n, layer // 2))

    odd_pieces = [(n, l) for n in ('odd_w_in', 'odd_w_out') for l in range(2)]
    joining = []

    def projection_grads(layer, dw_in, dw_out):
        behind = []
        if in_flight:
            land(dw_in)
        if layer == 0:
            (join,), token = _ici_start('join', [[mine[p] for p in odd_pieces]], "join_odd_start")
            joining.append(join)
            behind = [token]
        (swap,), token = _ici_start('swap', [[dw_in, dw_out]], "presum_start_%d" % layer, after=behind)
        swapping.append((layer, swap))
        return presum_and_scatter(token) if layer == 0 else token

    def presum_and_scatter(*after):
        layer, swap = swapping.pop()
        partial, from_sibling = _ici_wait('swap', swap, after, "presum_wait_%d" % layer)
        chip_part = [_add_halves(g, r, core, MXU_DTYPE, "big_presum_add_%s_%d" % (n, layer // 2))
                     for g, r, n in zip(partial, from_sibling, names_of(layer))]
        (scatter,), started[layer] = _ici_start('scatter', [chip_part], "scatter_start_%d" % layer)
        in_flight.append((layer, scatter))
        return started[layer]

    norm_grads_done = lambda layer, dz: presum_and_scatter(dz) if swapping else dz
    loss, grad_x, grads = _local_step(x[0], positions[0], loss_target[0], weights, projections_of, projections_ahead,
                                      projection_grads, norm_grads_done)

    small = _small_grad_parts(grads, loss)
    chip_sums = [_add2(a, b, "small_presum_add_%d" % i)
                 for i, (a, b) in enumerate(zip(small, _sibling_swap(small, "small_swap", after=[started[0]])))]
    (small_gather,), small_token = _ici_start('gather', [chip_sums], "small_gather_start")

    reduced, deltas, new_m, new_v = {}, {}, {}, {}

    def adamw_projections(kind, after):
        names = [kind + '_w_in', kind + '_w_out']
        pieces = [(n, l) for n in names for l in range(2)]
        if kind == 'odd':
            halves, theirs = _ici_wait('join', joining.pop(), after, "join_odd_wait")
            mine.update(zip(pieces, halves))
        else:
            theirs = _sibling_swap([mine[p] for p in pieces], "big_join_" + kind, after=after)
        theirs = dict(zip(pieces, theirs))
        done = []
        for n in names:
            outs = _adamw_projection(stored(n, local[n]), stored(n, mom[n]), stored(n, vel[n]),
                                     [mine[n, l] for l in range(2)], [theirs[n, l] for l in range(2)], core,
                                     "adamw_" + n)
            reduced[n], deltas[n], new_m[n], new_v[n] = [stored(n, o) for o in outs]
            done.append(outs[-1])
        return done

    done = adamw_projections('odd', [small_token])
    land(grad_x, *done)
    done = adamw_projections('even', [])

    own, lands = _ici_wait('gather', small_gather, done, "small_gather_wait")
    of_chips = _with_own_slot(_forward_to_sibling(lands, "small_gather_forward"), own, chip)
    small_totals, total_loss = _from_small_grad_parts(
        [_sum_slots(g, "small_sum_%d" % i) for i, g in enumerate(of_chips)], chip)
    reduced.update(small_totals)
    for n in SMALL_NAMES:
        deltas[n], new_m[n], new_v[n] = _adamw(local[n], reduced[n], mom[n], vel[n], "adamw_" + n)

    return (total_loss, grad_x[None], *[reduced[n] for n in WEIGHT_NAMES], *[deltas[n] for n in WEIGHT_NAMES],
            *[new_m[n] for n in WEIGHT_NAMES], *[new_v[n] for n in WEIGHT_NAMES])
```

```python
import math

import jax
import jax.numpy as jnp
from jax import lax
from jax.experimental import pallas as pl
from jax.experimental.pallas import tpu as pltpu

F32 = jnp.float32
BF16 = jnp.bfloat16
MXU_DTYPE = jnp.bfloat16

D_MODEL = 2048
DEPTH = 4
A_WIDTH = 1024
A_GROUPS = 8
CHUNK = 128
B_HEAD_DIM = 64
B_Q_HEADS = 16
B_KV_HEADS = 2
WINDOW = 128
ROT_DIM = 16
ROPE_THETA = 500000.0
C_WIDTH = 1024
C_HEADS = 8
CONV_WIDTH = 4
LRU_C = 8.0
D_WIDTH = 1024
D_GROUPS = 4
D_GROUP_DIM = 256
EVEN_IN = 5376
ODD_IN = 4096
DN_ALPHA = (2 * DEPTH) ** 0.25
LN_EPS = 1e-5
ATTN_SCALE = B_HEAD_DIM ** -0.5
NEG_BIG = -1e30

ADAM_LR = 0.001
ADAM_B1 = 0.9
ADAM_B2 = 0.999
ADAM_EPS = 1e-08
ADAM_WD = 0.01
ADAM_STEP = 10

LANES = 128
VMEM_LIMIT = 56 * 1024 * 1024
N_CHIPS = 4
MESH = pl.DeviceIdType.MESH

WEIGHT_NAMES = ['even_w_in', 'even_a_ln_g', 'even_a_ln_b', 'even_a_ws', 'even_a_bs', 'even_b_sinks', 'even_w_out',
                'even_ln_g', 'even_ln_b', 'odd_w_in', 'odd_conv_w', 'odd_conv_b', 'odd_w_a', 'odd_b_a', 'odd_w_x',
                'odd_b_x', 'odd_lam', 'odd_w_pool', 'odd_d_scale', 'odd_w_out', 'odd_ln_g', 'odd_ln_b']
BIG_SHARD_AXIS = {'even_w_in': 2, 'even_w_out': 1, 'odd_w_in': 2, 'odd_w_out': 1}
SMALL_SHARD_AXIS = {'odd_conv_w': 2, 'odd_conv_b': 1, 'odd_b_a': 1, 'odd_b_x': 1, 'odd_lam': 1, 'odd_w_pool': 2,
                    'odd_d_scale': 1, 'odd_ln_g': 1, 'odd_ln_b': 1}
FULL_SHAPES = {
    'even_w_in': (2, 2048, 5376), 'even_a_ln_g': (2, 1024), 'even_a_ln_b': (2, 1024), 'even_a_ws': (2, 8, 128, 128),
    'even_a_bs': (2, 8, 128), 'even_b_sinks': (2, 16), 'even_w_out': (2, 2048, 2048), 'even_ln_g': (2, 2048),
    'even_ln_b': (2, 2048), 'odd_w_in': (2, 2048, 4096), 'odd_conv_w': (2, 4, 1024), 'odd_conv_b': (2, 1024),
    'odd_w_a': (2, 8, 128, 128), 'odd_b_a': (2, 1024), 'odd_w_x': (2, 8, 128, 128), 'odd_b_x': (2, 1024),
    'odd_lam': (2, 1024), 'odd_w_pool': (2, 4, 256, 256), 'odd_d_scale': (2, 1024), 'odd_w_out': (2, 2048, 2048),
    'odd_ln_g': (2, 2048), 'odd_ln_b': (2, 2048)}
BIG_NAMES = ['even_w_in', 'even_w_out', 'odd_w_in', 'odd_w_out']
SMALL_SHARDED = ['odd_conv_w', 'odd_conv_b', 'odd_b_a', 'odd_b_x', 'odd_lam', 'odd_d_scale', 'odd_ln_g', 'odd_ln_b',
                 'odd_w_pool']
SMALL_NAMES = [n for n in WEIGHT_NAMES if n not in BIG_NAMES]
W_IN_TRANSPOSED = {'even': True, 'odd': False}
PACK_COLS = 1024


def _params(sem):
    return pltpu.CompilerParams(dimension_semantics=sem, vmem_limit_bytes=VMEM_LIMIT)


def _dot(a, b):
    return lax.dot_general(a, b, (((1,), (0,)), ((), ())), preferred_element_type=F32)


def _dot_nt(a, b):
    return lax.dot_general(a, b, (((1,), (1,)), ((), ())), preferred_element_type=F32)


def _dot_tn(a, b):
    return lax.dot_general(a, b, (((0,), (0,)), ((), ())), preferred_element_type=F32)


def _sigmoid(x):
    return 0.5 * jnp.tanh(0.5 * x) + 0.5


def _silu_and_grad(x):
    s = _sigmoid(x)
    return x * s, s * (1.0 + x * (1.0 - s))


def _mm_call(body, name, grid, in_specs, out_spec, out_shape, args):
    return pl.pallas_call(body, name=name, grid=grid, in_specs=in_specs, out_specs=out_spec, out_shape=out_shape,
                          compiler_params=_params(("parallel",) * len(grid)))(*args)


def _resident(block, index_map):
    return pl.BlockSpec(block, index_map, pipeline_mode=pl.Buffered(1))


ROWS_OF_WT = 768


def _mm_in(x, wg, transposed, name):
    s, k = x.shape
    x_spec = _resident((s, k), lambda j: (0, 0))

    def body(x_ref, w_ref, o_ref):
        o_ref[...] = (_dot_nt if transposed else _dot)(x_ref[...], w_ref[...])

    if transposed:
        wt = wg.reshape(-1, k)
        n, tn = wt.shape[0], ROWS_OF_WT
        return _mm_call(body, name, (n // tn,), [x_spec, pl.BlockSpec((tn, k), lambda j: (j, 0))],
                        pl.BlockSpec((s, tn), lambda j: (0, j)), jax.ShapeDtypeStruct((s, n), F32), (x, wt))
    nc = wg.shape[2]
    return _mm_call(body, name, (N_CHIPS,), [x_spec, pl.BlockSpec((None, k, nc), lambda j: (j, 0, 0))],
                    pl.BlockSpec((None, s, nc), lambda j: (j, 0, 0)),
                    jax.ShapeDtypeStruct((N_CHIPS, s, nc), F32), (x, wg))


def _mm_dx(dh, wg, dz, transposed, name):
    s, d = dz.shape
    tm, tn = min(1024, s), 512
    res = pl.BlockSpec((tm, tn), lambda i, j: (i, j))
    if transposed:
        wt = wg.reshape(-1, d)
        n = wt.shape[0]

        def body(a_ref, w_ref, r_ref, o_ref):
            o_ref[...] = DN_ALPHA * r_ref[...] + _dot(a_ref[...], w_ref[...])

        return _mm_call(body, name, (s // tm, d // tn),
                        [pl.BlockSpec((tm, n), lambda i, j: (i, 0)), pl.BlockSpec((n, tn), lambda i, j: (0, j)), res],
                        res, jax.ShapeDtypeStruct((s, d), F32), (dh, wt, dz))
    nc = dh.shape[2]

    def body(a_ref, w_ref, r_ref, o_ref):
        acc = DN_ALPHA * r_ref[...]
        for j in range(N_CHIPS):
            acc = acc + _dot_nt(a_ref[j], w_ref[j])
        o_ref[...] = acc

    return _mm_call(body, name, (s // tm, d // tn),
                    [pl.BlockSpec((N_CHIPS, tm, nc), lambda i, j: (0, i, 0)),
                     pl.BlockSpec((N_CHIPS, tn, nc), lambda i, j: (0, j, 0)), res],
                    res, jax.ShapeDtypeStruct((s, d), F32), (dh, wg, dz))


def _mm_dw_in(x, dh, transposed, name):
    s, d = x.shape
    x_spec = _resident((s, d), lambda j: (0, 0))

    def body(a_ref, b_ref, o_ref):
        o_ref[...] = _dot_tn(a_ref[...], b_ref[...]).astype(o_ref.dtype)

    if transposed:
        n, tn = dh.shape[1], ROWS_OF_WT
        flat = _mm_call(body, name, (n // tn,), [pl.BlockSpec((s, tn), lambda j: (0, j)), x_spec],
                        pl.BlockSpec((tn, d), lambda j: (j, 0)), jax.ShapeDtypeStruct((n, d), MXU_DTYPE), (dh, x))
        return flat.reshape(N_CHIPS, n // N_CHIPS, d)
    nc = dh.shape[2]
    return _mm_call(body, name, (N_CHIPS,), [x_spec, pl.BlockSpec((None, s, nc), lambda j: (j, 0, 0))],
                    pl.BlockSpec((None, d, nc), lambda j: (j, 0, 0)),
                    jax.ShapeDtypeStruct((N_CHIPS, d, nc), MXU_DTYPE), (x, dh))


def _mm_out_norm(mix, wg, x, g, b, name, target=None, after=None):
    s, k = mix.shape
    _, kc, d = wg.shape
    tm = min(512, s)

    def normed(a_ref, w_ref, x_ref, g_ref, b_ref, z_ref):
        z = DN_ALPHA * x_ref[...] + _dot(a_ref[...], w_ref[...].reshape(N_CHIPS * kc, d))
        mu = jnp.mean(z, axis=1, keepdims=True)
        zc = z - mu
        var = jnp.mean(zc * zc, axis=1, keepdims=True)
        z_ref[...] = z
        return zc * lax.rsqrt(var + LN_EPS) * g_ref[...] + b_ref[...]

    def body(a_ref, w_ref, x_ref, g_ref, b_ref, after_ref, z_ref, o_ref, ob_ref):
        o = normed(a_ref, w_ref, x_ref, g_ref, b_ref, z_ref)
        o_ref[...] = o
        ob_ref[...] = o.astype(ob_ref.dtype)

    def body_with_loss(a_ref, w_ref, x_ref, g_ref, b_ref, t_ref, z_ref, do_ref, l_ref):
        @pl.when(pl.program_id(0) == 0)
        def _():
            l_ref[...] = jnp.zeros_like(l_ref)

        e = normed(a_ref, w_ref, x_ref, g_ref, b_ref, z_ref) - t_ref[...]
        do_ref[...] = e * (1.0 / d)
        l_ref[...] += (0.5 / d) * jnp.sum(jnp.sum(e * e, axis=1, keepdims=True), axis=0, keepdims=True)

    row = pl.BlockSpec((tm, d), lambda i: (i, 0))
    vec = pl.BlockSpec((1, d), lambda i: (0, 0))
    in_specs = [pl.BlockSpec((tm, k), lambda i: (i, 0)), _resident((N_CHIPS, kc, d), lambda i: (0, 0, 0)), row, vec,
                vec]
    args = (mix, wg, x, g.reshape(1, d), b.reshape(1, d))
    f32_rows = jax.ShapeDtypeStruct((s, d), F32)
    if target is None:
        return pl.pallas_call(
            body, name=name, grid=(s // tm,), in_specs=in_specs + [ANY], out_specs=[row, row, row],
            out_shape=[f32_rows, f32_rows, jax.ShapeDtypeStruct((s, d), MXU_DTYPE)],
            compiler_params=_params(("parallel",)))(*args, after)
    return pl.pallas_call(
        body_with_loss, name=name, grid=(s // tm,), in_specs=in_specs + [row],
        out_specs=[row, row, pl.BlockSpec((1, 1), lambda i: (0, 0))],
        out_shape=[f32_rows, f32_rows, jax.ShapeDtypeStruct((1, 1), F32)],
        compiler_params=_params(("arbitrary",)))(*args, target)


def _mm_dmix(dz, wg, after, name):
    s, d = dz.shape
    kc = wg.shape[1]

    def body(a_ref, w_ref, after_ref, o_ref):
        o_ref[...] = _dot_nt(a_ref[...], w_ref[...])

    return _mm_call(body, name, (N_CHIPS,),
                    [_resident((s, d), lambda j: (0, 0)), pl.BlockSpec((None, kc, d), lambda j: (j, 0, 0)), ANY],
                    pl.BlockSpec((s, kc), lambda j: (0, j)), jax.ShapeDtypeStruct((s, N_CHIPS * kc), F32),
                    (dz, wg, after))


def _mm_dw_out(mix, dz, name):
    s, k = mix.shape
    d = dz.shape[1]
    kc = k // N_CHIPS

    def body(a_ref, b_ref, o_ref):
        o_ref[...] = _dot_tn(a_ref[...], b_ref[...]).astype(o_ref.dtype)

    return _mm_call(body, name, (N_CHIPS,),
                    [pl.BlockSpec((s, kc), lambda i: (0, i)), _resident((s, d), lambda i: (0, 0))],
                    pl.BlockSpec((None, kc, d), lambda i: (i, 0, 0)),
                    jax.ShapeDtypeStruct((N_CHIPS, kc, d), MXU_DTYPE), (mix, dz))


def _ln_bwd(dout, z, g, after):
    s, d = z.shape
    tr = min(256, s)

    def body(do_ref, z_ref, g_ref, after_ref, dz_ref, dzb_ref, dg_ref, db_ref):
        @pl.when(pl.program_id(0) == 0)
        def _():
            dg_ref[...] = jnp.zeros_like(dg_ref)
            db_ref[...] = jnp.zeros_like(db_ref)

        zz = z_ref[...]
        do = do_ref[...]
        mu = jnp.mean(zz, axis=1, keepdims=True)
        zc = zz - mu
        var = jnp.mean(zc * zc, axis=1, keepdims=True)
        rstd = lax.rsqrt(var + LN_EPS)
        zh = zc * rstd
        dg_ref[...] += jnp.sum(do * zh, axis=0, keepdims=True)
        db_ref[...] += jnp.sum(do, axis=0, keepdims=True)
        dzh = do * g_ref[...]
        dz = rstd * (dzh - jnp.mean(dzh, axis=1, keepdims=True) - zh * jnp.mean(dzh * zh, axis=1, keepdims=True))
        dz_ref[...] = dz
        dzb_ref[...] = dz.astype(MXU_DTYPE)

    row = pl.BlockSpec((tr, d), lambda i: (i, 0))
    vec = pl.BlockSpec((1, d), lambda i: (0, 0))
    return pl.pallas_call(
        body, name="ln_bwd", grid=(s // tr,), in_specs=[row, row, vec, ANY], out_specs=[row, row, vec, vec],
        out_shape=[jax.ShapeDtypeStruct((s, d), F32), jax.ShapeDtypeStruct((s, d), MXU_DTYPE),
                   jax.ShapeDtypeStruct((1, d), F32), jax.ShapeDtypeStruct((1, d), F32)],
        compiler_params=_params(("arbitrary",)),
    )(dout, z, g.reshape(1, d), after)


Q0, K0, V0, BG0 = 3072, 4096, 4224, 4352
T = CHUNK


def _lane_ids(width):
    return lax.broadcasted_iota(jnp.int32, (T, width), 1)


def _rope_swap(x):
    w = x.shape[1]
    l64 = _lane_ids(w) % B_HEAD_DIM
    half = ROT_DIM // 2
    return jnp.where(l64 < half, pltpu.roll(x, w - half, 1), jnp.where(l64 < ROT_DIM, pltpu.roll(x, half, 1), 0.0))


def _rope(x, c, s):
    return x * c + _rope_swap(x) * s


def _rope_bwd(d, c, s):
    return d * c + _rope_swap(d * s)


def _tril():
    return lax.broadcasted_iota(jnp.int32, (T, T), 0) >= lax.broadcasted_iota(jnp.int32, (T, T), 1)


def _band_mask(has_prev):
    qi = lax.broadcasted_iota(jnp.int32, (T, 2 * T), 0)
    kj = lax.broadcasted_iota(jnp.int32, (T, 2 * T), 1)
    return (kj > qi) & (kj <= qi + WINDOW) & ((kj >= WINDOW) | has_prev)


def _a_norm(v, g, b):
    mu = jnp.mean(v, axis=1, keepdims=True)
    vc = v - mu
    var = jnp.mean(vc * vc, axis=1, keepdims=True)
    rstd = lax.rsqrt(var + LN_EPS)
    vh = vc * rstd
    return vh, rstd, vh * g + b


def _kv_operands(hk, k_rot, kp_rot, vv, vp):
    lo = _lane_ids(LANES) < B_HEAD_DIM
    mine = lo if hk == 0 else jnp.logical_not(lo)
    both = lambda t: jnp.where(mine, t, 0.0) + pltpu.roll(jnp.where(mine, t, 0.0), B_HEAD_DIM, 1)
    k2 = jnp.concatenate([both(kp_rot), both(k_rot)], axis=0)
    v2 = jnp.concatenate([both(vp), both(vv)], axis=0)
    return k2.astype(MXU_DTYPE), v2.astype(MXU_DTYPE)


GROUP_HEADS = B_Q_HEADS // B_KV_HEADS


def _pair_columns(hk):
    return [(hk * GROUP_HEADS // 2 + pp) * LANES for pp in range(GROUP_HEADS // 2)]


def _stacked_heads(pairs):
    lo = _lane_ids(LANES) < B_HEAD_DIM
    rows = []
    for tile in pairs:
        rows += [jnp.where(lo, tile, 0.0), jnp.where(lo, 0.0, tile)]
    return jnp.concatenate(rows, axis=0).astype(MXU_DTYPE)


def _pair_of(stacked, pp):
    lo = _lane_ids(LANES) < B_HEAD_DIM
    return jnp.where(lo, stacked[2 * pp * T:(2 * pp + 1) * T], stacked[(2 * pp + 1) * T:(2 * pp + 2) * T])


def _softmax_with_sink(qm, k2, valid, sink_ref, head0):
    scores = (_dot_nt(qm, k2) * ATTN_SCALE).reshape(GROUP_HEADS, T, 2 * T)
    s = jnp.where(valid[None], scores, NEG_BIG).reshape(GROUP_HEADS * T, 2 * T)
    sink = jnp.concatenate([jnp.broadcast_to(sink_ref[:, head0 + h:head0 + h + 1], (T, 1))
                            for h in range(GROUP_HEADS)], axis=0)
    m = jnp.maximum(jnp.max(s, axis=1, keepdims=True), sink)
    ex = jnp.exp(s - m)
    es = jnp.exp(sink - m)
    inv = 1.0 / (jnp.sum(ex, axis=1, keepdims=True) + es)
    return ex * inv, es * inv


def _even_specs(s):
    nb = s // T
    prev = lambda r: lambda i: jnp.maximum(r(i) - 1, 0)
    return nb, {
        'h': lambda r: pl.BlockSpec((T, EVEN_IN), lambda i: (r(i), 0)),
        'kprev': lambda r: pl.BlockSpec((T, LANES), lambda i: (prev(r)(i), K0 // LANES)),
        'vprev': lambda r: pl.BlockSpec((T, LANES), lambda i: (prev(r)(i), V0 // LANES)),
        'tab': lambda r: pl.BlockSpec((T, LANES), lambda i: (r(i), 0)),
        'tabprev': lambda r: pl.BlockSpec((T, LANES), lambda i: (prev(r)(i), 0)),
    }


def _full(shape):
    nd = len(shape)
    return pl.BlockSpec(shape, lambda i: (0,) * nd)


def _even_fwd(h, cq, sq, ln_g, ln_b, ws, bs_t, sinks, after):
    s = h.shape[0]
    nb, sp = _even_specs(s)

    def body(h_ref, kp_ref, vp_ref, cq_ref, sq_ref, ckp_ref, skp_ref, g_ref, b_ref, ws_ref, bst_ref, sink_ref,
             after_ref, mix_ref):
        i = pl.program_id(0)
        tril = _tril()
        _, _, vn = _a_norm(h_ref[:, 1024:2048], g_ref[...], b_ref[...])
        vnb = vn.astype(MXU_DTYPE)
        for g in range(A_GROUPS):
            c0, c1 = g * LANES, (g + 1) * LANES
            wg = jnp.where(tril, ws_ref[g], 0.0).astype(MXU_DTYPE)
            mixed = _dot(wg, vnb[:, c0:c1]) + bst_ref[:, g:g + 1]
            sil, _ = _silu_and_grad(h_ref[:, 2048 + c0:2048 + c1])
            mix_ref[:, c0:c1] = (h_ref[:, c0:c1] * mixed * sil).astype(mix_ref.dtype)

        cq, sq = cq_ref[...], sq_ref[...]
        k_rot = _rope(h_ref[:, K0:K0 + LANES], cq, sq)
        kp_rot = _rope(kp_ref[...], ckp_ref[...], skp_ref[...])
        valid = _band_mask(i > 0)
        for hk in range(B_KV_HEADS):
            k2, v2 = _kv_operands(hk, k_rot, kp_rot, h_ref[:, V0:V0 + LANES], vp_ref[...])
            qm = _stacked_heads([_rope(h_ref[:, Q0 + c0:Q0 + c0 + LANES], cq, sq) for c0 in _pair_columns(hk)])
            p, _ = _softmax_with_sink(qm, k2, valid, sink_ref, GROUP_HEADS * hk)
            out = _dot(p.astype(MXU_DTYPE), v2)
            for pp, c0 in enumerate(_pair_columns(hk)):
                sil, _ = _silu_and_grad(h_ref[:, BG0 + c0:BG0 + c0 + LANES])
                mix_ref[:, A_WIDTH + c0:A_WIDTH + c0 + LANES] = (_pair_of(out, pp) * sil).astype(mix_ref.dtype)

    ident = lambda i: i
    return pl.pallas_call(
        body, name="even_fwd", grid=(nb,),
        in_specs=[sp['h'](ident), sp['kprev'](ident), sp['vprev'](ident), sp['tab'](ident), sp['tab'](ident),
                  sp['tabprev'](ident), sp['tabprev'](ident), _full((1, A_WIDTH)), _full((1, A_WIDTH)),
                  _full((A_GROUPS, T, T)), _full((T, A_GROUPS)), _full((1, B_Q_HEADS)), ANY],
        out_specs=pl.BlockSpec((T, 2 * A_WIDTH), lambda i: (i, 0)),
        out_shape=jax.ShapeDtypeStruct((s, 2 * A_WIDTH), MXU_DTYPE),
        compiler_params=_params(("parallel",)),
    )(h, h, h, cq, sq, cq, sq, ln_g.reshape(1, -1), ln_b.reshape(1, -1), ws, bs_t, sinks.reshape(1, -1), after)


def _even_bwd(h, dmix, cq, sq, ln_g, ln_b, ws, bs_t, sinks):
    s = h.shape[0]
    nb, sp = _even_specs(s)

    def body(h_ref, kp_ref, vp_ref, cq_ref, sq_ref, ckp_ref, skp_ref, dm_ref, g_ref, b_ref, ws_ref, bst_ref, sink_ref,
             dh_ref, dg_ref, db_ref, dws_ref, dbst_ref, dsink_ref, dvn_scr, ck_scr, cv_scr):
        i = pl.program_id(0)

        @pl.when(i == 0)
        def _():
            for ref in (dg_ref, db_ref, dws_ref, dbst_ref, dsink_ref, ck_scr, cv_scr):
                ref[...] = jnp.zeros_like(ref)

        tril = _tril()
        vh, rstd, vn = _a_norm(h_ref[:, 1024:2048], g_ref[...], b_ref[...])
        vnb = vn.astype(MXU_DTYPE)
        for g in range(A_GROUPS):
            c0, c1 = g * LANES, (g + 1) * LANES
            wg = jnp.where(tril, ws_ref[g], 0.0).astype(MXU_DTYPE)
            mixed = _dot(wg, vnb[:, c0:c1]) + bst_ref[:, g:g + 1]
            ag = h_ref[:, 2048 + c0:2048 + c1]
            u = h_ref[:, c0:c1]
            sil, dsil = _silu_and_grad(ag)
            da = dm_ref[:, c0:c1]
            dh_ref[:, c0:c1] = (da * mixed * sil).astype(dh_ref.dtype)
            dh_ref[:, 2048 + c0:2048 + c1] = (da * u * mixed * dsil).astype(dh_ref.dtype)
            dmx = da * u * sil
            dmb = dmx.astype(MXU_DTYPE)
            dvn_scr[:, c0:c1] = _dot_tn(wg, dmb)
            dws_ref[g] += jnp.where(tril, _dot_nt(dmb, vnb[:, c0:c1]), 0.0)
            dbst_ref[:, g:g + 1] += jnp.sum(dmx, axis=1, keepdims=True)
        dvn = dvn_scr[...]
        dg_ref[...] += jnp.sum(dvn * vh, axis=0, keepdims=True)
        db_ref[...] += jnp.sum(dvn, axis=0, keepdims=True)
        dvh = dvn * g_ref[...]
        dv = rstd * (dvh - jnp.mean(dvh, axis=1, keepdims=True) - vh * jnp.mean(dvh * vh, axis=1, keepdims=True))
        dh_ref[:, 1024:2048] = dv.astype(dh_ref.dtype)

        cq, sq = cq_ref[...], sq_ref[...]
        k_rot = _rope(h_ref[:, K0:K0 + LANES], cq, sq)
        kp_rot = _rope(kp_ref[...], ckp_ref[...], skp_ref[...])
        valid = _band_mask(i < nb - 1)
        dk_all = jnp.zeros((2 * T, LANES), F32)
        dv_all = jnp.zeros((2 * T, LANES), F32)
        for hk in range(B_KV_HEADS):
            k2, v2 = _kv_operands(hk, k_rot, kp_rot, h_ref[:, V0:V0 + LANES], vp_ref[...])
            cols = _pair_columns(hk)
            qm = _stacked_heads([_rope(h_ref[:, Q0 + c0:Q0 + c0 + LANES], cq, sq) for c0 in cols])
            p, psink = _softmax_with_sink(qm, k2, valid, sink_ref, GROUP_HEADS * hk)
            pb = p.astype(MXU_DTYPE)
            out = _dot(pb, v2)
            dobs = []
            for pp, c0 in enumerate(cols):
                sil, dsil = _silu_and_grad(h_ref[:, BG0 + c0:BG0 + c0 + LANES])
                dbo = dm_ref[:, A_WIDTH + c0:A_WIDTH + c0 + LANES]
                dh_ref[:, BG0 + c0:BG0 + c0 + LANES] = (dbo * _pair_of(out, pp) * dsil).astype(dh_ref.dtype)
                dobs.append(dbo * sil)
            dobm = _stacked_heads(dobs)
            dp = _dot_nt(dobm, v2)
            rs = jnp.sum(p * dp, axis=1, keepdims=True)
            dsb = (p * (dp - rs) * ATTN_SCALE).astype(MXU_DTYPE)
            to_sink = psink * rs
            for h in range(GROUP_HEADS):
                head = GROUP_HEADS * hk + h
                dsink_ref[:, head:head + 1] += -jnp.sum(to_sink[h * T:(h + 1) * T], axis=0, keepdims=True)
            dq = _dot(dsb, k2)
            for pp, c0 in enumerate(cols):
                dh_ref[:, Q0 + c0:Q0 + c0 + LANES] = _rope_bwd(_pair_of(dq, pp), cq, sq).astype(dh_ref.dtype)
            acc_k = _dot_tn(dsb, qm)
            acc_v = _dot_tn(pb, dobm)
            lo2 = lax.broadcasted_iota(jnp.int32, (2 * T, LANES), 1) < B_HEAD_DIM
            mine = lo2 if hk == 0 else jnp.logical_not(lo2)
            dk_all = dk_all + jnp.where(mine, acc_k + pltpu.roll(acc_k, B_HEAD_DIM, 1), 0.0)
            dv_all = dv_all + jnp.where(mine, acc_v + pltpu.roll(acc_v, B_HEAD_DIM, 1), 0.0)
        dk_rot = dk_all[T:2 * T] + ck_scr[...]
        dvv = dv_all[T:2 * T] + cv_scr[...]
        ck_scr[...] = dk_all[0:T]
        cv_scr[...] = dv_all[0:T]
        dh_ref[:, K0:K0 + LANES] = _rope_bwd(dk_rot, cq, sq).astype(dh_ref.dtype)
        dh_ref[:, V0:V0 + LANES] = dvv.astype(dh_ref.dtype)

    rev = lambda i: nb - 1 - i
    return pl.pallas_call(
        body, name="even_bwd", grid=(nb,),
        in_specs=[sp['h'](rev), sp['kprev'](rev), sp['vprev'](rev), sp['tab'](rev), sp['tab'](rev),
                  sp['tabprev'](rev), sp['tabprev'](rev), pl.BlockSpec((T, 2 * A_WIDTH), lambda i: (rev(i), 0)),
                  _full((1, A_WIDTH)), _full((1, A_WIDTH)), _full((A_GROUPS, T, T)), _full((T, A_GROUPS)),
                  _full((1, B_Q_HEADS))],
        out_specs=[sp['h'](rev), _full((1, A_WIDTH)), _full((1, A_WIDTH)),
                   _full((A_GROUPS, T, T)), _full((T, A_GROUPS)), _full((1, B_Q_HEADS))],
        out_shape=[jax.ShapeDtypeStruct((s, EVEN_IN), MXU_DTYPE), jax.ShapeDtypeStruct((1, A_WIDTH), F32),
                   jax.ShapeDtypeStruct((1, A_WIDTH), F32), jax.ShapeDtypeStruct((A_GROUPS, T, T), F32),
                   jax.ShapeDtypeStruct((T, A_GROUPS), F32), jax.ShapeDtypeStruct((1, B_Q_HEADS), F32)],
        scratch_shapes=[pltpu.VMEM((T, A_WIDTH), F32), pltpu.VMEM((T, LANES), F32), pltpu.VMEM((T, LANES), F32)],
        compiler_params=_params(("arbitrary",)),
    )(h, h, h, cq, sq, cq, sq, dmix, ln_g.reshape(1, -1), ln_b.reshape(1, -1), ws, bs_t, sinks.reshape(1, -1))


def _shift_down(x, d, fill=0.0):
    rows = lax.broadcasted_iota(jnp.int32, x.shape, 0)
    return jnp.where(rows >= d, pltpu.roll(x, d, 0), fill)


def _shift_up(x, d, fill=0.0):
    n = x.shape[0]
    rows = lax.broadcasted_iota(jnp.int32, x.shape, 0)
    return jnp.where(rows < n - d, pltpu.roll(x, n - d, 0), fill)


SCAN_BLOCK = 128
SUBLANES = 8


def _scan_block(a, b, up):
    n = a.shape[0]
    d = 1
    while d < n:
        if d < SUBLANES:
            shift = _shift_up if up else _shift_down
            a, b = a * shift(a, d, 1.0), a * shift(b, d) + b
        elif up:
            a, b = (jnp.concatenate([a[:n - d] * a[d:], a[n - d:]], axis=0),
                    jnp.concatenate([a[:n - d] * b[d:] + b[:n - d], b[n - d:]], axis=0))
        else:
            a, b = (jnp.concatenate([a[:d], a[d:] * a[:n - d]], axis=0),
                    jnp.concatenate([b[:d], a[d:] * b[:n - d] + b[d:]], axis=0))
        d *= 2
    return a, b


def _scan(a, b, up):
    n = a.shape[0]
    size = min(SCAN_BLOCK, n)
    blocks = n // size
    out, carry = [None] * blocks, None
    for k in (reversed(range(blocks)) if up else range(blocks)):
        prod, h = _scan_block(a[k * size:(k + 1) * size], b[k * size:(k + 1) * size], up)
        if carry is not None:
            h = h + prod * carry
        carry = h[0:1] if up else h[size - 1:size]
        out[k] = h
    return jnp.concatenate(out, axis=0)


def _neg_expm1(y):
    t = jnp.tanh(-0.5 * y)
    return 2.0 * t / (1.0 + t)


def _softplus(x):
    return jnp.maximum(x, 0.0) + jnp.log1p(jnp.exp(-jnp.abs(x)))


def _lru_forward(xc, cw, cb, wa, ba, wx, bx, lam):
    shifted = [_shift_down(xc, CONV_WIDTH - 1 - j) if j < CONV_WIDTH - 1 else xc for j in range(CONV_WIDTH)]
    xconv = cb
    for j in range(CONV_WIDTH):
        xconv = xconv + cw[j:j + 1, :] * shifted[j]
    xb = xconv.astype(MXU_DTYPE)
    r = _sigmoid(_dot(xb, wa.astype(MXU_DTYPE)) + ba)
    ig = _sigmoid(_dot(xb, wx.astype(MXU_DTYPE)) + bx)
    sp = _softplus(-lam)
    log_a = -LRU_C * r * sp
    a = jnp.exp(log_a)
    mult = jnp.sqrt(_neg_expm1(2.0 * log_a))
    hs = _scan(a, mult * ig * xconv, up=False)
    return dict(shifted=shifted, xconv=xconv, xb=xb, r=r, ig=ig, sp=sp, a=a, mult=mult, hs=hs)


def _col(width, off):
    return lambda s: pl.BlockSpec((s, width), lambda i: (0, off + i))


def _part(width, slot):
    return lambda s: pl.BlockSpec((None, s, width), lambda i: (slot, 0, i))


def _lru_specs(s):
    w = LANES
    return [_part(w, 0)(s), _part(w, 1)(s), pl.BlockSpec((CONV_WIDTH, w), lambda i: (0, i)),
            pl.BlockSpec((1, w), lambda i: (0, i)), pl.BlockSpec((None, w, w), lambda i: (i, 0, 0)),
            pl.BlockSpec((1, w), lambda i: (0, i)), pl.BlockSpec((None, w, w), lambda i: (i, 0, 0)),
            pl.BlockSpec((1, w), lambda i: (0, i)), pl.BlockSpec((1, w), lambda i: (0, i))]


def _lru_fwd(h, cw, cb, wa, ba, wx, bx, lam):
    s = h.shape[1]

    def body(xc_ref, cg_ref, cw_ref, cb_ref, wa_ref, ba_ref, wx_ref, bx_ref, lam_ref, o_ref):
        f = _lru_forward(xc_ref[...], cw_ref[...], cb_ref[...], wa_ref[...], ba_ref[...], wx_ref[...], bx_ref[...],
                         lam_ref[...])
        sil, _ = _silu_and_grad(cg_ref[...])
        o_ref[...] = (f['hs'] * sil).astype(o_ref.dtype)

    return pl.pallas_call(
        body, name="lru_fwd", grid=(C_HEADS,), in_specs=_lru_specs(s), out_specs=_col(LANES, 0)(s),
        out_shape=jax.ShapeDtypeStruct((s, C_WIDTH + D_WIDTH), MXU_DTYPE), compiler_params=_params(("parallel",)),
    )(h, h, cw, cb.reshape(1, -1), wa, ba.reshape(1, -1), wx, bx.reshape(1, -1), lam.reshape(1, -1))


def _lru_bwd(h, dmix, cw, cb, wa, ba, wx, bx, lam):
    s = h.shape[1]

    def body(xc_ref, cg_ref, cw_ref, cb_ref, wa_ref, ba_ref, wx_ref, bx_ref, lam_ref, dco_ref,
             dh_ref, dcw_ref, dcb_ref, dwa_ref, dba_ref, dwx_ref, dbx_ref, dlam_ref):
        dxc_ref, dcg_ref = dh_ref.at[0], dh_ref.at[1]
        cw, lam = cw_ref[...], lam_ref[...]
        wab, wxb = wa_ref[...].astype(MXU_DTYPE), wx_ref[...].astype(MXU_DTYPE)
        f = _lru_forward(xc_ref[...], cw, cb_ref[...], wa_ref[...], ba_ref[...], wx_ref[...], bx_ref[...], lam)
        sil, dsil = _silu_and_grad(cg_ref[...])
        dco = dco_ref[...]
        hs, a, mult, ig, r, xconv = f['hs'], f['a'], f['mult'], f['ig'], f['r'], f['xconv']
        dcg_ref[...] = (dco * hs * dsil).astype(dcg_ref.dtype)
        lamb = _scan(_shift_up(a, 1), dco * sil, up=True)
        da = lamb * _shift_down(hs, 1)
        d_ig = lamb * mult * xconv
        d_mult = lamb * ig * xconv
        dxconv = lamb * mult * ig
        d_log_a = da * a - d_mult * (a * a) / mult
        dlam_ref[...] = jnp.sum(d_log_a * r, axis=0, keepdims=True) * LRU_C * _sigmoid(-lam)
        dpa = d_log_a * (-LRU_C * f['sp']) * r * (1.0 - r)
        dpx = d_ig * ig * (1.0 - ig)
        dba_ref[...] = jnp.sum(dpa, axis=0, keepdims=True)
        dbx_ref[...] = jnp.sum(dpx, axis=0, keepdims=True)
        dpab, dpxb = dpa.astype(MXU_DTYPE), dpx.astype(MXU_DTYPE)
        dwa_ref[...] = _dot_tn(f['xb'], dpab)
        dwx_ref[...] = _dot_tn(f['xb'], dpxb)
        dxconv = dxconv + _dot_nt(dpab, wab) + _dot_nt(dpxb, wxb)
        dcb_ref[...] = jnp.sum(dxconv, axis=0, keepdims=True)
        dxc = jnp.zeros_like(dxconv)
        for j in range(CONV_WIDTH):
            dcw_ref[j:j + 1, :] = jnp.sum(dxconv * f['shifted'][j], axis=0, keepdims=True)
            back = _shift_up(dxconv, CONV_WIDTH - 1 - j) if j < CONV_WIDTH - 1 else dxconv
            dxc = dxc + cw[j:j + 1, :] * back
        dxc_ref[...] = dxc.astype(dxc_ref.dtype)

    w = LANES
    vec = pl.BlockSpec((1, w), lambda i: (0, i))
    mat = pl.BlockSpec((None, w, w), lambda i: (i, 0, 0))
    vshape = jax.ShapeDtypeStruct((1, C_WIDTH), F32)
    mshape = jax.ShapeDtypeStruct((C_HEADS, w, w), F32)
    return pl.pallas_call(
        body, name="lru_bwd", grid=(C_HEADS,), in_specs=_lru_specs(s) + [_col(w, 0)(s)],
        out_specs=[pl.BlockSpec((2, s, w), lambda i: (0, 0, i)), pl.BlockSpec((CONV_WIDTH, w), lambda i: (0, i)),
                   vec, mat, vec, mat, vec, vec],
        out_shape=[jax.ShapeDtypeStruct((N_CHIPS, s, C_WIDTH), MXU_DTYPE),
                   jax.ShapeDtypeStruct((CONV_WIDTH, C_WIDTH), F32), vshape, mshape, vshape, mshape, vshape, vshape],
        compiler_params=_params(("parallel",)),
    )(h, h, cw, cb.reshape(1, -1), wa, ba.reshape(1, -1), wx, bx.reshape(1, -1), lam.reshape(1, -1), dmix)


def _pool_window(g):
    return jnp.left_shift(2, g).astype(F32)


def _select_window(g, sums):
    out = sums[-1]
    for k in range(len(sums) - 2, -1, -1):
        out = jnp.where(g == k, sums[k], out)
    return out


def _window_sums(x, shift):
    sums, cur = [], x
    for k in range(D_GROUPS):
        cur = cur + shift(cur, 2 ** k)
        sums.append(cur)
    return sums


def _pool_specs(s):
    w = D_GROUP_DIM
    return [_part(w, 2)(s), _part(w, 3)(s),
            pl.BlockSpec((None, w, w), lambda i: (i, 0, 0)), pl.BlockSpec((1, w), lambda i: (0, i))]


def _pool_forward(xd, g):
    rows = lax.broadcasted_iota(jnp.int32, xd.shape, 0).astype(F32) + 1.0
    cnt = jnp.minimum(rows, _pool_window(g))
    pooled = _select_window(g, _window_sums(xd, _shift_down)) / cnt - xd
    return pooled, cnt


def _pool_fwd(h, mix, w_pool, d_scale):
    s = h.shape[1]

    def body(xd_ref, dg_ref, wp_ref, sc_ref, mix_ref, o_ref):
        pooled, _ = _pool_forward(xd_ref[...], pl.program_id(0))
        mixed = _dot(pooled.astype(MXU_DTYPE), wp_ref[...].astype(MXU_DTYPE))
        sil, _ = _silu_and_grad(dg_ref[...])
        o_ref[...] = (mixed * sc_ref[...] * sil).astype(o_ref.dtype)

    return pl.pallas_call(
        body, name="pool_fwd", grid=(D_GROUPS,), in_specs=_pool_specs(s) + [ANY],
        out_specs=_col(D_GROUP_DIM, C_WIDTH // D_GROUP_DIM)(s), out_shape=jax.ShapeDtypeStruct(mix.shape, mix.dtype),
        input_output_aliases={4: 0}, compiler_params=_params(("parallel",)),
    )(h, h, w_pool, d_scale.reshape(1, -1), mix)


def _pool_bwd(h, dmix, dh, w_pool, d_scale):
    s = h.shape[1]
    w = D_GROUP_DIM

    def body(xd_ref, dg_ref, wp_ref, sc_ref, ddo_ref, dh_in_ref, dh_ref, dwp_ref, dsc_ref):
        dxd_ref, ddg_ref = dh_ref.at[0], dh_ref.at[1]
        g = pl.program_id(0)
        pooled, cnt = _pool_forward(xd_ref[...], g)
        pb = pooled.astype(MXU_DTYPE)
        wpb = wp_ref[...].astype(MXU_DTYPE)
        mixed = _dot(pb, wpb)
        sil, dsil = _silu_and_grad(dg_ref[...])
        ddo = ddo_ref[...]
        sc = sc_ref[...]
        ddg_ref[...] = (ddo * mixed * sc * dsil).astype(ddg_ref.dtype)
        dms = ddo * sil
        dsc_ref[...] = jnp.sum(dms * mixed, axis=0, keepdims=True)
        dmb = (dms * sc).astype(MXU_DTYPE)
        dwp_ref[...] = _dot_tn(pb, dmb)
        dpooled = _dot_nt(dmb, wpb)
        dxd = _select_window(g, _window_sums(dpooled / cnt, _shift_up)) - dpooled
        dxd_ref[...] = dxd.astype(dxd_ref.dtype)

    return pl.pallas_call(
        body, name="pool_bwd", grid=(D_GROUPS,), in_specs=_pool_specs(s) + [_col(w, C_WIDTH // w)(s), ANY],
        out_specs=[pl.BlockSpec((2, s, w), lambda i: (1, 0, i)), pl.BlockSpec((None, w, w), lambda i: (i, 0, 0)),
                   pl.BlockSpec((1, w), lambda i: (0, i))],
        out_shape=[jax.ShapeDtypeStruct(dh.shape, dh.dtype),
                   jax.ShapeDtypeStruct((D_GROUPS, w, w), F32), jax.ShapeDtypeStruct((1, D_WIDTH), F32)],
        input_output_aliases={5: 0}, compiler_params=_params(("parallel",)),
    )(h, h, w_pool, d_scale.reshape(1, -1), dmix, dh)


def _rope_tables(positions):
    s = positions.shape[0]
    inv_freq = ROPE_THETA ** (-jnp.arange(0, ROT_DIM, 2, dtype=F32) / ROT_DIM)
    ang = positions.astype(F32)[:, None] * inv_freq
    c, sn = jnp.cos(ang), jnp.sin(ang)
    rest = B_HEAD_DIM - ROT_DIM
    c64 = jnp.concatenate([c, c, jnp.ones((s, rest), F32)], axis=1)
    s64 = jnp.concatenate([-sn, sn, jnp.zeros((s, rest), F32)], axis=1)
    return jnp.tile(c64, (1, LANES // B_HEAD_DIM)), jnp.tile(s64, (1, LANES // B_HEAD_DIM))


def _local_step(x, positions, target, w, projections_of, projections_ahead, projection_grads, norm_grads_done):
    cq, sq = _rope_tables(positions)
    saved = []
    xin, xin_b = x, x.astype(MXU_DTYPE)
    for layer in range(DEPTH):
        j = layer // 2
        kind = 'even' if layer % 2 == 0 else 'odd'
        w_in, w_out = projections_of(layer, xin_b)
        h = _mm_in(xin_b, w_in, W_IN_TRANSPOSED[kind], "mm_" + kind + "_in")
        if kind == 'even':
            behind, w_out_after = w_out(h) if callable(w_out) else (h, None)
            mix = _even_fwd(h, cq, sq, w['even_a_ln_g'][j], w['even_a_ln_b'][j], w['even_a_ws'][j],
                            w['even_a_bs'][j].T, w['even_b_sinks'][j], behind)
            w_out = w_out_after(mix) if w_out_after else w_out
        else:
            mix = _lru_fwd(h, w['odd_conv_w'][j], w['odd_conv_b'][j], w['odd_w_a'][j], w['odd_b_a'][j],
                           w['odd_w_x'][j], w['odd_b_x'][j], w['odd_lam'][j])
            mix = _pool_fwd(h, mix, w['odd_w_pool'][j], w['odd_d_scale'][j])
        if layer == DEPTH - 1:
            z, out, out_b = _mm_out_norm(mix, w_out, xin, w[kind + '_ln_g'][j], w[kind + '_ln_b'][j],
                                         "mm_out_norm_loss", target=target)
        else:
            z, out, out_b = _mm_out_norm(mix, w_out, xin, w[kind + '_ln_g'][j], w[kind + '_ln_b'][j], "mm_out_norm",
                                         after=projections_ahead(layer + 1, mix))
        saved.append((xin_b, h, mix, z, w_in, w_out))
        xin, xin_b = out, out_b

    dout, loss = xin, xin_b
    per_layer = {n: [None, None] for n in SMALL_NAMES}
    behind = loss
    for layer in reversed(range(DEPTH)):
        j = layer // 2
        kind = 'even' if layer % 2 == 0 else 'odd'
        xin_b, h, mix, z, w_in, w_out = saved[layer]
        dz, dz_b, dg, db = _ln_bwd(dout, z, w[kind + '_ln_g'][j], behind)
        per_layer[kind + '_ln_g'][j], per_layer[kind + '_ln_b'][j] = dg[0], db[0]
        dmix = _mm_dmix(dz_b, w_out, norm_grads_done(layer, dz_b), "mm_dmix")
        dw_out = _mm_dw_out(mix, dz_b, "mm_dw_out")
        if kind == 'even':
            dh, dlg, dlb, dws, dbst, dsink = _even_bwd(
                h, dmix, cq, sq, w['even_a_ln_g'][j], w['even_a_ln_b'][j], w['even_a_ws'][j], w['even_a_bs'][j].T,
                w['even_b_sinks'][j])
            for n, val in (('even_a_ln_g', dlg[0]), ('even_a_ln_b', dlb[0]), ('even_a_ws', dws),
                           ('even_a_bs', dbst.T), ('even_b_sinks', dsink[0])):
                per_layer[n][j] = val
        else:
            dh, dcw, dcb, dwa, dba, dwx, dbx, dlam = _lru_bwd(
                h, dmix, w['odd_conv_w'][j], w['odd_conv_b'][j], w['odd_w_a'][j], w['odd_b_a'][j], w['odd_w_x'][j],
                w['odd_b_x'][j], w['odd_lam'][j])
            dh, dwp, dsc = _pool_bwd(h, dmix, dh, w['odd_w_pool'][j], w['odd_d_scale'][j])
            for n, val in (('odd_conv_w', dcw), ('odd_conv_b', dcb[0]), ('odd_w_a', dwa), ('odd_b_a', dba[0]),
                           ('odd_w_x', dwx), ('odd_b_x', dbx[0]), ('odd_lam', dlam[0]), ('odd_w_pool', dwp),
                           ('odd_d_scale', dsc[0])):
                per_layer[n][j] = val
        dout = _mm_dx(dh, w_in, dz, W_IN_TRANSPOSED[kind], "mm_dx_" + kind)
        dw_in = _mm_dw_in(xin_b, dh, W_IN_TRANSPOSED[kind], "mm_dw_in_" + kind)
        behind = projection_grads(layer, dw_in, dw_out)
    return loss, dout, {n: jnp.stack(v) for n, v in per_layer.items()}


ANY = pl.BlockSpec(memory_space=pl.ANY)


def _where_am_i():
    return lax.axis_index("x"), lax.axis_index("y"), lax.axis_index("c")


def _other_chips(x, y):
    return [(1 - x, y), (x, 1 - y), (1 - x, 1 - y)]


PACKED_ROWS = 16


def _half_rows(rows, which):
    half = rows // 2
    assert half % PACKED_ROWS == 0, rows
    return pl.ds(pl.multiple_of(which * half, PACKED_ROWS), half)


def _sibling_swap(bufs, name, *, other_half_of_axis1=False, after=()):
    n, na = len(bufs), len(after)
    out_shapes = [(b.shape[0], b.shape[1] // 2, b.shape[2]) if other_half_of_axis1 else b.shape for b in bufs]

    def body(*refs):
        x_refs, out_refs, (send_sems, recv_sems) = refs[:n], refs[n + na:2 * n + na], refs[2 * n + na:]
        x, y, c = _where_am_i()
        copies = []
        for p in range(n):
            src = x_refs[p].at[:, _half_rows(bufs[p].shape[1], 1 - c), :] if other_half_of_axis1 else x_refs[p]
            copies.append(pltpu.make_async_remote_copy(
                src_ref=src, dst_ref=out_refs[p], send_sem=send_sems.at[p], recv_sem=recv_sems.at[p],
                device_id=(x, y, 1 - c), device_id_type=MESH))
            copies[-1].start()
        for cp in copies:
            cp.wait()

    return pl.pallas_call(
        body, name=name, in_specs=[ANY] * (n + na), out_specs=[ANY] * n,
        out_shape=[jax.ShapeDtypeStruct(s, b.dtype) for s, b in zip(out_shapes, bufs)],
        scratch_shapes=[pltpu.SemaphoreType.DMA((n,)), pltpu.SemaphoreType.DMA((n,))],
    )(*bufs, *after)


HBM = pl.BlockSpec(memory_space=pltpu.HBM)
SEM = pl.BlockSpec(memory_space=pltpu.SEMAPHORE)
IN_FLIGHT = pltpu.CompilerParams(has_side_effects=pltpu.SideEffectType.DATAFLOW_SIDE_EFFECTING)


IN_PLACE = ('forward',)


def _peers(kind, x, y, c):
    if kind in ('swap', 'join'):
        return [(x, y, 1 - c)]
    return [(x, y, 1 - c) if kind == 'forward' else (px, py, c) for px, py in _other_chips(x, y)]


def _n_peers(kind):
    return 1 if kind in ('swap', 'join') else N_CHIPS - 1


def _land_shape(kind, a):
    if kind == 'gather':
        return (N_CHIPS,) + a.shape
    return (a.shape[0], a.shape[1] // 2, a.shape[2]) if kind == 'swap' else a.shape


def _ici_copy(kind, src_ref, land_ref, send_sems, recv_sems, k, peer, x, y, c, arriving=False):
    me = 2 * x + y
    if kind == 'gather':
        half = _half_rows(src_ref.shape[0], c)
        src, dst = src_ref.at[half, :], land_ref.at[me, half, :]
    elif kind == 'scatter':
        src, dst = src_ref.at[2 * peer[0] + peer[1]], land_ref.at[me]
    elif kind == 'swap':
        src, dst = src_ref.at[:, _half_rows(src_ref.shape[1], 1 - c), :], land_ref
    elif kind == 'join':
        src, dst = src_ref, land_ref
    else:
        px, py = _other_chips(x, y)[k % (N_CHIPS - 1)]
        rows = src_ref.at[2 * px + py, _half_rows(src_ref.shape[1], 1 - c if arriving else c), :]
        src, dst = rows, rows
    return pltpu.make_async_remote_copy(src_ref=src, dst_ref=dst, send_sem=send_sems.at[k], recv_sem=recv_sems.at[k],
                                        device_id=peer, device_id_type=MESH)


def _ici_start(kind, groups, name, after=()):
    flat = [a for g in groups for a in g]
    n, ng, np_, na = len(flat), len(groups), _n_peers(kind), len(after)
    nl = 0 if kind in IN_PLACE else n
    land_shapes = [_land_shape(kind, a) for a in flat[:nl]]

    def body(*refs):
        srcs, sems, token = refs[:n], refs[n + nl + na:n + nl + na + 2 * ng], refs[-1]
        lands = refs[n:n + nl] if nl else srcs
        x, y, c = _where_am_i()
        p = 0
        for gi, g in enumerate(groups):
            for q in range(len(g)):
                for k, peer in enumerate(_peers(kind, x, y, c)):
                    _ici_copy(kind, srcs[p], lands[p], sems[2 * gi], sems[2 * gi + 1], np_ * q + k, peer, x, y,
                              c).start()
                p += 1
        token[...] = jnp.zeros_like(token)

    sem_shapes = [pltpu.SemaphoreType.DMA((np_ * len(g),)) for g in groups for _ in range(2)]
    outs = pl.pallas_call(
        body, name=name, in_specs=[HBM] * (n + nl) + [ANY] * na,
        out_specs=[SEM] * (2 * ng) + [HBM] * (n + nl) + [pl.BlockSpec(memory_space=pltpu.VMEM)],
        out_shape=sem_shapes + [pltpu.HBM(a.shape, a.dtype) for a in flat]
        + [pltpu.HBM(s, a.dtype) for s, a in zip(land_shapes, flat)] + [jax.ShapeDtypeStruct((8, LANES), F32)],
        input_output_aliases={i: 2 * ng + i for i in range(n + nl)}, compiler_params=IN_FLIGHT,
    )(*[pltpu.with_memory_space_constraint(a, pltpu.HBM) for a in flat],
      *[pltpu.with_memory_space_constraint(lax.empty(s, a.dtype), pltpu.HBM) for s, a in zip(land_shapes, flat)],
      *after)
    sems, srcs, token = outs[:2 * ng], outs[2 * ng:2 * ng + n], outs[-1]
    lands = outs[2 * ng + n:2 * ng + n + nl] if nl else srcs
    started, p = [], 0
    for gi, g in enumerate(groups):
        started.append((sems[2 * gi], sems[2 * gi + 1], srcs[p:p + len(g)], lands[p:p + len(g)]))
        p += len(g)
    return started, token


def _ici_wait(kind, started, after, name):
    send_sems, recv_sems, srcs, lands = started
    n, np_ = len(srcs), _n_peers(kind)
    nl = 0 if kind in IN_PLACE else n
    after = list(after)

    def body(*refs):
        src_refs, send_ref, recv_ref = refs[:n], refs[n + nl], refs[n + nl + 1]
        land_refs = refs[n:n + nl] if nl else src_refs
        x, y, c = _where_am_i()
        for q in range(n):
            for k, peer in enumerate(_peers(kind, x, y, c)):
                args = (kind, src_refs[q], land_refs[q], send_ref, recv_ref, np_ * q + k, peer, x, y, c)
                _ici_copy(*args).wait_send()
                _ici_copy(*args, arriving=True).wait_recv()

    arrays = list(srcs) + (list(lands) if nl else [])
    outs = pl.pallas_call(
        body, name=name, in_specs=[HBM] * (n + nl) + [SEM, SEM] + [ANY] * len(after), out_specs=[HBM] * (n + nl),
        out_shape=[pltpu.HBM(a.shape, a.dtype) for a in arrays],
        input_output_aliases={i: i for i in range(n + nl)}, compiler_params=IN_FLIGHT,
    )(*arrays, send_sems, recv_sems, *after)
    return outs[:n], (outs[n:] if nl else outs[:n])


def _forward_to_sibling(lands, name):
    n = len(lands)

    def body(*refs):
        out_refs, (send_sems, recv_sems) = refs[n:2 * n], refs[2 * n:]
        x, y, c = _where_am_i()

        def copy(q, k, peer, half_of):
            rows = out_refs[q].at[2 * peer[0] + peer[1], _half_rows(lands[q].shape[1], half_of), :]
            return pltpu.make_async_remote_copy(
                src_ref=rows, dst_ref=rows, send_sem=send_sems.at[3 * q + k], recv_sem=recv_sems.at[3 * q + k],
                device_id=(x, y, 1 - c), device_id_type=MESH)

        sends = [copy(q, k, peer, c) for q in range(n) for k, peer in enumerate(_other_chips(x, y))]
        for cp in sends:
            cp.start()
        for q in range(n):
            for k, peer in enumerate(_other_chips(x, y)):
                copy(q, k, peer, 1 - c).wait_recv()
        for cp in sends:
            cp.wait_send()

    return pl.pallas_call(
        body, name=name, in_specs=[ANY] * n, out_specs=[ANY] * n,
        out_shape=[jax.ShapeDtypeStruct(a.shape, a.dtype) for a in lands],
        input_output_aliases={i: i for i in range(n)},
        scratch_shapes=[pltpu.SemaphoreType.DMA((3 * n,)), pltpu.SemaphoreType.DMA((3 * n,))],
    )(*lands)


def _with_own_slot(lands, own, chip):
    out = []
    for land, mine in zip(lands, own):
        mine = lax.dynamic_slice_in_dim(mine, chip, 1, axis=0) if mine.ndim == land.ndim else mine[None]
        out.append(lax.dynamic_update_slice(land, mine, (chip,) + (0,) * (land.ndim - 1)))
    return out


def _row_tile(rows, row_bytes, n_bufs, budget=24 * 1024 * 1024):
    best = None
    for t in range(PACKED_ROWS, rows + 1, PACKED_ROWS):
        if rows % t == 0 and 2 * n_bufs * t * row_bytes <= budget:
            best = t
    assert best is not None, (rows, row_bytes)
    return best


def _add_halves(full, recv, core, out_dtype, name):
    n, h, cols = recv.shape
    tr = _row_tile(h, cols * 4, 3)
    nt = h // tr

    def body(core_ref, a_ref, b_ref, o_ref):
        o_ref[...] = (a_ref[...].astype(F32) + b_ref[...].astype(F32)).astype(out_dtype)

    blk = (None, tr, cols)
    return pl.pallas_call(
        body, name=name,
        grid_spec=pltpu.PrefetchScalarGridSpec(
            num_scalar_prefetch=1, grid=(n, nt),
            in_specs=[pl.BlockSpec(blk, lambda j, i, core_ref: (j, core_ref[0] * nt + i, 0)),
                      pl.BlockSpec(blk, lambda j, i, core_ref: (j, i, 0))],
            out_specs=pl.BlockSpec(blk, lambda j, i, core_ref: (j, i, 0))),
        out_shape=jax.ShapeDtypeStruct((n, h, cols), out_dtype),
        compiler_params=_params(("parallel", "parallel")),
    )(core.reshape(1), full, recv)


def _add2(a, b, name):
    r, cols = a.shape
    tr = _row_tile(r, cols * 4, 3)

    def body(a_ref, b_ref, o_ref):
        o_ref[...] = a_ref[...] + b_ref[...]

    blk = pl.BlockSpec((tr, cols), lambda i: (i, 0))
    return pl.pallas_call(body, name=name, grid=(r // tr,), in_specs=[blk, blk], out_specs=blk,
                          out_shape=jax.ShapeDtypeStruct(a.shape, a.dtype), compiler_params=_params(("parallel",)))(a, b)


def _sum_slots(parts, name):
    n, h, cols = parts.shape
    tr = _row_tile(h, cols * 4, n + 1)

    def body(p_ref, o_ref):
        acc = p_ref[0].astype(F32)
        for k in range(1, n):
            acc = acc + p_ref[k].astype(F32)
        o_ref[...] = acc

    return pl.pallas_call(
        body, name=name, grid=(h // tr,), in_specs=[pl.BlockSpec((n, tr, cols), lambda i: (0, i, 0))],
        out_specs=pl.BlockSpec((tr, cols), lambda i: (i, 0)), out_shape=jax.ShapeDtypeStruct((h, cols), F32),
        compiler_params=_params(("parallel",)),
    )(parts)


def _adamw(w, g, m, v, name):
    shape = w.shape
    cols = shape[-1]
    rows = math.prod(shape[:-1])
    tr = rows
    if rows % 8 == 0:
        tr = 8
        for t in range(8, rows + 1, 8):
            if rows % t == 0 and t * cols * 4 <= 1536 * 1024:
                tr = t
    c1 = 1.0 - ADAM_B1 ** ADAM_STEP
    c2 = 1.0 - ADAM_B2 ** ADAM_STEP

    def body(w_ref, g_ref, m_ref, v_ref, d_ref, nm_ref, nv_ref):
        gg = g_ref[...]
        nm = ADAM_B1 * m_ref[...] + (1.0 - ADAM_B1) * gg
        nv = ADAM_B2 * v_ref[...] + (1.0 - ADAM_B2) * (gg * gg)
        d_ref[...] = -ADAM_LR * ((nm / c1) / (jnp.sqrt(nv / c2) + ADAM_EPS) + ADAM_WD * w_ref[...])
        nm_ref[...] = nm
        nv_ref[...] = nv

    blk = pl.BlockSpec((tr, cols), lambda i: (i, 0))
    flat = jax.ShapeDtypeStruct((rows, cols), F32)
    outs = pl.pallas_call(
        body, name=name, grid=(rows // tr,), in_specs=[blk] * 4, out_specs=[blk] * 3, out_shape=[flat] * 3,
        compiler_params=_params(("parallel",)),
    )(*[a.reshape(rows, cols) for a in (w, g, m, v)])
    return [o.reshape(shape) for o in outs]


def _adamw_update(w, g, m, v):
    c1 = 1.0 - ADAM_B1 ** ADAM_STEP
    c2 = 1.0 - ADAM_B2 ** ADAM_STEP
    nm = ADAM_B1 * m + (1.0 - ADAM_B1) * g
    nv = ADAM_B2 * v + (1.0 - ADAM_B2) * (g * g)
    return -ADAM_LR * ((nm / c1) / (jnp.sqrt(nv / c2) + ADAM_EPS) + ADAM_WD * w), nm, nv


def _adamw_projection(w, m, v, mine, theirs, core, name):
    _, r, cols = w.shape
    h = r // 2
    tr = _row_tile(h, cols * 4, 12, budget=40 * 1024 * 1024)
    nt = h // tr

    def body(core_ref, w_ref, m_ref, v_ref, a0, b0, a1, b1, g_ref, d_ref, nm_ref, nv_ref):
        layer, half = pl.program_id(0), pl.program_id(1)
        own = half == core_ref[0]
        g = jnp.where(layer == 0, jnp.where(own, a0[...], b0[...]), jnp.where(own, a1[...], b1[...]))
        g_ref[...] = g
        d_ref[...], nm_ref[...], nv_ref[...] = _adamw_update(w_ref[...], g, m_ref[...], v_ref[...])

    def piece(layer, own):
        def index(l, hh, i, core_ref):
            used = (l == layer) & ((hh == core_ref[0]) == own)
            return (jnp.where(used, i, 0), 0)
        return pl.BlockSpec((tr, cols), index)

    whole = pl.BlockSpec((None, tr, cols), lambda l, hh, i, core_ref: (l, hh * nt + i, 0))
    return pl.pallas_call(
        body, name=name,
        grid_spec=pltpu.PrefetchScalarGridSpec(
            num_scalar_prefetch=1, grid=(2, 2, nt),
            in_specs=[whole] * 3 + [piece(0, True), piece(0, False), piece(1, True), piece(1, False)],
            out_specs=[whole] * 4),
        out_shape=[jax.ShapeDtypeStruct(w.shape, F32)] * 4,
        compiler_params=_params(("arbitrary", "arbitrary", "arbitrary")),
    )(core.reshape(1), w, m, v, mine[0], theirs[0], mine[1], theirs[1])


def _chip_shape(name):
    shape = list(FULL_SHAPES[name])
    axis = BIG_SHARD_AXIS.get(name, SMALL_SHARD_AXIS.get(name))
    if axis is not None:
        shape[axis] //= N_CHIPS
    return tuple(shape)


def _round_up(n, k):
    return -(-n // k) * k


SHARDED_SMALL_WORDS = sum(math.prod(_chip_shape(n)) for n in SMALL_SHARDED)
SHARDED_SMALL_ROWS = _round_up(-(-SHARDED_SMALL_WORDS // PACK_COLS), 32)
SMALL_MATRICES = ['even_a_ws', 'odd_w_a', 'odd_w_x', 'odd_w_pool']
SMALL_VECTORS = [n for n in SMALL_NAMES if n not in SMALL_MATRICES]
VECTOR_ROWS = _round_up(-(-(sum(math.prod(FULL_SHAPES[n]) for n in SMALL_VECTORS) + 1) // PACK_COLS), 32)


def _pack_rows(flat, rows):
    return jnp.pad(flat, (0, rows * PACK_COLS - flat.shape[0])).reshape(rows, PACK_COLS)


def _pack_small_sharded(shards):
    return _pack_rows(jnp.concatenate([shards[n].reshape(-1) for n in SMALL_SHARDED]), SHARDED_SMALL_ROWS)


def _unpack_small_sharded(gathered_small):
    out = {}
    words = gathered_small.reshape(N_CHIPS, -1)
    w0 = 0
    for n in SMALL_SHARDED:
        size = math.prod(_chip_shape(n))
        blocks = words[:, w0:w0 + size].reshape((N_CHIPS,) + _chip_shape(n))
        out[n] = jnp.concatenate([blocks[j] for j in range(N_CHIPS)], axis=SMALL_SHARD_AXIS[n])
        w0 += size
    return out


def _small_grad_parts(grads, loss):
    words = [grads[n].reshape(-1) for n in SMALL_VECTORS] + [loss.reshape(-1)]
    return ([_pack_rows(jnp.concatenate(words), VECTOR_ROWS)]
            + [grads[n].reshape(-1, FULL_SHAPES[n][-1]) for n in SMALL_MATRICES])


def _own_block(name, full, chip):
    if name not in SMALL_SHARD_AXIS:
        return full
    axis = SMALL_SHARD_AXIS[name]
    width = FULL_SHAPES[name][axis] // N_CHIPS
    return lax.dynamic_slice_in_dim(full, chip * width, width, axis)


def _from_small_grad_parts(totals, chip):
    flat = totals[0].reshape(-1)
    out = {}
    w0 = 0
    for n in SMALL_VECTORS:
        size = math.prod(FULL_SHAPES[n])
        out[n] = _own_block(n, flat[w0:w0 + size].reshape(FULL_SHAPES[n]), chip)
        w0 += size
    for n, total in zip(SMALL_MATRICES, totals[1:]):
        out[n] = _own_block(n, total.reshape(FULL_SHAPES[n]), chip)
    return out, flat[w0]


def kernel(x, positions, even_w_in, even_a_ln_g, even_a_ln_b, even_a_ws, even_a_bs, even_b_sinks, even_w_out, even_ln_g, even_ln_b, odd_w_in, odd_conv_w, odd_conv_b, odd_w_a, odd_b_a, odd_w_x, odd_b_x, odd_lam, odd_w_pool, odd_d_scale, odd_w_out, odd_ln_g, odd_ln_b, loss_target, m_even_w_in, m_even_a_ln_g, m_even_a_ln_b, m_even_a_ws, m_even_a_bs, m_even_b_sinks, m_even_w_out, m_even_ln_g, m_even_ln_b, m_odd_w_in, m_odd_conv_w, m_odd_conv_b, m_odd_w_a, m_odd_b_a, m_odd_w_x, m_odd_b_x, m_odd_lam, m_odd_w_pool, m_odd_d_scale, m_odd_w_out, m_odd_ln_g, m_odd_ln_b, v_even_w_in, v_even_a_ln_g, v_even_a_ln_b, v_even_a_ws, v_even_a_bs, v_even_b_sinks, v_even_w_out, v_even_ln_g, v_even_ln_b, v_odd_w_in, v_odd_conv_w, v_odd_conv_b, v_odd_w_a, v_odd_b_a, v_odd_w_x, v_odd_b_x, v_odd_lam, v_odd_w_pool, v_odd_d_scale, v_odd_w_out, v_odd_ln_g, v_odd_ln_b):
    local = dict(zip(WEIGHT_NAMES, (even_w_in, even_a_ln_g, even_a_ln_b, even_a_ws, even_a_bs, even_b_sinks, even_w_out, even_ln_g, even_ln_b, odd_w_in, odd_conv_w, odd_conv_b, odd_w_a, odd_b_a, odd_w_x, odd_b_x, odd_lam, odd_w_pool, odd_d_scale, odd_w_out, odd_ln_g, odd_ln_b)))
    mom = dict(zip(WEIGHT_NAMES, (m_even_w_in, m_even_a_ln_g, m_even_a_ln_b, m_even_a_ws, m_even_a_bs, m_even_b_sinks, m_even_w_out, m_even_ln_g, m_even_ln_b, m_odd_w_in, m_odd_conv_w, m_odd_conv_b, m_odd_w_a, m_odd_b_a, m_odd_w_x, m_odd_b_x, m_odd_lam, m_odd_w_pool, m_odd_d_scale, m_odd_w_out, m_odd_ln_g, m_odd_ln_b)))
    vel = dict(zip(WEIGHT_NAMES, (v_even_w_in, v_even_a_ln_g, v_even_a_ln_b, v_even_a_ws, v_even_a_bs, v_even_b_sinks, v_even_w_out, v_even_ln_g, v_even_ln_b, v_odd_w_in, v_odd_conv_w, v_odd_conv_b, v_odd_w_a, v_odd_b_a, v_odd_w_x, v_odd_b_x, v_odd_lam, v_odd_w_pool, v_odd_d_scale, v_odd_w_out, v_odd_ln_g, v_odd_ln_b)))
    mx, my, core = _where_am_i()
    chip = 2 * mx + my

    stored = lambda n, a: jnp.swapaxes(a, 1, 2) if n == 'even_w_in' and W_IN_TRANSPOSED['even'] else a
    names_of = lambda layer: [('even' if layer % 2 == 0 else 'odd') + s for s in ('_w_in', '_w_out')]

    shards = [[stored(n, local[n])[layer // 2].astype(MXU_DTYPE) for n in names_of(layer)] for layer in range(DEPTH)]
    (first_in, first_out), first_token = _ici_start(
        'gather', [[_pack_small_sharded(local), shards[0][0]], [shards[0][1]]], "gather_start_first")
    later, gather_token = _ici_start('gather', shards[1:], "gather_start", after=[first_token])
    gathers = [None] + later

    own, lands = _ici_wait('gather', first_in, [first_token, gather_token], "gather_wait_0")
    small_sharded, first_w_in = _with_own_slot(_forward_to_sibling(lands, "gather_forward_0"), own, chip)
    weights = dict(local)
    weights.update(_unpack_small_sharded(small_sharded))
    passing = {}

    def passed_on(started, after, name):
        own, lands = _ici_wait('gather', started, [after], "gather_wait_" + name)
        (forward,), token = _ici_start('forward', [lands], "forward_start_" + name)

        def finish(then):
            done, _ = _ici_wait('forward', forward, [then], "forward_wait_" + name)
            return _with_own_slot(done, own, chip)

        return token, finish

    def first_w_out(h):
        token, finish = passed_on(first_out, h, "0_out")
        return token, lambda mix: finish(mix)[0]

    def projections_ahead(layer, after):
        token, passing[layer] = passed_on(gathers[layer], after, str(layer))
        return token

    def projections_of(layer, after):
        return (first_w_in, first_w_out) if layer == 0 else passing.pop(layer)(after)

    swapping, in_flight, mine, started = [], [], {}, {}

    def land(*after):
        layer, scatter = in_flight.pop()
        parts, lands = _ici_wait('scatter', scatter, after, "scatter_wait_%d" % layer)
        for n, arrived in zip(names_of(layer), _with_own_slot(lands, parts, chip)):
            mine[n, layer // 2] = _sum_slots(arrived, "big_sum_%s_%d" % (n, layer // 2))

    odd_pieces = [(n, l) for n in ('odd_w_in', 'odd_w_out') for l in range(2)]
    joining = []

    def projection_grads(layer, dw_in, dw_out):
        behind = []
        if in_flight:
            land(dw_in)
        if layer == 0:
            (join,), token = _ici_start('join', [[mine[p] for p in odd_pieces]], "join_odd_start")
            joining.append(join)
            behind = [token]
        (swap,), token = _ici_start('swap', [[dw_in, dw_out]], "presum_start_%d" % layer, after=behind)
        swapping.append((layer, swap))
        return presum_and_scatter(token) if layer == 0 else token

    def presum_and_scatter(*after):
        layer, swap = swapping.pop()
        partial, from_sibling = _ici_wait('swap', swap, after, "presum_wait_%d" % layer)
        chip_part = [_add_halves(g, r, core, MXU_DTYPE, "big_presum_add_%s_%d" % (n, layer // 2))
                     for g, r, n in zip(partial, from_sibling, names_of(layer))]
        (scatter,), started[layer] = _ici_start('scatter', [chip_part], "scatter_start_%d" % layer)
        in_flight.append((layer, scatter))
        return started[layer]

    norm_grads_done = lambda layer, dz: presum_and_scatter(dz) if swapping else dz
    loss, grad_x, grads = _local_step(x[0], positions[0], loss_target[0], weights, projections_of, projections_ahead,
                                      projection_grads, norm_grads_done)

    small = _small_grad_parts(grads, loss)
    chip_sums = [_add2(a, b, "small_presum_add_%d" % i)
                 for i, (a, b) in enumerate(zip(small, _sibling_swap(small, "small_swap", after=[started[0]])))]
    (small_gather,), small_token = _ici_start('gather', [chip_sums], "small_gather_start")

    reduced, deltas, new_m, new_v = {}, {}, {}, {}

    def adamw_projections(kind, after):
        names = [kind + '_w_in', kind + '_w_out']
        pieces = [(n, l) for n in names for l in range(2)]
        if kind == 'odd':
            halves, theirs = _ici_wait('join', joining.pop(), after, "join_odd_wait")
            mine.update(zip(pieces, halves))
        else:
            theirs = _sibling_swap([mine[p] for p in pieces], "big_join_" + kind, after=after)
        theirs = dict(zip(pieces, theirs))
        done = []
        for n in names:
            outs = _adamw_projection(stored(n, local[n]), stored(n, mom[n]), stored(n, vel[n]),
                                     [mine[n, l] for l in range(2)], [theirs[n, l] for l in range(2)], core,
                                     "adamw_" + n)
            reduced[n], deltas[n], new_m[n], new_v[n] = [stored(n, o) for o in outs]
            done.append(outs[-1])
        return done

    done = adamw_projections('odd', [small_token])
    land(grad_x, *done)
    done = adamw_projections('even', [])

    own, lands = _ici_wait('gather', small_gather, done, "small_gather_wait")
    of_chips = _with_own_slot(_forward_to_sibling(lands, "small_gather_forward"), own, chip)
    small_totals, total_loss = _from_small_grad_parts(
        [_sum_slots(g, "small_sum_%d" % i) for i, g in enumerate(of_chips)], chip)
    reduced.update(small_totals)
    for n in SMALL_NAMES:
        deltas[n], new_m[n], new_v[n] = _adamw(local[n], reduced[n], mom[n], vel[n], "adamw_" + n)

    return (total_loss, grad_x[None], *[reduced[n] for n in WEIGHT_NAMES], *[deltas[n] for n in WEIGHT_NAMES],
            *[new_m[n] for n in WEIGHT_NAMES], *[new_v[n] for n in WEIGHT_NAMES])
```

```python
import math

import jax
import jax.numpy as jnp
from jax import lax
from jax.experimental import pallas as pl
from jax.experimental.pallas import tpu as pltpu

F32 = jnp.float32
BF16 = jnp.bfloat16
MXU_DTYPE = jnp.bfloat16

D_MODEL = 2048
DEPTH = 4
A_WIDTH = 1024
A_GROUPS = 8
CHUNK = 128
B_HEAD_DIM = 64
B_Q_HEADS = 16
B_KV_HEADS = 2
WINDOW = 128
ROT_DIM = 16
ROPE_THETA = 500000.0
C_WIDTH = 1024
C_HEADS = 8
CONV_WIDTH = 4
LRU_C = 8.0
D_WIDTH = 1024
D_GROUPS = 4
D_GROUP_DIM = 256
EVEN_IN = 5376
ODD_IN = 4096
DN_ALPHA = (2 * DEPTH) ** 0.25
LN_EPS = 1e-5
ATTN_SCALE = B_HEAD_DIM ** -0.5
NEG_BIG = -1e30

ADAM_LR = 0.001
ADAM_B1 = 0.9
ADAM_B2 = 0.999
ADAM_EPS = 1e-08
ADAM_WD = 0.01
ADAM_STEP = 10

LANES = 128
VMEM_LIMIT = 56 * 1024 * 1024
N_CHIPS = 4
MESH = pl.DeviceIdType.MESH

WEIGHT_NAMES = ['even_w_in', 'even_a_ln_g', 'even_a_ln_b', 'even_a_ws', 'even_a_bs', 'even_b_sinks', 'even_w_out',
                'even_ln_g', 'even_ln_b', 'odd_w_in', 'odd_conv_w', 'odd_conv_b', 'odd_w_a', 'odd_b_a', 'odd_w_x',
                'odd_b_x', 'odd_lam', 'odd_w_pool', 'odd_d_scale', 'odd_w_out', 'odd_ln_g', 'odd_ln_b']
BIG_SHARD_AXIS = {'even_w_in': 2, 'even_w_out': 1, 'odd_w_in': 2, 'odd_w_out': 1}
SMALL_SHARD_AXIS = {'odd_conv_w': 2, 'odd_conv_b': 1, 'odd_b_a': 1, 'odd_b_x': 1, 'odd_lam': 1, 'odd_w_pool': 2,
                    'odd_d_scale': 1, 'odd_ln_g': 1, 'odd_ln_b': 1}
FULL_SHAPES = {
    'even_w_in': (2, 2048, 5376), 'even_a_ln_g': (2, 1024), 'even_a_ln_b': (2, 1024), 'even_a_ws': (2, 8, 128, 128),
    'even_a_bs': (2, 8, 128), 'even_b_sinks': (2, 16), 'even_w_out': (2, 2048, 2048), 'even_ln_g': (2, 2048),
    'even_ln_b': (2, 2048), 'odd_w_in': (2, 2048, 4096), 'odd_conv_w': (2, 4, 1024), 'odd_conv_b': (2, 1024),
    'odd_w_a': (2, 8, 128, 128), 'odd_b_a': (2, 1024), 'odd_w_x': (2, 8, 128, 128), 'odd_b_x': (2, 1024),
    'odd_lam': (2, 1024), 'odd_w_pool': (2, 4, 256, 256), 'odd_d_scale': (2, 1024), 'odd_w_out': (2, 2048, 2048),
    'odd_ln_g': (2, 2048), 'odd_ln_b': (2, 2048)}
BIG_NAMES = ['even_w_in', 'even_w_out', 'odd_w_in', 'odd_w_out']
SMALL_SHARDED = ['odd_conv_w', 'odd_conv_b', 'odd_b_a', 'odd_b_x', 'odd_lam', 'odd_d_scale', 'odd_ln_g', 'odd_ln_b',
                 'odd_w_pool']
SMALL_NAMES = [n for n in WEIGHT_NAMES if n not in BIG_NAMES]
W_IN_TRANSPOSED = {'even': True, 'odd': False}
PACK_COLS = 1024


def _params(sem):
    return pltpu.CompilerParams(dimension_semantics=sem, vmem_limit_bytes=VMEM_LIMIT)


def _dot(a, b):
    return lax.dot_general(a, b, (((1,), (0,)), ((), ())), preferred_element_type=F32)


def _dot_nt(a, b):
    return lax.dot_general(a, b, (((1,), (1,)), ((), ())), preferred_element_type=F32)


def _dot_tn(a, b):
    return lax.dot_general(a, b, (((0,), (0,)), ((), ())), preferred_element_type=F32)


def _sigmoid(x):
    return 0.5 * jnp.tanh(0.5 * x) + 0.5


def _silu_and_grad(x):
    s = _sigmoid(x)
    return x * s, s * (1.0 + x * (1.0 - s))


def _mm_call(body, name, grid, in_specs, out_spec, out_shape, args):
    return pl.pallas_call(body, name=name, grid=grid, in_specs=in_specs, out_specs=out_spec, out_shape=out_shape,
                          compiler_params=_params(("parallel",) * len(grid)))(*args)


def _resident(block, index_map):
    return pl.BlockSpec(block, index_map, pipeline_mode=pl.Buffered(1))


ROWS_OF_WT = 768


def _mm_in(x, wg, transposed, name):
    s, k = x.shape
    x_spec = _resident((s, k), lambda j: (0, 0))

    def body(x_ref, w_ref, o_ref):
        o_ref[...] = (_dot_nt if transposed else _dot)(x_ref[...], w_ref[...])

    if transposed:
        wt = wg.reshape(-1, k)
        n, tn = wt.shape[0], ROWS_OF_WT
        return _mm_call(body, name, (n // tn,), [x_spec, pl.BlockSpec((tn, k), lambda j: (j, 0))],
                        pl.BlockSpec((s, tn), lambda j: (0, j)), jax.ShapeDtypeStruct((s, n), F32), (x, wt))
    nc = wg.shape[2]
    return _mm_call(body, name, (N_CHIPS,), [x_spec, pl.BlockSpec((None, k, nc), lambda j: (j, 0, 0))],
                    pl.BlockSpec((None, s, nc), lambda j: (j, 0, 0)),
                    jax.ShapeDtypeStruct((N_CHIPS, s, nc), F32), (x, wg))


def _mm_dx(dh, wg, dz, transposed, name):
    s, d = dz.shape
    tm, tn = min(1024, s), 512
    res = pl.BlockSpec((tm, tn), lambda i, j: (i, j))
    if transposed:
        wt = wg.reshape(-1, d)
        n = wt.shape[0]

        def body(a_ref, w_ref, r_ref, o_ref):
            o_ref[...] = DN_ALPHA * r_ref[...] + _dot(a_ref[...], w_ref[...])

        return _mm_call(body, name, (s // tm, d // tn),
                        [pl.BlockSpec((tm, n), lambda i, j: (i, 0)), pl.BlockSpec((n, tn), lambda i, j: (0, j)), res],
                        res, jax.ShapeDtypeStruct((s, d), F32), (dh, wt, dz))
    nc = dh.shape[2]

    def body(a_ref, w_ref, r_ref, o_ref):
        acc = DN_ALPHA * r_ref[...]
        for j in range(N_CHIPS):
            acc = acc + _dot_nt(a_ref[j], w_ref[j])
        o_ref[...] = acc

    return _mm_call(body, name, (s // tm, d // tn),
                    [pl.BlockSpec((N_CHIPS, tm, nc), lambda i, j: (0, i, 0)),
                     pl.BlockSpec((N_CHIPS, tn, nc), lambda i, j: (0, j, 0)), res],
                    res, jax.ShapeDtypeStruct((s, d), F32), (dh, wg, dz))


def _mm_dw_in(x, dh, transposed, name):
    s, d = x.shape
    x_spec = _resident((s, d), lambda j: (0, 0))

    def body(a_ref, b_ref, o_ref):
        o_ref[...] = _dot_tn(a_ref[...], b_ref[...]).astype(o_ref.dtype)

    if transposed:
        n, tn = dh.shape[1], ROWS_OF_WT
        flat = _mm_call(body, name, (n // tn,), [pl.BlockSpec((s, tn), lambda j: (0, j)), x_spec],
                        pl.BlockSpec((tn, d), lambda j: (j, 0)), jax.ShapeDtypeStruct((n, d), MXU_DTYPE), (dh, x))
        return flat.reshape(N_CHIPS, n // N_CHIPS, d)
    nc = dh.shape[2]
    return _mm_call(body, name, (N_CHIPS,), [x_spec, pl.BlockSpec((None, s, nc), lambda j: (j, 0, 0))],
                    pl.BlockSpec((None, d, nc), lambda j: (j, 0, 0)),
                    jax.ShapeDtypeStruct((N_CHIPS, d, nc), MXU_DTYPE), (x, dh))


def _mm_out_norm(mix, wg, x, g, b, name, target=None, after=None):
    s, k = mix.shape
    _, kc, d = wg.shape
    tm = min(512, s)

    def normed(a_ref, w_ref, x_ref, g_ref, b_ref, z_ref):
        z = DN_ALPHA * x_ref[...] + _dot(a_ref[...], w_ref[...].reshape(N_CHIPS * kc, d))
        mu = jnp.mean(z, axis=1, keepdims=True)
        zc = z - mu
        var = jnp.mean(zc * zc, axis=1, keepdims=True)
        z_ref[...] = z
        return zc * lax.rsqrt(var + LN_EPS) * g_ref[...] + b_ref[...]

    def body(a_ref, w_ref, x_ref, g_ref, b_ref, after_ref, z_ref, o_ref, ob_ref):
        o = normed(a_ref, w_ref, x_ref, g_ref, b_ref, z_ref)
        o_ref[...] = o
        ob_ref[...] = o.astype(ob_ref.dtype)

    def body_with_loss(a_ref, w_ref, x_ref, g_ref, b_ref, t_ref, z_ref, do_ref, l_ref):
        @pl.when(pl.program_id(0) == 0)
        def _():
            l_ref[...] = jnp.zeros_like(l_ref)

        e = normed(a_ref, w_ref, x_ref, g_ref, b_ref, z_ref) - t_ref[...]
        do_ref[...] = e * (1.0 / d)
        l_ref[...] += (0.5 / d) * jnp.sum(jnp.sum(e * e, axis=1, keepdims=True), axis=0, keepdims=True)

    row = pl.BlockSpec((tm, d), lambda i: (i, 0))
    vec = pl.BlockSpec((1, d), lambda i: (0, 0))
    in_specs = [pl.BlockSpec((tm, k), lambda i: (i, 0)), _resident((N_CHIPS, kc, d), lambda i: (0, 0, 0)), row, vec,
                vec]
    args = (mix, wg, x, g.reshape(1, d), b.reshape(1, d))
    f32_rows = jax.ShapeDtypeStruct((s, d), F32)
    if target is None:
        return pl.pallas_call(
            body, name=name, grid=(s // tm,), in_specs=in_specs + [ANY], out_specs=[row, row, row],
            out_shape=[f32_rows, f32_rows, jax.ShapeDtypeStruct((s, d), MXU_DTYPE)],
            compiler_params=_params(("parallel",)))(*args, after)
    return pl.pallas_call(
        body_with_loss, name=name, grid=(s // tm,), in_specs=in_specs + [row],
        out_specs=[row, row, pl.BlockSpec((1, 1), lambda i: (0, 0))],
        out_shape=[f32_rows, f32_rows, jax.ShapeDtypeStruct((1, 1), F32)],
        compiler_params=_params(("arbitrary",)))(*args, target)


def _mm_dmix(dz, wg, after, name):
    s, d = dz.shape
    kc = wg.shape[1]

    def body(a_ref, w_ref, after_ref, o_ref):
        o_ref[...] = _dot_nt(a_ref[...], w_ref[...])

    return _mm_call(body, name, (N_CHIPS,),
                    [_resident((s, d), lambda j: (0, 0)), pl.BlockSpec((None, kc, d), lambda j: (j, 0, 0)), ANY],
                    pl.BlockSpec((s, kc), lambda j: (0, j)), jax.ShapeDtypeStruct((s, N_CHIPS * kc), F32),
                    (dz, wg, after))


def _mm_dw_out(mix, dz, name):
    s, k = mix.shape
    d = dz.shape[1]
    kc = k // N_CHIPS

    def body(a_ref, b_ref, o_ref):
        o_ref[...] = _dot_tn(a_ref[...], b_ref[...]).astype(o_ref.dtype)

    return _mm_call(body, name, (N_CHIPS,),
                    [pl.BlockSpec((s, kc), lambda i: (0, i)), _resident((s, d), lambda i: (0, 0))],
                    pl.BlockSpec((None, kc, d), lambda i: (i, 0, 0)),
                    jax.ShapeDtypeStruct((N_CHIPS, kc, d), MXU_DTYPE), (mix, dz))


def _ln_bwd(dout, z, g, after):
    s, d = z.shape
    tr = min(256, s)

    def body(do_ref, z_ref, g_ref, after_ref, dz_ref, dzb_ref, dg_ref, db_ref):
        @pl.when(pl.program_id(0) == 0)
        def _():
            dg_ref[...] = jnp.zeros_like(dg_ref)
            db_ref[...] = jnp.zeros_like(db_ref)

        zz = z_ref[...]
        do = do_ref[...]
        mu = jnp.mean(zz, axis=1, keepdims=True)
        zc = zz - mu
        var = jnp.mean(zc * zc, axis=1, keepdims=True)
        rstd = lax.rsqrt(var + LN_EPS)
        zh = zc * rstd
        dg_ref[...] += jnp.sum(do * zh, axis=0, keepdims=True)
        db_ref[...] += jnp.sum(do, axis=0, keepdims=True)
        dzh = do * g_ref[...]
        dz = rstd * (dzh - jnp.mean(dzh, axis=1, keepdims=True) - zh * jnp.mean(dzh * zh, axis=1, keepdims=True))
        dz_ref[...] = dz
        dzb_ref[...] = dz.astype(MXU_DTYPE)

    row = pl.BlockSpec((tr, d), lambda i: (i, 0))
    vec = pl.BlockSpec((1, d), lambda i: (0, 0))
    return pl.pallas_call(
        body, name="ln_bwd", grid=(s // tr,), in_specs=[row, row, vec, ANY], out_specs=[row, row, vec, vec],
        out_shape=[jax.ShapeDtypeStruct((s, d), F32), jax.ShapeDtypeStruct((s, d), MXU_DTYPE),
                   jax.ShapeDtypeStruct((1, d), F32), jax.ShapeDtypeStruct((1, d), F32)],
        compiler_params=_params(("arbitrary",)),
    )(dout, z, g.reshape(1, d), after)


Q0, K0, V0, BG0 = 3072, 4096, 4224, 4352
T = CHUNK


def _lane_ids(width):
    return lax.broadcasted_iota(jnp.int32, (T, width), 1)


def _rope_swap(x):
    w = x.shape[1]
    l64 = _lane_ids(w) % B_HEAD_DIM
    half = ROT_DIM // 2
    return jnp.where(l64 < half, pltpu.roll(x, w - half, 1), jnp.where(l64 < ROT_DIM, pltpu.roll(x, half, 1), 0.0))


def _rope(x, c, s):
    return x * c + _rope_swap(x) * s


def _rope_bwd(d, c, s):
    return d * c + _rope_swap(d * s)


def _tril():
    return lax.broadcasted_iota(jnp.int32, (T, T), 0) >= lax.broadcasted_iota(jnp.int32, (T, T), 1)


def _band_mask(has_prev):
    qi = lax.broadcasted_iota(jnp.int32, (T, 2 * T), 0)
    kj = lax.broadcasted_iota(jnp.int32, (T, 2 * T), 1)
    return (kj > qi) & (kj <= qi + WINDOW) & ((kj >= WINDOW) | has_prev)


def _a_norm(v, g, b):
    mu = jnp.mean(v, axis=1, keepdims=True)
    vc = v - mu
    var = jnp.mean(vc * vc, axis=1, keepdims=True)
    rstd = lax.rsqrt(var + LN_EPS)
    vh = vc * rstd
    return vh, rstd, vh * g + b


def _kv_operands(hk, k_rot, kp_rot, vv, vp):
    lo = _lane_ids(LANES) < B_HEAD_DIM
    mine = lo if hk == 0 else jnp.logical_not(lo)
    both = lambda t: jnp.where(mine, t, 0.0) + pltpu.roll(jnp.where(mine, t, 0.0), B_HEAD_DIM, 1)
    k2 = jnp.concatenate([both(kp_rot), both(k_rot)], axis=0)
    v2 = jnp.concatenate([both(vp), both(vv)], axis=0)
    return k2.astype(MXU_DTYPE), v2.astype(MXU_DTYPE)


GROUP_HEADS = B_Q_HEADS // B_KV_HEADS


def _pair_columns(hk):
    return [(hk * GROUP_HEADS // 2 + pp) * LANES for pp in range(GROUP_HEADS // 2)]


def _stacked_heads(pairs):
    lo = _lane_ids(LANES) < B_HEAD_DIM
    rows = []
    for tile in pairs:
        rows += [jnp.where(lo, tile, 0.0), jnp.where(lo, 0.0, tile)]
    return jnp.concatenate(rows, axis=0).astype(MXU_DTYPE)


def _pair_of(stacked, pp):
    lo = _lane_ids(LANES) < B_HEAD_DIM
    return jnp.where(lo, stacked[2 * pp * T:(2 * pp + 1) * T], stacked[(2 * pp + 1) * T:(2 * pp + 2) * T])


def _softmax_with_sink(qm, k2, valid, sink_ref, head0):
    scores = (_dot_nt(qm, k2) * ATTN_SCALE).reshape(GROUP_HEADS, T, 2 * T)
    s = jnp.where(valid[None], scores, NEG_BIG).reshape(GROUP_HEADS * T, 2 * T)
    sink = jnp.concatenate([jnp.broadcast_to(sink_ref[:, head0 + h:head0 + h + 1], (T, 1))
                            for h in range(GROUP_HEADS)], axis=0)
    m = jnp.maximum(jnp.max(s, axis=1, keepdims=True), sink)
    ex = jnp.exp(s - m)
    es = jnp.exp(sink - m)
    inv = 1.0 / (jnp.sum(ex, axis=1, keepdims=True) + es)
    return ex * inv, es * inv


def _even_specs(s):
    nb = s // T
    prev = lambda r: lambda i: jnp.maximum(r(i) - 1, 0)
    return nb, {
        'h': lambda r: pl.BlockSpec((T, EVEN_IN), lambda i: (r(i), 0)),
        'kprev': lambda r: pl.BlockSpec((T, LANES), lambda i: (prev(r)(i), K0 // LANES)),
        'vprev': lambda r: pl.BlockSpec((T, LANES), lambda i: (prev(r)(i), V0 // LANES)),
        'tab': lambda r: pl.BlockSpec((T, LANES), lambda i: (r(i), 0)),
        'tabprev': lambda r: pl.BlockSpec((T, LANES), lambda i: (prev(r)(i), 0)),
    }


def _full(shape):
    nd = len(shape)
    return pl.BlockSpec(shape, lambda i: (0,) * nd)


def _even_fwd(h, cq, sq, ln_g, ln_b, ws, bs_t, sinks, after):
    s = h.shape[0]
    nb, sp = _even_specs(s)

    def body(h_ref, kp_ref, vp_ref, cq_ref, sq_ref, ckp_ref, skp_ref, g_ref, b_ref, ws_ref, bst_ref, sink_ref,
             after_ref, mix_ref):
        i = pl.program_id(0)
        tril = _tril()
        _, _, vn = _a_norm(h_ref[:, 1024:2048], g_ref[...], b_ref[...])
        vnb = vn.astype(MXU_DTYPE)
        for g in range(A_GROUPS):
            c0, c1 = g * LANES, (g + 1) * LANES
            wg = jnp.where(tril, ws_ref[g], 0.0).astype(MXU_DTYPE)
            mixed = _dot(wg, vnb[:, c0:c1]) + bst_ref[:, g:g + 1]
            sil, _ = _silu_and_grad(h_ref[:, 2048 + c0:2048 + c1])
            mix_ref[:, c0:c1] = (h_ref[:, c0:c1] * mixed * sil).astype(mix_ref.dtype)

        cq, sq = cq_ref[...], sq_ref[...]
        k_rot = _rope(h_ref[:, K0:K0 + LANES], cq, sq)
        kp_rot = _rope(kp_ref[...], ckp_ref[...], skp_ref[...])
        valid = _band_mask(i > 0)
        for hk in range(B_KV_HEADS):
            k2, v2 = _kv_operands(hk, k_rot, kp_rot, h_ref[:, V0:V0 + LANES], vp_ref[...])
            qm = _stacked_heads([_rope(h_ref[:, Q0 + c0:Q0 + c0 + LANES], cq, sq) for c0 in _pair_columns(hk)])
            p, _ = _softmax_with_sink(qm, k2, valid, sink_ref, GROUP_HEADS * hk)
            out = _dot(p.astype(MXU_DTYPE), v2)
            for pp, c0 in enumerate(_pair_columns(hk)):
                sil, _ = _silu_and_grad(h_ref[:, BG0 + c0:BG0 + c0 + LANES])
                mix_ref[:, A_WIDTH + c0:A_WIDTH + c0 + LANES] = (_pair_of(out, pp) * sil).astype(mix_ref.dtype)

    ident = lambda i: i
    return pl.pallas_call(
        body, name="even_fwd", grid=(nb,),
        in_specs=[sp['h'](ident), sp['kprev'](ident), sp['vprev'](ident), sp['tab'](ident), sp['tab'](ident),
                  sp['tabprev'](ident), sp['tabprev'](ident), _full((1, A_WIDTH)), _full((1, A_WIDTH)),
                  _full((A_GROUPS, T, T)), _full((T, A_GROUPS)), _full((1, B_Q_HEADS)), ANY],
        out_specs=pl.BlockSpec((T, 2 * A_WIDTH), lambda i: (i, 0)),
        out_shape=jax.ShapeDtypeStruct((s, 2 * A_WIDTH), MXU_DTYPE),
        compiler_params=_params(("parallel",)),
    )(h, h, h, cq, sq, cq, sq, ln_g.reshape(1, -1), ln_b.reshape(1, -1), ws, bs_t, sinks.reshape(1, -1), after)


def _even_bwd(h, dmix, cq, sq, ln_g, ln_b, ws, bs_t, sinks):
    s = h.shape[0]
    nb, sp = _even_specs(s)

    def body(h_ref, kp_ref, vp_ref, cq_ref, sq_ref, ckp_ref, skp_ref, dm_ref, g_ref, b_ref, ws_ref, bst_ref, sink_ref,
             dh_ref, dg_ref, db_ref, dws_ref, dbst_ref, dsink_ref, dvn_scr, ck_scr, cv_scr):
        i = pl.program_id(0)

        @pl.when(i == 0)
        def _():
            for ref in (dg_ref, db_ref, dws_ref, dbst_ref, dsink_ref, ck_scr, cv_scr):
                ref[...] = jnp.zeros_like(ref)

        tril = _tril()
        vh, rstd, vn = _a_norm(h_ref[:, 1024:2048], g_ref[...], b_ref[...])
        vnb = vn.astype(MXU_DTYPE)
        for g in range(A_GROUPS):
            c0, c1 = g * LANES, (g + 1) * LANES
            wg = jnp.where(tril, ws_ref[g], 0.0).astype(MXU_DTYPE)
            mixed = _dot(wg, vnb[:, c0:c1]) + bst_ref[:, g:g + 1]
            ag = h_ref[:, 2048 + c0:2048 + c1]
            u = h_ref[:, c0:c1]
            sil, dsil = _silu_and_grad(ag)
            da = dm_ref[:, c0:c1]
            dh_ref[:, c0:c1] = (da * mixed * sil).astype(dh_ref.dtype)
            dh_ref[:, 2048 + c0:2048 + c1] = (da * u * mixed * dsil).astype(dh_ref.dtype)
            dmx = da * u * sil
            dmb = dmx.astype(MXU_DTYPE)
            dvn_scr[:, c0:c1] = _dot_tn(wg, dmb)
            dws_ref[g] += jnp.where(tril, _dot_nt(dmb, vnb[:, c0:c1]), 0.0)
            dbst_ref[:, g:g + 1] += jnp.sum(dmx, axis=1, keepdims=True)
        dvn = dvn_scr[...]
        dg_ref[...] += jnp.sum(dvn * vh, axis=0, keepdims=True)
        db_ref[...] += jnp.sum(dvn, axis=0, keepdims=True)
        dvh = dvn * g_ref[...]
        dv = rstd * (dvh - jnp.mean(dvh, axis=1, keepdims=True) - vh * jnp.mean(dvh * vh, axis=1, keepdims=True))
        dh_ref[:, 1024:2048] = dv.astype(dh_ref.dtype)

        cq, sq = cq_ref[...], sq_ref[...]
        k_rot = _rope(h_ref[:, K0:K0 + LANES], cq, sq)
        kp_rot = _rope(kp_ref[...], ckp_ref[...], skp_ref[...])
        valid = _band_mask(i < nb - 1)
        dk_all = jnp.zeros((2 * T, LANES), F32)
        dv_all = jnp.zeros((2 * T, LANES), F32)
        for hk in range(B_KV_HEADS):
            k2, v2 = _kv_operands(hk, k_rot, kp_rot, h_ref[:, V0:V0 + LANES], vp_ref[...])
            cols = _pair_columns(hk)
            qm = _stacked_heads([_rope(h_ref[:, Q0 + c0:Q0 + c0 + LANES], cq, sq) for c0 in cols])
            p, psink = _softmax_with_sink(qm, k2, valid, sink_ref, GROUP_HEADS * hk)
            pb = p.astype(MXU_DTYPE)
            out = _dot(pb, v2)
            dobs = []
            for pp, c0 in enumerate(cols):
                sil, dsil = _silu_and_grad(h_ref[:, BG0 + c0:BG0 + c0 + LANES])
                dbo = dm_ref[:, A_WIDTH + c0:A_WIDTH + c0 + LANES]
                dh_ref[:, BG0 + c0:BG0 + c0 + LANES] = (dbo * _pair_of(out, pp) * dsil).astype(dh_ref.dtype)
                dobs.append(dbo * sil)
            dobm = _stacked_heads(dobs)
            dp = _dot_nt(dobm, v2)
            rs = jnp.sum(p * dp, axis=1, keepdims=True)
            dsb = (p * (dp - rs) * ATTN_SCALE).astype(MXU_DTYPE)
            to_sink = psink * rs
            for h in range(GROUP_HEADS):
                head = GROUP_HEADS * hk + h
                dsink_ref[:, head:head + 1] += -jnp.sum(to_sink[h * T:(h + 1) * T], axis=0, keepdims=True)
            dq = _dot(dsb, k2)
            for pp, c0 in enumerate(cols):
                dh_ref[:, Q0 + c0:Q0 + c0 + LANES] = _rope_bwd(_pair_of(dq, pp), cq, sq).astype(dh_ref.dtype)
            acc_k = _dot_tn(dsb, qm)
            acc_v = _dot_tn(pb, dobm)
            lo2 = lax.broadcasted_iota(jnp.int32, (2 * T, LANES), 1) < B_HEAD_DIM
            mine = lo2 if hk == 0 else jnp.logical_not(lo2)
            dk_all = dk_all + jnp.where(mine, acc_k + pltpu.roll(acc_k, B_HEAD_DIM, 1), 0.0)
            dv_all = dv_all + jnp.where(mine, acc_v + pltpu.roll(acc_v, B_HEAD_DIM, 1), 0.0)
        dk_rot = dk_all[T:2 * T] + ck_scr[...]
        dvv = dv_all[T:2 * T] + cv_scr[...]
        ck_scr[...] = dk_all[0:T]
        cv_scr[...] = dv_all[0:T]
        dh_ref[:, K0:K0 + LANES] = _rope_bwd(dk_rot, cq, sq).astype(dh_ref.dtype)
        dh_ref[:, V0:V0 + LANES] = dvv.astype(dh_ref.dtype)

    rev = lambda i: nb - 1 - i
    return pl.pallas_call(
        body, name="even_bwd", grid=(nb,),
        in_specs=[sp['h'](rev), sp['kprev'](rev), sp['vprev'](rev), sp['tab'](rev), sp['tab'](rev),
                  sp['tabprev'](rev), sp['tabprev'](rev), pl.BlockSpec((T, 2 * A_WIDTH), lambda i: (rev(i), 0)),
                  _full((1, A_WIDTH)), _full((1, A_WIDTH)), _full((A_GROUPS, T, T)), _full((T, A_GROUPS)),
                  _full((1, B_Q_HEADS))],
        out_specs=[sp['h'](rev), _full((1, A_WIDTH)), _full((1, A_WIDTH)),
                   _full((A_GROUPS, T, T)), _full((T, A_GROUPS)), _full((1, B_Q_HEADS))],
        out_shape=[jax.ShapeDtypeStruct((s, EVEN_IN), MXU_DTYPE), jax.ShapeDtypeStruct((1, A_WIDTH), F32),
                   jax.ShapeDtypeStruct((1, A_WIDTH), F32), jax.ShapeDtypeStruct((A_GROUPS, T, T), F32),
                   jax.ShapeDtypeStruct((T, A_GROUPS), F32), jax.ShapeDtypeStruct((1, B_Q_HEADS), F32)],
        scratch_shapes=[pltpu.VMEM((T, A_WIDTH), F32), pltpu.VMEM((T, LANES), F32), pltpu.VMEM((T, LANES), F32)],
        compiler_params=_params(("arbitrary",)),
    )(h, h, h, cq, sq, cq, sq, dmix, ln_g.reshape(1, -1), ln_b.reshape(1, -1), ws, bs_t, sinks.reshape(1, -1))


def _shift_down(x, d, fill=0.0):
    rows = lax.broadcasted_iota(jnp.int32, x.shape, 0)
    return jnp.where(rows >= d, pltpu.roll(x, d, 0), fill)


def _shift_up(x, d, fill=0.0):
    n = x.shape[0]
    rows = lax.broadcasted_iota(jnp.int32, x.shape, 0)
    return jnp.where(rows < n - d, pltpu.roll(x, n - d, 0), fill)


SCAN_BLOCK = 32
SUBLANES = 8


def _scan_block(a, b, up):
    n = a.shape[0]
    d = 1
    while d < n:
        if d < SUBLANES:
            shift = _shift_up if up else _shift_down
            a, b = a * shift(a, d, 1.0), a * shift(b, d) + b
        elif up:
            a, b = (jnp.concatenate([a[:n - d] * a[d:], a[n - d:]], axis=0),
                    jnp.concatenate([a[:n - d] * b[d:] + b[:n - d], b[n - d:]], axis=0))
        else:
            a, b = (jnp.concatenate([a[:d], a[d:] * a[:n - d]], axis=0),
                    jnp.concatenate([b[:d], a[d:] * b[:n - d] + b[d:]], axis=0))
        d *= 2
    return a, b


def _scan(a, b, up):
    n = a.shape[0]
    size = min(SCAN_BLOCK, n)
    blocks = n // size
    out, carry = [None] * blocks, None
    for k in (reversed(range(blocks)) if up else range(blocks)):
        prod, h = _scan_block(a[k * size:(k + 1) * size], b[k * size:(k + 1) * size], up)
        if carry is not None:
            h = h + prod * carry
        carry = h[0:1] if up else h[size - 1:size]
        out[k] = h
    return jnp.concatenate(out, axis=0)


def _neg_expm1(y):
    t = jnp.tanh(-0.5 * y)
    return 2.0 * t / (1.0 + t)


def _softplus(x):
    return jnp.maximum(x, 0.0) + jnp.log1p(jnp.exp(-jnp.abs(x)))


def _lru_forward(xc, cw, cb, wa, ba, wx, bx, lam):
    shifted = [_shift_down(xc, CONV_WIDTH - 1 - j) if j < CONV_WIDTH - 1 else xc for j in range(CONV_WIDTH)]
    xconv = cb
    for j in range(CONV_WIDTH):
        xconv = xconv + cw[j:j + 1, :] * shifted[j]
    xb = xconv.astype(MXU_DTYPE)
    r = _sigmoid(_dot(xb, wa.astype(MXU_DTYPE)) + ba)
    ig = _sigmoid(_dot(xb, wx.astype(MXU_DTYPE)) + bx)
    sp = _softplus(-lam)
    log_a = -LRU_C * r * sp
    a = jnp.exp(log_a)
    mult = jnp.sqrt(_neg_expm1(2.0 * log_a))
    hs = _scan(a, mult * ig * xconv, up=False)
    return dict(shifted=shifted, xconv=xconv, xb=xb, r=r, ig=ig, sp=sp, a=a, mult=mult, hs=hs)


def _col(width, off):
    return lambda s: pl.BlockSpec((s, width), lambda i: (0, off + i))


def _part(width, slot):
    return lambda s: pl.BlockSpec((None, s, width), lambda i: (slot, 0, i))


def _lru_specs(s):
    w = LANES
    return [_part(w, 0)(s), _part(w, 1)(s), pl.BlockSpec((CONV_WIDTH, w), lambda i: (0, i)),
            pl.BlockSpec((1, w), lambda i: (0, i)), pl.BlockSpec((None, w, w), lambda i: (i, 0, 0)),
            pl.BlockSpec((1, w), lambda i: (0, i)), pl.BlockSpec((None, w, w), lambda i: (i, 0, 0)),
            pl.BlockSpec((1, w), lambda i: (0, i)), pl.BlockSpec((1, w), lambda i: (0, i))]


def _lru_fwd(h, cw, cb, wa, ba, wx, bx, lam):
    s = h.shape[1]

    def body(xc_ref, cg_ref, cw_ref, cb_ref, wa_ref, ba_ref, wx_ref, bx_ref, lam_ref, o_ref):
        f = _lru_forward(xc_ref[...], cw_ref[...], cb_ref[...], wa_ref[...], ba_ref[...], wx_ref[...], bx_ref[...],
                         lam_ref[...])
        sil, _ = _silu_and_grad(cg_ref[...])
        o_ref[...] = (f['hs'] * sil).astype(o_ref.dtype)

    return pl.pallas_call(
        body, name="lru_fwd", grid=(C_HEADS,), in_specs=_lru_specs(s), out_specs=_col(LANES, 0)(s),
        out_shape=jax.ShapeDtypeStruct((s, C_WIDTH + D_WIDTH), MXU_DTYPE), compiler_params=_params(("parallel",)),
    )(h, h, cw, cb.reshape(1, -1), wa, ba.reshape(1, -1), wx, bx.reshape(1, -1), lam.reshape(1, -1))


def _lru_bwd(h, dmix, cw, cb, wa, ba, wx, bx, lam):
    s = h.shape[1]

    def body(xc_ref, cg_ref, cw_ref, cb_ref, wa_ref, ba_ref, wx_ref, bx_ref, lam_ref, dco_ref,
             dh_ref, dcw_ref, dcb_ref, dwa_ref, dba_ref, dwx_ref, dbx_ref, dlam_ref):
        dxc_ref, dcg_ref = dh_ref.at[0], dh_ref.at[1]
        cw, lam = cw_ref[...], lam_ref[...]
        wab, wxb = wa_ref[...].astype(MXU_DTYPE), wx_ref[...].astype(MXU_DTYPE)
        f = _lru_forward(xc_ref[...], cw, cb_ref[...], wa_ref[...], ba_ref[...], wx_ref[...], bx_ref[...], lam)
        sil, dsil = _silu_and_grad(cg_ref[...])
        dco = dco_ref[...]
        hs, a, mult, ig, r, xconv = f['hs'], f['a'], f['mult'], f['ig'], f['r'], f['xconv']
        dcg_ref[...] = (dco * hs * dsil).astype(dcg_ref.dtype)
        lamb = _scan(_shift_up(a, 1), dco * sil, up=True)
        da = lamb * _shift_down(hs, 1)
        d_ig = lamb * mult * xconv
        d_mult = lamb * ig * xconv
        dxconv = lamb * mult * ig
        d_log_a = da * a - d_mult * (a * a) / mult
        dlam_ref[...] = jnp.sum(d_log_a * r, axis=0, keepdims=True) * LRU_C * _sigmoid(-lam)
        dpa = d_log_a * (-LRU_C * f['sp']) * r * (1.0 - r)
        dpx = d_ig * ig * (1.0 - ig)
        dba_ref[...] = jnp.sum(dpa, axis=0, keepdims=True)
        dbx_ref[...] = jnp.sum(dpx, axis=0, keepdims=True)
        dpab, dpxb = dpa.astype(MXU_DTYPE), dpx.astype(MXU_DTYPE)
        dwa_ref[...] = _dot_tn(f['xb'], dpab)
        dwx_ref[...] = _dot_tn(f['xb'], dpxb)
        dxconv = dxconv + _dot_nt(dpab, wab) + _dot_nt(dpxb, wxb)
        dcb_ref[...] = jnp.sum(dxconv, axis=0, keepdims=True)
        dxc = jnp.zeros_like(dxconv)
        for j in range(CONV_WIDTH):
            dcw_ref[j:j + 1, :] = jnp.sum(dxconv * f['shifted'][j], axis=0, keepdims=True)
            back = _shift_up(dxconv, CONV_WIDTH - 1 - j) if j < CONV_WIDTH - 1 else dxconv
            dxc = dxc + cw[j:j + 1, :] * back
        dxc_ref[...] = dxc.astype(dxc_ref.dtype)

    w = LANES
    vec = pl.BlockSpec((1, w), lambda i: (0, i))
    mat = pl.BlockSpec((None, w, w), lambda i: (i, 0, 0))
    vshape = jax.ShapeDtypeStruct((1, C_WIDTH), F32)
    mshape = jax.ShapeDtypeStruct((C_HEADS, w, w), F32)
    return pl.pallas_call(
        body, name="lru_bwd", grid=(C_HEADS,), in_specs=_lru_specs(s) + [_col(w, 0)(s)],
        out_specs=[pl.BlockSpec((2, s, w), lambda i: (0, 0, i)), pl.BlockSpec((CONV_WIDTH, w), lambda i: (0, i)),
                   vec, mat, vec, mat, vec, vec],
        out_shape=[jax.ShapeDtypeStruct((N_CHIPS, s, C_WIDTH), MXU_DTYPE),
                   jax.ShapeDtypeStruct((CONV_WIDTH, C_WIDTH), F32), vshape, mshape, vshape, mshape, vshape, vshape],
        compiler_params=_params(("parallel",)),
    )(h, h, cw, cb.reshape(1, -1), wa, ba.reshape(1, -1), wx, bx.reshape(1, -1), lam.reshape(1, -1), dmix)


def _pool_window(g):
    return jnp.left_shift(2, g).astype(F32)


def _select_window(g, sums):
    out = sums[-1]
    for k in range(len(sums) - 2, -1, -1):
        out = jnp.where(g == k, sums[k], out)
    return out


def _window_sums(x, shift):
    sums, cur = [], x
    for k in range(D_GROUPS):
        cur = cur + shift(cur, 2 ** k)
        sums.append(cur)
    return sums


def _pool_specs(s):
    w = D_GROUP_DIM
    return [_part(w, 2)(s), _part(w, 3)(s),
            pl.BlockSpec((None, w, w), lambda i: (i, 0, 0)), pl.BlockSpec((1, w), lambda i: (0, i))]


def _pool_forward(xd, g):
    rows = lax.broadcasted_iota(jnp.int32, xd.shape, 0).astype(F32) + 1.0
    cnt = jnp.minimum(rows, _pool_window(g))
    pooled = _select_window(g, _window_sums(xd, _shift_down)) / cnt - xd
    return pooled, cnt


def _pool_fwd(h, mix, w_pool, d_scale):
    s = h.shape[1]

    def body(xd_ref, dg_ref, wp_ref, sc_ref, mix_ref, o_ref):
        pooled, _ = _pool_forward(xd_ref[...], pl.program_id(0))
        mixed = _dot(pooled.astype(MXU_DTYPE), wp_ref[...].astype(MXU_DTYPE))
        sil, _ = _silu_and_grad(dg_ref[...])
        o_ref[...] = (mixed * sc_ref[...] * sil).astype(o_ref.dtype)

    return pl.pallas_call(
        body, name="pool_fwd", grid=(D_GROUPS,), in_specs=_pool_specs(s) + [ANY],
        out_specs=_col(D_GROUP_DIM, C_WIDTH // D_GROUP_DIM)(s), out_shape=jax.ShapeDtypeStruct(mix.shape, mix.dtype),
        input_output_aliases={4: 0}, compiler_params=_params(("parallel",)),
    )(h, h, w_pool, d_scale.reshape(1, -1), mix)


def _pool_bwd(h, dmix, dh, w_pool, d_scale):
    s = h.shape[1]
    w = D_GROUP_DIM

    def body(xd_ref, dg_ref, wp_ref, sc_ref, ddo_ref, dh_in_ref, dh_ref, dwp_ref, dsc_ref):
        dxd_ref, ddg_ref = dh_ref.at[0], dh_ref.at[1]
        g = pl.program_id(0)
        pooled, cnt = _pool_forward(xd_ref[...], g)
        pb = pooled.astype(MXU_DTYPE)
        wpb = wp_ref[...].astype(MXU_DTYPE)
        mixed = _dot(pb, wpb)
        sil, dsil = _silu_and_grad(dg_ref[...])
        ddo = ddo_ref[...]
        sc = sc_ref[...]
        ddg_ref[...] = (ddo * mixed * sc * dsil).astype(ddg_ref.dtype)
        dms = ddo * sil
        dsc_ref[...] = jnp.sum(dms * mixed, axis=0, keepdims=True)
        dmb = (dms * sc).astype(MXU_DTYPE)
        dwp_ref[...] = _dot_tn(pb, dmb)
        dpooled = _dot_nt(dmb, wpb)
        dxd = _select_window(g, _window_sums(dpooled / cnt, _shift_up)) - dpooled
        dxd_ref[...] = dxd.astype(dxd_ref.dtype)

    return pl.pallas_call(
        body, name="pool_bwd", grid=(D_GROUPS,), in_specs=_pool_specs(s) + [_col(w, C_WIDTH // w)(s), ANY],
        out_specs=[pl.BlockSpec((2, s, w), lambda i: (1, 0, i)), pl.BlockSpec((None, w, w), lambda i: (i, 0, 0)),
                   pl.BlockSpec((1, w), lambda i: (0, i))],
        out_shape=[jax.ShapeDtypeStruct(dh.shape, dh.dtype),
                   jax.ShapeDtypeStruct((D_GROUPS, w, w), F32), jax.ShapeDtypeStruct((1, D_WIDTH), F32)],
        input_output_aliases={5: 0}, compiler_params=_params(("parallel",)),
    )(h, h, w_pool, d_scale.reshape(1, -1), dmix, dh)


def _rope_tables(positions):
    s = positions.shape[0]
    inv_freq = ROPE_THETA ** (-jnp.arange(0, ROT_DIM, 2, dtype=F32) / ROT_DIM)
    ang = positions.astype(F32)[:, None] * inv_freq
    c, sn = jnp.cos(ang), jnp.sin(ang)
    rest = B_HEAD_DIM - ROT_DIM
    c64 = jnp.concatenate([c, c, jnp.ones((s, rest), F32)], axis=1)
    s64 = jnp.concatenate([-sn, sn, jnp.zeros((s, rest), F32)], axis=1)
    return jnp.tile(c64, (1, LANES // B_HEAD_DIM)), jnp.tile(s64, (1, LANES // B_HEAD_DIM))


def _local_step(x, positions, target, w, projections_of, projections_ahead, projection_grads, norm_grads_done):
    cq, sq = _rope_tables(positions)
    saved = []
    xin, xin_b = x, x.astype(MXU_DTYPE)
    for layer in range(DEPTH):
        j = layer // 2
        kind = 'even' if layer % 2 == 0 else 'odd'
        w_in, w_out = projections_of(layer, xin_b)
        h = _mm_in(xin_b, w_in, W_IN_TRANSPOSED[kind], "mm_" + kind + "_in")
        if kind == 'even':
            behind, w_out_after = w_out(h) if callable(w_out) else (h, None)
            mix = _even_fwd(h, cq, sq, w['even_a_ln_g'][j], w['even_a_ln_b'][j], w['even_a_ws'][j],
                            w['even_a_bs'][j].T, w['even_b_sinks'][j], behind)
            w_out = w_out_after(mix) if w_out_after else w_out
        else:
            mix = _lru_fwd(h, w['odd_conv_w'][j], w['odd_conv_b'][j], w['odd_w_a'][j], w['odd_b_a'][j],
                           w['odd_w_x'][j], w['odd_b_x'][j], w['odd_lam'][j])
            mix = _pool_fwd(h, mix, w['odd_w_pool'][j], w['odd_d_scale'][j])
        if layer == DEPTH - 1:
            z, out, out_b = _mm_out_norm(mix, w_out, xin, w[kind + '_ln_g'][j], w[kind + '_ln_b'][j],
                                         "mm_out_norm_loss", target=target)
        else:
            z, out, out_b = _mm_out_norm(mix, w_out, xin, w[kind + '_ln_g'][j], w[kind + '_ln_b'][j], "mm_out_norm",
                                         after=projections_ahead(layer + 1, mix))
        saved.append((xin_b, h, mix, z, w_in, w_out))
        xin, xin_b = out, out_b

    dout, loss = xin, xin_b
    per_layer = {n: [None, None] for n in SMALL_NAMES}
    behind = loss
    for layer in reversed(range(DEPTH)):
        j = layer // 2
        kind = 'even' if layer % 2 == 0 else 'odd'
        xin_b, h, mix, z, w_in, w_out = saved[layer]
        dz, dz_b, dg, db = _ln_bwd(dout, z, w[kind + '_ln_g'][j], behind)
        per_layer[kind + '_ln_g'][j], per_layer[kind + '_ln_b'][j] = dg[0], db[0]
        dmix = _mm_dmix(dz_b, w_out, norm_grads_done(layer, dz_b), "mm_dmix")
        dw_out = _mm_dw_out(mix, dz_b, "mm_dw_out")
        if kind == 'even':
            dh, dlg, dlb, dws, dbst, dsink = _even_bwd(
                h, dmix, cq, sq, w['even_a_ln_g'][j], w['even_a_ln_b'][j], w['even_a_ws'][j], w['even_a_bs'][j].T,
                w['even_b_sinks'][j])
            for n, val in (('even_a_ln_g', dlg[0]), ('even_a_ln_b', dlb[0]), ('even_a_ws', dws),
                           ('even_a_bs', dbst.T), ('even_b_sinks', dsink[0])):
                per_layer[n][j] = val
        else:
            dh, dcw, dcb, dwa, dba, dwx, dbx, dlam = _lru_bwd(
                h, dmix, w['odd_conv_w'][j], w['odd_conv_b'][j], w['odd_w_a'][j], w['odd_b_a'][j], w['odd_w_x'][j],
                w['odd_b_x'][j], w['odd_lam'][j])
            dh, dwp, dsc = _pool_bwd(h, dmix, dh, w['odd_w_pool'][j], w['odd_d_scale'][j])
            for n, val in (('odd_conv_w', dcw), ('odd_conv_b', dcb[0]), ('odd_w_a', dwa), ('odd_b_a', dba[0]),
                           ('odd_w_x', dwx), ('odd_b_x', dbx[0]), ('odd_lam', dlam[0]), ('odd_w_pool', dwp),
                           ('odd_d_scale', dsc[0])):
                per_layer[n][j] = val
        dout = _mm_dx(dh, w_in, dz, W_IN_TRANSPOSED[kind], "mm_dx_" + kind)
        dw_in = _mm_dw_in(xin_b, dh, W_IN_TRANSPOSED[kind], "mm_dw_in_" + kind)
        behind = projection_grads(layer, dw_in, dw_out)
    return loss, dout, {n: jnp.stack(v) for n, v in per_layer.items()}


ANY = pl.BlockSpec(memory_space=pl.ANY)


def _where_am_i():
    return lax.axis_index("x"), lax.axis_index("y"), lax.axis_index("c")


def _other_chips(x, y):
    return [(1 - x, y), (x, 1 - y), (1 - x, 1 - y)]


PACKED_ROWS = 16


def _half_rows(rows, which):
    half = rows // 2
    assert half % PACKED_ROWS == 0, rows
    return pl.ds(pl.multiple_of(which * half, PACKED_ROWS), half)


def _sibling_swap(bufs, name, *, other_half_of_axis1=False, after=()):
    n, na = len(bufs), len(after)
    out_shapes = [(b.shape[0], b.shape[1] // 2, b.shape[2]) if other_half_of_axis1 else b.shape for b in bufs]

    def body(*refs):
        x_refs, out_refs, (send_sems, recv_sems) = refs[:n], refs[n + na:2 * n + na], refs[2 * n + na:]
        x, y, c = _where_am_i()
        copies = []
        for p in range(n):
            src = x_refs[p].at[:, _half_rows(bufs[p].shape[1], 1 - c), :] if other_half_of_axis1 else x_refs[p]
            copies.append(pltpu.make_async_remote_copy(
                src_ref=src, dst_ref=out_refs[p], send_sem=send_sems.at[p], recv_sem=recv_sems.at[p],
                device_id=(x, y, 1 - c), device_id_type=MESH))
            copies[-1].start()
        for cp in copies:
            cp.wait()

    return pl.pallas_call(
        body, name=name, in_specs=[ANY] * (n + na), out_specs=[ANY] * n,
        out_shape=[jax.ShapeDtypeStruct(s, b.dtype) for s, b in zip(out_shapes, bufs)],
        scratch_shapes=[pltpu.SemaphoreType.DMA((n,)), pltpu.SemaphoreType.DMA((n,))],
    )(*bufs, *after)


HBM = pl.BlockSpec(memory_space=pltpu.HBM)
SEM = pl.BlockSpec(memory_space=pltpu.SEMAPHORE)
IN_FLIGHT = pltpu.CompilerParams(has_side_effects=pltpu.SideEffectType.DATAFLOW_SIDE_EFFECTING)


IN_PLACE = ('forward',)


def _peers(kind, x, y, c):
    if kind in ('swap', 'join'):
        return [(x, y, 1 - c)]
    return [(x, y, 1 - c) if kind == 'forward' else (px, py, c) for px, py in _other_chips(x, y)]


def _n_peers(kind):
    return 1 if kind in ('swap', 'join') else N_CHIPS - 1


def _land_shape(kind, a):
    if kind == 'gather':
        return (N_CHIPS,) + a.shape
    return (a.shape[0], a.shape[1] // 2, a.shape[2]) if kind == 'swap' else a.shape


def _ici_copy(kind, src_ref, land_ref, send_sems, recv_sems, k, peer, x, y, c, arriving=False):
    me = 2 * x + y
    if kind == 'gather':
        half = _half_rows(src_ref.shape[0], c)
        src, dst = src_ref.at[half, :], land_ref.at[me, half, :]
    elif kind == 'scatter':
        src, dst = src_ref.at[2 * peer[0] + peer[1]], land_ref.at[me]
    elif kind == 'swap':
        src, dst = src_ref.at[:, _half_rows(src_ref.shape[1], 1 - c), :], land_ref
    elif kind == 'join':
        src, dst = src_ref, land_ref
    else:
        px, py = _other_chips(x, y)[k % (N_CHIPS - 1)]
        rows = src_ref.at[2 * px + py, _half_rows(src_ref.shape[1], 1 - c if arriving else c), :]
        src, dst = rows, rows
    return pltpu.make_async_remote_copy(src_ref=src, dst_ref=dst, send_sem=send_sems.at[k], recv_sem=recv_sems.at[k],
                                        device_id=peer, device_id_type=MESH)


def _ici_start(kind, groups, name, after=()):
    flat = [a for g in groups for a in g]
    n, ng, np_, na = len(flat), len(groups), _n_peers(kind), len(after)
    nl = 0 if kind in IN_PLACE else n
    land_shapes = [_land_shape(kind, a) for a in flat[:nl]]

    def body(*refs):
        srcs, sems, token = refs[:n], refs[n + nl + na:n + nl + na + 2 * ng], refs[-1]
        lands = refs[n:n + nl] if nl else srcs
        x, y, c = _where_am_i()
        p = 0
        for gi, g in enumerate(groups):
            for q in range(len(g)):
                for k, peer in enumerate(_peers(kind, x, y, c)):
                    _ici_copy(kind, srcs[p], lands[p], sems[2 * gi], sems[2 * gi + 1], np_ * q + k, peer, x, y,
                              c).start()
                p += 1
        token[...] = jnp.zeros_like(token)

    sem_shapes = [pltpu.SemaphoreType.DMA((np_ * len(g),)) for g in groups for _ in range(2)]
    outs = pl.pallas_call(
        body, name=name, in_specs=[HBM] * (n + nl) + [ANY] * na,
        out_specs=[SEM] * (2 * ng) + [HBM] * (n + nl) + [pl.BlockSpec(memory_space=pltpu.VMEM)],
        out_shape=sem_shapes + [pltpu.HBM(a.shape, a.dtype) for a in flat]
        + [pltpu.HBM(s, a.dtype) for s, a in zip(land_shapes, flat)] + [jax.ShapeDtypeStruct((8, LANES), F32)],
        input_output_aliases={i: 2 * ng + i for i in range(n + nl)}, compiler_params=IN_FLIGHT,
    )(*[pltpu.with_memory_space_constraint(a, pltpu.HBM) for a in flat],
      *[pltpu.with_memory_space_constraint(lax.empty(s, a.dtype), pltpu.HBM) for s, a in zip(land_shapes, flat)],
      *after)
    sems, srcs, token = outs[:2 * ng], outs[2 * ng:2 * ng + n], outs[-1]
    lands = outs[2 * ng + n:2 * ng + n + nl] if nl else srcs
    started, p = [], 0
    for gi, g in enumerate(groups):
        started.append((sems[2 * gi], sems[2 * gi + 1], srcs[p:p + len(g)], lands[p:p + len(g)]))
        p += len(g)
    return started, token


def _ici_wait(kind, started, after, name):
    send_sems, recv_sems, srcs, lands = started
    n, np_ = len(srcs), _n_peers(kind)
    nl = 0 if kind in IN_PLACE else n
    after = list(after)

    def body(*refs):
        src_refs, send_ref, recv_ref = refs[:n], refs[n + nl], refs[n + nl + 1]
        land_refs = refs[n:n + nl] if nl else src_refs
        x, y, c = _where_am_i()
        for q in range(n):
            for k, peer in enumerate(_peers(kind, x, y, c)):
                args = (kind, src_refs[q], land_refs[q], send_ref, recv_ref, np_ * q + k, peer, x, y, c)
                _ici_copy(*args).wait_send()
                _ici_copy(*args, arriving=True).wait_recv()

    arrays = list(srcs) + (list(lands) if nl else [])
    outs = pl.pallas_call(
        body, name=name, in_specs=[HBM] * (n + nl) + [SEM, SEM] + [ANY] * len(after), out_specs=[HBM] * (n + nl),
        out_shape=[pltpu.HBM(a.shape, a.dtype) for a in arrays],
        input_output_aliases={i: i for i in range(n + nl)}, compiler_params=IN_FLIGHT,
    )(*arrays, send_sems, recv_sems, *after)
    return outs[:n], (outs[n:] if nl else outs[:n])


def _forward_to_sibling(lands, name):
    n = len(lands)

    def body(*refs):
        out_refs, (send_sems, recv_sems) = refs[n:2 * n], refs[2 * n:]
        x, y, c = _where_am_i()

        def copy(q, k, peer, half_of):
            rows = out_refs[q].at[2 * peer[0] + peer[1], _half_rows(lands[q].shape[1], half_of), :]
            return pltpu.make_async_remote_copy(
                src_ref=rows, dst_ref=rows, send_sem=send_sems.at[3 * q + k], recv_sem=recv_sems.at[3 * q + k],
                device_id=(x, y, 1 - c), device_id_type=MESH)

        sends = [copy(q, k, peer, c) for q in range(n) for k, peer in enumerate(_other_chips(x, y))]
        for cp in sends:
            cp.start()
        for q in range(n):
            for k, peer in enumerate(_other_chips(x, y)):
                copy(q, k, peer, 1 - c).wait_recv()
        for cp in sends:
            cp.wait_send()

    return pl.pallas_call(
        body, name=name, in_specs=[ANY] * n, out_specs=[ANY] * n,
        out_shape=[jax.ShapeDtypeStruct(a.shape, a.dtype) for a in lands],
        input_output_aliases={i: i for i in range(n)},
        scratch_shapes=[pltpu.SemaphoreType.DMA((3 * n,)), pltpu.SemaphoreType.DMA((3 * n,))],
    )(*lands)


def _with_own_slot(lands, own, chip):
    out = []
    for land, mine in zip(lands, own):
        mine = lax.dynamic_slice_in_dim(mine, chip, 1, axis=0) if mine.ndim == land.ndim else mine[None]
        out.append(lax.dynamic_update_slice(land, mine, (chip,) + (0,) * (land.ndim - 1)))
    return out


def _row_tile(rows, row_bytes, n_bufs, budget=24 * 1024 * 1024):
    best = None
    for t in range(PACKED_ROWS, rows + 1, PACKED_ROWS):
        if rows % t == 0 and 2 * n_bufs * t * row_bytes <= budget:
            best = t
    assert best is not None, (rows, row_bytes)
    return best


def _add_halves(full, recv, core, out_dtype, name):
    n, h, cols = recv.shape
    tr = _row_tile(h, cols * 4, 3)
    nt = h // tr

    def body(core_ref, a_ref, b_ref, o_ref):
        o_ref[...] = (a_ref[...].astype(F32) + b_ref[...].astype(F32)).astype(out_dtype)

    blk = (None, tr, cols)
    return pl.pallas_call(
        body, name=name,
        grid_spec=pltpu.PrefetchScalarGridSpec(
            num_scalar_prefetch=1, grid=(n, nt),
            in_specs=[pl.BlockSpec(blk, lambda j, i, core_ref: (j, core_ref[0] * nt + i, 0)),
                      pl.BlockSpec(blk, lambda j, i, core_ref: (j, i, 0))],
            out_specs=pl.BlockSpec(blk, lambda j, i, core_ref: (j, i, 0))),
        out_shape=jax.ShapeDtypeStruct((n, h, cols), out_dtype),
        compiler_params=_params(("parallel", "parallel")),
    )(core.reshape(1), full, recv)


def _add2(a, b, name):
    r, cols = a.shape
    tr = _row_tile(r, cols * 4, 3)

    def body(a_ref, b_ref, o_ref):
        o_ref[...] = a_ref[...] + b_ref[...]

    blk = pl.BlockSpec((tr, cols), lambda i: (i, 0))
    return pl.pallas_call(body, name=name, grid=(r // tr,), in_specs=[blk, blk], out_specs=blk,
                          out_shape=jax.ShapeDtypeStruct(a.shape, a.dtype), compiler_params=_params(("parallel",)))(a, b)


def _sum_slots(parts, own, chip, name):
    n, h, cols = parts.shape
    tr = _row_tile(h, cols * 4, n + 2)

    def body(chip_ref, p_ref, own_ref, o_ref):
        acc = None
        for k in range(n):
            term = jnp.where(chip_ref[0] == k, own_ref[...], p_ref[k]).astype(F32)
            acc = term if acc is None else acc + term
        o_ref[...] = acc

    own_spec = (pl.BlockSpec((None, tr, cols), lambda i, chip_ref: (chip_ref[0], i, 0)) if own.ndim == 3
                else pl.BlockSpec((tr, cols), lambda i, chip_ref: (i, 0)))
    return pl.pallas_call(
        body, name=name,
        grid_spec=pltpu.PrefetchScalarGridSpec(
            num_scalar_prefetch=1, grid=(h // tr,),
            in_specs=[pl.BlockSpec((n, tr, cols), lambda i, chip_ref: (0, i, 0)), own_spec],
            out_specs=pl.BlockSpec((tr, cols), lambda i, chip_ref: (i, 0))),
        out_shape=jax.ShapeDtypeStruct((h, cols), F32), compiler_params=_params(("parallel",)),
    )(chip.reshape(1), parts, own)


def _adamw(w, g, m, v, name):
    shape = w.shape
    cols = shape[-1]
    rows = math.prod(shape[:-1])
    tr = rows
    if rows % 8 == 0:
        tr = 8
        for t in range(8, rows + 1, 8):
            if rows % t == 0 and t * cols * 4 <= 1536 * 1024:
                tr = t
    c1 = 1.0 - ADAM_B1 ** ADAM_STEP
    c2 = 1.0 - ADAM_B2 ** ADAM_STEP

    def body(w_ref, g_ref, m_ref, v_ref, d_ref, nm_ref, nv_ref):
        gg = g_ref[...]
        nm = ADAM_B1 * m_ref[...] + (1.0 - ADAM_B1) * gg
        nv = ADAM_B2 * v_ref[...] + (1.0 - ADAM_B2) * (gg * gg)
        d_ref[...] = -ADAM_LR * ((nm / c1) / (jnp.sqrt(nv / c2) + ADAM_EPS) + ADAM_WD * w_ref[...])
        nm_ref[...] = nm
        nv_ref[...] = nv

    blk = pl.BlockSpec((tr, cols), lambda i: (i, 0))
    flat = jax.ShapeDtypeStruct((rows, cols), F32)
    outs = pl.pallas_call(
        body, name=name, grid=(rows // tr,), in_specs=[blk] * 4, out_specs=[blk] * 3, out_shape=[flat] * 3,
        compiler_params=_params(("parallel",)),
    )(*[a.reshape(rows, cols) for a in (w, g, m, v)])
    return [o.reshape(shape) for o in outs]


def _adamw_update(w, g, m, v):
    c1 = 1.0 - ADAM_B1 ** ADAM_STEP
    c2 = 1.0 - ADAM_B2 ** ADAM_STEP
    nm = ADAM_B1 * m + (1.0 - ADAM_B1) * g
    nv = ADAM_B2 * v + (1.0 - ADAM_B2) * (g * g)
    return -ADAM_LR * ((nm / c1) / (jnp.sqrt(nv / c2) + ADAM_EPS) + ADAM_WD * w), nm, nv


def _adamw_projection(w, m, v, mine, theirs, core, name):
    _, r, cols = w.shape
    h = r // 2
    tr = _row_tile(h, cols * 4, 12, budget=40 * 1024 * 1024)
    nt = h // tr

    def body(core_ref, w_ref, m_ref, v_ref, a0, b0, a1, b1, g_ref, d_ref, nm_ref, nv_ref):
        layer, half = pl.program_id(0), pl.program_id(1)
        own = half == core_ref[0]
        g = jnp.where(layer == 0, jnp.where(own, a0[...], b0[...]), jnp.where(own, a1[...], b1[...]))
        g_ref[...] = g
        d_ref[...], nm_ref[...], nv_ref[...] = _adamw_update(w_ref[...], g, m_ref[...], v_ref[...])

    def piece(layer, own):
        def index(l, hh, i, core_ref):
            used = (l == layer) & ((hh == core_ref[0]) == own)
            return (jnp.where(used, i, 0), 0)
        return pl.BlockSpec((tr, cols), index)

    whole = pl.BlockSpec((None, tr, cols), lambda l, hh, i, core_ref: (l, hh * nt + i, 0))
    return pl.pallas_call(
        body, name=name,
        grid_spec=pltpu.PrefetchScalarGridSpec(
            num_scalar_prefetch=1, grid=(2, 2, nt),
            in_specs=[whole] * 3 + [piece(0, True), piece(0, False), piece(1, True), piece(1, False)],
            out_specs=[whole] * 4),
        out_shape=[jax.ShapeDtypeStruct(w.shape, F32)] * 4,
        compiler_params=_params(("arbitrary", "arbitrary", "arbitrary")),
    )(core.reshape(1), w, m, v, mine[0], theirs[0], mine[1], theirs[1])


def _chip_shape(name):
    shape = list(FULL_SHAPES[name])
    axis = BIG_SHARD_AXIS.get(name, SMALL_SHARD_AXIS.get(name))
    if axis is not None:
        shape[axis] //= N_CHIPS
    return tuple(shape)


def _round_up(n, k):
    return -(-n // k) * k


SHARDED_SMALL_WORDS = sum(math.prod(_chip_shape(n)) for n in SMALL_SHARDED)
SHARDED_SMALL_ROWS = _round_up(-(-SHARDED_SMALL_WORDS // PACK_COLS), 32)
SMALL_MATRICES = ['even_a_ws', 'odd_w_a', 'odd_w_x', 'odd_w_pool']
SMALL_VECTORS = [n for n in SMALL_NAMES if n not in SMALL_MATRICES]
VECTOR_ROWS = _round_up(-(-(sum(math.prod(FULL_SHAPES[n]) for n in SMALL_VECTORS) + 1) // PACK_COLS), 32)


def _pack_rows(flat, rows):
    return jnp.pad(flat, (0, rows * PACK_COLS - flat.shape[0])).reshape(rows, PACK_COLS)


def _pack_small_sharded(shards):
    return _pack_rows(jnp.concatenate([shards[n].reshape(-1) for n in SMALL_SHARDED]), SHARDED_SMALL_ROWS)


def _unpack_small_sharded(gathered_small):
    out = {}
    words = gathered_small.reshape(N_CHIPS, -1)
    w0 = 0
    for n in SMALL_SHARDED:
        size = math.prod(_chip_shape(n))
        blocks = words[:, w0:w0 + size].reshape((N_CHIPS,) + _chip_shape(n))
        out[n] = jnp.concatenate([blocks[j] for j in range(N_CHIPS)], axis=SMALL_SHARD_AXIS[n])
        w0 += size
    return out


def _small_grad_parts(grads, loss):
    words = [grads[n].reshape(-1) for n in SMALL_VECTORS] + [loss.reshape(-1)]
    return ([_pack_rows(jnp.concatenate(words), VECTOR_ROWS)]
            + [grads[n].reshape(-1, FULL_SHAPES[n][-1]) for n in SMALL_MATRICES])


def _own_block(name, full, chip):
    if name not in SMALL_SHARD_AXIS:
        return full
    axis = SMALL_SHARD_AXIS[name]
    width = FULL_SHAPES[name][axis] // N_CHIPS
    return lax.dynamic_slice_in_dim(full, chip * width, width, axis)


def _from_small_grad_parts(totals, chip):
    flat = totals[0].reshape(-1)
    out = {}
    w0 = 0
    for n in SMALL_VECTORS:
        size = math.prod(FULL_SHAPES[n])
        out[n] = _own_block(n, flat[w0:w0 + size].reshape(FULL_SHAPES[n]), chip)
        w0 += size
    for n, total in zip(SMALL_MATRICES, totals[1:]):
        out[n] = _own_block(n, total.reshape(FULL_SHAPES[n]), chip)
    return out, flat[w0]


def kernel(x, positions, even_w_in, even_a_ln_g, even_a_ln_b, even_a_ws, even_a_bs, even_b_sinks, even_w_out, even_ln_g, even_ln_b, odd_w_in, odd_conv_w, odd_conv_b, odd_w_a, odd_b_a, odd_w_x, odd_b_x, odd_lam, odd_w_pool, odd_d_scale, odd_w_out, odd_ln_g, odd_ln_b, loss_target, m_even_w_in, m_even_a_ln_g, m_even_a_ln_b, m_even_a_ws, m_even_a_bs, m_even_b_sinks, m_even_w_out, m_even_ln_g, m_even_ln_b, m_odd_w_in, m_odd_conv_w, m_odd_conv_b, m_odd_w_a, m_odd_b_a, m_odd_w_x, m_odd_b_x, m_odd_lam, m_odd_w_pool, m_odd_d_scale, m_odd_w_out, m_odd_ln_g, m_odd_ln_b, v_even_w_in, v_even_a_ln_g, v_even_a_ln_b, v_even_a_ws, v_even_a_bs, v_even_b_sinks, v_even_w_out, v_even_ln_g, v_even_ln_b, v_odd_w_in, v_odd_conv_w, v_odd_conv_b, v_odd_w_a, v_odd_b_a, v_odd_w_x, v_odd_b_x, v_odd_lam, v_odd_w_pool, v_odd_d_scale, v_odd_w_out, v_odd_ln_g, v_odd_ln_b):
    local = dict(zip(WEIGHT_NAMES, (even_w_in, even_a_ln_g, even_a_ln_b, even_a_ws, even_a_bs, even_b_sinks, even_w_out, even_ln_g, even_ln_b, odd_w_in, odd_conv_w, odd_conv_b, odd_w_a, odd_b_a, odd_w_x, odd_b_x, odd_lam, odd_w_pool, odd_d_scale, odd_w_out, odd_ln_g, odd_ln_b)))
    mom = dict(zip(WEIGHT_NAMES, (m_even_w_in, m_even_a_ln_g, m_even_a_ln_b, m_even_a_ws, m_even_a_bs, m_even_b_sinks, m_even_w_out, m_even_ln_g, m_even_ln_b, m_odd_w_in, m_odd_conv_w, m_odd_conv_b, m_odd_w_a, m_odd_b_a, m_odd_w_x, m_odd_b_x, m_odd_lam, m_odd_w_pool, m_odd_d_scale, m_odd_w_out, m_odd_ln_g, m_odd_ln_b)))
    vel = dict(zip(WEIGHT_NAMES, (v_even_w_in, v_even_a_ln_g, v_even_a_ln_b, v_even_a_ws, v_even_a_bs, v_even_b_sinks, v_even_w_out, v_even_ln_g, v_even_ln_b, v_odd_w_in, v_odd_conv_w, v_odd_conv_b, v_odd_w_a, v_odd_b_a, v_odd_w_x, v_odd_b_x, v_odd_lam, v_odd_w_pool, v_odd_d_scale, v_odd_w_out, v_odd_ln_g, v_odd_ln_b)))
    mx, my, core = _where_am_i()
    chip = 2 * mx + my

    stored = lambda n, a: jnp.swapaxes(a, 1, 2) if n == 'even_w_in' and W_IN_TRANSPOSED['even'] else a
    names_of = lambda layer: [('even' if layer % 2 == 0 else 'odd') + s for s in ('_w_in', '_w_out')]

    shards = [[stored(n, local[n])[layer // 2].astype(MXU_DTYPE) for n in names_of(layer)] for layer in range(DEPTH)]
    (first_in, first_out), first_token = _ici_start(
        'gather', [[_pack_small_sharded(local), shards[0][0]], [shards[0][1]]], "gather_start_first")
    later, gather_token = _ici_start('gather', shards[1:], "gather_start", after=[first_token])
    gathers = [None] + later

    own, lands = _ici_wait('gather', first_in, [first_token, gather_token], "gather_wait_0")
    small_sharded, first_w_in = _with_own_slot(_forward_to_sibling(lands, "gather_forward_0"), own, chip)
    weights = dict(local)
    weights.update(_unpack_small_sharded(small_sharded))
    passing = {}

    def passed_on(started, after, name):
        own, lands = _ici_wait('gather', started, [after], "gather_wait_" + name)
        (forward,), token = _ici_start('forward', [lands], "forward_start_" + name)

        def finish(then):
            done, _ = _ici_wait('forward', forward, [then], "forward_wait_" + name)
            return _with_own_slot(done, own, chip)

        return token, finish

    def first_w_out(h):
        token, finish = passed_on(first_out, h, "0_out")
        return token, lambda mix: finish(mix)[0]

    def projections_ahead(layer, after):
        token, passing[layer] = passed_on(gathers[layer], after, str(layer))
        return token

    def projections_of(layer, after):
        return (first_w_in, first_w_out) if layer == 0 else passing.pop(layer)(after)

    swapping, in_flight, mine, started = [], [], {}, {}

    def land(*after):
        layer, scatter = in_flight.pop()
        parts, lands = _ici_wait('scatter', scatter, after, "scatter_wait_%d" % layer)
        for n, arrived, part in zip(names_of(layer), lands, parts):
            mine[n, layer // 2] = _sum_slots(arrived, part, chip, "big_sum_%s_%d" % (n, layer // 2))

    odd_pieces = [(n, l) for n in ('odd_w_in', 'odd_w_out') for l in range(2)]
    joining = []

    def projection_grads(layer, dw_in, dw_out):
        behind = []
        if in_flight:
            land(dw_in)
        if layer == 0:
            (join,), token = _ici_start('join', [[mine[p] for p in odd_pieces]], "join_odd_start")
            joining.append(join)
            behind = [token]
        (swap,), token = _ici_start('swap', [[dw_in, dw_out]], "presum_start_%d" % layer, after=behind)
        swapping.append((layer, swap))
        return presum_and_scatter(token) if layer == 0 else token

    def presum_and_scatter(*after):
        layer, swap = swapping.pop()
        partial, from_sibling = _ici_wait('swap', swap, after, "presum_wait_%d" % layer)
        chip_part = [_add_halves(g, r, core, MXU_DTYPE, "big_presum_add_%s_%d" % (n, layer // 2))
                     for g, r, n in zip(partial, from_sibling, names_of(layer))]
        (scatter,), started[layer] = _ici_start('scatter', [chip_part], "scatter_start_%d" % layer)
        in_flight.append((layer, scatter))
        return started[layer]

    norm_grads_done = lambda layer, dz: presum_and_scatter(dz) if swapping else dz
    loss, grad_x, grads = _local_step(x[0], positions[0], loss_target[0], weights, projections_of, projections_ahead,
                                      projection_grads, norm_grads_done)

    small = _small_grad_parts(grads, loss)
    chip_sums = [_add2(a, b, "small_presum_add_%d" % i)
                 for i, (a, b) in enumerate(zip(small, _sibling_swap(small, "small_swap", after=[started[0]])))]
    (small_gather,), small_token = _ici_start('gather', [chip_sums], "small_gather_start")

    reduced, deltas, new_m, new_v = {}, {}, {}, {}

    def adamw_projections(kind, after):
        names = [kind + '_w_in', kind + '_w_out']
        pieces = [(n, l) for n in names for l in range(2)]
        if kind == 'odd':
            halves, theirs = _ici_wait('join', joining.pop(), after, "join_odd_wait")
            mine.update(zip(pieces, halves))
        else:
            theirs = _sibling_swap([mine[p] for p in pieces], "big_join_" + kind, after=after)
        theirs = dict(zip(pieces, theirs))
        done = []
        for n in names:
            outs = _adamw_projection(stored(n, local[n]), stored(n, mom[n]), stored(n, vel[n]),
                                     [mine[n, l] for l in range(2)], [theirs[n, l] for l in range(2)], core,
                                     "adamw_" + n)
            reduced[n], deltas[n], new_m[n], new_v[n] = [stored(n, o) for o in outs]
            done.append(outs[-1])
        return done

    done = adamw_projections('odd', [small_token])
    land(grad_x, *done)
    done = adamw_projections('even', [])

    own, lands = _ici_wait('gather', small_gather, done, "small_gather_wait")
    of_chips = _forward_to_sibling(lands, "small_gather_forward")
    small_totals, total_loss = _from_small_grad_parts(
        [_sum_slots(g, mine_i, chip, "small_sum_%d" % i) for i, (g, mine_i) in enumerate(zip(of_chips, own))], chip)
    reduced.update(small_totals)
    for n in SMALL_NAMES:
        deltas[n], new_m[n], new_v[n] = _adamw(local[n], reduced[n], mom[n], vel[n], "adamw_" + n)

    return (total_loss, grad_x[None], *[reduced[n] for n in WEIGHT_NAMES], *[deltas[n] for n in WEIGHT_NAMES],
            *[new_m[n] for n in WEIGHT_NAMES], *[new_v[n] for n in WEIGHT_NAMES])
```

```python
import math

import jax
import jax.numpy as jnp
from jax import lax
from jax.experimental import pallas as pl
from jax.experimental.pallas import tpu as pltpu

F32 = jnp.float32
MXU_DTYPE = jnp.bfloat16

DEPTH = 4
A_WIDTH = 1024
A_GROUPS = 8
CHUNK = 128
B_HEAD_DIM = 64
B_Q_HEADS = 16
B_KV_HEADS = 2
WINDOW = 128
ROT_DIM = 16
ROPE_THETA = 500000.0
C_WIDTH = 1024
C_HEADS = 8
CONV_WIDTH = 4
LRU_C = 8.0
D_WIDTH = 1024
D_GROUPS = 4
D_GROUP_DIM = 256
EVEN_IN = 5376
DN_ALPHA = (2 * DEPTH) ** 0.25
LN_EPS = 1e-5
ATTN_SCALE = B_HEAD_DIM ** -0.5
NEG_BIG = -1e30

ADAM_LR = 0.001
ADAM_B1 = 0.9
ADAM_B2 = 0.999
ADAM_EPS = 1e-08
ADAM_WD = 0.01
ADAM_STEP = 10

LANES = 128
VMEM_LIMIT = 56 * 1024 * 1024
N_CHIPS = 4
MESH = pl.DeviceIdType.MESH

WEIGHT_NAMES = ['even_w_in', 'even_a_ln_g', 'even_a_ln_b', 'even_a_ws', 'even_a_bs', 'even_b_sinks', 'even_w_out',
                'even_ln_g', 'even_ln_b', 'odd_w_in', 'odd_conv_w', 'odd_conv_b', 'odd_w_a', 'odd_b_a', 'odd_w_x',
                'odd_b_x', 'odd_lam', 'odd_w_pool', 'odd_d_scale', 'odd_w_out', 'odd_ln_g', 'odd_ln_b']
BIG_SHARD_AXIS = {'even_w_in': 2, 'even_w_out': 1, 'odd_w_in': 2, 'odd_w_out': 1}
SMALL_SHARD_AXIS = {'odd_conv_w': 2, 'odd_conv_b': 1, 'odd_b_a': 1, 'odd_b_x': 1, 'odd_lam': 1, 'odd_w_pool': 2,
                    'odd_d_scale': 1, 'odd_ln_g': 1, 'odd_ln_b': 1}
FULL_SHAPES = {
    'even_w_in': (2, 2048, 5376), 'even_a_ln_g': (2, 1024), 'even_a_ln_b': (2, 1024), 'even_a_ws': (2, 8, 128, 128),
    'even_a_bs': (2, 8, 128), 'even_b_sinks': (2, 16), 'even_w_out': (2, 2048, 2048), 'even_ln_g': (2, 2048),
    'even_ln_b': (2, 2048), 'odd_w_in': (2, 2048, 4096), 'odd_conv_w': (2, 4, 1024), 'odd_conv_b': (2, 1024),
    'odd_w_a': (2, 8, 128, 128), 'odd_b_a': (2, 1024), 'odd_w_x': (2, 8, 128, 128), 'odd_b_x': (2, 1024),
    'odd_lam': (2, 1024), 'odd_w_pool': (2, 4, 256, 256), 'odd_d_scale': (2, 1024), 'odd_w_out': (2, 2048, 2048),
    'odd_ln_g': (2, 2048), 'odd_ln_b': (2, 2048)}
BIG_NAMES = ['even_w_in', 'even_w_out', 'odd_w_in', 'odd_w_out']
SMALL_SHARDED = ['odd_conv_w', 'odd_conv_b', 'odd_b_a', 'odd_b_x', 'odd_lam', 'odd_d_scale', 'odd_ln_g', 'odd_ln_b',
                 'odd_w_pool']
SMALL_NAMES = [n for n in WEIGHT_NAMES if n not in BIG_NAMES]
W_IN_TRANSPOSED = {'even': True, 'odd': False}
PACK_COLS = 1024


def _params(sem):
    return pltpu.CompilerParams(dimension_semantics=sem, vmem_limit_bytes=VMEM_LIMIT)


def _dot(a, b):
    return lax.dot_general(a, b, (((1,), (0,)), ((), ())), preferred_element_type=F32)


def _dot_nt(a, b):
    return lax.dot_general(a, b, (((1,), (1,)), ((), ())), preferred_element_type=F32)


def _dot_tn(a, b):
    return lax.dot_general(a, b, (((0,), (0,)), ((), ())), preferred_element_type=F32)


def _sigmoid(x):
    return 0.5 * jnp.tanh(0.5 * x) + 0.5


def _silu_and_grad(x):
    s = _sigmoid(x)
    return x * s, s * (1.0 + x * (1.0 - s))


def _mm_call(body, name, grid, in_specs, out_spec, out_shape, args):
    return pl.pallas_call(body, name=name, grid=grid, in_specs=in_specs, out_specs=out_spec, out_shape=out_shape,
                          compiler_params=_params(("parallel",) * len(grid)))(*args)


def _resident(block, index_map):
    return pl.BlockSpec(block, index_map, pipeline_mode=pl.Buffered(1))


ROWS_OF_WT = 768


def _mm_in(x, wg, transposed, name):
    s, k = x.shape
    x_spec = _resident((s, k), lambda j: (0, 0))

    def body(x_ref, w_ref, o_ref):
        o_ref[...] = (_dot_nt if transposed else _dot)(x_ref[...], w_ref[...])

    if transposed:
        wt = wg.reshape(-1, k)
        n, tn = wt.shape[0], ROWS_OF_WT
        return _mm_call(body, name, (n // tn,), [x_spec, pl.BlockSpec((tn, k), lambda j: (j, 0))],
                        pl.BlockSpec((s, tn), lambda j: (0, j)), jax.ShapeDtypeStruct((s, n), F32), (x, wt))
    nc = wg.shape[2]
    return _mm_call(body, name, (N_CHIPS,), [x_spec, pl.BlockSpec((None, k, nc), lambda j: (j, 0, 0))],
                    pl.BlockSpec((None, s, nc), lambda j: (j, 0, 0)),
                    jax.ShapeDtypeStruct((N_CHIPS, s, nc), F32), (x, wg))


def _mm_dx(dh, wg, dz, transposed, name):
    s, d = dz.shape
    tm, tn = min(1024, s), 512
    res = pl.BlockSpec((tm, tn), lambda i, j: (i, j))
    if transposed:
        wt = wg.reshape(-1, d)
        n = wt.shape[0]

        def body(a_ref, w_ref, r_ref, o_ref):
            o_ref[...] = DN_ALPHA * r_ref[...] + _dot(a_ref[...], w_ref[...])

        return _mm_call(body, name, (s // tm, d // tn),
                        [pl.BlockSpec((tm, n), lambda i, j: (i, 0)), pl.BlockSpec((n, tn), lambda i, j: (0, j)), res],
                        res, jax.ShapeDtypeStruct((s, d), F32), (dh, wt, dz))
    nc = dh.shape[2]

    def body(a_ref, w_ref, r_ref, o_ref):
        acc = DN_ALPHA * r_ref[...]
        for j in range(N_CHIPS):
            acc = acc + _dot_nt(a_ref[j], w_ref[j])
        o_ref[...] = acc

    return _mm_call(body, name, (s // tm, d // tn),
                    [pl.BlockSpec((N_CHIPS, tm, nc), lambda i, j: (0, i, 0)),
                     pl.BlockSpec((N_CHIPS, tn, nc), lambda i, j: (0, j, 0)), res],
                    res, jax.ShapeDtypeStruct((s, d), F32), (dh, wg, dz))


def _mm_dw_in(x, dh, transposed, name):
    s, d = x.shape
    x_spec = _resident((s, d), lambda j: (0, 0))

    def body(a_ref, b_ref, o_ref):
        o_ref[...] = _dot_tn(a_ref[...], b_ref[...]).astype(o_ref.dtype)

    if transposed:
        n, tn = dh.shape[1], ROWS_OF_WT
        flat = _mm_call(body, name, (n // tn,), [pl.BlockSpec((s, tn), lambda j: (0, j)), x_spec],
                        pl.BlockSpec((tn, d), lambda j: (j, 0)), jax.ShapeDtypeStruct((n, d), MXU_DTYPE), (dh, x))
        return flat.reshape(N_CHIPS, n // N_CHIPS, d)
    nc = dh.shape[2]
    return _mm_call(body, name, (N_CHIPS,), [x_spec, pl.BlockSpec((None, s, nc), lambda j: (j, 0, 0))],
                    pl.BlockSpec((None, d, nc), lambda j: (j, 0, 0)),
                    jax.ShapeDtypeStruct((N_CHIPS, d, nc), MXU_DTYPE), (x, dh))


def _mm_out_norm(mix, wg, x, g, b, name, target=None, after=None):
    s, k = mix.shape
    _, kc, d = wg.shape
    tm = min(512, s)

    def normed(a_ref, w_ref, x_ref, g_ref, b_ref, z_ref):
        z = DN_ALPHA * x_ref[...] + _dot(a_ref[...], w_ref[...].reshape(N_CHIPS * kc, d))
        mu = jnp.mean(z, axis=1, keepdims=True)
        zc = z - mu
        var = jnp.mean(zc * zc, axis=1, keepdims=True)
        z_ref[...] = z
        return zc * lax.rsqrt(var + LN_EPS) * g_ref[...] + b_ref[...]

    def body(a_ref, w_ref, x_ref, g_ref, b_ref, after_ref, z_ref, o_ref, ob_ref):
        o = normed(a_ref, w_ref, x_ref, g_ref, b_ref, z_ref)
        o_ref[...] = o
        ob_ref[...] = o.astype(ob_ref.dtype)

    def body_with_loss(a_ref, w_ref, x_ref, g_ref, b_ref, t_ref, z_ref, do_ref, l_ref):
        @pl.when(pl.program_id(0) == 0)
        def _():
            l_ref[...] = jnp.zeros_like(l_ref)

        e = normed(a_ref, w_ref, x_ref, g_ref, b_ref, z_ref) - t_ref[...]
        do_ref[...] = e * (1.0 / d)
        l_ref[...] += (0.5 / d) * jnp.sum(jnp.sum(e * e, axis=1, keepdims=True), axis=0, keepdims=True)

    row = pl.BlockSpec((tm, d), lambda i: (i, 0))
    vec = pl.BlockSpec((1, d), lambda i: (0, 0))
    in_specs = [pl.BlockSpec((tm, k), lambda i: (i, 0)), _resident((N_CHIPS, kc, d), lambda i: (0, 0, 0)), row, vec,
                vec]
    args = (mix, wg, x, g.reshape(1, d), b.reshape(1, d))
    f32_rows = jax.ShapeDtypeStruct((s, d), F32)
    if target is None:
        return pl.pallas_call(
            body, name=name, grid=(s // tm,), in_specs=in_specs + [ANY], out_specs=[row, row, row],
            out_shape=[f32_rows, f32_rows, jax.ShapeDtypeStruct((s, d), MXU_DTYPE)],
            compiler_params=_params(("parallel",)))(*args, after)
    return pl.pallas_call(
        body_with_loss, name=name, grid=(s // tm,), in_specs=in_specs + [row],
        out_specs=[row, row, pl.BlockSpec((1, 1), lambda i: (0, 0))],
        out_shape=[f32_rows, f32_rows, jax.ShapeDtypeStruct((1, 1), F32)],
        compiler_params=_params(("arbitrary",)))(*args, target)


def _mm_out_bwd(dz, wg, mix, after, name):
    s, d = dz.shape
    kc = wg.shape[1]

    def body(dz_ref, w_ref, mix_ref, after_ref, dmix_ref, dw_ref):
        dmix_ref[...] = _dot_nt(dz_ref[...], w_ref[...])
        dw_ref[...] = _dot_tn(mix_ref[...], dz_ref[...]).astype(dw_ref.dtype)

    return pl.pallas_call(
        body, name=name, grid=(N_CHIPS,),
        in_specs=[_resident((s, d), lambda j: (0, 0)), pl.BlockSpec((None, kc, d), lambda j: (j, 0, 0)),
                  pl.BlockSpec((s, kc), lambda j: (0, j)), ANY],
        out_specs=[pl.BlockSpec((s, kc), lambda j: (0, j)), pl.BlockSpec((None, kc, d), lambda j: (j, 0, 0))],
        out_shape=[jax.ShapeDtypeStruct((s, N_CHIPS * kc), F32), jax.ShapeDtypeStruct((N_CHIPS, kc, d), MXU_DTYPE)],
        compiler_params=_params(("parallel",)),
    )(dz, wg, mix, after)


def _ln_bwd(dout, z, g, after):
    s, d = z.shape
    tr = min(256, s)

    def body(do_ref, z_ref, g_ref, after_ref, dz_ref, dzb_ref, dg_ref, db_ref):
        @pl.when(pl.program_id(0) == 0)
        def _():
            dg_ref[...] = jnp.zeros_like(dg_ref)
            db_ref[...] = jnp.zeros_like(db_ref)

        zz = z_ref[...]
        do = do_ref[...]
        mu = jnp.mean(zz, axis=1, keepdims=True)
        zc = zz - mu
        var = jnp.mean(zc * zc, axis=1, keepdims=True)
        rstd = lax.rsqrt(var + LN_EPS)
        zh = zc * rstd
        dg_ref[...] += jnp.sum(do * zh, axis=0, keepdims=True)
        db_ref[...] += jnp.sum(do, axis=0, keepdims=True)
        dzh = do * g_ref[...]
        dz = rstd * (dzh - jnp.mean(dzh, axis=1, keepdims=True) - zh * jnp.mean(dzh * zh, axis=1, keepdims=True))
        dz_ref[...] = dz
        dzb_ref[...] = dz.astype(MXU_DTYPE)

    row = pl.BlockSpec((tr, d), lambda i: (i, 0))
    vec = pl.BlockSpec((1, d), lambda i: (0, 0))
    return pl.pallas_call(
        body, name="ln_bwd", grid=(s // tr,), in_specs=[row, row, vec, ANY], out_specs=[row, row, vec, vec],
        out_shape=[jax.ShapeDtypeStruct((s, d), F32), jax.ShapeDtypeStruct((s, d), MXU_DTYPE),
                   jax.ShapeDtypeStruct((1, d), F32), jax.ShapeDtypeStruct((1, d), F32)],
        compiler_params=_params(("arbitrary",)),
    )(dout, z, g.reshape(1, d), after)


Q0, K0, V0, BG0 = 3072, 4096, 4224, 4352
T = CHUNK


def _lane_ids(width):
    return lax.broadcasted_iota(jnp.int32, (T, width), 1)


def _rope_swap(x):
    w = x.shape[1]
    l64 = _lane_ids(w) % B_HEAD_DIM
    half = ROT_DIM // 2
    return jnp.where(l64 < half, pltpu.roll(x, w - half, 1), jnp.where(l64 < ROT_DIM, pltpu.roll(x, half, 1), 0.0))


def _rope(x, c, s):
    return x * c + _rope_swap(x) * s


def _rope_bwd(d, c, s):
    return d * c + _rope_swap(d * s)


def _tril():
    return lax.broadcasted_iota(jnp.int32, (T, T), 0) >= lax.broadcasted_iota(jnp.int32, (T, T), 1)


def _band_mask(has_prev):
    qi = lax.broadcasted_iota(jnp.int32, (T, 2 * T), 0)
    kj = lax.broadcasted_iota(jnp.int32, (T, 2 * T), 1)
    return (kj > qi) & (kj <= qi + WINDOW) & ((kj >= WINDOW) | has_prev)


def _a_norm(v, g, b):
    mu = jnp.mean(v, axis=1, keepdims=True)
    vc = v - mu
    var = jnp.mean(vc * vc, axis=1, keepdims=True)
    rstd = lax.rsqrt(var + LN_EPS)
    vh = vc * rstd
    return vh, rstd, vh * g + b


def _kv_operands(hk, k_rot, kp_rot, vv, vp):
    lo = _lane_ids(LANES) < B_HEAD_DIM
    mine = lo if hk == 0 else jnp.logical_not(lo)
    both = lambda t: jnp.where(mine, t, 0.0) + pltpu.roll(jnp.where(mine, t, 0.0), B_HEAD_DIM, 1)
    k2 = jnp.concatenate([both(kp_rot), both(k_rot)], axis=0)
    v2 = jnp.concatenate([both(vp), both(vv)], axis=0)
    return k2.astype(MXU_DTYPE), v2.astype(MXU_DTYPE)


GROUP_HEADS = B_Q_HEADS // B_KV_HEADS


def _pair_columns(hk):
    return [(hk * GROUP_HEADS // 2 + pp) * LANES for pp in range(GROUP_HEADS // 2)]


def _stacked_heads(pairs):
    lo = _lane_ids(LANES) < B_HEAD_DIM
    rows = []
    for tile in pairs:
        rows += [jnp.where(lo, tile, 0.0), jnp.where(lo, 0.0, tile)]
    return jnp.concatenate(rows, axis=0).astype(MXU_DTYPE)


def _pair_of(stacked, pp):
    lo = _lane_ids(LANES) < B_HEAD_DIM
    return jnp.where(lo, stacked[2 * pp * T:(2 * pp + 1) * T], stacked[(2 * pp + 1) * T:(2 * pp + 2) * T])


def _softmax_with_sink(qm, k2, valid, sink_ref, head0):
    scores = (_dot_nt(qm, k2) * ATTN_SCALE).reshape(GROUP_HEADS, T, 2 * T)
    s = jnp.where(valid[None], scores, NEG_BIG).reshape(GROUP_HEADS * T, 2 * T)
    sink = jnp.concatenate([jnp.broadcast_to(sink_ref[:, head0 + h:head0 + h + 1], (T, 1))
                            for h in range(GROUP_HEADS)], axis=0)
    m = jnp.maximum(jnp.max(s, axis=1, keepdims=True), sink)
    ex = jnp.exp(s - m)
    es = jnp.exp(sink - m)
    inv = 1.0 / (jnp.sum(ex, axis=1, keepdims=True) + es)
    return ex * inv, es * inv


def _even_specs(s):
    nb = s // T
    prev = lambda r: lambda i: jnp.maximum(r(i) - 1, 0)
    return nb, {
        'h': lambda r: pl.BlockSpec((T, EVEN_IN), lambda i: (r(i), 0)),
        'kprev': lambda r: pl.BlockSpec((T, LANES), lambda i: (prev(r)(i), K0 // LANES)),
        'vprev': lambda r: pl.BlockSpec((T, LANES), lambda i: (prev(r)(i), V0 // LANES)),
        'tab': lambda r: pl.BlockSpec((T, LANES), lambda i: (r(i), 0)),
        'tabprev': lambda r: pl.BlockSpec((T, LANES), lambda i: (prev(r)(i), 0)),
    }


def _full(shape):
    nd = len(shape)
    return pl.BlockSpec(shape, lambda i: (0,) * nd)


def _even_fwd(h, cq, sq, ln_g, ln_b, ws, bs_t, sinks, after):
    s = h.shape[0]
    nb, sp = _even_specs(s)

    def body(h_ref, kp_ref, vp_ref, cq_ref, sq_ref, ckp_ref, skp_ref, g_ref, b_ref, ws_ref, bst_ref, sink_ref,
             after_ref, mix_ref):
        i = pl.program_id(0)
        tril = _tril()
        _, _, vn = _a_norm(h_ref[:, 1024:2048], g_ref[...], b_ref[...])
        vnb = vn.astype(MXU_DTYPE)
        for g in range(A_GROUPS):
            c0, c1 = g * LANES, (g + 1) * LANES
            wg = jnp.where(tril, ws_ref[g], 0.0).astype(MXU_DTYPE)
            mixed = _dot(wg, vnb[:, c0:c1]) + bst_ref[:, g:g + 1]
            sil, _ = _silu_and_grad(h_ref[:, 2048 + c0:2048 + c1])
            mix_ref[:, c0:c1] = (h_ref[:, c0:c1] * mixed * sil).astype(mix_ref.dtype)

        cq, sq = cq_ref[...], sq_ref[...]
        k_rot = _rope(h_ref[:, K0:K0 + LANES], cq, sq)
        kp_rot = _rope(kp_ref[...], ckp_ref[...], skp_ref[...])
        valid = _band_mask(i > 0)
        for hk in range(B_KV_HEADS):
            k2, v2 = _kv_operands(hk, k_rot, kp_rot, h_ref[:, V0:V0 + LANES], vp_ref[...])
            qm = _stacked_heads([_rope(h_ref[:, Q0 + c0:Q0 + c0 + LANES], cq, sq) for c0 in _pair_columns(hk)])
            p, _ = _softmax_with_sink(qm, k2, valid, sink_ref, GROUP_HEADS * hk)
            out = _dot(p.astype(MXU_DTYPE), v2)
            for pp, c0 in enumerate(_pair_columns(hk)):
                sil, _ = _silu_and_grad(h_ref[:, BG0 + c0:BG0 + c0 + LANES])
                mix_ref[:, A_WIDTH + c0:A_WIDTH + c0 + LANES] = (_pair_of(out, pp) * sil).astype(mix_ref.dtype)

    ident = lambda i: i
    return pl.pallas_call(
        body, name="even_fwd", grid=(nb,),
        in_specs=[sp['h'](ident), sp['kprev'](ident), sp['vprev'](ident), sp['tab'](ident), sp['tab'](ident),
                  sp['tabprev'](ident), sp['tabprev'](ident), _full((1, A_WIDTH)), _full((1, A_WIDTH)),
                  _full((A_GROUPS, T, T)), _full((T, A_GROUPS)), _full((1, B_Q_HEADS)), ANY],
        out_specs=pl.BlockSpec((T, 2 * A_WIDTH), lambda i: (i, 0)),
        out_shape=jax.ShapeDtypeStruct((s, 2 * A_WIDTH), MXU_DTYPE),
        compiler_params=_params(("parallel",)),
    )(h, h, h, cq, sq, cq, sq, ln_g.reshape(1, -1), ln_b.reshape(1, -1), ws, bs_t, sinks.reshape(1, -1), after)


def _even_bwd(h, dmix, cq, sq, ln_g, ln_b, ws, bs_t, sinks):
    s = h.shape[0]
    nb, sp = _even_specs(s)

    def body(h_ref, kp_ref, vp_ref, cq_ref, sq_ref, ckp_ref, skp_ref, dm_ref, g_ref, b_ref, ws_ref, bst_ref, sink_ref,
             dh_ref, dg_ref, db_ref, dws_ref, dbst_ref, dsink_ref, dvn_scr, ck_scr, cv_scr):
        i = pl.program_id(0)

        @pl.when(i == 0)
        def _():
            for ref in (dg_ref, db_ref, dws_ref, dbst_ref, dsink_ref, ck_scr, cv_scr):
                ref[...] = jnp.zeros_like(ref)

        tril = _tril()
        vh, rstd, vn = _a_norm(h_ref[:, 1024:2048], g_ref[...], b_ref[...])
        vnb = vn.astype(MXU_DTYPE)
        for g in range(A_GROUPS):
            c0, c1 = g * LANES, (g + 1) * LANES
            wg = jnp.where(tril, ws_ref[g], 0.0).astype(MXU_DTYPE)
            mixed = _dot(wg, vnb[:, c0:c1]) + bst_ref[:, g:g + 1]
            ag = h_ref[:, 2048 + c0:2048 + c1]
            u = h_ref[:, c0:c1]
            sil, dsil = _silu_and_grad(ag)
            da = dm_ref[:, c0:c1]
            dh_ref[:, c0:c1] = (da * mixed * sil).astype(dh_ref.dtype)
            dh_ref[:, 2048 + c0:2048 + c1] = (da * u * mixed * dsil).astype(dh_ref.dtype)
            dmx = da * u * sil
            dmb = dmx.astype(MXU_DTYPE)
            dvn_scr[:, c0:c1] = _dot_tn(wg, dmb)
            dws_ref[g] += jnp.where(tril, _dot_nt(dmb, vnb[:, c0:c1]), 0.0)
            dbst_ref[:, g:g + 1] += jnp.sum(dmx, axis=1, keepdims=True)
        dvn = dvn_scr[...]
        dg_ref[...] += jnp.sum(dvn * vh, axis=0, keepdims=True)
        db_ref[...] += jnp.sum(dvn, axis=0, keepdims=True)
        dvh = dvn * g_ref[...]
        dv = rstd * (dvh - jnp.mean(dvh, axis=1, keepdims=True) - vh * jnp.mean(dvh * vh, axis=1, keepdims=True))
        dh_ref[:, 1024:2048] = dv.astype(dh_ref.dtype)

        cq, sq = cq_ref[...], sq_ref[...]
        k_rot = _rope(h_ref[:, K0:K0 + LANES], cq, sq)
        kp_rot = _rope(kp_ref[...], ckp_ref[...], skp_ref[...])
        valid = _band_mask(i < nb - 1)
        dk_all = jnp.zeros((2 * T, LANES), F32)
        dv_all = jnp.zeros((2 * T, LANES), F32)
        for hk in range(B_KV_HEADS):
            k2, v2 = _kv_operands(hk, k_rot, kp_rot, h_ref[:, V0:V0 + LANES], vp_ref[...])
            cols = _pair_columns(hk)
            qm = _stacked_heads([_rope(h_ref[:, Q0 + c0:Q0 + c0 + LANES], cq, sq) for c0 in cols])
            p, psink = _softmax_with_sink(qm, k2, valid, sink_ref, GROUP_HEADS * hk)
            pb = p.astype(MXU_DTYPE)
            out = _dot(pb, v2)
            dobs = []
            for pp, c0 in enumerate(cols):
                sil, dsil = _silu_and_grad(h_ref[:, BG0 + c0:BG0 + c0 + LANES])
                dbo = dm_ref[:, A_WIDTH + c0:A_WIDTH + c0 + LANES]
                dh_ref[:, BG0 + c0:BG0 + c0 + LANES] = (dbo * _pair_of(out, pp) * dsil).astype(dh_ref.dtype)
                dobs.append(dbo * sil)
            dobm = _stacked_heads(dobs)
            dp = _dot_nt(dobm, v2)
            rs = jnp.sum(p * dp, axis=1, keepdims=True)
            dsb = (p * (dp - rs) * ATTN_SCALE).astype(MXU_DTYPE)
            to_sink = psink * rs
            for h in range(GROUP_HEADS):
                head = GROUP_HEADS * hk + h
                dsink_ref[:, head:head + 1] += -jnp.sum(to_sink[h * T:(h + 1) * T], axis=0, keepdims=True)
            dq = _dot(dsb, k2)
            for pp, c0 in enumerate(cols):
                dh_ref[:, Q0 + c0:Q0 + c0 + LANES] = _rope_bwd(_pair_of(dq, pp), cq, sq).astype(dh_ref.dtype)
            acc_k = _dot_tn(dsb, qm)
            acc_v = _dot_tn(pb, dobm)
            lo2 = lax.broadcasted_iota(jnp.int32, (2 * T, LANES), 1) < B_HEAD_DIM
            mine = lo2 if hk == 0 else jnp.logical_not(lo2)
            dk_all = dk_all + jnp.where(mine, acc_k + pltpu.roll(acc_k, B_HEAD_DIM, 1), 0.0)
            dv_all = dv_all + jnp.where(mine, acc_v + pltpu.roll(acc_v, B_HEAD_DIM, 1), 0.0)
        dk_rot = dk_all[T:2 * T] + ck_scr[...]
        dvv = dv_all[T:2 * T] + cv_scr[...]
        ck_scr[...] = dk_all[0:T]
        cv_scr[...] = dv_all[0:T]
        dh_ref[:, K0:K0 + LANES] = _rope_bwd(dk_rot, cq, sq).astype(dh_ref.dtype)
        dh_ref[:, V0:V0 + LANES] = dvv.astype(dh_ref.dtype)

    rev = lambda i: nb - 1 - i
    return pl.pallas_call(
        body, name="even_bwd", grid=(nb,),
        in_specs=[sp['h'](rev), sp['kprev'](rev), sp['vprev'](rev), sp['tab'](rev), sp['tab'](rev),
                  sp['tabprev'](rev), sp['tabprev'](rev), pl.BlockSpec((T, 2 * A_WIDTH), lambda i: (rev(i), 0)),
                  _full((1, A_WIDTH)), _full((1, A_WIDTH)), _full((A_GROUPS, T, T)), _full((T, A_GROUPS)),
                  _full((1, B_Q_HEADS))],
        out_specs=[sp['h'](rev), _full((1, A_WIDTH)), _full((1, A_WIDTH)),
                   _full((A_GROUPS, T, T)), _full((T, A_GROUPS)), _full((1, B_Q_HEADS))],
        out_shape=[jax.ShapeDtypeStruct((s, EVEN_IN), MXU_DTYPE), jax.ShapeDtypeStruct((1, A_WIDTH), F32),
                   jax.ShapeDtypeStruct((1, A_WIDTH), F32), jax.ShapeDtypeStruct((A_GROUPS, T, T), F32),
                   jax.ShapeDtypeStruct((T, A_GROUPS), F32), jax.ShapeDtypeStruct((1, B_Q_HEADS), F32)],
        scratch_shapes=[pltpu.VMEM((T, A_WIDTH), F32), pltpu.VMEM((T, LANES), F32), pltpu.VMEM((T, LANES), F32)],
        compiler_params=_params(("arbitrary",)),
    )(h, h, h, cq, sq, cq, sq, dmix, ln_g.reshape(1, -1), ln_b.reshape(1, -1), ws, bs_t, sinks.reshape(1, -1))


def _shift_down(x, d, fill=0.0):
    rows = lax.broadcasted_iota(jnp.int32, x.shape, 0)
    return jnp.where(rows >= d, pltpu.roll(x, d, 0), fill)


def _shift_up(x, d, fill=0.0):
    n = x.shape[0]
    rows = lax.broadcasted_iota(jnp.int32, x.shape, 0)
    return jnp.where(rows < n - d, pltpu.roll(x, n - d, 0), fill)


SCAN_BLOCK = 32
SUBLANES = 8


def _scan_block(a, b, up):
    n = a.shape[0]
    d = 1
    while d < n:
        if d < SUBLANES:
            shift = _shift_up if up else _shift_down
            a, b = a * shift(a, d, 1.0), a * shift(b, d) + b
        elif up:
            a, b = (jnp.concatenate([a[:n - d] * a[d:], a[n - d:]], axis=0),
                    jnp.concatenate([a[:n - d] * b[d:] + b[:n - d], b[n - d:]], axis=0))
        else:
            a, b = (jnp.concatenate([a[:d], a[d:] * a[:n - d]], axis=0),
                    jnp.concatenate([b[:d], a[d:] * b[:n - d] + b[d:]], axis=0))
        d *= 2
    return a, b


def _scan(a, b, up):
    n = a.shape[0]
    size = min(SCAN_BLOCK, n)
    blocks = n // size
    out, carry = [None] * blocks, None
    for k in (reversed(range(blocks)) if up else range(blocks)):
        prod, h = _scan_block(a[k * size:(k + 1) * size], b[k * size:(k + 1) * size], up)
        if carry is not None:
            h = h + prod * carry
        carry = h[0:1] if up else h[size - 1:size]
        out[k] = h
    return jnp.concatenate(out, axis=0)


def _neg_expm1(y):
    t = jnp.tanh(-0.5 * y)
    return 2.0 * t / (1.0 + t)


def _softplus(x):
    return jnp.maximum(x, 0.0) + jnp.log1p(jnp.exp(-jnp.abs(x)))


def _lru_forward(xc, cw, cb, wa, ba, wx, bx, lam):
    shifted = [_shift_down(xc, CONV_WIDTH - 1 - j) if j < CONV_WIDTH - 1 else xc for j in range(CONV_WIDTH)]
    xconv = cb
    for j in range(CONV_WIDTH):
        xconv = xconv + cw[j:j + 1, :] * shifted[j]
    xb = xconv.astype(MXU_DTYPE)
    r = _sigmoid(_dot(xb, wa.astype(MXU_DTYPE)) + ba)
    ig = _sigmoid(_dot(xb, wx.astype(MXU_DTYPE)) + bx)
    sp = _softplus(-lam)
    log_a = -LRU_C * r * sp
    a = jnp.exp(log_a)
    mult = jnp.sqrt(_neg_expm1(2.0 * log_a))
    hs = _scan(a, mult * ig * xconv, up=False)
    return dict(shifted=shifted, xconv=xconv, xb=xb, r=r, ig=ig, sp=sp, a=a, mult=mult, hs=hs)


def _col(width, off):
    return lambda s: pl.BlockSpec((s, width), lambda i: (0, off + i))


def _part(width, slot):
    return lambda s: pl.BlockSpec((None, s, width), lambda i: (slot, 0, i))


def _lru_specs(s):
    w = LANES
    return [_part(w, 0)(s), _part(w, 1)(s), pl.BlockSpec((CONV_WIDTH, w), lambda i: (0, i)),
            pl.BlockSpec((1, w), lambda i: (0, i)), pl.BlockSpec((None, w, w), lambda i: (i, 0, 0)),
            pl.BlockSpec((1, w), lambda i: (0, i)), pl.BlockSpec((None, w, w), lambda i: (i, 0, 0)),
            pl.BlockSpec((1, w), lambda i: (0, i)), pl.BlockSpec((1, w), lambda i: (0, i))]


def _lru_fwd(h, cw, cb, wa, ba, wx, bx, lam):
    s = h.shape[1]

    def body(xc_ref, cg_ref, cw_ref, cb_ref, wa_ref, ba_ref, wx_ref, bx_ref, lam_ref, o_ref):
        f = _lru_forward(xc_ref[...], cw_ref[...], cb_ref[...], wa_ref[...], ba_ref[...], wx_ref[...], bx_ref[...],
                         lam_ref[...])
        sil, _ = _silu_and_grad(cg_ref[...])
        o_ref[...] = (f['hs'] * sil).astype(o_ref.dtype)

    return pl.pallas_call(
        body, name="lru_fwd", grid=(C_HEADS,), in_specs=_lru_specs(s), out_specs=_col(LANES, 0)(s),
        out_shape=jax.ShapeDtypeStruct((s, C_WIDTH + D_WIDTH), MXU_DTYPE), compiler_params=_params(("parallel",)),
    )(h, h, cw, cb.reshape(1, -1), wa, ba.reshape(1, -1), wx, bx.reshape(1, -1), lam.reshape(1, -1))


def _lru_bwd(h, dmix, cw, cb, wa, ba, wx, bx, lam):
    s = h.shape[1]

    def body(xc_ref, cg_ref, cw_ref, cb_ref, wa_ref, ba_ref, wx_ref, bx_ref, lam_ref, dco_ref,
             dh_ref, dcw_ref, dcb_ref, dwa_ref, dba_ref, dwx_ref, dbx_ref, dlam_ref):
        dxc_ref, dcg_ref = dh_ref.at[0], dh_ref.at[1]
        cw, lam = cw_ref[...], lam_ref[...]
        wab, wxb = wa_ref[...].astype(MXU_DTYPE), wx_ref[...].astype(MXU_DTYPE)
        f = _lru_forward(xc_ref[...], cw, cb_ref[...], wa_ref[...], ba_ref[...], wx_ref[...], bx_ref[...], lam)
        sil, dsil = _silu_and_grad(cg_ref[...])
        dco = dco_ref[...]
        hs, a, mult, ig, r, xconv = f['hs'], f['a'], f['mult'], f['ig'], f['r'], f['xconv']
        dcg_ref[...] = (dco * hs * dsil).astype(dcg_ref.dtype)
        lamb = _scan(_shift_up(a, 1), dco * sil, up=True)
        da = lamb * _shift_down(hs, 1)
        d_ig = lamb * mult * xconv
        d_mult = lamb * ig * xconv
        dxconv = lamb * mult * ig
        d_log_a = da * a - d_mult * (a * a) / mult
        dlam_ref[...] = jnp.sum(d_log_a * r, axis=0, keepdims=True) * LRU_C * _sigmoid(-lam)
        dpa = d_log_a * (-LRU_C * f['sp']) * r * (1.0 - r)
        dpx = d_ig * ig * (1.0 - ig)
        dba_ref[...] = jnp.sum(dpa, axis=0, keepdims=True)
        dbx_ref[...] = jnp.sum(dpx, axis=0, keepdims=True)
        dpab, dpxb = dpa.astype(MXU_DTYPE), dpx.astype(MXU_DTYPE)
        dwa_ref[...] = _dot_tn(f['xb'], dpab)
        dwx_ref[...] = _dot_tn(f['xb'], dpxb)
        dxconv = dxconv + _dot_nt(dpab, wab) + _dot_nt(dpxb, wxb)
        dcb_ref[...] = jnp.sum(dxconv, axis=0, keepdims=True)
        dxc = jnp.zeros_like(dxconv)
        for j in range(CONV_WIDTH):
            dcw_ref[j:j + 1, :] = jnp.sum(dxconv * f['shifted'][j], axis=0, keepdims=True)
            back = _shift_up(dxconv, CONV_WIDTH - 1 - j) if j < CONV_WIDTH - 1 else dxconv
            dxc = dxc + cw[j:j + 1, :] * back
        dxc_ref[...] = dxc.astype(dxc_ref.dtype)

    w = LANES
    vec = pl.BlockSpec((1, w), lambda i: (0, i))
    mat = pl.BlockSpec((None, w, w), lambda i: (i, 0, 0))
    vshape = jax.ShapeDtypeStruct((1, C_WIDTH), F32)
    mshape = jax.ShapeDtypeStruct((C_HEADS, w, w), F32)
    return pl.pallas_call(
        body, name="lru_bwd", grid=(C_HEADS,), in_specs=_lru_specs(s) + [_col(w, 0)(s)],
        out_specs=[pl.BlockSpec((2, s, w), lambda i: (0, 0, i)), pl.BlockSpec((CONV_WIDTH, w), lambda i: (0, i)),
                   vec, mat, vec, mat, vec, vec],
        out_shape=[jax.ShapeDtypeStruct((N_CHIPS, s, C_WIDTH), MXU_DTYPE),
                   jax.ShapeDtypeStruct((CONV_WIDTH, C_WIDTH), F32), vshape, mshape, vshape, mshape, vshape, vshape],
        compiler_params=_params(("parallel",)),
    )(h, h, cw, cb.reshape(1, -1), wa, ba.reshape(1, -1), wx, bx.reshape(1, -1), lam.reshape(1, -1), dmix)


def _pool_window(g):
    return jnp.left_shift(2, g).astype(F32)


def _select_window(g, sums):
    out = sums[-1]
    for k in range(len(sums) - 2, -1, -1):
        out = jnp.where(g == k, sums[k], out)
    return out


def _window_sums(x, shift):
    sums, cur = [], x
    for k in range(D_GROUPS):
        cur = cur + shift(cur, 2 ** k)
        sums.append(cur)
    return sums


def _pool_specs(s):
    w = D_GROUP_DIM
    return [_part(w, 2)(s), _part(w, 3)(s),
            pl.BlockSpec((None, w, w), lambda i: (i, 0, 0)), pl.BlockSpec((1, w), lambda i: (0, i))]


def _pool_forward(xd, g):
    rows = lax.broadcasted_iota(jnp.int32, xd.shape, 0).astype(F32) + 1.0
    cnt = jnp.minimum(rows, _pool_window(g))
    pooled = _select_window(g, _window_sums(xd, _shift_down)) / cnt - xd
    return pooled, cnt


def _pool_fwd(h, mix, w_pool, d_scale):
    s = h.shape[1]

    def body(xd_ref, dg_ref, wp_ref, sc_ref, mix_ref, o_ref):
        pooled, _ = _pool_forward(xd_ref[...], pl.program_id(0))
        mixed = _dot(pooled.astype(MXU_DTYPE), wp_ref[...].astype(MXU_DTYPE))
        sil, _ = _silu_and_grad(dg_ref[...])
        o_ref[...] = (mixed * sc_ref[...] * sil).astype(o_ref.dtype)

    return pl.pallas_call(
        body, name="pool_fwd", grid=(D_GROUPS,), in_specs=_pool_specs(s) + [ANY],
        out_specs=_col(D_GROUP_DIM, C_WIDTH // D_GROUP_DIM)(s), out_shape=jax.ShapeDtypeStruct(mix.shape, mix.dtype),
        input_output_aliases={4: 0}, compiler_params=_params(("parallel",)),
    )(h, h, w_pool, d_scale.reshape(1, -1), mix)


def _pool_bwd(h, dmix, dh, w_pool, d_scale):
    s = h.shape[1]
    w = D_GROUP_DIM

    def body(xd_ref, dg_ref, wp_ref, sc_ref, ddo_ref, dh_in_ref, dh_ref, dwp_ref, dsc_ref):
        dxd_ref, ddg_ref = dh_ref.at[0], dh_ref.at[1]
        g = pl.program_id(0)
        pooled, cnt = _pool_forward(xd_ref[...], g)
        pb = pooled.astype(MXU_DTYPE)
        wpb = wp_ref[...].astype(MXU_DTYPE)
        mixed = _dot(pb, wpb)
        sil, dsil = _silu_and_grad(dg_ref[...])
        ddo = ddo_ref[...]
        sc = sc_ref[...]
        ddg_ref[...] = (ddo * mixed * sc * dsil).astype(ddg_ref.dtype)
        dms = ddo * sil
        dsc_ref[...] = jnp.sum(dms * mixed, axis=0, keepdims=True)
        dmb = (dms * sc).astype(MXU_DTYPE)
        dwp_ref[...] = _dot_tn(pb, dmb)
        dpooled = _dot_nt(dmb, wpb)
        dxd = _select_window(g, _window_sums(dpooled / cnt, _shift_up)) - dpooled
        dxd_ref[...] = dxd.astype(dxd_ref.dtype)

    return pl.pallas_call(
        body, name="pool_bwd", grid=(D_GROUPS,), in_specs=_pool_specs(s) + [_col(w, C_WIDTH // w)(s), ANY],
        out_specs=[pl.BlockSpec((2, s, w), lambda i: (1, 0, i)), pl.BlockSpec((None, w, w), lambda i: (i, 0, 0)),
                   pl.BlockSpec((1, w), lambda i: (0, i))],
        out_shape=[jax.ShapeDtypeStruct(dh.shape, dh.dtype),
                   jax.ShapeDtypeStruct((D_GROUPS, w, w), F32), jax.ShapeDtypeStruct((1, D_WIDTH), F32)],
        input_output_aliases={5: 0}, compiler_params=_params(("parallel",)),
    )(h, h, w_pool, d_scale.reshape(1, -1), dmix, dh)


def _rope_tables(positions):
    s = positions.shape[0]
    inv_freq = ROPE_THETA ** (-jnp.arange(0, ROT_DIM, 2, dtype=F32) / ROT_DIM)
    ang = positions.astype(F32)[:, None] * inv_freq
    c, sn = jnp.cos(ang), jnp.sin(ang)
    rest = B_HEAD_DIM - ROT_DIM
    c64 = jnp.concatenate([c, c, jnp.ones((s, rest), F32)], axis=1)
    s64 = jnp.concatenate([-sn, sn, jnp.zeros((s, rest), F32)], axis=1)
    return jnp.tile(c64, (1, LANES // B_HEAD_DIM)), jnp.tile(s64, (1, LANES // B_HEAD_DIM))


def _local_step(x, positions, target, w, projections_of, projections_ahead, projection_grads, norm_grads_done):
    cq, sq = _rope_tables(positions)
    saved = []
    xin, xin_b = x, x.astype(MXU_DTYPE)
    for layer in range(DEPTH):
        j = layer // 2
        kind = 'even' if layer % 2 == 0 else 'odd'
        w_in, w_out = projections_of(layer, xin_b)
        h = _mm_in(xin_b, w_in, W_IN_TRANSPOSED[kind], "mm_" + kind + "_in")
        if kind == 'even':
            behind, w_out_after = w_out(h) if callable(w_out) else (h, None)
            mix = _even_fwd(h, cq, sq, w['even_a_ln_g'][j], w['even_a_ln_b'][j], w['even_a_ws'][j],
                            w['even_a_bs'][j].T, w['even_b_sinks'][j], behind)
            w_out = w_out_after(mix) if w_out_after else w_out
        else:
            mix = _lru_fwd(h, w['odd_conv_w'][j], w['odd_conv_b'][j], w['odd_w_a'][j], w['odd_b_a'][j],
                           w['odd_w_x'][j], w['odd_b_x'][j], w['odd_lam'][j])
            mix = _pool_fwd(h, mix, w['odd_w_pool'][j], w['odd_d_scale'][j])
        if layer == DEPTH - 1:
            z, out, out_b = _mm_out_norm(mix, w_out, xin, w[kind + '_ln_g'][j], w[kind + '_ln_b'][j],
                                         "mm_out_norm_loss", target=target)
        else:
            z, out, out_b = _mm_out_norm(mix, w_out, xin, w[kind + '_ln_g'][j], w[kind + '_ln_b'][j], "mm_out_norm",
                                         after=projections_ahead(layer + 1, mix))
        saved.append((xin_b, h, mix, z, w_in, w_out))
        xin, xin_b = out, out_b

    dout, loss = xin, xin_b
    per_layer = {n: [None, None] for n in SMALL_NAMES}
    behind = loss
    for layer in reversed(range(DEPTH)):
        j = layer // 2
        kind = 'even' if layer % 2 == 0 else 'odd'
        xin_b, h, mix, z, w_in, w_out = saved[layer]
        dz, dz_b, dg, db = _ln_bwd(dout, z, w[kind + '_ln_g'][j], behind)
        per_layer[kind + '_ln_g'][j], per_layer[kind + '_ln_b'][j] = dg[0], db[0]
        dmix, dw_out = _mm_out_bwd(dz_b, w_out, mix, norm_grads_done(layer, dz_b), "mm_out_bwd")
        if kind == 'even':
            dh, dlg, dlb, dws, dbst, dsink = _even_bwd(
                h, dmix, cq, sq, w['even_a_ln_g'][j], w['even_a_ln_b'][j], w['even_a_ws'][j], w['even_a_bs'][j].T,
                w['even_b_sinks'][j])
            for n, val in (('even_a_ln_g', dlg[0]), ('even_a_ln_b', dlb[0]), ('even_a_ws', dws),
                           ('even_a_bs', dbst.T), ('even_b_sinks', dsink[0])):
                per_layer[n][j] = val
        else:
            dh, dcw, dcb, dwa, dba, dwx, dbx, dlam = _lru_bwd(
                h, dmix, w['odd_conv_w'][j], w['odd_conv_b'][j], w['odd_w_a'][j], w['odd_b_a'][j], w['odd_w_x'][j],
                w['odd_b_x'][j], w['odd_lam'][j])
            dh, dwp, dsc = _pool_bwd(h, dmix, dh, w['odd_w_pool'][j], w['odd_d_scale'][j])
            for n, val in (('odd_conv_w', dcw), ('odd_conv_b', dcb[0]), ('odd_w_a', dwa), ('odd_b_a', dba[0]),
                           ('odd_w_x', dwx), ('odd_b_x', dbx[0]), ('odd_lam', dlam[0]), ('odd_w_pool', dwp),
                           ('odd_d_scale', dsc[0])):
                per_layer[n][j] = val
        dout = _mm_dx(dh, w_in, dz, W_IN_TRANSPOSED[kind], "mm_dx_" + kind)
        dw_in = _mm_dw_in(xin_b, dh, W_IN_TRANSPOSED[kind], "mm_dw_in_" + kind)
        behind = projection_grads(layer, dw_in, dw_out)
    return loss, dout, {n: jnp.stack(v) for n, v in per_layer.items()}


ANY = pl.BlockSpec(memory_space=pl.ANY)


def _where_am_i():
    return lax.axis_index("x"), lax.axis_index("y"), lax.axis_index("c")


def _other_chips(x, y):
    return [(1 - x, y), (x, 1 - y), (1 - x, 1 - y)]


PACKED_ROWS = 16


def _half_rows(rows, which):
    half = rows // 2
    assert half % PACKED_ROWS == 0, rows
    return pl.ds(pl.multiple_of(which * half, PACKED_ROWS), half)


def _sibling_swap(bufs, name, after=()):
    n, na = len(bufs), len(after)

    def body(*refs):
        x_refs, out_refs, (send_sems, recv_sems) = refs[:n], refs[n + na:2 * n + na], refs[2 * n + na:]
        x, y, c = _where_am_i()
        copies = []
        for p in range(n):
            copies.append(pltpu.make_async_remote_copy(
                src_ref=x_refs[p], dst_ref=out_refs[p], send_sem=send_sems.at[p], recv_sem=recv_sems.at[p],
                device_id=(x, y, 1 - c), device_id_type=MESH))
            copies[-1].start()
        for cp in copies:
            cp.wait()

    return pl.pallas_call(
        body, name=name, in_specs=[ANY] * (n + na), out_specs=[ANY] * n,
        out_shape=[jax.ShapeDtypeStruct(b.shape, b.dtype) for b in bufs],
        scratch_shapes=[pltpu.SemaphoreType.DMA((n,)), pltpu.SemaphoreType.DMA((n,))],
    )(*bufs, *after)


HBM = pl.BlockSpec(memory_space=pltpu.HBM)
SEM = pl.BlockSpec(memory_space=pltpu.SEMAPHORE)
IN_FLIGHT = pltpu.CompilerParams(has_side_effects=pltpu.SideEffectType.DATAFLOW_SIDE_EFFECTING)


IN_PLACE = ('forward',)


def _peers(kind, x, y, c):
    if kind in ('swap', 'join'):
        return [(x, y, 1 - c)]
    return [(x, y, 1 - c) if kind == 'forward' else (px, py, c) for px, py in _other_chips(x, y)]


def _n_peers(kind):
    return 1 if kind in ('swap', 'join') else N_CHIPS - 1


def _land_shape(kind, a):
    if kind == 'gather':
        return (N_CHIPS,) + a.shape
    return (a.shape[0], a.shape[1] // 2, a.shape[2]) if kind == 'swap' else a.shape


def _ici_copy(kind, src_ref, land_ref, send_sems, recv_sems, k, peer, x, y, c, arriving=False):
    me = 2 * x + y
    if kind == 'gather':
        half = _half_rows(src_ref.shape[0], c)
        src, dst = src_ref.at[half, :], land_ref.at[me, half, :]
    elif kind == 'scatter':
        src, dst = src_ref.at[2 * peer[0] + peer[1]], land_ref.at[me]
    elif kind == 'swap':
        src, dst = src_ref.at[:, _half_rows(src_ref.shape[1], 1 - c), :], land_ref
    elif kind == 'join':
        src, dst = src_ref, land_ref
    else:
        px, py = _other_chips(x, y)[k % (N_CHIPS - 1)]
        rows = src_ref.at[2 * px + py, _half_rows(src_ref.shape[1], 1 - c if arriving else c), :]
        src, dst = rows, rows
    return pltpu.make_async_remote_copy(src_ref=src, dst_ref=dst, send_sem=send_sems.at[k], recv_sem=recv_sems.at[k],
                                        device_id=peer, device_id_type=MESH)


def _ici_start(kind, groups, name, after=()):
    flat = [a for g in groups for a in g]
    n, ng, np_, na = len(flat), len(groups), _n_peers(kind), len(after)
    nl = 0 if kind in IN_PLACE else n
    land_shapes = [_land_shape(kind, a) for a in flat[:nl]]

    def body(*refs):
        srcs, sems, token = refs[:n], refs[n + nl + na:n + nl + na + 2 * ng], refs[-1]
        lands = refs[n:n + nl] if nl else srcs
        x, y, c = _where_am_i()
        p = 0
        for gi, g in enumerate(groups):
            for q in range(len(g)):
                for k, peer in enumerate(_peers(kind, x, y, c)):
                    _ici_copy(kind, srcs[p], lands[p], sems[2 * gi], sems[2 * gi + 1], np_ * q + k, peer, x, y,
                              c).start()
                p += 1
        token[...] = jnp.zeros_like(token)

    sem_shapes = [pltpu.SemaphoreType.DMA((np_ * len(g),)) for g in groups for _ in range(2)]
    outs = pl.pallas_call(
        body, name=name, in_specs=[HBM] * (n + nl) + [ANY] * na,
        out_specs=[SEM] * (2 * ng) + [HBM] * (n + nl) + [pl.BlockSpec(memory_space=pltpu.VMEM)],
        out_shape=sem_shapes + [pltpu.HBM(a.shape, a.dtype) for a in flat]
        + [pltpu.HBM(s, a.dtype) for s, a in zip(land_shapes, flat)] + [jax.ShapeDtypeStruct((8, LANES), F32)],
        input_output_aliases={i: 2 * ng + i for i in range(n + nl)}, compiler_params=IN_FLIGHT,
    )(*[pltpu.with_memory_space_constraint(a, pltpu.HBM) for a in flat],
      *[pltpu.with_memory_space_constraint(lax.empty(s, a.dtype), pltpu.HBM) for s, a in zip(land_shapes, flat)],
      *after)
    sems, srcs, token = outs[:2 * ng], outs[2 * ng:2 * ng + n], outs[-1]
    lands = outs[2 * ng + n:2 * ng + n + nl] if nl else srcs
    started, p = [], 0
    for gi, g in enumerate(groups):
        started.append((sems[2 * gi], sems[2 * gi + 1], srcs[p:p + len(g)], lands[p:p + len(g)]))
        p += len(g)
    return started, token


def _ici_wait(kind, started, after, name):
    send_sems, recv_sems, srcs, lands = started
    n, np_ = len(srcs), _n_peers(kind)
    nl = 0 if kind in IN_PLACE else n
    after = list(after)

    def body(*refs):
        src_refs, send_ref, recv_ref = refs[:n], refs[n + nl], refs[n + nl + 1]
        land_refs = refs[n:n + nl] if nl else src_refs
        x, y, c = _where_am_i()
        for q in range(n):
            for k, peer in enumerate(_peers(kind, x, y, c)):
                args = (kind, src_refs[q], land_refs[q], send_ref, recv_ref, np_ * q + k, peer, x, y, c)
                _ici_copy(*args).wait_send()
                _ici_copy(*args, arriving=True).wait_recv()

    arrays = list(srcs) + (list(lands) if nl else [])
    outs = pl.pallas_call(
        body, name=name, in_specs=[HBM] * (n + nl) + [SEM, SEM] + [ANY] * len(after), out_specs=[HBM] * (n + nl),
        out_shape=[pltpu.HBM(a.shape, a.dtype) for a in arrays],
        input_output_aliases={i: i for i in range(n + nl)}, compiler_params=IN_FLIGHT,
    )(*arrays, send_sems, recv_sems, *after)
    return outs[:n], (outs[n:] if nl else outs[:n])


def _forward_to_sibling(lands, name):
    n = len(lands)

    def body(*refs):
        out_refs, (send_sems, recv_sems) = refs[n:2 * n], refs[2 * n:]
        x, y, c = _where_am_i()

        def copy(q, k, peer, half_of):
            rows = out_refs[q].at[2 * peer[0] + peer[1], _half_rows(lands[q].shape[1], half_of), :]
            return pltpu.make_async_remote_copy(
                src_ref=rows, dst_ref=rows, send_sem=send_sems.at[3 * q + k], recv_sem=recv_sems.at[3 * q + k],
                device_id=(x, y, 1 - c), device_id_type=MESH)

        sends = [copy(q, k, peer, c) for q in range(n) for k, peer in enumerate(_other_chips(x, y))]
        for cp in sends:
            cp.start()
        for q in range(n):
            for k, peer in enumerate(_other_chips(x, y)):
                copy(q, k, peer, 1 - c).wait_recv()
        for cp in sends:
            cp.wait_send()

    return pl.pallas_call(
        body, name=name, in_specs=[ANY] * n, out_specs=[ANY] * n,
        out_shape=[jax.ShapeDtypeStruct(a.shape, a.dtype) for a in lands],
        input_output_aliases={i: i for i in range(n)},
        scratch_shapes=[pltpu.SemaphoreType.DMA((3 * n,)), pltpu.SemaphoreType.DMA((3 * n,))],
    )(*lands)


def _with_own_slot(lands, own, chip):
    return [lax.dynamic_update_slice(land, mine[None], (chip, 0, 0)) for land, mine in zip(lands, own)]


def _row_tile(rows, row_bytes, n_bufs, budget=24 * 1024 * 1024):
    best = None
    for t in range(PACKED_ROWS, rows + 1, PACKED_ROWS):
        if rows % t == 0 and 2 * n_bufs * t * row_bytes <= budget:
            best = t
    assert best is not None, (rows, row_bytes)
    return best


def _add_halves(full, recv, core, out_dtype, name):
    n, h, cols = recv.shape
    tr = _row_tile(h, cols * 4, 3)
    nt = h // tr

    def body(core_ref, a_ref, b_ref, o_ref):
        o_ref[...] = (a_ref[...].astype(F32) + b_ref[...].astype(F32)).astype(out_dtype)

    blk = (None, tr, cols)
    return pl.pallas_call(
        body, name=name,
        grid_spec=pltpu.PrefetchScalarGridSpec(
            num_scalar_prefetch=1, grid=(n, nt),
            in_specs=[pl.BlockSpec(blk, lambda j, i, core_ref: (j, core_ref[0] * nt + i, 0)),
                      pl.BlockSpec(blk, lambda j, i, core_ref: (j, i, 0))],
            out_specs=pl.BlockSpec(blk, lambda j, i, core_ref: (j, i, 0))),
        out_shape=jax.ShapeDtypeStruct((n, h, cols), out_dtype),
        compiler_params=_params(("parallel", "parallel")),
    )(core.reshape(1), full, recv)


def _add2(a, b, name):
    r, cols = a.shape
    tr = _row_tile(r, cols * 4, 3)

    def body(a_ref, b_ref, o_ref):
        o_ref[...] = a_ref[...] + b_ref[...]

    blk = pl.BlockSpec((tr, cols), lambda i: (i, 0))
    return pl.pallas_call(body, name=name, grid=(r // tr,), in_specs=[blk, blk], out_specs=blk,
                          out_shape=jax.ShapeDtypeStruct(a.shape, a.dtype), compiler_params=_params(("parallel",)))(a, b)


def _sum_slots(parts, own, chip, name):
    n, h, cols = parts.shape
    tr = _row_tile(h, cols * 4, n + 2)

    def body(chip_ref, p_ref, own_ref, o_ref):
        acc = None
        for k in range(n):
            term = jnp.where(chip_ref[0] == k, own_ref[...], p_ref[k]).astype(F32)
            acc = term if acc is None else acc + term
        o_ref[...] = acc

    own_spec = (pl.BlockSpec((None, tr, cols), lambda i, chip_ref: (chip_ref[0], i, 0)) if own.ndim == 3
                else pl.BlockSpec((tr, cols), lambda i, chip_ref: (i, 0)))
    return pl.pallas_call(
        body, name=name,
        grid_spec=pltpu.PrefetchScalarGridSpec(
            num_scalar_prefetch=1, grid=(h // tr,),
            in_specs=[pl.BlockSpec((n, tr, cols), lambda i, chip_ref: (0, i, 0)), own_spec],
            out_specs=pl.BlockSpec((tr, cols), lambda i, chip_ref: (i, 0))),
        out_shape=jax.ShapeDtypeStruct((h, cols), F32), compiler_params=_params(("parallel",)),
    )(chip.reshape(1), parts, own)


def _adamw(w, g, m, v, name):
    shape = w.shape
    cols = shape[-1]
    rows = math.prod(shape[:-1])
    tr = rows
    if rows % 8 == 0:
        tr = 8
        for t in range(8, rows + 1, 8):
            if rows % t == 0 and t * cols * 4 <= 1536 * 1024:
                tr = t
    c1 = 1.0 - ADAM_B1 ** ADAM_STEP
    c2 = 1.0 - ADAM_B2 ** ADAM_STEP

    def body(w_ref, g_ref, m_ref, v_ref, d_ref, nm_ref, nv_ref):
        gg = g_ref[...]
        nm = ADAM_B1 * m_ref[...] + (1.0 - ADAM_B1) * gg
        nv = ADAM_B2 * v_ref[...] + (1.0 - ADAM_B2) * (gg * gg)
        d_ref[...] = -ADAM_LR * ((nm / c1) / (jnp.sqrt(nv / c2) + ADAM_EPS) + ADAM_WD * w_ref[...])
        nm_ref[...] = nm
        nv_ref[...] = nv

    blk = pl.BlockSpec((tr, cols), lambda i: (i, 0))
    flat = jax.ShapeDtypeStruct((rows, cols), F32)
    outs = pl.pallas_call(
        body, name=name, grid=(rows // tr,), in_specs=[blk] * 4, out_specs=[blk] * 3, out_shape=[flat] * 3,
        compiler_params=_params(("parallel",)),
    )(*[a.reshape(rows, cols) for a in (w, g, m, v)])
    return [o.reshape(shape) for o in outs]


def _adamw_update(w, g, m, v):
    c1 = 1.0 - ADAM_B1 ** ADAM_STEP
    c2 = 1.0 - ADAM_B2 ** ADAM_STEP
    nm = ADAM_B1 * m + (1.0 - ADAM_B1) * g
    nv = ADAM_B2 * v + (1.0 - ADAM_B2) * (g * g)
    return -ADAM_LR * ((nm / c1) / (jnp.sqrt(nv / c2) + ADAM_EPS) + ADAM_WD * w), nm, nv


def _adamw_projection(w, m, v, mine, theirs, core, name):
    _, r, cols = w.shape
    h = r // 2
    tr = _row_tile(h, cols * 4, 12, budget=40 * 1024 * 1024)
    nt = h // tr

    def body(core_ref, w_ref, m_ref, v_ref, a0, b0, a1, b1, g_ref, d_ref, nm_ref, nv_ref):
        layer, half = pl.program_id(0), pl.program_id(1)
        own = half == core_ref[0]
        g = jnp.where(layer == 0, jnp.where(own, a0[...], b0[...]), jnp.where(own, a1[...], b1[...]))
        g_ref[...] = g
        d_ref[...], nm_ref[...], nv_ref[...] = _adamw_update(w_ref[...], g, m_ref[...], v_ref[...])

    def piece(layer, own):
        def index(l, hh, i, core_ref):
            used = (l == layer) & ((hh == core_ref[0]) == own)
            return (jnp.where(used, i, 0), 0)
        return pl.BlockSpec((tr, cols), index)

    whole = pl.BlockSpec((None, tr, cols), lambda l, hh, i, core_ref: (l, hh * nt + i, 0))
    return pl.pallas_call(
        body, name=name,
        grid_spec=pltpu.PrefetchScalarGridSpec(
            num_scalar_prefetch=1, grid=(2, 2, nt),
            in_specs=[whole] * 3 + [piece(0, True), piece(0, False), piece(1, True), piece(1, False)],
            out_specs=[whole] * 4),
        out_shape=[jax.ShapeDtypeStruct(w.shape, F32)] * 4,
        compiler_params=_params(("arbitrary", "arbitrary", "arbitrary")),
    )(core.reshape(1), w, m, v, mine[0], theirs[0], mine[1], theirs[1])


def _chip_shape(name):
    shape = list(FULL_SHAPES[name])
    axis = BIG_SHARD_AXIS.get(name, SMALL_SHARD_AXIS.get(name))
    if axis is not None:
        shape[axis] //= N_CHIPS
    return tuple(shape)


def _round_up(n, k):
    return -(-n // k) * k


SHARDED_SMALL_WORDS = sum(math.prod(_chip_shape(n)) for n in SMALL_SHARDED)
SHARDED_SMALL_ROWS = _round_up(-(-SHARDED_SMALL_WORDS // PACK_COLS), 32)
SMALL_MATRICES = ['even_a_ws', 'odd_w_a', 'odd_w_x', 'odd_w_pool']
SMALL_VECTORS = [n for n in SMALL_NAMES if n not in SMALL_MATRICES]
VECTOR_ROWS = _round_up(-(-(sum(math.prod(FULL_SHAPES[n]) for n in SMALL_VECTORS) + 1) // PACK_COLS), 32)


def _pack_rows(flat, rows):
    return jnp.pad(flat, (0, rows * PACK_COLS - flat.shape[0])).reshape(rows, PACK_COLS)


def _pack_small_sharded(shards):
    return _pack_rows(jnp.concatenate([shards[n].reshape(-1) for n in SMALL_SHARDED]), SHARDED_SMALL_ROWS)


def _unpack_small_sharded(gathered_small):
    out = {}
    words = gathered_small.reshape(N_CHIPS, -1)
    w0 = 0
    for n in SMALL_SHARDED:
        size = math.prod(_chip_shape(n))
        blocks = words[:, w0:w0 + size].reshape((N_CHIPS,) + _chip_shape(n))
        out[n] = jnp.concatenate([blocks[j] for j in range(N_CHIPS)], axis=SMALL_SHARD_AXIS[n])
        w0 += size
    return out


def _small_grad_parts(grads, loss):
    words = [grads[n].reshape(-1) for n in SMALL_VECTORS] + [loss.reshape(-1)]
    return ([_pack_rows(jnp.concatenate(words), VECTOR_ROWS)]
            + [grads[n].reshape(-1, FULL_SHAPES[n][-1]) for n in SMALL_MATRICES])


def _own_block(name, full, chip):
    if name not in SMALL_SHARD_AXIS:
        return full
    axis = SMALL_SHARD_AXIS[name]
    width = FULL_SHAPES[name][axis] // N_CHIPS
    return lax.dynamic_slice_in_dim(full, chip * width, width, axis)


def _from_small_grad_parts(totals, chip):
    flat = totals[0].reshape(-1)
    out = {}
    w0 = 0
    for n in SMALL_VECTORS:
        size = math.prod(FULL_SHAPES[n])
        out[n] = _own_block(n, flat[w0:w0 + size].reshape(FULL_SHAPES[n]), chip)
        w0 += size
    for n, total in zip(SMALL_MATRICES, totals[1:]):
        out[n] = _own_block(n, total.reshape(FULL_SHAPES[n]), chip)
    return out, flat[w0]


def kernel(x, positions, even_w_in, even_a_ln_g, even_a_ln_b, even_a_ws, even_a_bs, even_b_sinks, even_w_out, even_ln_g, even_ln_b, odd_w_in, odd_conv_w, odd_conv_b, odd_w_a, odd_b_a, odd_w_x, odd_b_x, odd_lam, odd_w_pool, odd_d_scale, odd_w_out, odd_ln_g, odd_ln_b, loss_target, m_even_w_in, m_even_a_ln_g, m_even_a_ln_b, m_even_a_ws, m_even_a_bs, m_even_b_sinks, m_even_w_out, m_even_ln_g, m_even_ln_b, m_odd_w_in, m_odd_conv_w, m_odd_conv_b, m_odd_w_a, m_odd_b_a, m_odd_w_x, m_odd_b_x, m_odd_lam, m_odd_w_pool, m_odd_d_scale, m_odd_w_out, m_odd_ln_g, m_odd_ln_b, v_even_w_in, v_even_a_ln_g, v_even_a_ln_b, v_even_a_ws, v_even_a_bs, v_even_b_sinks, v_even_w_out, v_even_ln_g, v_even_ln_b, v_odd_w_in, v_odd_conv_w, v_odd_conv_b, v_odd_w_a, v_odd_b_a, v_odd_w_x, v_odd_b_x, v_odd_lam, v_odd_w_pool, v_odd_d_scale, v_odd_w_out, v_odd_ln_g, v_odd_ln_b):
    local = dict(zip(WEIGHT_NAMES, (even_w_in, even_a_ln_g, even_a_ln_b, even_a_ws, even_a_bs, even_b_sinks, even_w_out, even_ln_g, even_ln_b, odd_w_in, odd_conv_w, odd_conv_b, odd_w_a, odd_b_a, odd_w_x, odd_b_x, odd_lam, odd_w_pool, odd_d_scale, odd_w_out, odd_ln_g, odd_ln_b)))
    mom = dict(zip(WEIGHT_NAMES, (m_even_w_in, m_even_a_ln_g, m_even_a_ln_b, m_even_a_ws, m_even_a_bs, m_even_b_sinks, m_even_w_out, m_even_ln_g, m_even_ln_b, m_odd_w_in, m_odd_conv_w, m_odd_conv_b, m_odd_w_a, m_odd_b_a, m_odd_w_x, m_odd_b_x, m_odd_lam, m_odd_w_pool, m_odd_d_scale, m_odd_w_out, m_odd_ln_g, m_odd_ln_b)))
    vel = dict(zip(WEIGHT_NAMES, (v_even_w_in, v_even_a_ln_g, v_even_a_ln_b, v_even_a_ws, v_even_a_bs, v_even_b_sinks, v_even_w_out, v_even_ln_g, v_even_ln_b, v_odd_w_in, v_odd_conv_w, v_odd_conv_b, v_odd_w_a, v_odd_b_a, v_odd_w_x, v_odd_b_x, v_odd_lam, v_odd_w_pool, v_odd_d_scale, v_odd_w_out, v_odd_ln_g, v_odd_ln_b)))
    mx, my, core = _where_am_i()
    chip = 2 * mx + my

    stored = lambda n, a: jnp.swapaxes(a, 1, 2) if n == 'even_w_in' and W_IN_TRANSPOSED['even'] else a
    names_of = lambda layer: [('even' if layer % 2 == 0 else 'odd') + s for s in ('_w_in', '_w_out')]

    shards = [[stored(n, local[n])[layer // 2].astype(MXU_DTYPE) for n in names_of(layer)] for layer in range(DEPTH)]
    (first_in, first_out), first_token = _ici_start(
        'gather', [[_pack_small_sharded(local), shards[0][0]], [shards[0][1]]], "gather_start_first")
    later, gather_token = _ici_start('gather', shards[1:], "gather_start", after=[first_token])
    gathers = [None] + later

    own, lands = _ici_wait('gather', first_in, [first_token, gather_token], "gather_wait_0")
    small_sharded, first_w_in = _with_own_slot(_forward_to_sibling(lands, "gather_forward_0"), own, chip)
    weights = dict(local)
    weights.update(_unpack_small_sharded(small_sharded))
    passing = {}

    def passed_on(started, after, name):
        own, lands = _ici_wait('gather', started, [after], "gather_wait_" + name)
        (forward,), token = _ici_start('forward', [lands], "forward_start_" + name)

        def finish(then):
            done, _ = _ici_wait('forward', forward, [then], "forward_wait_" + name)
            return _with_own_slot(done, own, chip)

        return token, finish

    def first_w_out(h):
        token, finish = passed_on(first_out, h, "0_out")
        return token, lambda mix: finish(mix)[0]

    def projections_ahead(layer, after):
        token, passing[layer] = passed_on(gathers[layer], after, str(layer))
        return token

    def projections_of(layer, after):
        return (first_w_in, first_w_out) if layer == 0 else passing.pop(layer)(after)

    swapping, in_flight, mine, started = [], [], {}, {}

    def land(*after):
        layer, scatter = in_flight.pop()
        parts, lands = _ici_wait('scatter', scatter, after, "scatter_wait_%d" % layer)
        for n, arrived, part in zip(names_of(layer), lands, parts):
            mine[n, layer // 2] = _sum_slots(arrived, part, chip, "big_sum_%s_%d" % (n, layer // 2))

    odd_pieces = [(n, l) for n in ('odd_w_in', 'odd_w_out') for l in range(2)]
    joining = []

    def projection_grads(layer, dw_in, dw_out):
        behind = []
        if in_flight:
            land(dw_in)
        if layer == 0:
            (join,), token = _ici_start('join', [[mine[p] for p in odd_pieces]], "join_odd_start")
            joining.append(join)
            behind = [token]
        (swap,), token = _ici_start('swap', [[dw_in, dw_out]], "presum_start_%d" % layer, after=behind)
        swapping.append((layer, swap))
        return presum_and_scatter(token) if layer == 0 else token

    def presum_and_scatter(*after):
        layer, swap = swapping.pop()
        partial, from_sibling = _ici_wait('swap', swap, after, "presum_wait_%d" % layer)
        chip_part = [_add_halves(g, r, core, MXU_DTYPE, "big_presum_add_%s_%d" % (n, layer // 2))
                     for g, r, n in zip(partial, from_sibling, names_of(layer))]
        (scatter,), started[layer] = _ici_start('scatter', [chip_part], "scatter_start_%d" % layer)
        in_flight.append((layer, scatter))
        return started[layer]

    norm_grads_done = lambda layer, dz: presum_and_scatter(dz) if swapping else dz
    loss, grad_x, grads = _local_step(x[0], positions[0], loss_target[0], weights, projections_of, projections_ahead,
                                      projection_grads, norm_grads_done)

    small = _small_grad_parts(grads, loss)
    chip_sums = [_add2(a, b, "small_presum_add_%d" % i)
                 for i, (a, b) in enumerate(zip(small, _sibling_swap(small, "small_swap", after=[started[0]])))]
    (small_gather,), small_token = _ici_start('gather', [chip_sums], "small_gather_start")

    reduced, deltas, new_m, new_v = {}, {}, {}, {}

    def adamw_projections(kind, after):
        names = [kind + '_w_in', kind + '_w_out']
        pieces = [(n, l) for n in names for l in range(2)]
        if kind == 'odd':
            halves, theirs = _ici_wait('join', joining.pop(), after, "join_odd_wait")
            mine.update(zip(pieces, halves))
        else:
            theirs = _sibling_swap([mine[p] for p in pieces], "big_join_" + kind, after=after)
        theirs = dict(zip(pieces, theirs))
        done = []
        for n in names:
            outs = _adamw_projection(stored(n, local[n]), stored(n, mom[n]), stored(n, vel[n]),
                                     [mine[n, l] for l in range(2)], [theirs[n, l] for l in range(2)], core,
                                     "adamw_" + n)
            reduced[n], deltas[n], new_m[n], new_v[n] = [stored(n, o) for o in outs]
            done.append(outs[-1])
        return done

    done = adamw_projections('odd', [small_token])
    land(grad_x, *done)
    done = adamw_projections('even', [])

    own, lands = _ici_wait('gather', small_gather, done, "small_gather_wait")
    of_chips = _forward_to_sibling(lands, "small_gather_forward")
    small_totals, total_loss = _from_small_grad_parts(
        [_sum_slots(g, mine_i, chip, "small_sum_%d" % i) for i, (g, mine_i) in enumerate(zip(of_chips, own))], chip)
    reduced.update(small_totals)
    for n in SMALL_NAMES:
        deltas[n], new_m[n], new_v[n] = _adamw(local[n], reduced[n], mom[n], vel[n], "adamw_" + n)

    return (total_loss, grad_x[None], *[reduced[n] for n in WEIGHT_NAMES], *[deltas[n] for n in WEIGHT_NAMES],
            *[new_m[n] for n in WEIGHT_NAMES], *[new_v[n] for n in WEIGHT_NAMES])
```

```python
import math

import jax
import jax.numpy as jnp
from jax import lax
from jax.experimental import pallas as pl
from jax.experimental.pallas import tpu as pltpu

F32 = jnp.float32
MXU_DTYPE = jnp.bfloat16

DEPTH = 4
A_WIDTH = 1024
A_GROUPS = 8
CHUNK = 128
B_HEAD_DIM = 64
B_Q_HEADS = 16
B_KV_HEADS = 2
WINDOW = 128
ROT_DIM = 16
ROPE_THETA = 500000.0
C_WIDTH = 1024
C_HEADS = 8
CONV_WIDTH = 4
LRU_C = 8.0
D_WIDTH = 1024
D_GROUPS = 4
D_GROUP_DIM = 256
EVEN_IN = 5376
DN_ALPHA = (2 * DEPTH) ** 0.25
LN_EPS = 1e-5
ATTN_SCALE = B_HEAD_DIM ** -0.5
NEG_BIG = -1e30

ADAM_LR = 0.001
ADAM_B1 = 0.9
ADAM_B2 = 0.999
ADAM_EPS = 1e-08
ADAM_WD = 0.01
ADAM_STEP = 10

LANES = 128
VMEM_LIMIT = 56 * 1024 * 1024
N_CHIPS = 4
MESH = pl.DeviceIdType.MESH

WEIGHT_NAMES = ['even_w_in', 'even_a_ln_g', 'even_a_ln_b', 'even_a_ws', 'even_a_bs', 'even_b_sinks', 'even_w_out',
                'even_ln_g', 'even_ln_b', 'odd_w_in', 'odd_conv_w', 'odd_conv_b', 'odd_w_a', 'odd_b_a', 'odd_w_x',
                'odd_b_x', 'odd_lam', 'odd_w_pool', 'odd_d_scale', 'odd_w_out', 'odd_ln_g', 'odd_ln_b']
BIG_SHARD_AXIS = {'even_w_in': 2, 'even_w_out': 1, 'odd_w_in': 2, 'odd_w_out': 1}
SMALL_SHARD_AXIS = {'odd_conv_w': 2, 'odd_conv_b': 1, 'odd_b_a': 1, 'odd_b_x': 1, 'odd_lam': 1, 'odd_w_pool': 2,
                    'odd_d_scale': 1, 'odd_ln_g': 1, 'odd_ln_b': 1}
FULL_SHAPES = {
    'even_w_in': (2, 2048, 5376), 'even_a_ln_g': (2, 1024), 'even_a_ln_b': (2, 1024), 'even_a_ws': (2, 8, 128, 128),
    'even_a_bs': (2, 8, 128), 'even_b_sinks': (2, 16), 'even_w_out': (2, 2048, 2048), 'even_ln_g': (2, 2048),
    'even_ln_b': (2, 2048), 'odd_w_in': (2, 2048, 4096), 'odd_conv_w': (2, 4, 1024), 'odd_conv_b': (2, 1024),
    'odd_w_a': (2, 8, 128, 128), 'odd_b_a': (2, 1024), 'odd_w_x': (2, 8, 128, 128), 'odd_b_x': (2, 1024),
    'odd_lam': (2, 1024), 'odd_w_pool': (2, 4, 256, 256), 'odd_d_scale': (2, 1024), 'odd_w_out': (2, 2048, 2048),
    'odd_ln_g': (2, 2048), 'odd_ln_b': (2, 2048)}
BIG_NAMES = ['even_w_in', 'even_w_out', 'odd_w_in', 'odd_w_out']
SMALL_SHARDED = ['odd_conv_w', 'odd_conv_b', 'odd_b_a', 'odd_b_x', 'odd_lam', 'odd_d_scale', 'odd_ln_g', 'odd_ln_b',
                 'odd_w_pool']
SMALL_NAMES = [n for n in WEIGHT_NAMES if n not in BIG_NAMES]
W_IN_TRANSPOSED = {'even': True, 'odd': False}
PACK_COLS = 1024


def _params(sem):
    return pltpu.CompilerParams(dimension_semantics=sem, vmem_limit_bytes=VMEM_LIMIT)


def _dot(a, b):
    return lax.dot_general(a, b, (((1,), (0,)), ((), ())), preferred_element_type=F32)


def _dot_nt(a, b):
    return lax.dot_general(a, b, (((1,), (1,)), ((), ())), preferred_element_type=F32)


def _dot_tn(a, b):
    return lax.dot_general(a, b, (((0,), (0,)), ((), ())), preferred_element_type=F32)


def _sigmoid(x):
    return 0.5 * jnp.tanh(0.5 * x) + 0.5


def _silu_and_grad(x):
    s = _sigmoid(x)
    return x * s, s * (1.0 + x * (1.0 - s))


def _mm_call(body, name, grid, in_specs, out_spec, out_shape, args):
    return pl.pallas_call(body, name=name, grid=grid, in_specs=in_specs, out_specs=out_spec, out_shape=out_shape,
                          compiler_params=_params(("parallel",) * len(grid)))(*args)


def _resident(block, index_map):
    return pl.BlockSpec(block, index_map, pipeline_mode=pl.Buffered(1))


ROWS_OF_WT = 768


def _mm_in(x, wg, transposed, name):
    s, k = x.shape
    x_spec = _resident((s, k), lambda j: (0, 0))

    def body(x_ref, w_ref, o_ref):
        o_ref[...] = (_dot_nt if transposed else _dot)(x_ref[...], w_ref[...])

    if transposed:
        wt = wg.reshape(-1, k)
        n, tn = wt.shape[0], ROWS_OF_WT
        return _mm_call(body, name, (n // tn,), [x_spec, pl.BlockSpec((tn, k), lambda j: (j, 0))],
                        pl.BlockSpec((s, tn), lambda j: (0, j)), jax.ShapeDtypeStruct((s, n), F32), (x, wt))
    nc = wg.shape[2]
    return _mm_call(body, name, (N_CHIPS,), [x_spec, pl.BlockSpec((None, k, nc), lambda j: (j, 0, 0))],
                    pl.BlockSpec((None, s, nc), lambda j: (j, 0, 0)),
                    jax.ShapeDtypeStruct((N_CHIPS, s, nc), F32), (x, wg))


def _mm_dx(dh, wg, dz, transposed, after, name):
    s, d = dz.shape
    tm, tn = min(1024, s), 512
    res = pl.BlockSpec((tm, tn), lambda i, j: (i, j))
    if transposed:
        wt = wg.reshape(-1, d)
        n = wt.shape[0]

        def body(a_ref, w_ref, r_ref, after_ref, o_ref):
            o_ref[...] = DN_ALPHA * r_ref[...] + _dot(a_ref[...], w_ref[...])

        return _mm_call(body, name, (s // tm, d // tn),
                        [pl.BlockSpec((tm, n), lambda i, j: (i, 0)), pl.BlockSpec((n, tn), lambda i, j: (0, j)), res,
                         ANY], res, jax.ShapeDtypeStruct((s, d), F32), (dh, wt, dz, after))
    nc = dh.shape[2]

    def body(a_ref, w_ref, r_ref, after_ref, o_ref):
        acc = DN_ALPHA * r_ref[...]
        for j in range(N_CHIPS):
            acc = acc + _dot_nt(a_ref[j], w_ref[j])
        o_ref[...] = acc

    return _mm_call(body, name, (s // tm, d // tn),
                    [pl.BlockSpec((N_CHIPS, tm, nc), lambda i, j: (0, i, 0)),
                     pl.BlockSpec((N_CHIPS, tn, nc), lambda i, j: (0, j, 0)), res, ANY],
                    res, jax.ShapeDtypeStruct((s, d), F32), (dh, wg, dz, after))


def _mm_dw_in(x, dh, transposed, name):
    s, d = x.shape
    x_spec = _resident((s, d), lambda j: (0, 0))

    def body(a_ref, b_ref, o_ref):
        o_ref[...] = _dot_tn(a_ref[...], b_ref[...]).astype(o_ref.dtype)

    if transposed:
        n, tn = dh.shape[1], ROWS_OF_WT
        flat = _mm_call(body, name, (n // tn,), [pl.BlockSpec((s, tn), lambda j: (0, j)), x_spec],
                        pl.BlockSpec((tn, d), lambda j: (j, 0)), jax.ShapeDtypeStruct((n, d), MXU_DTYPE), (dh, x))
        return flat.reshape(N_CHIPS, n // N_CHIPS, d)
    nc = dh.shape[2]
    return _mm_call(body, name, (N_CHIPS,), [x_spec, pl.BlockSpec((None, s, nc), lambda j: (j, 0, 0))],
                    pl.BlockSpec((None, d, nc), lambda j: (j, 0, 0)),
                    jax.ShapeDtypeStruct((N_CHIPS, d, nc), MXU_DTYPE), (x, dh))


def _mm_out_norm(mix, wg, x, g, b, name, target=None, after=None):
    s, k = mix.shape
    _, kc, d = wg.shape
    tm = min(512, s)

    def normed(a_ref, w_ref, x_ref, g_ref, b_ref, z_ref):
        z = DN_ALPHA * x_ref[...] + _dot(a_ref[...], w_ref[...].reshape(N_CHIPS * kc, d))
        mu = jnp.mean(z, axis=1, keepdims=True)
        zc = z - mu
        var = jnp.mean(zc * zc, axis=1, keepdims=True)
        z_ref[...] = z
        return zc * lax.rsqrt(var + LN_EPS) * g_ref[...] + b_ref[...]

    def body(a_ref, w_ref, x_ref, g_ref, b_ref, after_ref, z_ref, o_ref, ob_ref):
        o = normed(a_ref, w_ref, x_ref, g_ref, b_ref, z_ref)
        o_ref[...] = o
        ob_ref[...] = o.astype(ob_ref.dtype)

    def body_with_loss(a_ref, w_ref, x_ref, g_ref, b_ref, t_ref, z_ref, do_ref, l_ref):
        @pl.when(pl.program_id(0) == 0)
        def _():
            l_ref[...] = jnp.zeros_like(l_ref)

        e = normed(a_ref, w_ref, x_ref, g_ref, b_ref, z_ref) - t_ref[...]
        do_ref[...] = e * (1.0 / d)
        l_ref[...] += (0.5 / d) * jnp.sum(jnp.sum(e * e, axis=1, keepdims=True), axis=0, keepdims=True)

    row = pl.BlockSpec((tm, d), lambda i: (i, 0))
    vec = pl.BlockSpec((1, d), lambda i: (0, 0))
    in_specs = [pl.BlockSpec((tm, k), lambda i: (i, 0)), _resident((N_CHIPS, kc, d), lambda i: (0, 0, 0)), row, vec,
                vec]
    args = (mix, wg, x, g.reshape(1, d), b.reshape(1, d))
    f32_rows = jax.ShapeDtypeStruct((s, d), F32)
    if target is None:
        return pl.pallas_call(
            body, name=name, grid=(s // tm,), in_specs=in_specs + [ANY], out_specs=[row, row, row],
            out_shape=[f32_rows, f32_rows, jax.ShapeDtypeStruct((s, d), MXU_DTYPE)],
            compiler_params=_params(("parallel",)))(*args, after)
    return pl.pallas_call(
        body_with_loss, name=name, grid=(s // tm,), in_specs=in_specs + [row],
        out_specs=[row, row, pl.BlockSpec((1, 1), lambda i: (0, 0))],
        out_shape=[f32_rows, f32_rows, jax.ShapeDtypeStruct((1, 1), F32)],
        compiler_params=_params(("arbitrary",)))(*args, target)


NORM_BWD_ROWS = 128


def _norm_out_bwd(dout, z, g, wg, mix, after, name):
    s, d = z.shape
    kc = wg.shape[1]
    tr = min(NORM_BWD_ROWS, s)
    nr = s // tr

    def body(do_ref, z_ref, g_ref, w_ref, mix_ref, after_ref, dz_ref, dg_ref, db_ref, dmix_ref, dw_ref, dzb_scr):
        i = pl.program_id(0)

        @pl.when(i == 0)
        def _():
            dg_ref[...] = jnp.zeros_like(dg_ref)
            db_ref[...] = jnp.zeros_like(db_ref)

        @pl.when(i < nr)
        def _():
            zz = z_ref[...]
            do = do_ref[...]
            mu = jnp.mean(zz, axis=1, keepdims=True)
            zc = zz - mu
            var = jnp.mean(zc * zc, axis=1, keepdims=True)
            rstd = lax.rsqrt(var + LN_EPS)
            zh = zc * rstd
            dg_ref[...] += jnp.sum(do * zh, axis=0, keepdims=True)
            db_ref[...] += jnp.sum(do, axis=0, keepdims=True)
            dzh = do * g_ref[...]
            dz = rstd * (dzh - jnp.mean(dzh, axis=1, keepdims=True) - zh * jnp.mean(dzh * zh, axis=1, keepdims=True))
            dz_ref[...] = dz
            dzb_scr[pl.ds(pl.multiple_of(i * tr, tr), tr), :] = dz.astype(dzb_scr.dtype)

        @pl.when(i >= nr)
        def _():
            dzb = dzb_scr[...]
            dmix_ref[...] = _dot_nt(dzb, w_ref[...])
            dw_ref[...] = _dot_tn(mix_ref[...], dzb).astype(dw_ref.dtype)

    row = pl.BlockSpec((tr, d), lambda i: (jnp.minimum(i, nr - 1), 0))
    vec = pl.BlockSpec((1, d), lambda i: (0, 0))
    shard = lambda i: jnp.maximum(i - nr, 0)
    return pl.pallas_call(
        body, name=name, grid=(nr + N_CHIPS,),
        in_specs=[row, row, vec, pl.BlockSpec((None, kc, d), lambda i: (shard(i), 0, 0)),
                  pl.BlockSpec((s, kc), lambda i: (0, shard(i))), ANY],
        out_specs=[row, vec, vec, pl.BlockSpec((s, kc), lambda i: (0, shard(i))),
                   pl.BlockSpec((None, kc, d), lambda i: (shard(i), 0, 0))],
        out_shape=[jax.ShapeDtypeStruct((s, d), F32), jax.ShapeDtypeStruct((1, d), F32),
                   jax.ShapeDtypeStruct((1, d), F32), jax.ShapeDtypeStruct((s, N_CHIPS * kc), F32),
                   jax.ShapeDtypeStruct((N_CHIPS, kc, d), MXU_DTYPE)],
        scratch_shapes=[pltpu.VMEM((s, d), MXU_DTYPE)],
        compiler_params=_params(("arbitrary",)),
    )(dout, z, g.reshape(1, d), wg, mix, after)


Q0, K0, V0, BG0 = 3072, 4096, 4224, 4352
T = CHUNK


def _lane_ids(width):
    return lax.broadcasted_iota(jnp.int32, (T, width), 1)


def _rope_swap(x):
    w = x.shape[1]
    l64 = _lane_ids(w) % B_HEAD_DIM
    half = ROT_DIM // 2
    return jnp.where(l64 < half, pltpu.roll(x, w - half, 1), jnp.where(l64 < ROT_DIM, pltpu.roll(x, half, 1), 0.0))


def _rope(x, c, s):
    return x * c + _rope_swap(x) * s


def _rope_bwd(d, c, s):
    return d * c + _rope_swap(d * s)


def _tril():
    return lax.broadcasted_iota(jnp.int32, (T, T), 0) >= lax.broadcasted_iota(jnp.int32, (T, T), 1)


def _band_mask(has_prev):
    qi = lax.broadcasted_iota(jnp.int32, (T, 2 * T), 0)
    kj = lax.broadcasted_iota(jnp.int32, (T, 2 * T), 1)
    return (kj > qi) & (kj <= qi + WINDOW) & ((kj >= WINDOW) | has_prev)


def _a_norm(v, g, b):
    mu = jnp.mean(v, axis=1, keepdims=True)
    vc = v - mu
    var = jnp.mean(vc * vc, axis=1, keepdims=True)
    rstd = lax.rsqrt(var + LN_EPS)
    vh = vc * rstd
    return vh, rstd, vh * g + b


def _kv_operands(hk, k_rot, kp_rot, vv, vp):
    lo = _lane_ids(LANES) < B_HEAD_DIM
    mine = lo if hk == 0 else jnp.logical_not(lo)
    both = lambda t: jnp.where(mine, t, 0.0) + pltpu.roll(jnp.where(mine, t, 0.0), B_HEAD_DIM, 1)
    k2 = jnp.concatenate([both(kp_rot), both(k_rot)], axis=0)
    v2 = jnp.concatenate([both(vp), both(vv)], axis=0)
    return k2.astype(MXU_DTYPE), v2.astype(MXU_DTYPE)


GROUP_HEADS = B_Q_HEADS // B_KV_HEADS


def _pair_columns(hk):
    return [(hk * GROUP_HEADS // 2 + pp) * LANES for pp in range(GROUP_HEADS // 2)]


def _stacked_heads(pairs):
    lo = _lane_ids(LANES) < B_HEAD_DIM
    rows = []
    for tile in pairs:
        rows += [jnp.where(lo, tile, 0.0), jnp.where(lo, 0.0, tile)]
    return jnp.concatenate(rows, axis=0).astype(MXU_DTYPE)


def _pair_of(stacked, pp):
    lo = _lane_ids(LANES) < B_HEAD_DIM
    return jnp.where(lo, stacked[2 * pp * T:(2 * pp + 1) * T], stacked[(2 * pp + 1) * T:(2 * pp + 2) * T])


def _softmax_with_sink(qm, k2, valid, sink_ref, head0):
    scores = (_dot_nt(qm, k2) * ATTN_SCALE).reshape(GROUP_HEADS, T, 2 * T)
    s = jnp.where(valid[None], scores, NEG_BIG).reshape(GROUP_HEADS * T, 2 * T)
    sink = jnp.concatenate([jnp.broadcast_to(sink_ref[:, head0 + h:head0 + h + 1], (T, 1))
                            for h in range(GROUP_HEADS)], axis=0)
    m = jnp.maximum(jnp.max(s, axis=1, keepdims=True), sink)
    ex = jnp.exp(s - m)
    es = jnp.exp(sink - m)
    inv = 1.0 / (jnp.sum(ex, axis=1, keepdims=True) + es)
    return ex * inv, es * inv


def _even_specs(s):
    nb = s // T
    prev = lambda r: lambda i: jnp.maximum(r(i) - 1, 0)
    return nb, {
        'h': lambda r: pl.BlockSpec((T, EVEN_IN), lambda i: (r(i), 0)),
        'kprev': lambda r: pl.BlockSpec((T, LANES), lambda i: (prev(r)(i), K0 // LANES)),
        'vprev': lambda r: pl.BlockSpec((T, LANES), lambda i: (prev(r)(i), V0 // LANES)),
        'tab': lambda r: pl.BlockSpec((T, LANES), lambda i: (r(i), 0)),
        'tabprev': lambda r: pl.BlockSpec((T, LANES), lambda i: (prev(r)(i), 0)),
    }


def _full(shape):
    nd = len(shape)
    return pl.BlockSpec(shape, lambda i: (0,) * nd)


def _even_fwd(h, cq, sq, ln_g, ln_b, ws, bs_t, sinks, after):
    s = h.shape[0]
    nb, sp = _even_specs(s)

    def body(h_ref, kp_ref, vp_ref, cq_ref, sq_ref, ckp_ref, skp_ref, g_ref, b_ref, ws_ref, bst_ref, sink_ref,
             after_ref, mix_ref):
        i = pl.program_id(0)
        tril = _tril()
        _, _, vn = _a_norm(h_ref[:, 1024:2048], g_ref[...], b_ref[...])
        vnb = vn.astype(MXU_DTYPE)
        for g in range(A_GROUPS):
            c0, c1 = g * LANES, (g + 1) * LANES
            wg = jnp.where(tril, ws_ref[g], 0.0).astype(MXU_DTYPE)
            mixed = _dot(wg, vnb[:, c0:c1]) + bst_ref[:, g:g + 1]
            sil, _ = _silu_and_grad(h_ref[:, 2048 + c0:2048 + c1])
            mix_ref[:, c0:c1] = (h_ref[:, c0:c1] * mixed * sil).astype(mix_ref.dtype)

        cq, sq = cq_ref[...], sq_ref[...]
        k_rot = _rope(h_ref[:, K0:K0 + LANES], cq, sq)
        kp_rot = _rope(kp_ref[...], ckp_ref[...], skp_ref[...])
        valid = _band_mask(i > 0)
        for hk in range(B_KV_HEADS):
            k2, v2 = _kv_operands(hk, k_rot, kp_rot, h_ref[:, V0:V0 + LANES], vp_ref[...])
            qm = _stacked_heads([_rope(h_ref[:, Q0 + c0:Q0 + c0 + LANES], cq, sq) for c0 in _pair_columns(hk)])
            p, _ = _softmax_with_sink(qm, k2, valid, sink_ref, GROUP_HEADS * hk)
            out = _dot(p.astype(MXU_DTYPE), v2)
            for pp, c0 in enumerate(_pair_columns(hk)):
                sil, _ = _silu_and_grad(h_ref[:, BG0 + c0:BG0 + c0 + LANES])
                mix_ref[:, A_WIDTH + c0:A_WIDTH + c0 + LANES] = (_pair_of(out, pp) * sil).astype(mix_ref.dtype)

    ident = lambda i: i
    return pl.pallas_call(
        body, name="even_fwd", grid=(nb,),
        in_specs=[sp['h'](ident), sp['kprev'](ident), sp['vprev'](ident), sp['tab'](ident), sp['tab'](ident),
                  sp['tabprev'](ident), sp['tabprev'](ident), _full((1, A_WIDTH)), _full((1, A_WIDTH)),
                  _full((A_GROUPS, T, T)), _full((T, A_GROUPS)), _full((1, B_Q_HEADS)), ANY],
        out_specs=pl.BlockSpec((T, 2 * A_WIDTH), lambda i: (i, 0)),
        out_shape=jax.ShapeDtypeStruct((s, 2 * A_WIDTH), MXU_DTYPE),
        compiler_params=_params(("parallel",)),
    )(h, h, h, cq, sq, cq, sq, ln_g.reshape(1, -1), ln_b.reshape(1, -1), ws, bs_t, sinks.reshape(1, -1), after)


def _even_bwd(h, dmix, cq, sq, ln_g, ln_b, ws, bs_t, sinks):
    s = h.shape[0]
    nb, sp = _even_specs(s)

    def body(h_ref, kp_ref, vp_ref, cq_ref, sq_ref, ckp_ref, skp_ref, dm_ref, g_ref, b_ref, ws_ref, bst_ref, sink_ref,
             dh_ref, dg_ref, db_ref, dws_ref, dbst_ref, dsink_ref, dvn_scr, ck_scr, cv_scr):
        i = pl.program_id(0)

        @pl.when(i == 0)
        def _():
            for ref in (dg_ref, db_ref, dws_ref, dbst_ref, dsink_ref, ck_scr, cv_scr):
                ref[...] = jnp.zeros_like(ref)

        tril = _tril()
        vh, rstd, vn = _a_norm(h_ref[:, 1024:2048], g_ref[...], b_ref[...])
        vnb = vn.astype(MXU_DTYPE)
        for g in range(A_GROUPS):
            c0, c1 = g * LANES, (g + 1) * LANES
            wg = jnp.where(tril, ws_ref[g], 0.0).astype(MXU_DTYPE)
            mixed = _dot(wg, vnb[:, c0:c1]) + bst_ref[:, g:g + 1]
            ag = h_ref[:, 2048 + c0:2048 + c1]
            u = h_ref[:, c0:c1]
            sil, dsil = _silu_and_grad(ag)
            da = dm_ref[:, c0:c1]
            dh_ref[:, c0:c1] = (da * mixed * sil).astype(dh_ref.dtype)
            dh_ref[:, 2048 + c0:2048 + c1] = (da * u * mixed * dsil).astype(dh_ref.dtype)
            dmx = da * u * sil
            dmb = dmx.astype(MXU_DTYPE)
            dvn_scr[:, c0:c1] = _dot_tn(wg, dmb)
            dws_ref[g] += jnp.where(tril, _dot_nt(dmb, vnb[:, c0:c1]), 0.0)
            dbst_ref[:, g:g + 1] += jnp.sum(dmx, axis=1, keepdims=True)
        dvn = dvn_scr[...]
        dg_ref[...] += jnp.sum(dvn * vh, axis=0, keepdims=True)
        db_ref[...] += jnp.sum(dvn, axis=0, keepdims=True)
        dvh = dvn * g_ref[...]
        dv = rstd * (dvh - jnp.mean(dvh, axis=1, keepdims=True) - vh * jnp.mean(dvh * vh, axis=1, keepdims=True))
        dh_ref[:, 1024:2048] = dv.astype(dh_ref.dtype)

        cq, sq = cq_ref[...], sq_ref[...]
        k_rot = _rope(h_ref[:, K0:K0 + LANES], cq, sq)
        kp_rot = _rope(kp_ref[...], ckp_ref[...], skp_ref[...])
        valid = _band_mask(i < nb - 1)
        dk_all = jnp.zeros((2 * T, LANES), F32)
        dv_all = jnp.zeros((2 * T, LANES), F32)
        for hk in range(B_KV_HEADS):
            k2, v2 = _kv_operands(hk, k_rot, kp_rot, h_ref[:, V0:V0 + LANES], vp_ref[...])
            cols = _pair_columns(hk)
            qm = _stacked_heads([_rope(h_ref[:, Q0 + c0:Q0 + c0 + LANES], cq, sq) for c0 in cols])
            p, psink = _softmax_with_sink(qm, k2, valid, sink_ref, GROUP_HEADS * hk)
            pb = p.astype(MXU_DTYPE)
            out = _dot(pb, v2)
            dobs = []
            for pp, c0 in enumerate(cols):
                sil, dsil = _silu_and_grad(h_ref[:, BG0 + c0:BG0 + c0 + LANES])
                dbo = dm_ref[:, A_WIDTH + c0:A_WIDTH + c0 + LANES]
                dh_ref[:, BG0 + c0:BG0 + c0 + LANES] = (dbo * _pair_of(out, pp) * dsil).astype(dh_ref.dtype)
                dobs.append(dbo * sil)
            dobm = _stacked_heads(dobs)
            dp = _dot_nt(dobm, v2)
            rs = jnp.sum(p * dp, axis=1, keepdims=True)
            dsb = (p * (dp - rs) * ATTN_SCALE).astype(MXU_DTYPE)
            to_sink = psink * rs
            for h in range(GROUP_HEADS):
                head = GROUP_HEADS * hk + h
                dsink_ref[:, head:head + 1] += -jnp.sum(to_sink[h * T:(h + 1) * T], axis=0, keepdims=True)
            dq = _dot(dsb, k2)
            for pp, c0 in enumerate(cols):
                dh_ref[:, Q0 + c0:Q0 + c0 + LANES] = _rope_bwd(_pair_of(dq, pp), cq, sq).astype(dh_ref.dtype)
            acc_k = _dot_tn(dsb, qm)
            acc_v = _dot_tn(pb, dobm)
            lo2 = lax.broadcasted_iota(jnp.int32, (2 * T, LANES), 1) < B_HEAD_DIM
            mine = lo2 if hk == 0 else jnp.logical_not(lo2)
            dk_all = dk_all + jnp.where(mine, acc_k + pltpu.roll(acc_k, B_HEAD_DIM, 1), 0.0)
            dv_all = dv_all + jnp.where(mine, acc_v + pltpu.roll(acc_v, B_HEAD_DIM, 1), 0.0)
        dk_rot = dk_all[T:2 * T] + ck_scr[...]
        dvv = dv_all[T:2 * T] + cv_scr[...]
        ck_scr[...] = dk_all[0:T]
        cv_scr[...] = dv_all[0:T]
        dh_ref[:, K0:K0 + LANES] = _rope_bwd(dk_rot, cq, sq).astype(dh_ref.dtype)
        dh_ref[:, V0:V0 + LANES] = dvv.astype(dh_ref.dtype)

    rev = lambda i: nb - 1 - i
    return pl.pallas_call(
        body, name="even_bwd", grid=(nb,),
        in_specs=[sp['h'](rev), sp['kprev'](rev), sp['vprev'](rev), sp['tab'](rev), sp['tab'](rev),
                  sp['tabprev'](rev), sp['tabprev'](rev), pl.BlockSpec((T, 2 * A_WIDTH), lambda i: (rev(i), 0)),
                  _full((1, A_WIDTH)), _full((1, A_WIDTH)), _full((A_GROUPS, T, T)), _full((T, A_GROUPS)),
                  _full((1, B_Q_HEADS))],
        out_specs=[sp['h'](rev), _full((1, A_WIDTH)), _full((1, A_WIDTH)),
                   _full((A_GROUPS, T, T)), _full((T, A_GROUPS)), _full((1, B_Q_HEADS))],
        out_shape=[jax.ShapeDtypeStruct((s, EVEN_IN), MXU_DTYPE), jax.ShapeDtypeStruct((1, A_WIDTH), F32),
                   jax.ShapeDtypeStruct((1, A_WIDTH), F32), jax.ShapeDtypeStruct((A_GROUPS, T, T), F32),
                   jax.ShapeDtypeStruct((T, A_GROUPS), F32), jax.ShapeDtypeStruct((1, B_Q_HEADS), F32)],
        scratch_shapes=[pltpu.VMEM((T, A_WIDTH), F32), pltpu.VMEM((T, LANES), F32), pltpu.VMEM((T, LANES), F32)],
        compiler_params=_params(("arbitrary",)),
    )(h, h, h, cq, sq, cq, sq, dmix, ln_g.reshape(1, -1), ln_b.reshape(1, -1), ws, bs_t, sinks.reshape(1, -1))


def _shift_down(x, d, fill=0.0):
    rows = lax.broadcasted_iota(jnp.int32, x.shape, 0)
    return jnp.where(rows >= d, pltpu.roll(x, d, 0), fill)


def _shift_up(x, d, fill=0.0):
    n = x.shape[0]
    rows = lax.broadcasted_iota(jnp.int32, x.shape, 0)
    return jnp.where(rows < n - d, pltpu.roll(x, n - d, 0), fill)


SCAN_BLOCK = 32
SUBLANES = 8


def _scan_block(a, b, up):
    n = a.shape[0]
    d = 1
    while d < n:
        if d < SUBLANES:
            shift = _shift_up if up else _shift_down
            a, b = a * shift(a, d, 1.0), a * shift(b, d) + b
        elif up:
            a, b = (jnp.concatenate([a[:n - d] * a[d:], a[n - d:]], axis=0),
                    jnp.concatenate([a[:n - d] * b[d:] + b[:n - d], b[n - d:]], axis=0))
        else:
            a, b = (jnp.concatenate([a[:d], a[d:] * a[:n - d]], axis=0),
                    jnp.concatenate([b[:d], a[d:] * b[:n - d] + b[d:]], axis=0))
        d *= 2
    return a, b


def _scan(a, b, up):
    n = a.shape[0]
    size = min(SCAN_BLOCK, n)
    blocks = n // size
    out, carry = [None] * blocks, None
    for k in (reversed(range(blocks)) if up else range(blocks)):
        prod, h = _scan_block(a[k * size:(k + 1) * size], b[k * size:(k + 1) * size], up)
        if carry is not None:
            h = h + prod * carry
        carry = h[0:1] if up else h[size - 1:size]
        out[k] = h
    return jnp.concatenate(out, axis=0)


def _neg_expm1(y):
    t = jnp.tanh(-0.5 * y)
    return 2.0 * t / (1.0 + t)


def _softplus(x):
    return jnp.maximum(x, 0.0) + jnp.log1p(jnp.exp(-jnp.abs(x)))


def _lru_forward(xc, cw, cb, wa, ba, wx, bx, lam):
    shifted = [_shift_down(xc, CONV_WIDTH - 1 - j) if j < CONV_WIDTH - 1 else xc for j in range(CONV_WIDTH)]
    xconv = cb
    for j in range(CONV_WIDTH):
        xconv = xconv + cw[j:j + 1, :] * shifted[j]
    xb = xconv.astype(MXU_DTYPE)
    r = _sigmoid(_dot(xb, wa.astype(MXU_DTYPE)) + ba)
    ig = _sigmoid(_dot(xb, wx.astype(MXU_DTYPE)) + bx)
    sp = _softplus(-lam)
    log_a = -LRU_C * r * sp
    a = jnp.exp(log_a)
    mult = jnp.sqrt(_neg_expm1(2.0 * log_a))
    hs = _scan(a, mult * ig * xconv, up=False)
    return dict(shifted=shifted, xconv=xconv, xb=xb, r=r, ig=ig, sp=sp, a=a, mult=mult, hs=hs)


def _col(width, off):
    return lambda s: pl.BlockSpec((s, width), lambda i: (0, off + i))


def _part(width, slot):
    return lambda s: pl.BlockSpec((None, s, width), lambda i: (slot, 0, i))


def _lru_specs(s):
    w = LANES
    return [_part(w, 0)(s), _part(w, 1)(s), pl.BlockSpec((CONV_WIDTH, w), lambda i: (0, i)),
            pl.BlockSpec((1, w), lambda i: (0, i)), pl.BlockSpec((None, w, w), lambda i: (i, 0, 0)),
            pl.BlockSpec((1, w), lambda i: (0, i)), pl.BlockSpec((None, w, w), lambda i: (i, 0, 0)),
            pl.BlockSpec((1, w), lambda i: (0, i)), pl.BlockSpec((1, w), lambda i: (0, i))]


def _lru_fwd(h, cw, cb, wa, ba, wx, bx, lam):
    s = h.shape[1]

    def body(xc_ref, cg_ref, cw_ref, cb_ref, wa_ref, ba_ref, wx_ref, bx_ref, lam_ref, o_ref):
        f = _lru_forward(xc_ref[...], cw_ref[...], cb_ref[...], wa_ref[...], ba_ref[...], wx_ref[...], bx_ref[...],
                         lam_ref[...])
        sil, _ = _silu_and_grad(cg_ref[...])
        o_ref[...] = (f['hs'] * sil).astype(o_ref.dtype)

    return pl.pallas_call(
        body, name="lru_fwd", grid=(C_HEADS,), in_specs=_lru_specs(s), out_specs=_col(LANES, 0)(s),
        out_shape=jax.ShapeDtypeStruct((s, C_WIDTH + D_WIDTH), MXU_DTYPE), compiler_params=_params(("parallel",)),
    )(h, h, cw, cb.reshape(1, -1), wa, ba.reshape(1, -1), wx, bx.reshape(1, -1), lam.reshape(1, -1))


def _lru_bwd(h, dmix, cw, cb, wa, ba, wx, bx, lam):
    s = h.shape[1]

    def body(xc_ref, cg_ref, cw_ref, cb_ref, wa_ref, ba_ref, wx_ref, bx_ref, lam_ref, dco_ref,
             dh_ref, dcw_ref, dcb_ref, dwa_ref, dba_ref, dwx_ref, dbx_ref, dlam_ref):
        dxc_ref, dcg_ref = dh_ref.at[0], dh_ref.at[1]
        cw, lam = cw_ref[...], lam_ref[...]
        wab, wxb = wa_ref[...].astype(MXU_DTYPE), wx_ref[...].astype(MXU_DTYPE)
        f = _lru_forward(xc_ref[...], cw, cb_ref[...], wa_ref[...], ba_ref[...], wx_ref[...], bx_ref[...], lam)
        sil, dsil = _silu_and_grad(cg_ref[...])
        dco = dco_ref[...]
        hs, a, mult, ig, r, xconv = f['hs'], f['a'], f['mult'], f['ig'], f['r'], f['xconv']
        dcg_ref[...] = (dco * hs * dsil).astype(dcg_ref.dtype)
        lamb = _scan(_shift_up(a, 1), dco * sil, up=True)
        da = lamb * _shift_down(hs, 1)
        d_ig = lamb * mult * xconv
        d_mult = lamb * ig * xconv
        dxconv = lamb * mult * ig
        d_log_a = da * a - d_mult * (a * a) / mult
        dlam_ref[...] = jnp.sum(d_log_a * r, axis=0, keepdims=True) * LRU_C * _sigmoid(-lam)
        dpa = d_log_a * (-LRU_C * f['sp']) * r * (1.0 - r)
        dpx = d_ig * ig * (1.0 - ig)
        dba_ref[...] = jnp.sum(dpa, axis=0, keepdims=True)
        dbx_ref[...] = jnp.sum(dpx, axis=0, keepdims=True)
        dpab, dpxb = dpa.astype(MXU_DTYPE), dpx.astype(MXU_DTYPE)
        dwa_ref[...] = _dot_tn(f['xb'], dpab)
        dwx_ref[...] = _dot_tn(f['xb'], dpxb)
        dxconv = dxconv + _dot_nt(dpab, wab) + _dot_nt(dpxb, wxb)
        dcb_ref[...] = jnp.sum(dxconv, axis=0, keepdims=True)
        dxc = jnp.zeros_like(dxconv)
        for j in range(CONV_WIDTH):
            dcw_ref[j:j + 1, :] = jnp.sum(dxconv * f['shifted'][j], axis=0, keepdims=True)
            back = _shift_up(dxconv, CONV_WIDTH - 1 - j) if j < CONV_WIDTH - 1 else dxconv
            dxc = dxc + cw[j:j + 1, :] * back
        dxc_ref[...] = dxc.astype(dxc_ref.dtype)

    w = LANES
    vec = pl.BlockSpec((1, w), lambda i: (0, i))
    mat = pl.BlockSpec((None, w, w), lambda i: (i, 0, 0))
    vshape = jax.ShapeDtypeStruct((1, C_WIDTH), F32)
    mshape = jax.ShapeDtypeStruct((C_HEADS, w, w), F32)
    return pl.pallas_call(
        body, name="lru_bwd", grid=(C_HEADS,), in_specs=_lru_specs(s) + [_col(w, 0)(s)],
        out_specs=[pl.BlockSpec((2, s, w), lambda i: (0, 0, i)), pl.BlockSpec((CONV_WIDTH, w), lambda i: (0, i)),
                   vec, mat, vec, mat, vec, vec],
        out_shape=[jax.ShapeDtypeStruct((N_CHIPS, s, C_WIDTH), MXU_DTYPE),
                   jax.ShapeDtypeStruct((CONV_WIDTH, C_WIDTH), F32), vshape, mshape, vshape, mshape, vshape, vshape],
        compiler_params=_params(("parallel",)),
    )(h, h, cw, cb.reshape(1, -1), wa, ba.reshape(1, -1), wx, bx.reshape(1, -1), lam.reshape(1, -1), dmix)


def _pool_window(g):
    return jnp.left_shift(2, g).astype(F32)


def _select_window(g, sums):
    out = sums[-1]
    for k in range(len(sums) - 2, -1, -1):
        out = jnp.where(g == k, sums[k], out)
    return out


def _window_sums(x, shift):
    sums, cur = [], x
    for k in range(D_GROUPS):
        cur = cur + shift(cur, 2 ** k)
        sums.append(cur)
    return sums


def _pool_specs(s):
    w = D_GROUP_DIM
    return [_part(w, 2)(s), _part(w, 3)(s),
            pl.BlockSpec((None, w, w), lambda i: (i, 0, 0)), pl.BlockSpec((1, w), lambda i: (0, i))]


def _pool_forward(xd, g):
    rows = lax.broadcasted_iota(jnp.int32, xd.shape, 0).astype(F32) + 1.0
    cnt = jnp.minimum(rows, _pool_window(g))
    pooled = _select_window(g, _window_sums(xd, _shift_down)) / cnt - xd
    return pooled, cnt


def _pool_fwd(h, mix, w_pool, d_scale):
    s = h.shape[1]

    def body(xd_ref, dg_ref, wp_ref, sc_ref, mix_ref, o_ref):
        pooled, _ = _pool_forward(xd_ref[...], pl.program_id(0))
        mixed = _dot(pooled.astype(MXU_DTYPE), wp_ref[...].astype(MXU_DTYPE))
        sil, _ = _silu_and_grad(dg_ref[...])
        o_ref[...] = (mixed * sc_ref[...] * sil).astype(o_ref.dtype)

    return pl.pallas_call(
        body, name="pool_fwd", grid=(D_GROUPS,), in_specs=_pool_specs(s) + [ANY],
        out_specs=_col(D_GROUP_DIM, C_WIDTH // D_GROUP_DIM)(s), out_shape=jax.ShapeDtypeStruct(mix.shape, mix.dtype),
        input_output_aliases={4: 0}, compiler_params=_params(("parallel",)),
    )(h, h, w_pool, d_scale.reshape(1, -1), mix)


def _pool_bwd(h, dmix, dh, w_pool, d_scale):
    s = h.shape[1]
    w = D_GROUP_DIM

    def body(xd_ref, dg_ref, wp_ref, sc_ref, ddo_ref, dh_in_ref, dh_ref, dwp_ref, dsc_ref):
        dxd_ref, ddg_ref = dh_ref.at[0], dh_ref.at[1]
        g = pl.program_id(0)
        pooled, cnt = _pool_forward(xd_ref[...], g)
        pb = pooled.astype(MXU_DTYPE)
        wpb = wp_ref[...].astype(MXU_DTYPE)
        mixed = _dot(pb, wpb)
        sil, dsil = _silu_and_grad(dg_ref[...])
        ddo = ddo_ref[...]
        sc = sc_ref[...]
        ddg_ref[...] = (ddo * mixed * sc * dsil).astype(ddg_ref.dtype)
        dms = ddo * sil
        dsc_ref[...] = jnp.sum(dms * mixed, axis=0, keepdims=True)
        dmb = (dms * sc).astype(MXU_DTYPE)
        dwp_ref[...] = _dot_tn(pb, dmb)
        dpooled = _dot_nt(dmb, wpb)
        dxd = _select_window(g, _window_sums(dpooled / cnt, _shift_up)) - dpooled
        dxd_ref[...] = dxd.astype(dxd_ref.dtype)

    return pl.pallas_call(
        body, name="pool_bwd", grid=(D_GROUPS,), in_specs=_pool_specs(s) + [_col(w, C_WIDTH // w)(s), ANY],
        out_specs=[pl.BlockSpec((2, s, w), lambda i: (1, 0, i)), pl.BlockSpec((None, w, w), lambda i: (i, 0, 0)),
                   pl.BlockSpec((1, w), lambda i: (0, i))],
        out_shape=[jax.ShapeDtypeStruct(dh.shape, dh.dtype),
                   jax.ShapeDtypeStruct((D_GROUPS, w, w), F32), jax.ShapeDtypeStruct((1, D_WIDTH), F32)],
        input_output_aliases={5: 0}, compiler_params=_params(("parallel",)),
    )(h, h, w_pool, d_scale.reshape(1, -1), dmix, dh)


def _rope_tables(positions):
    s = positions.shape[0]
    inv_freq = ROPE_THETA ** (-jnp.arange(0, ROT_DIM, 2, dtype=F32) / ROT_DIM)
    ang = positions.astype(F32)[:, None] * inv_freq
    c, sn = jnp.cos(ang), jnp.sin(ang)
    rest = B_HEAD_DIM - ROT_DIM
    c64 = jnp.concatenate([c, c, jnp.ones((s, rest), F32)], axis=1)
    s64 = jnp.concatenate([-sn, sn, jnp.zeros((s, rest), F32)], axis=1)
    return jnp.tile(c64, (1, LANES // B_HEAD_DIM)), jnp.tile(s64, (1, LANES // B_HEAD_DIM))


def _local_step(x, positions, target, w, projections_of, projections_ahead, projection_grads, layer_grads_begin):
    cq, sq = _rope_tables(positions)
    saved = []
    xin, xin_b = x, x.astype(MXU_DTYPE)
    for layer in range(DEPTH):
        j = layer // 2
        kind = 'even' if layer % 2 == 0 else 'odd'
        w_in, w_out = projections_of(layer, xin_b)
        h = _mm_in(xin_b, w_in, W_IN_TRANSPOSED[kind], "mm_" + kind + "_in")
        if kind == 'even':
            behind, w_out_after = w_out(h) if callable(w_out) else (h, None)
            mix = _even_fwd(h, cq, sq, w['even_a_ln_g'][j], w['even_a_ln_b'][j], w['even_a_ws'][j],
                            w['even_a_bs'][j].T, w['even_b_sinks'][j], behind)
            w_out = w_out_after(mix) if w_out_after else w_out
        else:
            mix = _lru_fwd(h, w['odd_conv_w'][j], w['odd_conv_b'][j], w['odd_w_a'][j], w['odd_b_a'][j],
                           w['odd_w_x'][j], w['odd_b_x'][j], w['odd_lam'][j])
            mix = _pool_fwd(h, mix, w['odd_w_pool'][j], w['odd_d_scale'][j])
        if layer == DEPTH - 1:
            z, out, out_b = _mm_out_norm(mix, w_out, xin, w[kind + '_ln_g'][j], w[kind + '_ln_b'][j],
                                         "mm_out_norm_loss", target=target)
        else:
            z, out, out_b = _mm_out_norm(mix, w_out, xin, w[kind + '_ln_g'][j], w[kind + '_ln_b'][j], "mm_out_norm",
                                         after=projections_ahead(layer + 1, mix))
        saved.append((xin_b, h, mix, z, w_in, w_out))
        xin, xin_b = out, out_b

    dout, loss = xin, xin_b
    per_layer = {n: [None, None] for n in SMALL_NAMES}
    for layer in reversed(range(DEPTH)):
        j = layer // 2
        kind = 'even' if layer % 2 == 0 else 'odd'
        xin_b, h, mix, z, w_in, w_out = saved[layer]
        dz, dg, db, dmix, dw_out = _norm_out_bwd(dout, z, w[kind + '_ln_g'][j], w_out, mix,
                                                 layer_grads_begin(layer, dout), "norm_out_bwd")
        per_layer[kind + '_ln_g'][j], per_layer[kind + '_ln_b'][j] = dg[0], db[0]
        if kind == 'even':
            dh, dlg, dlb, dws, dbst, dsink = _even_bwd(
                h, dmix, cq, sq, w['even_a_ln_g'][j], w['even_a_ln_b'][j], w['even_a_ws'][j], w['even_a_bs'][j].T,
                w['even_b_sinks'][j])
            for n, val in (('even_a_ln_g', dlg[0]), ('even_a_ln_b', dlb[0]), ('even_a_ws', dws),
                           ('even_a_bs', dbst.T), ('even_b_sinks', dsink[0])):
                per_layer[n][j] = val
        else:
            dh, dcw, dcb, dwa, dba, dwx, dbx, dlam = _lru_bwd(
                h, dmix, w['odd_conv_w'][j], w['odd_conv_b'][j], w['odd_w_a'][j], w['odd_b_a'][j], w['odd_w_x'][j],
                w['odd_b_x'][j], w['odd_lam'][j])
            dh, dwp, dsc = _pool_bwd(h, dmix, dh, w['odd_w_pool'][j], w['odd_d_scale'][j])
            for n, val in (('odd_conv_w', dcw), ('odd_conv_b', dcb[0]), ('odd_w_a', dwa), ('odd_b_a', dba[0]),
                           ('odd_w_x', dwx), ('odd_b_x', dbx[0]), ('odd_lam', dlam[0]), ('odd_w_pool', dwp),
                           ('odd_d_scale', dsc[0])):
                per_layer[n][j] = val
        dw_in = _mm_dw_in(xin_b, dh, W_IN_TRANSPOSED[kind], "mm_dw_in_" + kind)
        dout = _mm_dx(dh, w_in, dz, W_IN_TRANSPOSED[kind], projection_grads(layer, dw_in, dw_out), "mm_dx_" + kind)
    return loss, dout, {n: jnp.stack(v) for n, v in per_layer.items()}


ANY = pl.BlockSpec(memory_space=pl.ANY)


def _where_am_i():
    return lax.axis_index("x"), lax.axis_index("y"), lax.axis_index("c")


def _other_chips(x, y):
    return [(1 - x, y), (x, 1 - y), (1 - x, 1 - y)]


PACKED_ROWS = 16


def _half_rows(rows, which):
    half = rows // 2
    assert half % PACKED_ROWS == 0, rows
    return pl.ds(pl.multiple_of(which * half, PACKED_ROWS), half)


def _sibling_swap(bufs, name, after=()):
    n, na = len(bufs), len(after)

    def body(*refs):
        x_refs, out_refs, (send_sems, recv_sems) = refs[:n], refs[n + na:2 * n + na], refs[2 * n + na:]
        x, y, c = _where_am_i()
        copies = []
        for p in range(n):
            copies.append(pltpu.make_async_remote_copy(
                src_ref=x_refs[p], dst_ref=out_refs[p], send_sem=send_sems.at[p], recv_sem=recv_sems.at[p],
                device_id=(x, y, 1 - c), device_id_type=MESH))
            copies[-1].start()
        for cp in copies:
            cp.wait()

    return pl.pallas_call(
        body, name=name, in_specs=[ANY] * (n + na), out_specs=[ANY] * n,
        out_shape=[jax.ShapeDtypeStruct(b.shape, b.dtype) for b in bufs],
        scratch_shapes=[pltpu.SemaphoreType.DMA((n,)), pltpu.SemaphoreType.DMA((n,))],
    )(*bufs, *after)


HBM = pl.BlockSpec(memory_space=pltpu.HBM)
SEM = pl.BlockSpec(memory_space=pltpu.SEMAPHORE)
IN_FLIGHT = pltpu.CompilerParams(has_side_effects=pltpu.SideEffectType.DATAFLOW_SIDE_EFFECTING)


IN_PLACE = ('forward',)


def _peers(kind, x, y, c):
    if kind in ('swap', 'join'):
        return [(x, y, 1 - c)]
    return [(x, y, 1 - c) if kind == 'forward' else (px, py, c) for px, py in _other_chips(x, y)]


def _n_peers(kind):
    return 1 if kind in ('swap', 'join') else N_CHIPS - 1


def _land_shape(kind, a):
    if kind == 'gather':
        return (N_CHIPS,) + a.shape
    return (a.shape[0], a.shape[1] // 2, a.shape[2]) if kind == 'swap' else a.shape


def _ici_copy(kind, src_ref, land_ref, send_sems, recv_sems, k, peer, x, y, c, arriving=False):
    me = 2 * x + y
    if kind == 'gather':
        half = _half_rows(src_ref.shape[0], c)
        src, dst = src_ref.at[half, :], land_ref.at[me, half, :]
    elif kind == 'scatter':
        src, dst = src_ref.at[2 * peer[0] + peer[1]], land_ref.at[me]
    elif kind == 'swap':
        src, dst = src_ref.at[:, _half_rows(src_ref.shape[1], 1 - c), :], land_ref
    elif kind == 'join':
        src, dst = src_ref, land_ref
    else:
        px, py = _other_chips(x, y)[k % (N_CHIPS - 1)]
        rows = src_ref.at[2 * px + py, _half_rows(src_ref.shape[1], 1 - c if arriving else c), :]
        src, dst = rows, rows
    return pltpu.make_async_remote_copy(src_ref=src, dst_ref=dst, send_sem=send_sems.at[k], recv_sem=recv_sems.at[k],
                                        device_id=peer, device_id_type=MESH)


def _ici_start(kind, groups, name, after=()):
    flat = [a for g in groups for a in g]
    n, ng, np_, na = len(flat), len(groups), _n_peers(kind), len(after)
    nl = 0 if kind in IN_PLACE else n
    land_shapes = [_land_shape(kind, a) for a in flat[:nl]]

    def body(*refs):
        srcs, sems, token = refs[:n], refs[n + nl + na:n + nl + na + 2 * ng], refs[-1]
        lands = refs[n:n + nl] if nl else srcs
        x, y, c = _where_am_i()
        p = 0
        for gi, g in enumerate(groups):
            for q in range(len(g)):
                for k, peer in enumerate(_peers(kind, x, y, c)):
                    _ici_copy(kind, srcs[p], lands[p], sems[2 * gi], sems[2 * gi + 1], np_ * q + k, peer, x, y,
                              c).start()
                p += 1
        token[...] = jnp.zeros_like(token)

    sem_shapes = [pltpu.SemaphoreType.DMA((np_ * len(g),)) for g in groups for _ in range(2)]
    outs = pl.pallas_call(
        body, name=name, in_specs=[HBM] * (n + nl) + [ANY] * na,
        out_specs=[SEM] * (2 * ng) + [HBM] * (n + nl) + [pl.BlockSpec(memory_space=pltpu.VMEM)],
        out_shape=sem_shapes + [pltpu.HBM(a.shape, a.dtype) for a in flat]
        + [pltpu.HBM(s, a.dtype) for s, a in zip(land_shapes, flat)] + [jax.ShapeDtypeStruct((8, LANES), F32)],
        input_output_aliases={i: 2 * ng + i for i in range(n + nl)}, compiler_params=IN_FLIGHT,
    )(*[pltpu.with_memory_space_constraint(a, pltpu.HBM) for a in flat],
      *[pltpu.with_memory_space_constraint(lax.empty(s, a.dtype), pltpu.HBM) for s, a in zip(land_shapes, flat)],
      *after)
    sems, srcs, token = outs[:2 * ng], outs[2 * ng:2 * ng + n], outs[-1]
    lands = outs[2 * ng + n:2 * ng + n + nl] if nl else srcs
    started, p = [], 0
    for gi, g in enumerate(groups):
        started.append((sems[2 * gi], sems[2 * gi + 1], srcs[p:p + len(g)], lands[p:p + len(g)]))
        p += len(g)
    return started, token


def _ici_wait(kind, started, after, name):
    send_sems, recv_sems, srcs, lands = started
    n, np_ = len(srcs), _n_peers(kind)
    nl = 0 if kind in IN_PLACE else n
    after = list(after)

    def body(*refs):
        src_refs, send_ref, recv_ref = refs[:n], refs[n + nl], refs[n + nl + 1]
        land_refs = refs[n:n + nl] if nl else src_refs
        x, y, c = _where_am_i()
        for q in range(n):
            for k, peer in enumerate(_peers(kind, x, y, c)):
                args = (kind, src_refs[q], land_refs[q], send_ref, recv_ref, np_ * q + k, peer, x, y, c)
                _ici_copy(*args).wait_send()
                _ici_copy(*args, arriving=True).wait_recv()

    arrays = list(srcs) + (list(lands) if nl else [])
    outs = pl.pallas_call(
        body, name=name, in_specs=[HBM] * (n + nl) + [SEM, SEM] + [ANY] * len(after), out_specs=[HBM] * (n + nl),
        out_shape=[pltpu.HBM(a.shape, a.dtype) for a in arrays],
        input_output_aliases={i: i for i in range(n + nl)}, compiler_params=IN_FLIGHT,
    )(*arrays, send_sems, recv_sems, *after)
    return outs[:n], (outs[n:] if nl else outs[:n])


def _forward_to_sibling(lands, name):
    n = len(lands)

    def body(*refs):
        out_refs, (send_sems, recv_sems) = refs[n:2 * n], refs[2 * n:]
        x, y, c = _where_am_i()

        def copy(q, k, peer, half_of):
            rows = out_refs[q].at[2 * peer[0] + peer[1], _half_rows(lands[q].shape[1], half_of), :]
            return pltpu.make_async_remote_copy(
                src_ref=rows, dst_ref=rows, send_sem=send_sems.at[3 * q + k], recv_sem=recv_sems.at[3 * q + k],
                device_id=(x, y, 1 - c), device_id_type=MESH)

        sends = [copy(q, k, peer, c) for q in range(n) for k, peer in enumerate(_other_chips(x, y))]
        for cp in sends:
            cp.start()
        for q in range(n):
            for k, peer in enumerate(_other_chips(x, y)):
                copy(q, k, peer, 1 - c).wait_recv()
        for cp in sends:
            cp.wait_send()

    return pl.pallas_call(
        body, name=name, in_specs=[ANY] * n, out_specs=[ANY] * n,
        out_shape=[jax.ShapeDtypeStruct(a.shape, a.dtype) for a in lands],
        input_output_aliases={i: i for i in range(n)},
        scratch_shapes=[pltpu.SemaphoreType.DMA((3 * n,)), pltpu.SemaphoreType.DMA((3 * n,))],
    )(*lands)


def _with_own_slot(lands, own, chip):
    return [lax.dynamic_update_slice(land, mine[None], (chip, 0, 0)) for land, mine in zip(lands, own)]


def _row_tile(rows, row_bytes, n_bufs, budget=24 * 1024 * 1024):
    best = None
    for t in range(PACKED_ROWS, rows + 1, PACKED_ROWS):
        if rows % t == 0 and 2 * n_bufs * t * row_bytes <= budget:
            best = t
    assert best is not None, (rows, row_bytes)
    return best


def _add_halves(full, recv, core, out_dtype, name):
    n, h, cols = recv.shape
    tr = _row_tile(h, cols * 4, 3)
    nt = h // tr

    def body(core_ref, a_ref, b_ref, o_ref):
        o_ref[...] = (a_ref[...].astype(F32) + b_ref[...].astype(F32)).astype(out_dtype)

    blk = (None, tr, cols)
    return pl.pallas_call(
        body, name=name,
        grid_spec=pltpu.PrefetchScalarGridSpec(
            num_scalar_prefetch=1, grid=(n, nt),
            in_specs=[pl.BlockSpec(blk, lambda j, i, core_ref: (j, core_ref[0] * nt + i, 0)),
                      pl.BlockSpec(blk, lambda j, i, core_ref: (j, i, 0))],
            out_specs=pl.BlockSpec(blk, lambda j, i, core_ref: (j, i, 0))),
        out_shape=jax.ShapeDtypeStruct((n, h, cols), out_dtype),
        compiler_params=_params(("parallel", "parallel")),
    )(core.reshape(1), full, recv)


def _add2(a, b, name):
    r, cols = a.shape
    tr = _row_tile(r, cols * 4, 3)

    def body(a_ref, b_ref, o_ref):
        o_ref[...] = a_ref[...] + b_ref[...]

    blk = pl.BlockSpec((tr, cols), lambda i: (i, 0))
    return pl.pallas_call(body, name=name, grid=(r // tr,), in_specs=[blk, blk], out_specs=blk,
                          out_shape=jax.ShapeDtypeStruct(a.shape, a.dtype), compiler_params=_params(("parallel",)))(a, b)


def _sum_slots(parts, own, chip, name):
    n, h, cols = parts.shape
    tr = _row_tile(h, cols * 4, n + 2)

    def body(chip_ref, p_ref, own_ref, o_ref):
        acc = None
        for k in range(n):
            term = jnp.where(chip_ref[0] == k, own_ref[...], p_ref[k]).astype(F32)
            acc = term if acc is None else acc + term
        o_ref[...] = acc

    own_spec = (pl.BlockSpec((None, tr, cols), lambda i, chip_ref: (chip_ref[0], i, 0)) if own.ndim == 3
                else pl.BlockSpec((tr, cols), lambda i, chip_ref: (i, 0)))
    return pl.pallas_call(
        body, name=name,
        grid_spec=pltpu.PrefetchScalarGridSpec(
            num_scalar_prefetch=1, grid=(h // tr,),
            in_specs=[pl.BlockSpec((n, tr, cols), lambda i, chip_ref: (0, i, 0)), own_spec],
            out_specs=pl.BlockSpec((tr, cols), lambda i, chip_ref: (i, 0))),
        out_shape=jax.ShapeDtypeStruct((h, cols), F32), compiler_params=_params(("parallel",)),
    )(chip.reshape(1), parts, own)


def _adamw(w, g, m, v, name):
    shape = w.shape
    cols = shape[-1]
    rows = math.prod(shape[:-1])
    tr = rows
    if rows % 8 == 0:
        tr = 8
        for t in range(8, rows + 1, 8):
            if rows % t == 0 and t * cols * 4 <= 1536 * 1024:
                tr = t
    c1 = 1.0 - ADAM_B1 ** ADAM_STEP
    c2 = 1.0 - ADAM_B2 ** ADAM_STEP

    def body(w_ref, g_ref, m_ref, v_ref, d_ref, nm_ref, nv_ref):
        gg = g_ref[...]
        nm = ADAM_B1 * m_ref[...] + (1.0 - ADAM_B1) * gg
        nv = ADAM_B2 * v_ref[...] + (1.0 - ADAM_B2) * (gg * gg)
        d_ref[...] = -ADAM_LR * ((nm / c1) / (jnp.sqrt(nv / c2) + ADAM_EPS) + ADAM_WD * w_ref[...])
        nm_ref[...] = nm
        nv_ref[...] = nv

    blk = pl.BlockSpec((tr, cols), lambda i: (i, 0))
    flat = jax.ShapeDtypeStruct((rows, cols), F32)
    outs = pl.pallas_call(
        body, name=name, grid=(rows // tr,), in_specs=[blk] * 4, out_specs=[blk] * 3, out_shape=[flat] * 3,
        compiler_params=_params(("parallel",)),
    )(*[a.reshape(rows, cols) for a in (w, g, m, v)])
    return [o.reshape(shape) for o in outs]


def _adamw_update(w, g, m, v):
    c1 = 1.0 - ADAM_B1 ** ADAM_STEP
    c2 = 1.0 - ADAM_B2 ** ADAM_STEP
    nm = ADAM_B1 * m + (1.0 - ADAM_B1) * g
    nv = ADAM_B2 * v + (1.0 - ADAM_B2) * (g * g)
    return -ADAM_LR * ((nm / c1) / (jnp.sqrt(nv / c2) + ADAM_EPS) + ADAM_WD * w), nm, nv


def _adamw_projection(w, m, v, mine, theirs, core, name):
    _, r, cols = w.shape
    h = r // 2
    tr = _row_tile(h, cols * 4, 12, budget=40 * 1024 * 1024)
    nt = h // tr

    def body(core_ref, w_ref, m_ref, v_ref, a0, b0, a1, b1, g_ref, d_ref, nm_ref, nv_ref):
        layer, half = pl.program_id(0), pl.program_id(1)
        own = half == core_ref[0]
        g = jnp.where(layer == 0, jnp.where(own, a0[...], b0[...]), jnp.where(own, a1[...], b1[...]))
        g_ref[...] = g
        d_ref[...], nm_ref[...], nv_ref[...] = _adamw_update(w_ref[...], g, m_ref[...], v_ref[...])

    def piece(layer, own):
        def index(l, hh, i, core_ref):
            used = (l == layer) & ((hh == core_ref[0]) == own)
            return (jnp.where(used, i, 0), 0)
        return pl.BlockSpec((tr, cols), index)

    whole = pl.BlockSpec((None, tr, cols), lambda l, hh, i, core_ref: (l, hh * nt + i, 0))
    return pl.pallas_call(
        body, name=name,
        grid_spec=pltpu.PrefetchScalarGridSpec(
            num_scalar_prefetch=1, grid=(2, 2, nt),
            in_specs=[whole] * 3 + [piece(0, True), piece(0, False), piece(1, True), piece(1, False)],
            out_specs=[whole] * 4),
        out_shape=[jax.ShapeDtypeStruct(w.shape, F32)] * 4,
        compiler_params=_params(("arbitrary", "arbitrary", "arbitrary")),
    )(core.reshape(1), w, m, v, mine[0], theirs[0], mine[1], theirs[1])


def _chip_shape(name):
    shape = list(FULL_SHAPES[name])
    axis = BIG_SHARD_AXIS.get(name, SMALL_SHARD_AXIS.get(name))
    if axis is not None:
        shape[axis] //= N_CHIPS
    return tuple(shape)


def _round_up(n, k):
    return -(-n // k) * k


SHARDED_SMALL_WORDS = sum(math.prod(_chip_shape(n)) for n in SMALL_SHARDED)
SHARDED_SMALL_ROWS = _round_up(-(-SHARDED_SMALL_WORDS // PACK_COLS), 32)
SMALL_MATRICES = ['even_a_ws', 'odd_w_a', 'odd_w_x', 'odd_w_pool']
SMALL_VECTORS = [n for n in SMALL_NAMES if n not in SMALL_MATRICES]
VECTOR_ROWS = _round_up(-(-(sum(math.prod(FULL_SHAPES[n]) for n in SMALL_VECTORS) + 1) // PACK_COLS), 32)


def _pack_rows(flat, rows):
    return jnp.pad(flat, (0, rows * PACK_COLS - flat.shape[0])).reshape(rows, PACK_COLS)


def _pack_small_sharded(shards):
    return _pack_rows(jnp.concatenate([shards[n].reshape(-1) for n in SMALL_SHARDED]), SHARDED_SMALL_ROWS)


def _unpack_small_sharded(gathered_small):
    out = {}
    words = gathered_small.reshape(N_CHIPS, -1)
    w0 = 0
    for n in SMALL_SHARDED:
        size = math.prod(_chip_shape(n))
        blocks = words[:, w0:w0 + size].reshape((N_CHIPS,) + _chip_shape(n))
        out[n] = jnp.concatenate([blocks[j] for j in range(N_CHIPS)], axis=SMALL_SHARD_AXIS[n])
        w0 += size
    return out


def _small_grad_parts(grads, loss):
    words = [grads[n].reshape(-1) for n in SMALL_VECTORS] + [loss.reshape(-1)]
    return ([_pack_rows(jnp.concatenate(words), VECTOR_ROWS)]
            + [grads[n].reshape(-1, FULL_SHAPES[n][-1]) for n in SMALL_MATRICES])


def _own_block(name, full, chip):
    if name not in SMALL_SHARD_AXIS:
        return full
    axis = SMALL_SHARD_AXIS[name]
    width = FULL_SHAPES[name][axis] // N_CHIPS
    return lax.dynamic_slice_in_dim(full, chip * width, width, axis)


def _from_small_grad_parts(totals, chip):
    flat = totals[0].reshape(-1)
    out = {}
    w0 = 0
    for n in SMALL_VECTORS:
        size = math.prod(FULL_SHAPES[n])
        out[n] = _own_block(n, flat[w0:w0 + size].reshape(FULL_SHAPES[n]), chip)
        w0 += size
    for n, total in zip(SMALL_MATRICES, totals[1:]):
        out[n] = _own_block(n, total.reshape(FULL_SHAPES[n]), chip)
    return out, flat[w0]


def kernel(x, positions, even_w_in, even_a_ln_g, even_a_ln_b, even_a_ws, even_a_bs, even_b_sinks, even_w_out, even_ln_g, even_ln_b, odd_w_in, odd_conv_w, odd_conv_b, odd_w_a, odd_b_a, odd_w_x, odd_b_x, odd_lam, odd_w_pool, odd_d_scale, odd_w_out, odd_ln_g, odd_ln_b, loss_target, m_even_w_in, m_even_a_ln_g, m_even_a_ln_b, m_even_a_ws, m_even_a_bs, m_even_b_sinks, m_even_w_out, m_even_ln_g, m_even_ln_b, m_odd_w_in, m_odd_conv_w, m_odd_conv_b, m_odd_w_a, m_odd_b_a, m_odd_w_x, m_odd_b_x, m_odd_lam, m_odd_w_pool, m_odd_d_scale, m_odd_w_out, m_odd_ln_g, m_odd_ln_b, v_even_w_in, v_even_a_ln_g, v_even_a_ln_b, v_even_a_ws, v_even_a_bs, v_even_b_sinks, v_even_w_out, v_even_ln_g, v_even_ln_b, v_odd_w_in, v_odd_conv_w, v_odd_conv_b, v_odd_w_a, v_odd_b_a, v_odd_w_x, v_odd_b_x, v_odd_lam, v_odd_w_pool, v_odd_d_scale, v_odd_w_out, v_odd_ln_g, v_odd_ln_b):
    local = dict(zip(WEIGHT_NAMES, (even_w_in, even_a_ln_g, even_a_ln_b, even_a_ws, even_a_bs, even_b_sinks, even_w_out, even_ln_g, even_ln_b, odd_w_in, odd_conv_w, odd_conv_b, odd_w_a, odd_b_a, odd_w_x, odd_b_x, odd_lam, odd_w_pool, odd_d_scale, odd_w_out, odd_ln_g, odd_ln_b)))
    mom = dict(zip(WEIGHT_NAMES, (m_even_w_in, m_even_a_ln_g, m_even_a_ln_b, m_even_a_ws, m_even_a_bs, m_even_b_sinks, m_even_w_out, m_even_ln_g, m_even_ln_b, m_odd_w_in, m_odd_conv_w, m_odd_conv_b, m_odd_w_a, m_odd_b_a, m_odd_w_x, m_odd_b_x, m_odd_lam, m_odd_w_pool, m_odd_d_scale, m_odd_w_out, m_odd_ln_g, m_odd_ln_b)))
    vel = dict(zip(WEIGHT_NAMES, (v_even_w_in, v_even_a_ln_g, v_even_a_ln_b, v_even_a_ws, v_even_a_bs, v_even_b_sinks, v_even_w_out, v_even_ln_g, v_even_ln_b, v_odd_w_in, v_odd_conv_w, v_odd_conv_b, v_odd_w_a, v_odd_b_a, v_odd_w_x, v_odd_b_x, v_odd_lam, v_odd_w_pool, v_odd_d_scale, v_odd_w_out, v_odd_ln_g, v_odd_ln_b)))
    mx, my, core = _where_am_i()
    chip = 2 * mx + my

    stored = lambda n, a: jnp.swapaxes(a, 1, 2) if n == 'even_w_in' and W_IN_TRANSPOSED['even'] else a
    names_of = lambda layer: [('even' if layer % 2 == 0 else 'odd') + s for s in ('_w_in', '_w_out')]

    shards = [[stored(n, local[n])[layer // 2].astype(MXU_DTYPE) for n in names_of(layer)] for layer in range(DEPTH)]
    (first_in, first_out), first_token = _ici_start(
        'gather', [[_pack_small_sharded(local), shards[0][0]], [shards[0][1]]], "gather_start_first")
    later, gather_token = _ici_start('gather', shards[1:], "gather_start", after=[first_token])
    gathers = [None] + later

    own, lands = _ici_wait('gather', first_in, [first_token, gather_token], "gather_wait_0")
    small_sharded, first_w_in = _with_own_slot(_forward_to_sibling(lands, "gather_forward_0"), own, chip)
    weights = dict(local)
    weights.update(_unpack_small_sharded(small_sharded))
    passing = {}

    def passed_on(started, after, name):
        own, lands = _ici_wait('gather', started, [after], "gather_wait_" + name)
        (forward,), token = _ici_start('forward', [lands], "forward_start_" + name)

        def finish(then):
            done, _ = _ici_wait('forward', forward, [then], "forward_wait_" + name)
            return _with_own_slot(done, own, chip)

        return token, finish

    def first_w_out(h):
        token, finish = passed_on(first_out, h, "0_out")
        return token, lambda mix: finish(mix)[0]

    def projections_ahead(layer, after):
        token, passing[layer] = passed_on(gathers[layer], after, str(layer))
        return token

    def projections_of(layer, after):
        return (first_w_in, first_w_out) if layer == 0 else passing.pop(layer)(after)

    swapping, in_flight, mine, started = [], [], {}, {}

    def land(*after):
        layer, scatter = in_flight.pop()
        parts, lands = _ici_wait('scatter', scatter, after, "scatter_wait_%d" % layer)
        for n, arrived, part in zip(names_of(layer), lands, parts):
            mine[n, layer // 2] = _sum_slots(arrived, part, chip, "big_sum_%s_%d" % (n, layer // 2))

    odd_pieces = [(n, l) for n in ('odd_w_in', 'odd_w_out') for l in range(2)]
    joining = []

    def projection_grads(layer, dw_in, dw_out):
        behind = []
        if in_flight:
            land(dw_in)
        if layer == 0:
            (join,), token = _ici_start('join', [[mine[p] for p in odd_pieces]], "join_odd_start")
            joining.append(join)
            behind = [token]
        (swap,), token = _ici_start('swap', [[dw_in, dw_out]], "presum_start_%d" % layer, after=behind)
        swapping.append((layer, swap))
        return presum_and_scatter(token) if layer == 0 else token

    def presum_and_scatter(*after):
        layer, swap = swapping.pop()
        partial, from_sibling = _ici_wait('swap', swap, after, "presum_wait_%d" % layer)
        chip_part = [_add_halves(g, r, core, MXU_DTYPE, "big_presum_add_%s_%d" % (n, layer // 2))
                     for g, r, n in zip(partial, from_sibling, names_of(layer))]
        (scatter,), started[layer] = _ici_start('scatter', [chip_part], "scatter_start_%d" % layer)
        in_flight.append((layer, scatter))
        return started[layer]

    layer_grads_begin = lambda layer, dout: presum_and_scatter(dout) if swapping else dout
    loss, grad_x, grads = _local_step(x[0], positions[0], loss_target[0], weights, projections_of, projections_ahead,
                                      projection_grads, layer_grads_begin)

    small = _small_grad_parts(grads, loss)
    chip_sums = [_add2(a, b, "small_presum_add_%d" % i)
                 for i, (a, b) in enumerate(zip(small, _sibling_swap(small, "small_swap", after=[started[0]])))]
    (small_gather,), small_token = _ici_start('gather', [chip_sums], "small_gather_start")

    reduced, deltas, new_m, new_v = {}, {}, {}, {}

    def adamw_projections(kind, after):
        names = [kind + '_w_in', kind + '_w_out']
        pieces = [(n, l) for n in names for l in range(2)]
        if kind == 'odd':
            halves, theirs = _ici_wait('join', joining.pop(), after, "join_odd_wait")
            mine.update(zip(pieces, halves))
        else:
            theirs = _sibling_swap([mine[p] for p in pieces], "big_join_" + kind, after=after)
        theirs = dict(zip(pieces, theirs))
        done = []
        for n in names:
            outs = _adamw_projection(stored(n, local[n]), stored(n, mom[n]), stored(n, vel[n]),
                                     [mine[n, l] for l in range(2)], [theirs[n, l] for l in range(2)], core,
                                     "adamw_" + n)
            reduced[n], deltas[n], new_m[n], new_v[n] = [stored(n, o) for o in outs]
            done.append(outs[-1])
        return done

    done = adamw_projections('odd', [small_token])
    land(grad_x, *done)
    done = adamw_projections('even', [])

    own, lands = _ici_wait('gather', small_gather, done, "small_gather_wait")
    of_chips = _forward_to_sibling(lands, "small_gather_forward")
    small_totals, total_loss = _from_small_grad_parts(
        [_sum_slots(g, mine_i, chip, "small_sum_%d" % i) for i, (g, mine_i) in enumerate(zip(of_chips, own))], chip)
    reduced.update(small_totals)
    for n in SMALL_NAMES:
        deltas[n], new_m[n], new_v[n] = _adamw(local[n], reduced[n], mom[n], vel[n], "adamw_" + n)

    return (total_loss, grad_x[None], *[reduced[n] for n in WEIGHT_NAMES], *[deltas[n] for n in WEIGHT_NAMES],
            *[new_m[n] for n in WEIGHT_NAMES], *[new_v[n] for n in WEIGHT_NAMES])
```

```python
import math

import jax
import jax.numpy as jnp
from jax import lax
from jax.experimental import pallas as pl
from jax.experimental.pallas import tpu as pltpu

F32 = jnp.float32
MXU_DTYPE = jnp.bfloat16

DEPTH = 4
A_WIDTH = 1024
A_GROUPS = 8
CHUNK = 128
B_HEAD_DIM = 64
B_Q_HEADS = 16
B_KV_HEADS = 2
WINDOW = 128
ROT_DIM = 16
ROPE_THETA = 500000.0
C_WIDTH = 1024
C_HEADS = 8
CONV_WIDTH = 4
LRU_C = 8.0
D_WIDTH = 1024
D_GROUPS = 4
D_GROUP_DIM = 256
EVEN_IN = 5376
DN_ALPHA = (2 * DEPTH) ** 0.25
LN_EPS = 1e-5
ATTN_SCALE = B_HEAD_DIM ** -0.5
NEG_BIG = -1e30

ADAM_LR = 0.001
ADAM_B1 = 0.9
ADAM_B2 = 0.999
ADAM_EPS = 1e-08
ADAM_WD = 0.01
ADAM_STEP = 10

LANES = 128
VMEM_LIMIT = 56 * 1024 * 1024
N_CHIPS = 4
MESH = pl.DeviceIdType.MESH

WEIGHT_NAMES = ['even_w_in', 'even_a_ln_g', 'even_a_ln_b', 'even_a_ws', 'even_a_bs', 'even_b_sinks', 'even_w_out',
                'even_ln_g', 'even_ln_b', 'odd_w_in', 'odd_conv_w', 'odd_conv_b', 'odd_w_a', 'odd_b_a', 'odd_w_x',
                'odd_b_x', 'odd_lam', 'odd_w_pool', 'odd_d_scale', 'odd_w_out', 'odd_ln_g', 'odd_ln_b']
BIG_SHARD_AXIS = {'even_w_in': 2, 'even_w_out': 1, 'odd_w_in': 2, 'odd_w_out': 1}
SMALL_SHARD_AXIS = {'odd_conv_w': 2, 'odd_conv_b': 1, 'odd_b_a': 1, 'odd_b_x': 1, 'odd_lam': 1, 'odd_w_pool': 2,
                    'odd_d_scale': 1, 'odd_ln_g': 1, 'odd_ln_b': 1}
FULL_SHAPES = {
    'even_w_in': (2, 2048, 5376), 'even_a_ln_g': (2, 1024), 'even_a_ln_b': (2, 1024), 'even_a_ws': (2, 8, 128, 128),
    'even_a_bs': (2, 8, 128), 'even_b_sinks': (2, 16), 'even_w_out': (2, 2048, 2048), 'even_ln_g': (2, 2048),
    'even_ln_b': (2, 2048), 'odd_w_in': (2, 2048, 4096), 'odd_conv_w': (2, 4, 1024), 'odd_conv_b': (2, 1024),
    'odd_w_a': (2, 8, 128, 128), 'odd_b_a': (2, 1024), 'odd_w_x': (2, 8, 128, 128), 'odd_b_x': (2, 1024),
    'odd_lam': (2, 1024), 'odd_w_pool': (2, 4, 256, 256), 'odd_d_scale': (2, 1024), 'odd_w_out': (2, 2048, 2048),
    'odd_ln_g': (2, 2048), 'odd_ln_b': (2, 2048)}
BIG_NAMES = ['even_w_in', 'even_w_out', 'odd_w_in', 'odd_w_out']
SMALL_SHARDED = ['odd_conv_w', 'odd_conv_b', 'odd_b_a', 'odd_b_x', 'odd_lam', 'odd_d_scale', 'odd_ln_g', 'odd_ln_b',
                 'odd_w_pool']
SMALL_NAMES = [n for n in WEIGHT_NAMES if n not in BIG_NAMES]
W_IN_TRANSPOSED = {'even': True, 'odd': False}
PACK_COLS = 1024


def _params(sem):
    return pltpu.CompilerParams(dimension_semantics=sem, vmem_limit_bytes=VMEM_LIMIT)


def _dot(a, b):
    return lax.dot_general(a, b, (((1,), (0,)), ((), ())), preferred_element_type=F32)


def _dot_nt(a, b):
    return lax.dot_general(a, b, (((1,), (1,)), ((), ())), preferred_element_type=F32)


def _dot_tn(a, b):
    return lax.dot_general(a, b, (((0,), (0,)), ((), ())), preferred_element_type=F32)


def _sigmoid(x):
    return 0.5 * jnp.tanh(0.5 * x) + 0.5


def _silu_and_grad(x):
    s = _sigmoid(x)
    return x * s, s * (1.0 + x * (1.0 - s))


def _mm_call(body, name, grid, in_specs, out_spec, out_shape, args):
    return pl.pallas_call(body, name=name, grid=grid, in_specs=in_specs, out_specs=out_spec, out_shape=out_shape,
                          compiler_params=_params(("parallel",) * len(grid)))(*args)


def _resident(block, index_map):
    return pl.BlockSpec(block, index_map, pipeline_mode=pl.Buffered(1))


ROWS_OF_WT = 768


def _mm_in(x, wg, transposed, name):
    s, k = x.shape
    x_spec = _resident((s, k), lambda j: (0, 0))

    def body(x_ref, w_ref, o_ref):
        o_ref[...] = (_dot_nt if transposed else _dot)(x_ref[...], w_ref[...])

    if transposed:
        wt = wg.reshape(-1, k)
        n, tn = wt.shape[0], ROWS_OF_WT
        return _mm_call(body, name, (n // tn,), [x_spec, pl.BlockSpec((tn, k), lambda j: (j, 0))],
                        pl.BlockSpec((s, tn), lambda j: (0, j)), jax.ShapeDtypeStruct((s, n), F32), (x, wt))
    nc = wg.shape[2]
    return _mm_call(body, name, (N_CHIPS,), [x_spec, pl.BlockSpec((None, k, nc), lambda j: (j, 0, 0))],
                    pl.BlockSpec((None, s, nc), lambda j: (j, 0, 0)),
                    jax.ShapeDtypeStruct((N_CHIPS, s, nc), F32), (x, wg))


def _mm_dx(dh, wg, dz, transposed, after, name):
    s, d = dz.shape
    tm, tn = min(1024, s), 512
    res = pl.BlockSpec((tm, tn), lambda i, j: (i, j))
    if transposed:
        wt = wg.reshape(-1, d)
        n = wt.shape[0]

        def body(a_ref, w_ref, r_ref, after_ref, o_ref):
            o_ref[...] = DN_ALPHA * r_ref[...] + _dot(a_ref[...], w_ref[...])

        return _mm_call(body, name, (s // tm, d // tn),
                        [pl.BlockSpec((tm, n), lambda i, j: (i, 0)), pl.BlockSpec((n, tn), lambda i, j: (0, j)), res,
                         ANY], res, jax.ShapeDtypeStruct((s, d), F32), (dh, wt, dz, after))
    nc = dh.shape[2]

    def body(a_ref, w_ref, r_ref, after_ref, o_ref):
        acc = DN_ALPHA * r_ref[...]
        for j in range(N_CHIPS):
            acc = acc + _dot_nt(a_ref[j], w_ref[j])
        o_ref[...] = acc

    return _mm_call(body, name, (s // tm, d // tn),
                    [pl.BlockSpec((N_CHIPS, tm, nc), lambda i, j: (0, i, 0)),
                     pl.BlockSpec((N_CHIPS, tn, nc), lambda i, j: (0, j, 0)), res, ANY],
                    res, jax.ShapeDtypeStruct((s, d), F32), (dh, wg, dz, after))


def _mm_dw_in(x, dh, transposed, name):
    s, d = x.shape
    x_spec = _resident((s, d), lambda j: (0, 0))

    def body(a_ref, b_ref, o_ref):
        o_ref[...] = _dot_tn(a_ref[...], b_ref[...]).astype(o_ref.dtype)

    if transposed:
        n, tn = dh.shape[1], ROWS_OF_WT
        flat = _mm_call(body, name, (n // tn,), [pl.BlockSpec((s, tn), lambda j: (0, j)), x_spec],
                        pl.BlockSpec((tn, d), lambda j: (j, 0)), jax.ShapeDtypeStruct((n, d), MXU_DTYPE), (dh, x))
        return flat.reshape(N_CHIPS, n // N_CHIPS, d)
    nc = dh.shape[2]
    return _mm_call(body, name, (N_CHIPS,), [x_spec, pl.BlockSpec((None, s, nc), lambda j: (j, 0, 0))],
                    pl.BlockSpec((None, d, nc), lambda j: (j, 0, 0)),
                    jax.ShapeDtypeStruct((N_CHIPS, d, nc), MXU_DTYPE), (x, dh))


def _mm_out_norm(mix, wg, x, g, b, name, target=None, after=None):
    s, k = mix.shape
    _, kc, d = wg.shape
    tm = min(512, s)

    def normed(a_ref, w_ref, x_ref, g_ref, b_ref, z_ref):
        z = DN_ALPHA * x_ref[...] + _dot(a_ref[...], w_ref[...].reshape(N_CHIPS * kc, d))
        mu = jnp.mean(z, axis=1, keepdims=True)
        zc = z - mu
        var = jnp.mean(zc * zc, axis=1, keepdims=True)
        z_ref[...] = z
        return zc * lax.rsqrt(var + LN_EPS) * g_ref[...] + b_ref[...]

    def body(a_ref, w_ref, x_ref, g_ref, b_ref, after_ref, z_ref, o_ref, ob_ref):
        o = normed(a_ref, w_ref, x_ref, g_ref, b_ref, z_ref)
        o_ref[...] = o
        ob_ref[...] = o.astype(ob_ref.dtype)

    def body_with_loss(a_ref, w_ref, x_ref, g_ref, b_ref, t_ref, z_ref, do_ref, l_ref):
        @pl.when(pl.program_id(0) == 0)
        def _():
            l_ref[...] = jnp.zeros_like(l_ref)

        e = normed(a_ref, w_ref, x_ref, g_ref, b_ref, z_ref) - t_ref[...]
        do_ref[...] = e * (1.0 / d)
        l_ref[...] += (0.5 / d) * jnp.sum(jnp.sum(e * e, axis=1, keepdims=True), axis=0, keepdims=True)

    row = pl.BlockSpec((tm, d), lambda i: (i, 0))
    vec = pl.BlockSpec((1, d), lambda i: (0, 0))
    in_specs = [pl.BlockSpec((tm, k), lambda i: (i, 0)), _resident((N_CHIPS, kc, d), lambda i: (0, 0, 0)), row, vec,
                vec]
    args = (mix, wg, x, g.reshape(1, d), b.reshape(1, d))
    f32_rows = jax.ShapeDtypeStruct((s, d), F32)
    if target is None:
        return pl.pallas_call(
            body, name=name, grid=(s // tm,), in_specs=in_specs + [ANY], out_specs=[row, row, row],
            out_shape=[f32_rows, f32_rows, jax.ShapeDtypeStruct((s, d), MXU_DTYPE)],
            compiler_params=_params(("parallel",)))(*args, after)
    return pl.pallas_call(
        body_with_loss, name=name, grid=(s // tm,), in_specs=in_specs + [row],
        out_specs=[row, row, pl.BlockSpec((1, 1), lambda i: (0, 0))],
        out_shape=[f32_rows, f32_rows, jax.ShapeDtypeStruct((1, 1), F32)],
        compiler_params=_params(("arbitrary",)))(*args, target)


NORM_BWD_ROWS = 256


def _norm_out_bwd(dout, z, g, wg, mix, after, name):
    s, d = z.shape
    kc = wg.shape[1]
    tr = min(NORM_BWD_ROWS, s)
    nr = s // tr

    def body(do_ref, z_ref, g_ref, w_ref, mix_ref, after_ref, dz_ref, dg_ref, db_ref, dmix_ref, dw_ref, dzb_scr):
        i = pl.program_id(0)

        @pl.when(i == 0)
        def _():
            dg_ref[...] = jnp.zeros_like(dg_ref)
            db_ref[...] = jnp.zeros_like(db_ref)

        @pl.when(i < nr)
        def _():
            zz = z_ref[...]
            do = do_ref[...]
            mu = jnp.mean(zz, axis=1, keepdims=True)
            zc = zz - mu
            var = jnp.mean(zc * zc, axis=1, keepdims=True)
            rstd = lax.rsqrt(var + LN_EPS)
            zh = zc * rstd
            dg_ref[...] += jnp.sum(do * zh, axis=0, keepdims=True)
            db_ref[...] += jnp.sum(do, axis=0, keepdims=True)
            dzh = do * g_ref[...]
            dz = rstd * (dzh - jnp.mean(dzh, axis=1, keepdims=True) - zh * jnp.mean(dzh * zh, axis=1, keepdims=True))
            dz_ref[...] = dz
            dzb_scr[pl.ds(pl.multiple_of(i * tr, tr), tr), :] = dz.astype(dzb_scr.dtype)

        @pl.when(i >= nr)
        def _():
            dzb = dzb_scr[...]
            dmix_ref[...] = _dot_nt(dzb, w_ref[...])
            dw_ref[...] = _dot_tn(mix_ref[...], dzb).astype(dw_ref.dtype)

    row = pl.BlockSpec((tr, d), lambda i: (jnp.minimum(i, nr - 1), 0))
    vec = pl.BlockSpec((1, d), lambda i: (0, 0))
    shard = lambda i: jnp.maximum(i - nr, 0)
    return pl.pallas_call(
        body, name=name, grid=(nr + N_CHIPS,),
        in_specs=[row, row, vec, pl.BlockSpec((None, kc, d), lambda i: (shard(i), 0, 0)),
                  pl.BlockSpec((s, kc), lambda i: (0, shard(i))), ANY],
        out_specs=[row, vec, vec, pl.BlockSpec((s, kc), lambda i: (0, shard(i))),
                   pl.BlockSpec((None, kc, d), lambda i: (shard(i), 0, 0))],
        out_shape=[jax.ShapeDtypeStruct((s, d), F32), jax.ShapeDtypeStruct((1, d), F32),
                   jax.ShapeDtypeStruct((1, d), F32), jax.ShapeDtypeStruct((s, N_CHIPS * kc), F32),
                   jax.ShapeDtypeStruct((N_CHIPS, kc, d), MXU_DTYPE)],
        scratch_shapes=[pltpu.VMEM((s, d), MXU_DTYPE)],
        compiler_params=_params(("arbitrary",)),
    )(dout, z, g.reshape(1, d), wg, mix, after)


Q0, K0, V0, BG0 = 3072, 4096, 4224, 4352
T = CHUNK


def _lane_ids(width):
    return lax.broadcasted_iota(jnp.int32, (T, width), 1)


def _rope_swap(x):
    w = x.shape[1]
    l64 = _lane_ids(w) % B_HEAD_DIM
    half = ROT_DIM // 2
    return jnp.where(l64 < half, pltpu.roll(x, w - half, 1), jnp.where(l64 < ROT_DIM, pltpu.roll(x, half, 1), 0.0))


def _rope(x, c, s):
    return x * c + _rope_swap(x) * s


def _rope_bwd(d, c, s):
    return d * c + _rope_swap(d * s)


def _tril():
    return lax.broadcasted_iota(jnp.int32, (T, T), 0) >= lax.broadcasted_iota(jnp.int32, (T, T), 1)


def _band_mask(has_prev):
    qi = lax.broadcasted_iota(jnp.int32, (T, 2 * T), 0)
    kj = lax.broadcasted_iota(jnp.int32, (T, 2 * T), 1)
    return (kj > qi) & (kj <= qi + WINDOW) & ((kj >= WINDOW) | has_prev)


def _a_norm(v, g, b):
    mu = jnp.mean(v, axis=1, keepdims=True)
    vc = v - mu
    var = jnp.mean(vc * vc, axis=1, keepdims=True)
    rstd = lax.rsqrt(var + LN_EPS)
    vh = vc * rstd
    return vh, rstd, vh * g + b


def _kv_operands(hk, k_rot, kp_rot, vv, vp):
    lo = _lane_ids(LANES) < B_HEAD_DIM
    mine = lo if hk == 0 else jnp.logical_not(lo)
    both = lambda t: jnp.where(mine, t, 0.0) + pltpu.roll(jnp.where(mine, t, 0.0), B_HEAD_DIM, 1)
    k2 = jnp.concatenate([both(kp_rot), both(k_rot)], axis=0)
    v2 = jnp.concatenate([both(vp), both(vv)], axis=0)
    return k2.astype(MXU_DTYPE), v2.astype(MXU_DTYPE)


GROUP_HEADS = B_Q_HEADS // B_KV_HEADS


def _pair_columns(hk):
    return [(hk * GROUP_HEADS // 2 + pp) * LANES for pp in range(GROUP_HEADS // 2)]


def _stacked_heads(pairs):
    lo = _lane_ids(LANES) < B_HEAD_DIM
    rows = []
    for tile in pairs:
        rows += [jnp.where(lo, tile, 0.0), jnp.where(lo, 0.0, tile)]
    return jnp.concatenate(rows, axis=0).astype(MXU_DTYPE)


def _pair_of(stacked, pp):
    lo = _lane_ids(LANES) < B_HEAD_DIM
    return jnp.where(lo, stacked[2 * pp * T:(2 * pp + 1) * T], stacked[(2 * pp + 1) * T:(2 * pp + 2) * T])


def _softmax_with_sink(qm, k2, valid, sink_ref, head0):
    scores = (_dot_nt(qm, k2) * ATTN_SCALE).reshape(GROUP_HEADS, T, 2 * T)
    s = jnp.where(valid[None], scores, NEG_BIG).reshape(GROUP_HEADS * T, 2 * T)
    sink = jnp.concatenate([jnp.broadcast_to(sink_ref[:, head0 + h:head0 + h + 1], (T, 1))
                            for h in range(GROUP_HEADS)], axis=0)
    m = jnp.maximum(jnp.max(s, axis=1, keepdims=True), sink)
    ex = jnp.exp(s - m)
    es = jnp.exp(sink - m)
    inv = 1.0 / (jnp.sum(ex, axis=1, keepdims=True) + es)
    return ex * inv, es * inv


def _even_specs(s):
    nb = s // T
    prev = lambda r: lambda i: jnp.maximum(r(i) - 1, 0)
    return nb, {
        'h': lambda r: pl.BlockSpec((T, EVEN_IN), lambda i: (r(i), 0)),
        'kprev': lambda r: pl.BlockSpec((T, LANES), lambda i: (prev(r)(i), K0 // LANES)),
        'vprev': lambda r: pl.BlockSpec((T, LANES), lambda i: (prev(r)(i), V0 // LANES)),
        'tab': lambda r: pl.BlockSpec((T, LANES), lambda i: (r(i), 0)),
        'tabprev': lambda r: pl.BlockSpec((T, LANES), lambda i: (prev(r)(i), 0)),
    }


def _full(shape):
    nd = len(shape)
    return pl.BlockSpec(shape, lambda i: (0,) * nd)


def _even_fwd(h, cq, sq, ln_g, ln_b, ws, bs_t, sinks, after):
    s = h.shape[0]
    nb, sp = _even_specs(s)

    def body(h_ref, kp_ref, vp_ref, cq_ref, sq_ref, ckp_ref, skp_ref, g_ref, b_ref, ws_ref, bst_ref, sink_ref,
             after_ref, mix_ref):
        i = pl.program_id(0)
        tril = _tril()
        _, _, vn = _a_norm(h_ref[:, 1024:2048], g_ref[...], b_ref[...])
        vnb = vn.astype(MXU_DTYPE)
        for g in range(A_GROUPS):
            c0, c1 = g * LANES, (g + 1) * LANES
            wg = jnp.where(tril, ws_ref[g], 0.0).astype(MXU_DTYPE)
            mixed = _dot(wg, vnb[:, c0:c1]) + bst_ref[:, g:g + 1]
            sil, _ = _silu_and_grad(h_ref[:, 2048 + c0:2048 + c1])
            mix_ref[:, c0:c1] = (h_ref[:, c0:c1] * mixed * sil).astype(mix_ref.dtype)

        cq, sq = cq_ref[...], sq_ref[...]
        k_rot = _rope(h_ref[:, K0:K0 + LANES], cq, sq)
        kp_rot = _rope(kp_ref[...], ckp_ref[...], skp_ref[...])
        valid = _band_mask(i > 0)
        for hk in range(B_KV_HEADS):
            k2, v2 = _kv_operands(hk, k_rot, kp_rot, h_ref[:, V0:V0 + LANES], vp_ref[...])
            qm = _stacked_heads([_rope(h_ref[:, Q0 + c0:Q0 + c0 + LANES], cq, sq) for c0 in _pair_columns(hk)])
            p, _ = _softmax_with_sink(qm, k2, valid, sink_ref, GROUP_HEADS * hk)
            out = _dot(p.astype(MXU_DTYPE), v2)
            for pp, c0 in enumerate(_pair_columns(hk)):
                sil, _ = _silu_and_grad(h_ref[:, BG0 + c0:BG0 + c0 + LANES])
                mix_ref[:, A_WIDTH + c0:A_WIDTH + c0 + LANES] = (_pair_of(out, pp) * sil).astype(mix_ref.dtype)

    ident = lambda i: i
    return pl.pallas_call(
        body, name="even_fwd", grid=(nb,),
        in_specs=[sp['h'](ident), sp['kprev'](ident), sp['vprev'](ident), sp['tab'](ident), sp['tab'](ident),
                  sp['tabprev'](ident), sp['tabprev'](ident), _full((1, A_WIDTH)), _full((1, A_WIDTH)),
                  _full((A_GROUPS, T, T)), _full((T, A_GROUPS)), _full((1, B_Q_HEADS)), ANY],
        out_specs=pl.BlockSpec((T, 2 * A_WIDTH), lambda i: (i, 0)),
        out_shape=jax.ShapeDtypeStruct((s, 2 * A_WIDTH), MXU_DTYPE),
        compiler_params=_params(("parallel",)),
    )(h, h, h, cq, sq, cq, sq, ln_g.reshape(1, -1), ln_b.reshape(1, -1), ws, bs_t, sinks.reshape(1, -1), after)


def _even_bwd(h, dmix, cq, sq, ln_g, ln_b, ws, bs_t, sinks):
    s = h.shape[0]
    nb, sp = _even_specs(s)

    def body(h_ref, kp_ref, vp_ref, cq_ref, sq_ref, ckp_ref, skp_ref, dm_ref, g_ref, b_ref, ws_ref, bst_ref, sink_ref,
             dh_ref, dg_ref, db_ref, dws_ref, dbst_ref, dsink_ref, dvn_scr, ck_scr, cv_scr):
        i = pl.program_id(0)

        @pl.when(i == 0)
        def _():
            for ref in (dg_ref, db_ref, dws_ref, dbst_ref, dsink_ref, ck_scr, cv_scr):
                ref[...] = jnp.zeros_like(ref)

        tril = _tril()
        vh, rstd, vn = _a_norm(h_ref[:, 1024:2048], g_ref[...], b_ref[...])
        vnb = vn.astype(MXU_DTYPE)
        for g in range(A_GROUPS):
            c0, c1 = g * LANES, (g + 1) * LANES
            wg = jnp.where(tril, ws_ref[g], 0.0).astype(MXU_DTYPE)
            mixed = _dot(wg, vnb[:, c0:c1]) + bst_ref[:, g:g + 1]
            ag = h_ref[:, 2048 + c0:2048 + c1]
            u = h_ref[:, c0:c1]
            sil, dsil = _silu_and_grad(ag)
            da = dm_ref[:, c0:c1]
            dh_ref[:, c0:c1] = (da * mixed * sil).astype(dh_ref.dtype)
            dh_ref[:, 2048 + c0:2048 + c1] = (da * u * mixed * dsil).astype(dh_ref.dtype)
            dmx = da * u * sil
            dmb = dmx.astype(MXU_DTYPE)
            dvn_scr[:, c0:c1] = _dot_tn(wg, dmb)
            dws_ref[g] += jnp.where(tril, _dot_nt(dmb, vnb[:, c0:c1]), 0.0)
            dbst_ref[:, g:g + 1] += jnp.sum(dmx, axis=1, keepdims=True)
        dvn = dvn_scr[...]
        dg_ref[...] += jnp.sum(dvn * vh, axis=0, keepdims=True)
        db_ref[...] += jnp.sum(dvn, axis=0, keepdims=True)
        dvh = dvn * g_ref[...]
        dv = rstd * (dvh - jnp.mean(dvh, axis=1, keepdims=True) - vh * jnp.mean(dvh * vh, axis=1, keepdims=True))
        dh_ref[:, 1024:2048] = dv.astype(dh_ref.dtype)

        cq, sq = cq_ref[...], sq_ref[...]
        k_rot = _rope(h_ref[:, K0:K0 + LANES], cq, sq)
        kp_rot = _rope(kp_ref[...], ckp_ref[...], skp_ref[...])
        valid = _band_mask(i < nb - 1)
        dk_all = jnp.zeros((2 * T, LANES), F32)
        dv_all = jnp.zeros((2 * T, LANES), F32)
        for hk in range(B_KV_HEADS):
            k2, v2 = _kv_operands(hk, k_rot, kp_rot, h_ref[:, V0:V0 + LANES], vp_ref[...])
            cols = _pair_columns(hk)
            qm = _stacked_heads([_rope(h_ref[:, Q0 + c0:Q0 + c0 + LANES], cq, sq) for c0 in cols])
            p, psink = _softmax_with_sink(qm, k2, valid, sink_ref, GROUP_HEADS * hk)
            pb = p.astype(MXU_DTYPE)
            out = _dot(pb, v2)
            dobs = []
            for pp, c0 in enumerate(cols):
                sil, dsil = _silu_and_grad(h_ref[:, BG0 + c0:BG0 + c0 + LANES])
                dbo = dm_ref[:, A_WIDTH + c0:A_WIDTH + c0 + LANES]
                dh_ref[:, BG0 + c0:BG0 + c0 + LANES] = (dbo * _pair_of(out, pp) * dsil).astype(dh_ref.dtype)
                dobs.append(dbo * sil)
            dobm = _stacked_heads(dobs)
            dp = _dot_nt(dobm, v2)
            rs = jnp.sum(p * dp, axis=1, keepdims=True)
            dsb = (p * (dp - rs) * ATTN_SCALE).astype(MXU_DTYPE)
            to_sink = psink * rs
            for h in range(GROUP_HEADS):
                head = GROUP_HEADS * hk + h
                dsink_ref[:, head:head + 1] += -jnp.sum(to_sink[h * T:(h + 1) * T], axis=0, keepdims=True)
            dq = _dot(dsb, k2)
            for pp, c0 in enumerate(cols):
                dh_ref[:, Q0 + c0:Q0 + c0 + LANES] = _rope_bwd(_pair_of(dq, pp), cq, sq).astype(dh_ref.dtype)
            acc_k = _dot_tn(dsb, qm)
            acc_v = _dot_tn(pb, dobm)
            lo2 = lax.broadcasted_iota(jnp.int32, (2 * T, LANES), 1) < B_HEAD_DIM
            mine = lo2 if hk == 0 else jnp.logical_not(lo2)
            dk_all = dk_all + jnp.where(mine, acc_k + pltpu.roll(acc_k, B_HEAD_DIM, 1), 0.0)
            dv_all = dv_all + jnp.where(mine, acc_v + pltpu.roll(acc_v, B_HEAD_DIM, 1), 0.0)
        dk_rot = dk_all[T:2 * T] + ck_scr[...]
        dvv = dv_all[T:2 * T] + cv_scr[...]
        ck_scr[...] = dk_all[0:T]
        cv_scr[...] = dv_all[0:T]
        dh_ref[:, K0:K0 + LANES] = _rope_bwd(dk_rot, cq, sq).astype(dh_ref.dtype)
        dh_ref[:, V0:V0 + LANES] = dvv.astype(dh_ref.dtype)

    rev = lambda i: nb - 1 - i
    return pl.pallas_call(
        body, name="even_bwd", grid=(nb,),
        in_specs=[sp['h'](rev), sp['kprev'](rev), sp['vprev'](rev), sp['tab'](rev), sp['tab'](rev),
                  sp['tabprev'](rev), sp['tabprev'](rev), pl.BlockSpec((T, 2 * A_WIDTH), lambda i: (rev(i), 0)),
                  _full((1, A_WIDTH)), _full((1, A_WIDTH)), _full((A_GROUPS, T, T)), _full((T, A_GROUPS)),
                  _full((1, B_Q_HEADS))],
        out_specs=[sp['h'](rev), _full((1, A_WIDTH)), _full((1, A_WIDTH)),
                   _full((A_GROUPS, T, T)), _full((T, A_GROUPS)), _full((1, B_Q_HEADS))],
        out_shape=[jax.ShapeDtypeStruct((s, EVEN_IN), MXU_DTYPE), jax.ShapeDtypeStruct((1, A_WIDTH), F32),
                   jax.ShapeDtypeStruct((1, A_WIDTH), F32), jax.ShapeDtypeStruct((A_GROUPS, T, T), F32),
                   jax.ShapeDtypeStruct((T, A_GROUPS), F32), jax.ShapeDtypeStruct((1, B_Q_HEADS), F32)],
        scratch_shapes=[pltpu.VMEM((T, A_WIDTH), F32), pltpu.VMEM((T, LANES), F32), pltpu.VMEM((T, LANES), F32)],
        compiler_params=_params(("arbitrary",)),
    )(h, h, h, cq, sq, cq, sq, dmix, ln_g.reshape(1, -1), ln_b.reshape(1, -1), ws, bs_t, sinks.reshape(1, -1))


def _shift_down(x, d, fill=0.0):
    rows = lax.broadcasted_iota(jnp.int32, x.shape, 0)
    return jnp.where(rows >= d, pltpu.roll(x, d, 0), fill)


def _shift_up(x, d, fill=0.0):
    n = x.shape[0]
    rows = lax.broadcasted_iota(jnp.int32, x.shape, 0)
    return jnp.where(rows < n - d, pltpu.roll(x, n - d, 0), fill)


SCAN_BLOCK = 32
SUBLANES = 8


def _scan_block(a, b, up):
    n = a.shape[0]
    d = 1
    while d < n:
        if d < SUBLANES:
            shift = _shift_up if up else _shift_down
            a, b = a * shift(a, d, 1.0), a * shift(b, d) + b
        elif up:
            a, b = (jnp.concatenate([a[:n - d] * a[d:], a[n - d:]], axis=0),
                    jnp.concatenate([a[:n - d] * b[d:] + b[:n - d], b[n - d:]], axis=0))
        else:
            a, b = (jnp.concatenate([a[:d], a[d:] * a[:n - d]], axis=0),
                    jnp.concatenate([b[:d], a[d:] * b[:n - d] + b[d:]], axis=0))
        d *= 2
    return a, b


def _scan(a, b, up):
    n = a.shape[0]
    size = min(SCAN_BLOCK, n)
    blocks = n // size
    out, carry = [None] * blocks, None
    for k in (reversed(range(blocks)) if up else range(blocks)):
        prod, h = _scan_block(a[k * size:(k + 1) * size], b[k * size:(k + 1) * size], up)
        if carry is not None:
            h = h + prod * carry
        carry = h[0:1] if up else h[size - 1:size]
        out[k] = h
    return jnp.concatenate(out, axis=0)


def _neg_expm1(y):
    t = jnp.tanh(-0.5 * y)
    return 2.0 * t / (1.0 + t)


def _softplus(x):
    return jnp.maximum(x, 0.0) + jnp.log1p(jnp.exp(-jnp.abs(x)))


def _lru_forward(xc, cw, cb, wa, ba, wx, bx, lam):
    shifted = [_shift_down(xc, CONV_WIDTH - 1 - j) if j < CONV_WIDTH - 1 else xc for j in range(CONV_WIDTH)]
    xconv = cb
    for j in range(CONV_WIDTH):
        xconv = xconv + cw[j:j + 1, :] * shifted[j]
    xb = xconv.astype(MXU_DTYPE)
    r = _sigmoid(_dot(xb, wa.astype(MXU_DTYPE)) + ba)
    ig = _sigmoid(_dot(xb, wx.astype(MXU_DTYPE)) + bx)
    sp = _softplus(-lam)
    log_a = -LRU_C * r * sp
    a = jnp.exp(log_a)
    mult = jnp.sqrt(_neg_expm1(2.0 * log_a))
    hs = _scan(a, mult * ig * xconv, up=False)
    return dict(shifted=shifted, xconv=xconv, xb=xb, r=r, ig=ig, sp=sp, a=a, mult=mult, hs=hs)


def _col(width, off):
    return lambda s: pl.BlockSpec((s, width), lambda i: (0, off + i))


def _part(width, slot):
    return lambda s: pl.BlockSpec((None, s, width), lambda i: (slot, 0, i))


def _lru_specs(s):
    w = LANES
    return [_part(w, 0)(s), _part(w, 1)(s), pl.BlockSpec((CONV_WIDTH, w), lambda i: (0, i)),
            pl.BlockSpec((1, w), lambda i: (0, i)), pl.BlockSpec((None, w, w), lambda i: (i, 0, 0)),
            pl.BlockSpec((1, w), lambda i: (0, i)), pl.BlockSpec((None, w, w), lambda i: (i, 0, 0)),
            pl.BlockSpec((1, w), lambda i: (0, i)), pl.BlockSpec((1, w), lambda i: (0, i))]


def _lru_fwd(h, cw, cb, wa, ba, wx, bx, lam):
    s = h.shape[1]

    def body(xc_ref, cg_ref, cw_ref, cb_ref, wa_ref, ba_ref, wx_ref, bx_ref, lam_ref, o_ref):
        f = _lru_forward(xc_ref[...], cw_ref[...], cb_ref[...], wa_ref[...], ba_ref[...], wx_ref[...], bx_ref[...],
                         lam_ref[...])
        sil, _ = _silu_and_grad(cg_ref[...])
        o_ref[...] = (f['hs'] * sil).astype(o_ref.dtype)

    return pl.pallas_call(
        body, name="lru_fwd", grid=(C_HEADS,), in_specs=_lru_specs(s), out_specs=_col(LANES, 0)(s),
        out_shape=jax.ShapeDtypeStruct((s, C_WIDTH + D_WIDTH), MXU_DTYPE), compiler_params=_params(("parallel",)),
    )(h, h, cw, cb.reshape(1, -1), wa, ba.reshape(1, -1), wx, bx.reshape(1, -1), lam.reshape(1, -1))


def _lru_bwd(h, dmix, cw, cb, wa, ba, wx, bx, lam):
    s = h.shape[1]

    def body(xc_ref, cg_ref, cw_ref, cb_ref, wa_ref, ba_ref, wx_ref, bx_ref, lam_ref, dco_ref,
             dh_ref, dcw_ref, dcb_ref, dwa_ref, dba_ref, dwx_ref, dbx_ref, dlam_ref):
        dxc_ref, dcg_ref = dh_ref.at[0], dh_ref.at[1]
        cw, lam = cw_ref[...], lam_ref[...]
        wab, wxb = wa_ref[...].astype(MXU_DTYPE), wx_ref[...].astype(MXU_DTYPE)
        f = _lru_forward(xc_ref[...], cw, cb_ref[...], wa_ref[...], ba_ref[...], wx_ref[...], bx_ref[...], lam)
        sil, dsil = _silu_and_grad(cg_ref[...])
        dco = dco_ref[...]
        hs, a, mult, ig, r, xconv = f['hs'], f['a'], f['mult'], f['ig'], f['r'], f['xconv']
        dcg_ref[...] = (dco * hs * dsil).astype(dcg_ref.dtype)
        lamb = _scan(_shift_up(a, 1), dco * sil, up=True)
        da = lamb * _shift_down(hs, 1)
        d_ig = lamb * mult * xconv
        d_mult = lamb * ig * xconv
        dxconv = lamb * mult * ig
        d_log_a = da * a - d_mult * (a * a) / mult
        dlam_ref[...] = jnp.sum(d_log_a * r, axis=0, keepdims=True) * LRU_C * _sigmoid(-lam)
        dpa = d_log_a * (-LRU_C * f['sp']) * r * (1.0 - r)
        dpx = d_ig * ig * (1.0 - ig)
        dba_ref[...] = jnp.sum(dpa, axis=0, keepdims=True)
        dbx_ref[...] = jnp.sum(dpx, axis=0, keepdims=True)
        dpab, dpxb = dpa.astype(MXU_DTYPE), dpx.astype(MXU_DTYPE)
        dwa_ref[...] = _dot_tn(f['xb'], dpab)
        dwx_ref[...] = _dot_tn(f['xb'], dpxb)
        dxconv = dxconv + _dot_nt(dpab, wab) + _dot_nt(dpxb, wxb)
        dcb_ref[...] = jnp.sum(dxconv, axis=0, keepdims=True)
        dxc = jnp.zeros_like(dxconv)
        for j in range(CONV_WIDTH):
            dcw_ref[j:j + 1, :] = jnp.sum(dxconv * f['shifted'][j], axis=0, keepdims=True)
            back = _shift_up(dxconv, CONV_WIDTH - 1 - j) if j < CONV_WIDTH - 1 else dxconv
            dxc = dxc + cw[j:j + 1, :] * back
        dxc_ref[...] = dxc.astype(dxc_ref.dtype)

    w = LANES
    vec = pl.BlockSpec((1, w), lambda i: (0, i))
    mat = pl.BlockSpec((None, w, w), lambda i: (i, 0, 0))
    vshape = jax.ShapeDtypeStruct((1, C_WIDTH), F32)
    mshape = jax.ShapeDtypeStruct((C_HEADS, w, w), F32)
    return pl.pallas_call(
        body, name="lru_bwd", grid=(C_HEADS,), in_specs=_lru_specs(s) + [_col(w, 0)(s)],
        out_specs=[pl.BlockSpec((2, s, w), lambda i: (0, 0, i)), pl.BlockSpec((CONV_WIDTH, w), lambda i: (0, i)),
                   vec, mat, vec, mat, vec, vec],
        out_shape=[jax.ShapeDtypeStruct((N_CHIPS, s, C_WIDTH), MXU_DTYPE),
                   jax.ShapeDtypeStruct((CONV_WIDTH, C_WIDTH), F32), vshape, mshape, vshape, mshape, vshape, vshape],
        compiler_params=_params(("parallel",)),
    )(h, h, cw, cb.reshape(1, -1), wa, ba.reshape(1, -1), wx, bx.reshape(1, -1), lam.reshape(1, -1), dmix)


def _pool_window(g):
    return jnp.left_shift(2, g).astype(F32)


def _select_window(g, sums):
    out = sums[-1]
    for k in range(len(sums) - 2, -1, -1):
        out = jnp.where(g == k, sums[k], out)
    return out


def _window_sums(x, shift):
    sums, cur = [], x
    for k in range(D_GROUPS):
        cur = cur + shift(cur, 2 ** k)
        sums.append(cur)
    return sums


def _pool_specs(s):
    w = D_GROUP_DIM
    return [_part(w, 2)(s), _part(w, 3)(s),
            pl.BlockSpec((None, w, w), lambda i: (i, 0, 0)), pl.BlockSpec((1, w), lambda i: (0, i))]


def _pool_forward(xd, g):
    rows = lax.broadcasted_iota(jnp.int32, xd.shape, 0).astype(F32) + 1.0
    cnt = jnp.minimum(rows, _pool_window(g))
    pooled = _select_window(g, _window_sums(xd, _shift_down)) / cnt - xd
    return pooled, cnt


def _pool_fwd(h, mix, w_pool, d_scale):
    s = h.shape[1]

    def body(xd_ref, dg_ref, wp_ref, sc_ref, mix_ref, o_ref):
        pooled, _ = _pool_forward(xd_ref[...], pl.program_id(0))
        mixed = _dot(pooled.astype(MXU_DTYPE), wp_ref[...].astype(MXU_DTYPE))
        sil, _ = _silu_and_grad(dg_ref[...])
        o_ref[...] = (mixed * sc_ref[...] * sil).astype(o_ref.dtype)

    return pl.pallas_call(
        body, name="pool_fwd", grid=(D_GROUPS,), in_specs=_pool_specs(s) + [ANY],
        out_specs=_col(D_GROUP_DIM, C_WIDTH // D_GROUP_DIM)(s), out_shape=jax.ShapeDtypeStruct(mix.shape, mix.dtype),
        input_output_aliases={4: 0}, compiler_params=_params(("parallel",)),
    )(h, h, w_pool, d_scale.reshape(1, -1), mix)


def _pool_bwd(h, dmix, dh, w_pool, d_scale):
    s = h.shape[1]
    w = D_GROUP_DIM

    def body(xd_ref, dg_ref, wp_ref, sc_ref, ddo_ref, dh_in_ref, dh_ref, dwp_ref, dsc_ref):
        dxd_ref, ddg_ref = dh_ref.at[0], dh_ref.at[1]
        g = pl.program_id(0)
        pooled, cnt = _pool_forward(xd_ref[...], g)
        pb = pooled.astype(MXU_DTYPE)
        wpb = wp_ref[...].astype(MXU_DTYPE)
        mixed = _dot(pb, wpb)
        sil, dsil = _silu_and_grad(dg_ref[...])
        ddo = ddo_ref[...]
        sc = sc_ref[...]
        ddg_ref[...] = (ddo * mixed * sc * dsil).astype(ddg_ref.dtype)
        dms = ddo * sil
        dsc_ref[...] = jnp.sum(dms * mixed, axis=0, keepdims=True)
        dmb = (dms * sc).astype(MXU_DTYPE)
        dwp_ref[...] = _dot_tn(pb, dmb)
        dpooled = _dot_nt(dmb, wpb)
        dxd = _select_window(g, _window_sums(dpooled / cnt, _shift_up)) - dpooled
        dxd_ref[...] = dxd.astype(dxd_ref.dtype)

    return pl.pallas_call(
        body, name="pool_bwd", grid=(D_GROUPS,), in_specs=_pool_specs(s) + [_col(w, C_WIDTH // w)(s), ANY],
        out_specs=[pl.BlockSpec((2, s, w), lambda i: (1, 0, i)), pl.BlockSpec((None, w, w), lambda i: (i, 0, 0)),
                   pl.BlockSpec((1, w), lambda i: (0, i))],
        out_shape=[jax.ShapeDtypeStruct(dh.shape, dh.dtype),
                   jax.ShapeDtypeStruct((D_GROUPS, w, w), F32), jax.ShapeDtypeStruct((1, D_WIDTH), F32)],
        input_output_aliases={5: 0}, compiler_params=_params(("parallel",)),
    )(h, h, w_pool, d_scale.reshape(1, -1), dmix, dh)


def _rope_tables(positions):
    s = positions.shape[0]
    inv_freq = ROPE_THETA ** (-jnp.arange(0, ROT_DIM, 2, dtype=F32) / ROT_DIM)
    ang = positions.astype(F32)[:, None] * inv_freq
    c, sn = jnp.cos(ang), jnp.sin(ang)
    rest = B_HEAD_DIM - ROT_DIM
    c64 = jnp.concatenate([c, c, jnp.ones((s, rest), F32)], axis=1)
    s64 = jnp.concatenate([-sn, sn, jnp.zeros((s, rest), F32)], axis=1)
    return jnp.tile(c64, (1, LANES // B_HEAD_DIM)), jnp.tile(s64, (1, LANES // B_HEAD_DIM))


def _local_step(x, positions, target, w, projections_of, projections_ahead, projection_grads, layer_grads_begin):
    cq, sq = _rope_tables(positions)
    saved = []
    xin, xin_b = x, x.astype(MXU_DTYPE)
    for layer in range(DEPTH):
        j = layer // 2
        kind = 'even' if layer % 2 == 0 else 'odd'
        w_in, w_out = projections_of(layer, xin_b)
        h = _mm_in(xin_b, w_in, W_IN_TRANSPOSED[kind], "mm_" + kind + "_in")
        if kind == 'even':
            behind, w_out_after = w_out(h) if callable(w_out) else (h, None)
            mix = _even_fwd(h, cq, sq, w['even_a_ln_g'][j], w['even_a_ln_b'][j], w['even_a_ws'][j],
                            w['even_a_bs'][j].T, w['even_b_sinks'][j], behind)
            w_out = w_out_after(mix) if w_out_after else w_out
        else:
            mix = _lru_fwd(h, w['odd_conv_w'][j], w['odd_conv_b'][j], w['odd_w_a'][j], w['odd_b_a'][j],
                           w['odd_w_x'][j], w['odd_b_x'][j], w['odd_lam'][j])
            mix = _pool_fwd(h, mix, w['odd_w_pool'][j], w['odd_d_scale'][j])
        if layer == DEPTH - 1:
            z, out, out_b = _mm_out_norm(mix, w_out, xin, w[kind + '_ln_g'][j], w[kind + '_ln_b'][j],
                                         "mm_out_norm_loss", target=target)
        else:
            z, out, out_b = _mm_out_norm(mix, w_out, xin, w[kind + '_ln_g'][j], w[kind + '_ln_b'][j], "mm_out_norm",
                                         after=projections_ahead(layer + 1, mix))
        saved.append((xin_b, h, mix, z, w_in, w_out))
        xin, xin_b = out, out_b

    dout, loss = xin, xin_b
    per_layer = {n: [None, None] for n in SMALL_NAMES}
    for layer in reversed(range(DEPTH)):
        j = layer // 2
        kind = 'even' if layer % 2 == 0 else 'odd'
        xin_b, h, mix, z, w_in, w_out = saved[layer]
        dz, dg, db, dmix, dw_out = _norm_out_bwd(dout, z, w[kind + '_ln_g'][j], w_out, mix,
                                                 layer_grads_begin(layer, dout), "norm_out_bwd")
        per_layer[kind + '_ln_g'][j], per_layer[kind + '_ln_b'][j] = dg[0], db[0]
        if kind == 'even':
            dh, dlg, dlb, dws, dbst, dsink = _even_bwd(
                h, dmix, cq, sq, w['even_a_ln_g'][j], w['even_a_ln_b'][j], w['even_a_ws'][j], w['even_a_bs'][j].T,
                w['even_b_sinks'][j])
            for n, val in (('even_a_ln_g', dlg[0]), ('even_a_ln_b', dlb[0]), ('even_a_ws', dws),
                           ('even_a_bs', dbst.T), ('even_b_sinks', dsink[0])):
                per_layer[n][j] = val
        else:
            dh, dcw, dcb, dwa, dba, dwx, dbx, dlam = _lru_bwd(
                h, dmix, w['odd_conv_w'][j], w['odd_conv_b'][j], w['odd_w_a'][j], w['odd_b_a'][j], w['odd_w_x'][j],
                w['odd_b_x'][j], w['odd_lam'][j])
            dh, dwp, dsc = _pool_bwd(h, dmix, dh, w['odd_w_pool'][j], w['odd_d_scale'][j])
            for n, val in (('odd_conv_w', dcw), ('odd_conv_b', dcb[0]), ('odd_w_a', dwa), ('odd_b_a', dba[0]),
                           ('odd_w_x', dwx), ('odd_b_x', dbx[0]), ('odd_lam', dlam[0]), ('odd_w_pool', dwp),
                           ('odd_d_scale', dsc[0])):
                per_layer[n][j] = val
        dw_in = _mm_dw_in(xin_b, dh, W_IN_TRANSPOSED[kind], "mm_dw_in_" + kind)
        dout = _mm_dx(dh, w_in, dz, W_IN_TRANSPOSED[kind], projection_grads(layer, dw_in, dw_out), "mm_dx_" + kind)
    return loss, dout, {n: jnp.stack(v) for n, v in per_layer.items()}


ANY = pl.BlockSpec(memory_space=pl.ANY)


def _where_am_i():
    return lax.axis_index("x"), lax.axis_index("y"), lax.axis_index("c")


def _other_chips(x, y):
    return [(1 - x, y), (x, 1 - y), (1 - x, 1 - y)]


PACKED_ROWS = 16


def _half_rows(rows, which):
    half = rows // 2
    assert half % PACKED_ROWS == 0, rows
    return pl.ds(pl.multiple_of(which * half, PACKED_ROWS), half)


def _sibling_swap(bufs, name, after=()):
    n, na = len(bufs), len(after)

    def body(*refs):
        x_refs, out_refs, (send_sems, recv_sems) = refs[:n], refs[n + na:2 * n + na], refs[2 * n + na:]
        x, y, c = _where_am_i()
        copies = []
        for p in range(n):
            copies.append(pltpu.make_async_remote_copy(
                src_ref=x_refs[p], dst_ref=out_refs[p], send_sem=send_sems.at[p], recv_sem=recv_sems.at[p],
                device_id=(x, y, 1 - c), device_id_type=MESH))
            copies[-1].start()
        for cp in copies:
            cp.wait()

    return pl.pallas_call(
        body, name=name, in_specs=[ANY] * (n + na), out_specs=[ANY] * n,
        out_shape=[jax.ShapeDtypeStruct(b.shape, b.dtype) for b in bufs],
        scratch_shapes=[pltpu.SemaphoreType.DMA((n,)), pltpu.SemaphoreType.DMA((n,))],
    )(*bufs, *after)


HBM = pl.BlockSpec(memory_space=pltpu.HBM)
SEM = pl.BlockSpec(memory_space=pltpu.SEMAPHORE)
IN_FLIGHT = pltpu.CompilerParams(has_side_effects=pltpu.SideEffectType.DATAFLOW_SIDE_EFFECTING)


IN_PLACE = ('forward',)


def _peers(kind, x, y, c):
    if kind in ('swap', 'join'):
        return [(x, y, 1 - c)]
    return [(x, y, 1 - c) if kind == 'forward' else (px, py, c) for px, py in _other_chips(x, y)]


def _n_peers(kind):
    return 1 if kind in ('swap', 'join') else N_CHIPS - 1


def _land_shape(kind, a):
    if kind == 'gather':
        return (N_CHIPS,) + a.shape
    return (a.shape[0], a.shape[1] // 2, a.shape[2]) if kind == 'swap' else a.shape


def _ici_copy(kind, src_ref, land_ref, send_sems, recv_sems, k, peer, x, y, c, arriving=False):
    me = 2 * x + y
    if kind == 'gather':
        half = _half_rows(src_ref.shape[0], c)
        src, dst = src_ref.at[half, :], land_ref.at[me, half, :]
    elif kind == 'scatter':
        src, dst = src_ref.at[2 * peer[0] + peer[1]], land_ref.at[me]
    elif kind == 'swap':
        src, dst = src_ref.at[:, _half_rows(src_ref.shape[1], 1 - c), :], land_ref
    elif kind == 'join':
        src, dst = src_ref, land_ref
    else:
        px, py = _other_chips(x, y)[k % (N_CHIPS - 1)]
        rows = src_ref.at[2 * px + py, _half_rows(src_ref.shape[1], 1 - c if arriving else c), :]
        src, dst = rows, rows
    return pltpu.make_async_remote_copy(src_ref=src, dst_ref=dst, send_sem=send_sems.at[k], recv_sem=recv_sems.at[k],
                                        device_id=peer, device_id_type=MESH)


def _ici_start(kind, groups, name, after=()):
    flat = [a for g in groups for a in g]
    n, ng, np_, na = len(flat), len(groups), _n_peers(kind), len(after)
    nl = 0 if kind in IN_PLACE else n
    land_shapes = [_land_shape(kind, a) for a in flat[:nl]]

    def body(*refs):
        srcs, sems, token = refs[:n], refs[n + nl + na:n + nl + na + 2 * ng], refs[-1]
        lands = refs[n:n + nl] if nl else srcs
        x, y, c = _where_am_i()
        p = 0
        for gi, g in enumerate(groups):
            for q in range(len(g)):
                for k, peer in enumerate(_peers(kind, x, y, c)):
                    _ici_copy(kind, srcs[p], lands[p], sems[2 * gi], sems[2 * gi + 1], np_ * q + k, peer, x, y,
                              c).start()
                p += 1
        token[...] = jnp.zeros_like(token)

    sem_shapes = [pltpu.SemaphoreType.DMA((np_ * len(g),)) for g in groups for _ in range(2)]
    outs = pl.pallas_call(
        body, name=name, in_specs=[HBM] * (n + nl) + [ANY] * na,
        out_specs=[SEM] * (2 * ng) + [HBM] * (n + nl) + [pl.BlockSpec(memory_space=pltpu.VMEM)],
        out_shape=sem_shapes + [pltpu.HBM(a.shape, a.dtype) for a in flat]
        + [pltpu.HBM(s, a.dtype) for s, a in zip(land_shapes, flat)] + [jax.ShapeDtypeStruct((8, LANES), F32)],
        input_output_aliases={i: 2 * ng + i for i in range(n + nl)}, compiler_params=IN_FLIGHT,
    )(*[pltpu.with_memory_space_constraint(a, pltpu.HBM) for a in flat],
      *[pltpu.with_memory_space_constraint(lax.empty(s, a.dtype), pltpu.HBM) for s, a in zip(land_shapes, flat)],
      *after)
    sems, srcs, token = outs[:2 * ng], outs[2 * ng:2 * ng + n], outs[-1]
    lands = outs[2 * ng + n:2 * ng + n + nl] if nl else srcs
    started, p = [], 0
    for gi, g in enumerate(groups):
        started.append((sems[2 * gi], sems[2 * gi + 1], srcs[p:p + len(g)], lands[p:p + len(g)]))
        p += len(g)
    return started, token


def _ici_wait(kind, started, after, name):
    send_sems, recv_sems, srcs, lands = started
    n, np_ = len(srcs), _n_peers(kind)
    nl = 0 if kind in IN_PLACE else n
    after = list(after)

    def body(*refs):
        src_refs, send_ref, recv_ref = refs[:n], refs[n + nl], refs[n + nl + 1]
        land_refs = refs[n:n + nl] if nl else src_refs
        x, y, c = _where_am_i()
        for q in range(n):
            for k, peer in enumerate(_peers(kind, x, y, c)):
                args = (kind, src_refs[q], land_refs[q], send_ref, recv_ref, np_ * q + k, peer, x, y, c)
                _ici_copy(*args).wait_send()
                _ici_copy(*args, arriving=True).wait_recv()

    arrays = list(srcs) + (list(lands) if nl else [])
    outs = pl.pallas_call(
        body, name=name, in_specs=[HBM] * (n + nl) + [SEM, SEM] + [ANY] * len(after), out_specs=[HBM] * (n + nl),
        out_shape=[pltpu.HBM(a.shape, a.dtype) for a in arrays],
        input_output_aliases={i: i for i in range(n + nl)}, compiler_params=IN_FLIGHT,
    )(*arrays, send_sems, recv_sems, *after)
    return outs[:n], (outs[n:] if nl else outs[:n])


def _forward_to_sibling(lands, name):
    n = len(lands)

    def body(*refs):
        out_refs, (send_sems, recv_sems) = refs[n:2 * n], refs[2 * n:]
        x, y, c = _where_am_i()

        def copy(q, k, peer, half_of):
            rows = out_refs[q].at[2 * peer[0] + peer[1], _half_rows(lands[q].shape[1], half_of), :]
            return pltpu.make_async_remote_copy(
                src_ref=rows, dst_ref=rows, send_sem=send_sems.at[3 * q + k], recv_sem=recv_sems.at[3 * q + k],
                device_id=(x, y, 1 - c), device_id_type=MESH)

        sends = [copy(q, k, peer, c) for q in range(n) for k, peer in enumerate(_other_chips(x, y))]
        for cp in sends:
            cp.start()
        for q in range(n):
            for k, peer in enumerate(_other_chips(x, y)):
                copy(q, k, peer, 1 - c).wait_recv()
        for cp in sends:
            cp.wait_send()

    return pl.pallas_call(
        body, name=name, in_specs=[ANY] * n, out_specs=[ANY] * n,
        out_shape=[jax.ShapeDtypeStruct(a.shape, a.dtype) for a in lands],
        input_output_aliases={i: i for i in range(n)},
        scratch_shapes=[pltpu.SemaphoreType.DMA((3 * n,)), pltpu.SemaphoreType.DMA((3 * n,))],
    )(*lands)


def _with_own_slot(lands, own, chip):
    return [lax.dynamic_update_slice(land, mine[None], (chip, 0, 0)) for land, mine in zip(lands, own)]


def _row_tile(rows, row_bytes, n_bufs, budget=24 * 1024 * 1024):
    best = None
    for t in range(PACKED_ROWS, rows + 1, PACKED_ROWS):
        if rows % t == 0 and 2 * n_bufs * t * row_bytes <= budget:
            best = t
    assert best is not None, (rows, row_bytes)
    return best


def _add_halves(full, recv, core, out_dtype, name):
    n, h, cols = recv.shape
    tr = _row_tile(h, cols * 4, 3)
    nt = h // tr

    def body(core_ref, a_ref, b_ref, o_ref):
        o_ref[...] = (a_ref[...].astype(F32) + b_ref[...].astype(F32)).astype(out_dtype)

    blk = (None, tr, cols)
    return pl.pallas_call(
        body, name=name,
        grid_spec=pltpu.PrefetchScalarGridSpec(
            num_scalar_prefetch=1, grid=(n, nt),
            in_specs=[pl.BlockSpec(blk, lambda j, i, core_ref: (j, core_ref[0] * nt + i, 0)),
                      pl.BlockSpec(blk, lambda j, i, core_ref: (j, i, 0))],
            out_specs=pl.BlockSpec(blk, lambda j, i, core_ref: (j, i, 0))),
        out_shape=jax.ShapeDtypeStruct((n, h, cols), out_dtype),
        compiler_params=_params(("parallel", "parallel")),
    )(core.reshape(1), full, recv)


def _add2(a, b, name):
    r, cols = a.shape
    tr = _row_tile(r, cols * 4, 3)

    def body(a_ref, b_ref, o_ref):
        o_ref[...] = a_ref[...] + b_ref[...]

    blk = pl.BlockSpec((tr, cols), lambda i: (i, 0))
    return pl.pallas_call(body, name=name, grid=(r // tr,), in_specs=[blk, blk], out_specs=blk,
                          out_shape=jax.ShapeDtypeStruct(a.shape, a.dtype), compiler_params=_params(("parallel",)))(a, b)


def _sum_slots(parts, own, chip, name):
    n, h, cols = parts.shape
    tr = _row_tile(h, cols * 4, n + 2)

    def body(chip_ref, p_ref, own_ref, o_ref):
        acc = None
        for k in range(n):
            term = jnp.where(chip_ref[0] == k, own_ref[...], p_ref[k]).astype(F32)
            acc = term if acc is None else acc + term
        o_ref[...] = acc

    own_spec = (pl.BlockSpec((None, tr, cols), lambda i, chip_ref: (chip_ref[0], i, 0)) if own.ndim == 3
                else pl.BlockSpec((tr, cols), lambda i, chip_ref: (i, 0)))
    return pl.pallas_call(
        body, name=name,
        grid_spec=pltpu.PrefetchScalarGridSpec(
            num_scalar_prefetch=1, grid=(h // tr,),
            in_specs=[pl.BlockSpec((n, tr, cols), lambda i, chip_ref: (0, i, 0)), own_spec],
            out_specs=pl.BlockSpec((tr, cols), lambda i, chip_ref: (i, 0))),
        out_shape=jax.ShapeDtypeStruct((h, cols), F32), compiler_params=_params(("parallel",)),
    )(chip.reshape(1), parts, own)


def _adamw(w, g, m, v, name):
    shape = w.shape
    cols = shape[-1]
    rows = math.prod(shape[:-1])
    tr = rows
    if rows % 8 == 0:
        tr = 8
        for t in range(8, rows + 1, 8):
            if rows % t == 0 and t * cols * 4 <= 1536 * 1024:
                tr = t
    c1 = 1.0 - ADAM_B1 ** ADAM_STEP
    c2 = 1.0 - ADAM_B2 ** ADAM_STEP

    def body(w_ref, g_ref, m_ref, v_ref, d_ref, nm_ref, nv_ref):
        gg = g_ref[...]
        nm = ADAM_B1 * m_ref[...] + (1.0 - ADAM_B1) * gg
        nv = ADAM_B2 * v_ref[...] + (1.0 - ADAM_B2) * (gg * gg)
        d_ref[...] = -ADAM_LR * ((nm / c1) / (jnp.sqrt(nv / c2) + ADAM_EPS) + ADAM_WD * w_ref[...])
        nm_ref[...] = nm
        nv_ref[...] = nv

    blk = pl.BlockSpec((tr, cols), lambda i: (i, 0))
    flat = jax.ShapeDtypeStruct((rows, cols), F32)
    outs = pl.pallas_call(
        body, name=name, grid=(rows // tr,), in_specs=[blk] * 4, out_specs=[blk] * 3, out_shape=[flat] * 3,
        compiler_params=_params(("parallel",)),
    )(*[a.reshape(rows, cols) for a in (w, g, m, v)])
    return [o.reshape(shape) for o in outs]


def _adamw_update(w, g, m, v):
    c1 = 1.0 - ADAM_B1 ** ADAM_STEP
    c2 = 1.0 - ADAM_B2 ** ADAM_STEP
    nm = ADAM_B1 * m + (1.0 - ADAM_B1) * g
    nv = ADAM_B2 * v + (1.0 - ADAM_B2) * (g * g)
    return -ADAM_LR * ((nm / c1) / (jnp.sqrt(nv / c2) + ADAM_EPS) + ADAM_WD * w), nm, nv


def _adamw_projection(w, m, v, mine, theirs, core, name):
    _, r, cols = w.shape
    h = r // 2
    tr = _row_tile(h, cols * 4, 12, budget=40 * 1024 * 1024)
    nt = h // tr

    def body(core_ref, w_ref, m_ref, v_ref, a0, b0, a1, b1, g_ref, d_ref, nm_ref, nv_ref):
        layer, half = pl.program_id(0), pl.program_id(1)
        own = half == core_ref[0]
        g = jnp.where(layer == 0, jnp.where(own, a0[...], b0[...]), jnp.where(own, a1[...], b1[...]))
        g_ref[...] = g
        d_ref[...], nm_ref[...], nv_ref[...] = _adamw_update(w_ref[...], g, m_ref[...], v_ref[...])

    def piece(layer, own):
        def index(l, hh, i, core_ref):
            used = (l == layer) & ((hh == core_ref[0]) == own)
            return (jnp.where(used, i, 0), 0)
        return pl.BlockSpec((tr, cols), index)

    whole = pl.BlockSpec((None, tr, cols), lambda l, hh, i, core_ref: (l, hh * nt + i, 0))
    return pl.pallas_call(
        body, name=name,
        grid_spec=pltpu.PrefetchScalarGridSpec(
            num_scalar_prefetch=1, grid=(2, 2, nt),
            in_specs=[whole] * 3 + [piece(0, True), piece(0, False), piece(1, True), piece(1, False)],
            out_specs=[whole] * 4),
        out_shape=[jax.ShapeDtypeStruct(w.shape, F32)] * 4,
        compiler_params=_params(("arbitrary", "arbitrary", "arbitrary")),
    )(core.reshape(1), w, m, v, mine[0], theirs[0], mine[1], theirs[1])


def _chip_shape(name):
    shape = list(FULL_SHAPES[name])
    axis = BIG_SHARD_AXIS.get(name, SMALL_SHARD_AXIS.get(name))
    if axis is not None:
        shape[axis] //= N_CHIPS
    return tuple(shape)


def _round_up(n, k):
    return -(-n // k) * k


SHARDED_SMALL_WORDS = sum(math.prod(_chip_shape(n)) for n in SMALL_SHARDED)
SHARDED_SMALL_ROWS = _round_up(-(-SHARDED_SMALL_WORDS // PACK_COLS), 32)
SMALL_MATRICES = ['even_a_ws', 'odd_w_a', 'odd_w_x', 'odd_w_pool']
SMALL_VECTORS = [n for n in SMALL_NAMES if n not in SMALL_MATRICES]
VECTOR_ROWS = _round_up(-(-(sum(math.prod(FULL_SHAPES[n]) for n in SMALL_VECTORS) + 1) // PACK_COLS), 32)


def _pack_rows(flat, rows):
    return jnp.pad(flat, (0, rows * PACK_COLS - flat.shape[0])).reshape(rows, PACK_COLS)


def _pack_small_sharded(shards):
    return _pack_rows(jnp.concatenate([shards[n].reshape(-1) for n in SMALL_SHARDED]), SHARDED_SMALL_ROWS)


def _unpack_small_sharded(gathered_small):
    out = {}
    words = gathered_small.reshape(N_CHIPS, -1)
    w0 = 0
    for n in SMALL_SHARDED:
        size = math.prod(_chip_shape(n))
        blocks = words[:, w0:w0 + size].reshape((N_CHIPS,) + _chip_shape(n))
        out[n] = jnp.concatenate([blocks[j] for j in range(N_CHIPS)], axis=SMALL_SHARD_AXIS[n])
        w0 += size
    return out


def _small_grad_parts(grads, loss):
    words = [grads[n].reshape(-1) for n in SMALL_VECTORS] + [loss.reshape(-1)]
    return ([_pack_rows(jnp.concatenate(words), VECTOR_ROWS)]
            + [grads[n].reshape(-1, FULL_SHAPES[n][-1]) for n in SMALL_MATRICES])


def _own_block(name, full, chip):
    if name not in SMALL_SHARD_AXIS:
        return full
    axis = SMALL_SHARD_AXIS[name]
    width = FULL_SHAPES[name][axis] // N_CHIPS
    return lax.dynamic_slice_in_dim(full, chip * width, width, axis)


def _from_small_grad_parts(totals, chip):
    flat = totals[0].reshape(-1)
    out = {}
    w0 = 0
    for n in SMALL_VECTORS:
        size = math.prod(FULL_SHAPES[n])
        out[n] = _own_block(n, flat[w0:w0 + size].reshape(FULL_SHAPES[n]), chip)
        w0 += size
    for n, total in zip(SMALL_MATRICES, totals[1:]):
        out[n] = _own_block(n, total.reshape(FULL_SHAPES[n]), chip)
    return out, flat[w0]


def kernel(x, positions, even_w_in, even_a_ln_g, even_a_ln_b, even_a_ws, even_a_bs, even_b_sinks, even_w_out, even_ln_g, even_ln_b, odd_w_in, odd_conv_w, odd_conv_b, odd_w_a, odd_b_a, odd_w_x, odd_b_x, odd_lam, odd_w_pool, odd_d_scale, odd_w_out, odd_ln_g, odd_ln_b, loss_target, m_even_w_in, m_even_a_ln_g, m_even_a_ln_b, m_even_a_ws, m_even_a_bs, m_even_b_sinks, m_even_w_out, m_even_ln_g, m_even_ln_b, m_odd_w_in, m_odd_conv_w, m_odd_conv_b, m_odd_w_a, m_odd_b_a, m_odd_w_x, m_odd_b_x, m_odd_lam, m_odd_w_pool, m_odd_d_scale, m_odd_w_out, m_odd_ln_g, m_odd_ln_b, v_even_w_in, v_even_a_ln_g, v_even_a_ln_b, v_even_a_ws, v_even_a_bs, v_even_b_sinks, v_even_w_out, v_even_ln_g, v_even_ln_b, v_odd_w_in, v_odd_conv_w, v_odd_conv_b, v_odd_w_a, v_odd_b_a, v_odd_w_x, v_odd_b_x, v_odd_lam, v_odd_w_pool, v_odd_d_scale, v_odd_w_out, v_odd_ln_g, v_odd_ln_b):
    local = dict(zip(WEIGHT_NAMES, (even_w_in, even_a_ln_g, even_a_ln_b, even_a_ws, even_a_bs, even_b_sinks, even_w_out, even_ln_g, even_ln_b, odd_w_in, odd_conv_w, odd_conv_b, odd_w_a, odd_b_a, odd_w_x, odd_b_x, odd_lam, odd_w_pool, odd_d_scale, odd_w_out, odd_ln_g, odd_ln_b)))
    mom = dict(zip(WEIGHT_NAMES, (m_even_w_in, m_even_a_ln_g, m_even_a_ln_b, m_even_a_ws, m_even_a_bs, m_even_b_sinks, m_even_w_out, m_even_ln_g, m_even_ln_b, m_odd_w_in, m_odd_conv_w, m_odd_conv_b, m_odd_w_a, m_odd_b_a, m_odd_w_x, m_odd_b_x, m_odd_lam, m_odd_w_pool, m_odd_d_scale, m_odd_w_out, m_odd_ln_g, m_odd_ln_b)))
    vel = dict(zip(WEIGHT_NAMES, (v_even_w_in, v_even_a_ln_g, v_even_a_ln_b, v_even_a_ws, v_even_a_bs, v_even_b_sinks, v_even_w_out, v_even_ln_g, v_even_ln_b, v_odd_w_in, v_odd_conv_w, v_odd_conv_b, v_odd_w_a, v_odd_b_a, v_odd_w_x, v_odd_b_x, v_odd_lam, v_odd_w_pool, v_odd_d_scale, v_odd_w_out, v_odd_ln_g, v_odd_ln_b)))
    mx, my, core = _where_am_i()
    chip = 2 * mx + my

    stored = lambda n, a: jnp.swapaxes(a, 1, 2) if n == 'even_w_in' and W_IN_TRANSPOSED['even'] else a
    names_of = lambda layer: [('even' if layer % 2 == 0 else 'odd') + s for s in ('_w_in', '_w_out')]

    shards = [[stored(n, local[n])[layer // 2].astype(MXU_DTYPE) for n in names_of(layer)] for layer in range(DEPTH)]
    (first_in, first_out), first_token = _ici_start(
        'gather', [[_pack_small_sharded(local), shards[0][0]], [shards[0][1]]], "gather_start_first")
    later, gather_token = _ici_start('gather', shards[1:], "gather_start", after=[first_token])
    gathers = [None] + later

    own, lands = _ici_wait('gather', first_in, [first_token, gather_token], "gather_wait_0")
    small_sharded, first_w_in = _with_own_slot(_forward_to_sibling(lands, "gather_forward_0"), own, chip)
    weights = dict(local)
    weights.update(_unpack_small_sharded(small_sharded))
    passing = {}

    def passed_on(started, after, name):
        own, lands = _ici_wait('gather', started, [after], "gather_wait_" + name)
        (forward,), token = _ici_start('forward', [lands], "forward_start_" + name)

        def finish(then):
            done, _ = _ici_wait('forward', forward, [then], "forward_wait_" + name)
            return _with_own_slot(done, own, chip)

        return token, finish

    def first_w_out(h):
        token, finish = passed_on(first_out, h, "0_out")
        return token, lambda mix: finish(mix)[0]

    def projections_ahead(layer, after):
        token, passing[layer] = passed_on(gathers[layer], after, str(layer))
        return token

    def projections_of(layer, after):
        return (first_w_in, first_w_out) if layer == 0 else passing.pop(layer)(after)

    swapping, in_flight, mine, started = [], [], {}, {}

    def land(*after):
        layer, scatter = in_flight.pop()
        parts, lands = _ici_wait('scatter', scatter, after, "scatter_wait_%d" % layer)
        for n, arrived, part in zip(names_of(layer), lands, parts):
            mine[n, layer // 2] = _sum_slots(arrived, part, chip, "big_sum_%s_%d" % (n, layer // 2))

    odd_pieces = [(n, l) for n in ('odd_w_in', 'odd_w_out') for l in range(2)]
    joining = []

    def projection_grads(layer, dw_in, dw_out):
        if in_flight:
            land(dw_in)
        (swap,), token = _ici_start('swap', [[dw_in, dw_out]], "presum_start_%d" % layer)
        swapping.append((layer, swap))
        if layer > 0:
            return token
        (join,), token = _ici_start('join', [[mine[p] for p in odd_pieces]], "join_odd_start",
                                    after=[presum_and_scatter(token)])
        joining.append(join)
        return token

    def presum_and_scatter(*after):
        layer, swap = swapping.pop()
        partial, from_sibling = _ici_wait('swap', swap, after, "presum_wait_%d" % layer)
        chip_part = [_add_halves(g, r, core, MXU_DTYPE, "big_presum_add_%s_%d" % (n, layer // 2))
                     for g, r, n in zip(partial, from_sibling, names_of(layer))]
        (scatter,), started[layer] = _ici_start('scatter', [chip_part], "scatter_start_%d" % layer)
        in_flight.append((layer, scatter))
        return started[layer]

    layer_grads_begin = lambda layer, dout: presum_and_scatter(dout) if swapping else dout
    loss, grad_x, grads = _local_step(x[0], positions[0], loss_target[0], weights, projections_of, projections_ahead,
                                      projection_grads, layer_grads_begin)

    small = _small_grad_parts(grads, loss)
    chip_sums = [_add2(a, b, "small_presum_add_%d" % i)
                 for i, (a, b) in enumerate(zip(small, _sibling_swap(small, "small_swap", after=[started[0]])))]
    (small_gather,), small_token = _ici_start('gather', [chip_sums], "small_gather_start")

    reduced, deltas, new_m, new_v = {}, {}, {}, {}

    def adamw_projections(kind, after):
        names = [kind + '_w_in', kind + '_w_out']
        pieces = [(n, l) for n in names for l in range(2)]
        if kind == 'odd':
            halves, theirs = _ici_wait('join', joining.pop(), after, "join_odd_wait")
            mine.update(zip(pieces, halves))
        else:
            theirs = _sibling_swap([mine[p] for p in pieces], "big_join_" + kind, after=after)
        theirs = dict(zip(pieces, theirs))
        done = []
        for n in names:
            outs = _adamw_projection(stored(n, local[n]), stored(n, mom[n]), stored(n, vel[n]),
                                     [mine[n, l] for l in range(2)], [theirs[n, l] for l in range(2)], core,
                                     "adamw_" + n)
            reduced[n], deltas[n], new_m[n], new_v[n] = [stored(n, o) for o in outs]
            done.append(outs[-1])
        return done

    done = adamw_projections('odd', [small_token])
    land(grad_x, *done)
    done = adamw_projections('even', [])

    own, lands = _ici_wait('gather', small_gather, done, "small_gather_wait")
    of_chips = _forward_to_sibling(lands, "small_gather_forward")
    small_totals, total_loss = _from_small_grad_parts(
        [_sum_slots(g, mine_i, chip, "small_sum_%d" % i) for i, (g, mine_i) in enumerate(zip(of_chips, own))], chip)
    reduced.update(small_totals)
    for n in SMALL_NAMES:
        deltas[n], new_m[n], new_v[n] = _adamw(local[n], reduced[n], mom[n], vel[n], "adamw_" + n)

    return (total_loss, grad_x[None], *[reduced[n] for n in WEIGHT_NAMES], *[deltas[n] for n in WEIGHT_NAMES],
            *[new_m[n] for n in WEIGHT_NAMES], *[new_v[n] for n in WEIGHT_NAMES])
```

```python
import math

import jax
import jax.numpy as jnp
from jax import lax
from jax.experimental import pallas as pl
from jax.experimental.pallas import tpu as pltpu

F32 = jnp.float32
MXU_DTYPE = jnp.bfloat16

DEPTH = 4
A_WIDTH = 1024
A_GROUPS = 8
CHUNK = 128
B_HEAD_DIM = 64
B_Q_HEADS = 16
B_KV_HEADS = 2
WINDOW = 128
ROT_DIM = 16
ROPE_THETA = 500000.0
C_WIDTH = 1024
C_HEADS = 8
CONV_WIDTH = 4
LRU_C = 8.0
D_WIDTH = 1024
D_GROUPS = 4
D_GROUP_DIM = 256
EVEN_IN = 5376
DN_ALPHA = (2 * DEPTH) ** 0.25
LN_EPS = 1e-5
ATTN_SCALE = B_HEAD_DIM ** -0.5
NEG_BIG = -1e30

ADAM_LR = 0.001
ADAM_B1 = 0.9
ADAM_B2 = 0.999
ADAM_EPS = 1e-08
ADAM_WD = 0.01
ADAM_STEP = 10

LANES = 128
VMEM_LIMIT = 56 * 1024 * 1024
N_CHIPS = 4
MESH = pl.DeviceIdType.MESH

WEIGHT_NAMES = ['even_w_in', 'even_a_ln_g', 'even_a_ln_b', 'even_a_ws', 'even_a_bs', 'even_b_sinks', 'even_w_out',
                'even_ln_g', 'even_ln_b', 'odd_w_in', 'odd_conv_w', 'odd_conv_b', 'odd_w_a', 'odd_b_a', 'odd_w_x',
                'odd_b_x', 'odd_lam', 'odd_w_pool', 'odd_d_scale', 'odd_w_out', 'odd_ln_g', 'odd_ln_b']
BIG_SHARD_AXIS = {'even_w_in': 2, 'even_w_out': 1, 'odd_w_in': 2, 'odd_w_out': 1}
SMALL_SHARD_AXIS = {'odd_conv_w': 2, 'odd_conv_b': 1, 'odd_b_a': 1, 'odd_b_x': 1, 'odd_lam': 1, 'odd_w_pool': 2,
                    'odd_d_scale': 1, 'odd_ln_g': 1, 'odd_ln_b': 1}
FULL_SHAPES = {
    'even_w_in': (2, 2048, 5376), 'even_a_ln_g': (2, 1024), 'even_a_ln_b': (2, 1024), 'even_a_ws': (2, 8, 128, 128),
    'even_a_bs': (2, 8, 128), 'even_b_sinks': (2, 16), 'even_w_out': (2, 2048, 2048), 'even_ln_g': (2, 2048),
    'even_ln_b': (2, 2048), 'odd_w_in': (2, 2048, 4096), 'odd_conv_w': (2, 4, 1024), 'odd_conv_b': (2, 1024),
    'odd_w_a': (2, 8, 128, 128), 'odd_b_a': (2, 1024), 'odd_w_x': (2, 8, 128, 128), 'odd_b_x': (2, 1024),
    'odd_lam': (2, 1024), 'odd_w_pool': (2, 4, 256, 256), 'odd_d_scale': (2, 1024), 'odd_w_out': (2, 2048, 2048),
    'odd_ln_g': (2, 2048), 'odd_ln_b': (2, 2048)}
BIG_NAMES = ['even_w_in', 'even_w_out', 'odd_w_in', 'odd_w_out']
SMALL_SHARDED = ['odd_conv_w', 'odd_conv_b', 'odd_b_a', 'odd_b_x', 'odd_lam', 'odd_d_scale', 'odd_ln_g', 'odd_ln_b',
                 'odd_w_pool']
SMALL_NAMES = [n for n in WEIGHT_NAMES if n not in BIG_NAMES]
W_IN_TRANSPOSED = {'even': True, 'odd': False}
PACK_COLS = 1024


def _params(sem):
    return pltpu.CompilerParams(dimension_semantics=sem, vmem_limit_bytes=VMEM_LIMIT)


def _dot(a, b):
    return lax.dot_general(a, b, (((1,), (0,)), ((), ())), preferred_element_type=F32)


def _dot_nt(a, b):
    return lax.dot_general(a, b, (((1,), (1,)), ((), ())), preferred_element_type=F32)


def _dot_tn(a, b):
    return lax.dot_general(a, b, (((0,), (0,)), ((), ())), preferred_element_type=F32)


def _sigmoid(x):
    return 0.5 * jnp.tanh(0.5 * x) + 0.5


def _silu_and_grad(x):
    s = _sigmoid(x)
    return x * s, s * (1.0 + x * (1.0 - s))


def _mm_call(body, name, grid, in_specs, out_spec, out_shape, args):
    return pl.pallas_call(body, name=name, grid=grid, in_specs=in_specs, out_specs=out_spec, out_shape=out_shape,
                          compiler_params=_params(("parallel",) * len(grid)))(*args)


def _resident(block, index_map):
    return pl.BlockSpec(block, index_map, pipeline_mode=pl.Buffered(1))


ROWS_OF_WT = 768


def _mm_in(x, wg, transposed, name):
    s, k = x.shape
    x_spec = _resident((s, k), lambda j: (0, 0))

    def body(x_ref, w_ref, o_ref):
        o_ref[...] = (_dot_nt if transposed else _dot)(x_ref[...], w_ref[...])

    if transposed:
        wt = wg.reshape(-1, k)
        n, tn = wt.shape[0], ROWS_OF_WT
        return _mm_call(body, name, (n // tn,), [x_spec, pl.BlockSpec((tn, k), lambda j: (j, 0))],
                        pl.BlockSpec((s, tn), lambda j: (0, j)), jax.ShapeDtypeStruct((s, n), F32), (x, wt))
    nc = wg.shape[2]
    return _mm_call(body, name, (N_CHIPS,), [x_spec, pl.BlockSpec((None, k, nc), lambda j: (j, 0, 0))],
                    pl.BlockSpec((None, s, nc), lambda j: (j, 0, 0)),
                    jax.ShapeDtypeStruct((N_CHIPS, s, nc), F32), (x, wg))


def _mm_dx(dh, wg, dz, transposed, after, name):
    s, d = dz.shape
    tm, tn = min(1024, s), 512
    res = pl.BlockSpec((tm, tn), lambda i, j: (i, j))
    if transposed:
        wt = wg.reshape(-1, d)
        n = wt.shape[0]

        def body(a_ref, w_ref, r_ref, after_ref, o_ref):
            o_ref[...] = DN_ALPHA * r_ref[...] + _dot(a_ref[...], w_ref[...])

        return _mm_call(body, name, (s // tm, d // tn),
                        [pl.BlockSpec((tm, n), lambda i, j: (i, 0)), pl.BlockSpec((n, tn), lambda i, j: (0, j)), res,
                         ANY], res, jax.ShapeDtypeStruct((s, d), F32), (dh, wt, dz, after))
    nc = dh.shape[2]

    def body(a_ref, w_ref, r_ref, after_ref, o_ref):
        acc = DN_ALPHA * r_ref[...]
        for j in range(N_CHIPS):
            acc = acc + _dot_nt(a_ref[j], w_ref[j])
        o_ref[...] = acc

    return _mm_call(body, name, (s // tm, d // tn),
                    [pl.BlockSpec((N_CHIPS, tm, nc), lambda i, j: (0, i, 0)),
                     pl.BlockSpec((N_CHIPS, tn, nc), lambda i, j: (0, j, 0)), res, ANY],
                    res, jax.ShapeDtypeStruct((s, d), F32), (dh, wg, dz, after))


def _mm_dw_in(x, dh, transposed, name):
    s, d = x.shape
    x_spec = _resident((s, d), lambda j: (0, 0))

    def body(a_ref, b_ref, o_ref):
        o_ref[...] = _dot_tn(a_ref[...], b_ref[...]).astype(o_ref.dtype)

    if transposed:
        n, tn = dh.shape[1], ROWS_OF_WT
        flat = _mm_call(body, name, (n // tn,), [pl.BlockSpec((s, tn), lambda j: (0, j)), x_spec],
                        pl.BlockSpec((tn, d), lambda j: (j, 0)), jax.ShapeDtypeStruct((n, d), MXU_DTYPE), (dh, x))
        return flat.reshape(N_CHIPS, n // N_CHIPS, d)
    nc = dh.shape[2]
    return _mm_call(body, name, (N_CHIPS,), [x_spec, pl.BlockSpec((None, s, nc), lambda j: (j, 0, 0))],
                    pl.BlockSpec((None, d, nc), lambda j: (j, 0, 0)),
                    jax.ShapeDtypeStruct((N_CHIPS, d, nc), MXU_DTYPE), (x, dh))


def _mm_out_norm(mix, wg, x, g, b, name, target=None, after=None):
    s, k = mix.shape
    _, kc, d = wg.shape
    tm = min(512, s)

    def normed(a_ref, w_ref, x_ref, g_ref, b_ref, z_ref):
        z = DN_ALPHA * x_ref[...] + _dot(a_ref[...], w_ref[...].reshape(N_CHIPS * kc, d))
        mu = jnp.mean(z, axis=1, keepdims=True)
        zc = z - mu
        var = jnp.mean(zc * zc, axis=1, keepdims=True)
        z_ref[...] = z
        return zc * lax.rsqrt(var + LN_EPS) * g_ref[...] + b_ref[...]

    def body(a_ref, w_ref, x_ref, g_ref, b_ref, after_ref, z_ref, o_ref, ob_ref):
        o = normed(a_ref, w_ref, x_ref, g_ref, b_ref, z_ref)
        o_ref[...] = o
        ob_ref[...] = o.astype(ob_ref.dtype)

    def body_with_loss(a_ref, w_ref, x_ref, g_ref, b_ref, t_ref, z_ref, do_ref, l_ref):
        @pl.when(pl.program_id(0) == 0)
        def _():
            l_ref[...] = jnp.zeros_like(l_ref)

        e = normed(a_ref, w_ref, x_ref, g_ref, b_ref, z_ref) - t_ref[...]
        do_ref[...] = e * (1.0 / d)
        l_ref[...] += (0.5 / d) * jnp.sum(jnp.sum(e * e, axis=1, keepdims=True), axis=0, keepdims=True)

    row = pl.BlockSpec((tm, d), lambda i: (i, 0))
    vec = pl.BlockSpec((1, d), lambda i: (0, 0))
    in_specs = [pl.BlockSpec((tm, k), lambda i: (i, 0)), _resident((N_CHIPS, kc, d), lambda i: (0, 0, 0)), row, vec,
                vec]
    args = (mix, wg, x, g.reshape(1, d), b.reshape(1, d))
    f32_rows = jax.ShapeDtypeStruct((s, d), F32)
    if target is None:
        return pl.pallas_call(
            body, name=name, grid=(s // tm,), in_specs=in_specs + [ANY], out_specs=[row, row, row],
            out_shape=[f32_rows, f32_rows, jax.ShapeDtypeStruct((s, d), MXU_DTYPE)],
            compiler_params=_params(("parallel",)))(*args, after)
    return pl.pallas_call(
        body_with_loss, name=name, grid=(s // tm,), in_specs=in_specs + [row],
        out_specs=[row, row, pl.BlockSpec((1, 1), lambda i: (0, 0))],
        out_shape=[f32_rows, f32_rows, jax.ShapeDtypeStruct((1, 1), F32)],
        compiler_params=_params(("arbitrary",)))(*args, target)


NORM_BWD_ROWS = 256


def _norm_out_bwd(dout, z, g, wg, mix, after, name):
    s, d = z.shape
    kc = wg.shape[1]
    tr = min(NORM_BWD_ROWS, s)
    nr = s // tr

    def body(do_ref, z_ref, g_ref, w_ref, mix_ref, after_ref, dz_ref, dg_ref, db_ref, dmix_ref, dw_ref, dzb_scr):
        i = pl.program_id(0)

        @pl.when(i == 0)
        def _():
            dg_ref[...] = jnp.zeros_like(dg_ref)
            db_ref[...] = jnp.zeros_like(db_ref)

        @pl.when(i < nr)
        def _():
            zz = z_ref[...]
            do = do_ref[...]
            mu = jnp.mean(zz, axis=1, keepdims=True)
            zc = zz - mu
            var = jnp.mean(zc * zc, axis=1, keepdims=True)
            rstd = lax.rsqrt(var + LN_EPS)
            zh = zc * rstd
            dg_ref[...] += jnp.sum(do * zh, axis=0, keepdims=True)
            db_ref[...] += jnp.sum(do, axis=0, keepdims=True)
            dzh = do * g_ref[...]
            dz = rstd * (dzh - jnp.mean(dzh, axis=1, keepdims=True) - zh * jnp.mean(dzh * zh, axis=1, keepdims=True))
            dz_ref[...] = dz
            dzb_scr[pl.ds(pl.multiple_of(i * tr, tr), tr), :] = dz.astype(dzb_scr.dtype)

        @pl.when(i >= nr)
        def _():
            dzb = dzb_scr[...]
            dmix_ref[...] = _dot_nt(dzb, w_ref[...])
            dw_ref[...] = _dot_tn(mix_ref[...], dzb).astype(dw_ref.dtype)

    row = pl.BlockSpec((tr, d), lambda i: (jnp.minimum(i, nr - 1), 0))
    vec = pl.BlockSpec((1, d), lambda i: (0, 0))
    shard = lambda i: jnp.maximum(i - nr, 0)
    return pl.pallas_call(
        body, name=name, grid=(nr + N_CHIPS,),
        in_specs=[row, row, vec, pl.BlockSpec((None, kc, d), lambda i: (shard(i), 0, 0)),
                  pl.BlockSpec((s, kc), lambda i: (0, shard(i))), ANY],
        out_specs=[row, vec, vec, pl.BlockSpec((s, kc), lambda i: (0, shard(i))),
                   pl.BlockSpec((None, kc, d), lambda i: (shard(i), 0, 0))],
        out_shape=[jax.ShapeDtypeStruct((s, d), F32), jax.ShapeDtypeStruct((1, d), F32),
                   jax.ShapeDtypeStruct((1, d), F32), jax.ShapeDtypeStruct((s, N_CHIPS * kc), F32),
                   jax.ShapeDtypeStruct((N_CHIPS, kc, d), MXU_DTYPE)],
        scratch_shapes=[pltpu.VMEM((s, d), MXU_DTYPE)],
        compiler_params=_params(("arbitrary",)),
    )(dout, z, g.reshape(1, d), wg, mix, after)


Q0, K0, V0, BG0 = 3072, 4096, 4224, 4352
T = CHUNK


def _lane_ids(width):
    return lax.broadcasted_iota(jnp.int32, (T, width), 1)


def _rope_swap(x):
    w = x.shape[1]
    l64 = _lane_ids(w) % B_HEAD_DIM
    half = ROT_DIM // 2
    return jnp.where(l64 < half, pltpu.roll(x, w - half, 1), jnp.where(l64 < ROT_DIM, pltpu.roll(x, half, 1), 0.0))


def _rope(x, c, s):
    return x * c + _rope_swap(x) * s


def _rope_bwd(d, c, s):
    return d * c + _rope_swap(d * s)


def _tril():
    return lax.broadcasted_iota(jnp.int32, (T, T), 0) >= lax.broadcasted_iota(jnp.int32, (T, T), 1)


def _band_mask(has_prev):
    qi = lax.broadcasted_iota(jnp.int32, (T, 2 * T), 0)
    kj = lax.broadcasted_iota(jnp.int32, (T, 2 * T), 1)
    return (kj > qi) & (kj <= qi + WINDOW) & ((kj >= WINDOW) | has_prev)


def _a_norm(v, g, b):
    mu = jnp.mean(v, axis=1, keepdims=True)
    vc = v - mu
    var = jnp.mean(vc * vc, axis=1, keepdims=True)
    rstd = lax.rsqrt(var + LN_EPS)
    vh = vc * rstd
    return vh, rstd, vh * g + b


def _kv_operands(hk, k_rot, kp_rot, vv, vp):
    lo = _lane_ids(LANES) < B_HEAD_DIM
    mine = lo if hk == 0 else jnp.logical_not(lo)
    both = lambda t: jnp.where(mine, t, 0.0) + pltpu.roll(jnp.where(mine, t, 0.0), B_HEAD_DIM, 1)
    k2 = jnp.concatenate([both(kp_rot), both(k_rot)], axis=0)
    v2 = jnp.concatenate([both(vp), both(vv)], axis=0)
    return k2.astype(MXU_DTYPE), v2.astype(MXU_DTYPE)


GROUP_HEADS = B_Q_HEADS // B_KV_HEADS


def _pair_columns(hk):
    return [(hk * GROUP_HEADS // 2 + pp) * LANES for pp in range(GROUP_HEADS // 2)]


def _stacked_heads(pairs):
    lo = _lane_ids(LANES) < B_HEAD_DIM
    rows = []
    for tile in pairs:
        rows += [jnp.where(lo, tile, 0.0), jnp.where(lo, 0.0, tile)]
    return jnp.concatenate(rows, axis=0).astype(MXU_DTYPE)


def _pair_of(stacked, pp):
    lo = _lane_ids(LANES) < B_HEAD_DIM
    return jnp.where(lo, stacked[2 * pp * T:(2 * pp + 1) * T], stacked[(2 * pp + 1) * T:(2 * pp + 2) * T])


def _softmax_with_sink(qm, k2, valid, sink_ref, head0):
    scores = (_dot_nt(qm, k2) * ATTN_SCALE).reshape(GROUP_HEADS, T, 2 * T)
    s = jnp.where(valid[None], scores, NEG_BIG).reshape(GROUP_HEADS * T, 2 * T)
    sink = jnp.concatenate([jnp.broadcast_to(sink_ref[:, head0 + h:head0 + h + 1], (T, 1))
                            for h in range(GROUP_HEADS)], axis=0)
    m = jnp.maximum(jnp.max(s, axis=1, keepdims=True), sink)
    ex = jnp.exp(s - m)
    es = jnp.exp(sink - m)
    inv = 1.0 / (jnp.sum(ex, axis=1, keepdims=True) + es)
    return ex * inv, es * inv


def _even_specs(s):
    nb = s // T
    prev = lambda r: lambda i: jnp.maximum(r(i) - 1, 0)
    return nb, {
        'h': lambda r: pl.BlockSpec((T, EVEN_IN), lambda i: (r(i), 0)),
        'kprev': lambda r: pl.BlockSpec((T, LANES), lambda i: (prev(r)(i), K0 // LANES)),
        'vprev': lambda r: pl.BlockSpec((T, LANES), lambda i: (prev(r)(i), V0 // LANES)),
        'tab': lambda r: pl.BlockSpec((T, LANES), lambda i: (r(i), 0)),
        'tabprev': lambda r: pl.BlockSpec((T, LANES), lambda i: (prev(r)(i), 0)),
    }


def _full(shape):
    nd = len(shape)
    return pl.BlockSpec(shape, lambda i: (0,) * nd)


def _even_fwd(h, cq, sq, ln_g, ln_b, ws, bs_t, sinks, after):
    s = h.shape[0]
    nb, sp = _even_specs(s)

    def body(h_ref, kp_ref, vp_ref, cq_ref, sq_ref, ckp_ref, skp_ref, g_ref, b_ref, ws_ref, bst_ref, sink_ref,
             after_ref, mix_ref):
        i = pl.program_id(0)
        tril = _tril()
        _, _, vn = _a_norm(h_ref[:, 1024:2048], g_ref[...], b_ref[...])
        vnb = vn.astype(MXU_DTYPE)
        for g in range(A_GROUPS):
            c0, c1 = g * LANES, (g + 1) * LANES
            wg = jnp.where(tril, ws_ref[g], 0.0).astype(MXU_DTYPE)
            mixed = _dot(wg, vnb[:, c0:c1]) + bst_ref[:, g:g + 1]
            sil, _ = _silu_and_grad(h_ref[:, 2048 + c0:2048 + c1])
            mix_ref[:, c0:c1] = (h_ref[:, c0:c1] * mixed * sil).astype(mix_ref.dtype)

        cq, sq = cq_ref[...], sq_ref[...]
        k_rot = _rope(h_ref[:, K0:K0 + LANES], cq, sq)
        kp_rot = _rope(kp_ref[...], ckp_ref[...], skp_ref[...])
        valid = _band_mask(i > 0)
        for hk in range(B_KV_HEADS):
            k2, v2 = _kv_operands(hk, k_rot, kp_rot, h_ref[:, V0:V0 + LANES], vp_ref[...])
            qm = _stacked_heads([_rope(h_ref[:, Q0 + c0:Q0 + c0 + LANES], cq, sq) for c0 in _pair_columns(hk)])
            p, _ = _softmax_with_sink(qm, k2, valid, sink_ref, GROUP_HEADS * hk)
            out = _dot(p.astype(MXU_DTYPE), v2)
            for pp, c0 in enumerate(_pair_columns(hk)):
                sil, _ = _silu_and_grad(h_ref[:, BG0 + c0:BG0 + c0 + LANES])
                mix_ref[:, A_WIDTH + c0:A_WIDTH + c0 + LANES] = (_pair_of(out, pp) * sil).astype(mix_ref.dtype)

    ident = lambda i: i
    return pl.pallas_call(
        body, name="even_fwd", grid=(nb,),
        in_specs=[sp['h'](ident), sp['kprev'](ident), sp['vprev'](ident), sp['tab'](ident), sp['tab'](ident),
                  sp['tabprev'](ident), sp['tabprev'](ident), _full((1, A_WIDTH)), _full((1, A_WIDTH)),
                  _full((A_GROUPS, T, T)), _full((T, A_GROUPS)), _full((1, B_Q_HEADS)), ANY],
        out_specs=pl.BlockSpec((T, 2 * A_WIDTH), lambda i: (i, 0)),
        out_shape=jax.ShapeDtypeStruct((s, 2 * A_WIDTH), MXU_DTYPE),
        compiler_params=_params(("parallel",)),
    )(h, h, h, cq, sq, cq, sq, ln_g.reshape(1, -1), ln_b.reshape(1, -1), ws, bs_t, sinks.reshape(1, -1), after)


def _even_bwd(h, dmix, cq, sq, ln_g, ln_b, ws, bs_t, sinks):
    s = h.shape[0]
    nb, sp = _even_specs(s)

    def body(h_ref, kp_ref, vp_ref, cq_ref, sq_ref, ckp_ref, skp_ref, dm_ref, g_ref, b_ref, ws_ref, bst_ref, sink_ref,
             dh_ref, dg_ref, db_ref, dws_ref, dbst_ref, dsink_ref, dvn_scr, ck_scr, cv_scr):
        i = pl.program_id(0)

        @pl.when(i == 0)
        def _():
            for ref in (dg_ref, db_ref, dws_ref, dbst_ref, dsink_ref, ck_scr, cv_scr):
                ref[...] = jnp.zeros_like(ref)

        tril = _tril()
        vh, rstd, vn = _a_norm(h_ref[:, 1024:2048], g_ref[...], b_ref[...])
        vnb = vn.astype(MXU_DTYPE)
        for g in range(A_GROUPS):
            c0, c1 = g * LANES, (g + 1) * LANES
            wg = jnp.where(tril, ws_ref[g], 0.0).astype(MXU_DTYPE)
            mixed = _dot(wg, vnb[:, c0:c1]) + bst_ref[:, g:g + 1]
            ag = h_ref[:, 2048 + c0:2048 + c1]
            u = h_ref[:, c0:c1]
            sil, dsil = _silu_and_grad(ag)
            da = dm_ref[:, c0:c1]
            dh_ref[:, c0:c1] = (da * mixed * sil).astype(dh_ref.dtype)
            dh_ref[:, 2048 + c0:2048 + c1] = (da * u * mixed * dsil).astype(dh_ref.dtype)
            dmx = da * u * sil
            dmb = dmx.astype(MXU_DTYPE)
            dvn_scr[:, c0:c1] = _dot_tn(wg, dmb)
            dws_ref[g] += jnp.where(tril, _dot_nt(dmb, vnb[:, c0:c1]), 0.0)
            dbst_ref[:, g:g + 1] += jnp.sum(dmx, axis=1, keepdims=True)
        dvn = dvn_scr[...]
        dg_ref[...] += jnp.sum(dvn * vh, axis=0, keepdims=True)
        db_ref[...] += jnp.sum(dvn, axis=0, keepdims=True)
        dvh = dvn * g_ref[...]
        dv = rstd * (dvh - jnp.mean(dvh, axis=1, keepdims=True) - vh * jnp.mean(dvh * vh, axis=1, keepdims=True))
        dh_ref[:, 1024:2048] = dv.astype(dh_ref.dtype)

        cq, sq = cq_ref[...], sq_ref[...]
        k_rot = _rope(h_ref[:, K0:K0 + LANES], cq, sq)
        kp_rot = _rope(kp_ref[...], ckp_ref[...], skp_ref[...])
        valid = _band_mask(i < nb - 1)
        dk_all = jnp.zeros((2 * T, LANES), F32)
        dv_all = jnp.zeros((2 * T, LANES), F32)
        for hk in range(B_KV_HEADS):
            k2, v2 = _kv_operands(hk, k_rot, kp_rot, h_ref[:, V0:V0 + LANES], vp_ref[...])
            cols = _pair_columns(hk)
            qm = _stacked_heads([_rope(h_ref[:, Q0 + c0:Q0 + c0 + LANES], cq, sq) for c0 in cols])
            p, psink = _softmax_with_sink(qm, k2, valid, sink_ref, GROUP_HEADS * hk)
            pb = p.astype(MXU_DTYPE)
            out = _dot(pb, v2)
            dobs = []
            for pp, c0 in enumerate(cols):
                sil, dsil = _silu_and_grad(h_ref[:, BG0 + c0:BG0 + c0 + LANES])
                dbo = dm_ref[:, A_WIDTH + c0:A_WIDTH + c0 + LANES]
                dh_ref[:, BG0 + c0:BG0 + c0 + LANES] = (dbo * _pair_of(out, pp) * dsil).astype(dh_ref.dtype)
                dobs.append(dbo * sil)
            dobm = _stacked_heads(dobs)
            dp = _dot_nt(dobm, v2)
            rs = jnp.sum(p * dp, axis=1, keepdims=True)
            dsb = (p * (dp - rs) * ATTN_SCALE).astype(MXU_DTYPE)
            to_sink = psink * rs
            for h in range(GROUP_HEADS):
                head = GROUP_HEADS * hk + h
                dsink_ref[:, head:head + 1] += -jnp.sum(to_sink[h * T:(h + 1) * T], axis=0, keepdims=True)
            dq = _dot(dsb, k2)
            for pp, c0 in enumerate(cols):
                dh_ref[:, Q0 + c0:Q0 + c0 + LANES] = _rope_bwd(_pair_of(dq, pp), cq, sq).astype(dh_ref.dtype)
            acc_k = _dot_tn(dsb, qm)
            acc_v = _dot_tn(pb, dobm)
            lo2 = lax.broadcasted_iota(jnp.int32, (2 * T, LANES), 1) < B_HEAD_DIM
            mine = lo2 if hk == 0 else jnp.logical_not(lo2)
            dk_all = dk_all + jnp.where(mine, acc_k + pltpu.roll(acc_k, B_HEAD_DIM, 1), 0.0)
            dv_all = dv_all + jnp.where(mine, acc_v + pltpu.roll(acc_v, B_HEAD_DIM, 1), 0.0)
        dk_rot = dk_all[T:2 * T] + ck_scr[...]
        dvv = dv_all[T:2 * T] + cv_scr[...]
        ck_scr[...] = dk_all[0:T]
        cv_scr[...] = dv_all[0:T]
        dh_ref[:, K0:K0 + LANES] = _rope_bwd(dk_rot, cq, sq).astype(dh_ref.dtype)
        dh_ref[:, V0:V0 + LANES] = dvv.astype(dh_ref.dtype)

    rev = lambda i: nb - 1 - i
    return pl.pallas_call(
        body, name="even_bwd", grid=(nb,),
        in_specs=[sp['h'](rev), sp['kprev'](rev), sp['vprev'](rev), sp['tab'](rev), sp['tab'](rev),
                  sp['tabprev'](rev), sp['tabprev'](rev), pl.BlockSpec((T, 2 * A_WIDTH), lambda i: (rev(i), 0)),
                  _full((1, A_WIDTH)), _full((1, A_WIDTH)), _full((A_GROUPS, T, T)), _full((T, A_GROUPS)),
                  _full((1, B_Q_HEADS))],
        out_specs=[sp['h'](rev), _full((1, A_WIDTH)), _full((1, A_WIDTH)),
                   _full((A_GROUPS, T, T)), _full((T, A_GROUPS)), _full((1, B_Q_HEADS))],
        out_shape=[jax.ShapeDtypeStruct((s, EVEN_IN), MXU_DTYPE), jax.ShapeDtypeStruct((1, A_WIDTH), F32),
                   jax.ShapeDtypeStruct((1, A_WIDTH), F32), jax.ShapeDtypeStruct((A_GROUPS, T, T), F32),
                   jax.ShapeDtypeStruct((T, A_GROUPS), F32), jax.ShapeDtypeStruct((1, B_Q_HEADS), F32)],
        scratch_shapes=[pltpu.VMEM((T, A_WIDTH), F32), pltpu.VMEM((T, LANES), F32), pltpu.VMEM((T, LANES), F32)],
        compiler_params=_params(("arbitrary",)),
    )(h, h, h, cq, sq, cq, sq, dmix, ln_g.reshape(1, -1), ln_b.reshape(1, -1), ws, bs_t, sinks.reshape(1, -1))


def _shift_down(x, d, fill=0.0):
    rows = lax.broadcasted_iota(jnp.int32, x.shape, 0)
    return jnp.where(rows >= d, pltpu.roll(x, d, 0), fill)


def _shift_up(x, d, fill=0.0):
    n = x.shape[0]
    rows = lax.broadcasted_iota(jnp.int32, x.shape, 0)
    return jnp.where(rows < n - d, pltpu.roll(x, n - d, 0), fill)


SCAN_BLOCK = 32
SUBLANES = 8


def _scan_block(a, b, up):
    n = a.shape[0]
    d = 1
    while d < n:
        if d < SUBLANES:
            shift = _shift_up if up else _shift_down
            a, b = a * shift(a, d, 1.0), a * shift(b, d) + b
        elif up:
            a, b = (jnp.concatenate([a[:n - d] * a[d:], a[n - d:]], axis=0),
                    jnp.concatenate([a[:n - d] * b[d:] + b[:n - d], b[n - d:]], axis=0))
        else:
            a, b = (jnp.concatenate([a[:d], a[d:] * a[:n - d]], axis=0),
                    jnp.concatenate([b[:d], a[d:] * b[:n - d] + b[d:]], axis=0))
        d *= 2
    return a, b


def _scan(a, b, up):
    n = a.shape[0]
    size = min(SCAN_BLOCK, n)
    blocks = n // size
    out, carry = [None] * blocks, None
    for k in (reversed(range(blocks)) if up else range(blocks)):
        prod, h = _scan_block(a[k * size:(k + 1) * size], b[k * size:(k + 1) * size], up)
        if carry is not None:
            h = h + prod * carry
        carry = h[0:1] if up else h[size - 1:size]
        out[k] = h
    return jnp.concatenate(out, axis=0)


def _neg_expm1(y):
    t = jnp.tanh(-0.5 * y)
    return 2.0 * t / (1.0 + t)


def _softplus(x):
    return jnp.maximum(x, 0.0) + jnp.log1p(jnp.exp(-jnp.abs(x)))


def _lru_forward(xc, cw, cb, wa, ba, wx, bx, lam):
    shifted = [_shift_down(xc, CONV_WIDTH - 1 - j) if j < CONV_WIDTH - 1 else xc for j in range(CONV_WIDTH)]
    xconv = cb
    for j in range(CONV_WIDTH):
        xconv = xconv + cw[j:j + 1, :] * shifted[j]
    xb = xconv.astype(MXU_DTYPE)
    r = _sigmoid(_dot(xb, wa.astype(MXU_DTYPE)) + ba)
    ig = _sigmoid(_dot(xb, wx.astype(MXU_DTYPE)) + bx)
    sp = _softplus(-lam)
    log_a = -LRU_C * r * sp
    a = jnp.exp(log_a)
    mult = jnp.sqrt(_neg_expm1(2.0 * log_a))
    hs = _scan(a, mult * ig * xconv, up=False)
    return dict(shifted=shifted, xconv=xconv, xb=xb, r=r, ig=ig, sp=sp, a=a, mult=mult, hs=hs)


def _col(width, off):
    return lambda s: pl.BlockSpec((s, width), lambda i: (0, off + i))


def _part(width, slot):
    return lambda s: pl.BlockSpec((None, s, width), lambda i: (slot, 0, i))


def _lru_specs(s):
    w = LANES
    return [_part(w, 0)(s), _part(w, 1)(s), pl.BlockSpec((CONV_WIDTH, w), lambda i: (0, i)),
            pl.BlockSpec((1, w), lambda i: (0, i)), pl.BlockSpec((None, w, w), lambda i: (i, 0, 0)),
            pl.BlockSpec((1, w), lambda i: (0, i)), pl.BlockSpec((None, w, w), lambda i: (i, 0, 0)),
            pl.BlockSpec((1, w), lambda i: (0, i)), pl.BlockSpec((1, w), lambda i: (0, i))]


def _lru_fwd(h, cw, cb, wa, ba, wx, bx, lam):
    s = h.shape[1]

    def body(xc_ref, cg_ref, cw_ref, cb_ref, wa_ref, ba_ref, wx_ref, bx_ref, lam_ref, o_ref):
        f = _lru_forward(xc_ref[...], cw_ref[...], cb_ref[...], wa_ref[...], ba_ref[...], wx_ref[...], bx_ref[...],
                         lam_ref[...])
        sil, _ = _silu_and_grad(cg_ref[...])
        o_ref[...] = (f['hs'] * sil).astype(o_ref.dtype)

    return pl.pallas_call(
        body, name="lru_fwd", grid=(C_HEADS,), in_specs=_lru_specs(s), out_specs=_col(LANES, 0)(s),
        out_shape=jax.ShapeDtypeStruct((s, C_WIDTH + D_WIDTH), MXU_DTYPE), compiler_params=_params(("parallel",)),
    )(h, h, cw, cb.reshape(1, -1), wa, ba.reshape(1, -1), wx, bx.reshape(1, -1), lam.reshape(1, -1))


def _lru_bwd(h, dmix, cw, cb, wa, ba, wx, bx, lam):
    s = h.shape[1]

    def body(xc_ref, cg_ref, cw_ref, cb_ref, wa_ref, ba_ref, wx_ref, bx_ref, lam_ref, dco_ref,
             dh_ref, dcw_ref, dcb_ref, dwa_ref, dba_ref, dwx_ref, dbx_ref, dlam_ref):
        dxc_ref, dcg_ref = dh_ref.at[0], dh_ref.at[1]
        cw, lam = cw_ref[...], lam_ref[...]
        wab, wxb = wa_ref[...].astype(MXU_DTYPE), wx_ref[...].astype(MXU_DTYPE)
        f = _lru_forward(xc_ref[...], cw, cb_ref[...], wa_ref[...], ba_ref[...], wx_ref[...], bx_ref[...], lam)
        sil, dsil = _silu_and_grad(cg_ref[...])
        dco = dco_ref[...]
        hs, a, mult, ig, r, xconv = f['hs'], f['a'], f['mult'], f['ig'], f['r'], f['xconv']
        dcg_ref[...] = (dco * hs * dsil).astype(dcg_ref.dtype)
        lamb = _scan(_shift_up(a, 1), dco * sil, up=True)
        da = lamb * _shift_down(hs, 1)
        d_ig = lamb * mult * xconv
        d_mult = lamb * ig * xconv
        dxconv = lamb * mult * ig
        d_log_a = da * a - d_mult * (a * a) / mult
        dlam_ref[...] = jnp.sum(d_log_a * r, axis=0, keepdims=True) * LRU_C * _sigmoid(-lam)
        dpa = d_log_a * (-LRU_C * f['sp']) * r * (1.0 - r)
        dpx = d_ig * ig * (1.0 - ig)
        dba_ref[...] = jnp.sum(dpa, axis=0, keepdims=True)
        dbx_ref[...] = jnp.sum(dpx, axis=0, keepdims=True)
        dpab, dpxb = dpa.astype(MXU_DTYPE), dpx.astype(MXU_DTYPE)
        dwa_ref[...] = _dot_tn(f['xb'], dpab)
        dwx_ref[...] = _dot_tn(f['xb'], dpxb)
        dxconv = dxconv + _dot_nt(dpab, wab) + _dot_nt(dpxb, wxb)
        dcb_ref[...] = jnp.sum(dxconv, axis=0, keepdims=True)
        dxc = jnp.zeros_like(dxconv)
        for j in range(CONV_WIDTH):
            dcw_ref[j:j + 1, :] = jnp.sum(dxconv * f['shifted'][j], axis=0, keepdims=True)
            back = _shift_up(dxconv, CONV_WIDTH - 1 - j) if j < CONV_WIDTH - 1 else dxconv
            dxc = dxc + cw[j:j + 1, :] * back
        dxc_ref[...] = dxc.astype(dxc_ref.dtype)

    w = LANES
    vec = pl.BlockSpec((1, w), lambda i: (0, i))
    mat = pl.BlockSpec((None, w, w), lambda i: (i, 0, 0))
    vshape = jax.ShapeDtypeStruct((1, C_WIDTH), F32)
    mshape = jax.ShapeDtypeStruct((C_HEADS, w, w), F32)
    return pl.pallas_call(
        body, name="lru_bwd", grid=(C_HEADS,), in_specs=_lru_specs(s) + [_col(w, 0)(s)],
        out_specs=[pl.BlockSpec((2, s, w), lambda i: (0, 0, i)), pl.BlockSpec((CONV_WIDTH, w), lambda i: (0, i)),
                   vec, mat, vec, mat, vec, vec],
        out_shape=[jax.ShapeDtypeStruct((N_CHIPS, s, C_WIDTH), MXU_DTYPE),
                   jax.ShapeDtypeStruct((CONV_WIDTH, C_WIDTH), F32), vshape, mshape, vshape, mshape, vshape, vshape],
        compiler_params=_params(("parallel",)),
    )(h, h, cw, cb.reshape(1, -1), wa, ba.reshape(1, -1), wx, bx.reshape(1, -1), lam.reshape(1, -1), dmix)


def _pool_window(g):
    return jnp.left_shift(2, g).astype(F32)


def _select_window(g, sums):
    out = sums[-1]
    for k in range(len(sums) - 2, -1, -1):
        out = jnp.where(g == k, sums[k], out)
    return out


def _window_sums(x, shift):
    sums, cur = [], x
    for k in range(D_GROUPS):
        cur = cur + shift(cur, 2 ** k)
        sums.append(cur)
    return sums


def _pool_specs(s):
    w = D_GROUP_DIM
    return [_part(w, 2)(s), _part(w, 3)(s),
            pl.BlockSpec((None, w, w), lambda i: (i, 0, 0)), pl.BlockSpec((1, w), lambda i: (0, i))]


def _pool_forward(xd, g):
    rows = lax.broadcasted_iota(jnp.int32, xd.shape, 0).astype(F32) + 1.0
    cnt = jnp.minimum(rows, _pool_window(g))
    pooled = _select_window(g, _window_sums(xd, _shift_down)) / cnt - xd
    return pooled, cnt


def _pool_fwd(h, mix, w_pool, d_scale):
    s = h.shape[1]

    def body(xd_ref, dg_ref, wp_ref, sc_ref, mix_ref, o_ref):
        pooled, _ = _pool_forward(xd_ref[...], pl.program_id(0))
        mixed = _dot(pooled.astype(MXU_DTYPE), wp_ref[...].astype(MXU_DTYPE))
        sil, _ = _silu_and_grad(dg_ref[...])
        o_ref[...] = (mixed * sc_ref[...] * sil).astype(o_ref.dtype)

    return pl.pallas_call(
        body, name="pool_fwd", grid=(D_GROUPS,), in_specs=_pool_specs(s) + [ANY],
        out_specs=_col(D_GROUP_DIM, C_WIDTH // D_GROUP_DIM)(s), out_shape=jax.ShapeDtypeStruct(mix.shape, mix.dtype),
        input_output_aliases={4: 0}, compiler_params=_params(("parallel",)),
    )(h, h, w_pool, d_scale.reshape(1, -1), mix)


def _pool_bwd(h, dmix, dh, w_pool, d_scale):
    s = h.shape[1]
    w = D_GROUP_DIM

    def body(xd_ref, dg_ref, wp_ref, sc_ref, ddo_ref, dh_in_ref, dh_ref, dwp_ref, dsc_ref):
        dxd_ref, ddg_ref = dh_ref.at[0], dh_ref.at[1]
        g = pl.program_id(0)
        pooled, cnt = _pool_forward(xd_ref[...], g)
        pb = pooled.astype(MXU_DTYPE)
        wpb = wp_ref[...].astype(MXU_DTYPE)
        mixed = _dot(pb, wpb)
        sil, dsil = _silu_and_grad(dg_ref[...])
        ddo = ddo_ref[...]
        sc = sc_ref[...]
        ddg_ref[...] = (ddo * mixed * sc * dsil).astype(ddg_ref.dtype)
        dms = ddo * sil
        dsc_ref[...] = jnp.sum(dms * mixed, axis=0, keepdims=True)
        dmb = (dms * sc).astype(MXU_DTYPE)
        dwp_ref[...] = _dot_tn(pb, dmb)
        dpooled = _dot_nt(dmb, wpb)
        dxd = _select_window(g, _window_sums(dpooled / cnt, _shift_up)) - dpooled
        dxd_ref[...] = dxd.astype(dxd_ref.dtype)

    return pl.pallas_call(
        body, name="pool_bwd", grid=(D_GROUPS,), in_specs=_pool_specs(s) + [_col(w, C_WIDTH // w)(s), ANY],
        out_specs=[pl.BlockSpec((2, s, w), lambda i: (1, 0, i)), pl.BlockSpec((None, w, w), lambda i: (i, 0, 0)),
                   pl.BlockSpec((1, w), lambda i: (0, i))],
        out_shape=[jax.ShapeDtypeStruct(dh.shape, dh.dtype),
                   jax.ShapeDtypeStruct((D_GROUPS, w, w), F32), jax.ShapeDtypeStruct((1, D_WIDTH), F32)],
        input_output_aliases={5: 0}, compiler_params=_params(("parallel",)),
    )(h, h, w_pool, d_scale.reshape(1, -1), dmix, dh)


def _rope_tables(positions):
    s = positions.shape[0]
    inv_freq = ROPE_THETA ** (-jnp.arange(0, ROT_DIM, 2, dtype=F32) / ROT_DIM)
    ang = positions.astype(F32)[:, None] * inv_freq
    c, sn = jnp.cos(ang), jnp.sin(ang)
    rest = B_HEAD_DIM - ROT_DIM
    c64 = jnp.concatenate([c, c, jnp.ones((s, rest), F32)], axis=1)
    s64 = jnp.concatenate([-sn, sn, jnp.zeros((s, rest), F32)], axis=1)
    return jnp.tile(c64, (1, LANES // B_HEAD_DIM)), jnp.tile(s64, (1, LANES // B_HEAD_DIM))


def _local_step(x, positions, target, w, projections_of, projections_ahead, projection_grads, layer_grads_begin):
    cq, sq = _rope_tables(positions)
    saved = []
    xin, xin_b = x, x.astype(MXU_DTYPE)
    for layer in range(DEPTH):
        j = layer // 2
        kind = 'even' if layer % 2 == 0 else 'odd'
        w_in, w_out = projections_of(layer, xin_b)
        h = _mm_in(xin_b, w_in, W_IN_TRANSPOSED[kind], "mm_" + kind + "_in")
        if kind == 'even':
            behind, w_out_after = w_out(h) if callable(w_out) else (h, None)
            mix = _even_fwd(h, cq, sq, w['even_a_ln_g'][j], w['even_a_ln_b'][j], w['even_a_ws'][j],
                            w['even_a_bs'][j].T, w['even_b_sinks'][j], behind)
            w_out = w_out_after(mix) if w_out_after else w_out
        else:
            mix = _lru_fwd(h, w['odd_conv_w'][j], w['odd_conv_b'][j], w['odd_w_a'][j], w['odd_b_a'][j],
                           w['odd_w_x'][j], w['odd_b_x'][j], w['odd_lam'][j])
            mix = _pool_fwd(h, mix, w['odd_w_pool'][j], w['odd_d_scale'][j])
        if layer == DEPTH - 1:
            z, out, out_b = _mm_out_norm(mix, w_out, xin, w[kind + '_ln_g'][j], w[kind + '_ln_b'][j],
                                         "mm_out_norm_loss", target=target)
        else:
            z, out, out_b = _mm_out_norm(mix, w_out, xin, w[kind + '_ln_g'][j], w[kind + '_ln_b'][j], "mm_out_norm",
                                         after=projections_ahead(layer + 1, mix))
        saved.append((xin_b, h, mix, z, w_in, w_out))
        xin, xin_b = out, out_b

    dout, loss = xin, xin_b
    per_layer = {n: [None, None] for n in SMALL_NAMES}
    for layer in reversed(range(DEPTH)):
        j = layer // 2
        kind = 'even' if layer % 2 == 0 else 'odd'
        xin_b, h, mix, z, w_in, w_out = saved[layer]
        dz, dg, db, dmix, dw_out = _norm_out_bwd(dout, z, w[kind + '_ln_g'][j], w_out, mix,
                                                 layer_grads_begin(layer, dout), "norm_out_bwd")
        per_layer[kind + '_ln_g'][j], per_layer[kind + '_ln_b'][j] = dg[0], db[0]
        if kind == 'even':
            dh, dlg, dlb, dws, dbst, dsink = _even_bwd(
                h, dmix, cq, sq, w['even_a_ln_g'][j], w['even_a_ln_b'][j], w['even_a_ws'][j], w['even_a_bs'][j].T,
                w['even_b_sinks'][j])
            for n, val in (('even_a_ln_g', dlg[0]), ('even_a_ln_b', dlb[0]), ('even_a_ws', dws),
                           ('even_a_bs', dbst.T), ('even_b_sinks', dsink[0])):
                per_layer[n][j] = val
        else:
            dh, dcw, dcb, dwa, dba, dwx, dbx, dlam = _lru_bwd(
                h, dmix, w['odd_conv_w'][j], w['odd_conv_b'][j], w['odd_w_a'][j], w['odd_b_a'][j], w['odd_w_x'][j],
                w['odd_b_x'][j], w['odd_lam'][j])
            dh, dwp, dsc = _pool_bwd(h, dmix, dh, w['odd_w_pool'][j], w['odd_d_scale'][j])
            for n, val in (('odd_conv_w', dcw), ('odd_conv_b', dcb[0]), ('odd_w_a', dwa), ('odd_b_a', dba[0]),
                           ('odd_w_x', dwx), ('odd_b_x', dbx[0]), ('odd_lam', dlam[0]), ('odd_w_pool', dwp),
                           ('odd_d_scale', dsc[0])):
                per_layer[n][j] = val
        dw_in = _mm_dw_in(xin_b, dh, W_IN_TRANSPOSED[kind], "mm_dw_in_" + kind)
        dout = _mm_dx(dh, w_in, dz, W_IN_TRANSPOSED[kind], projection_grads(layer, dw_in, dw_out), "mm_dx_" + kind)
    return loss, dout, {n: jnp.stack(v) for n, v in per_layer.items()}


ANY = pl.BlockSpec(memory_space=pl.ANY)


def _where_am_i():
    return lax.axis_index("x"), lax.axis_index("y"), lax.axis_index("c")


def _other_chips(x, y):
    return [(1 - x, y), (x, 1 - y), (1 - x, 1 - y)]


PACKED_ROWS = 16


def _half_rows(rows, which):
    half = rows // 2
    assert half % PACKED_ROWS == 0, rows
    return pl.ds(pl.multiple_of(which * half, PACKED_ROWS), half)


def _sibling_swap(bufs, name, after=()):
    n, na = len(bufs), len(after)

    def body(*refs):
        x_refs, out_refs, (send_sems, recv_sems) = refs[:n], refs[n + na:2 * n + na], refs[2 * n + na:]
        x, y, c = _where_am_i()
        copies = []
        for p in range(n):
            copies.append(pltpu.make_async_remote_copy(
                src_ref=x_refs[p], dst_ref=out_refs[p], send_sem=send_sems.at[p], recv_sem=recv_sems.at[p],
                device_id=(x, y, 1 - c), device_id_type=MESH))
            copies[-1].start()
        for cp in copies:
            cp.wait()

    return pl.pallas_call(
        body, name=name, in_specs=[ANY] * (n + na), out_specs=[ANY] * n,
        out_shape=[jax.ShapeDtypeStruct(b.shape, b.dtype) for b in bufs],
        scratch_shapes=[pltpu.SemaphoreType.DMA((n,)), pltpu.SemaphoreType.DMA((n,))],
    )(*bufs, *after)


HBM = pl.BlockSpec(memory_space=pltpu.HBM)
SEM = pl.BlockSpec(memory_space=pltpu.SEMAPHORE)
IN_FLIGHT = pltpu.CompilerParams(has_side_effects=pltpu.SideEffectType.DATAFLOW_SIDE_EFFECTING)


IN_PLACE = ('forward',)


def _peers(kind, x, y, c):
    if kind in ('swap', 'join'):
        return [(x, y, 1 - c)]
    return [(x, y, 1 - c) if kind == 'forward' else (px, py, c) for px, py in _other_chips(x, y)]


def _n_peers(kind):
    return 1 if kind in ('swap', 'join') else N_CHIPS - 1


def _land_shape(kind, a):
    if kind == 'gather':
        return (N_CHIPS,) + a.shape
    return (a.shape[0], a.shape[1] // 2, a.shape[2]) if kind == 'swap' else a.shape


def _ici_copy(kind, src_ref, land_ref, send_sems, recv_sems, k, peer, x, y, c, arriving=False):
    me = 2 * x + y
    if kind == 'gather':
        half = _half_rows(src_ref.shape[0], c)
        src, dst = src_ref.at[half, :], land_ref.at[me, half, :]
    elif kind == 'scatter':
        src, dst = src_ref.at[2 * peer[0] + peer[1]], land_ref.at[me]
    elif kind == 'swap':
        src, dst = src_ref.at[:, _half_rows(src_ref.shape[1], 1 - c), :], land_ref
    elif kind == 'join':
        src, dst = src_ref, land_ref
    else:
        px, py = _other_chips(x, y)[k % (N_CHIPS - 1)]
        rows = src_ref.at[2 * px + py, _half_rows(src_ref.shape[1], 1 - c if arriving else c), :]
        src, dst = rows, rows
    return pltpu.make_async_remote_copy(src_ref=src, dst_ref=dst, send_sem=send_sems.at[k], recv_sem=recv_sems.at[k],
                                        device_id=peer, device_id_type=MESH)


def _ici_start(kind, groups, name, after=()):
    flat = [a for g in groups for a in g]
    n, ng, np_, na = len(flat), len(groups), _n_peers(kind), len(after)
    nl = 0 if kind in IN_PLACE else n
    land_shapes = [_land_shape(kind, a) for a in flat[:nl]]

    def body(*refs):
        srcs, sems, token = refs[:n], refs[n + nl + na:n + nl + na + 2 * ng], refs[-1]
        lands = refs[n:n + nl] if nl else srcs
        x, y, c = _where_am_i()
        p = 0
        for gi, g in enumerate(groups):
            for q in range(len(g)):
                for k, peer in enumerate(_peers(kind, x, y, c)):
                    _ici_copy(kind, srcs[p], lands[p], sems[2 * gi], sems[2 * gi + 1], np_ * q + k, peer, x, y,
                              c).start()
                p += 1
        token[...] = jnp.zeros_like(token)

    sem_shapes = [pltpu.SemaphoreType.DMA((np_ * len(g),)) for g in groups for _ in range(2)]
    outs = pl.pallas_call(
        body, name=name, in_specs=[HBM] * (n + nl) + [ANY] * na,
        out_specs=[SEM] * (2 * ng) + [HBM] * (n + nl) + [pl.BlockSpec(memory_space=pltpu.VMEM)],
        out_shape=sem_shapes + [pltpu.HBM(a.shape, a.dtype) for a in flat]
        + [pltpu.HBM(s, a.dtype) for s, a in zip(land_shapes, flat)] + [jax.ShapeDtypeStruct((8, LANES), F32)],
        input_output_aliases={i: 2 * ng + i for i in range(n + nl)}, compiler_params=IN_FLIGHT,
    )(*[pltpu.with_memory_space_constraint(a, pltpu.HBM) for a in flat],
      *[pltpu.with_memory_space_constraint(lax.empty(s, a.dtype), pltpu.HBM) for s, a in zip(land_shapes, flat)],
      *after)
    sems, srcs, token = outs[:2 * ng], outs[2 * ng:2 * ng + n], outs[-1]
    lands = outs[2 * ng + n:2 * ng + n + nl] if nl else srcs
    started, p = [], 0
    for gi, g in enumerate(groups):
        started.append((sems[2 * gi], sems[2 * gi + 1], srcs[p:p + len(g)], lands[p:p + len(g)]))
        p += len(g)
    return started, token


def _ici_wait(kind, started, after, name):
    send_sems, recv_sems, srcs, lands = started
    n, np_ = len(srcs), _n_peers(kind)
    nl = 0 if kind in IN_PLACE else n
    after = list(after)

    def body(*refs):
        src_refs, send_ref, recv_ref = refs[:n], refs[n + nl], refs[n + nl + 1]
        land_refs = refs[n:n + nl] if nl else src_refs
        x, y, c = _where_am_i()
        for q in range(n):
            for k, peer in enumerate(_peers(kind, x, y, c)):
                args = (kind, src_refs[q], land_refs[q], send_ref, recv_ref, np_ * q + k, peer, x, y, c)
                _ici_copy(*args).wait_send()
                _ici_copy(*args, arriving=True).wait_recv()

    arrays = list(srcs) + (list(lands) if nl else [])
    outs = pl.pallas_call(
        body, name=name, in_specs=[HBM] * (n + nl) + [SEM, SEM] + [ANY] * len(after), out_specs=[HBM] * (n + nl),
        out_shape=[pltpu.HBM(a.shape, a.dtype) for a in arrays],
        input_output_aliases={i: i for i in range(n + nl)}, compiler_params=IN_FLIGHT,
    )(*arrays, send_sems, recv_sems, *after)
    return outs[:n], (outs[n:] if nl else outs[:n])


def _forward_to_sibling(lands, name):
    n = len(lands)

    def body(*refs):
        out_refs, (send_sems, recv_sems) = refs[n:2 * n], refs[2 * n:]
        x, y, c = _where_am_i()

        def copy(q, k, peer, half_of):
            rows = out_refs[q].at[2 * peer[0] + peer[1], _half_rows(lands[q].shape[1], half_of), :]
            return pltpu.make_async_remote_copy(
                src_ref=rows, dst_ref=rows, send_sem=send_sems.at[3 * q + k], recv_sem=recv_sems.at[3 * q + k],
                device_id=(x, y, 1 - c), device_id_type=MESH)

        sends = [copy(q, k, peer, c) for q in range(n) for k, peer in enumerate(_other_chips(x, y))]
        for cp in sends:
            cp.start()
        for q in range(n):
            for k, peer in enumerate(_other_chips(x, y)):
                copy(q, k, peer, 1 - c).wait_recv()
        for cp in sends:
            cp.wait_send()

    return pl.pallas_call(
        body, name=name, in_specs=[ANY] * n, out_specs=[ANY] * n,
        out_shape=[jax.ShapeDtypeStruct(a.shape, a.dtype) for a in lands],
        input_output_aliases={i: i for i in range(n)},
        scratch_shapes=[pltpu.SemaphoreType.DMA((3 * n,)), pltpu.SemaphoreType.DMA((3 * n,))],
    )(*lands)


def _with_own_slot(lands, own, chip):
    return [lax.dynamic_update_slice(land, mine[None], (chip, 0, 0)) for land, mine in zip(lands, own)]


def _row_tile(rows, row_bytes, n_bufs, budget=24 * 1024 * 1024):
    best = None
    for t in range(PACKED_ROWS, rows + 1, PACKED_ROWS):
        if rows % t == 0 and 2 * n_bufs * t * row_bytes <= budget:
            best = t
    assert best is not None, (rows, row_bytes)
    return best


def _add_halves(full, recv, core, out_dtype, name):
    n, h, cols = recv.shape
    tr = _row_tile(h, cols * 4, 3)
    nt = h // tr

    def body(core_ref, a_ref, b_ref, o_ref):
        o_ref[...] = (a_ref[...].astype(F32) + b_ref[...].astype(F32)).astype(out_dtype)

    blk = (None, tr, cols)
    return pl.pallas_call(
        body, name=name,
        grid_spec=pltpu.PrefetchScalarGridSpec(
            num_scalar_prefetch=1, grid=(n, nt),
            in_specs=[pl.BlockSpec(blk, lambda j, i, core_ref: (j, core_ref[0] * nt + i, 0)),
                      pl.BlockSpec(blk, lambda j, i, core_ref: (j, i, 0))],
            out_specs=pl.BlockSpec(blk, lambda j, i, core_ref: (j, i, 0))),
        out_shape=jax.ShapeDtypeStruct((n, h, cols), out_dtype),
        compiler_params=_params(("parallel", "parallel")),
    )(core.reshape(1), full, recv)


def _add2(a, b, name):
    r, cols = a.shape
    tr = _row_tile(r, cols * 4, 3)

    def body(a_ref, b_ref, o_ref):
        o_ref[...] = a_ref[...] + b_ref[...]

    blk = pl.BlockSpec((tr, cols), lambda i: (i, 0))
    return pl.pallas_call(body, name=name, grid=(r // tr,), in_specs=[blk, blk], out_specs=blk,
                          out_shape=jax.ShapeDtypeStruct(a.shape, a.dtype), compiler_params=_params(("parallel",)))(a, b)


def _sum_slots(parts, own, chip, name):
    n, h, cols = parts.shape
    tr = _row_tile(h, cols * 4, n + 2)

    def body(chip_ref, p_ref, own_ref, o_ref):
        acc = None
        for k in range(n):
            term = jnp.where(chip_ref[0] == k, own_ref[...], p_ref[k]).astype(F32)
            acc = term if acc is None else acc + term
        o_ref[...] = acc

    own_spec = (pl.BlockSpec((None, tr, cols), lambda i, chip_ref: (chip_ref[0], i, 0)) if own.ndim == 3
                else pl.BlockSpec((tr, cols), lambda i, chip_ref: (i, 0)))
    return pl.pallas_call(
        body, name=name,
        grid_spec=pltpu.PrefetchScalarGridSpec(
            num_scalar_prefetch=1, grid=(h // tr,),
            in_specs=[pl.BlockSpec((n, tr, cols), lambda i, chip_ref: (0, i, 0)), own_spec],
            out_specs=pl.BlockSpec((tr, cols), lambda i, chip_ref: (i, 0))),
        out_shape=jax.ShapeDtypeStruct((h, cols), F32), compiler_params=_params(("parallel",)),
    )(chip.reshape(1), parts, own)


def _adamw(w, g, m, v, name):
    shape = w.shape
    cols = shape[-1]
    rows = math.prod(shape[:-1])
    tr = rows
    if rows % 8 == 0:
        tr = 8
        for t in range(8, rows + 1, 8):
            if rows % t == 0 and t * cols * 4 <= 1536 * 1024:
                tr = t
    c1 = 1.0 - ADAM_B1 ** ADAM_STEP
    c2 = 1.0 - ADAM_B2 ** ADAM_STEP

    def body(w_ref, g_ref, m_ref, v_ref, d_ref, nm_ref, nv_ref):
        gg = g_ref[...]
        nm = ADAM_B1 * m_ref[...] + (1.0 - ADAM_B1) * gg
        nv = ADAM_B2 * v_ref[...] + (1.0 - ADAM_B2) * (gg * gg)
        d_ref[...] = -ADAM_LR * ((nm / c1) / (jnp.sqrt(nv / c2) + ADAM_EPS) + ADAM_WD * w_ref[...])
        nm_ref[...] = nm
        nv_ref[...] = nv

    blk = pl.BlockSpec((tr, cols), lambda i: (i, 0))
    flat = jax.ShapeDtypeStruct((rows, cols), F32)
    outs = pl.pallas_call(
        body, name=name, grid=(rows // tr,), in_specs=[blk] * 4, out_specs=[blk] * 3, out_shape=[flat] * 3,
        compiler_params=_params(("parallel",)),
    )(*[a.reshape(rows, cols) for a in (w, g, m, v)])
    return [o.reshape(shape) for o in outs]


def _adamw_update(w, g, m, v):
    c1 = 1.0 - ADAM_B1 ** ADAM_STEP
    c2 = 1.0 - ADAM_B2 ** ADAM_STEP
    nm = ADAM_B1 * m + (1.0 - ADAM_B1) * g
    nv = ADAM_B2 * v + (1.0 - ADAM_B2) * (g * g)
    return -ADAM_LR * ((nm / c1) / (jnp.sqrt(nv / c2) + ADAM_EPS) + ADAM_WD * w), nm, nv


def _adamw_projection(w, m, v, mine, theirs, core, name):
    _, r, cols = w.shape
    h = r // 2
    tr = _row_tile(h, cols * 4, 12, budget=40 * 1024 * 1024)
    nt = h // tr

    def body(core_ref, w_ref, m_ref, v_ref, a0, b0, a1, b1, g_ref, d_ref, nm_ref, nv_ref):
        layer, half = pl.program_id(0), pl.program_id(1)
        own = half == core_ref[0]
        g = jnp.where(layer == 0, jnp.where(own, a0[...], b0[...]), jnp.where(own, a1[...], b1[...]))
        g_ref[...] = g
        d_ref[...], nm_ref[...], nv_ref[...] = _adamw_update(w_ref[...], g, m_ref[...], v_ref[...])

    def piece(layer, own):
        def index(l, hh, i, core_ref):
            used = (l == layer) & ((hh == core_ref[0]) == own)
            return (jnp.where(used, i, 0), 0)
        return pl.BlockSpec((tr, cols), index)

    whole = pl.BlockSpec((None, tr, cols), lambda l, hh, i, core_ref: (l, hh * nt + i, 0))
    return pl.pallas_call(
        body, name=name,
        grid_spec=pltpu.PrefetchScalarGridSpec(
            num_scalar_prefetch=1, grid=(2, 2, nt),
            in_specs=[whole] * 3 + [piece(0, True), piece(0, False), piece(1, True), piece(1, False)],
            out_specs=[whole] * 4),
        out_shape=[jax.ShapeDtypeStruct(w.shape, F32)] * 4,
        compiler_params=_params(("arbitrary", "arbitrary", "arbitrary")),
    )(core.reshape(1), w, m, v, mine[0], theirs[0], mine[1], theirs[1])


def _chip_shape(name):
    shape = list(FULL_SHAPES[name])
    axis = BIG_SHARD_AXIS.get(name, SMALL_SHARD_AXIS.get(name))
    if axis is not None:
        shape[axis] //= N_CHIPS
    return tuple(shape)


def _round_up(n, k):
    return -(-n // k) * k


SHARDED_SMALL_WORDS = sum(math.prod(_chip_shape(n)) for n in SMALL_SHARDED)
SHARDED_SMALL_ROWS = _round_up(-(-SHARDED_SMALL_WORDS // PACK_COLS), 32)
SMALL_MATRICES = ['even_a_ws', 'odd_w_a', 'odd_w_x', 'odd_w_pool']
SMALL_VECTORS = [n for n in SMALL_NAMES if n not in SMALL_MATRICES]
VECTOR_ROWS = _round_up(-(-(sum(math.prod(FULL_SHAPES[n]) for n in SMALL_VECTORS) + 1) // PACK_COLS), 32)


def _pack_rows(flat, rows):
    return jnp.pad(flat, (0, rows * PACK_COLS - flat.shape[0])).reshape(rows, PACK_COLS)


def _pack_small_sharded(shards):
    return _pack_rows(jnp.concatenate([shards[n].reshape(-1) for n in SMALL_SHARDED]), SHARDED_SMALL_ROWS)


def _unpack_small_sharded(gathered_small):
    out = {}
    words = gathered_small.reshape(N_CHIPS, -1)
    w0 = 0
    for n in SMALL_SHARDED:
        size = math.prod(_chip_shape(n))
        blocks = words[:, w0:w0 + size].reshape((N_CHIPS,) + _chip_shape(n))
        out[n] = jnp.concatenate([blocks[j] for j in range(N_CHIPS)], axis=SMALL_SHARD_AXIS[n])
        w0 += size
    return out


def _small_grad_parts(grads, loss):
    words = [grads[n].reshape(-1) for n in SMALL_VECTORS] + [loss.reshape(-1)]
    return ([_pack_rows(jnp.concatenate(words), VECTOR_ROWS)]
            + [grads[n].reshape(-1, FULL_SHAPES[n][-1]) for n in SMALL_MATRICES])


def _own_block(name, full, chip):
    if name not in SMALL_SHARD_AXIS:
        return full
    axis = SMALL_SHARD_AXIS[name]
    width = FULL_SHAPES[name][axis] // N_CHIPS
    return lax.dynamic_slice_in_dim(full, chip * width, width, axis)


def _from_small_grad_parts(totals, chip):
    flat = totals[0].reshape(-1)
    out = {}
    w0 = 0
    for n in SMALL_VECTORS:
        size = math.prod(FULL_SHAPES[n])
        out[n] = _own_block(n, flat[w0:w0 + size].reshape(FULL_SHAPES[n]), chip)
        w0 += size
    for n, total in zip(SMALL_MATRICES, totals[1:]):
        out[n] = _own_block(n, total.reshape(FULL_SHAPES[n]), chip)
    return out, flat[w0]


def kernel(x, positions, even_w_in, even_a_ln_g, even_a_ln_b, even_a_ws, even_a_bs, even_b_sinks, even_w_out, even_ln_g, even_ln_b, odd_w_in, odd_conv_w, odd_conv_b, odd_w_a, odd_b_a, odd_w_x, odd_b_x, odd_lam, odd_w_pool, odd_d_scale, odd_w_out, odd_ln_g, odd_ln_b, loss_target, m_even_w_in, m_even_a_ln_g, m_even_a_ln_b, m_even_a_ws, m_even_a_bs, m_even_b_sinks, m_even_w_out, m_even_ln_g, m_even_ln_b, m_odd_w_in, m_odd_conv_w, m_odd_conv_b, m_odd_w_a, m_odd_b_a, m_odd_w_x, m_odd_b_x, m_odd_lam, m_odd_w_pool, m_odd_d_scale, m_odd_w_out, m_odd_ln_g, m_odd_ln_b, v_even_w_in, v_even_a_ln_g, v_even_a_ln_b, v_even_a_ws, v_even_a_bs, v_even_b_sinks, v_even_w_out, v_even_ln_g, v_even_ln_b, v_odd_w_in, v_odd_conv_w, v_odd_conv_b, v_odd_w_a, v_odd_b_a, v_odd_w_x, v_odd_b_x, v_odd_lam, v_odd_w_pool, v_odd_d_scale, v_odd_w_out, v_odd_ln_g, v_odd_ln_b):
    local = dict(zip(WEIGHT_NAMES, (even_w_in, even_a_ln_g, even_a_ln_b, even_a_ws, even_a_bs, even_b_sinks, even_w_out, even_ln_g, even_ln_b, odd_w_in, odd_conv_w, odd_conv_b, odd_w_a, odd_b_a, odd_w_x, odd_b_x, odd_lam, odd_w_pool, odd_d_scale, odd_w_out, odd_ln_g, odd_ln_b)))
    mom = dict(zip(WEIGHT_NAMES, (m_even_w_in, m_even_a_ln_g, m_even_a_ln_b, m_even_a_ws, m_even_a_bs, m_even_b_sinks, m_even_w_out, m_even_ln_g, m_even_ln_b, m_odd_w_in, m_odd_conv_w, m_odd_conv_b, m_odd_w_a, m_odd_b_a, m_odd_w_x, m_odd_b_x, m_odd_lam, m_odd_w_pool, m_odd_d_scale, m_odd_w_out, m_odd_ln_g, m_odd_ln_b)))
    vel = dict(zip(WEIGHT_NAMES, (v_even_w_in, v_even_a_ln_g, v_even_a_ln_b, v_even_a_ws, v_even_a_bs, v_even_b_sinks, v_even_w_out, v_even_ln_g, v_even_ln_b, v_odd_w_in, v_odd_conv_w, v_odd_conv_b, v_odd_w_a, v_odd_b_a, v_odd_w_x, v_odd_b_x, v_odd_lam, v_odd_w_pool, v_odd_d_scale, v_odd_w_out, v_odd_ln_g, v_odd_ln_b)))
    mx, my, core = _where_am_i()
    chip = 2 * mx + my

    stored = lambda n, a: jnp.swapaxes(a, 1, 2) if n == 'even_w_in' and W_IN_TRANSPOSED['even'] else a
    names_of = lambda layer: [('even' if layer % 2 == 0 else 'odd') + s for s in ('_w_in', '_w_out')]

    shards = [[stored(n, local[n])[layer // 2].astype(MXU_DTYPE) for n in names_of(layer)] for layer in range(DEPTH)]
    (first_in, first_out), first_token = _ici_start(
        'gather', [[_pack_small_sharded(local), shards[0][0]], [shards[0][1]]], "gather_start_first")
    later, gather_token = _ici_start('gather', shards[1:], "gather_start", after=[first_token])
    gathers = [None] + later

    own, lands = _ici_wait('gather', first_in, [first_token, gather_token], "gather_wait_0")
    small_sharded, first_w_in = _with_own_slot(_forward_to_sibling(lands, "gather_forward_0"), own, chip)
    weights = dict(local)
    weights.update(_unpack_small_sharded(small_sharded))
    passing = {}

    def passed_on(started, after, name):
        own, lands = _ici_wait('gather', started, [after], "gather_wait_" + name)
        (forward,), token = _ici_start('forward', [lands], "forward_start_" + name)

        def finish(then):
            done, _ = _ici_wait('forward', forward, [then], "forward_wait_" + name)
            return _with_own_slot(done, own, chip)

        return token, finish

    def first_w_out(h):
        token, finish = passed_on(first_out, h, "0_out")
        return token, lambda mix: finish(mix)[0]

    def projections_ahead(layer, after):
        token, passing[layer] = passed_on(gathers[layer], after, str(layer))
        return token

    def projections_of(layer, after):
        return (first_w_in, first_w_out) if layer == 0 else passing.pop(layer)(after)

    swapping, in_flight, mine, started = [], [], {}, {}

    def land(*after):
        layer, scatter = in_flight.pop()
        parts, lands = _ici_wait('scatter', scatter, after, "scatter_wait_%d" % layer)
        for n, arrived, part in zip(names_of(layer), lands, parts):
            mine[n, layer // 2] = _sum_slots(arrived, part, chip, "big_sum_%s_%d" % (n, layer // 2))

    odd_pieces = [(n, l) for n in ('odd_w_in', 'odd_w_out') for l in range(2)]
    joining = []

    def projection_grads(layer, dw_in, dw_out):
        if in_flight:
            land(dw_in)
        (swap,), token = _ici_start('swap', [[dw_in, dw_out]], "presum_start_%d" % layer)
        swapping.append((layer, swap))
        if layer > 0:
            return token
        (join,), token = _ici_start('join', [[mine[p] for p in odd_pieces]], "join_odd_start",
                                    after=[presum_and_scatter(token)])
        joining.append(join)
        return token

    def presum_and_scatter(*after):
        layer, swap = swapping.pop()
        partial, from_sibling = _ici_wait('swap', swap, after, "presum_wait_%d" % layer)
        chip_part = [_add_halves(g, r, core, MXU_DTYPE, "big_presum_add_%s_%d" % (n, layer // 2))
                     for g, r, n in zip(partial, from_sibling, names_of(layer))]
        (scatter,), started[layer] = _ici_start('scatter', [chip_part], "scatter_start_%d" % layer)
        in_flight.append((layer, scatter))
        return started[layer]

    layer_grads_begin = lambda layer, dout: presum_and_scatter(dout) if swapping else dout
    loss, grad_x, grads = _local_step(x[0], positions[0], loss_target[0], weights, projections_of, projections_ahead,
                                      projection_grads, layer_grads_begin)

    (small_swap,), small_token = _ici_start('join', [_small_grad_parts(grads, loss)], "small_swap_start",
                                            after=[started[0]])

    reduced, deltas, new_m, new_v = {}, {}, {}, {}

    def adamw_projections(kind, after):
        names = [kind + '_w_in', kind + '_w_out']
        pieces = [(n, l) for n in names for l in range(2)]
        if kind == 'odd':
            halves, theirs = _ici_wait('join', joining.pop(), after, "join_odd_wait")
            mine.update(zip(pieces, halves))
        else:
            theirs = _sibling_swap([mine[p] for p in pieces], "big_join_" + kind, after=after)
        theirs = dict(zip(pieces, theirs))
        done = []
        for n in names:
            outs = _adamw_projection(stored(n, local[n]), stored(n, mom[n]), stored(n, vel[n]),
                                     [mine[n, l] for l in range(2)], [theirs[n, l] for l in range(2)], core,
                                     "adamw_" + n)
            reduced[n], deltas[n], new_m[n], new_v[n] = [stored(n, o) for o in outs]
            done.append(outs[-1])
        return done

    done = adamw_projections('odd', [small_token])
    small, from_sibling = _ici_wait('join', small_swap, done, "small_swap_wait")
    chip_sums = [_add2(a, b, "small_presum_add_%d" % i) for i, (a, b) in enumerate(zip(small, from_sibling))]
    (small_gather,), small_token = _ici_start('gather', [chip_sums], "small_gather_start")
    land(grad_x, small_token)
    done = adamw_projections('even', [])

    own, lands = _ici_wait('gather', small_gather, done, "small_gather_wait")
    of_chips = _forward_to_sibling(lands, "small_gather_forward")
    small_totals, total_loss = _from_small_grad_parts(
        [_sum_slots(g, mine_i, chip, "small_sum_%d" % i) for i, (g, mine_i) in enumerate(zip(of_chips, own))], chip)
    reduced.update(small_totals)
    for n in SMALL_NAMES:
        deltas[n], new_m[n], new_v[n] = _adamw(local[n], reduced[n], mom[n], vel[n], "adamw_" + n)

    return (total_loss, grad_x[None], *[reduced[n] for n in WEIGHT_NAMES], *[deltas[n] for n in WEIGHT_NAMES],
            *[new_m[n] for n in WEIGHT_NAMES], *[new_v[n] for n in WEIGHT_NAMES])
```

```python
import math

import jax
import jax.numpy as jnp
from jax import lax
from jax.experimental import pallas as pl
from jax.experimental.pallas import tpu as pltpu

F32 = jnp.float32
MXU_DTYPE = jnp.bfloat16

DEPTH = 4
A_WIDTH = 1024
A_GROUPS = 8
CHUNK = 128
B_HEAD_DIM = 64
B_Q_HEADS = 16
B_KV_HEADS = 2
WINDOW = 128
ROT_DIM = 16
ROPE_THETA = 500000.0
C_WIDTH = 1024
C_HEADS = 8
CONV_WIDTH = 4
LRU_C = 8.0
D_WIDTH = 1024
D_GROUPS = 4
D_GROUP_DIM = 256
EVEN_IN = 5376
DN_ALPHA = (2 * DEPTH) ** 0.25
LN_EPS = 1e-5
ATTN_SCALE = B_HEAD_DIM ** -0.5
NEG_BIG = -1e30

ADAM_LR = 0.001
ADAM_B1 = 0.9
ADAM_B2 = 0.999
ADAM_EPS = 1e-08
ADAM_WD = 0.01
ADAM_STEP = 10

LANES = 128
VMEM_LIMIT = 56 * 1024 * 1024
N_CHIPS = 4
MESH = pl.DeviceIdType.MESH

WEIGHT_NAMES = ['even_w_in', 'even_a_ln_g', 'even_a_ln_b', 'even_a_ws', 'even_a_bs', 'even_b_sinks', 'even_w_out',
                'even_ln_g', 'even_ln_b', 'odd_w_in', 'odd_conv_w', 'odd_conv_b', 'odd_w_a', 'odd_b_a', 'odd_w_x',
                'odd_b_x', 'odd_lam', 'odd_w_pool', 'odd_d_scale', 'odd_w_out', 'odd_ln_g', 'odd_ln_b']
BIG_SHARD_AXIS = {'even_w_in': 2, 'even_w_out': 1, 'odd_w_in': 2, 'odd_w_out': 1}
SMALL_SHARD_AXIS = {'odd_conv_w': 2, 'odd_conv_b': 1, 'odd_b_a': 1, 'odd_b_x': 1, 'odd_lam': 1, 'odd_w_pool': 2,
                    'odd_d_scale': 1, 'odd_ln_g': 1, 'odd_ln_b': 1}
FULL_SHAPES = {
    'even_w_in': (2, 2048, 5376), 'even_a_ln_g': (2, 1024), 'even_a_ln_b': (2, 1024), 'even_a_ws': (2, 8, 128, 128),
    'even_a_bs': (2, 8, 128), 'even_b_sinks': (2, 16), 'even_w_out': (2, 2048, 2048), 'even_ln_g': (2, 2048),
    'even_ln_b': (2, 2048), 'odd_w_in': (2, 2048, 4096), 'odd_conv_w': (2, 4, 1024), 'odd_conv_b': (2, 1024),
    'odd_w_a': (2, 8, 128, 128), 'odd_b_a': (2, 1024), 'odd_w_x': (2, 8, 128, 128), 'odd_b_x': (2, 1024),
    'odd_lam': (2, 1024), 'odd_w_pool': (2, 4, 256, 256), 'odd_d_scale': (2, 1024), 'odd_w_out': (2, 2048, 2048),
    'odd_ln_g': (2, 2048), 'odd_ln_b': (2, 2048)}
BIG_NAMES = ['even_w_in', 'even_w_out', 'odd_w_in', 'odd_w_out']
SMALL_SHARDED = ['odd_conv_w', 'odd_conv_b', 'odd_b_a', 'odd_b_x', 'odd_lam', 'odd_d_scale', 'odd_ln_g', 'odd_ln_b',
                 'odd_w_pool']
SMALL_NAMES = [n for n in WEIGHT_NAMES if n not in BIG_NAMES]
W_IN_TRANSPOSED = {'even': True, 'odd': False}
PACK_COLS = 1024


def _params(sem):
    return pltpu.CompilerParams(dimension_semantics=sem, vmem_limit_bytes=VMEM_LIMIT)


def _dot(a, b):
    return lax.dot_general(a, b, (((1,), (0,)), ((), ())), preferred_element_type=F32)


def _dot_nt(a, b):
    return lax.dot_general(a, b, (((1,), (1,)), ((), ())), preferred_element_type=F32)


def _dot_tn(a, b):
    return lax.dot_general(a, b, (((0,), (0,)), ((), ())), preferred_element_type=F32)


def _sigmoid(x):
    return 0.5 * jnp.tanh(0.5 * x) + 0.5


def _silu_and_grad(x):
    s = _sigmoid(x)
    return x * s, s * (1.0 + x * (1.0 - s))


def _mm_call(body, name, grid, in_specs, out_spec, out_shape, args):
    return pl.pallas_call(body, name=name, grid=grid, in_specs=in_specs, out_specs=out_spec, out_shape=out_shape,
                          compiler_params=_params(("parallel",) * len(grid)))(*args)


def _resident(block, index_map):
    return pl.BlockSpec(block, index_map, pipeline_mode=pl.Buffered(1))


ROWS_OF_WT = 768


def _mm_in(x, wg, transposed, name):
    s, k = x.shape
    x_spec = _resident((s, k), lambda j: (0, 0))

    def body(x_ref, w_ref, o_ref):
        o_ref[...] = (_dot_nt if transposed else _dot)(x_ref[...], w_ref[...])

    if transposed:
        wt = wg.reshape(-1, k)
        n, tn = wt.shape[0], ROWS_OF_WT
        return _mm_call(body, name, (n // tn,), [x_spec, pl.BlockSpec((tn, k), lambda j: (j, 0))],
                        pl.BlockSpec((s, tn), lambda j: (0, j)), jax.ShapeDtypeStruct((s, n), F32), (x, wt))
    nc = wg.shape[2]
    return _mm_call(body, name, (N_CHIPS,), [x_spec, pl.BlockSpec((None, k, nc), lambda j: (j, 0, 0))],
                    pl.BlockSpec((None, s, nc), lambda j: (j, 0, 0)),
                    jax.ShapeDtypeStruct((N_CHIPS, s, nc), F32), (x, wg))


def _mm_dx(dh, wg, dz, transposed, after, name):
    s, d = dz.shape
    tm, tn = min(1024, s), 512
    res = pl.BlockSpec((tm, tn), lambda i, j: (i, j))
    if transposed:
        wt = wg.reshape(-1, d)
        n = wt.shape[0]

        def body(a_ref, w_ref, r_ref, after_ref, o_ref):
            o_ref[...] = DN_ALPHA * r_ref[...] + _dot(a_ref[...], w_ref[...])

        return _mm_call(body, name, (s // tm, d // tn),
                        [pl.BlockSpec((tm, n), lambda i, j: (i, 0)), pl.BlockSpec((n, tn), lambda i, j: (0, j)), res,
                         ANY], res, jax.ShapeDtypeStruct((s, d), F32), (dh, wt, dz, after))
    nc = dh.shape[2]

    def body(a_ref, w_ref, r_ref, after_ref, o_ref):
        acc = DN_ALPHA * r_ref[...]
        for j in range(N_CHIPS):
            acc = acc + _dot_nt(a_ref[j], w_ref[j])
        o_ref[...] = acc

    return _mm_call(body, name, (s // tm, d // tn),
                    [pl.BlockSpec((N_CHIPS, tm, nc), lambda i, j: (0, i, 0)),
                     pl.BlockSpec((N_CHIPS, tn, nc), lambda i, j: (0, j, 0)), res, ANY],
                    res, jax.ShapeDtypeStruct((s, d), F32), (dh, wg, dz, after))


def _mm_dw_in(x, dh, transposed, name):
    s, d = x.shape
    x_spec = _resident((s, d), lambda j: (0, 0))

    def body(a_ref, b_ref, o_ref):
        o_ref[...] = _dot_tn(a_ref[...], b_ref[...]).astype(o_ref.dtype)

    if transposed:
        n, tn = dh.shape[1], ROWS_OF_WT
        flat = _mm_call(body, name, (n // tn,), [pl.BlockSpec((s, tn), lambda j: (0, j)), x_spec],
                        pl.BlockSpec((tn, d), lambda j: (j, 0)), jax.ShapeDtypeStruct((n, d), MXU_DTYPE), (dh, x))
        return flat.reshape(N_CHIPS, n // N_CHIPS, d)
    nc = dh.shape[2]
    return _mm_call(body, name, (N_CHIPS,), [x_spec, pl.BlockSpec((None, s, nc), lambda j: (j, 0, 0))],
                    pl.BlockSpec((None, d, nc), lambda j: (j, 0, 0)),
                    jax.ShapeDtypeStruct((N_CHIPS, d, nc), MXU_DTYPE), (x, dh))


def _mm_out_norm(mix, wg, x, g, b, name, target=None, after=None):
    s, k = mix.shape
    _, kc, d = wg.shape
    tm = min(512, s)

    def normed(a_ref, w_ref, x_ref, g_ref, b_ref, z_ref):
        z = DN_ALPHA * x_ref[...] + _dot(a_ref[...], w_ref[...].reshape(N_CHIPS * kc, d))
        mu = jnp.mean(z, axis=1, keepdims=True)
        zc = z - mu
        var = jnp.mean(zc * zc, axis=1, keepdims=True)
        z_ref[...] = z
        return zc * lax.rsqrt(var + LN_EPS) * g_ref[...] + b_ref[...]

    def body(a_ref, w_ref, x_ref, g_ref, b_ref, after_ref, z_ref, o_ref, ob_ref):
        o = normed(a_ref, w_ref, x_ref, g_ref, b_ref, z_ref)
        o_ref[...] = o
        ob_ref[...] = o.astype(ob_ref.dtype)

    def body_with_loss(a_ref, w_ref, x_ref, g_ref, b_ref, t_ref, z_ref, do_ref, l_ref):
        @pl.when(pl.program_id(0) == 0)
        def _():
            l_ref[...] = jnp.zeros_like(l_ref)

        e = normed(a_ref, w_ref, x_ref, g_ref, b_ref, z_ref) - t_ref[...]
        do_ref[...] = e * (1.0 / d)
        l_ref[...] += (0.5 / d) * jnp.sum(jnp.sum(e * e, axis=1, keepdims=True), axis=0, keepdims=True)

    row = pl.BlockSpec((tm, d), lambda i: (i, 0))
    vec = pl.BlockSpec((1, d), lambda i: (0, 0))
    in_specs = [pl.BlockSpec((tm, k), lambda i: (i, 0)), _resident((N_CHIPS, kc, d), lambda i: (0, 0, 0)), row, vec,
                vec]
    args = (mix, wg, x, g.reshape(1, d), b.reshape(1, d))
    f32_rows = jax.ShapeDtypeStruct((s, d), F32)
    if target is None:
        return pl.pallas_call(
            body, name=name, grid=(s // tm,), in_specs=in_specs + [ANY], out_specs=[row, row, row],
            out_shape=[f32_rows, f32_rows, jax.ShapeDtypeStruct((s, d), MXU_DTYPE)],
            compiler_params=_params(("parallel",)))(*args, after)
    return pl.pallas_call(
        body_with_loss, name=name, grid=(s // tm,), in_specs=in_specs + [row],
        out_specs=[row, row, pl.BlockSpec((1, 1), lambda i: (0, 0))],
        out_shape=[f32_rows, f32_rows, jax.ShapeDtypeStruct((1, 1), F32)],
        compiler_params=_params(("arbitrary",)))(*args, target)


NORM_BWD_ROWS = 256


def _norm_out_bwd(dout, z, g, wg, mix, after, name):
    s, d = z.shape
    kc = wg.shape[1]
    tr = min(NORM_BWD_ROWS, s)
    nr = s // tr

    def body(do_ref, z_ref, g_ref, w_ref, mix_ref, after_ref, dz_ref, dg_ref, db_ref, dmix_ref, dw_ref, dzb_scr):
        i = pl.program_id(0)

        @pl.when(i == 0)
        def _():
            dg_ref[...] = jnp.zeros_like(dg_ref)
            db_ref[...] = jnp.zeros_like(db_ref)

        @pl.when(i < nr)
        def _():
            zz = z_ref[...]
            do = do_ref[...]
            mu = jnp.mean(zz, axis=1, keepdims=True)
            zc = zz - mu
            var = jnp.mean(zc * zc, axis=1, keepdims=True)
            rstd = lax.rsqrt(var + LN_EPS)
            zh = zc * rstd
            dg_ref[...] += jnp.sum(do * zh, axis=0, keepdims=True)
            db_ref[...] += jnp.sum(do, axis=0, keepdims=True)
            dzh = do * g_ref[...]
            dz = rstd * (dzh - jnp.mean(dzh, axis=1, keepdims=True) - zh * jnp.mean(dzh * zh, axis=1, keepdims=True))
            dz_ref[...] = dz
            dzb_scr[pl.ds(pl.multiple_of(i * tr, tr), tr), :] = dz.astype(dzb_scr.dtype)

        @pl.when(i >= nr)
        def _():
            dzb = dzb_scr[...]
            dmix_ref[...] = _dot_nt(dzb, w_ref[...])
            dw_ref[...] = _dot_tn(mix_ref[...], dzb).astype(dw_ref.dtype)

    row = pl.BlockSpec((tr, d), lambda i: (jnp.minimum(i, nr - 1), 0))
    vec = pl.BlockSpec((1, d), lambda i: (0, 0))
    shard = lambda i: jnp.maximum(i - nr, 0)
    return pl.pallas_call(
        body, name=name, grid=(nr + N_CHIPS,),
        in_specs=[row, row, vec, pl.BlockSpec((None, kc, d), lambda i: (shard(i), 0, 0)),
                  pl.BlockSpec((s, kc), lambda i: (0, shard(i))), ANY],
        out_specs=[row, vec, vec, pl.BlockSpec((s, kc), lambda i: (0, shard(i))),
                   pl.BlockSpec((None, kc, d), lambda i: (shard(i), 0, 0))],
        out_shape=[jax.ShapeDtypeStruct((s, d), F32), jax.ShapeDtypeStruct((1, d), F32),
                   jax.ShapeDtypeStruct((1, d), F32), jax.ShapeDtypeStruct((s, N_CHIPS * kc), F32),
                   jax.ShapeDtypeStruct((N_CHIPS, kc, d), MXU_DTYPE)],
        scratch_shapes=[pltpu.VMEM((s, d), MXU_DTYPE)],
        compiler_params=_params(("arbitrary",)),
    )(dout, z, g.reshape(1, d), wg, mix, after)


Q0, K0, V0, BG0 = 3072, 4096, 4224, 4352
T = CHUNK


def _lane_ids(width):
    return lax.broadcasted_iota(jnp.int32, (T, width), 1)


def _rope_swap(x):
    w = x.shape[1]
    l64 = _lane_ids(w) % B_HEAD_DIM
    half = ROT_DIM // 2
    return jnp.where(l64 < half, pltpu.roll(x, w - half, 1), jnp.where(l64 < ROT_DIM, pltpu.roll(x, half, 1), 0.0))


def _rope(x, c, s):
    return x * c + _rope_swap(x) * s


def _rope_bwd(d, c, s):
    return d * c + _rope_swap(d * s)


def _tril():
    return lax.broadcasted_iota(jnp.int32, (T, T), 0) >= lax.broadcasted_iota(jnp.int32, (T, T), 1)


def _band_mask(has_prev):
    qi = lax.broadcasted_iota(jnp.int32, (T, 2 * T), 0)
    kj = lax.broadcasted_iota(jnp.int32, (T, 2 * T), 1)
    return (kj > qi) & (kj <= qi + WINDOW) & ((kj >= WINDOW) | has_prev)


def _a_norm(v, g, b):
    mu = jnp.mean(v, axis=1, keepdims=True)
    vc = v - mu
    var = jnp.mean(vc * vc, axis=1, keepdims=True)
    rstd = lax.rsqrt(var + LN_EPS)
    vh = vc * rstd
    return vh, rstd, vh * g + b


def _kv_operands(hk, k_rot, kp_rot, vv, vp):
    lo = _lane_ids(LANES) < B_HEAD_DIM
    mine = lo if hk == 0 else jnp.logical_not(lo)
    both = lambda t: jnp.where(mine, t, 0.0) + pltpu.roll(jnp.where(mine, t, 0.0), B_HEAD_DIM, 1)
    k2 = jnp.concatenate([both(kp_rot), both(k_rot)], axis=0)
    v2 = jnp.concatenate([both(vp), both(vv)], axis=0)
    return k2.astype(MXU_DTYPE), v2.astype(MXU_DTYPE)


GROUP_HEADS = B_Q_HEADS // B_KV_HEADS


def _pair_columns(hk):
    return [(hk * GROUP_HEADS // 2 + pp) * LANES for pp in range(GROUP_HEADS // 2)]


def _stacked_heads(pairs):
    lo = _lane_ids(LANES) < B_HEAD_DIM
    rows = []
    for tile in pairs:
        rows += [jnp.where(lo, tile, 0.0), jnp.where(lo, 0.0, tile)]
    return jnp.concatenate(rows, axis=0).astype(MXU_DTYPE)


def _pair_of(stacked, pp):
    lo = _lane_ids(LANES) < B_HEAD_DIM
    return jnp.where(lo, stacked[2 * pp * T:(2 * pp + 1) * T], stacked[(2 * pp + 1) * T:(2 * pp + 2) * T])


def _softmax_with_sink(qm, k2, valid, sink_ref, head0):
    scores = (_dot_nt(qm, k2) * ATTN_SCALE).reshape(GROUP_HEADS, T, 2 * T)
    s = jnp.where(valid[None], scores, NEG_BIG).reshape(GROUP_HEADS * T, 2 * T)
    sink = jnp.concatenate([jnp.broadcast_to(sink_ref[:, head0 + h:head0 + h + 1], (T, 1))
                            for h in range(GROUP_HEADS)], axis=0)
    m = jnp.maximum(jnp.max(s, axis=1, keepdims=True), sink)
    ex = jnp.exp(s - m)
    es = jnp.exp(sink - m)
    inv = 1.0 / (jnp.sum(ex, axis=1, keepdims=True) + es)
    return ex * inv, es * inv


def _even_specs(s):
    nb = s // T
    prev = lambda r: lambda i: jnp.maximum(r(i) - 1, 0)
    return nb, {
        'h': lambda r: pl.BlockSpec((T, EVEN_IN), lambda i: (r(i), 0)),
        'kprev': lambda r: pl.BlockSpec((T, LANES), lambda i: (prev(r)(i), K0 // LANES)),
        'vprev': lambda r: pl.BlockSpec((T, LANES), lambda i: (prev(r)(i), V0 // LANES)),
        'tab': lambda r: pl.BlockSpec((T, LANES), lambda i: (r(i), 0)),
        'tabprev': lambda r: pl.BlockSpec((T, LANES), lambda i: (prev(r)(i), 0)),
    }


def _full(shape):
    nd = len(shape)
    return pl.BlockSpec(shape, lambda i: (0,) * nd)


def _even_fwd(h, cq, sq, ln_g, ln_b, ws, bs_t, sinks, after):
    s = h.shape[0]
    nb, sp = _even_specs(s)

    def body(h_ref, kp_ref, vp_ref, cq_ref, sq_ref, ckp_ref, skp_ref, g_ref, b_ref, ws_ref, bst_ref, sink_ref,
             after_ref, mix_ref):
        i = pl.program_id(0)
        tril = _tril()
        _, _, vn = _a_norm(h_ref[:, 1024:2048], g_ref[...], b_ref[...])
        vnb = vn.astype(MXU_DTYPE)
        for g in range(A_GROUPS):
            c0, c1 = g * LANES, (g + 1) * LANES
            wg = jnp.where(tril, ws_ref[g], 0.0).astype(MXU_DTYPE)
            mixed = _dot(wg, vnb[:, c0:c1]) + bst_ref[:, g:g + 1]
            sil, _ = _silu_and_grad(h_ref[:, 2048 + c0:2048 + c1])
            mix_ref[:, c0:c1] = (h_ref[:, c0:c1] * mixed * sil).astype(mix_ref.dtype)

        cq, sq = cq_ref[...], sq_ref[...]
        k_rot = _rope(h_ref[:, K0:K0 + LANES], cq, sq)
        kp_rot = _rope(kp_ref[...], ckp_ref[...], skp_ref[...])
        valid = _band_mask(i > 0)
        for hk in range(B_KV_HEADS):
            k2, v2 = _kv_operands(hk, k_rot, kp_rot, h_ref[:, V0:V0 + LANES], vp_ref[...])
            qm = _stacked_heads([_rope(h_ref[:, Q0 + c0:Q0 + c0 + LANES], cq, sq) for c0 in _pair_columns(hk)])
            p, _ = _softmax_with_sink(qm, k2, valid, sink_ref, GROUP_HEADS * hk)
            out = _dot(p.astype(MXU_DTYPE), v2)
            for pp, c0 in enumerate(_pair_columns(hk)):
                sil, _ = _silu_and_grad(h_ref[:, BG0 + c0:BG0 + c0 + LANES])
                mix_ref[:, A_WIDTH + c0:A_WIDTH + c0 + LANES] = (_pair_of(out, pp) * sil).astype(mix_ref.dtype)

    ident = lambda i: i
    return pl.pallas_call(
        body, name="even_fwd", grid=(nb,),
        in_specs=[sp['h'](ident), sp['kprev'](ident), sp['vprev'](ident), sp['tab'](ident), sp['tab'](ident),
                  sp['tabprev'](ident), sp['tabprev'](ident), _full((1, A_WIDTH)), _full((1, A_WIDTH)),
                  _full((A_GROUPS, T, T)), _full((T, A_GROUPS)), _full((1, B_Q_HEADS)), ANY],
        out_specs=pl.BlockSpec((T, 2 * A_WIDTH), lambda i: (i, 0)),
        out_shape=jax.ShapeDtypeStruct((s, 2 * A_WIDTH), MXU_DTYPE),
        compiler_params=_params(("parallel",)),
    )(h, h, h, cq, sq, cq, sq, ln_g.reshape(1, -1), ln_b.reshape(1, -1), ws, bs_t, sinks.reshape(1, -1), after)


def _even_bwd(h, dmix, cq, sq, ln_g, ln_b, ws, bs_t, sinks):
    s = h.shape[0]
    nb, sp = _even_specs(s)

    def body(h_ref, kp_ref, vp_ref, cq_ref, sq_ref, ckp_ref, skp_ref, dm_ref, g_ref, b_ref, ws_ref, bst_ref, sink_ref,
             dh_ref, dg_ref, db_ref, dws_ref, dbst_ref, dsink_ref, dvn_scr, ck_scr, cv_scr):
        i = pl.program_id(0)

        @pl.when(i == 0)
        def _():
            for ref in (dg_ref, db_ref, dws_ref, dbst_ref, dsink_ref, ck_scr, cv_scr):
                ref[...] = jnp.zeros_like(ref)

        tril = _tril()
        vh, rstd, vn = _a_norm(h_ref[:, 1024:2048], g_ref[...], b_ref[...])
        vnb = vn.astype(MXU_DTYPE)
        for g in range(A_GROUPS):
            c0, c1 = g * LANES, (g + 1) * LANES
            wg = jnp.where(tril, ws_ref[g], 0.0).astype(MXU_DTYPE)
            mixed = _dot(wg, vnb[:, c0:c1]) + bst_ref[:, g:g + 1]
            ag = h_ref[:, 2048 + c0:2048 + c1]
            u = h_ref[:, c0:c1]
            sil, dsil = _silu_and_grad(ag)
            da = dm_ref[:, c0:c1]
            dh_ref[:, c0:c1] = (da * mixed * sil).astype(dh_ref.dtype)
            dh_ref[:, 2048 + c0:2048 + c1] = (da * u * mixed * dsil).astype(dh_ref.dtype)
            dmx = da * u * sil
            dmb = dmx.astype(MXU_DTYPE)
            dvn_scr[:, c0:c1] = _dot_tn(wg, dmb)
            dws_ref[g] += jnp.where(tril, _dot_nt(dmb, vnb[:, c0:c1]), 0.0)
            dbst_ref[:, g:g + 1] += jnp.sum(dmx, axis=1, keepdims=True)
        dvn = dvn_scr[...]
        dg_ref[...] += jnp.sum(dvn * vh, axis=0, keepdims=True)
        db_ref[...] += jnp.sum(dvn, axis=0, keepdims=True)
        dvh = dvn * g_ref[...]
        dv = rstd * (dvh - jnp.mean(dvh, axis=1, keepdims=True) - vh * jnp.mean(dvh * vh, axis=1, keepdims=True))
        dh_ref[:, 1024:2048] = dv.astype(dh_ref.dtype)

        cq, sq = cq_ref[...], sq_ref[...]
        k_rot = _rope(h_ref[:, K0:K0 + LANES], cq, sq)
        kp_rot = _rope(kp_ref[...], ckp_ref[...], skp_ref[...])
        valid = _band_mask(i < nb - 1)
        dk_all = jnp.zeros((2 * T, LANES), F32)
        dv_all = jnp.zeros((2 * T, LANES), F32)
        for hk in range(B_KV_HEADS):
            k2, v2 = _kv_operands(hk, k_rot, kp_rot, h_ref[:, V0:V0 + LANES], vp_ref[...])
            cols = _pair_columns(hk)
            qm = _stacked_heads([_rope(h_ref[:, Q0 + c0:Q0 + c0 + LANES], cq, sq) for c0 in cols])
            p, psink = _softmax_with_sink(qm, k2, valid, sink_ref, GROUP_HEADS * hk)
            pb = p.astype(MXU_DTYPE)
            out = _dot(pb, v2)
            dobs = []
            for pp, c0 in enumerate(cols):
                sil, dsil = _silu_and_grad(h_ref[:, BG0 + c0:BG0 + c0 + LANES])
                dbo = dm_ref[:, A_WIDTH + c0:A_WIDTH + c0 + LANES]
                dh_ref[:, BG0 + c0:BG0 + c0 + LANES] = (dbo * _pair_of(out, pp) * dsil).astype(dh_ref.dtype)
                dobs.append(dbo * sil)
            dobm = _stacked_heads(dobs)
            dp = _dot_nt(dobm, v2)
            rs = jnp.sum(p * dp, axis=1, keepdims=True)
            dsb = (p * (dp - rs) * ATTN_SCALE).astype(MXU_DTYPE)
            to_sink = psink * rs
            for h in range(GROUP_HEADS):
                head = GROUP_HEADS * hk + h
                dsink_ref[:, head:head + 1] += -jnp.sum(to_sink[h * T:(h + 1) * T], axis=0, keepdims=True)
            dq = _dot(dsb, k2)
            for pp, c0 in enumerate(cols):
                dh_ref[:, Q0 + c0:Q0 + c0 + LANES] = _rope_bwd(_pair_of(dq, pp), cq, sq).astype(dh_ref.dtype)
            acc_k = _dot_tn(dsb, qm)
            acc_v = _dot_tn(pb, dobm)
            lo2 = lax.broadcasted_iota(jnp.int32, (2 * T, LANES), 1) < B_HEAD_DIM
            mine = lo2 if hk == 0 else jnp.logical_not(lo2)
            dk_all = dk_all + jnp.where(mine, acc_k + pltpu.roll(acc_k, B_HEAD_DIM, 1), 0.0)
            dv_all = dv_all + jnp.where(mine, acc_v + pltpu.roll(acc_v, B_HEAD_DIM, 1), 0.0)
        dk_rot = dk_all[T:2 * T] + ck_scr[...]
        dvv = dv_all[T:2 * T] + cv_scr[...]
        ck_scr[...] = dk_all[0:T]
        cv_scr[...] = dv_all[0:T]
        dh_ref[:, K0:K0 + LANES] = _rope_bwd(dk_rot, cq, sq).astype(dh_ref.dtype)
        dh_ref[:, V0:V0 + LANES] = dvv.astype(dh_ref.dtype)

    rev = lambda i: nb - 1 - i
    return pl.pallas_call(
        body, name="even_bwd", grid=(nb,),
        in_specs=[sp['h'](rev), sp['kprev'](rev), sp['vprev'](rev), sp['tab'](rev), sp['tab'](rev),
                  sp['tabprev'](rev), sp['tabprev'](rev), pl.BlockSpec((T, 2 * A_WIDTH), lambda i: (rev(i), 0)),
                  _full((1, A_WIDTH)), _full((1, A_WIDTH)), _full((A_GROUPS, T, T)), _full((T, A_GROUPS)),
                  _full((1, B_Q_HEADS))],
        out_specs=[sp['h'](rev), _full((1, A_WIDTH)), _full((1, A_WIDTH)),
                   _full((A_GROUPS, T, T)), _full((T, A_GROUPS)), _full((1, B_Q_HEADS))],
        out_shape=[jax.ShapeDtypeStruct((s, EVEN_IN), MXU_DTYPE), jax.ShapeDtypeStruct((1, A_WIDTH), F32),
                   jax.ShapeDtypeStruct((1, A_WIDTH), F32), jax.ShapeDtypeStruct((A_GROUPS, T, T), F32),
                   jax.ShapeDtypeStruct((T, A_GROUPS), F32), jax.ShapeDtypeStruct((1, B_Q_HEADS), F32)],
        scratch_shapes=[pltpu.VMEM((T, A_WIDTH), F32), pltpu.VMEM((T, LANES), F32), pltpu.VMEM((T, LANES), F32)],
        compiler_params=_params(("arbitrary",)),
    )(h, h, h, cq, sq, cq, sq, dmix, ln_g.reshape(1, -1), ln_b.reshape(1, -1), ws, bs_t, sinks.reshape(1, -1))


def _shift_down(x, d, fill=0.0):
    rows = lax.broadcasted_iota(jnp.int32, x.shape, 0)
    return jnp.where(rows >= d, pltpu.roll(x, d, 0), fill)


def _shift_up(x, d, fill=0.0):
    n = x.shape[0]
    rows = lax.broadcasted_iota(jnp.int32, x.shape, 0)
    return jnp.where(rows < n - d, pltpu.roll(x, n - d, 0), fill)


SCAN_BLOCK = 32
SUBLANES = 8


def _scan_block(a, b, up):
    n = a.shape[0]
    d = 1
    while d < n:
        if d < SUBLANES:
            shift = _shift_up if up else _shift_down
            a, b = a * shift(a, d, 1.0), a * shift(b, d) + b
        elif up:
            a, b = (jnp.concatenate([a[:n - d] * a[d:], a[n - d:]], axis=0),
                    jnp.concatenate([a[:n - d] * b[d:] + b[:n - d], b[n - d:]], axis=0))
        else:
            a, b = (jnp.concatenate([a[:d], a[d:] * a[:n - d]], axis=0),
                    jnp.concatenate([b[:d], a[d:] * b[:n - d] + b[d:]], axis=0))
        d *= 2
    return a, b


def _scan(a, b, up):
    n = a.shape[0]
    size = min(SCAN_BLOCK, n)
    blocks = n // size
    out, carry = [None] * blocks, None
    for k in (reversed(range(blocks)) if up else range(blocks)):
        prod, h = _scan_block(a[k * size:(k + 1) * size], b[k * size:(k + 1) * size], up)
        if carry is not None:
            h = h + prod * carry
        carry = h[0:1] if up else h[size - 1:size]
        out[k] = h
    return jnp.concatenate(out, axis=0)


def _neg_expm1(y):
    t = jnp.tanh(-0.5 * y)
    return 2.0 * t / (1.0 + t)


def _softplus(x):
    return jnp.maximum(x, 0.0) + jnp.log1p(jnp.exp(-jnp.abs(x)))


def _lru_forward(xc, cw, cb, wa, ba, wx, bx, lam):
    shifted = [_shift_down(xc, CONV_WIDTH - 1 - j) if j < CONV_WIDTH - 1 else xc for j in range(CONV_WIDTH)]
    xconv = cb
    for j in range(CONV_WIDTH):
        xconv = xconv + cw[j:j + 1, :] * shifted[j]
    xb = xconv.astype(MXU_DTYPE)
    r = _sigmoid(_dot(xb, wa.astype(MXU_DTYPE)) + ba)
    ig = _sigmoid(_dot(xb, wx.astype(MXU_DTYPE)) + bx)
    sp = _softplus(-lam)
    log_a = -LRU_C * r * sp
    a = jnp.exp(log_a)
    mult = jnp.sqrt(_neg_expm1(2.0 * log_a))
    hs = _scan(a, mult * ig * xconv, up=False)
    return dict(shifted=shifted, xconv=xconv, xb=xb, r=r, ig=ig, sp=sp, a=a, mult=mult, hs=hs)


def _col(width, off):
    return lambda s: pl.BlockSpec((s, width), lambda i: (0, off + i))


def _part(width, slot):
    return lambda s: pl.BlockSpec((None, s, width), lambda i: (slot, 0, i))


def _lru_specs(s):
    w = LANES
    return [_part(w, 0)(s), _part(w, 1)(s), pl.BlockSpec((CONV_WIDTH, w), lambda i: (0, i)),
            pl.BlockSpec((1, w), lambda i: (0, i)), pl.BlockSpec((None, w, w), lambda i: (i, 0, 0)),
            pl.BlockSpec((1, w), lambda i: (0, i)), pl.BlockSpec((None, w, w), lambda i: (i, 0, 0)),
            pl.BlockSpec((1, w), lambda i: (0, i)), pl.BlockSpec((1, w), lambda i: (0, i))]


def _lru_fwd(h, cw, cb, wa, ba, wx, bx, lam):
    s = h.shape[1]

    def body(xc_ref, cg_ref, cw_ref, cb_ref, wa_ref, ba_ref, wx_ref, bx_ref, lam_ref, o_ref):
        f = _lru_forward(xc_ref[...], cw_ref[...], cb_ref[...], wa_ref[...], ba_ref[...], wx_ref[...], bx_ref[...],
                         lam_ref[...])
        sil, _ = _silu_and_grad(cg_ref[...])
        o_ref[...] = (f['hs'] * sil).astype(o_ref.dtype)

    return pl.pallas_call(
        body, name="lru_fwd", grid=(C_HEADS,), in_specs=_lru_specs(s), out_specs=_col(LANES, 0)(s),
        out_shape=jax.ShapeDtypeStruct((s, C_WIDTH + D_WIDTH), MXU_DTYPE), compiler_params=_params(("parallel",)),
    )(h, h, cw, cb.reshape(1, -1), wa, ba.reshape(1, -1), wx, bx.reshape(1, -1), lam.reshape(1, -1))


def _lru_bwd(h, dmix, cw, cb, wa, ba, wx, bx, lam):
    s = h.shape[1]

    def body(xc_ref, cg_ref, cw_ref, cb_ref, wa_ref, ba_ref, wx_ref, bx_ref, lam_ref, dco_ref,
             dh_ref, dcw_ref, dcb_ref, dwa_ref, dba_ref, dwx_ref, dbx_ref, dlam_ref):
        dxc_ref, dcg_ref = dh_ref.at[0], dh_ref.at[1]
        cw, lam = cw_ref[...], lam_ref[...]
        wab, wxb = wa_ref[...].astype(MXU_DTYPE), wx_ref[...].astype(MXU_DTYPE)
        f = _lru_forward(xc_ref[...], cw, cb_ref[...], wa_ref[...], ba_ref[...], wx_ref[...], bx_ref[...], lam)
        sil, dsil = _silu_and_grad(cg_ref[...])
        dco = dco_ref[...]
        hs, a, mult, ig, r, xconv = f['hs'], f['a'], f['mult'], f['ig'], f['r'], f['xconv']
        dcg_ref[...] = (dco * hs * dsil).astype(dcg_ref.dtype)
        lamb = _scan(_shift_up(a, 1), dco * sil, up=True)
        da = lamb * _shift_down(hs, 1)
        d_ig = lamb * mult * xconv
        d_mult = lamb * ig * xconv
        dxconv = lamb * mult * ig
        d_log_a = da * a - d_mult * (a * a) / mult
        dlam_ref[...] = jnp.sum(d_log_a * r, axis=0, keepdims=True) * LRU_C * _sigmoid(-lam)
        dpa = d_log_a * (-LRU_C * f['sp']) * r * (1.0 - r)
        dpx = d_ig * ig * (1.0 - ig)
        dba_ref[...] = jnp.sum(dpa, axis=0, keepdims=True)
        dbx_ref[...] = jnp.sum(dpx, axis=0, keepdims=True)
        dpab, dpxb = dpa.astype(MXU_DTYPE), dpx.astype(MXU_DTYPE)
        dwa_ref[...] = _dot_tn(f['xb'], dpab)
        dwx_ref[...] = _dot_tn(f['xb'], dpxb)
        dxconv = dxconv + _dot_nt(dpab, wab) + _dot_nt(dpxb, wxb)
        dcb_ref[...] = jnp.sum(dxconv, axis=0, keepdims=True)
        dxc = jnp.zeros_like(dxconv)
        for j in range(CONV_WIDTH):
            dcw_ref[j:j + 1, :] = jnp.sum(dxconv * f['shifted'][j], axis=0, keepdims=True)
            back = _shift_up(dxconv, CONV_WIDTH - 1 - j) if j < CONV_WIDTH - 1 else dxconv
            dxc = dxc + cw[j:j + 1, :] * back
        dxc_ref[...] = dxc.astype(dxc_ref.dtype)

    w = LANES
    vec = pl.BlockSpec((1, w), lambda i: (0, i))
    mat = pl.BlockSpec((None, w, w), lambda i: (i, 0, 0))
    vshape = jax.ShapeDtypeStruct((1, C_WIDTH), F32)
    mshape = jax.ShapeDtypeStruct((C_HEADS, w, w), F32)
    return pl.pallas_call(
        body, name="lru_bwd", grid=(C_HEADS,), in_specs=_lru_specs(s) + [_col(w, 0)(s)],
        out_specs=[pl.BlockSpec((2, s, w), lambda i: (0, 0, i)), pl.BlockSpec((CONV_WIDTH, w), lambda i: (0, i)),
                   vec, mat, vec, mat, vec, vec],
        out_shape=[jax.ShapeDtypeStruct((N_CHIPS, s, C_WIDTH), MXU_DTYPE),
                   jax.ShapeDtypeStruct((CONV_WIDTH, C_WIDTH), F32), vshape, mshape, vshape, mshape, vshape, vshape],
        compiler_params=_params(("parallel",)),
    )(h, h, cw, cb.reshape(1, -1), wa, ba.reshape(1, -1), wx, bx.reshape(1, -1), lam.reshape(1, -1), dmix)


def _pool_window(g):
    return jnp.left_shift(2, g).astype(F32)


def _select_window(g, sums):
    out = sums[-1]
    for k in range(len(sums) - 2, -1, -1):
        out = jnp.where(g == k, sums[k], out)
    return out


def _window_sums(x, shift):
    sums, cur = [], x
    for k in range(D_GROUPS):
        cur = cur + shift(cur, 2 ** k)
        sums.append(cur)
    return sums


def _pool_specs(s):
    w = D_GROUP_DIM
    return [_part(w, 2)(s), _part(w, 3)(s),
            pl.BlockSpec((None, w, w), lambda i: (i, 0, 0)), pl.BlockSpec((1, w), lambda i: (0, i))]


def _pool_forward(xd, g):
    rows = lax.broadcasted_iota(jnp.int32, xd.shape, 0).astype(F32) + 1.0
    cnt = jnp.minimum(rows, _pool_window(g))
    pooled = _select_window(g, _window_sums(xd, _shift_down)) / cnt - xd
    return pooled, cnt


def _pool_fwd(h, mix, w_pool, d_scale):
    s = h.shape[1]

    def body(xd_ref, dg_ref, wp_ref, sc_ref, mix_ref, o_ref):
        pooled, _ = _pool_forward(xd_ref[...], pl.program_id(0))
        mixed = _dot(pooled.astype(MXU_DTYPE), wp_ref[...].astype(MXU_DTYPE))
        sil, _ = _silu_and_grad(dg_ref[...])
        o_ref[...] = (mixed * sc_ref[...] * sil).astype(o_ref.dtype)

    return pl.pallas_call(
        body, name="pool_fwd", grid=(D_GROUPS,), in_specs=_pool_specs(s) + [ANY],
        out_specs=_col(D_GROUP_DIM, C_WIDTH // D_GROUP_DIM)(s), out_shape=jax.ShapeDtypeStruct(mix.shape, mix.dtype),
        input_output_aliases={4: 0}, compiler_params=_params(("parallel",)),
    )(h, h, w_pool, d_scale.reshape(1, -1), mix)


def _pool_bwd(h, dmix, dh, w_pool, d_scale):
    s = h.shape[1]
    w = D_GROUP_DIM

    def body(xd_ref, dg_ref, wp_ref, sc_ref, ddo_ref, dh_in_ref, dh_ref, dwp_ref, dsc_ref):
        dxd_ref, ddg_ref = dh_ref.at[0], dh_ref.at[1]
        g = pl.program_id(0)
        pooled, cnt = _pool_forward(xd_ref[...], g)
        pb = pooled.astype(MXU_DTYPE)
        wpb = wp_ref[...].astype(MXU_DTYPE)
        mixed = _dot(pb, wpb)
        sil, dsil = _silu_and_grad(dg_ref[...])
        ddo = ddo_ref[...]
        sc = sc_ref[...]
        ddg_ref[...] = (ddo * mixed * sc * dsil).astype(ddg_ref.dtype)
        dms = ddo * sil
        dsc_ref[...] = jnp.sum(dms * mixed, axis=0, keepdims=True)
        dmb = (dms * sc).astype(MXU_DTYPE)
        dwp_ref[...] = _dot_tn(pb, dmb)
        dpooled = _dot_nt(dmb, wpb)
        dxd = _select_window(g, _window_sums(dpooled / cnt, _shift_up)) - dpooled
        dxd_ref[...] = dxd.astype(dxd_ref.dtype)

    return pl.pallas_call(
        body, name="pool_bwd", grid=(D_GROUPS,), in_specs=_pool_specs(s) + [_col(w, C_WIDTH // w)(s), ANY],
        out_specs=[pl.BlockSpec((2, s, w), lambda i: (1, 0, i)), pl.BlockSpec((None, w, w), lambda i: (i, 0, 0)),
                   pl.BlockSpec((1, w), lambda i: (0, i))],
        out_shape=[jax.ShapeDtypeStruct(dh.shape, dh.dtype),
                   jax.ShapeDtypeStruct((D_GROUPS, w, w), F32), jax.ShapeDtypeStruct((1, D_WIDTH), F32)],
        input_output_aliases={5: 0}, compiler_params=_params(("parallel",)),
    )(h, h, w_pool, d_scale.reshape(1, -1), dmix, dh)


def _rope_tables(positions):
    s = positions.shape[0]
    inv_freq = ROPE_THETA ** (-jnp.arange(0, ROT_DIM, 2, dtype=F32) / ROT_DIM)
    ang = positions.astype(F32)[:, None] * inv_freq
    c, sn = jnp.cos(ang), jnp.sin(ang)
    rest = B_HEAD_DIM - ROT_DIM
    c64 = jnp.concatenate([c, c, jnp.ones((s, rest), F32)], axis=1)
    s64 = jnp.concatenate([-sn, sn, jnp.zeros((s, rest), F32)], axis=1)
    return jnp.tile(c64, (1, LANES // B_HEAD_DIM)), jnp.tile(s64, (1, LANES // B_HEAD_DIM))


def _local_step(x, positions, target, w, projections_of, projections_ahead, projection_grads, layer_grads_begin):
    cq, sq = _rope_tables(positions)
    saved = []
    xin, xin_b = x, x.astype(MXU_DTYPE)
    for layer in range(DEPTH):
        j = layer // 2
        kind = 'even' if layer % 2 == 0 else 'odd'
        w_in, w_out = projections_of(layer, xin_b)
        h = _mm_in(xin_b, w_in, W_IN_TRANSPOSED[kind], "mm_" + kind + "_in")
        if kind == 'even':
            behind, w_out_after = w_out(h) if callable(w_out) else (h, None)
            mix = _even_fwd(h, cq, sq, w['even_a_ln_g'][j], w['even_a_ln_b'][j], w['even_a_ws'][j],
                            w['even_a_bs'][j].T, w['even_b_sinks'][j], behind)
            w_out = w_out_after(mix) if w_out_after else w_out
        else:
            mix = _lru_fwd(h, w['odd_conv_w'][j], w['odd_conv_b'][j], w['odd_w_a'][j], w['odd_b_a'][j],
                           w['odd_w_x'][j], w['odd_b_x'][j], w['odd_lam'][j])
            mix = _pool_fwd(h, mix, w['odd_w_pool'][j], w['odd_d_scale'][j])
        if layer == DEPTH - 1:
            z, out, out_b = _mm_out_norm(mix, w_out, xin, w[kind + '_ln_g'][j], w[kind + '_ln_b'][j],
                                         "mm_out_norm_loss", target=target)
        else:
            z, out, out_b = _mm_out_norm(mix, w_out, xin, w[kind + '_ln_g'][j], w[kind + '_ln_b'][j], "mm_out_norm",
                                         after=projections_ahead(layer + 1, mix))
        saved.append((xin_b, h, mix, z, w_in, w_out))
        xin, xin_b = out, out_b

    dout, loss = xin, xin_b
    per_layer = {n: [None, None] for n in SMALL_NAMES}
    for layer in reversed(range(DEPTH)):
        j = layer // 2
        kind = 'even' if layer % 2 == 0 else 'odd'
        xin_b, h, mix, z, w_in, w_out = saved[layer]
        dz, dg, db, dmix, dw_out = _norm_out_bwd(dout, z, w[kind + '_ln_g'][j], w_out, mix,
                                                 layer_grads_begin(layer, dout), "norm_out_bwd")
        per_layer[kind + '_ln_g'][j], per_layer[kind + '_ln_b'][j] = dg[0], db[0]
        if kind == 'even':
            dh, dlg, dlb, dws, dbst, dsink = _even_bwd(
                h, dmix, cq, sq, w['even_a_ln_g'][j], w['even_a_ln_b'][j], w['even_a_ws'][j], w['even_a_bs'][j].T,
                w['even_b_sinks'][j])
            for n, val in (('even_a_ln_g', dlg[0]), ('even_a_ln_b', dlb[0]), ('even_a_ws', dws),
                           ('even_a_bs', dbst.T), ('even_b_sinks', dsink[0])):
                per_layer[n][j] = val
        else:
            dh, dcw, dcb, dwa, dba, dwx, dbx, dlam = _lru_bwd(
                h, dmix, w['odd_conv_w'][j], w['odd_conv_b'][j], w['odd_w_a'][j], w['odd_b_a'][j], w['odd_w_x'][j],
                w['odd_b_x'][j], w['odd_lam'][j])
            dh, dwp, dsc = _pool_bwd(h, dmix, dh, w['odd_w_pool'][j], w['odd_d_scale'][j])
            for n, val in (('odd_conv_w', dcw), ('odd_conv_b', dcb[0]), ('odd_w_a', dwa), ('odd_b_a', dba[0]),
                           ('odd_w_x', dwx), ('odd_b_x', dbx[0]), ('odd_lam', dlam[0]), ('odd_w_pool', dwp),
                           ('odd_d_scale', dsc[0])):
                per_layer[n][j] = val
        dw_in = _mm_dw_in(xin_b, dh, W_IN_TRANSPOSED[kind], "mm_dw_in_" + kind)
        dout = _mm_dx(dh, w_in, dz, W_IN_TRANSPOSED[kind], projection_grads(layer, dw_in, dw_out), "mm_dx_" + kind)
    return loss, dout, {n: jnp.stack(v) for n, v in per_layer.items()}


ANY = pl.BlockSpec(memory_space=pl.ANY)


def _where_am_i():
    return lax.axis_index("x"), lax.axis_index("y"), lax.axis_index("c")


def _other_chips(x, y):
    return [(1 - x, y), (x, 1 - y), (1 - x, 1 - y)]


PACKED_ROWS = 16


def _half_rows(rows, which):
    half = rows // 2
    assert half % PACKED_ROWS == 0, rows
    return pl.ds(pl.multiple_of(which * half, PACKED_ROWS), half)


def _sibling_swap(bufs, name, after=()):
    n, na = len(bufs), len(after)

    def body(*refs):
        x_refs, out_refs, (send_sems, recv_sems) = refs[:n], refs[n + na:2 * n + na], refs[2 * n + na:]
        x, y, c = _where_am_i()
        copies = []
        for p in range(n):
            copies.append(pltpu.make_async_remote_copy(
                src_ref=x_refs[p], dst_ref=out_refs[p], send_sem=send_sems.at[p], recv_sem=recv_sems.at[p],
                device_id=(x, y, 1 - c), device_id_type=MESH))
            copies[-1].start()
        for cp in copies:
            cp.wait()

    return pl.pallas_call(
        body, name=name, in_specs=[ANY] * (n + na), out_specs=[ANY] * n,
        out_shape=[jax.ShapeDtypeStruct(b.shape, b.dtype) for b in bufs],
        scratch_shapes=[pltpu.SemaphoreType.DMA((n,)), pltpu.SemaphoreType.DMA((n,))],
    )(*bufs, *after)


HBM = pl.BlockSpec(memory_space=pltpu.HBM)
SEM = pl.BlockSpec(memory_space=pltpu.SEMAPHORE)
IN_FLIGHT = pltpu.CompilerParams(has_side_effects=pltpu.SideEffectType.DATAFLOW_SIDE_EFFECTING)


IN_PLACE = ('forward',)


def _peers(kind, x, y, c):
    if kind in ('swap', 'join'):
        return [(x, y, 1 - c)]
    return [(x, y, 1 - c) if kind == 'forward' else (px, py, c) for px, py in _other_chips(x, y)]


def _n_peers(kind):
    return 1 if kind in ('swap', 'join') else N_CHIPS - 1


def _land_shape(kind, a):
    if kind == 'gather':
        return (N_CHIPS,) + a.shape
    return (a.shape[0], a.shape[1] // 2, a.shape[2]) if kind == 'swap' else a.shape


def _ici_copy(kind, src_ref, land_ref, send_sems, recv_sems, k, peer, x, y, c, arriving=False):
    me = 2 * x + y
    if kind == 'gather':
        half = _half_rows(src_ref.shape[0], c)
        src, dst = src_ref.at[half, :], land_ref.at[me, half, :]
    elif kind == 'scatter':
        src, dst = src_ref.at[2 * peer[0] + peer[1]], land_ref.at[me]
    elif kind == 'swap':
        src, dst = src_ref.at[:, _half_rows(src_ref.shape[1], 1 - c), :], land_ref
    elif kind == 'join':
        src, dst = src_ref, land_ref
    else:
        px, py = _other_chips(x, y)[k % (N_CHIPS - 1)]
        rows = src_ref.at[2 * px + py, _half_rows(src_ref.shape[1], 1 - c if arriving else c), :]
        src, dst = rows, rows
    return pltpu.make_async_remote_copy(src_ref=src, dst_ref=dst, send_sem=send_sems.at[k], recv_sem=recv_sems.at[k],
                                        device_id=peer, device_id_type=MESH)


def _ici_start(kind, groups, name, after=()):
    flat = [a for g in groups for a in g]
    n, ng, np_, na = len(flat), len(groups), _n_peers(kind), len(after)
    nl = 0 if kind in IN_PLACE else n
    land_shapes = [_land_shape(kind, a) for a in flat[:nl]]

    def body(*refs):
        srcs, sems, token = refs[:n], refs[n + nl + na:n + nl + na + 2 * ng], refs[-1]
        lands = refs[n:n + nl] if nl else srcs
        x, y, c = _where_am_i()
        p = 0
        for gi, g in enumerate(groups):
            for q in range(len(g)):
                for k, peer in enumerate(_peers(kind, x, y, c)):
                    _ici_copy(kind, srcs[p], lands[p], sems[2 * gi], sems[2 * gi + 1], np_ * q + k, peer, x, y,
                              c).start()
                p += 1
        token[...] = jnp.zeros_like(token)

    sem_shapes = [pltpu.SemaphoreType.DMA((np_ * len(g),)) for g in groups for _ in range(2)]
    outs = pl.pallas_call(
        body, name=name, in_specs=[HBM] * (n + nl) + [ANY] * na,
        out_specs=[SEM] * (2 * ng) + [HBM] * (n + nl) + [pl.BlockSpec(memory_space=pltpu.VMEM)],
        out_shape=sem_shapes + [pltpu.HBM(a.shape, a.dtype) for a in flat]
        + [pltpu.HBM(s, a.dtype) for s, a in zip(land_shapes, flat)] + [jax.ShapeDtypeStruct((8, LANES), F32)],
        input_output_aliases={i: 2 * ng + i for i in range(n + nl)}, compiler_params=IN_FLIGHT,
    )(*[pltpu.with_memory_space_constraint(a, pltpu.HBM) for a in flat],
      *[pltpu.with_memory_space_constraint(lax.empty(s, a.dtype), pltpu.HBM) for s, a in zip(land_shapes, flat)],
      *after)
    sems, srcs, token = outs[:2 * ng], outs[2 * ng:2 * ng + n], outs[-1]
    lands = outs[2 * ng + n:2 * ng + n + nl] if nl else srcs
    started, p = [], 0
    for gi, g in enumerate(groups):
        started.append((sems[2 * gi], sems[2 * gi + 1], srcs[p:p + len(g)], lands[p:p + len(g)]))
        p += len(g)
    return started, token


def _ici_wait(kind, started, after, name):
    send_sems, recv_sems, srcs, lands = started
    n, np_ = len(srcs), _n_peers(kind)
    nl = 0 if kind in IN_PLACE else n
    after = list(after)

    def body(*refs):
        src_refs, send_ref, recv_ref = refs[:n], refs[n + nl], refs[n + nl + 1]
        land_refs = refs[n:n + nl] if nl else src_refs
        x, y, c = _where_am_i()
        for q in range(n):
            for k, peer in enumerate(_peers(kind, x, y, c)):
                args = (kind, src_refs[q], land_refs[q], send_ref, recv_ref, np_ * q + k, peer, x, y, c)
                _ici_copy(*args).wait_send()
                _ici_copy(*args, arriving=True).wait_recv()

    arrays = list(srcs) + (list(lands) if nl else [])
    outs = pl.pallas_call(
        body, name=name, in_specs=[HBM] * (n + nl) + [SEM, SEM] + [ANY] * len(after), out_specs=[HBM] * (n + nl),
        out_shape=[pltpu.HBM(a.shape, a.dtype) for a in arrays],
        input_output_aliases={i: i for i in range(n + nl)}, compiler_params=IN_FLIGHT,
    )(*arrays, send_sems, recv_sems, *after)
    return outs[:n], (outs[n:] if nl else outs[:n])


def _forward_to_sibling(lands, name):
    n = len(lands)

    def body(*refs):
        out_refs, (send_sems, recv_sems) = refs[n:2 * n], refs[2 * n:]
        x, y, c = _where_am_i()

        def copy(q, k, peer, half_of):
            rows = out_refs[q].at[2 * peer[0] + peer[1], _half_rows(lands[q].shape[1], half_of), :]
            return pltpu.make_async_remote_copy(
                src_ref=rows, dst_ref=rows, send_sem=send_sems.at[3 * q + k], recv_sem=recv_sems.at[3 * q + k],
                device_id=(x, y, 1 - c), device_id_type=MESH)

        sends = [copy(q, k, peer, c) for q in range(n) for k, peer in enumerate(_other_chips(x, y))]
        for cp in sends:
            cp.start()
        for q in range(n):
            for k, peer in enumerate(_other_chips(x, y)):
                copy(q, k, peer, 1 - c).wait_recv()
        for cp in sends:
            cp.wait_send()

    return pl.pallas_call(
        body, name=name, in_specs=[ANY] * n, out_specs=[ANY] * n,
        out_shape=[jax.ShapeDtypeStruct(a.shape, a.dtype) for a in lands],
        input_output_aliases={i: i for i in range(n)},
        scratch_shapes=[pltpu.SemaphoreType.DMA((3 * n,)), pltpu.SemaphoreType.DMA((3 * n,))],
    )(*lands)


def _with_own_slot(lands, own, chip):
    return [lax.dynamic_update_slice(land, mine[None], (chip, 0, 0)) for land, mine in zip(lands, own)]


def _row_tile(rows, row_bytes, n_bufs, budget=24 * 1024 * 1024):
    best = None
    for t in range(PACKED_ROWS, rows + 1, PACKED_ROWS):
        if rows % t == 0 and 2 * n_bufs * t * row_bytes <= budget:
            best = t
    assert best is not None, (rows, row_bytes)
    return best


def _add_halves(full, recv, core, out_dtype, name):
    n, h, cols = recv.shape
    tr = _row_tile(h, cols * 4, 3)
    nt = h // tr

    def body(core_ref, a_ref, b_ref, o_ref):
        o_ref[...] = (a_ref[...].astype(F32) + b_ref[...].astype(F32)).astype(out_dtype)

    blk = (None, tr, cols)
    return pl.pallas_call(
        body, name=name,
        grid_spec=pltpu.PrefetchScalarGridSpec(
            num_scalar_prefetch=1, grid=(n, nt),
            in_specs=[pl.BlockSpec(blk, lambda j, i, core_ref: (j, core_ref[0] * nt + i, 0)),
                      pl.BlockSpec(blk, lambda j, i, core_ref: (j, i, 0))],
            out_specs=pl.BlockSpec(blk, lambda j, i, core_ref: (j, i, 0))),
        out_shape=jax.ShapeDtypeStruct((n, h, cols), out_dtype),
        compiler_params=_params(("parallel", "parallel")),
    )(core.reshape(1), full, recv)


def _add2(a, b, name):
    r, cols = a.shape
    tr = _row_tile(r, cols * 4, 3)

    def body(a_ref, b_ref, o_ref):
        o_ref[...] = a_ref[...] + b_ref[...]

    blk = pl.BlockSpec((tr, cols), lambda i: (i, 0))
    return pl.pallas_call(body, name=name, grid=(r // tr,), in_specs=[blk, blk], out_specs=blk,
                          out_shape=jax.ShapeDtypeStruct(a.shape, a.dtype), compiler_params=_params(("parallel",)))(a, b)


def _sum_slots(parts, own, chip, name):
    n, h, cols = parts.shape
    tr = _row_tile(h, cols * 4, n + 2)

    def body(chip_ref, p_ref, own_ref, o_ref):
        acc = None
        for k in range(n):
            term = jnp.where(chip_ref[0] == k, own_ref[...], p_ref[k]).astype(F32)
            acc = term if acc is None else acc + term
        o_ref[...] = acc

    own_spec = (pl.BlockSpec((None, tr, cols), lambda i, chip_ref: (chip_ref[0], i, 0)) if own.ndim == 3
                else pl.BlockSpec((tr, cols), lambda i, chip_ref: (i, 0)))
    return pl.pallas_call(
        body, name=name,
        grid_spec=pltpu.PrefetchScalarGridSpec(
            num_scalar_prefetch=1, grid=(h // tr,),
            in_specs=[pl.BlockSpec((n, tr, cols), lambda i, chip_ref: (0, i, 0)), own_spec],
            out_specs=pl.BlockSpec((tr, cols), lambda i, chip_ref: (i, 0))),
        out_shape=jax.ShapeDtypeStruct((h, cols), F32), compiler_params=_params(("parallel",)),
    )(chip.reshape(1), parts, own)


def _adamw(w, g, m, v, name):
    shape = w.shape
    cols = shape[-1]
    rows = math.prod(shape[:-1])
    tr = rows
    if rows % 8 == 0:
        tr = 8
        for t in range(8, rows + 1, 8):
            if rows % t == 0 and t * cols * 4 <= 1536 * 1024:
                tr = t
    c1 = 1.0 - ADAM_B1 ** ADAM_STEP
    c2 = 1.0 - ADAM_B2 ** ADAM_STEP

    def body(w_ref, g_ref, m_ref, v_ref, d_ref, nm_ref, nv_ref):
        gg = g_ref[...]
        nm = ADAM_B1 * m_ref[...] + (1.0 - ADAM_B1) * gg
        nv = ADAM_B2 * v_ref[...] + (1.0 - ADAM_B2) * (gg * gg)
        d_ref[...] = -ADAM_LR * ((nm / c1) / (jnp.sqrt(nv / c2) + ADAM_EPS) + ADAM_WD * w_ref[...])
        nm_ref[...] = nm
        nv_ref[...] = nv

    blk = pl.BlockSpec((tr, cols), lambda i: (i, 0))
    flat = jax.ShapeDtypeStruct((rows, cols), F32)
    outs = pl.pallas_call(
        body, name=name, grid=(rows // tr,), in_specs=[blk] * 4, out_specs=[blk] * 3, out_shape=[flat] * 3,
        compiler_params=_params(("parallel",)),
    )(*[a.reshape(rows, cols) for a in (w, g, m, v)])
    return [o.reshape(shape) for o in outs]


def _adamw_update(w, g, m, v):
    c1 = 1.0 - ADAM_B1 ** ADAM_STEP
    c2 = 1.0 - ADAM_B2 ** ADAM_STEP
    nm = ADAM_B1 * m + (1.0 - ADAM_B1) * g
    nv = ADAM_B2 * v + (1.0 - ADAM_B2) * (g * g)
    return -ADAM_LR * ((nm / c1) / (jnp.sqrt(nv / c2) + ADAM_EPS) + ADAM_WD * w), nm, nv


def _adamw_projection(w, m, v, mine, theirs, core, name):
    _, r, cols = w.shape
    h = r // 2
    tr = _row_tile(h, cols * 4, 12, budget=40 * 1024 * 1024)
    nt = h // tr

    def body(core_ref, w_ref, m_ref, v_ref, a0, b0, a1, b1, g_ref, d_ref, nm_ref, nv_ref):
        layer, half = pl.program_id(0), pl.program_id(1)
        own = half == core_ref[0]
        g = jnp.where(layer == 0, jnp.where(own, a0[...], b0[...]), jnp.where(own, a1[...], b1[...]))
        g_ref[...] = g
        d_ref[...], nm_ref[...], nv_ref[...] = _adamw_update(w_ref[...], g, m_ref[...], v_ref[...])

    def piece(layer, own):
        def index(l, hh, i, core_ref):
            used = (l == layer) & ((hh == core_ref[0]) == own)
            return (jnp.where(used, i, 0), 0)
        return pl.BlockSpec((tr, cols), index)

    whole = pl.BlockSpec((None, tr, cols), lambda l, hh, i, core_ref: (l, hh * nt + i, 0))
    return pl.pallas_call(
        body, name=name,
        grid_spec=pltpu.PrefetchScalarGridSpec(
            num_scalar_prefetch=1, grid=(2, 2, nt),
            in_specs=[whole] * 3 + [piece(0, True), piece(0, False), piece(1, True), piece(1, False)],
            out_specs=[whole] * 4),
        out_shape=[jax.ShapeDtypeStruct(w.shape, F32)] * 4,
        compiler_params=_params(("arbitrary", "arbitrary", "arbitrary")),
    )(core.reshape(1), w, m, v, mine[0], theirs[0], mine[1], theirs[1])


def _chip_shape(name):
    shape = list(FULL_SHAPES[name])
    axis = BIG_SHARD_AXIS.get(name, SMALL_SHARD_AXIS.get(name))
    if axis is not None:
        shape[axis] //= N_CHIPS
    return tuple(shape)


def _round_up(n, k):
    return -(-n // k) * k


SHARDED_SMALL_WORDS = sum(math.prod(_chip_shape(n)) for n in SMALL_SHARDED)
SHARDED_SMALL_ROWS = _round_up(-(-SHARDED_SMALL_WORDS // PACK_COLS), 32)
SMALL_MATRICES = ['even_a_ws', 'odd_w_a', 'odd_w_x', 'odd_w_pool']
SMALL_VECTORS = [n for n in SMALL_NAMES if n not in SMALL_MATRICES]
VECTOR_ROWS = _round_up(-(-(sum(math.prod(FULL_SHAPES[n]) for n in SMALL_VECTORS) + 1) // PACK_COLS), 32)


def _pack_rows(flat, rows):
    return jnp.pad(flat, (0, rows * PACK_COLS - flat.shape[0])).reshape(rows, PACK_COLS)


def _pack_small_sharded(shards):
    return _pack_rows(jnp.concatenate([shards[n].reshape(-1) for n in SMALL_SHARDED]), SHARDED_SMALL_ROWS)


def _unpack_small_sharded(gathered_small):
    out = {}
    words = gathered_small.reshape(N_CHIPS, -1)
    w0 = 0
    for n in SMALL_SHARDED:
        size = math.prod(_chip_shape(n))
        blocks = words[:, w0:w0 + size].reshape((N_CHIPS,) + _chip_shape(n))
        out[n] = jnp.concatenate([blocks[j] for j in range(N_CHIPS)], axis=SMALL_SHARD_AXIS[n])
        w0 += size
    return out


def _small_grad_parts(grads, loss):
    words = [grads[n].reshape(-1) for n in SMALL_VECTORS] + [loss.reshape(-1)]
    return ([_pack_rows(jnp.concatenate(words), VECTOR_ROWS)]
            + [grads[n].reshape(-1, FULL_SHAPES[n][-1]) for n in SMALL_MATRICES])


def _own_block(name, full, chip):
    if name not in SMALL_SHARD_AXIS:
        return full
    axis = SMALL_SHARD_AXIS[name]
    width = FULL_SHAPES[name][axis] // N_CHIPS
    return lax.dynamic_slice_in_dim(full, chip * width, width, axis)


def _from_small_grad_parts(totals, chip):
    flat = totals[0].reshape(-1)
    out = {}
    w0 = 0
    for n in SMALL_VECTORS:
        size = math.prod(FULL_SHAPES[n])
        out[n] = _own_block(n, flat[w0:w0 + size].reshape(FULL_SHAPES[n]), chip)
        w0 += size
    for n, total in zip(SMALL_MATRICES, totals[1:]):
        out[n] = _own_block(n, total.reshape(FULL_SHAPES[n]), chip)
    return out, flat[w0]


def kernel(x, positions, even_w_in, even_a_ln_g, even_a_ln_b, even_a_ws, even_a_bs, even_b_sinks, even_w_out, even_ln_g, even_ln_b, odd_w_in, odd_conv_w, odd_conv_b, odd_w_a, odd_b_a, odd_w_x, odd_b_x, odd_lam, odd_w_pool, odd_d_scale, odd_w_out, odd_ln_g, odd_ln_b, loss_target, m_even_w_in, m_even_a_ln_g, m_even_a_ln_b, m_even_a_ws, m_even_a_bs, m_even_b_sinks, m_even_w_out, m_even_ln_g, m_even_ln_b, m_odd_w_in, m_odd_conv_w, m_odd_conv_b, m_odd_w_a, m_odd_b_a, m_odd_w_x, m_odd_b_x, m_odd_lam, m_odd_w_pool, m_odd_d_scale, m_odd_w_out, m_odd_ln_g, m_odd_ln_b, v_even_w_in, v_even_a_ln_g, v_even_a_ln_b, v_even_a_ws, v_even_a_bs, v_even_b_sinks, v_even_w_out, v_even_ln_g, v_even_ln_b, v_odd_w_in, v_odd_conv_w, v_odd_conv_b, v_odd_w_a, v_odd_b_a, v_odd_w_x, v_odd_b_x, v_odd_lam, v_odd_w_pool, v_odd_d_scale, v_odd_w_out, v_odd_ln_g, v_odd_ln_b):
    local = dict(zip(WEIGHT_NAMES, (even_w_in, even_a_ln_g, even_a_ln_b, even_a_ws, even_a_bs, even_b_sinks, even_w_out, even_ln_g, even_ln_b, odd_w_in, odd_conv_w, odd_conv_b, odd_w_a, odd_b_a, odd_w_x, odd_b_x, odd_lam, odd_w_pool, odd_d_scale, odd_w_out, odd_ln_g, odd_ln_b)))
    mom = dict(zip(WEIGHT_NAMES, (m_even_w_in, m_even_a_ln_g, m_even_a_ln_b, m_even_a_ws, m_even_a_bs, m_even_b_sinks, m_even_w_out, m_even_ln_g, m_even_ln_b, m_odd_w_in, m_odd_conv_w, m_odd_conv_b, m_odd_w_a, m_odd_b_a, m_odd_w_x, m_odd_b_x, m_odd_lam, m_odd_w_pool, m_odd_d_scale, m_odd_w_out, m_odd_ln_g, m_odd_ln_b)))
    vel = dict(zip(WEIGHT_NAMES, (v_even_w_in, v_even_a_ln_g, v_even_a_ln_b, v_even_a_ws, v_even_a_bs, v_even_b_sinks, v_even_w_out, v_even_ln_g, v_even_ln_b, v_odd_w_in, v_odd_conv_w, v_odd_conv_b, v_odd_w_a, v_odd_b_a, v_odd_w_x, v_odd_b_x, v_odd_lam, v_odd_w_pool, v_odd_d_scale, v_odd_w_out, v_odd_ln_g, v_odd_ln_b)))
    mx, my, core = _where_am_i()
    chip = 2 * mx + my

    stored = lambda n, a: jnp.swapaxes(a, 1, 2) if n == 'even_w_in' and W_IN_TRANSPOSED['even'] else a
    names_of = lambda layer: [('even' if layer % 2 == 0 else 'odd') + s for s in ('_w_in', '_w_out')]

    shards = [[stored(n, local[n])[layer // 2].astype(MXU_DTYPE) for n in names_of(layer)] for layer in range(DEPTH)]
    (first_in, first_out), first_token = _ici_start(
        'gather', [[_pack_small_sharded(local), shards[0][0]], [shards[0][1]]], "gather_start_first")
    later, gather_token = _ici_start('gather', shards[1:], "gather_start", after=[first_token])
    gathers = [None] + later

    own, lands = _ici_wait('gather', first_in, [first_token, gather_token], "gather_wait_0")
    small_sharded, first_w_in = _with_own_slot(_forward_to_sibling(lands, "gather_forward_0"), own, chip)
    weights = dict(local)
    weights.update(_unpack_small_sharded(small_sharded))
    passing = {}

    def passed_on(started, after, name):
        own, lands = _ici_wait('gather', started, [after], "gather_wait_" + name)
        (forward,), token = _ici_start('forward', [lands], "forward_start_" + name)

        def finish(then):
            done, _ = _ici_wait('forward', forward, [then], "forward_wait_" + name)
            return _with_own_slot(done, own, chip)

        return token, finish

    def first_w_out(h):
        token, finish = passed_on(first_out, h, "0_out")
        return token, lambda mix: finish(mix)[0]

    def projections_ahead(layer, after):
        token, passing[layer] = passed_on(gathers[layer], after, str(layer))
        return token

    def projections_of(layer, after):
        return (first_w_in, first_w_out) if layer == 0 else passing.pop(layer)(after)

    swapping, in_flight, mine, started = [], [], {}, {}

    def land(*after):
        layer, scatter = in_flight.pop()
        parts, lands = _ici_wait('scatter', scatter, after, "scatter_wait_%d" % layer)
        for n, arrived, part in zip(names_of(layer), lands, parts):
            mine[n, layer // 2] = _sum_slots(arrived, part, chip, "big_sum_%s_%d" % (n, layer // 2))

    early_pieces = [(n, l) for n in ('odd_w_in', 'odd_w_out') for l in range(2)] + [('even_w_in', 1), ('even_w_out', 1)]
    joining, theirs = [], {}

    def projection_grads(layer, dw_in, dw_out):
        if in_flight:
            land(dw_in)
        (swap,), token = _ici_start('swap', [[dw_in, dw_out]], "presum_start_%d" % layer)
        swapping.append((layer, swap))
        if layer > 0:
            return token
        (join,), token = _ici_start('join', [[mine[p] for p in early_pieces]], "join_early_start",
                                    after=[presum_and_scatter(token)])
        joining.append(join)
        return token

    def presum_and_scatter(*after):
        layer, swap = swapping.pop()
        partial, from_sibling = _ici_wait('swap', swap, after, "presum_wait_%d" % layer)
        chip_part = [_add_halves(g, r, core, MXU_DTYPE, "big_presum_add_%s_%d" % (n, layer // 2))
                     for g, r, n in zip(partial, from_sibling, names_of(layer))]
        (scatter,), started[layer] = _ici_start('scatter', [chip_part], "scatter_start_%d" % layer)
        in_flight.append((layer, scatter))
        return started[layer]

    layer_grads_begin = lambda layer, dout: presum_and_scatter(dout) if swapping else dout
    loss, grad_x, grads = _local_step(x[0], positions[0], loss_target[0], weights, projections_of, projections_ahead,
                                      projection_grads, layer_grads_begin)

    (small_swap,), small_token = _ici_start('join', [_small_grad_parts(grads, loss)], "small_swap_start",
                                            after=[started[0]])

    reduced, deltas, new_m, new_v = {}, {}, {}, {}

    def adamw_projections(kind, after):
        names = [kind + '_w_in', kind + '_w_out']
        if kind == 'odd':
            halves, landed = _ici_wait('join', joining.pop(), after, "join_early_wait")
            mine.update(zip(early_pieces, halves))
            theirs.update(zip(early_pieces, landed))
        else:
            late = [(n, 0) for n in names]
            theirs.update(zip(late, _sibling_swap([mine[p] for p in late], "big_join_" + kind, after=after)))
        done = []
        for n in names:
            outs = _adamw_projection(stored(n, local[n]), stored(n, mom[n]), stored(n, vel[n]),
                                     [mine[n, l] for l in range(2)], [theirs[n, l] for l in range(2)], core,
                                     "adamw_" + n)
            reduced[n], deltas[n], new_m[n], new_v[n] = [stored(n, o) for o in outs]
            done.append(outs[-1])
        return done

    done = adamw_projections('odd', [small_token])
    small, from_sibling = _ici_wait('join', small_swap, done, "small_swap_wait")
    chip_sums = [_add2(a, b, "small_presum_add_%d" % i) for i, (a, b) in enumerate(zip(small, from_sibling))]
    (small_gather,), small_token = _ici_start('gather', [chip_sums], "small_gather_start")
    land(grad_x, small_token)
    done = adamw_projections('even', [])

    own, lands = _ici_wait('gather', small_gather, done, "small_gather_wait")
    of_chips = _forward_to_sibling(lands, "small_gather_forward")
    small_totals, total_loss = _from_small_grad_parts(
        [_sum_slots(g, mine_i, chip, "small_sum_%d" % i) for i, (g, mine_i) in enumerate(zip(of_chips, own))], chip)
    reduced.update(small_totals)
    for n in SMALL_NAMES:
        deltas[n], new_m[n], new_v[n] = _adamw(local[n], reduced[n], mom[n], vel[n], "adamw_" + n)

    return (total_loss, grad_x[None], *[reduced[n] for n in WEIGHT_NAMES], *[deltas[n] for n in WEIGHT_NAMES],
            *[new_m[n] for n in WEIGHT_NAMES], *[new_v[n] for n in WEIGHT_NAMES])
```

```python
import math

import jax
import jax.numpy as jnp
from jax import lax
from jax.experimental import pallas as pl
from jax.experimental.pallas import tpu as pltpu

F32 = jnp.float32
MXU_DTYPE = jnp.bfloat16

DEPTH = 4
A_WIDTH = 1024
A_GROUPS = 8
CHUNK = 128
B_HEAD_DIM = 64
B_Q_HEADS = 16
B_KV_HEADS = 2
WINDOW = 128
ROT_DIM = 16
ROPE_THETA = 500000.0
C_WIDTH = 1024
C_HEADS = 8
CONV_WIDTH = 4
LRU_C = 8.0
D_WIDTH = 1024
D_GROUPS = 4
D_GROUP_DIM = 256
EVEN_IN = 5376
DN_ALPHA = (2 * DEPTH) ** 0.25
LN_EPS = 1e-5
ATTN_SCALE = B_HEAD_DIM ** -0.5
NEG_BIG = -1e30

ADAM_LR = 0.001
ADAM_B1 = 0.9
ADAM_B2 = 0.999
ADAM_EPS = 1e-08
ADAM_WD = 0.01
ADAM_STEP = 10

LANES = 128
VMEM_LIMIT = 56 * 1024 * 1024
N_CHIPS = 4
MESH = pl.DeviceIdType.MESH

WEIGHT_NAMES = ['even_w_in', 'even_a_ln_g', 'even_a_ln_b', 'even_a_ws', 'even_a_bs', 'even_b_sinks', 'even_w_out',
                'even_ln_g', 'even_ln_b', 'odd_w_in', 'odd_conv_w', 'odd_conv_b', 'odd_w_a', 'odd_b_a', 'odd_w_x',
                'odd_b_x', 'odd_lam', 'odd_w_pool', 'odd_d_scale', 'odd_w_out', 'odd_ln_g', 'odd_ln_b']
BIG_SHARD_AXIS = {'even_w_in': 2, 'even_w_out': 1, 'odd_w_in': 2, 'odd_w_out': 1}
SMALL_SHARD_AXIS = {'odd_conv_w': 2, 'odd_conv_b': 1, 'odd_b_a': 1, 'odd_b_x': 1, 'odd_lam': 1, 'odd_w_pool': 2,
                    'odd_d_scale': 1, 'odd_ln_g': 1, 'odd_ln_b': 1}
FULL_SHAPES = {
    'even_w_in': (2, 2048, 5376), 'even_a_ln_g': (2, 1024), 'even_a_ln_b': (2, 1024), 'even_a_ws': (2, 8, 128, 128),
    'even_a_bs': (2, 8, 128), 'even_b_sinks': (2, 16), 'even_w_out': (2, 2048, 2048), 'even_ln_g': (2, 2048),
    'even_ln_b': (2, 2048), 'odd_w_in': (2, 2048, 4096), 'odd_conv_w': (2, 4, 1024), 'odd_conv_b': (2, 1024),
    'odd_w_a': (2, 8, 128, 128), 'odd_b_a': (2, 1024), 'odd_w_x': (2, 8, 128, 128), 'odd_b_x': (2, 1024),
    'odd_lam': (2, 1024), 'odd_w_pool': (2, 4, 256, 256), 'odd_d_scale': (2, 1024), 'odd_w_out': (2, 2048, 2048),
    'odd_ln_g': (2, 2048), 'odd_ln_b': (2, 2048)}
BIG_NAMES = ['even_w_in', 'even_w_out', 'odd_w_in', 'odd_w_out']
SMALL_SHARDED = ['odd_conv_w', 'odd_conv_b', 'odd_b_a', 'odd_b_x', 'odd_lam', 'odd_d_scale', 'odd_ln_g', 'odd_ln_b',
                 'odd_w_pool']
SMALL_NAMES = [n for n in WEIGHT_NAMES if n not in BIG_NAMES]
W_IN_TRANSPOSED = {'even': True, 'odd': False}
PACK_COLS = 1024


def _params(sem):
    return pltpu.CompilerParams(dimension_semantics=sem, vmem_limit_bytes=VMEM_LIMIT)


def _dot(a, b):
    return lax.dot_general(a, b, (((1,), (0,)), ((), ())), preferred_element_type=F32)


def _dot_nt(a, b):
    return lax.dot_general(a, b, (((1,), (1,)), ((), ())), preferred_element_type=F32)


def _dot_tn(a, b):
    return lax.dot_general(a, b, (((0,), (0,)), ((), ())), preferred_element_type=F32)


def _sigmoid(x):
    return 0.5 * jnp.tanh(0.5 * x) + 0.5


def _silu_and_grad(x):
    s = _sigmoid(x)
    return x * s, s * (1.0 + x * (1.0 - s))


def _mm_call(body, name, grid, in_specs, out_spec, out_shape, args):
    return pl.pallas_call(body, name=name, grid=grid, in_specs=in_specs, out_specs=out_spec, out_shape=out_shape,
                          compiler_params=_params(("parallel",) * len(grid)))(*args)


def _resident(block, index_map):
    return pl.BlockSpec(block, index_map, pipeline_mode=pl.Buffered(1))


ROWS_OF_WT = 768


def _mm_in(x, wg, transposed, name):
    s, k = x.shape
    x_spec = _resident((s, k), lambda j: (0, 0))

    def body(x_ref, w_ref, o_ref):
        o_ref[...] = (_dot_nt if transposed else _dot)(x_ref[...], w_ref[...])

    if transposed:
        wt = wg.reshape(-1, k)
        n, tn = wt.shape[0], ROWS_OF_WT
        return _mm_call(body, name, (n // tn,), [x_spec, pl.BlockSpec((tn, k), lambda j: (j, 0))],
                        pl.BlockSpec((s, tn), lambda j: (0, j)), jax.ShapeDtypeStruct((s, n), F32), (x, wt))
    nc = wg.shape[2]
    return _mm_call(body, name, (N_CHIPS,), [x_spec, pl.BlockSpec((None, k, nc), lambda j: (j, 0, 0))],
                    pl.BlockSpec((None, s, nc), lambda j: (j, 0, 0)),
                    jax.ShapeDtypeStruct((N_CHIPS, s, nc), F32), (x, wg))


def _mm_dx(dh, wg, dz, transposed, after, name):
    s, d = dz.shape
    tm, tn = min(1024, s), 512
    res = pl.BlockSpec((tm, tn), lambda i, j: (i, j))
    if transposed:
        wt = wg.reshape(-1, d)
        n = wt.shape[0]

        def body(a_ref, w_ref, r_ref, after_ref, o_ref):
            o_ref[...] = DN_ALPHA * r_ref[...] + _dot(a_ref[...], w_ref[...])

        return _mm_call(body, name, (s // tm, d // tn),
                        [pl.BlockSpec((tm, n), lambda i, j: (i, 0)), pl.BlockSpec((n, tn), lambda i, j: (0, j)), res,
                         ANY], res, jax.ShapeDtypeStruct((s, d), F32), (dh, wt, dz, after))
    nc = dh.shape[2]

    def body(a_ref, w_ref, r_ref, after_ref, o_ref):
        acc = DN_ALPHA * r_ref[...]
        for j in range(N_CHIPS):
            acc = acc + _dot_nt(a_ref[j], w_ref[j])
        o_ref[...] = acc

    return _mm_call(body, name, (s // tm, d // tn),
                    [pl.BlockSpec((N_CHIPS, tm, nc), lambda i, j: (0, i, 0)),
                     pl.BlockSpec((N_CHIPS, tn, nc), lambda i, j: (0, j, 0)), res, ANY],
                    res, jax.ShapeDtypeStruct((s, d), F32), (dh, wg, dz, after))


def _mm_dw_in(x, dh, transposed, name):
    s, d = x.shape
    x_spec = _resident((s, d), lambda j: (0, 0))

    def body(a_ref, b_ref, o_ref):
        o_ref[...] = _dot_tn(a_ref[...], b_ref[...]).astype(o_ref.dtype)

    if transposed:
        n, tn = dh.shape[1], ROWS_OF_WT
        flat = _mm_call(body, name, (n // tn,), [pl.BlockSpec((s, tn), lambda j: (0, j)), x_spec],
                        pl.BlockSpec((tn, d), lambda j: (j, 0)), jax.ShapeDtypeStruct((n, d), MXU_DTYPE), (dh, x))
        return flat.reshape(N_CHIPS, n // N_CHIPS, d)
    nc = dh.shape[2]
    return _mm_call(body, name, (N_CHIPS,), [x_spec, pl.BlockSpec((None, s, nc), lambda j: (j, 0, 0))],
                    pl.BlockSpec((None, d, nc), lambda j: (j, 0, 0)),
                    jax.ShapeDtypeStruct((N_CHIPS, d, nc), MXU_DTYPE), (x, dh))


def _mm_out_norm(mix, wg, x, g, b, name, target=None, after=None):
    s, k = mix.shape
    _, kc, d = wg.shape
    tm = min(512, s)

    def normed(a_ref, w_ref, x_ref, g_ref, b_ref, z_ref):
        z = DN_ALPHA * x_ref[...] + _dot(a_ref[...], w_ref[...].reshape(N_CHIPS * kc, d))
        mu = jnp.mean(z, axis=1, keepdims=True)
        zc = z - mu
        var = jnp.mean(zc * zc, axis=1, keepdims=True)
        z_ref[...] = z
        return zc * lax.rsqrt(var + LN_EPS) * g_ref[...] + b_ref[...]

    def body(a_ref, w_ref, x_ref, g_ref, b_ref, after_ref, z_ref, o_ref, ob_ref):
        o = normed(a_ref, w_ref, x_ref, g_ref, b_ref, z_ref)
        o_ref[...] = o
        ob_ref[...] = o.astype(ob_ref.dtype)

    def body_with_loss(a_ref, w_ref, x_ref, g_ref, b_ref, t_ref, z_ref, do_ref, l_ref):
        @pl.when(pl.program_id(0) == 0)
        def _():
            l_ref[...] = jnp.zeros_like(l_ref)

        e = normed(a_ref, w_ref, x_ref, g_ref, b_ref, z_ref) - t_ref[...]
        do_ref[...] = e * (1.0 / d)
        l_ref[...] += (0.5 / d) * jnp.sum(jnp.sum(e * e, axis=1, keepdims=True), axis=0, keepdims=True)

    row = pl.BlockSpec((tm, d), lambda i: (i, 0))
    vec = pl.BlockSpec((1, d), lambda i: (0, 0))
    in_specs = [pl.BlockSpec((tm, k), lambda i: (i, 0)), _resident((N_CHIPS, kc, d), lambda i: (0, 0, 0)), row, vec,
                vec]
    args = (mix, wg, x, g.reshape(1, d), b.reshape(1, d))
    f32_rows = jax.ShapeDtypeStruct((s, d), F32)
    if target is None:
        return pl.pallas_call(
            body, name=name, grid=(s // tm,), in_specs=in_specs + [ANY], out_specs=[row, row, row],
            out_shape=[f32_rows, f32_rows, jax.ShapeDtypeStruct((s, d), MXU_DTYPE)],
            compiler_params=_params(("parallel",)))(*args, after)
    return pl.pallas_call(
        body_with_loss, name=name, grid=(s // tm,), in_specs=in_specs + [row],
        out_specs=[row, row, pl.BlockSpec((1, 1), lambda i: (0, 0))],
        out_shape=[f32_rows, f32_rows, jax.ShapeDtypeStruct((1, 1), F32)],
        compiler_params=_params(("arbitrary",)))(*args, target)


NORM_BWD_ROWS = 256


def _norm_out_bwd(dout, z, g, wg, mix, after, name):
    s, d = z.shape
    kc = wg.shape[1]
    tr = min(NORM_BWD_ROWS, s)
    nr = s // tr

    def body(do_ref, z_ref, g_ref, w_ref, mix_ref, after_ref, dz_ref, dg_ref, db_ref, dmix_ref, dw_ref, dzb_scr):
        i = pl.program_id(0)

        @pl.when(i == 0)
        def _():
            dg_ref[...] = jnp.zeros_like(dg_ref)
            db_ref[...] = jnp.zeros_like(db_ref)

        @pl.when(i < nr)
        def _():
            zz = z_ref[...]
            do = do_ref[...]
            mu = jnp.mean(zz, axis=1, keepdims=True)
            zc = zz - mu
            var = jnp.mean(zc * zc, axis=1, keepdims=True)
            rstd = lax.rsqrt(var + LN_EPS)
            zh = zc * rstd
            dg_ref[...] += jnp.sum(do * zh, axis=0, keepdims=True)
            db_ref[...] += jnp.sum(do, axis=0, keepdims=True)
            dzh = do * g_ref[...]
            dz = rstd * (dzh - jnp.mean(dzh, axis=1, keepdims=True) - zh * jnp.mean(dzh * zh, axis=1, keepdims=True))
            dz_ref[...] = dz
            dzb_scr[pl.ds(pl.multiple_of(i * tr, tr), tr), :] = dz.astype(dzb_scr.dtype)

        @pl.when(i >= nr)
        def _():
            dzb = dzb_scr[...]
            dmix_ref[...] = _dot_nt(dzb, w_ref[...])
            dw_ref[...] = _dot_tn(mix_ref[...], dzb).astype(dw_ref.dtype)

    row = pl.BlockSpec((tr, d), lambda i: (jnp.minimum(i, nr - 1), 0))
    vec = pl.BlockSpec((1, d), lambda i: (0, 0))
    shard = lambda i: jnp.maximum(i - nr, 0)
    return pl.pallas_call(
        body, name=name, grid=(nr + N_CHIPS,),
        in_specs=[row, row, vec, pl.BlockSpec((None, kc, d), lambda i: (shard(i), 0, 0)),
                  pl.BlockSpec((s, kc), lambda i: (0, shard(i))), ANY],
        out_specs=[row, vec, vec, pl.BlockSpec((s, kc), lambda i: (0, shard(i))),
                   pl.BlockSpec((None, kc, d), lambda i: (shard(i), 0, 0))],
        out_shape=[jax.ShapeDtypeStruct((s, d), F32), jax.ShapeDtypeStruct((1, d), F32),
                   jax.ShapeDtypeStruct((1, d), F32), jax.ShapeDtypeStruct((s, N_CHIPS * kc), F32),
                   jax.ShapeDtypeStruct((N_CHIPS, kc, d), MXU_DTYPE)],
        scratch_shapes=[pltpu.VMEM((s, d), MXU_DTYPE)],
        compiler_params=_params(("arbitrary",)),
    )(dout, z, g.reshape(1, d), wg, mix, after)


Q0, K0, V0, BG0 = 3072, 4096, 4224, 4352
T = CHUNK


def _lane_ids(width):
    return lax.broadcasted_iota(jnp.int32, (T, width), 1)


def _rope_swap(x):
    w = x.shape[1]
    l64 = _lane_ids(w) % B_HEAD_DIM
    half = ROT_DIM // 2
    return jnp.where(l64 < half, pltpu.roll(x, w - half, 1), jnp.where(l64 < ROT_DIM, pltpu.roll(x, half, 1), 0.0))


def _rope(x, c, s):
    return x * c + _rope_swap(x) * s


def _rope_bwd(d, c, s):
    return d * c + _rope_swap(d * s)


def _tril():
    return lax.broadcasted_iota(jnp.int32, (T, T), 0) >= lax.broadcasted_iota(jnp.int32, (T, T), 1)


def _band_mask(has_prev):
    qi = lax.broadcasted_iota(jnp.int32, (T, 2 * T), 0)
    kj = lax.broadcasted_iota(jnp.int32, (T, 2 * T), 1)
    return (kj > qi) & (kj <= qi + WINDOW) & ((kj >= WINDOW) | has_prev)


def _a_norm(v, g, b):
    mu = jnp.mean(v, axis=1, keepdims=True)
    vc = v - mu
    var = jnp.mean(vc * vc, axis=1, keepdims=True)
    rstd = lax.rsqrt(var + LN_EPS)
    vh = vc * rstd
    return vh, rstd, vh * g + b


def _kv_operands(hk, k_rot, kp_rot, vv, vp):
    lo = _lane_ids(LANES) < B_HEAD_DIM
    mine = lo if hk == 0 else jnp.logical_not(lo)
    both = lambda t: jnp.where(mine, t, 0.0) + pltpu.roll(jnp.where(mine, t, 0.0), B_HEAD_DIM, 1)
    k2 = jnp.concatenate([both(kp_rot), both(k_rot)], axis=0)
    v2 = jnp.concatenate([both(vp), both(vv)], axis=0)
    return k2.astype(MXU_DTYPE), v2.astype(MXU_DTYPE)


GROUP_HEADS = B_Q_HEADS // B_KV_HEADS


def _pair_columns(hk):
    return [(hk * GROUP_HEADS // 2 + pp) * LANES for pp in range(GROUP_HEADS // 2)]


def _stacked_heads(pairs):
    lo = _lane_ids(LANES) < B_HEAD_DIM
    rows = []
    for tile in pairs:
        rows += [jnp.where(lo, tile, 0.0), jnp.where(lo, 0.0, tile)]
    return jnp.concatenate(rows, axis=0).astype(MXU_DTYPE)


def _pair_of(stacked, pp):
    lo = _lane_ids(LANES) < B_HEAD_DIM
    return jnp.where(lo, stacked[2 * pp * T:(2 * pp + 1) * T], stacked[(2 * pp + 1) * T:(2 * pp + 2) * T])


def _softmax_with_sink(qm, k2, valid, sink_ref, head0):
    scores = (_dot_nt(qm, k2) * ATTN_SCALE).reshape(GROUP_HEADS, T, 2 * T)
    s = jnp.where(valid[None], scores, NEG_BIG).reshape(GROUP_HEADS * T, 2 * T)
    sink = jnp.concatenate([jnp.broadcast_to(sink_ref[:, head0 + h:head0 + h + 1], (T, 1))
                            for h in range(GROUP_HEADS)], axis=0)
    m = jnp.maximum(jnp.max(s, axis=1, keepdims=True), sink)
    ex = jnp.exp(s - m)
    es = jnp.exp(sink - m)
    inv = 1.0 / (jnp.sum(ex, axis=1, keepdims=True) + es)
    return ex * inv, es * inv


def _even_specs(s):
    nb = s // T
    prev = lambda r: lambda i: jnp.maximum(r(i) - 1, 0)
    return nb, {
        'h': lambda r: pl.BlockSpec((T, EVEN_IN), lambda i: (r(i), 0)),
        'kprev': lambda r: pl.BlockSpec((T, LANES), lambda i: (prev(r)(i), K0 // LANES)),
        'vprev': lambda r: pl.BlockSpec((T, LANES), lambda i: (prev(r)(i), V0 // LANES)),
        'tab': lambda r: pl.BlockSpec((T, LANES), lambda i: (r(i), 0)),
        'tabprev': lambda r: pl.BlockSpec((T, LANES), lambda i: (prev(r)(i), 0)),
    }


def _full(shape):
    nd = len(shape)
    return pl.BlockSpec(shape, lambda i: (0,) * nd)


def _even_fwd(h, cq, sq, ln_g, ln_b, ws, bs_t, sinks, after):
    s = h.shape[0]
    nb, sp = _even_specs(s)

    def body(h_ref, kp_ref, vp_ref, cq_ref, sq_ref, ckp_ref, skp_ref, g_ref, b_ref, ws_ref, bst_ref, sink_ref,
             after_ref, mix_ref):
        i = pl.program_id(0)
        tril = _tril()
        _, _, vn = _a_norm(h_ref[:, 1024:2048], g_ref[...], b_ref[...])
        vnb = vn.astype(MXU_DTYPE)
        for g in range(A_GROUPS):
            c0, c1 = g * LANES, (g + 1) * LANES
            wg = jnp.where(tril, ws_ref[g], 0.0).astype(MXU_DTYPE)
            mixed = _dot(wg, vnb[:, c0:c1]) + bst_ref[:, g:g + 1]
            sil, _ = _silu_and_grad(h_ref[:, 2048 + c0:2048 + c1])
            mix_ref[:, c0:c1] = (h_ref[:, c0:c1] * mixed * sil).astype(mix_ref.dtype)

        cq, sq = cq_ref[...], sq_ref[...]
        k_rot = _rope(h_ref[:, K0:K0 + LANES], cq, sq)
        kp_rot = _rope(kp_ref[...], ckp_ref[...], skp_ref[...])
        valid = _band_mask(i > 0)
        for hk in range(B_KV_HEADS):
            k2, v2 = _kv_operands(hk, k_rot, kp_rot, h_ref[:, V0:V0 + LANES], vp_ref[...])
            qm = _stacked_heads([_rope(h_ref[:, Q0 + c0:Q0 + c0 + LANES], cq, sq) for c0 in _pair_columns(hk)])
            p, _ = _softmax_with_sink(qm, k2, valid, sink_ref, GROUP_HEADS * hk)
            out = _dot(p.astype(MXU_DTYPE), v2)
            for pp, c0 in enumerate(_pair_columns(hk)):
                sil, _ = _silu_and_grad(h_ref[:, BG0 + c0:BG0 + c0 + LANES])
                mix_ref[:, A_WIDTH + c0:A_WIDTH + c0 + LANES] = (_pair_of(out, pp) * sil).astype(mix_ref.dtype)

    ident = lambda i: i
    return pl.pallas_call(
        body, name="even_fwd", grid=(nb,),
        in_specs=[sp['h'](ident), sp['kprev'](ident), sp['vprev'](ident), sp['tab'](ident), sp['tab'](ident),
                  sp['tabprev'](ident), sp['tabprev'](ident), _full((1, A_WIDTH)), _full((1, A_WIDTH)),
                  _full((A_GROUPS, T, T)), _full((T, A_GROUPS)), _full((1, B_Q_HEADS)), ANY],
        out_specs=pl.BlockSpec((T, 2 * A_WIDTH), lambda i: (i, 0)),
        out_shape=jax.ShapeDtypeStruct((s, 2 * A_WIDTH), MXU_DTYPE),
        compiler_params=_params(("parallel",)),
    )(h, h, h, cq, sq, cq, sq, ln_g.reshape(1, -1), ln_b.reshape(1, -1), ws, bs_t, sinks.reshape(1, -1), after)


def _even_bwd(h, dmix, cq, sq, ln_g, ln_b, ws, bs_t, sinks):
    s = h.shape[0]
    nb, sp = _even_specs(s)

    def body(h_ref, kp_ref, vp_ref, cq_ref, sq_ref, ckp_ref, skp_ref, dm_ref, g_ref, b_ref, ws_ref, bst_ref, sink_ref,
             dh_ref, dg_ref, db_ref, dws_ref, dbst_ref, dsink_ref, dvn_scr, ck_scr, cv_scr):
        i = pl.program_id(0)

        @pl.when(i == 0)
        def _():
            for ref in (dg_ref, db_ref, dws_ref, dbst_ref, dsink_ref, ck_scr, cv_scr):
                ref[...] = jnp.zeros_like(ref)

        tril = _tril()
        vh, rstd, vn = _a_norm(h_ref[:, 1024:2048], g_ref[...], b_ref[...])
        vnb = vn.astype(MXU_DTYPE)
        for g in range(A_GROUPS):
            c0, c1 = g * LANES, (g + 1) * LANES
            wg = jnp.where(tril, ws_ref[g], 0.0).astype(MXU_DTYPE)
            mixed = _dot(wg, vnb[:, c0:c1]) + bst_ref[:, g:g + 1]
            ag = h_ref[:, 2048 + c0:2048 + c1]
            u = h_ref[:, c0:c1]
            sil, dsil = _silu_and_grad(ag)
            da = dm_ref[:, c0:c1]
            dh_ref[:, c0:c1] = (da * mixed * sil).astype(dh_ref.dtype)
            dh_ref[:, 2048 + c0:2048 + c1] = (da * u * mixed * dsil).astype(dh_ref.dtype)
            dmx = da * u * sil
            dmb = dmx.astype(MXU_DTYPE)
            dvn_scr[:, c0:c1] = _dot_tn(wg, dmb)
            dws_ref[g] += jnp.where(tril, _dot_nt(dmb, vnb[:, c0:c1]), 0.0)
            dbst_ref[:, g:g + 1] += jnp.sum(dmx, axis=1, keepdims=True)
        dvn = dvn_scr[...]
        dg_ref[...] += jnp.sum(dvn * vh, axis=0, keepdims=True)
        db_ref[...] += jnp.sum(dvn, axis=0, keepdims=True)
        dvh = dvn * g_ref[...]
        dv = rstd * (dvh - jnp.mean(dvh, axis=1, keepdims=True) - vh * jnp.mean(dvh * vh, axis=1, keepdims=True))
        dh_ref[:, 1024:2048] = dv.astype(dh_ref.dtype)

        cq, sq = cq_ref[...], sq_ref[...]
        k_rot = _rope(h_ref[:, K0:K0 + LANES], cq, sq)
        kp_rot = _rope(kp_ref[...], ckp_ref[...], skp_ref[...])
        valid = _band_mask(i < nb - 1)
        dk_all = jnp.zeros((2 * T, LANES), F32)
        dv_all = jnp.zeros((2 * T, LANES), F32)
        for hk in range(B_KV_HEADS):
            k2, v2 = _kv_operands(hk, k_rot, kp_rot, h_ref[:, V0:V0 + LANES], vp_ref[...])
            cols = _pair_columns(hk)
            qm = _stacked_heads([_rope(h_ref[:, Q0 + c0:Q0 + c0 + LANES], cq, sq) for c0 in cols])
            p, psink = _softmax_with_sink(qm, k2, valid, sink_ref, GROUP_HEADS * hk)
            pb = p.astype(MXU_DTYPE)
            out = _dot(pb, v2)
            dobs = []
            for pp, c0 in enumerate(cols):
                sil, dsil = _silu_and_grad(h_ref[:, BG0 + c0:BG0 + c0 + LANES])
                dbo = dm_ref[:, A_WIDTH + c0:A_WIDTH + c0 + LANES]
                dh_ref[:, BG0 + c0:BG0 + c0 + LANES] = (dbo * _pair_of(out, pp) * dsil).astype(dh_ref.dtype)
                dobs.append(dbo * sil)
            dobm = _stacked_heads(dobs)
            dp = _dot_nt(dobm, v2)
            rs = jnp.sum(p * dp, axis=1, keepdims=True)
            dsb = (p * (dp - rs) * ATTN_SCALE).astype(MXU_DTYPE)
            to_sink = psink * rs
            for h in range(GROUP_HEADS):
                head = GROUP_HEADS * hk + h
                dsink_ref[:, head:head + 1] += -jnp.sum(to_sink[h * T:(h + 1) * T], axis=0, keepdims=True)
            dq = _dot(dsb, k2)
            for pp, c0 in enumerate(cols):
                dh_ref[:, Q0 + c0:Q0 + c0 + LANES] = _rope_bwd(_pair_of(dq, pp), cq, sq).astype(dh_ref.dtype)
            acc_k = _dot_tn(dsb, qm)
            acc_v = _dot_tn(pb, dobm)
            lo2 = lax.broadcasted_iota(jnp.int32, (2 * T, LANES), 1) < B_HEAD_DIM
            mine = lo2 if hk == 0 else jnp.logical_not(lo2)
            dk_all = dk_all + jnp.where(mine, acc_k + pltpu.roll(acc_k, B_HEAD_DIM, 1), 0.0)
            dv_all = dv_all + jnp.where(mine, acc_v + pltpu.roll(acc_v, B_HEAD_DIM, 1), 0.0)
        dk_rot = dk_all[T:2 * T] + ck_scr[...]
        dvv = dv_all[T:2 * T] + cv_scr[...]
        ck_scr[...] = dk_all[0:T]
        cv_scr[...] = dv_all[0:T]
        dh_ref[:, K0:K0 + LANES] = _rope_bwd(dk_rot, cq, sq).astype(dh_ref.dtype)
        dh_ref[:, V0:V0 + LANES] = dvv.astype(dh_ref.dtype)

    rev = lambda i: nb - 1 - i
    return pl.pallas_call(
        body, name="even_bwd", grid=(nb,),
        in_specs=[sp['h'](rev), sp['kprev'](rev), sp['vprev'](rev), sp['tab'](rev), sp['tab'](rev),
                  sp['tabprev'](rev), sp['tabprev'](rev), pl.BlockSpec((T, 2 * A_WIDTH), lambda i: (rev(i), 0)),
                  _full((1, A_WIDTH)), _full((1, A_WIDTH)), _full((A_GROUPS, T, T)), _full((T, A_GROUPS)),
                  _full((1, B_Q_HEADS))],
        out_specs=[sp['h'](rev), _full((1, A_WIDTH)), _full((1, A_WIDTH)),
                   _full((A_GROUPS, T, T)), _full((T, A_GROUPS)), _full((1, B_Q_HEADS))],
        out_shape=[jax.ShapeDtypeStruct((s, EVEN_IN), MXU_DTYPE), jax.ShapeDtypeStruct((1, A_WIDTH), F32),
                   jax.ShapeDtypeStruct((1, A_WIDTH), F32), jax.ShapeDtypeStruct((A_GROUPS, T, T), F32),
                   jax.ShapeDtypeStruct((T, A_GROUPS), F32), jax.ShapeDtypeStruct((1, B_Q_HEADS), F32)],
        scratch_shapes=[pltpu.VMEM((T, A_WIDTH), F32), pltpu.VMEM((T, LANES), F32), pltpu.VMEM((T, LANES), F32)],
        compiler_params=_params(("arbitrary",)),
    )(h, h, h, cq, sq, cq, sq, dmix, ln_g.reshape(1, -1), ln_b.reshape(1, -1), ws, bs_t, sinks.reshape(1, -1))


def _shift_down(x, d, fill=0.0):
    rows = lax.broadcasted_iota(jnp.int32, x.shape, 0)
    return jnp.where(rows >= d, pltpu.roll(x, d, 0), fill)


def _shift_up(x, d, fill=0.0):
    n = x.shape[0]
    rows = lax.broadcasted_iota(jnp.int32, x.shape, 0)
    return jnp.where(rows < n - d, pltpu.roll(x, n - d, 0), fill)


SCAN_BLOCK = 32
SUBLANES = 8


def _scan_block(a, b, up):
    n = a.shape[0]
    d = 1
    while d < n:
        if d < SUBLANES:
            shift = _shift_up if up else _shift_down
            a, b = a * shift(a, d, 1.0), a * shift(b, d) + b
        elif up:
            a, b = (jnp.concatenate([a[:n - d] * a[d:], a[n - d:]], axis=0),
                    jnp.concatenate([a[:n - d] * b[d:] + b[:n - d], b[n - d:]], axis=0))
        else:
            a, b = (jnp.concatenate([a[:d], a[d:] * a[:n - d]], axis=0),
                    jnp.concatenate([b[:d], a[d:] * b[:n - d] + b[d:]], axis=0))
        d *= 2
    return a, b


def _scan(a, b, up):
    n = a.shape[0]
    size = min(SCAN_BLOCK, n)
    blocks = n // size
    out, carry = [None] * blocks, None
    for k in (reversed(range(blocks)) if up else range(blocks)):
        prod, h = _scan_block(a[k * size:(k + 1) * size], b[k * size:(k + 1) * size], up)
        if carry is not None:
            h = h + prod * carry
        carry = h[0:1] if up else h[size - 1:size]
        out[k] = h
    return jnp.concatenate(out, axis=0)


def _neg_expm1(y):
    t = jnp.tanh(-0.5 * y)
    return 2.0 * t / (1.0 + t)


def _softplus(x):
    return jnp.maximum(x, 0.0) + jnp.log1p(jnp.exp(-jnp.abs(x)))


def _lru_forward(xc, cw, cb, wa, ba, wx, bx, lam):
    shifted = [_shift_down(xc, CONV_WIDTH - 1 - j) if j < CONV_WIDTH - 1 else xc for j in range(CONV_WIDTH)]
    xconv = cb
    for j in range(CONV_WIDTH):
        xconv = xconv + cw[j:j + 1, :] * shifted[j]
    xb = xconv.astype(MXU_DTYPE)
    r = _sigmoid(_dot(xb, wa.astype(MXU_DTYPE)) + ba)
    ig = _sigmoid(_dot(xb, wx.astype(MXU_DTYPE)) + bx)
    sp = _softplus(-lam)
    log_a = -LRU_C * r * sp
    a = jnp.exp(log_a)
    mult = jnp.sqrt(_neg_expm1(2.0 * log_a))
    hs = _scan(a, mult * ig * xconv, up=False)
    return dict(shifted=shifted, xconv=xconv, xb=xb, r=r, ig=ig, sp=sp, a=a, mult=mult, hs=hs)


def _col(width, off):
    return lambda s: pl.BlockSpec((s, width), lambda i: (0, off + i))


def _part(width, slot):
    return lambda s: pl.BlockSpec((None, s, width), lambda i: (slot, 0, i))


def _lru_specs(s):
    w = LANES
    return [_part(w, 0)(s), _part(w, 1)(s), pl.BlockSpec((CONV_WIDTH, w), lambda i: (0, i)),
            pl.BlockSpec((1, w), lambda i: (0, i)), pl.BlockSpec((None, w, w), lambda i: (i, 0, 0)),
            pl.BlockSpec((1, w), lambda i: (0, i)), pl.BlockSpec((None, w, w), lambda i: (i, 0, 0)),
            pl.BlockSpec((1, w), lambda i: (0, i)), pl.BlockSpec((1, w), lambda i: (0, i))]


def _lru_fwd(h, cw, cb, wa, ba, wx, bx, lam):
    s = h.shape[1]

    def body(xc_ref, cg_ref, cw_ref, cb_ref, wa_ref, ba_ref, wx_ref, bx_ref, lam_ref, o_ref):
        f = _lru_forward(xc_ref[...], cw_ref[...], cb_ref[...], wa_ref[...], ba_ref[...], wx_ref[...], bx_ref[...],
                         lam_ref[...])
        sil, _ = _silu_and_grad(cg_ref[...])
        o_ref[...] = (f['hs'] * sil).astype(o_ref.dtype)

    return pl.pallas_call(
        body, name="lru_fwd", grid=(C_HEADS,), in_specs=_lru_specs(s), out_specs=_col(LANES, 0)(s),
        out_shape=jax.ShapeDtypeStruct((s, C_WIDTH + D_WIDTH), MXU_DTYPE), compiler_params=_params(("parallel",)),
    )(h, h, cw, cb.reshape(1, -1), wa, ba.reshape(1, -1), wx, bx.reshape(1, -1), lam.reshape(1, -1))


def _lru_bwd(h, dmix, cw, cb, wa, ba, wx, bx, lam):
    s = h.shape[1]

    def body(xc_ref, cg_ref, cw_ref, cb_ref, wa_ref, ba_ref, wx_ref, bx_ref, lam_ref, dco_ref,
             dh_ref, dcw_ref, dcb_ref, dwa_ref, dba_ref, dwx_ref, dbx_ref, dlam_ref):
        dxc_ref, dcg_ref = dh_ref.at[0], dh_ref.at[1]
        cw, lam = cw_ref[...], lam_ref[...]
        wab, wxb = wa_ref[...].astype(MXU_DTYPE), wx_ref[...].astype(MXU_DTYPE)
        f = _lru_forward(xc_ref[...], cw, cb_ref[...], wa_ref[...], ba_ref[...], wx_ref[...], bx_ref[...], lam)
        sil, dsil = _silu_and_grad(cg_ref[...])
        dco = dco_ref[...]
        hs, a, mult, ig, r, xconv = f['hs'], f['a'], f['mult'], f['ig'], f['r'], f['xconv']
        dcg_ref[...] = (dco * hs * dsil).astype(dcg_ref.dtype)
        lamb = _scan(_shift_up(a, 1), dco * sil, up=True)
        da = lamb * _shift_down(hs, 1)
        d_ig = lamb * mult * xconv
        d_mult = lamb * ig * xconv
        dxconv = lamb * mult * ig
        d_log_a = da * a - d_mult * (a * a) / mult
        dlam_ref[...] = jnp.sum(d_log_a * r, axis=0, keepdims=True) * LRU_C * _sigmoid(-lam)
        dpa = d_log_a * (-LRU_C * f['sp']) * r * (1.0 - r)
        dpx = d_ig * ig * (1.0 - ig)
        dba_ref[...] = jnp.sum(dpa, axis=0, keepdims=True)
        dbx_ref[...] = jnp.sum(dpx, axis=0, keepdims=True)
        dpab, dpxb = dpa.astype(MXU_DTYPE), dpx.astype(MXU_DTYPE)
        dwa_ref[...] = _dot_tn(f['xb'], dpab)
        dwx_ref[...] = _dot_tn(f['xb'], dpxb)
        dxconv = dxconv + _dot_nt(dpab, wab) + _dot_nt(dpxb, wxb)
        dcb_ref[...] = jnp.sum(dxconv, axis=0, keepdims=True)
        dxc = jnp.zeros_like(dxconv)
        for j in range(CONV_WIDTH):
            dcw_ref[j:j + 1, :] = jnp.sum(dxconv * f['shifted'][j], axis=0, keepdims=True)
            back = _shift_up(dxconv, CONV_WIDTH - 1 - j) if j < CONV_WIDTH - 1 else dxconv
            dxc = dxc + cw[j:j + 1, :] * back
        dxc_ref[...] = dxc.astype(dxc_ref.dtype)

    w = LANES
    vec = pl.BlockSpec((1, w), lambda i: (0, i))
    mat = pl.BlockSpec((None, w, w), lambda i: (i, 0, 0))
    vshape = jax.ShapeDtypeStruct((1, C_WIDTH), F32)
    mshape = jax.ShapeDtypeStruct((C_HEADS, w, w), F32)
    return pl.pallas_call(
        body, name="lru_bwd", grid=(C_HEADS,), in_specs=_lru_specs(s) + [_col(w, 0)(s)],
        out_specs=[pl.BlockSpec((2, s, w), lambda i: (0, 0, i)), pl.BlockSpec((CONV_WIDTH, w), lambda i: (0, i)),
                   vec, mat, vec, mat, vec, vec],
        out_shape=[jax.ShapeDtypeStruct((N_CHIPS, s, C_WIDTH), MXU_DTYPE),
                   jax.ShapeDtypeStruct((CONV_WIDTH, C_WIDTH), F32), vshape, mshape, vshape, mshape, vshape, vshape],
        compiler_params=_params(("parallel",)),
    )(h, h, cw, cb.reshape(1, -1), wa, ba.reshape(1, -1), wx, bx.reshape(1, -1), lam.reshape(1, -1), dmix)


def _pool_window(g):
    return jnp.left_shift(2, g).astype(F32)


def _select_window(g, sums):
    out = sums[-1]
    for k in range(len(sums) - 2, -1, -1):
        out = jnp.where(g == k, sums[k], out)
    return out


def _window_sums(x, shift):
    sums, cur = [], x
    for k in range(D_GROUPS):
        cur = cur + shift(cur, 2 ** k)
        sums.append(cur)
    return sums


def _pool_specs(s):
    w = D_GROUP_DIM
    return [_part(w, 2)(s), _part(w, 3)(s),
            pl.BlockSpec((None, w, w), lambda i: (i, 0, 0)), pl.BlockSpec((1, w), lambda i: (0, i))]


def _pool_forward(xd, g):
    rows = lax.broadcasted_iota(jnp.int32, xd.shape, 0).astype(F32) + 1.0
    cnt = jnp.minimum(rows, _pool_window(g))
    pooled = _select_window(g, _window_sums(xd, _shift_down)) / cnt - xd
    return pooled, cnt


def _pool_fwd(h, mix, w_pool, d_scale):
    s = h.shape[1]

    def body(xd_ref, dg_ref, wp_ref, sc_ref, mix_ref, o_ref):
        pooled, _ = _pool_forward(xd_ref[...], pl.program_id(0))
        mixed = _dot(pooled.astype(MXU_DTYPE), wp_ref[...].astype(MXU_DTYPE))
        sil, _ = _silu_and_grad(dg_ref[...])
        o_ref[...] = (mixed * sc_ref[...] * sil).astype(o_ref.dtype)

    return pl.pallas_call(
        body, name="pool_fwd", grid=(D_GROUPS,), in_specs=_pool_specs(s) + [ANY],
        out_specs=_col(D_GROUP_DIM, C_WIDTH // D_GROUP_DIM)(s), out_shape=jax.ShapeDtypeStruct(mix.shape, mix.dtype),
        input_output_aliases={4: 0}, compiler_params=_params(("parallel",)),
    )(h, h, w_pool, d_scale.reshape(1, -1), mix)


def _pool_bwd(h, dmix, dh, w_pool, d_scale):
    s = h.shape[1]
    w = D_GROUP_DIM

    def body(xd_ref, dg_ref, wp_ref, sc_ref, ddo_ref, dh_in_ref, dh_ref, dwp_ref, dsc_ref):
        dxd_ref, ddg_ref = dh_ref.at[0], dh_ref.at[1]
        g = pl.program_id(0)
        pooled, cnt = _pool_forward(xd_ref[...], g)
        pb = pooled.astype(MXU_DTYPE)
        wpb = wp_ref[...].astype(MXU_DTYPE)
        mixed = _dot(pb, wpb)
        sil, dsil = _silu_and_grad(dg_ref[...])
        ddo = ddo_ref[...]
        sc = sc_ref[...]
        ddg_ref[...] = (ddo * mixed * sc * dsil).astype(ddg_ref.dtype)
        dms = ddo * sil
        dsc_ref[...] = jnp.sum(dms * mixed, axis=0, keepdims=True)
        dmb = (dms * sc).astype(MXU_DTYPE)
        dwp_ref[...] = _dot_tn(pb, dmb)
        dpooled = _dot_nt(dmb, wpb)
        dxd = _select_window(g, _window_sums(dpooled / cnt, _shift_up)) - dpooled
        dxd_ref[...] = dxd.astype(dxd_ref.dtype)

    return pl.pallas_call(
        body, name="pool_bwd", grid=(D_GROUPS,), in_specs=_pool_specs(s) + [_col(w, C_WIDTH // w)(s), ANY],
        out_specs=[pl.BlockSpec((2, s, w), lambda i: (1, 0, i)), pl.BlockSpec((None, w, w), lambda i: (i, 0, 0)),
                   pl.BlockSpec((1, w), lambda i: (0, i))],
        out_shape=[jax.ShapeDtypeStruct(dh.shape, dh.dtype),
                   jax.ShapeDtypeStruct((D_GROUPS, w, w), F32), jax.ShapeDtypeStruct((1, D_WIDTH), F32)],
        input_output_aliases={5: 0}, compiler_params=_params(("parallel",)),
    )(h, h, w_pool, d_scale.reshape(1, -1), dmix, dh)


def _rope_tables(positions):
    s = positions.shape[0]
    inv_freq = ROPE_THETA ** (-jnp.arange(0, ROT_DIM, 2, dtype=F32) / ROT_DIM)
    ang = positions.astype(F32)[:, None] * inv_freq
    c, sn = jnp.cos(ang), jnp.sin(ang)
    rest = B_HEAD_DIM - ROT_DIM
    c64 = jnp.concatenate([c, c, jnp.ones((s, rest), F32)], axis=1)
    s64 = jnp.concatenate([-sn, sn, jnp.zeros((s, rest), F32)], axis=1)
    return jnp.tile(c64, (1, LANES // B_HEAD_DIM)), jnp.tile(s64, (1, LANES // B_HEAD_DIM))


def _local_step(x, positions, target, w, projections_of, projections_ahead, projection_grads, layer_grads_begin):
    cq, sq = _rope_tables(positions)
    saved = []
    xin, xin_b = x, x.astype(MXU_DTYPE)
    for layer in range(DEPTH):
        j = layer // 2
        kind = 'even' if layer % 2 == 0 else 'odd'
        w_in, w_out = projections_of(layer, xin_b)
        h = _mm_in(xin_b, w_in, W_IN_TRANSPOSED[kind], "mm_" + kind + "_in")
        if kind == 'even':
            behind, w_out_after = w_out(h) if callable(w_out) else (h, None)
            mix = _even_fwd(h, cq, sq, w['even_a_ln_g'][j], w['even_a_ln_b'][j], w['even_a_ws'][j],
                            w['even_a_bs'][j].T, w['even_b_sinks'][j], behind)
            w_out = w_out_after(mix) if w_out_after else w_out
        else:
            mix = _lru_fwd(h, w['odd_conv_w'][j], w['odd_conv_b'][j], w['odd_w_a'][j], w['odd_b_a'][j],
                           w['odd_w_x'][j], w['odd_b_x'][j], w['odd_lam'][j])
            mix = _pool_fwd(h, mix, w['odd_w_pool'][j], w['odd_d_scale'][j])
        if layer == DEPTH - 1:
            z, out, out_b = _mm_out_norm(mix, w_out, xin, w[kind + '_ln_g'][j], w[kind + '_ln_b'][j],
                                         "mm_out_norm_loss", target=target)
        else:
            z, out, out_b = _mm_out_norm(mix, w_out, xin, w[kind + '_ln_g'][j], w[kind + '_ln_b'][j], "mm_out_norm",
                                         after=projections_ahead(layer + 1, mix))
        saved.append((xin_b, h, mix, z, w_in, w_out))
        xin, xin_b = out, out_b

    dout, loss = xin, xin_b
    per_layer = {n: [None, None] for n in SMALL_NAMES}
    for layer in reversed(range(DEPTH)):
        j = layer // 2
        kind = 'even' if layer % 2 == 0 else 'odd'
        xin_b, h, mix, z, w_in, w_out = saved[layer]
        dz, dg, db, dmix, dw_out = _norm_out_bwd(dout, z, w[kind + '_ln_g'][j], w_out, mix,
                                                 layer_grads_begin(layer, dout), "norm_out_bwd")
        per_layer[kind + '_ln_g'][j], per_layer[kind + '_ln_b'][j] = dg[0], db[0]
        if kind == 'even':
            dh, dlg, dlb, dws, dbst, dsink = _even_bwd(
                h, dmix, cq, sq, w['even_a_ln_g'][j], w['even_a_ln_b'][j], w['even_a_ws'][j], w['even_a_bs'][j].T,
                w['even_b_sinks'][j])
            for n, val in (('even_a_ln_g', dlg[0]), ('even_a_ln_b', dlb[0]), ('even_a_ws', dws),
                           ('even_a_bs', dbst.T), ('even_b_sinks', dsink[0])):
                per_layer[n][j] = val
        else:
            dh, dcw, dcb, dwa, dba, dwx, dbx, dlam = _lru_bwd(
                h, dmix, w['odd_conv_w'][j], w['odd_conv_b'][j], w['odd_w_a'][j], w['odd_b_a'][j], w['odd_w_x'][j],
                w['odd_b_x'][j], w['odd_lam'][j])
            dh, dwp, dsc = _pool_bwd(h, dmix, dh, w['odd_w_pool'][j], w['odd_d_scale'][j])
            for n, val in (('odd_conv_w', dcw), ('odd_conv_b', dcb[0]), ('odd_w_a', dwa), ('odd_b_a', dba[0]),
                           ('odd_w_x', dwx), ('odd_b_x', dbx[0]), ('odd_lam', dlam[0]), ('odd_w_pool', dwp),
                           ('odd_d_scale', dsc[0])):
                per_layer[n][j] = val
        dw_in = _mm_dw_in(xin_b, dh, W_IN_TRANSPOSED[kind], "mm_dw_in_" + kind)
        dout = _mm_dx(dh, w_in, dz, W_IN_TRANSPOSED[kind], projection_grads(layer, dw_in, dw_out), "mm_dx_" + kind)
    return loss, dout, {n: jnp.stack(v) for n, v in per_layer.items()}


ANY = pl.BlockSpec(memory_space=pl.ANY)


def _where_am_i():
    return lax.axis_index("x"), lax.axis_index("y"), lax.axis_index("c")


def _other_chips(x, y):
    return [(1 - x, y), (x, 1 - y), (1 - x, 1 - y)]


PACKED_ROWS = 16


def _half_rows(rows, which):
    half = rows // 2
    assert half % PACKED_ROWS == 0, rows
    return pl.ds(pl.multiple_of(which * half, PACKED_ROWS), half)


def _sibling_swap(bufs, name, after=()):
    n, na = len(bufs), len(after)

    def body(*refs):
        x_refs, out_refs, (send_sems, recv_sems) = refs[:n], refs[n + na:2 * n + na], refs[2 * n + na:]
        x, y, c = _where_am_i()
        copies = []
        for p in range(n):
            copies.append(pltpu.make_async_remote_copy(
                src_ref=x_refs[p], dst_ref=out_refs[p], send_sem=send_sems.at[p], recv_sem=recv_sems.at[p],
                device_id=(x, y, 1 - c), device_id_type=MESH))
            copies[-1].start()
        for cp in copies:
            cp.wait()

    return pl.pallas_call(
        body, name=name, in_specs=[ANY] * (n + na), out_specs=[ANY] * n,
        out_shape=[jax.ShapeDtypeStruct(b.shape, b.dtype) for b in bufs],
        scratch_shapes=[pltpu.SemaphoreType.DMA((n,)), pltpu.SemaphoreType.DMA((n,))],
    )(*bufs, *after)


HBM = pl.BlockSpec(memory_space=pltpu.HBM)
SEM = pl.BlockSpec(memory_space=pltpu.SEMAPHORE)
IN_FLIGHT = pltpu.CompilerParams(has_side_effects=pltpu.SideEffectType.DATAFLOW_SIDE_EFFECTING)


IN_PLACE = ('forward',)


def _peers(kind, x, y, c):
    if kind in ('swap', 'join'):
        return [(x, y, 1 - c)]
    return [(x, y, 1 - c) if kind == 'forward' else (px, py, c) for px, py in _other_chips(x, y)]


def _n_peers(kind):
    return 1 if kind in ('swap', 'join') else N_CHIPS - 1


def _land_shape(kind, a):
    if kind == 'gather':
        return (N_CHIPS,) + a.shape
    return (a.shape[0], a.shape[1] // 2, a.shape[2]) if kind == 'swap' else a.shape


def _ici_copy(kind, src_ref, land_ref, send_sems, recv_sems, k, peer, x, y, c, arriving=False):
    me = 2 * x + y
    if kind == 'gather':
        half = _half_rows(src_ref.shape[0], c)
        src, dst = src_ref.at[half, :], land_ref.at[me, half, :]
    elif kind == 'scatter':
        src, dst = src_ref.at[2 * peer[0] + peer[1]], land_ref.at[me]
    elif kind == 'swap':
        src, dst = src_ref.at[:, _half_rows(src_ref.shape[1], 1 - c), :], land_ref
    elif kind == 'join':
        src, dst = src_ref, land_ref
    else:
        px, py = _other_chips(x, y)[k % (N_CHIPS - 1)]
        rows = src_ref.at[2 * px + py, _half_rows(src_ref.shape[1], 1 - c if arriving else c), :]
        src, dst = rows, rows
    return pltpu.make_async_remote_copy(src_ref=src, dst_ref=dst, send_sem=send_sems.at[k], recv_sem=recv_sems.at[k],
                                        device_id=peer, device_id_type=MESH)


def _ici_start(kind, groups, name, after=()):
    flat = [a for g in groups for a in g]
    n, ng, np_, na = len(flat), len(groups), _n_peers(kind), len(after)
    nl = 0 if kind in IN_PLACE else n
    land_shapes = [_land_shape(kind, a) for a in flat[:nl]]

    def body(*refs):
        srcs, sems, token = refs[:n], refs[n + nl + na:n + nl + na + 2 * ng], refs[-1]
        lands = refs[n:n + nl] if nl else srcs
        x, y, c = _where_am_i()
        p = 0
        for gi, g in enumerate(groups):
            for q in range(len(g)):
                for k, peer in enumerate(_peers(kind, x, y, c)):
                    _ici_copy(kind, srcs[p], lands[p], sems[2 * gi], sems[2 * gi + 1], np_ * q + k, peer, x, y,
                              c).start()
                p += 1
        token[...] = jnp.zeros_like(token)

    sem_shapes = [pltpu.SemaphoreType.DMA((np_ * len(g),)) for g in groups for _ in range(2)]
    outs = pl.pallas_call(
        body, name=name, in_specs=[HBM] * (n + nl) + [ANY] * na,
        out_specs=[SEM] * (2 * ng) + [HBM] * (n + nl) + [pl.BlockSpec(memory_space=pltpu.VMEM)],
        out_shape=sem_shapes + [pltpu.HBM(a.shape, a.dtype) for a in flat]
        + [pltpu.HBM(s, a.dtype) for s, a in zip(land_shapes, flat)] + [jax.ShapeDtypeStruct((8, LANES), F32)],
        input_output_aliases={i: 2 * ng + i for i in range(n + nl)}, compiler_params=IN_FLIGHT,
    )(*[pltpu.with_memory_space_constraint(a, pltpu.HBM) for a in flat],
      *[pltpu.with_memory_space_constraint(lax.empty(s, a.dtype), pltpu.HBM) for s, a in zip(land_shapes, flat)],
      *after)
    sems, srcs, token = outs[:2 * ng], outs[2 * ng:2 * ng + n], outs[-1]
    lands = outs[2 * ng + n:2 * ng + n + nl] if nl else srcs
    started, p = [], 0
    for gi, g in enumerate(groups):
        started.append((sems[2 * gi], sems[2 * gi + 1], srcs[p:p + len(g)], lands[p:p + len(g)]))
        p += len(g)
    return started, token


def _ici_wait(kind, started, after, name):
    send_sems, recv_sems, srcs, lands = started
    n, np_ = len(srcs), _n_peers(kind)
    nl = 0 if kind in IN_PLACE else n
    after = list(after)

    def body(*refs):
        src_refs, send_ref, recv_ref = refs[:n], refs[n + nl], refs[n + nl + 1]
        land_refs = refs[n:n + nl] if nl else src_refs
        x, y, c = _where_am_i()
        for q in range(n):
            for k, peer in enumerate(_peers(kind, x, y, c)):
                args = (kind, src_refs[q], land_refs[q], send_ref, recv_ref, np_ * q + k, peer, x, y, c)
                _ici_copy(*args).wait_send()
                _ici_copy(*args, arriving=True).wait_recv()

    arrays = list(srcs) + (list(lands) if nl else [])
    outs = pl.pallas_call(
        body, name=name, in_specs=[HBM] * (n + nl) + [SEM, SEM] + [ANY] * len(after), out_specs=[HBM] * (n + nl),
        out_shape=[pltpu.HBM(a.shape, a.dtype) for a in arrays],
        input_output_aliases={i: i for i in range(n + nl)}, compiler_params=IN_FLIGHT,
    )(*arrays, send_sems, recv_sems, *after)
    return outs[:n], (outs[n:] if nl else outs[:n])


def _forward_to_sibling(lands, name):
    n = len(lands)

    def body(*refs):
        out_refs, (send_sems, recv_sems) = refs[n:2 * n], refs[2 * n:]
        x, y, c = _where_am_i()

        def copy(q, k, peer, half_of):
            rows = out_refs[q].at[2 * peer[0] + peer[1], _half_rows(lands[q].shape[1], half_of), :]
            return pltpu.make_async_remote_copy(
                src_ref=rows, dst_ref=rows, send_sem=send_sems.at[3 * q + k], recv_sem=recv_sems.at[3 * q + k],
                device_id=(x, y, 1 - c), device_id_type=MESH)

        sends = [copy(q, k, peer, c) for q in range(n) for k, peer in enumerate(_other_chips(x, y))]
        for cp in sends:
            cp.start()
        for q in range(n):
            for k, peer in enumerate(_other_chips(x, y)):
                copy(q, k, peer, 1 - c).wait_recv()
        for cp in sends:
            cp.wait_send()

    return pl.pallas_call(
        body, name=name, in_specs=[ANY] * n, out_specs=[ANY] * n,
        out_shape=[jax.ShapeDtypeStruct(a.shape, a.dtype) for a in lands],
        input_output_aliases={i: i for i in range(n)},
        scratch_shapes=[pltpu.SemaphoreType.DMA((3 * n,)), pltpu.SemaphoreType.DMA((3 * n,))],
    )(*lands)


def _with_own_slot(lands, own, chip):
    return [lax.dynamic_update_slice(land, mine[None], (chip, 0, 0)) for land, mine in zip(lands, own)]


def _row_tile(rows, row_bytes, n_bufs, budget=24 * 1024 * 1024):
    best = None
    for t in range(PACKED_ROWS, rows + 1, PACKED_ROWS):
        if rows % t == 0 and 2 * n_bufs * t * row_bytes <= budget:
            best = t
    assert best is not None, (rows, row_bytes)
    return best


def _add_halves(full, recv, core, out_dtype, name):
    n, h, cols = recv.shape
    tr = _row_tile(h, cols * 4, 3)
    nt = h // tr

    def body(core_ref, a_ref, b_ref, o_ref):
        o_ref[...] = (a_ref[...].astype(F32) + b_ref[...].astype(F32)).astype(out_dtype)

    blk = (None, tr, cols)
    return pl.pallas_call(
        body, name=name,
        grid_spec=pltpu.PrefetchScalarGridSpec(
            num_scalar_prefetch=1, grid=(n, nt),
            in_specs=[pl.BlockSpec(blk, lambda j, i, core_ref: (j, core_ref[0] * nt + i, 0)),
                      pl.BlockSpec(blk, lambda j, i, core_ref: (j, i, 0))],
            out_specs=pl.BlockSpec(blk, lambda j, i, core_ref: (j, i, 0))),
        out_shape=jax.ShapeDtypeStruct((n, h, cols), out_dtype),
        compiler_params=_params(("parallel", "parallel")),
    )(core.reshape(1), full, recv)


def _add_halves_pair(fulls, recvs, core, out_dtype, name):
    shapes = [r.shape for r in recvs]
    trs = [_row_tile(h, cols * 4, 3) for _, h, cols in shapes]
    nts = [h // tr for (_, h, _), tr in zip(shapes, trs)]
    steps = [n * nt for (n, _, _), nt in zip(shapes, nts)]
    first = steps[0]

    def body(core_ref, a0_ref, b0_ref, a1_ref, b1_ref, o0_ref, o1_ref):
        i = pl.program_id(0)

        @pl.when(i < first)
        def _():
            o0_ref[...] = (a0_ref[...].astype(F32) + b0_ref[...].astype(F32)).astype(out_dtype)

        @pl.when(i >= first)
        def _():
            o1_ref[...] = (a1_ref[...].astype(F32) + b1_ref[...].astype(F32)).astype(out_dtype)

    def specs(p):
        blk = (None, trs[p], shapes[p][2])

        def tile(i):
            k = jnp.clip(i - p * first, 0, steps[p] - 1)
            return k // nts[p], k % nts[p]

        full = pl.BlockSpec(blk, lambda i, core_ref: (tile(i)[0], core_ref[0] * nts[p] + tile(i)[1], 0))
        half = pl.BlockSpec(blk, lambda i, core_ref: (tile(i)[0], tile(i)[1], 0))
        return full, half

    (full0, half0), (full1, half1) = specs(0), specs(1)
    return pl.pallas_call(
        body, name=name,
        grid_spec=pltpu.PrefetchScalarGridSpec(
            num_scalar_prefetch=1, grid=(sum(steps),),
            in_specs=[full0, half0, full1, half1], out_specs=[half0, half1]),
        out_shape=[jax.ShapeDtypeStruct(s, out_dtype) for s in shapes],
        compiler_params=_params(("arbitrary",)),
    )(core.reshape(1), fulls[0], recvs[0], fulls[1], recvs[1])


def _add2(a, b, name):
    r, cols = a.shape
    tr = _row_tile(r, cols * 4, 3)

    def body(a_ref, b_ref, o_ref):
        o_ref[...] = a_ref[...] + b_ref[...]

    blk = pl.BlockSpec((tr, cols), lambda i: (i, 0))
    return pl.pallas_call(body, name=name, grid=(r // tr,), in_specs=[blk, blk], out_specs=blk,
                          out_shape=jax.ShapeDtypeStruct(a.shape, a.dtype), compiler_params=_params(("parallel",)))(a, b)


def _sum_slots(parts, own, chip, name):
    n, h, cols = parts.shape
    tr = _row_tile(h, cols * 4, n + 2)

    def body(chip_ref, p_ref, own_ref, o_ref):
        acc = None
        for k in range(n):
            term = jnp.where(chip_ref[0] == k, own_ref[...], p_ref[k]).astype(F32)
            acc = term if acc is None else acc + term
        o_ref[...] = acc

    own_spec = (pl.BlockSpec((None, tr, cols), lambda i, chip_ref: (chip_ref[0], i, 0)) if own.ndim == 3
                else pl.BlockSpec((tr, cols), lambda i, chip_ref: (i, 0)))
    return pl.pallas_call(
        body, name=name,
        grid_spec=pltpu.PrefetchScalarGridSpec(
            num_scalar_prefetch=1, grid=(h // tr,),
            in_specs=[pl.BlockSpec((n, tr, cols), lambda i, chip_ref: (0, i, 0)), own_spec],
            out_specs=pl.BlockSpec((tr, cols), lambda i, chip_ref: (i, 0))),
        out_shape=jax.ShapeDtypeStruct((h, cols), F32), compiler_params=_params(("parallel",)),
    )(chip.reshape(1), parts, own)


def _adamw(w, g, m, v, name):
    shape = w.shape
    cols = shape[-1]
    rows = math.prod(shape[:-1])
    tr = rows
    if rows % 8 == 0:
        tr = 8
        for t in range(8, rows + 1, 8):
            if rows % t == 0 and t * cols * 4 <= 1536 * 1024:
                tr = t
    c1 = 1.0 - ADAM_B1 ** ADAM_STEP
    c2 = 1.0 - ADAM_B2 ** ADAM_STEP

    def body(w_ref, g_ref, m_ref, v_ref, d_ref, nm_ref, nv_ref):
        gg = g_ref[...]
        nm = ADAM_B1 * m_ref[...] + (1.0 - ADAM_B1) * gg
        nv = ADAM_B2 * v_ref[...] + (1.0 - ADAM_B2) * (gg * gg)
        d_ref[...] = -ADAM_LR * ((nm / c1) / (jnp.sqrt(nv / c2) + ADAM_EPS) + ADAM_WD * w_ref[...])
        nm_ref[...] = nm
        nv_ref[...] = nv

    blk = pl.BlockSpec((tr, cols), lambda i: (i, 0))
    flat = jax.ShapeDtypeStruct((rows, cols), F32)
    outs = pl.pallas_call(
        body, name=name, grid=(rows // tr,), in_specs=[blk] * 4, out_specs=[blk] * 3, out_shape=[flat] * 3,
        compiler_params=_params(("parallel",)),
    )(*[a.reshape(rows, cols) for a in (w, g, m, v)])
    return [o.reshape(shape) for o in outs]


def _adamw_update(w, g, m, v):
    c1 = 1.0 - ADAM_B1 ** ADAM_STEP
    c2 = 1.0 - ADAM_B2 ** ADAM_STEP
    nm = ADAM_B1 * m + (1.0 - ADAM_B1) * g
    nv = ADAM_B2 * v + (1.0 - ADAM_B2) * (g * g)
    return -ADAM_LR * ((nm / c1) / (jnp.sqrt(nv / c2) + ADAM_EPS) + ADAM_WD * w), nm, nv


def _adamw_projection(w, m, v, mine, theirs, core, name):
    _, r, cols = w.shape
    h = r // 2
    tr = _row_tile(h, cols * 4, 12, budget=40 * 1024 * 1024)
    nt = h // tr

    def body(core_ref, w_ref, m_ref, v_ref, a0, b0, a1, b1, g_ref, d_ref, nm_ref, nv_ref):
        layer, half = pl.program_id(0), pl.program_id(1)
        own = half == core_ref[0]
        g = jnp.where(layer == 0, jnp.where(own, a0[...], b0[...]), jnp.where(own, a1[...], b1[...]))
        g_ref[...] = g
        d_ref[...], nm_ref[...], nv_ref[...] = _adamw_update(w_ref[...], g, m_ref[...], v_ref[...])

    def piece(layer, own):
        def index(l, hh, i, core_ref):
            used = (l == layer) & ((hh == core_ref[0]) == own)
            return (jnp.where(used, i, 0), 0)
        return pl.BlockSpec((tr, cols), index)

    whole = pl.BlockSpec((None, tr, cols), lambda l, hh, i, core_ref: (l, hh * nt + i, 0))
    return pl.pallas_call(
        body, name=name,
        grid_spec=pltpu.PrefetchScalarGridSpec(
            num_scalar_prefetch=1, grid=(2, 2, nt),
            in_specs=[whole] * 3 + [piece(0, True), piece(0, False), piece(1, True), piece(1, False)],
            out_specs=[whole] * 4),
        out_shape=[jax.ShapeDtypeStruct(w.shape, F32)] * 4,
        compiler_params=_params(("arbitrary", "arbitrary", "arbitrary")),
    )(core.reshape(1), w, m, v, mine[0], theirs[0], mine[1], theirs[1])


def _chip_shape(name):
    shape = list(FULL_SHAPES[name])
    axis = BIG_SHARD_AXIS.get(name, SMALL_SHARD_AXIS.get(name))
    if axis is not None:
        shape[axis] //= N_CHIPS
    return tuple(shape)


def _round_up(n, k):
    return -(-n // k) * k


SHARDED_SMALL_WORDS = sum(math.prod(_chip_shape(n)) for n in SMALL_SHARDED)
SHARDED_SMALL_ROWS = _round_up(-(-SHARDED_SMALL_WORDS // PACK_COLS), 32)
SMALL_MATRICES = ['even_a_ws', 'odd_w_a', 'odd_w_x', 'odd_w_pool']
SMALL_VECTORS = [n for n in SMALL_NAMES if n not in SMALL_MATRICES]
VECTOR_ROWS = _round_up(-(-(sum(math.prod(FULL_SHAPES[n]) for n in SMALL_VECTORS) + 1) // PACK_COLS), 32)


def _pack_rows(flat, rows):
    return jnp.pad(flat, (0, rows * PACK_COLS - flat.shape[0])).reshape(rows, PACK_COLS)


def _pack_small_sharded(shards):
    return _pack_rows(jnp.concatenate([shards[n].reshape(-1) for n in SMALL_SHARDED]), SHARDED_SMALL_ROWS)


def _unpack_small_sharded(gathered_small):
    out = {}
    words = gathered_small.reshape(N_CHIPS, -1)
    w0 = 0
    for n in SMALL_SHARDED:
        size = math.prod(_chip_shape(n))
        blocks = words[:, w0:w0 + size].reshape((N_CHIPS,) + _chip_shape(n))
        out[n] = jnp.concatenate([blocks[j] for j in range(N_CHIPS)], axis=SMALL_SHARD_AXIS[n])
        w0 += size
    return out


def _small_grad_parts(grads, loss):
    words = [grads[n].reshape(-1) for n in SMALL_VECTORS] + [loss.reshape(-1)]
    return ([_pack_rows(jnp.concatenate(words), VECTOR_ROWS)]
            + [grads[n].reshape(-1, FULL_SHAPES[n][-1]) for n in SMALL_MATRICES])


def _own_block(name, full, chip):
    if name not in SMALL_SHARD_AXIS:
        return full
    axis = SMALL_SHARD_AXIS[name]
    width = FULL_SHAPES[name][axis] // N_CHIPS
    return lax.dynamic_slice_in_dim(full, chip * width, width, axis)


def _from_small_grad_parts(totals, chip):
    flat = totals[0].reshape(-1)
    out = {}
    w0 = 0
    for n in SMALL_VECTORS:
        size = math.prod(FULL_SHAPES[n])
        out[n] = _own_block(n, flat[w0:w0 + size].reshape(FULL_SHAPES[n]), chip)
        w0 += size
    for n, total in zip(SMALL_MATRICES, totals[1:]):
        out[n] = _own_block(n, total.reshape(FULL_SHAPES[n]), chip)
    return out, flat[w0]


def kernel(x, positions, even_w_in, even_a_ln_g, even_a_ln_b, even_a_ws, even_a_bs, even_b_sinks, even_w_out, even_ln_g, even_ln_b, odd_w_in, odd_conv_w, odd_conv_b, odd_w_a, odd_b_a, odd_w_x, odd_b_x, odd_lam, odd_w_pool, odd_d_scale, odd_w_out, odd_ln_g, odd_ln_b, loss_target, m_even_w_in, m_even_a_ln_g, m_even_a_ln_b, m_even_a_ws, m_even_a_bs, m_even_b_sinks, m_even_w_out, m_even_ln_g, m_even_ln_b, m_odd_w_in, m_odd_conv_w, m_odd_conv_b, m_odd_w_a, m_odd_b_a, m_odd_w_x, m_odd_b_x, m_odd_lam, m_odd_w_pool, m_odd_d_scale, m_odd_w_out, m_odd_ln_g, m_odd_ln_b, v_even_w_in, v_even_a_ln_g, v_even_a_ln_b, v_even_a_ws, v_even_a_bs, v_even_b_sinks, v_even_w_out, v_even_ln_g, v_even_ln_b, v_odd_w_in, v_odd_conv_w, v_odd_conv_b, v_odd_w_a, v_odd_b_a, v_odd_w_x, v_odd_b_x, v_odd_lam, v_odd_w_pool, v_odd_d_scale, v_odd_w_out, v_odd_ln_g, v_odd_ln_b):
    local = dict(zip(WEIGHT_NAMES, (even_w_in, even_a_ln_g, even_a_ln_b, even_a_ws, even_a_bs, even_b_sinks, even_w_out, even_ln_g, even_ln_b, odd_w_in, odd_conv_w, odd_conv_b, odd_w_a, odd_b_a, odd_w_x, odd_b_x, odd_lam, odd_w_pool, odd_d_scale, odd_w_out, odd_ln_g, odd_ln_b)))
    mom = dict(zip(WEIGHT_NAMES, (m_even_w_in, m_even_a_ln_g, m_even_a_ln_b, m_even_a_ws, m_even_a_bs, m_even_b_sinks, m_even_w_out, m_even_ln_g, m_even_ln_b, m_odd_w_in, m_odd_conv_w, m_odd_conv_b, m_odd_w_a, m_odd_b_a, m_odd_w_x, m_odd_b_x, m_odd_lam, m_odd_w_pool, m_odd_d_scale, m_odd_w_out, m_odd_ln_g, m_odd_ln_b)))
    vel = dict(zip(WEIGHT_NAMES, (v_even_w_in, v_even_a_ln_g, v_even_a_ln_b, v_even_a_ws, v_even_a_bs, v_even_b_sinks, v_even_w_out, v_even_ln_g, v_even_ln_b, v_odd_w_in, v_odd_conv_w, v_odd_conv_b, v_odd_w_a, v_odd_b_a, v_odd_w_x, v_odd_b_x, v_odd_lam, v_odd_w_pool, v_odd_d_scale, v_odd_w_out, v_odd_ln_g, v_odd_ln_b)))
    mx, my, core = _where_am_i()
    chip = 2 * mx + my

    stored = lambda n, a: jnp.swapaxes(a, 1, 2) if n == 'even_w_in' and W_IN_TRANSPOSED['even'] else a
    names_of = lambda layer: [('even' if layer % 2 == 0 else 'odd') + s for s in ('_w_in', '_w_out')]

    shards = [[stored(n, local[n])[layer // 2].astype(MXU_DTYPE) for n in names_of(layer)] for layer in range(DEPTH)]
    (first_in, first_out), first_token = _ici_start(
        'gather', [[_pack_small_sharded(local), shards[0][0]], [shards[0][1]]], "gather_start_first")
    later, gather_token = _ici_start('gather', shards[1:], "gather_start", after=[first_token])
    gathers = [None] + later

    own, lands = _ici_wait('gather', first_in, [first_token, gather_token], "gather_wait_0")
    small_sharded, first_w_in = _with_own_slot(_forward_to_sibling(lands, "gather_forward_0"), own, chip)
    weights = dict(local)
    weights.update(_unpack_small_sharded(small_sharded))
    passing = {}

    def passed_on(started, after, name):
        own, lands = _ici_wait('gather', started, [after], "gather_wait_" + name)
        (forward,), token = _ici_start('forward', [lands], "forward_start_" + name)

        def finish(then):
            done, _ = _ici_wait('forward', forward, [then], "forward_wait_" + name)
            return _with_own_slot(done, own, chip)

        return token, finish

    def first_w_out(h):
        token, finish = passed_on(first_out, h, "0_out")
        return token, lambda mix: finish(mix)[0]

    def projections_ahead(layer, after):
        token, passing[layer] = passed_on(gathers[layer], after, str(layer))
        return token

    def projections_of(layer, after):
        return (first_w_in, first_w_out) if layer == 0 else passing.pop(layer)(after)

    swapping, in_flight, mine, started = [], [], {}, {}

    def land(*after):
        layer, scatter = in_flight.pop()
        parts, lands = _ici_wait('scatter', scatter, after, "scatter_wait_%d" % layer)
        for n, arrived, part in zip(names_of(layer), lands, parts):
            mine[n, layer // 2] = _sum_slots(arrived, part, chip, "big_sum_%s_%d" % (n, layer // 2))

    early_pieces = [(n, l) for n in ('odd_w_in', 'odd_w_out') for l in range(2)] + [('even_w_in', 1), ('even_w_out', 1)]
    joining, theirs = [], {}

    def projection_grads(layer, dw_in, dw_out):
        if in_flight:
            land(dw_in)
        (swap,), token = _ici_start('swap', [[dw_in, dw_out]], "presum_start_%d" % layer)
        swapping.append((layer, swap))
        if layer > 0:
            return token
        (join,), token = _ici_start('join', [[mine[p] for p in early_pieces]], "join_early_start",
                                    after=[presum_and_scatter(token)])
        joining.append(join)
        return token

    def presum_and_scatter(*after):
        layer, swap = swapping.pop()
        partial, from_sibling = _ici_wait('swap', swap, after, "presum_wait_%d" % layer)
        chip_part = _add_halves_pair(partial, from_sibling, core, MXU_DTYPE, "big_presum_add_%d" % layer)
        (scatter,), started[layer] = _ici_start('scatter', [chip_part], "scatter_start_%d" % layer)
        in_flight.append((layer, scatter))
        return started[layer]

    layer_grads_begin = lambda layer, dout: presum_and_scatter(dout) if swapping else dout
    loss, grad_x, grads = _local_step(x[0], positions[0], loss_target[0], weights, projections_of, projections_ahead,
                                      projection_grads, layer_grads_begin)

    (small_swap,), small_token = _ici_start('join', [_small_grad_parts(grads, loss)], "small_swap_start",
                                            after=[started[0]])

    reduced, deltas, new_m, new_v = {}, {}, {}, {}

    def adamw_projections(kind, after):
        names = [kind + '_w_in', kind + '_w_out']
        if kind == 'odd':
            halves, landed = _ici_wait('join', joining.pop(), after, "join_early_wait")
            mine.update(zip(early_pieces, halves))
            theirs.update(zip(early_pieces, landed))
        else:
            late = [(n, 0) for n in names]
            theirs.update(zip(late, _sibling_swap([mine[p] for p in late], "big_join_" + kind, after=after)))
        done = []
        for n in names:
            outs = _adamw_projection(stored(n, local[n]), stored(n, mom[n]), stored(n, vel[n]),
                                     [mine[n, l] for l in range(2)], [theirs[n, l] for l in range(2)], core,
                                     "adamw_" + n)
            reduced[n], deltas[n], new_m[n], new_v[n] = [stored(n, o) for o in outs]
            done.append(outs[-1])
        return done

    done = adamw_projections('odd', [small_token])
    small, from_sibling = _ici_wait('join', small_swap, done, "small_swap_wait")
    chip_sums = [_add2(a, b, "small_presum_add_%d" % i) for i, (a, b) in enumerate(zip(small, from_sibling))]
    (small_gather,), small_token = _ici_start('gather', [chip_sums], "small_gather_start")
    land(grad_x, small_token)
    done = adamw_projections('even', [])

    own, lands = _ici_wait('gather', small_gather, done, "small_gather_wait")
    of_chips = _forward_to_sibling(lands, "small_gather_forward")
    small_totals, total_loss = _from_small_grad_parts(
        [_sum_slots(g, mine_i, chip, "small_sum_%d" % i) for i, (g, mine_i) in enumerate(zip(of_chips, own))], chip)
    reduced.update(small_totals)
    for n in SMALL_NAMES:
        deltas[n], new_m[n], new_v[n] = _adamw(local[n], reduced[n], mom[n], vel[n], "adamw_" + n)

    return (total_loss, grad_x[None], *[reduced[n] for n in WEIGHT_NAMES], *[deltas[n] for n in WEIGHT_NAMES],
            *[new_m[n] for n in WEIGHT_NAMES], *[new_v[n] for n in WEIGHT_NAMES])
```
